```python
import math
import jax, jax.numpy as jnp
from jax import lax
import numpy as np

D_MODEL = 1024
BATCH = 8
SEQ = 8192
DEPTH = 2

D_MIX = 3 * D_MODEL // 2
W_A = D_MIX // 3
W_B = D_MIX // 3
W_C = D_MIX - W_A - W_B
HEAD_DIM = 64
N_HEADS_A = W_A // HEAD_DIM
N_HEADS_B = W_B // HEAD_DIM
POOL_WINDOWS = (2, 4, 8, 16)
N_POOL_GROUPS = len(POOL_WINDOWS)
POOL_GROUP_DIM = W_C // N_POOL_GROUPS
CONV_A_WIDTH = 3
CONV_B_WIDTH = 31
D_IN = 4 * W_A + 3 * W_B + 2 * W_C
DEEPNORM_ALPHA = (2.0 * DEPTH) ** 0.25
DEEPNORM_BETA = (8.0 * DEPTH) ** -0.25
LN_EPS = 1e-5

kernel_name = "hybrid_conv_pool_deepnorm_trunk"


def layer_norm(x, g, b):
    xf = x.astype(jnp.float32)
    mu = jnp.mean(xf, axis=-1, keepdims=True)
    xc = xf - mu
    var = jnp.mean(xc * xc, axis=-1, keepdims=True)
    y = xc * lax.rsqrt(var + LN_EPS) * g.astype(jnp.float32) + b.astype(jnp.float32)
    return y.astype(x.dtype)


def causal_depthwise_conv(u, w, b):
    k, c = w.shape
    y = lax.conv_general_dilated(
        u, w[:, None, :].astype(u.dtype),
        window_strides=(1,), padding=[(k - 1, 0)],
        dimension_numbers=("NWC", "WIO", "NWC"),
        feature_group_count=c)
    return y + b.astype(u.dtype)


def multiscale_causal_pool(u):
    t = u.shape[1]
    uf = u.astype(jnp.float32)
    cs = jnp.cumsum(uf, axis=1)
    pos = jnp.arange(t, dtype=jnp.float32)[None, :, None]
    outs = []
    for g, w in enumerate(POOL_WINDOWS):
        sl = slice(g * POOL_GROUP_DIM, (g + 1) * POOL_GROUP_DIM)
        cs_g = cs[..., sl]
        cs_shift = jnp.pad(cs_g[:, : t - w], ((0, 0), (w, 0), (0, 0)))
        count = jnp.minimum(pos + 1.0, float(w))
        mean = (cs_g - cs_shift) / count
        outs.append(mean - uf[..., sl])
    return jnp.stack(outs, axis=2).astype(u.dtype)


def hybrid_layer(x, w_in, conv_a_w, conv_a_b, conv_b_w, conv_b_b, ln_b_g, ln_b_b,
                 pool_w, pool_b, pool_scale, w_out, ln_g, ln_b):
    bsz, t, _ = x.shape
    h = jnp.einsum("btd,de->bte", x, w_in)
    splits = np.cumsum([W_A, W_A, W_A, W_A, W_B, W_B, W_B, W_C])
    a_bg, a_cg, a_v, a_z, b_v, b_g, b_z, c_u, c_z = jnp.split(h, splits, axis=-1)

    y_a = a_bg * causal_depthwise_conv(a_cg * a_v, conv_a_w, conv_a_b)
    y_a = y_a * jax.nn.silu(a_z)

    u_b = b_v * jax.nn.sigmoid(b_g)
    u_b = causal_depthwise_conv(u_b, conv_b_w, conv_b_b)
    u_b = jax.nn.silu(layer_norm(u_b, ln_b_g, ln_b_b))
    y_b = u_b * jax.nn.silu(b_z)

    p = multiscale_causal_pool(c_u)
    p = jnp.einsum("btgc,gcd->btgd", p, pool_w) + pool_b
    y_c = p.reshape(bsz, t, W_C) * pool_scale
    y_c = y_c * jax.nn.silu(c_z)

    y = jnp.concatenate([y_a, y_b, y_c], axis=-1)
    out = jnp.einsum("bte,ed->btd", y, w_out)
    return layer_norm(DEEPNORM_ALPHA * x + out, ln_g, ln_b)


def _fwd_setup_inputs(seed: int = 0) -> dict:
    key = jax.random.key(seed)
    ks = jax.random.split(key, 16)
    f32 = jnp.float32
    nrm = lambda k, s, sc: jax.random.normal(k, s, f32) * sc
    return {
        "x": jax.random.normal(ks[0], (BATCH, SEQ, D_MODEL), f32),
        "w_in": nrm(ks[1], (DEPTH, D_MODEL, D_IN), D_MODEL ** -0.5),
        "conv_a_w": nrm(ks[2], (DEPTH, CONV_A_WIDTH, W_A), CONV_A_WIDTH ** -0.5),
        "conv_a_b": nrm(ks[3], (DEPTH, W_A), 0.02),
        "conv_b_w": nrm(ks[4], (DEPTH, CONV_B_WIDTH, W_B), CONV_B_WIDTH ** -0.5),
        "conv_b_b": nrm(ks[5], (DEPTH, W_B), 0.02),
        "ln_b_g": 1.0 + nrm(ks[6], (DEPTH, W_B), 0.02),
        "ln_b_b": nrm(ks[7], (DEPTH, W_B), 0.02),
        "pool_w": nrm(ks[8], (DEPTH, N_POOL_GROUPS, POOL_GROUP_DIM, POOL_GROUP_DIM), POOL_GROUP_DIM ** -0.5),
        "pool_b": nrm(ks[9], (DEPTH, N_POOL_GROUPS, POOL_GROUP_DIM), 0.02),
        "pool_scale": 1.0 + nrm(ks[10], (DEPTH, W_C), 0.02),
        "w_out": nrm(ks[11], (DEPTH, D_MIX, D_MODEL), DEEPNORM_BETA * D_MIX ** -0.5),
        "ln_g": 1.0 + nrm(ks[12], (DEPTH, D_MODEL), 0.02),
        "ln_b": nrm(ks[13], (DEPTH, D_MODEL), 0.02),
    }


def _fwd_reference(x, w_in, conv_a_w, conv_a_b, conv_b_w, conv_b_b, ln_b_g, ln_b_b,
              pool_w, pool_b, pool_scale, w_out, ln_g, ln_b):
    for l in range(DEPTH):
        x = hybrid_layer(x, w_in[l], conv_a_w[l], conv_a_b[l], conv_b_w[l], conv_b_b[l],
                         ln_b_g[l], ln_b_b[l], pool_w[l], pool_b[l], pool_scale[l],
                         w_out[l], ln_g[l], ln_b[l])
    return x


import jax as _jax
import jax.numpy as _jnp

TWIN_FORMAT = 'train_step'
FWD_PARAMS = ['x', 'w_in', 'conv_a_w', 'conv_a_b', 'conv_b_w', 'conv_b_b', 'ln_b_g', 'ln_b_b', 'pool_w', 'pool_b', 'pool_scale', 'w_out', 'ln_g', 'ln_b']
TWIN_WEIGHTS = ['w_in', 'conv_a_w', 'conv_a_b', 'conv_b_w', 'conv_b_b', 'ln_b_g', 'ln_b_b', 'pool_w', 'pool_b', 'pool_scale', 'w_out', 'ln_g', 'ln_b']
TWIN_DIFF_INPUT = 'x'
TWIN_INPUTS = ['x', 'w_in', 'conv_a_w', 'conv_a_b', 'conv_b_w', 'conv_b_b', 'ln_b_g', 'ln_b_b', 'pool_w', 'pool_b', 'pool_scale', 'w_out', 'ln_g', 'ln_b', 'loss_target', 'm_w_in', 'm_conv_a_w', 'm_conv_a_b', 'm_conv_b_w', 'm_conv_b_b', 'm_ln_b_g', 'm_ln_b_b', 'm_pool_w', 'm_pool_b', 'm_pool_scale', 'm_w_out', 'm_ln_g', 'm_ln_b', 'v_w_in', 'v_conv_a_w', 'v_conv_a_b', 'v_conv_b_w', 'v_conv_b_b', 'v_ln_b_g', 'v_ln_b_b', 'v_pool_w', 'v_pool_b', 'v_pool_scale', 'v_w_out', 'v_ln_g', 'v_ln_b']
TWIN_OUTPUTS = ['loss', 'grad_x', 'grad_w_in', 'grad_conv_a_w', 'grad_conv_a_b', 'grad_conv_b_w', 'grad_conv_b_b', 'grad_ln_b_g', 'grad_ln_b_b', 'grad_pool_w', 'grad_pool_b', 'grad_pool_scale', 'grad_w_out', 'grad_ln_g', 'grad_ln_b', 'delta_w_in', 'delta_conv_a_w', 'delta_conv_a_b', 'delta_conv_b_w', 'delta_conv_b_b', 'delta_ln_b_g', 'delta_ln_b_b', 'delta_pool_w', 'delta_pool_b', 'delta_pool_scale', 'delta_w_out', 'delta_ln_g', 'delta_ln_b', 'new_m_w_in', 'new_m_conv_a_w', 'new_m_conv_a_b', 'new_m_conv_b_w', 'new_m_conv_b_b', 'new_m_ln_b_g', 'new_m_ln_b_b', 'new_m_pool_w', 'new_m_pool_b', 'new_m_pool_scale', 'new_m_w_out', 'new_m_ln_g', 'new_m_ln_b', 'new_v_w_in', 'new_v_conv_a_w', 'new_v_conv_a_b', 'new_v_conv_b_w', 'new_v_conv_b_b', 'new_v_ln_b_g', 'new_v_ln_b_b', 'new_v_pool_w', 'new_v_pool_b', 'new_v_pool_scale', 'new_v_w_out', 'new_v_ln_g', 'new_v_ln_b']
TWIN_LEAF_KINDS = {'loss': 'loss', 'grad_x': 'grad_x', 'grad_w_in': 'grad_w', 'grad_conv_a_w': 'grad_w', 'grad_conv_a_b': 'grad_w', 'grad_conv_b_w': 'grad_w', 'grad_conv_b_b': 'grad_w', 'grad_ln_b_g': 'grad_w', 'grad_ln_b_b': 'grad_w', 'grad_pool_w': 'grad_w', 'grad_pool_b': 'grad_w', 'grad_pool_scale': 'grad_w', 'grad_w_out': 'grad_w', 'grad_ln_g': 'grad_w', 'grad_ln_b': 'grad_w', 'delta_w_in': 'delta_w', 'delta_conv_a_w': 'delta_w', 'delta_conv_a_b': 'delta_w', 'delta_conv_b_w': 'delta_w', 'delta_conv_b_b': 'delta_w', 'delta_ln_b_g': 'delta_w', 'delta_ln_b_b': 'delta_w', 'delta_pool_w': 'delta_w', 'delta_pool_b': 'delta_w', 'delta_pool_scale': 'delta_w', 'delta_w_out': 'delta_w', 'delta_ln_g': 'delta_w', 'delta_ln_b': 'delta_w', 'new_m_w_in': 'new_m', 'new_m_conv_a_w': 'new_m', 'new_m_conv_a_b': 'new_m', 'new_m_conv_b_w': 'new_m', 'new_m_conv_b_b': 'new_m', 'new_m_ln_b_g': 'new_m', 'new_m_ln_b_b': 'new_m', 'new_m_pool_w': 'new_m', 'new_m_pool_b': 'new_m', 'new_m_pool_scale': 'new_m', 'new_m_w_out': 'new_m', 'new_m_ln_g': 'new_m', 'new_m_ln_b': 'new_m', 'new_v_w_in': 'new_v', 'new_v_conv_a_w': 'new_v', 'new_v_conv_a_b': 'new_v', 'new_v_conv_b_w': 'new_v', 'new_v_conv_b_b': 'new_v', 'new_v_ln_b_g': 'new_v', 'new_v_ln_b_b': 'new_v', 'new_v_pool_w': 'new_v', 'new_v_pool_b': 'new_v', 'new_v_pool_scale': 'new_v', 'new_v_w_out': 'new_v', 'new_v_ln_g': 'new_v', 'new_v_ln_b': 'new_v'}


def _forward(args):
    return _fwd_reference(*[args[k] for k in FWD_PARAMS])


def _output_shape():
    def fwd():
        inp = _fwd_setup_inputs(0)
        return _fwd_reference(*[inp[k] for k in FWD_PARAMS])
    out = _jax.eval_shape(fwd)
    return out.shape, out.dtype

N_MICROBATCH = 1
ADAM_LR = 0.001
ADAM_B1 = 0.9
ADAM_B2 = 0.999
ADAM_EPS = 1e-08
ADAM_WD = 0.01
ADAM_STEP = 10
PER_EXAMPLE_BATCH_AXIS = {'x': 0, 'loss_target': 0}
SHARED_INPUTS = []
_WEIGHT_DTYPES = {'w_in': _jnp.float32, 'conv_a_w': _jnp.float32, 'conv_a_b': _jnp.float32, 'conv_b_w': _jnp.float32, 'conv_b_b': _jnp.float32, 'ln_b_g': _jnp.float32, 'ln_b_b': _jnp.float32, 'pool_w': _jnp.float32, 'pool_b': _jnp.float32, 'pool_scale': _jnp.float32, 'w_out': _jnp.float32, 'ln_g': _jnp.float32, 'ln_b': _jnp.float32}
MOMENT_SCALE = {'w_in': 3.670521e-02, 'conv_a_w': 4.419482e-02, 'conv_a_b': 4.321492e-02, 'conv_b_w': 2.671033e-02, 'conv_b_b': 5.683133e-02, 'ln_b_g': 3.227756e-02, 'ln_b_b': 3.214283e-02, 'pool_w': 3.789891e-02, 'pool_b': 6.508032e-02, 'pool_scale': 3.762664e-02, 'w_out': 8.991041e-02, 'ln_g': 4.529276e+01, 'ln_b': 1.355220e+00}


def _to_microbatches(a, axis):
    t = _jnp.moveaxis(a, axis, 0)
    t = t.reshape((N_MICROBATCH, t.shape[0] // N_MICROBATCH) + t.shape[1:])
    return _jnp.moveaxis(t, 1, axis + 1)


def setup_inputs(seed: int = 0) -> dict:
    inp = _fwd_setup_inputs(seed)
    key = _jax.random.fold_in(_jax.random.key(seed), 7919)
    shape, _ = _output_shape()
    out = dict(inp)
    out["loss_target"] = _jax.random.normal(_jax.random.fold_in(key, 0), shape, _jnp.float32)
    for i, name in enumerate(TWIN_WEIGHTS):
        w = inp[name].astype(_jnp.float32)
        if MOMENT_SCALE is None:
            s = _jnp.sqrt(_jnp.mean(_jnp.square(w)) + 1e-30)
        else:
            s = MOMENT_SCALE[name]
        km, kv = _jax.random.split(_jax.random.fold_in(key, i + 1))
        out[name] = w
        out["m_" + name] = s * _jax.random.normal(km, w.shape, _jnp.float32)
        out["v_" + name] = (s * s) * _jax.random.uniform(kv, w.shape, _jnp.float32, 0.5, 1.5)
    if N_MICROBATCH > 1:
        for name, axis in PER_EXAMPLE_BATCH_AXIS.items():
            out[name] = _to_microbatches(out[name], axis)
    return {'x': out['x'], 'w_in': out['w_in'], 'conv_a_w': out['conv_a_w'], 'conv_a_b': out['conv_a_b'], 'conv_b_w': out['conv_b_w'], 'conv_b_b': out['conv_b_b'], 'ln_b_g': out['ln_b_g'], 'ln_b_b': out['ln_b_b'], 'pool_w': out['pool_w'], 'pool_b': out['pool_b'], 'pool_scale': out['pool_scale'], 'w_out': out['w_out'], 'ln_g': out['ln_g'], 'ln_b': out['ln_b'], 'loss_target': out['loss_target'], 'm_w_in': out['m_w_in'], 'm_conv_a_w': out['m_conv_a_w'], 'm_conv_a_b': out['m_conv_a_b'], 'm_conv_b_w': out['m_conv_b_w'], 'm_conv_b_b': out['m_conv_b_b'], 'm_ln_b_g': out['m_ln_b_g'], 'm_ln_b_b': out['m_ln_b_b'], 'm_pool_w': out['m_pool_w'], 'm_pool_b': out['m_pool_b'], 'm_pool_scale': out['m_pool_scale'], 'm_w_out': out['m_w_out'], 'm_ln_g': out['m_ln_g'], 'm_ln_b': out['m_ln_b'], 'v_w_in': out['v_w_in'], 'v_conv_a_w': out['v_conv_a_w'], 'v_conv_a_b': out['v_conv_a_b'], 'v_conv_b_w': out['v_conv_b_w'], 'v_conv_b_b': out['v_conv_b_b'], 'v_ln_b_g': out['v_ln_b_g'], 'v_ln_b_b': out['v_ln_b_b'], 'v_pool_w': out['v_pool_w'], 'v_pool_b': out['v_pool_b'], 'v_pool_scale': out['v_pool_scale'], 'v_w_out': out['v_w_out'], 'v_ln_g': out['v_ln_g'], 'v_ln_b': out['v_ln_b']}


def _loss(weights, diff, rest, loss_target):
    with _jax.named_scope("forward"):
        args = {**rest, TWIN_DIFF_INPUT: diff, **{k: w.astype(_WEIGHT_DTYPES[k]) for k, w in weights.items()}}
        y = _forward(args)
    with _jax.named_scope("loss_head"):
        err = _jnp.square(y.astype(_jnp.float32) - loss_target)
        return 0.5 * _jnp.sum(_jnp.mean(err, axis=-1)) if err.ndim else 0.5 * err


def _adamw(w, g, m, v):
    m = ADAM_B1 * m + (1.0 - ADAM_B1) * g
    v = ADAM_B2 * v + (1.0 - ADAM_B2) * _jnp.square(g)
    m_hat = m / (1.0 - ADAM_B1 ** ADAM_STEP)
    v_hat = v / (1.0 - ADAM_B2 ** ADAM_STEP)
    delta = -ADAM_LR * (m_hat / (_jnp.sqrt(v_hat) + ADAM_EPS) + ADAM_WD * w)
    return delta, m, v


def reference(x, w_in, conv_a_w, conv_a_b, conv_b_w, conv_b_b, ln_b_g, ln_b_b, pool_w, pool_b, pool_scale, w_out, ln_g, ln_b, loss_target, m_w_in, m_conv_a_w, m_conv_a_b, m_conv_b_w, m_conv_b_b, m_ln_b_g, m_ln_b_b, m_pool_w, m_pool_b, m_pool_scale, m_w_out, m_ln_g, m_ln_b, v_w_in, v_conv_a_w, v_conv_a_b, v_conv_b_w, v_conv_b_b, v_ln_b_g, v_ln_b_b, v_pool_w, v_pool_b, v_pool_scale, v_w_out, v_ln_g, v_ln_b):
    given = dict(x=x, w_in=w_in, conv_a_w=conv_a_w, conv_a_b=conv_a_b, conv_b_w=conv_b_w, conv_b_b=conv_b_b, ln_b_g=ln_b_g, ln_b_b=ln_b_b, pool_w=pool_w, pool_b=pool_b, pool_scale=pool_scale, w_out=w_out, ln_g=ln_g, ln_b=ln_b, loss_target=loss_target, m_w_in=m_w_in, m_conv_a_w=m_conv_a_w, m_conv_a_b=m_conv_a_b, m_conv_b_w=m_conv_b_w, m_conv_b_b=m_conv_b_b, m_ln_b_g=m_ln_b_g, m_ln_b_b=m_ln_b_b, m_pool_w=m_pool_w, m_pool_b=m_pool_b, m_pool_scale=m_pool_scale, m_w_out=m_w_out, m_ln_g=m_ln_g, m_ln_b=m_ln_b, v_w_in=v_w_in, v_conv_a_w=v_conv_a_w, v_conv_a_b=v_conv_a_b, v_conv_b_w=v_conv_b_w, v_conv_b_b=v_conv_b_b, v_ln_b_g=v_ln_b_g, v_ln_b_b=v_ln_b_b, v_pool_w=v_pool_w, v_pool_b=v_pool_b, v_pool_scale=v_pool_scale, v_w_out=v_w_out, v_ln_g=v_ln_g, v_ln_b=v_ln_b)
    weights = {n: given[n] for n in TWIN_WEIGHTS}
    shared = {n: given[n] for n in SHARED_INPUTS}
    per_example = {n: given[n] for n in ['x']}
    grad_fn = _jax.value_and_grad(_loss, argnums=(0, 1))

    def one_microbatch(ex, loss_target):
        ex = dict(ex)
        diff = ex.pop(TWIN_DIFF_INPUT)
        return grad_fn(weights, diff, {**shared, **ex}, loss_target)

    if N_MICROBATCH == 1:
        loss, (grad_w, grad_x) = one_microbatch(per_example, given["loss_target"])
    else:
        def body(carry, xs):
            loss_sum, grad_sum = carry
            l_k, (gw_k, gx_k) = one_microbatch(xs[0], xs[1])
            with _jax.named_scope("update"):
                return (loss_sum + l_k, _jax.tree.map(_jnp.add, grad_sum, gw_k)), gx_k

        init = (_jnp.zeros((), _jnp.float32), _jax.tree.map(_jnp.zeros_like, weights))
        (loss, grad_w), grad_x = _jax.lax.scan(body, init, (per_example, given["loss_target"]))
    with _jax.named_scope("update"):
        delta_w, new_m, new_v = {}, {}, {}
        for n in TWIN_WEIGHTS:
            delta_w[n], new_m[n], new_v[n] = _adamw(weights[n], grad_w[n], given["m_" + n], given["v_" + n])
    return (loss, grad_x, *[grad_w[n] for n in TWIN_WEIGHTS], *[delta_w[n] for n in TWIN_WEIGHTS],
            *[new_m[n] for n in TWIN_WEIGHTS], *[new_v[n] for n in TWIN_WEIGHTS])
```

```python
import functools

import jax
import jax.numpy as jnp
from jax import lax
from jax.experimental import pallas as pl
from jax.experimental.pallas import tpu as pltpu

F32 = jnp.float32
BF16 = jnp.bfloat16

D_MODEL = 1024
DEPTH = 2
GW = 512
D_IN = 9 * GW
D_MIX = 3 * GW
POOL_WINDOWS = (2, 4, 8, 16)
PGD = 128
KA = 3
KB = 31
ALPHA = (2.0 * DEPTH) ** 0.25
LN_EPS = 1e-5
ADAM_LR, ADAM_B1, ADAM_B2, ADAM_EPS, ADAM_WD, ADAM_STEP = 0.001, 0.9, 0.999, 1e-08, 0.01, 10

N_CHIP = 4
SHARD_IN = D_IN // N_CHIP
SHARD_OUT = D_MIX // N_CHIP

SUBLANES = 8
RC = 32
HALO = 32
VMEM_LIMIT = 60 * 1024 * 1024

R_DWA, R_DCAB, R_DWB, R_DCBB, R_DLBG, R_DLBB, R_DPB, R_DPS, N_RACC = 0, 3, 4, 35, 36, 37, 38, 39, 40


def _sig(v):
    return 1.0 / (1.0 + jnp.exp(-v))


def _chunks(n_rows, fn):
    def step(c, carry):
        fn(pl.multiple_of(c * RC, RC))
        return carry
    lax.fori_loop(0, n_rows // RC, step, 0)


def _fold8(v):
    return v.reshape(RC // SUBLANES, SUBLANES, v.shape[-1]).sum(axis=0)


def _build_shifts(ext_ref, sh_ref, shifts, n_rows):
    for r in shifts:
        for c0 in range(0, n_rows, RC):
            n = min(RC, n_rows - c0)
            sh_ref[r, pl.ds(c0, n), :] = ext_ref[pl.ds(c0 + r, n), :]


def _tap(ext_ref, sh_ref, off, base, lanes=None):
    a, r = divmod(off, SUBLANES)
    src = ext_ref if r == 0 else sh_ref.at[r]
    if lanes is None:
        return src[pl.ds(base + SUBLANES * a, RC), :]
    return src[pl.ds(base + SUBLANES * a, RC), lanes]


def _ln_stats(v):
    mu = jnp.mean(v, axis=-1, keepdims=True)
    vc = v - mu
    var = jnp.mean(vc * vc, axis=-1, keepdims=True)
    rstd = lax.rsqrt(var + LN_EPS)
    return vc * rstd, rstd


def _ln_bwd(dy, xhat, rstd, g):
    dxh = dy * g
    m1 = jnp.mean(dxh, axis=-1, keepdims=True)
    m2 = jnp.mean(dxh * xhat, axis=-1, keepdims=True)
    return rstd * (dxh - m1 - xhat * m2)


def _inv_count(base, t0, w):
    t = (lax.broadcasted_iota(jnp.int32, (RC, PGD), 0) + (base + t0 + 1)).astype(F32)
    return 1.0 / jnp.minimum(t, float(w))


def _hcol(h_ref, j, base):
    return h_ref[pl.ds(base, RC), j * GW:(j + 1) * GW].astype(F32)


def _fwd_mixers(h_ref, y_scr, q_ext, ub_ext, cu_ext, sh, p_scr, pl_scr, prm, tt, t0):
    caw, cab, cbw, cbb, lbg, lbb, pw, pb, ps = prm

    def a1(base):
        q_ext[pl.ds(SUBLANES + base, RC), :] = _hcol(h_ref, 1, base) * _hcol(h_ref, 2, base)
    _chunks(tt, a1)
    _build_shifts(q_ext, sh, (6, 7), tt)

    def a2(base):
        ca = cab[...] + caw[0:1, :] * _tap(q_ext, sh, 6, base) + caw[1:2, :] * _tap(q_ext, sh, 7, base) \
            + caw[2:3, :] * _tap(q_ext, sh, 8, base)
        z = _hcol(h_ref, 3, base)
        y_scr[pl.ds(base, RC), 0:GW] = (_hcol(h_ref, 0, base) * ca * (z * _sig(z))).astype(BF16)
    _chunks(tt, a2)
    q_ext[0:SUBLANES, :] = q_ext[tt:tt + SUBLANES, :]

    def b1(base):
        ub_ext[pl.ds(HALO + base, RC), :] = _hcol(h_ref, 4, base) * _sig(_hcol(h_ref, 5, base))
    _chunks(tt, b1)
    _build_shifts(ub_ext, sh, range(1, 8), tt + HALO - SUBLANES)

    def b2(base):
        cb = cbb[...] + jnp.zeros((RC, GW), F32)
        for k in range(KB):
            cb = cb + cbw[k:k + 1, :] * _tap(ub_ext, sh, 2 + k, base)
        xhat, _ = _ln_stats(cb)
        lnv = xhat * lbg[...] + lbb[...]
        z = _hcol(h_ref, 6, base)
        y_scr[pl.ds(base, RC), GW:2 * GW] = (lnv * _sig(lnv) * (z * _sig(z))).astype(BF16)
    _chunks(tt, b2)
    ub_ext[0:HALO, :] = ub_ext[tt:tt + HALO, :]

    def c1(base):
        cu_ext[pl.ds(16 + base, RC), :] = _hcol(h_ref, 7, base)
    _chunks(tt, c1)
    _build_shifts(cu_ext, sh, range(1, 8), tt + SUBLANES)

    def c2(base):
        for g, w in enumerate(POOL_WINDOWS):
            lanes = slice(g * PGD, (g + 1) * PGD)
            acc = _tap(cu_ext, sh, 16, base, lanes)
            for j in range(1, w):
                acc = acc + _tap(cu_ext, sh, 16 - j, base, lanes)
            p = acc * _inv_count(base, t0, w) - _tap(cu_ext, sh, 16, base, lanes)
            p_scr[pl.ds(base, RC), lanes] = p.astype(BF16)
    _chunks(tt, c2)
    cu_ext[0:16, :] = cu_ext[tt:tt + 16, :]
    for g in range(len(POOL_WINDOWS)):
        lanes = slice(g * PGD, (g + 1) * PGD)
        pl_scr[:, lanes] = jnp.dot(p_scr[:, lanes], pw[g], preferred_element_type=F32)

    def c3(base):
        z = _hcol(h_ref, 8, base)
        yc0 = (pl_scr[pl.ds(base, RC), :] + pb[...]) * ps[...]
        y_scr[pl.ds(base, RC), 2 * GW:3 * GW] = (yc0 * (z * _sig(z))).astype(BF16)
    _chunks(tt, c3)


def _fwd_layer(x, win_b, wout_b, prm, ln_g, ln_b, target, *, tt, last):
    t_len = x.shape[0]
    n_t = t_len // tt

    def body(*refs):
        if last:
            (x_ref, win_hbm, wout_hbm, caw, cab, cbw, cbb, lbg, lbb, pw, pb, ps, lng, lnb, tgt_ref,
             h_ref, xb_ref, dz_ref, dln_ref, loss_ref,
             win_v, wout_v, y_scr, o_scr, q_ext, ub_ext, cu_ext, sh, p_scr, pl_scr, acc2, lacc) = refs
        else:
            (x_ref, win_hbm, wout_hbm, caw, cab, cbw, cbb, lbg, lbb, pw, pb, ps, lng, lnb,
             h_ref, xb_ref, z_ref, xn_ref,
             win_v, wout_v, y_scr, o_scr, q_ext, ub_ext, cu_ext, sh, p_scr, pl_scr) = refs
        i = pl.program_id(0)

        @pl.when(i == 0)
        def _():
            pltpu.sync_copy(win_hbm, win_v)
            pltpu.sync_copy(wout_hbm, wout_v)
            q_ext[0:SUBLANES, :] = jnp.zeros((SUBLANES, GW), F32)
            ub_ext[0:HALO, :] = jnp.zeros((HALO, GW), F32)
            cu_ext[0:16, :] = jnp.zeros((16, GW), F32)
            if last:
                acc2[...] = jnp.zeros_like(acc2)
                lacc[...] = jnp.zeros_like(lacc)

        xb_ref[...] = x_ref[...].astype(BF16)
        for j in range(D_IN // GW):
            h_ref[:, j * GW:(j + 1) * GW] = jnp.dot(
                xb_ref[...], win_v[:, j * GW:(j + 1) * GW], preferred_element_type=F32).astype(BF16)

        _fwd_mixers(h_ref, y_scr, q_ext, ub_ext, cu_ext, sh, p_scr, pl_scr,
                    (caw, cab, cbw, cbb, lbg, lbb, pw, pb, ps), tt, i * tt)

        o_scr[...] = jnp.dot(y_scr[...], wout_v[...], preferred_element_type=F32)

        def post(base):
            rows = pl.ds(base, RC)
            z = ALPHA * x_ref[rows, :] + o_scr[rows, :]
            xhat, rstd = _ln_stats(z)
            xn = xhat * lng[...] + lnb[...]
            if last:
                err = xn - tgt_ref[rows, :]
                lacc[...] += _fold8(err * err)
                dxn = err * (1.0 / D_MODEL)
                acc2[0] += _fold8(dxn * xhat)
                acc2[1] += _fold8(dxn)
                dz_ref[rows, :] = _ln_bwd(dxn, xhat, rstd, lng[...])
            else:
                z_ref[rows, :] = z
                xn_ref[rows, :] = xn
        _chunks(tt, post)

        if last:
            @pl.when(i == n_t - 1)
            def _():
                dln_ref[...] = jnp.sum(acc2[...], axis=1)
                loss_ref[...] = jnp.zeros((SUBLANES, 128), F32) + (0.5 / D_MODEL) * jnp.sum(lacc[...])

    tile = lambda c: pl.BlockSpec((tt, c), lambda i: (i, 0))
    full = lambda a: pl.BlockSpec(a.shape, lambda i: (0,) * a.ndim)
    hbm = pl.BlockSpec(memory_space=pl.ANY)
    ins = [x, win_b, wout_b, *prm, ln_g, ln_b] + ([target] if last else [])
    in_specs = [tile(D_MODEL), hbm, hbm] + [full(a) for a in (*prm, ln_g, ln_b)] + ([tile(D_MODEL)] if last else [])
    out_shape = [jax.ShapeDtypeStruct((t_len, D_IN), BF16), jax.ShapeDtypeStruct((t_len, D_MODEL), BF16)]
    out_specs = [tile(D_IN), tile(D_MODEL)]
    if last:
        out_shape += [jax.ShapeDtypeStruct((t_len, D_MODEL), F32), jax.ShapeDtypeStruct((2, D_MODEL), F32),
                      jax.ShapeDtypeStruct((SUBLANES, 128), F32)]
        out_specs += [tile(D_MODEL), pl.BlockSpec((2, D_MODEL), lambda i: (0, 0)),
                      pl.BlockSpec((SUBLANES, 128), lambda i: (0, 0))]
    else:
        out_shape += [jax.ShapeDtypeStruct((t_len, D_MODEL), F32), jax.ShapeDtypeStruct((t_len, D_MODEL), F32)]
        out_specs += [tile(D_MODEL), tile(D_MODEL)]
    scratch = [
        pltpu.VMEM((D_MODEL, D_IN), BF16), pltpu.VMEM((D_MIX, D_MODEL), BF16),
        pltpu.VMEM((tt, D_MIX), BF16), pltpu.VMEM((tt, D_MODEL), F32),
        pltpu.VMEM((tt + SUBLANES, GW), F32), pltpu.VMEM((tt + HALO, GW), F32), pltpu.VMEM((tt + 16, GW), F32),
        pltpu.VMEM((SUBLANES, tt + HALO, GW), F32),
        pltpu.VMEM((tt, GW), BF16), pltpu.VMEM((tt, GW), F32),
    ]
    if last:
        scratch += [pltpu.VMEM((2, SUBLANES, D_MODEL), F32), pltpu.VMEM((SUBLANES, D_MODEL), F32)]
    return pl.pallas_call(
        body, name="fwd_last" if last else "fwd_layer", grid=(n_t,),
        in_specs=in_specs, out_specs=out_specs, out_shape=out_shape, scratch_shapes=scratch,
        compiler_params=pltpu.CompilerParams(dimension_semantics=("arbitrary",), vmem_limit_bytes=VMEM_LIMIT),
    )(*ins)


def _dsilu(z, sz):
    return sz * (1.0 + z * (1.0 - sz))


def _bwd_layer(dz, h, win_b, wout_b, prm, z_prev, lng_prev, *, tt):
    t_len = dz.shape[0]
    n_t = t_len // tt
    has_prev = z_prev is not None
    hb = tt // HALO

    def body(*refs):
        refs = list(refs)
        dz_ref, h_ref, halo_ref, win_hbm, wout_hbm, caw, cab, cbw, cbb, lbg, lbb, pw, pb, ps = refs[:14]
        k = 14
        if has_prev:
            zp_ref, lngp = refs[k:k + 2]
            k += 2
        dxo_ref, dh_ref, dwout_hbm, small_ref, dpw_ref = refs[k:k + 5]
        k += 5
        if has_prev:
            dlnp_ref = refs[k]
            k += 1
        (win_v, wout_v, dzb, dy_scr, y_scr, dx_scr, q_ext, ub_ext, cu_ext, dca_ext, dcb_ext, dpn_ext, sh,
         p_scr, pl_scr, dpl_scr, dp_scr, racc, dpw_acc, dwout_acc) = refs[k:k + 20]
        k += 20
        if has_prev:
            acc2 = refs[k]
        i = pl.program_id(0)
        ti = n_t - 1 - i
        t0 = ti * tt

        @pl.when(i == 0)
        def _():
            pltpu.sync_copy(win_hbm, win_v)
            pltpu.sync_copy(wout_hbm, wout_v)
            dca_ext[tt:tt + SUBLANES, :] = jnp.zeros((SUBLANES, GW), F32)
            dcb_ext[tt:tt + HALO, :] = jnp.zeros((HALO, GW), F32)
            dpn_ext[tt:tt + 16, :] = jnp.zeros((16, GW), F32)
            racc[...] = jnp.zeros_like(racc)
            dpw_acc[...] = jnp.zeros_like(dpw_acc)
            dwout_acc[...] = jnp.zeros_like(dwout_acc)
            if has_prev:
                acc2[...] = jnp.zeros_like(acc2)

        dzb[...] = dz_ref[...].astype(BF16)
        dy_scr[...] = lax.dot_general(dzb[...], wout_v[...], (((1,), (1,)), ((), ())), preferred_element_type=F32)

        live = (ti > 0).astype(F32)
        hh = lambda j, r0, r1: halo_ref[r0:r1, j * GW:(j + 1) * GW].astype(F32)
        q_ext[0:SUBLANES, :] = live * hh(1, 24, 32) * hh(2, 24, 32)
        ub_ext[0:HALO, :] = live * hh(4, 0, 32) * _sig(hh(5, 0, 32))
        cu_ext[0:16, :] = live * hh(7, 16, 32)

        def a1(base):
            q_ext[pl.ds(SUBLANES + base, RC), :] = _hcol(h_ref, 1, base) * _hcol(h_ref, 2, base)
        _chunks(tt, a1)
        _build_shifts(q_ext, sh, (6, 7), tt)

        def a2(base):
            rows = pl.ds(base, RC)
            q6, q7, q8 = _tap(q_ext, sh, 6, base), _tap(q_ext, sh, 7, base), _tap(q_ext, sh, 8, base)
            ca = cab[...] + caw[0:1, :] * q6 + caw[1:2, :] * q7 + caw[2:3, :] * q8
            bg, z = _hcol(h_ref, 0, base), _hcol(h_ref, 3, base)
            sz = _sig(z)
            sza = z * sz
            dya = dy_scr[rows, 0:GW]
            ya0 = bg * ca
            y_scr[rows, 0:GW] = (ya0 * sza).astype(BF16)
            dya0 = dya * sza
            dh_ref[rows, 3 * GW:4 * GW] = (dya * ya0 * _dsilu(z, sz)).astype(BF16)
            dh_ref[rows, 0:GW] = (dya0 * ca).astype(BF16)
            dca = dya0 * bg
            dca_ext[rows, :] = dca
            racc[R_DCAB] += _fold8(dca)
            racc[R_DWA + 0] += _fold8(dca * q6)
            racc[R_DWA + 1] += _fold8(dca * q7)
            racc[R_DWA + 2] += _fold8(dca * q8)
        _chunks(tt, a2)
        _build_shifts(dca_ext, sh, (1, 2), tt)

        def a3(base):
            rows = pl.ds(base, RC)
            dq = caw[0:1, :] * _tap(dca_ext, sh, 2, base) + caw[1:2, :] * _tap(dca_ext, sh, 1, base) \
                + caw[2:3, :] * _tap(dca_ext, sh, 0, base)
            dh_ref[rows, GW:2 * GW] = (dq * _hcol(h_ref, 2, base)).astype(BF16)
            dh_ref[rows, 2 * GW:3 * GW] = (dq * _hcol(h_ref, 1, base)).astype(BF16)
        _chunks(tt, a3)
        dca_ext[tt:tt + SUBLANES, :] = dca_ext[0:SUBLANES, :]

        def b1(base):
            ub_ext[pl.ds(HALO + base, RC), :] = _hcol(h_ref, 4, base) * _sig(_hcol(h_ref, 5, base))
        _chunks(tt, b1)
        _build_shifts(ub_ext, sh, range(1, 8), tt + HALO - SUBLANES)

        def b2(base):
            rows = pl.ds(base, RC)
            cb = cbb[...] + jnp.zeros((RC, GW), F32)
            for kk in range(KB):
                cb = cb + cbw[kk:kk + 1, :] * _tap(ub_ext, sh, 2 + kk, base)
            xhat, rstd = _ln_stats(cb)
            lnv = xhat * lbg[...] + lbb[...]
            sl = _sig(lnv)
            s = lnv * sl
            z = _hcol(h_ref, 6, base)
            sz = _sig(z)
            szb = z * sz
            y_scr[rows, GW:2 * GW] = (s * szb).astype(BF16)
            dyb = dy_scr[rows, GW:2 * GW]
            dh_ref[rows, 6 * GW:7 * GW] = (dyb * s * _dsilu(z, sz)).astype(BF16)
            dlnv = dyb * szb * _dsilu(lnv, sl)
            racc[R_DLBG] += _fold8(dlnv * xhat)
            racc[R_DLBB] += _fold8(dlnv)
            dcb = _ln_bwd(dlnv, xhat, rstd, lbg[...])
            dcb_ext[rows, :] = dcb
            racc[R_DCBB] += _fold8(dcb)
            for kk in range(KB):
                racc[R_DWB + kk] += _fold8(dcb * _tap(ub_ext, sh, 2 + kk, base))
        _chunks(tt, b2)
        _build_shifts(dcb_ext, sh, range(1, 8), tt + HALO - SUBLANES)

        def b3(base):
            rows = pl.ds(base, RC)
            dub = jnp.zeros((RC, GW), F32)
            for kk in range(KB):
                dub = dub + cbw[kk:kk + 1, :] * _tap(dcb_ext, sh, KB - 1 - kk, base)
            v, gt = _hcol(h_ref, 4, base), _hcol(h_ref, 5, base)
            sg = _sig(gt)
            dh_ref[rows, 4 * GW:5 * GW] = (dub * sg).astype(BF16)
            dh_ref[rows, 5 * GW:6 * GW] = (dub * v * sg * (1.0 - sg)).astype(BF16)
        _chunks(tt, b3)
        dcb_ext[tt:tt + HALO, :] = dcb_ext[0:HALO, :]

        def c1(base):
            cu_ext[pl.ds(16 + base, RC), :] = _hcol(h_ref, 7, base)
        _chunks(tt, c1)
        _build_shifts(cu_ext, sh, range(1, 8), tt + SUBLANES)

        def c2(base):
            for g, w in enumerate(POOL_WINDOWS):
                lanes = slice(g * PGD, (g + 1) * PGD)
                acc = _tap(cu_ext, sh, 16, base, lanes)
                for j in range(1, w):
                    acc = acc + _tap(cu_ext, sh, 16 - j, base, lanes)
                p = acc * _inv_count(base, t0, w) - _tap(cu_ext, sh, 16, base, lanes)
                p_scr[pl.ds(base, RC), lanes] = p.astype(BF16)
        _chunks(tt, c2)
        for g in range(len(POOL_WINDOWS)):
            lanes = slice(g * PGD, (g + 1) * PGD)
            pl_scr[:, lanes] = jnp.dot(p_scr[:, lanes], pw[g], preferred_element_type=F32)

        def c3(base):
            rows = pl.ds(base, RC)
            z = _hcol(h_ref, 8, base)
            sz = _sig(z)
            szc = z * sz
            plb = pl_scr[rows, :] + pb[...]
            yc0 = plb * ps[...]
            y_scr[rows, 2 * GW:3 * GW] = (yc0 * szc).astype(BF16)
            dyc = dy_scr[rows, 2 * GW:3 * GW]
            dh_ref[rows, 8 * GW:9 * GW] = (dyc * yc0 * _dsilu(z, sz)).astype(BF16)
            dyc0 = dyc * szc
            racc[R_DPS] += _fold8(dyc0 * plb)
            dpl = dyc0 * ps[...]
            racc[R_DPB] += _fold8(dpl)
            dpl_scr[rows, :] = dpl.astype(BF16)
        _chunks(tt, c3)
        for g in range(len(POOL_WINDOWS)):
            lanes = slice(g * PGD, (g + 1) * PGD)
            dpw_acc[g] += lax.dot_general(p_scr[:, lanes], dpl_scr[:, lanes], (((0,), (0,)), ((), ())),
                                          preferred_element_type=F32)
            dp_scr[:, lanes] = lax.dot_general(dpl_scr[:, lanes], pw[g], (((1,), (1,)), ((), ())),
                                               preferred_element_type=F32)

        def c4(base):
            rows = pl.ds(base, RC)
            for g, w in enumerate(POOL_WINDOWS):
                lanes = slice(g * PGD, (g + 1) * PGD)
                dpn_ext[rows, lanes] = dp_scr[rows, lanes] * _inv_count(base, t0, w)
        _chunks(tt, c4)
        _build_shifts(dpn_ext, sh, range(1, 8), tt + SUBLANES)

        def c5(base):
            rows = pl.ds(base, RC)
            for g, w in enumerate(POOL_WINDOWS):
                lanes = slice(g * PGD, (g + 1) * PGD)
                acc = _tap(dpn_ext, sh, 0, base, lanes)
                for j in range(1, w):
                    acc = acc + _tap(dpn_ext, sh, j, base, lanes)
                dh_ref[rows, 7 * GW + g * PGD:7 * GW + (g + 1) * PGD] = (acc - dp_scr[rows, lanes]).astype(BF16)
        _chunks(tt, c5)
        dpn_ext[tt:tt + 16, :] = dpn_ext[0:16, :]

        for r in range(D_MIX // GW):
            dwout_acc[r * GW:(r + 1) * GW, :] += lax.dot_general(
                y_scr[:, r * GW:(r + 1) * GW], dzb[...], (((0,), (0,)), ((), ())), preferred_element_type=F32)
        dx_scr[...] = lax.dot_general(dh_ref[...], win_v[...], (((1,), (1,)), ((), ())), preferred_element_type=F32)

        def post(base):
            rows = pl.ds(base, RC)
            dx = ALPHA * dz_ref[rows, :] + dx_scr[rows, :]
            if has_prev:
                xhat, rstd = _ln_stats(zp_ref[rows, :])
                acc2[0] += _fold8(dx * xhat)
                acc2[1] += _fold8(dx)
                dxo_ref[rows, :] = _ln_bwd(dx, xhat, rstd, lngp[...])
            else:
                dxo_ref[rows, :] = dx
        _chunks(tt, post)

        @pl.when(i == n_t - 1)
        def _():
            small_ref[...] = jnp.sum(racc[...], axis=1)
            dpw_ref[...] = dpw_acc[...]
            pltpu.sync_copy(dwout_acc, dwout_hbm)
            if has_prev:
                dlnp_ref[...] = jnp.sum(acc2[...], axis=1)

    rtile = lambda c: pl.BlockSpec((tt, c), lambda i: (n_t - 1 - i, 0))
    full = lambda a: pl.BlockSpec(a.shape, lambda i: (0,) * a.ndim)
    const = lambda shp: pl.BlockSpec(shp, lambda i: (0,) * len(shp))
    hbm = pl.BlockSpec(memory_space=pl.ANY)
    halo_spec = pl.BlockSpec((HALO, D_IN), lambda i: (jnp.maximum((n_t - 1 - i) * hb - 1, 0), 0))
    ins = [dz, h, h, win_b, wout_b, *prm] + ([z_prev, lng_prev] if has_prev else [])
    in_specs = [rtile(D_MODEL), rtile(D_IN), halo_spec, hbm, hbm] + [full(a) for a in prm] \
        + ([rtile(D_MODEL), full(lng_prev)] if has_prev else [])
    out_shape = [jax.ShapeDtypeStruct((t_len, D_MODEL), F32), jax.ShapeDtypeStruct((t_len, D_IN), BF16),
                 jax.ShapeDtypeStruct((D_MIX, D_MODEL), F32), jax.ShapeDtypeStruct((N_RACC, GW), F32),
                 jax.ShapeDtypeStruct((len(POOL_WINDOWS), PGD, PGD), F32)]
    out_specs = [rtile(D_MODEL), rtile(D_IN), hbm, const((N_RACC, GW)), const((len(POOL_WINDOWS), PGD, PGD))]
    if has_prev:
        out_shape.append(jax.ShapeDtypeStruct((2, D_MODEL), F32))
        out_specs.append(const((2, D_MODEL)))
    scratch = [
        pltpu.VMEM((D_MODEL, D_IN), BF16), pltpu.VMEM((D_MIX, D_MODEL), BF16),
        pltpu.VMEM((tt, D_MODEL), BF16), pltpu.VMEM((tt, D_MIX), F32), pltpu.VMEM((tt, D_MIX), BF16),
        pltpu.VMEM((tt, D_MODEL), F32),
        pltpu.VMEM((tt + SUBLANES, GW), F32), pltpu.VMEM((tt + HALO, GW), F32), pltpu.VMEM((tt + 16, GW), F32),
        pltpu.VMEM((tt + SUBLANES, GW), F32), pltpu.VMEM((tt + HALO, GW), F32), pltpu.VMEM((tt + 16, GW), F32),
        pltpu.VMEM((SUBLANES, tt + HALO, GW), F32),
        pltpu.VMEM((tt, GW), BF16), pltpu.VMEM((tt, GW), F32), pltpu.VMEM((tt, GW), BF16), pltpu.VMEM((tt, GW), F32),
        pltpu.VMEM((N_RACC, SUBLANES, GW), F32), pltpu.VMEM((len(POOL_WINDOWS), PGD, PGD), F32),
        pltpu.VMEM((D_MIX, D_MODEL), F32),
    ]
    if has_prev:
        scratch.append(pltpu.VMEM((2, SUBLANES, D_MODEL), F32))
    return pl.pallas_call(
        body, name="bwd_layer_prev" if has_prev else "bwd_layer", grid=(n_t,),
        in_specs=in_specs, out_specs=out_specs, out_shape=out_shape, scratch_shapes=scratch,
        compiler_params=pltpu.CompilerParams(dimension_semantics=("arbitrary",), vmem_limit_bytes=VMEM_LIMIT),
    )(*ins)


def _wgrad_in(xb, dh, *, tk):
    t_len = xb.shape[0]
    n_k = t_len // tk

    def body(x_ref, dh_ref, o_ref):
        @pl.when(pl.program_id(1) == 0)
        def _():
            o_ref[...] = jnp.zeros_like(o_ref)
        o_ref[0] += lax.dot_general(x_ref[...], dh_ref[...], (((0,), (0,)), ((), ())), preferred_element_type=F32)

    return pl.pallas_call(
        body, name="wgrad_in", grid=(N_CHIP, n_k),
        in_specs=[pl.BlockSpec((tk, D_MODEL), lambda j, k: (k, 0)), pl.BlockSpec((tk, SHARD_IN), lambda j, k: (k, j))],
        out_specs=pl.BlockSpec((1, D_MODEL, SHARD_IN), lambda j, k: (j, 0, 0)),
        out_shape=jax.ShapeDtypeStruct((N_CHIP, D_MODEL, SHARD_IN), F32),
        compiler_params=pltpu.CompilerParams(dimension_semantics=("arbitrary", "arbitrary"), vmem_limit_bytes=VMEM_LIMIT),
    )(xb, dh)


MESH = pl.DeviceIdType.MESH
ANY = pl.BlockSpec(memory_space=pl.ANY)


def _place():
    x, y, c = lax.axis_index("x"), lax.axis_index("y"), lax.axis_index("c")
    others = [(1 - x, y), (x, 1 - y), (1 - x, 1 - y)]
    return x, y, c, 2 * x + y, [(ox, oy, 2 * ox + oy) for ox, oy in others]


def _rcopy(src, dst, send_sems, recv_sems, k, dev):
    return pltpu.make_async_remote_copy(src_ref=src, dst_ref=dst, send_sem=send_sems.at[k], recv_sem=recv_sems.at[k],
                                        device_id=dev, device_id_type=MESH)


def _gather_weights(w_in, w_out, cw):
    hi, ho = D_MODEL // 2, SHARD_OUT // 2

    def body(win_ref, wout_ref, cw_ref, owin, owout, ocw, bin_v, bout_v, send_sems, recv_sems, lsem):
        x, y, c, me, others = _place()
        for l in range(DEPTH):
            for r0 in range(0, D_MODEL, 256):
                bin_v[l, r0:r0 + 256, :] = win_ref[l, r0:r0 + 256, :].astype(BF16)
            bout_v[l] = wout_ref[l].astype(BF16)
        cin = pl.ds(pl.multiple_of(me * SHARD_IN, 128), SHARD_IN)
        rout = pl.ds(pl.multiple_of(me * SHARD_OUT, 128), SHARD_OUT)
        local = [pltpu.make_async_copy(bin_v, owin.at[:, :, cin], lsem.at[0]),
                 pltpu.make_async_copy(bout_v, owout.at[:, rout, :], lsem.at[1]),
                 pltpu.make_async_copy(cw_ref, ocw.at[me], lsem.at[2])]
        for cp in local:
            cp.start()

        def in_half(chip, core):
            return owin.at[:, pl.ds(pl.multiple_of(core * hi, 256), hi), pl.ds(pl.multiple_of(chip * SHARD_IN, 128), SHARD_IN)]

        def out_half(chip, core):
            return owout.at[:, pl.ds(pl.multiple_of(chip * SHARD_OUT + core * ho, 64), ho), :]

        first = []
        for k, (ox, oy, _) in enumerate(others):
            dev = (ox, oy, c)
            first.append(_rcopy(bin_v.at[:, pl.ds(pl.multiple_of(c * hi, 256), hi), :], in_half(me, c), send_sems, recv_sems, k, dev))
            first.append(_rcopy(bout_v.at[:, pl.ds(pl.multiple_of(c * ho, 64), ho), :], out_half(me, c), send_sems, recv_sems, 3 + k, dev))
            first.append(_rcopy(cw_ref, ocw.at[me], send_sems, recv_sems, 6 + k, dev))
        for cp in first:
            cp.start()
        sib = (x, y, 1 - c)
        passed = []
        for k, (ox, oy, oc) in enumerate(others):
            _rcopy(in_half(oc, c), in_half(oc, c), send_sems, recv_sems, k, sib).wait_recv()
            fwd_in = _rcopy(in_half(oc, c), in_half(oc, c), send_sems, recv_sems, 9 + k, sib)
            fwd_in.start()
            _rcopy(out_half(oc, c), out_half(oc, c), send_sems, recv_sems, 3 + k, sib).wait_recv()
            fwd_out = _rcopy(out_half(oc, c), out_half(oc, c), send_sems, recv_sems, 12 + k, sib)
            fwd_out.start()
            passed += [fwd_in, fwd_out]
        for k, (ox, oy, oc) in enumerate(others):
            _rcopy(cw_ref, ocw.at[oc], send_sems, recv_sems, 6 + k, sib).wait_recv()
            _rcopy(in_half(oc, 1 - c), in_half(oc, 1 - c), send_sems, recv_sems, 9 + k, sib).wait_recv()
            _rcopy(out_half(oc, 1 - c), out_half(oc, 1 - c), send_sems, recv_sems, 12 + k, sib).wait_recv()
        for cp in first + passed:
            cp.wait_send()
        for cp in local:
            cp.wait()

    vm = pl.BlockSpec(memory_space=pltpu.VMEM)
    return pl.pallas_call(
        body, name="gather_weights",
        in_specs=[vm, vm, vm], out_specs=[ANY, ANY, ANY],
        out_shape=[jax.ShapeDtypeStruct((DEPTH, D_MODEL, D_IN), BF16), jax.ShapeDtypeStruct((DEPTH, D_MIX, D_MODEL), BF16),
                   jax.ShapeDtypeStruct((N_CHIP,) + cw.shape, F32)],
        scratch_shapes=[pltpu.VMEM((DEPTH, D_MODEL, SHARD_IN), BF16), pltpu.VMEM((DEPTH, SHARD_OUT, D_MODEL), BF16),
                        pltpu.SemaphoreType.DMA((15,)), pltpu.SemaphoreType.DMA((15,)), pltpu.SemaphoreType.DMA((3,))],
        compiler_params=pltpu.CompilerParams(vmem_limit_bytes=VMEM_LIMIT),
    )(w_in, w_out, cw)


def _exchange_halves(arrs):
    n = len(arrs)

    def body(*refs):
        ins, outs, (send_sems, recv_sems) = refs[:n], refs[n:2 * n], refs[2 * n:]
        x, y, c, _, _ = _place()
        cps = []
        for m in range(n):
            half = ins[m].shape[1] // 2
            cps.append(_rcopy(ins[m].at[:, pl.ds((1 - c) * half, half), :], outs[m], send_sems, recv_sems, m, (x, y, 1 - c)))
        for cp in cps:
            cp.start()
        for cp in cps:
            cp.wait()

    return pl.pallas_call(
        body, name="exchange_halves", in_specs=[ANY] * n, out_specs=[ANY] * n,
        out_shape=[jax.ShapeDtypeStruct((a.shape[0], a.shape[1] // 2, a.shape[2]), F32) for a in arrs],
        scratch_shapes=[pltpu.SemaphoreType.DMA((n,)), pltpu.SemaphoreType.DMA((n,))],
    )(*arrs)


def _add_own_half(a, got, core, *, rb):
    nj, r, cdim = a.shape
    half = r // 2

    def body(core_ref, a_ref, g_ref, o_ref):
        o_ref[...] = a_ref[0] + g_ref[...]

    return pl.pallas_call(
        body, name="add_own_half",
        grid_spec=pltpu.PrefetchScalarGridSpec(
            num_scalar_prefetch=1, grid=(nj, half // rb),
            in_specs=[pl.BlockSpec((1, 1, rb, cdim), lambda j, i, cr: (j, cr[0], i, 0)),
                      pl.BlockSpec((1, rb, cdim), lambda j, i, cr: (j, i, 0))],
            out_specs=pl.BlockSpec((1, rb, cdim), lambda j, i, cr: (j, i, 0))),
        out_shape=jax.ShapeDtypeStruct((nj, half, cdim), F32),
    )(core, a.reshape(nj, 2, half, cdim), got)


def _send_to_owners(arrs):
    n = len(arrs)

    def body(*refs):
        ins, outs, (send_sems, recv_sems, lsem) = refs[:n], refs[n:2 * n], refs[2 * n:]
        x, y, c, me, others = _place()
        local = [pltpu.make_async_copy(ins[m].at[me], outs[m].at[me], lsem.at[m]) for m in range(n)]
        for cp in local:
            cp.start()
        cps = []
        for m in range(n):
            for k, (ox, oy, oc) in enumerate(others):
                cps.append(_rcopy(ins[m].at[oc], outs[m].at[me], send_sems, recv_sems, 3 * m + k, (ox, oy, c)))
        for cp in cps:
            cp.start()
        for cp in cps:
            cp.wait()
        for cp in local:
            cp.wait()

    return pl.pallas_call(
        body, name="send_to_owners", in_specs=[ANY] * n, out_specs=[ANY] * n,
        out_shape=[jax.ShapeDtypeStruct(a.shape, F32) for a in arrs],
        scratch_shapes=[pltpu.SemaphoreType.DMA((3 * n,)), pltpu.SemaphoreType.DMA((3 * n,)), pltpu.SemaphoreType.DMA((n,))],
    )(*arrs)


def _sum_chips(a, *, rb):
    nj, r, cdim = a.shape

    def body(a_ref, o_ref):
        o_ref[...] = ((a_ref[0] + a_ref[1]) + a_ref[2]) + a_ref[3]

    return pl.pallas_call(
        body, name="sum_chips", grid=(r // rb,),
        in_specs=[pl.BlockSpec((nj, rb, cdim), lambda i: (0, i, 0))],
        out_specs=pl.BlockSpec((rb, cdim), lambda i: (i, 0)),
        out_shape=jax.ShapeDtypeStruct((r, cdim), F32),
    )(a)


def _spread_reduced(red_in, red_out, red_small):
    hs = red_small.shape[0]

    def body(in0, in1, out0, out1, sm, fin, fout, fsm, gsm, send_sems, recv_sems, lsem):
        x, y, c, me, others = _place()
        sib = (x, y, 1 - c)
        hi, ho = D_MODEL // 2, SHARD_OUT // 2
        jobs = [(in0, fin.at[0, pl.ds(c * hi, hi), :]), (in1, fin.at[1, pl.ds(c * hi, hi), :]),
                (out0, fout.at[0, pl.ds(c * ho, ho), :]), (out1, fout.at[1, pl.ds(c * ho, ho), :])]
        local = [pltpu.make_async_copy(src, dst, lsem.at[m]) for m, (src, dst) in enumerate(jobs)]
        remote = [_rcopy(src, dst, send_sems, recv_sems, m, sib) for m, (src, dst) in enumerate(jobs)]
        own_small = pltpu.make_async_copy(sm, gsm.at[me], lsem.at[4])
        small = [_rcopy(sm, gsm.at[me], send_sems, recv_sems, 4 + k, (ox, oy, c)) for k, (ox, oy, _) in enumerate(others)]
        for cp in local + remote + [own_small] + small:
            cp.start()
        own_small.wait()
        for cp in small:
            cp.wait()
        mine = fsm.at[:, pl.ds(c * hs, hs), :]
        keep = pltpu.make_async_copy(gsm, mine, lsem.at[5])
        give = _rcopy(gsm, mine, send_sems, recv_sems, 7, sib)
        keep.start()
        give.start()
        for cp in remote + [give]:
            cp.wait()
        for cp in local + [keep]:
            cp.wait()

    return pl.pallas_call(
        body, name="spread_reduced", in_specs=[ANY] * 5, out_specs=[ANY] * 4,
        out_shape=[jax.ShapeDtypeStruct((DEPTH, D_MODEL, SHARD_IN), F32), jax.ShapeDtypeStruct((DEPTH, SHARD_OUT, D_MODEL), F32),
                   jax.ShapeDtypeStruct((N_CHIP, 2 * hs, GW), F32), jax.ShapeDtypeStruct((N_CHIP, hs, GW), F32)],
        scratch_shapes=[pltpu.SemaphoreType.DMA((8,)), pltpu.SemaphoreType.DMA((8,)), pltpu.SemaphoreType.DMA((6,))],
    )(red_in[0], red_in[1], red_out[0], red_out[1], red_small)[:3]


def _adamw_math(w, g, m, v):
    m = ADAM_B1 * m + (1.0 - ADAM_B1) * g
    v = ADAM_B2 * v + (1.0 - ADAM_B2) * (g * g)
    m_hat = m / (1.0 - ADAM_B1 ** ADAM_STEP)
    v_hat = v / (1.0 - ADAM_B2 ** ADAM_STEP)
    delta = -ADAM_LR * (m_hat / (jnp.sqrt(v_hat) + ADAM_EPS) + ADAM_WD * w)
    return delta, m, v


def _adamw_big(w, g, m, v, *, rb):
    r, cdim = w.shape

    def body(w_ref, g_ref, m_ref, v_ref, d_ref, nm_ref, nv_ref):
        d_ref[...], nm_ref[...], nv_ref[...] = _adamw_math(w_ref[...], g_ref[...], m_ref[...], v_ref[...])

    spec = pl.BlockSpec((rb, cdim), lambda i: (i, 0))
    return pl.pallas_call(
        body, name="adamw_big", grid=(r // rb,), in_specs=[spec] * 4, out_specs=[spec] * 3,
        out_shape=[jax.ShapeDtypeStruct((r, cdim), F32)] * 3,
    )(w, g, m, v)


def _adamw_small(ws, gs, ms, vs):
    n = len(ws)

    def body(*refs):
        w, g, m, v = refs[:n], refs[n:2 * n], refs[2 * n:3 * n], refs[3 * n:4 * n]
        d, nm, nv = refs[4 * n:5 * n], refs[5 * n:6 * n], refs[6 * n:7 * n]
        for k in range(n):
            d[k][...], nm[k][...], nv[k][...] = _adamw_math(w[k][...], g[k][...], m[k][...], v[k][...])

    shapes = [jax.ShapeDtypeStruct(a.shape, F32) for a in ws]
    outs = pl.pallas_call(body, name="adamw_small", out_shape=shapes * 3)(*ws, *gs, *ms, *vs)
    return outs[:n], outs[n:2 * n], outs[2 * n:]


TT = 256
TK = 512
CW_ROWS = 40
SMALL_ROWS = 384


def _pack_small(small, dpw, dln):
    rows = [small[0], small[1], dpw[0].reshape(PGD, GW), dpw[1].reshape(PGD, GW), dln[0].reshape(4, GW), dln[1].reshape(4, GW)]
    packed = jnp.concatenate(rows, axis=0)
    packed = jnp.pad(packed, ((0, SMALL_ROWS - packed.shape[0]), (0, 0)))
    return packed.reshape(N_CHIP, SMALL_ROWS // N_CHIP, GW)


def _unpack_small(packed):
    p = packed.reshape(SMALL_ROWS, GW)
    small = [p[0:N_RACC], p[N_RACC:2 * N_RACC]]
    o = 2 * N_RACC
    dpw = [p[o:o + PGD].reshape(len(POOL_WINDOWS), PGD, PGD), p[o + PGD:o + 2 * PGD].reshape(len(POOL_WINDOWS), PGD, PGD)]
    o += 2 * PGD
    dln = [p[o:o + 4].reshape(2, D_MODEL), p[o + 4:o + 8].reshape(2, D_MODEL)]
    return small, dpw, dln


def kernel(x, w_in, conv_a_w, conv_a_b, conv_b_w, conv_b_b, ln_b_g, ln_b_b, pool_w, pool_b, pool_scale, w_out, ln_g, ln_b, loss_target, m_w_in, m_conv_a_w, m_conv_a_b, m_conv_b_w, m_conv_b_b, m_ln_b_g, m_ln_b_b, m_pool_w, m_pool_b, m_pool_scale, m_w_out, m_ln_g, m_ln_b, v_w_in, v_conv_a_w, v_conv_a_b, v_conv_b_w, v_conv_b_b, v_ln_b_g, v_ln_b_b, v_pool_w, v_pool_b, v_pool_scale, v_w_out, v_ln_g, v_ln_b):
    chip = 2 * lax.axis_index("x") + lax.axis_index("y")
    core = lax.axis_index("c")
    x2, tgt = x[0], loss_target[0]

    cw = jnp.zeros((DEPTH, CW_ROWS, PGD), F32).at[:, 0:KA].set(conv_a_w).at[:, 8:8 + KB].set(conv_b_w)
    win_b, wout_b, cw_all = _gather_weights(w_in, w_out, cw)
    cw_full = jnp.transpose(cw_all, (1, 2, 0, 3)).reshape(DEPTH, CW_ROWS, GW)
    row = lambda a, l: a[l].reshape(1, -1)
    prm = [(cw_full[l, 0:KA], row(conv_a_b, l), cw_full[l, 8:8 + KB], row(conv_b_b, l), row(ln_b_g, l), row(ln_b_b, l),
            pool_w[l].astype(BF16), row(pool_b, l), row(pool_scale, l)) for l in range(DEPTH)]

    h0, xb0, z0, x1 = _fwd_layer(x2, win_b[0], wout_b[0], prm[0], row(ln_g, 0), row(ln_b, 0), None, tt=TT, last=False)
    h1, xb1, dz1, dln1, loss8 = _fwd_layer(x1, win_b[1], wout_b[1], prm[1], row(ln_g, 1), row(ln_b, 1), tgt, tt=TT, last=True)

    dz0, dh1, dwout1, small1, dpw1, dln0 = _bwd_layer(dz1, h1, win_b[1], wout_b[1], prm[1], z0, row(ln_g, 0), tt=TT)
    dwin1 = _wgrad_in(xb1, dh1, tk=TK)
    gx, dh0, dwout0, small0, dpw0 = _bwd_layer(dz0, h0, win_b[0], wout_b[0], prm[0], None, None, tt=TT)
    dwin0 = _wgrad_in(xb0, dh0, tk=TK)

    parts = [dwin0, dwin1, dwout0.reshape(N_CHIP, SHARD_OUT, D_MODEL), dwout1.reshape(N_CHIP, SHARD_OUT, D_MODEL),
             _pack_small([small0, small1], [dpw0, dpw1], [dln0, dln1])]
    rbs = [256, 256, SHARD_OUT // 2, SHARD_OUT // 2, SMALL_ROWS // N_CHIP // 2]
    got = _exchange_halves(parts)
    core1 = core.reshape(1).astype(jnp.int32)
    chip_sums = [_add_own_half(a, g, core1, rb=rb) for a, g, rb in zip(parts, got, rbs)]
    landed = _send_to_owners(chip_sums)
    red = [_sum_chips(a, rb=rb) for a, rb in zip(landed, rbs)]
    g_in, g_out, g_small = _spread_reduced(red[0:2], red[2:4], red[4])

    flat = lambda a: a.reshape(-1, a.shape[-1])
    unflat = lambda a, like: a.reshape(like.shape)
    d_in, nm_in, nv_in = [unflat(a, w_in) for a in _adamw_big(flat(w_in), flat(g_in), flat(m_w_in), flat(v_w_in), rb=256)]
    d_out, nm_out, nv_out = [unflat(a, w_out) for a in _adamw_big(flat(w_out), flat(g_out), flat(m_w_out), flat(v_w_out), rb=SHARD_OUT)]

    small, dpw, dln = _unpack_small(g_small)
    mine = lambda a: lax.dynamic_slice_in_dim(a, chip * PGD, PGD, axis=-1)
    stack = lambda f: jnp.stack([f(0), f(1)])
    g_caw = stack(lambda l: mine(small[l][R_DWA:R_DWA + KA]))
    g_cab = stack(lambda l: small[l][R_DCAB])
    g_cbw = stack(lambda l: mine(small[l][R_DWB:R_DWB + KB]))
    g_cbb = stack(lambda l: small[l][R_DCBB])
    g_lbg = stack(lambda l: small[l][R_DLBG])
    g_lbb = stack(lambda l: small[l][R_DLBB])
    g_pw = stack(lambda l: dpw[l])
    g_pb = stack(lambda l: small[l][R_DPB].reshape(len(POOL_WINDOWS), PGD))
    g_ps = stack(lambda l: small[l][R_DPS])
    g_lng = jnp.stack([dln[0][0], dln[1][0]])
    g_lnb = jnp.stack([dln[0][1], dln[1][1]])
    ws = [conv_a_w, conv_a_b, conv_b_w, conv_b_b, ln_b_g, ln_b_b, pool_w, pool_b, pool_scale, ln_g, ln_b]
    gs = [g_caw, g_cab, g_cbw, g_cbb, g_lbg, g_lbb, g_pw, g_pb, g_ps, g_lng, g_lnb]
    ms = [m_conv_a_w, m_conv_a_b, m_conv_b_w, m_conv_b_b, m_ln_b_g, m_ln_b_b, m_pool_w, m_pool_b, m_pool_scale, m_ln_g, m_ln_b]
    vs = [v_conv_a_w, v_conv_a_b, v_conv_b_w, v_conv_b_b, v_ln_b_g, v_ln_b_b, v_pool_w, v_pool_b, v_pool_scale, v_ln_g, v_ln_b]
    ds, nms, nvs = _adamw_small([flat(a) for a in ws], [flat(a) for a in gs], [flat(a) for a in ms], [flat(a) for a in vs])
    ds, nms, nvs = ([unflat(a, w) for a, w in zip(t, ws)] for t in (ds, nms, nvs))

    loss = lax.psum(loss8[0, 0], ("x", "y", "c"))

    def order(in_, small_, out_):
        return [in_, *small_[:9], out_, *small_[9:]]
    return (loss, gx[None], *order(g_in, gs, g_out), *order(d_in, ds, d_out), *order(nm_in, nms, nm_out), *order(nv_in, nvs, nv_out))
```

```python
import functools

import jax
import jax.numpy as jnp
import numpy as np
from jax import lax
from jax.experimental import pallas as pl
from jax.experimental.pallas import tpu as pltpu

F32 = jnp.float32
BF16 = jnp.bfloat16

D_MODEL = 1024
DEPTH = 2
GW = 512
D_IN = 9 * GW
D_MIX = 3 * GW
POOL_WINDOWS = (2, 4, 8, 16)
PGD = 128
KA = 3
KB = 31
ALPHA = (2.0 * DEPTH) ** 0.25
LN_EPS = 1e-5
ADAM_LR, ADAM_B1, ADAM_B2, ADAM_EPS, ADAM_WD, ADAM_STEP = 0.001, 0.9, 0.999, 1e-08, 0.01, 10

N_CHIP = 4
SHARD_IN = D_IN // N_CHIP
SHARD_OUT = D_MIX // N_CHIP

SUBLANES = 8
RC = 32
HALO = 32
VMEM_LIMIT = 60 * 1024 * 1024

R_DWA, R_DCAB, R_DWB, R_DCBB, R_DLBG, R_DLBB, R_DPB, R_DPS, N_RACC = 0, 3, 4, 35, 36, 37, 38, 39, 40


def _sig(v):
    return 0.5 * jnp.tanh(0.5 * v) + 0.5


def _chunks(n_rows, fn, unroll=1):
    def step(c, carry):
        fn(pl.multiple_of(c * RC, RC))
        return carry
    lax.fori_loop(0, n_rows // RC, step, 0, unroll=unroll)


def _fold8(v):
    return v.reshape(RC // SUBLANES, SUBLANES, v.shape[-1]).sum(axis=0)


def _build_shifts(ext_ref, sh_ref, shifts, n_rows):
    for r in shifts:
        for c0 in range(0, n_rows, RC):
            n = min(RC, n_rows - c0)
            sh_ref[r, pl.ds(c0, n), :] = ext_ref[pl.ds(c0 + r, n), :]


def _tap(ext_ref, sh_ref, off, base, lanes=None):
    a, r = divmod(off, SUBLANES)
    src = ext_ref if r == 0 else sh_ref.at[r]
    if lanes is None:
        return src[pl.ds(base + SUBLANES * a, RC), :]
    return src[pl.ds(base + SUBLANES * a, RC), lanes]


def _ln_stats(v):
    mu = jnp.mean(v, axis=-1, keepdims=True)
    vc = v - mu
    var = jnp.mean(vc * vc, axis=-1, keepdims=True)
    rstd = lax.rsqrt(var + LN_EPS)
    return vc * rstd, rstd


def _ln_bwd(dy, xhat, rstd, g):
    dxh = dy * g
    m1 = jnp.mean(dxh, axis=-1, keepdims=True)
    m2 = jnp.mean(dxh * xhat, axis=-1, keepdims=True)
    return rstd * (dxh - m1 - xhat * m2)


def _for_taps(ext_ref, sh_ref, base, offsets, fn):
    for r in range(SUBLANES):
        offs = [o for o in offsets if o % SUBLANES == r]
        if not offs:
            continue
        a0, a1 = min(offs) // SUBLANES, max(offs) // SUBLANES
        src = ext_ref if r == 0 else sh_ref.at[r]
        win = src[pl.ds(base + SUBLANES * a0, RC + SUBLANES * (a1 - a0)), :]
        for o in offs:
            a = o // SUBLANES - a0
            fn(o, win[SUBLANES * a:SUBLANES * a + RC])


def _count_table():
    t = np.arange(1, RC + 1, dtype=np.float64)[:, None]
    w = np.repeat(np.asarray(POOL_WINDOWS, np.float64), PGD)[None, :]
    return jnp.asarray(1.0 / np.minimum(t, w), F32)


def _inv_count(cnt_ref, first):
    return jnp.where(first, cnt_ref[...], cnt_ref[RC - 1:RC, :])


def _hcol(h_ref, j, base):
    return h_ref[pl.ds(base, RC), j * GW:(j + 1) * GW].astype(F32)


def _fwd_mixers(h_ref, y_scr, q_ext, ub_ext, cu_ext, sh, p_scr, pl_scr, prm, tt, t0):
    caw, cab, cbw, cbb, lbg, lbb, pw, pb, ps, cnt = prm

    def a1(base):
        q_ext[pl.ds(SUBLANES + base, RC), :] = _hcol(h_ref, 1, base) * _hcol(h_ref, 2, base)
    _chunks(tt, a1)
    _build_shifts(q_ext, sh, (6, 7), tt)

    def a2(base):
        ca = cab[...] + caw[0:1, :] * _tap(q_ext, sh, 6, base) + caw[1:2, :] * _tap(q_ext, sh, 7, base) \
            + caw[2:3, :] * _tap(q_ext, sh, 8, base)
        z = _hcol(h_ref, 3, base)
        y_scr[pl.ds(base, RC), 0:GW] = (_hcol(h_ref, 0, base) * ca * (z * _sig(z))).astype(BF16)
    _chunks(tt, a2, unroll=2)
    q_ext[0:SUBLANES, :] = q_ext[tt:tt + SUBLANES, :]

    def b1(base):
        ub_ext[pl.ds(HALO + base, RC), :] = _hcol(h_ref, 4, base) * _sig(_hcol(h_ref, 5, base))
    _chunks(tt, b1)
    _build_shifts(ub_ext, sh, range(1, 8), tt + HALO - SUBLANES)

    def b2(base):
        cb = [cbb[...] + jnp.zeros((RC, GW), F32)]

        def tap(off, v):
            cb[0] = cb[0] + cbw[off - 2:off - 1, :] * v
        _for_taps(ub_ext, sh, base, range(2, 2 + KB), tap)
        xhat, _ = _ln_stats(cb[0])
        lnv = xhat * lbg[...] + lbb[...]
        z = _hcol(h_ref, 6, base)
        y_scr[pl.ds(base, RC), GW:2 * GW] = (lnv * _sig(lnv) * (z * _sig(z))).astype(BF16)
    _chunks(tt, b2, unroll=2)
    ub_ext[0:HALO, :] = ub_ext[tt:tt + HALO, :]

    def c1(base):
        cu_ext[pl.ds(16 + base, RC), :] = _hcol(h_ref, 7, base)
    _chunks(tt, c1)
    _build_shifts(cu_ext, sh, range(1, 8), tt + SUBLANES)

    def c2(base):
        ic = _inv_count(cnt, base + t0 == 0)
        for g, w in enumerate(POOL_WINDOWS):
            lanes = slice(g * PGD, (g + 1) * PGD)
            acc = _tap(cu_ext, sh, 16, base, lanes)
            for j in range(1, w):
                acc = acc + _tap(cu_ext, sh, 16 - j, base, lanes)
            p = acc * ic[:, lanes] - _tap(cu_ext, sh, 16, base, lanes)
            p_scr[pl.ds(base, RC), lanes] = p.astype(BF16)
    _chunks(tt, c2)
    cu_ext[0:16, :] = cu_ext[tt:tt + 16, :]
    for g in range(len(POOL_WINDOWS)):
        lanes = slice(g * PGD, (g + 1) * PGD)
        pl_scr[:, lanes] = jnp.dot(p_scr[:, lanes], pw[g], preferred_element_type=F32)

    def c3(base):
        z = _hcol(h_ref, 8, base)
        yc0 = (pl_scr[pl.ds(base, RC), :] + pb[...]) * ps[...]
        y_scr[pl.ds(base, RC), 2 * GW:3 * GW] = (yc0 * (z * _sig(z))).astype(BF16)
    _chunks(tt, c3, unroll=2)


def _fwd_layer(x, win_b, wout_b, prm, ln_g, ln_b, target, *, tt, last):
    t_len = x.shape[0]
    n_t = t_len // tt

    def body(*refs):
        if last:
            (x_ref, win_hbm, wout_hbm, caw, cab, cbw, cbb, lbg, lbb, pw, pb, ps, cnt, lng, lnb, tgt_ref,
             h_ref, xb_ref, dz_ref, dln_ref, loss_ref,
             win_v, wout_v, y_scr, o_scr, q_ext, ub_ext, cu_ext, sh, p_scr, pl_scr, acc2, lacc) = refs
        else:
            (x_ref, win_hbm, wout_hbm, caw, cab, cbw, cbb, lbg, lbb, pw, pb, ps, cnt, lng, lnb,
             h_ref, xb_ref, z_ref, xn_ref,
             win_v, wout_v, y_scr, o_scr, q_ext, ub_ext, cu_ext, sh, p_scr, pl_scr) = refs
        i = pl.program_id(0)

        @pl.when(i == 0)
        def _():
            pltpu.sync_copy(win_hbm, win_v)
            pltpu.sync_copy(wout_hbm, wout_v)
            q_ext[0:SUBLANES, :] = jnp.zeros((SUBLANES, GW), F32)
            ub_ext[0:HALO, :] = jnp.zeros((HALO, GW), F32)
            cu_ext[0:16, :] = jnp.zeros((16, GW), F32)
            if last:
                acc2[...] = jnp.zeros_like(acc2)
                lacc[...] = jnp.zeros_like(lacc)

        xb_ref[...] = x_ref[...].astype(BF16)
        for j in range(D_IN // GW):
            h_ref[:, j * GW:(j + 1) * GW] = jnp.dot(
                xb_ref[...], win_v[:, j * GW:(j + 1) * GW], preferred_element_type=F32).astype(BF16)

        _fwd_mixers(h_ref, y_scr, q_ext, ub_ext, cu_ext, sh, p_scr, pl_scr,
                    (caw, cab, cbw, cbb, lbg, lbb, pw, pb, ps, cnt), tt, i * tt)

        o_scr[...] = jnp.dot(y_scr[...], wout_v[...], preferred_element_type=F32)

        def post(base):
            rows = pl.ds(base, RC)
            z = ALPHA * x_ref[rows, :] + o_scr[rows, :]
            xhat, rstd = _ln_stats(z)
            xn = xhat * lng[...] + lnb[...]
            if last:
                err = xn - tgt_ref[rows, :]
                lacc[...] += _fold8(err * err)
                dxn = err * (1.0 / D_MODEL)
                acc2[0] += _fold8(dxn * xhat)
                acc2[1] += _fold8(dxn)
                dz_ref[rows, :] = _ln_bwd(dxn, xhat, rstd, lng[...])
            else:
                z_ref[rows, :] = z
                xn_ref[rows, :] = xn
        _chunks(tt, post, unroll=2)

        if last:
            @pl.when(i == n_t - 1)
            def _():
                dln_ref[...] = jnp.sum(acc2[...], axis=1)
                loss_ref[...] = jnp.zeros((SUBLANES, 128), F32) + (0.5 / D_MODEL) * jnp.sum(lacc[...])

    tile = lambda c: pl.BlockSpec((tt, c), lambda i: (i, 0))
    full = lambda a: pl.BlockSpec(a.shape, lambda i: (0,) * a.ndim)
    hbm = pl.BlockSpec(memory_space=pl.ANY)
    ins = [x, win_b, wout_b, *prm, ln_g, ln_b] + ([target] if last else [])
    in_specs = [tile(D_MODEL), hbm, hbm] + [full(a) for a in (*prm, ln_g, ln_b)] + ([tile(D_MODEL)] if last else [])
    out_shape = [jax.ShapeDtypeStruct((t_len, D_IN), BF16), jax.ShapeDtypeStruct((t_len, D_MODEL), BF16)]
    out_specs = [tile(D_IN), tile(D_MODEL)]
    if last:
        out_shape += [jax.ShapeDtypeStruct((t_len, D_MODEL), F32), jax.ShapeDtypeStruct((2, D_MODEL), F32),
                      jax.ShapeDtypeStruct((SUBLANES, 128), F32)]
        out_specs += [tile(D_MODEL), pl.BlockSpec((2, D_MODEL), lambda i: (0, 0)),
                      pl.BlockSpec((SUBLANES, 128), lambda i: (0, 0))]
    else:
        out_shape += [jax.ShapeDtypeStruct((t_len, D_MODEL), F32), jax.ShapeDtypeStruct((t_len, D_MODEL), F32)]
        out_specs += [tile(D_MODEL), tile(D_MODEL)]
    scratch = [
        pltpu.VMEM((D_MODEL, D_IN), BF16), pltpu.VMEM((D_MIX, D_MODEL), BF16),
        pltpu.VMEM((tt, D_MIX), BF16), pltpu.VMEM((tt, D_MODEL), F32),
        pltpu.VMEM((tt + SUBLANES, GW), F32), pltpu.VMEM((tt + HALO, GW), F32), pltpu.VMEM((tt + 16, GW), F32),
        pltpu.VMEM((SUBLANES, tt + HALO, GW), F32),
        pltpu.VMEM((tt, GW), BF16), pltpu.VMEM((tt, GW), F32),
    ]
    if last:
        scratch += [pltpu.VMEM((2, SUBLANES, D_MODEL), F32), pltpu.VMEM((SUBLANES, D_MODEL), F32)]
    return pl.pallas_call(
        body, name="fwd_last" if last else "fwd_layer", grid=(n_t,),
        in_specs=in_specs, out_specs=out_specs, out_shape=out_shape, scratch_shapes=scratch,
        compiler_params=pltpu.CompilerParams(dimension_semantics=("arbitrary",), vmem_limit_bytes=VMEM_LIMIT),
    )(*ins)


def _dsilu(z, sz):
    return sz * (1.0 + z * (1.0 - sz))


def _bwd_layer(dz, h, win_b, wout_b, prm, z_prev, lng_prev, *, tt):
    t_len = dz.shape[0]
    n_t = t_len // tt
    has_prev = z_prev is not None
    hb = tt // HALO

    def body(*refs):
        refs = list(refs)
        dz_ref, h_ref, halo_ref, win_hbm, wout_hbm, caw, cab, cbw, cbb, lbg, lbb, pw, pb, ps, cnt = refs[:15]
        k = 15
        if has_prev:
            zp_ref, lngp = refs[k:k + 2]
            k += 2
        dxo_ref, dh_ref, dwout_hbm, small_ref, dpw_ref = refs[k:k + 5]
        k += 5
        if has_prev:
            dlnp_ref = refs[k]
            k += 1
        (win_v, wout_v, dzb, dy_scr, y_scr, dx_scr, q_ext, ub_ext, cu_ext, dca_ext, dcb_ext, dpn_ext, sh,
         p_scr, pl_scr, dpl_scr, dp_scr, racc, dpw_acc, dwout_acc) = refs[k:k + 20]
        k += 20
        if has_prev:
            acc2 = refs[k]
        i = pl.program_id(0)
        ti = n_t - 1 - i
        t0 = ti * tt

        @pl.when(i == 0)
        def _():
            pltpu.sync_copy(win_hbm, win_v)
            pltpu.sync_copy(wout_hbm, wout_v)
            dca_ext[tt:tt + SUBLANES, :] = jnp.zeros((SUBLANES, GW), F32)
            dcb_ext[tt:tt + HALO, :] = jnp.zeros((HALO, GW), F32)
            dpn_ext[tt:tt + 16, :] = jnp.zeros((16, GW), F32)
            racc[...] = jnp.zeros_like(racc)
            dpw_acc[...] = jnp.zeros_like(dpw_acc)
            dwout_acc[...] = jnp.zeros_like(dwout_acc)
            if has_prev:
                acc2[...] = jnp.zeros_like(acc2)

        dzb[...] = dz_ref[...].astype(BF16)
        dy_scr[...] = lax.dot_general(dzb[...], wout_v[...], (((1,), (1,)), ((), ())), preferred_element_type=F32)

        live = (ti > 0).astype(F32)
        hh = lambda j, r0, r1: halo_ref[r0:r1, j * GW:(j + 1) * GW].astype(F32)
        q_ext[0:SUBLANES, :] = live * hh(1, 24, 32) * hh(2, 24, 32)
        ub_ext[0:HALO, :] = live * hh(4, 0, 32) * _sig(hh(5, 0, 32))
        cu_ext[0:16, :] = live * hh(7, 16, 32)

        def a1(base):
            q_ext[pl.ds(SUBLANES + base, RC), :] = _hcol(h_ref, 1, base) * _hcol(h_ref, 2, base)
        _chunks(tt, a1)
        _build_shifts(q_ext, sh, (6, 7), tt)

        def a2(base):
            rows = pl.ds(base, RC)
            q6, q7, q8 = _tap(q_ext, sh, 6, base), _tap(q_ext, sh, 7, base), _tap(q_ext, sh, 8, base)
            ca = cab[...] + caw[0:1, :] * q6 + caw[1:2, :] * q7 + caw[2:3, :] * q8
            bg, z = _hcol(h_ref, 0, base), _hcol(h_ref, 3, base)
            sz = _sig(z)
            sza = z * sz
            dya = dy_scr[rows, 0:GW]
            ya0 = bg * ca
            y_scr[rows, 0:GW] = (ya0 * sza).astype(BF16)
            dya0 = dya * sza
            dh_ref[rows, 3 * GW:4 * GW] = (dya * ya0 * _dsilu(z, sz)).astype(BF16)
            dh_ref[rows, 0:GW] = (dya0 * ca).astype(BF16)
            dca = dya0 * bg
            dca_ext[rows, :] = dca
            racc[R_DCAB] += _fold8(dca)
            racc[R_DWA + 0] += _fold8(dca * q6)
            racc[R_DWA + 1] += _fold8(dca * q7)
            racc[R_DWA + 2] += _fold8(dca * q8)
        _chunks(tt, a2)
        _build_shifts(dca_ext, sh, (1, 2), tt)

        def a3(base):
            rows = pl.ds(base, RC)
            dq = caw[0:1, :] * _tap(dca_ext, sh, 2, base) + caw[1:2, :] * _tap(dca_ext, sh, 1, base) \
                + caw[2:3, :] * _tap(dca_ext, sh, 0, base)
            dh_ref[rows, GW:2 * GW] = (dq * _hcol(h_ref, 2, base)).astype(BF16)
            dh_ref[rows, 2 * GW:3 * GW] = (dq * _hcol(h_ref, 1, base)).astype(BF16)
        _chunks(tt, a3)
        dca_ext[tt:tt + SUBLANES, :] = dca_ext[0:SUBLANES, :]

        def b1(base):
            ub_ext[pl.ds(HALO + base, RC), :] = _hcol(h_ref, 4, base) * _sig(_hcol(h_ref, 5, base))
        _chunks(tt, b1)
        _build_shifts(ub_ext, sh, range(1, 8), tt + HALO - SUBLANES)

        def b2(base):
            rows = pl.ds(base, RC)
            cbv = [cbb[...] + jnp.zeros((RC, GW), F32)]

            def tap(off, v):
                cbv[0] = cbv[0] + cbw[off - 2:off - 1, :] * v
            _for_taps(ub_ext, sh, base, range(2, 2 + KB), tap)
            xhat, rstd = _ln_stats(cbv[0])
            lnv = xhat * lbg[...] + lbb[...]
            sl = _sig(lnv)
            s = lnv * sl
            z = _hcol(h_ref, 6, base)
            sz = _sig(z)
            szb = z * sz
            y_scr[rows, GW:2 * GW] = (s * szb).astype(BF16)
            dyb = dy_scr[rows, GW:2 * GW]
            dh_ref[rows, 6 * GW:7 * GW] = (dyb * s * _dsilu(z, sz)).astype(BF16)
            dlnv = dyb * szb * _dsilu(lnv, sl)
            racc[R_DLBG] += _fold8(dlnv * xhat)
            racc[R_DLBB] += _fold8(dlnv)
            dcb = _ln_bwd(dlnv, xhat, rstd, lbg[...])
            dcb_ext[rows, :] = dcb
            racc[R_DCBB] += _fold8(dcb)

            def wtap(off, v):
                racc[R_DWB + off - 2] += _fold8(dcb * v)
            _for_taps(ub_ext, sh, base, range(2, 2 + KB), wtap)
        _chunks(tt, b2, unroll=2)
        _build_shifts(dcb_ext, sh, range(1, 8), tt + HALO - SUBLANES)

        def b3(base):
            rows = pl.ds(base, RC)
            dubv = [jnp.zeros((RC, GW), F32)]

            def tap(off, v):
                dubv[0] = dubv[0] + cbw[KB - 1 - off:KB - off, :] * v
            _for_taps(dcb_ext, sh, base, range(KB), tap)
            dub = dubv[0]
            v, gt = _hcol(h_ref, 4, base), _hcol(h_ref, 5, base)
            sg = _sig(gt)
            dh_ref[rows, 4 * GW:5 * GW] = (dub * sg).astype(BF16)
            dh_ref[rows, 5 * GW:6 * GW] = (dub * v * sg * (1.0 - sg)).astype(BF16)
        _chunks(tt, b3)
        dcb_ext[tt:tt + HALO, :] = dcb_ext[0:HALO, :]

        def c1(base):
            cu_ext[pl.ds(16 + base, RC), :] = _hcol(h_ref, 7, base)
        _chunks(tt, c1)
        _build_shifts(cu_ext, sh, range(1, 8), tt + SUBLANES)

        def c2(base):
            ic = _inv_count(cnt, base + t0 == 0)
            for g, w in enumerate(POOL_WINDOWS):
                lanes = slice(g * PGD, (g + 1) * PGD)
                acc = _tap(cu_ext, sh, 16, base, lanes)
                for j in range(1, w):
                    acc = acc + _tap(cu_ext, sh, 16 - j, base, lanes)
                p = acc * ic[:, lanes] - _tap(cu_ext, sh, 16, base, lanes)
                p_scr[pl.ds(base, RC), lanes] = p.astype(BF16)
        _chunks(tt, c2)
        for g in range(len(POOL_WINDOWS)):
            lanes = slice(g * PGD, (g + 1) * PGD)
            pl_scr[:, lanes] = jnp.dot(p_scr[:, lanes], pw[g], preferred_element_type=F32)

        def c3(base):
            rows = pl.ds(base, RC)
            z = _hcol(h_ref, 8, base)
            sz = _sig(z)
            szc = z * sz
            plb = pl_scr[rows, :] + pb[...]
            yc0 = plb * ps[...]
            y_scr[rows, 2 * GW:3 * GW] = (yc0 * szc).astype(BF16)
            dyc = dy_scr[rows, 2 * GW:3 * GW]
            dh_ref[rows, 8 * GW:9 * GW] = (dyc * yc0 * _dsilu(z, sz)).astype(BF16)
            dyc0 = dyc * szc
            racc[R_DPS] += _fold8(dyc0 * plb)
            dpl = dyc0 * ps[...]
            racc[R_DPB] += _fold8(dpl)
            dpl_scr[rows, :] = dpl.astype(BF16)
        _chunks(tt, c3)
        for g in range(len(POOL_WINDOWS)):
            lanes = slice(g * PGD, (g + 1) * PGD)
            dpw_acc[g] += lax.dot_general(p_scr[:, lanes], dpl_scr[:, lanes], (((0,), (0,)), ((), ())),
                                          preferred_element_type=F32)
            dp_scr[:, lanes] = lax.dot_general(dpl_scr[:, lanes], pw[g], (((1,), (1,)), ((), ())),
                                               preferred_element_type=F32)

        def c4(base):
            rows = pl.ds(base, RC)
            dpn_ext[rows, :] = dp_scr[rows, :] * _inv_count(cnt, base + t0 == 0)
        _chunks(tt, c4)
        _build_shifts(dpn_ext, sh, range(1, 8), tt + SUBLANES)

        def c5(base):
            rows = pl.ds(base, RC)
            for g, w in enumerate(POOL_WINDOWS):
                lanes = slice(g * PGD, (g + 1) * PGD)
                acc = _tap(dpn_ext, sh, 0, base, lanes)
                for j in range(1, w):
                    acc = acc + _tap(dpn_ext, sh, j, base, lanes)
                dh_ref[rows, 7 * GW + g * PGD:7 * GW + (g + 1) * PGD] = (acc - dp_scr[rows, lanes]).astype(BF16)
        _chunks(tt, c5)
        dpn_ext[tt:tt + 16, :] = dpn_ext[0:16, :]

        for r in range(D_MIX // GW):
            dwout_acc[r * GW:(r + 1) * GW, :] += lax.dot_general(
                y_scr[:, r * GW:(r + 1) * GW], dzb[...], (((0,), (0,)), ((), ())), preferred_element_type=F32)
        dx_scr[...] = lax.dot_general(dh_ref[...], win_v[...], (((1,), (1,)), ((), ())), preferred_element_type=F32)

        def post(base):
            rows = pl.ds(base, RC)
            dx = ALPHA * dz_ref[rows, :] + dx_scr[rows, :]
            if has_prev:
                xhat, rstd = _ln_stats(zp_ref[rows, :])
                acc2[0] += _fold8(dx * xhat)
                acc2[1] += _fold8(dx)
                dxo_ref[rows, :] = _ln_bwd(dx, xhat, rstd, lngp[...])
            else:
                dxo_ref[rows, :] = dx
        _chunks(tt, post, unroll=2)

        @pl.when(i == n_t - 1)
        def _():
            small_ref[...] = jnp.sum(racc[...], axis=1)
            dpw_ref[...] = dpw_acc[...]
            pltpu.sync_copy(dwout_acc, dwout_hbm)
            if has_prev:
                dlnp_ref[...] = jnp.sum(acc2[...], axis=1)

    rtile = lambda c: pl.BlockSpec((tt, c), lambda i: (n_t - 1 - i, 0))
    full = lambda a: pl.BlockSpec(a.shape, lambda i: (0,) * a.ndim)
    const = lambda shp: pl.BlockSpec(shp, lambda i: (0,) * len(shp))
    hbm = pl.BlockSpec(memory_space=pl.ANY)
    halo_spec = pl.BlockSpec((HALO, D_IN), lambda i: (jnp.maximum((n_t - 1 - i) * hb - 1, 0), 0))
    ins = [dz, h, h, win_b, wout_b, *prm] + ([z_prev, lng_prev] if has_prev else [])
    in_specs = [rtile(D_MODEL), rtile(D_IN), halo_spec, hbm, hbm] + [full(a) for a in prm] \
        + ([rtile(D_MODEL), full(lng_prev)] if has_prev else [])
    out_shape = [jax.ShapeDtypeStruct((t_len, D_MODEL), F32), jax.ShapeDtypeStruct((t_len, D_IN), BF16),
                 jax.ShapeDtypeStruct((D_MIX, D_MODEL), F32), jax.ShapeDtypeStruct((N_RACC, GW), F32),
                 jax.ShapeDtypeStruct((len(POOL_WINDOWS), PGD, PGD), F32)]
    out_specs = [rtile(D_MODEL), rtile(D_IN), hbm, const((N_RACC, GW)), const((len(POOL_WINDOWS), PGD, PGD))]
    if has_prev:
        out_shape.append(jax.ShapeDtypeStruct((2, D_MODEL), F32))
        out_specs.append(const((2, D_MODEL)))
    scratch = [
        pltpu.VMEM((D_MODEL, D_IN), BF16), pltpu.VMEM((D_MIX, D_MODEL), BF16),
        pltpu.VMEM((tt, D_MODEL), BF16), pltpu.VMEM((tt, D_MIX), F32), pltpu.VMEM((tt, D_MIX), BF16),
        pltpu.VMEM((tt, D_MODEL), F32),
        pltpu.VMEM((tt + SUBLANES, GW), F32), pltpu.VMEM((tt + HALO, GW), F32), pltpu.VMEM((tt + 16, GW), F32),
        pltpu.VMEM((tt + SUBLANES, GW), F32), pltpu.VMEM((tt + HALO, GW), F32), pltpu.VMEM((tt + 16, GW), F32),
        pltpu.VMEM((SUBLANES, tt + HALO, GW), F32),
        pltpu.VMEM((tt, GW), BF16), pltpu.VMEM((tt, GW), F32), pltpu.VMEM((tt, GW), BF16), pltpu.VMEM((tt, GW), F32),
        pltpu.VMEM((N_RACC, SUBLANES, GW), F32), pltpu.VMEM((len(POOL_WINDOWS), PGD, PGD), F32),
        pltpu.VMEM((D_MIX, D_MODEL), F32),
    ]
    if has_prev:
        scratch.append(pltpu.VMEM((2, SUBLANES, D_MODEL), F32))
    return pl.pallas_call(
        body, name="bwd_layer_prev" if has_prev else "bwd_layer", grid=(n_t,),
        in_specs=in_specs, out_specs=out_specs, out_shape=out_shape, scratch_shapes=scratch,
        compiler_params=pltpu.CompilerParams(dimension_semantics=("arbitrary",), vmem_limit_bytes=VMEM_LIMIT),
    )(*ins)


def _wgrad_in(xb, dh, *, tk):
    t_len = xb.shape[0]
    n_k = t_len // tk

    def body(x_ref, dh_ref, o_ref):
        @pl.when(pl.program_id(1) == 0)
        def _():
            o_ref[...] = jnp.zeros_like(o_ref)
        o_ref[0] += lax.dot_general(x_ref[...], dh_ref[...], (((0,), (0,)), ((), ())), preferred_element_type=F32)

    return pl.pallas_call(
        body, name="wgrad_in", grid=(N_CHIP, n_k),
        in_specs=[pl.BlockSpec((tk, D_MODEL), lambda j, k: (k, 0)), pl.BlockSpec((tk, SHARD_IN), lambda j, k: (k, j))],
        out_specs=pl.BlockSpec((1, D_MODEL, SHARD_IN), lambda j, k: (j, 0, 0)),
        out_shape=jax.ShapeDtypeStruct((N_CHIP, D_MODEL, SHARD_IN), F32),
        compiler_params=pltpu.CompilerParams(dimension_semantics=("arbitrary", "arbitrary"), vmem_limit_bytes=VMEM_LIMIT),
    )(xb, dh)


MESH = pl.DeviceIdType.MESH
ANY = pl.BlockSpec(memory_space=pl.ANY)


def _place():
    x, y, c = lax.axis_index("x"), lax.axis_index("y"), lax.axis_index("c")
    others = [(1 - x, y), (x, 1 - y), (1 - x, 1 - y)]
    return x, y, c, 2 * x + y, [(ox, oy, 2 * ox + oy) for ox, oy in others]


def _rcopy(src, dst, send_sems, recv_sems, k, dev):
    return pltpu.make_async_remote_copy(src_ref=src, dst_ref=dst, send_sem=send_sems.at[k], recv_sem=recv_sems.at[k],
                                        device_id=dev, device_id_type=MESH)


def _gather_weights(w_in, w_out, cw):
    hi, ho = D_MODEL // 2, SHARD_OUT // 2

    def body(win_ref, wout_ref, cw_ref, owin, owout, ocw, bin_v, bout_v, send_sems, recv_sems, lsem):
        x, y, c, me, others = _place()
        for l in range(DEPTH):
            for r0 in range(0, D_MODEL, 256):
                bin_v[l, r0:r0 + 256, :] = win_ref[l, r0:r0 + 256, :].astype(BF16)
            bout_v[l] = wout_ref[l].astype(BF16)
        cin = pl.ds(pl.multiple_of(me * SHARD_IN, 128), SHARD_IN)
        rout = pl.ds(pl.multiple_of(me * SHARD_OUT, 128), SHARD_OUT)
        local = [pltpu.make_async_copy(bin_v, owin.at[:, :, cin], lsem.at[0]),
                 pltpu.make_async_copy(bout_v, owout.at[:, rout, :], lsem.at[1]),
                 pltpu.make_async_copy(cw_ref, ocw.at[me], lsem.at[2])]
        for cp in local:
            cp.start()

        def in_half(chip, core):
            return owin.at[:, pl.ds(pl.multiple_of(core * hi, 256), hi), pl.ds(pl.multiple_of(chip * SHARD_IN, 128), SHARD_IN)]

        def out_half(chip, core):
            return owout.at[:, pl.ds(pl.multiple_of(chip * SHARD_OUT + core * ho, 64), ho), :]

        first = []
        for k, (ox, oy, _) in enumerate(others):
            dev = (ox, oy, c)
            first.append(_rcopy(bin_v.at[:, pl.ds(pl.multiple_of(c * hi, 256), hi), :], in_half(me, c), send_sems, recv_sems, k, dev))
            first.append(_rcopy(bout_v.at[:, pl.ds(pl.multiple_of(c * ho, 64), ho), :], out_half(me, c), send_sems, recv_sems, 3 + k, dev))
            first.append(_rcopy(cw_ref, ocw.at[me], send_sems, recv_sems, 6 + k, dev))
        for cp in first:
            cp.start()
        sib = (x, y, 1 - c)
        passed = []
        for k, (ox, oy, oc) in enumerate(others):
            _rcopy(in_half(oc, c), in_half(oc, c), send_sems, recv_sems, k, sib).wait_recv()
            fwd_in = _rcopy(in_half(oc, c), in_half(oc, c), send_sems, recv_sems, 9 + k, sib)
            fwd_in.start()
            _rcopy(out_half(oc, c), out_half(oc, c), send_sems, recv_sems, 3 + k, sib).wait_recv()
            fwd_out = _rcopy(out_half(oc, c), out_half(oc, c), send_sems, recv_sems, 12 + k, sib)
            fwd_out.start()
            passed += [fwd_in, fwd_out]
        for k, (ox, oy, oc) in enumerate(others):
            _rcopy(cw_ref, ocw.at[oc], send_sems, recv_sems, 6 + k, sib).wait_recv()
            _rcopy(in_half(oc, 1 - c), in_half(oc, 1 - c), send_sems, recv_sems, 9 + k, sib).wait_recv()
            _rcopy(out_half(oc, 1 - c), out_half(oc, 1 - c), send_sems, recv_sems, 12 + k, sib).wait_recv()
        for cp in first + passed:
            cp.wait_send()
        for cp in local:
            cp.wait()

    vm = pl.BlockSpec(memory_space=pltpu.VMEM)
    return pl.pallas_call(
        body, name="gather_weights",
        in_specs=[vm, vm, vm], out_specs=[ANY, ANY, ANY],
        out_shape=[jax.ShapeDtypeStruct((DEPTH, D_MODEL, D_IN), BF16), jax.ShapeDtypeStruct((DEPTH, D_MIX, D_MODEL), BF16),
                   jax.ShapeDtypeStruct((N_CHIP,) + cw.shape, F32)],
        scratch_shapes=[pltpu.VMEM((DEPTH, D_MODEL, SHARD_IN), BF16), pltpu.VMEM((DEPTH, SHARD_OUT, D_MODEL), BF16),
                        pltpu.SemaphoreType.DMA((15,)), pltpu.SemaphoreType.DMA((15,)), pltpu.SemaphoreType.DMA((3,))],
        compiler_params=pltpu.CompilerParams(vmem_limit_bytes=VMEM_LIMIT),
    )(w_in, w_out, cw)


def _exchange_halves(arrs):
    n = len(arrs)

    def body(*refs):
        ins, outs, (send_sems, recv_sems) = refs[:n], refs[n:2 * n], refs[2 * n:]
        x, y, c, _, _ = _place()
        cps = []
        for m in range(n):
            half = ins[m].shape[1] // 2
            cps.append(_rcopy(ins[m].at[:, pl.ds((1 - c) * half, half), :], outs[m], send_sems, recv_sems, m, (x, y, 1 - c)))
        for cp in cps:
            cp.start()
        for cp in cps:
            cp.wait()

    return pl.pallas_call(
        body, name="exchange_halves", in_specs=[ANY] * n, out_specs=[ANY] * n,
        out_shape=[jax.ShapeDtypeStruct((a.shape[0], a.shape[1] // 2, a.shape[2]), F32) for a in arrs],
        scratch_shapes=[pltpu.SemaphoreType.DMA((n,)), pltpu.SemaphoreType.DMA((n,))],
    )(*arrs)


def _add_own_half(a, got, core, *, rb):
    nj, r, cdim = a.shape
    half = r // 2

    def body(core_ref, a_ref, g_ref, o_ref):
        o_ref[...] = a_ref[0] + g_ref[...]

    return pl.pallas_call(
        body, name="add_own_half",
        grid_spec=pltpu.PrefetchScalarGridSpec(
            num_scalar_prefetch=1, grid=(nj, half // rb),
            in_specs=[pl.BlockSpec((1, 1, rb, cdim), lambda j, i, cr: (j, cr[0], i, 0)),
                      pl.BlockSpec((1, rb, cdim), lambda j, i, cr: (j, i, 0))],
            out_specs=pl.BlockSpec((1, rb, cdim), lambda j, i, cr: (j, i, 0))),
        out_shape=jax.ShapeDtypeStruct((nj, half, cdim), F32),
    )(core, a.reshape(nj, 2, half, cdim), got)


def _send_to_owners(arrs):
    n = len(arrs)

    def body(*refs):
        ins, outs, (send_sems, recv_sems, lsem) = refs[:n], refs[n:2 * n], refs[2 * n:]
        x, y, c, me, others = _place()
        local = [pltpu.make_async_copy(ins[m].at[me], outs[m].at[me], lsem.at[m]) for m in range(n)]
        for cp in local:
            cp.start()
        cps = []
        for m in range(n):
            for k, (ox, oy, oc) in enumerate(others):
                cps.append(_rcopy(ins[m].at[oc], outs[m].at[me], send_sems, recv_sems, 3 * m + k, (ox, oy, c)))
        for cp in cps:
            cp.start()
        for cp in cps:
            cp.wait()
        for cp in local:
            cp.wait()

    return pl.pallas_call(
        body, name="send_to_owners", in_specs=[ANY] * n, out_specs=[ANY] * n,
        out_shape=[jax.ShapeDtypeStruct(a.shape, F32) for a in arrs],
        scratch_shapes=[pltpu.SemaphoreType.DMA((3 * n,)), pltpu.SemaphoreType.DMA((3 * n,)), pltpu.SemaphoreType.DMA((n,))],
    )(*arrs)


def _sum_chips(a, *, rb):
    nj, r, cdim = a.shape

    def body(a_ref, o_ref):
        o_ref[...] = ((a_ref[0] + a_ref[1]) + a_ref[2]) + a_ref[3]

    return pl.pallas_call(
        body, name="sum_chips", grid=(r // rb,),
        in_specs=[pl.BlockSpec((nj, rb, cdim), lambda i: (0, i, 0))],
        out_specs=pl.BlockSpec((rb, cdim), lambda i: (i, 0)),
        out_shape=jax.ShapeDtypeStruct((r, cdim), F32),
    )(a)


def _spread_reduced(red_in, red_out, red_small):
    hs = red_small.shape[0]

    def body(in0, in1, out0, out1, sm, fin, fout, fsm, gsm, send_sems, recv_sems, lsem):
        x, y, c, me, others = _place()
        sib = (x, y, 1 - c)
        hi, ho = D_MODEL // 2, SHARD_OUT // 2
        jobs = [(in0, fin.at[0, pl.ds(c * hi, hi), :]), (in1, fin.at[1, pl.ds(c * hi, hi), :]),
                (out0, fout.at[0, pl.ds(c * ho, ho), :]), (out1, fout.at[1, pl.ds(c * ho, ho), :])]
        local = [pltpu.make_async_copy(src, dst, lsem.at[m]) for m, (src, dst) in enumerate(jobs)]
        remote = [_rcopy(src, dst, send_sems, recv_sems, m, sib) for m, (src, dst) in enumerate(jobs)]
        own_small = pltpu.make_async_copy(sm, gsm.at[me], lsem.at[4])
        small = [_rcopy(sm, gsm.at[me], send_sems, recv_sems, 4 + k, (ox, oy, c)) for k, (ox, oy, _) in enumerate(others)]
        for cp in local + remote + [own_small] + small:
            cp.start()
        own_small.wait()
        for cp in small:
            cp.wait()
        mine = fsm.at[:, pl.ds(c * hs, hs), :]
        keep = pltpu.make_async_copy(gsm, mine, lsem.at[5])
        give = _rcopy(gsm, mine, send_sems, recv_sems, 7, sib)
        keep.start()
        give.start()
        for cp in remote + [give]:
            cp.wait()
        for cp in local + [keep]:
            cp.wait()

    return pl.pallas_call(
        body, name="spread_reduced", in_specs=[ANY] * 5, out_specs=[ANY] * 4,
        out_shape=[jax.ShapeDtypeStruct((DEPTH, D_MODEL, SHARD_IN), F32), jax.ShapeDtypeStruct((DEPTH, SHARD_OUT, D_MODEL), F32),
                   jax.ShapeDtypeStruct((N_CHIP, 2 * hs, GW), F32), jax.ShapeDtypeStruct((N_CHIP, hs, GW), F32)],
        scratch_shapes=[pltpu.SemaphoreType.DMA((8,)), pltpu.SemaphoreType.DMA((8,)), pltpu.SemaphoreType.DMA((6,))],
    )(red_in[0], red_in[1], red_out[0], red_out[1], red_small)[:3]


def _adamw_math(w, g, m, v):
    m = ADAM_B1 * m + (1.0 - ADAM_B1) * g
    v = ADAM_B2 * v + (1.0 - ADAM_B2) * (g * g)
    m_hat = m / (1.0 - ADAM_B1 ** ADAM_STEP)
    v_hat = v / (1.0 - ADAM_B2 ** ADAM_STEP)
    delta = -ADAM_LR * (m_hat / (jnp.sqrt(v_hat) + ADAM_EPS) + ADAM_WD * w)
    return delta, m, v


def _adamw_big(w, g, m, v, *, rb):
    r, cdim = w.shape

    def body(w_ref, g_ref, m_ref, v_ref, d_ref, nm_ref, nv_ref):
        d_ref[...], nm_ref[...], nv_ref[...] = _adamw_math(w_ref[...], g_ref[...], m_ref[...], v_ref[...])

    spec = pl.BlockSpec((rb, cdim), lambda i: (i, 0))
    return pl.pallas_call(
        body, name="adamw_big", grid=(r // rb,), in_specs=[spec] * 4, out_specs=[spec] * 3,
        out_shape=[jax.ShapeDtypeStruct((r, cdim), F32)] * 3,
    )(w, g, m, v)


def _adamw_small(ws, gs, ms, vs):
    n = len(ws)

    def body(*refs):
        w, g, m, v = refs[:n], refs[n:2 * n], refs[2 * n:3 * n], refs[3 * n:4 * n]
        d, nm, nv = refs[4 * n:5 * n], refs[5 * n:6 * n], refs[6 * n:7 * n]
        for k in range(n):
            d[k][...], nm[k][...], nv[k][...] = _adamw_math(w[k][...], g[k][...], m[k][...], v[k][...])

    shapes = [jax.ShapeDtypeStruct(a.shape, F32) for a in ws]
    outs = pl.pallas_call(body, name="adamw_small", out_shape=shapes * 3)(*ws, *gs, *ms, *vs)
    return outs[:n], outs[n:2 * n], outs[2 * n:]


TT = 256
TK = 512
CW_ROWS = 40
SMALL_ROWS = 384


def _pack_small(small, dpw, dln):
    rows = [small[0], small[1], dpw[0].reshape(PGD, GW), dpw[1].reshape(PGD, GW), dln[0].reshape(4, GW), dln[1].reshape(4, GW)]
    packed = jnp.concatenate(rows, axis=0)
    packed = jnp.pad(packed, ((0, SMALL_ROWS - packed.shape[0]), (0, 0)))
    return packed.reshape(N_CHIP, SMALL_ROWS // N_CHIP, GW)


def _unpack_small(packed):
    p = packed.reshape(SMALL_ROWS, GW)
    small = [p[0:N_RACC], p[N_RACC:2 * N_RACC]]
    o = 2 * N_RACC
    dpw = [p[o:o + PGD].reshape(len(POOL_WINDOWS), PGD, PGD), p[o + PGD:o + 2 * PGD].reshape(len(POOL_WINDOWS), PGD, PGD)]
    o += 2 * PGD
    dln = [p[o:o + 4].reshape(2, D_MODEL), p[o + 4:o + 8].reshape(2, D_MODEL)]
    return small, dpw, dln


def kernel(x, w_in, conv_a_w, conv_a_b, conv_b_w, conv_b_b, ln_b_g, ln_b_b, pool_w, pool_b, pool_scale, w_out, ln_g, ln_b, loss_target, m_w_in, m_conv_a_w, m_conv_a_b, m_conv_b_w, m_conv_b_b, m_ln_b_g, m_ln_b_b, m_pool_w, m_pool_b, m_pool_scale, m_w_out, m_ln_g, m_ln_b, v_w_in, v_conv_a_w, v_conv_a_b, v_conv_b_w, v_conv_b_b, v_ln_b_g, v_ln_b_b, v_pool_w, v_pool_b, v_pool_scale, v_w_out, v_ln_g, v_ln_b):
    chip = 2 * lax.axis_index("x") + lax.axis_index("y")
    core = lax.axis_index("c")
    x2, tgt = x[0], loss_target[0]

    cw = jnp.zeros((DEPTH, CW_ROWS, PGD), F32).at[:, 0:KA].set(conv_a_w).at[:, 8:8 + KB].set(conv_b_w)
    win_b, wout_b, cw_all = _gather_weights(w_in, w_out, cw)
    cw_full = jnp.transpose(cw_all, (1, 2, 0, 3)).reshape(DEPTH, CW_ROWS, GW)
    row = lambda a, l: a[l].reshape(1, -1)
    cnt = _count_table()
    prm = [(cw_full[l, 0:KA], row(conv_a_b, l), cw_full[l, 8:8 + KB], row(conv_b_b, l), row(ln_b_g, l), row(ln_b_b, l),
            pool_w[l].astype(BF16), row(pool_b, l), row(pool_scale, l), cnt) for l in range(DEPTH)]

    h0, xb0, z0, x1 = _fwd_layer(x2, win_b[0], wout_b[0], prm[0], row(ln_g, 0), row(ln_b, 0), None, tt=TT, last=False)
    h1, xb1, dz1, dln1, loss8 = _fwd_layer(x1, win_b[1], wout_b[1], prm[1], row(ln_g, 1), row(ln_b, 1), tgt, tt=TT, last=True)

    dz0, dh1, dwout1, small1, dpw1, dln0 = _bwd_layer(dz1, h1, win_b[1], wout_b[1], prm[1], z0, row(ln_g, 0), tt=TT)
    dwin1 = _wgrad_in(xb1, dh1, tk=TK)
    gx, dh0, dwout0, small0, dpw0 = _bwd_layer(dz0, h0, win_b[0], wout_b[0], prm[0], None, None, tt=TT)
    dwin0 = _wgrad_in(xb0, dh0, tk=TK)

    parts = [dwin0, dwin1, dwout0.reshape(N_CHIP, SHARD_OUT, D_MODEL), dwout1.reshape(N_CHIP, SHARD_OUT, D_MODEL),
             _pack_small([small0, small1], [dpw0, dpw1], [dln0, dln1])]
    rbs = [256, 256, SHARD_OUT // 2, SHARD_OUT // 2, SMALL_ROWS // N_CHIP // 2]
    got = _exchange_halves(parts)
    core1 = core.reshape(1).astype(jnp.int32)
    chip_sums = [_add_own_half(a, g, core1, rb=rb) for a, g, rb in zip(parts, got, rbs)]
    landed = _send_to_owners(chip_sums)
    red = [_sum_chips(a, rb=rb) for a, rb in zip(landed, rbs)]
    g_in, g_out, g_small = _spread_reduced(red[0:2], red[2:4], red[4])

    flat = lambda a: a.reshape(-1, a.shape[-1])
    unflat = lambda a, like: a.reshape(like.shape)
    d_in, nm_in, nv_in = [unflat(a, w_in) for a in _adamw_big(flat(w_in), flat(g_in), flat(m_w_in), flat(v_w_in), rb=256)]
    d_out, nm_out, nv_out = [unflat(a, w_out) for a in _adamw_big(flat(w_out), flat(g_out), flat(m_w_out), flat(v_w_out), rb=SHARD_OUT)]

    small, dpw, dln = _unpack_small(g_small)
    mine = lambda a: lax.dynamic_slice_in_dim(a, chip * PGD, PGD, axis=-1)
    stack = lambda f: jnp.stack([f(0), f(1)])
    g_caw = stack(lambda l: mine(small[l][R_DWA:R_DWA + KA]))
    g_cab = stack(lambda l: small[l][R_DCAB])
    g_cbw = stack(lambda l: mine(small[l][R_DWB:R_DWB + KB]))
    g_cbb = stack(lambda l: small[l][R_DCBB])
    g_lbg = stack(lambda l: small[l][R_DLBG])
    g_lbb = stack(lambda l: small[l][R_DLBB])
    g_pw = stack(lambda l: dpw[l])
    g_pb = stack(lambda l: small[l][R_DPB].reshape(len(POOL_WINDOWS), PGD))
    g_ps = stack(lambda l: small[l][R_DPS])
    g_lng = jnp.stack([dln[0][0], dln[1][0]])
    g_lnb = jnp.stack([dln[0][1], dln[1][1]])
    ws = [conv_a_w, conv_a_b, conv_b_w, conv_b_b, ln_b_g, ln_b_b, pool_w, pool_b, pool_scale, ln_g, ln_b]
    gs = [g_caw, g_cab, g_cbw, g_cbb, g_lbg, g_lbb, g_pw, g_pb, g_ps, g_lng, g_lnb]
    ms = [m_conv_a_w, m_conv_a_b, m_conv_b_w, m_conv_b_b, m_ln_b_g, m_ln_b_b, m_pool_w, m_pool_b, m_pool_scale, m_ln_g, m_ln_b]
    vs = [v_conv_a_w, v_conv_a_b, v_conv_b_w, v_conv_b_b, v_ln_b_g, v_ln_b_b, v_pool_w, v_pool_b, v_pool_scale, v_ln_g, v_ln_b]
    ds, nms, nvs = _adamw_small([flat(a) for a in ws], [flat(a) for a in gs], [flat(a) for a in ms], [flat(a) for a in vs])
    ds, nms, nvs = ([unflat(a, w) for a, w in zip(t, ws)] for t in (ds, nms, nvs))

    loss = lax.psum(loss8[0, 0], ("x", "y", "c"))

    def order(in_, small_, out_):
        return [in_, *small_[:9], out_, *small_[9:]]
    return (loss, gx[None], *order(g_in, gs, g_out), *order(d_in, ds, d_out), *order(nm_in, nms, nm_out), *order(nv_in, nvs, nv_out))
```

```python
import functools

import jax
import jax.numpy as jnp
import numpy as np
from jax import lax
from jax.experimental import pallas as pl
from jax.experimental.pallas import tpu as pltpu

F32 = jnp.float32
BF16 = jnp.bfloat16

D_MODEL = 1024
DEPTH = 2
GW = 512
D_IN = 9 * GW
D_MIX = 3 * GW
POOL_WINDOWS = (2, 4, 8, 16)
PGD = 128
KA = 3
KB = 31
ALPHA = (2.0 * DEPTH) ** 0.25
LN_EPS = 1e-5
ADAM_LR, ADAM_B1, ADAM_B2, ADAM_EPS, ADAM_WD, ADAM_STEP = 0.001, 0.9, 0.999, 1e-08, 0.01, 10

N_CHIP = 4
SHARD_IN = D_IN // N_CHIP
SHARD_OUT = D_MIX // N_CHIP

SUBLANES = 8
RC = 32
HALO = 32
VMEM_LIMIT = 60 * 1024 * 1024

R_DWA, R_DCAB, R_DWB, R_DCBB, R_DLBG, R_DLBB, R_DPB, R_DPS, N_RACC = 0, 3, 4, 35, 36, 37, 38, 39, 40


def _sig(v):
    return 0.5 * jnp.tanh(0.5 * v) + 0.5


def _chunks(n_rows, fn, unroll=1):
    def step(c, carry):
        fn(pl.multiple_of(c * RC, RC))
        return carry
    lax.fori_loop(0, n_rows // RC, step, 0, unroll=unroll)


def _fold8(v):
    return v.reshape(RC // SUBLANES, SUBLANES, v.shape[-1]).sum(axis=0)


def _build_shifts(ext_ref, sh_ref, shifts, n_rows):
    for r in shifts:
        for c0 in range(0, n_rows, RC):
            n = min(RC, n_rows - c0)
            sh_ref[r, pl.ds(c0, n), :] = ext_ref[pl.ds(c0 + r, n), :]


def _tap(ext_ref, sh_ref, off, base, lanes=None):
    a, r = divmod(off, SUBLANES)
    src = ext_ref if r == 0 else sh_ref.at[r]
    if lanes is None:
        return src[pl.ds(base + SUBLANES * a, RC), :]
    return src[pl.ds(base + SUBLANES * a, RC), lanes]


def _ln_stats(v):
    mu = jnp.mean(v, axis=-1, keepdims=True)
    vc = v - mu
    var = jnp.mean(vc * vc, axis=-1, keepdims=True)
    rstd = lax.rsqrt(var + LN_EPS)
    return vc * rstd, rstd


def _ln_bwd(dy, xhat, rstd, g):
    dxh = dy * g
    m1 = jnp.mean(dxh, axis=-1, keepdims=True)
    m2 = jnp.mean(dxh * xhat, axis=-1, keepdims=True)
    return rstd * (dxh - m1 - xhat * m2)


def _for_taps(ext_ref, sh_ref, base, offsets, fn):
    for r in range(SUBLANES):
        offs = [o for o in offsets if o % SUBLANES == r]
        if not offs:
            continue
        a0, a1 = min(offs) // SUBLANES, max(offs) // SUBLANES
        src = ext_ref if r == 0 else sh_ref.at[r]
        win = src[pl.ds(base + SUBLANES * a0, RC + SUBLANES * (a1 - a0)), :]
        for o in offs:
            a = o // SUBLANES - a0
            fn(o, win[SUBLANES * a:SUBLANES * a + RC])


def _count_table():
    t = np.arange(1, RC + 1, dtype=np.float64)[:, None]
    w = np.repeat(np.asarray(POOL_WINDOWS, np.float64), PGD)[None, :]
    return jnp.asarray(1.0 / np.minimum(t, w), F32)


def _inv_count(cnt_ref, first):
    return jnp.where(first, cnt_ref[...], cnt_ref[RC - 1:RC, :])


def _hcol(h_ref, j, base):
    return h_ref[pl.ds(base, RC), j * GW:(j + 1) * GW].astype(F32)


def _with_comm(comm, ins, in_specs, out_shape, out_specs, scratch):
    if comm is None:
        return ins, in_specs, out_shape, out_specs, scratch
    hbm = pl.BlockSpec(memory_space=pl.ANY)
    return (ins + list(comm["ins"]), in_specs + [hbm] * len(comm["ins"]), out_shape + list(comm["out_shape"]),
            out_specs + [hbm] * len(comm["out_shape"]), scratch + list(comm["sems"]))


def _split_comm(refs, comm, n_in, n_out):
    refs = list(refs)
    if comm is None:
        return refs, None
    ci, co, cs = len(comm["ins"]), len(comm["out_shape"]), len(comm["sems"])
    own = refs[:n_in] + refs[n_in + ci:n_in + ci + n_out] + refs[n_in + ci + n_out + co:len(refs) - cs]
    return own, (refs[n_in:n_in + ci], refs[n_in + ci + n_out:n_in + ci + n_out + co], refs[len(refs) - cs:])


def _fwd_mixers(h_ref, y_scr, q_ext, ub_ext, cu_ext, sh, p_scr, pl_scr, prm, tt, t0):
    caw, cab, cbw, cbb, lbg, lbb, pw, pb, ps, cnt = prm

    def a1(base):
        q_ext[pl.ds(SUBLANES + base, RC), :] = _hcol(h_ref, 1, base) * _hcol(h_ref, 2, base)
    _chunks(tt, a1)
    _build_shifts(q_ext, sh, (6, 7), tt)

    def a2(base):
        ca = cab[...] + caw[0:1, :] * _tap(q_ext, sh, 6, base) + caw[1:2, :] * _tap(q_ext, sh, 7, base) \
            + caw[2:3, :] * _tap(q_ext, sh, 8, base)
        z = _hcol(h_ref, 3, base)
        y_scr[pl.ds(base, RC), 0:GW] = (_hcol(h_ref, 0, base) * ca * (z * _sig(z))).astype(BF16)
    _chunks(tt, a2, unroll=2)
    q_ext[0:SUBLANES, :] = q_ext[tt:tt + SUBLANES, :]

    def b1(base):
        ub_ext[pl.ds(HALO + base, RC), :] = _hcol(h_ref, 4, base) * _sig(_hcol(h_ref, 5, base))
    _chunks(tt, b1)
    _build_shifts(ub_ext, sh, range(1, 8), tt + HALO - SUBLANES)

    def b2(base):
        cb = [cbb[...] + jnp.zeros((RC, GW), F32)]

        def tap(off, v):
            cb[0] = cb[0] + cbw[off - 2:off - 1, :] * v
        _for_taps(ub_ext, sh, base, range(2, 2 + KB), tap)
        xhat, _ = _ln_stats(cb[0])
        lnv = xhat * lbg[...] + lbb[...]
        z = _hcol(h_ref, 6, base)
        y_scr[pl.ds(base, RC), GW:2 * GW] = (lnv * _sig(lnv) * (z * _sig(z))).astype(BF16)
    _chunks(tt, b2, unroll=2)
    ub_ext[0:HALO, :] = ub_ext[tt:tt + HALO, :]

    def c1(base):
        cu_ext[pl.ds(16 + base, RC), :] = _hcol(h_ref, 7, base)
    _chunks(tt, c1)
    _build_shifts(cu_ext, sh, range(1, 8), tt + SUBLANES)

    def c2(base):
        ic = _inv_count(cnt, base + t0 == 0)
        for g, w in enumerate(POOL_WINDOWS):
            lanes = slice(g * PGD, (g + 1) * PGD)
            acc = _tap(cu_ext, sh, 16, base, lanes)
            for j in range(1, w):
                acc = acc + _tap(cu_ext, sh, 16 - j, base, lanes)
            p = acc * ic[:, lanes] - _tap(cu_ext, sh, 16, base, lanes)
            p_scr[pl.ds(base, RC), lanes] = p.astype(BF16)
    _chunks(tt, c2)
    cu_ext[0:16, :] = cu_ext[tt:tt + 16, :]
    for g in range(len(POOL_WINDOWS)):
        lanes = slice(g * PGD, (g + 1) * PGD)
        pl_scr[:, lanes] = jnp.dot(p_scr[:, lanes], pw[g], preferred_element_type=F32)

    def c3(base):
        z = _hcol(h_ref, 8, base)
        yc0 = (pl_scr[pl.ds(base, RC), :] + pb[...]) * ps[...]
        y_scr[pl.ds(base, RC), 2 * GW:3 * GW] = (yc0 * (z * _sig(z))).astype(BF16)
    _chunks(tt, c3, unroll=2)


def _fwd_layer(x, win_b, wout_b, prm, ln_g, ln_b, target, *, tt, last, comm=None):
    t_len = x.shape[0]
    n_t = t_len // tt

    def body(*refs):
        refs, comm_refs = _split_comm(refs, comm, n_in, n_out)
        if last:
            (x_ref, win_hbm, wout_hbm, caw, cab, cbw, cbb, lbg, lbb, pw, pb, ps, cnt, lng, lnb, tgt_ref,
             h_ref, xb_ref, dz_ref, dln_ref, loss_ref,
             win_v, wout_v, y_scr, o_scr, q_ext, ub_ext, cu_ext, sh, p_scr, pl_scr, acc2, lacc) = refs
        else:
            (x_ref, win_hbm, wout_hbm, caw, cab, cbw, cbb, lbg, lbb, pw, pb, ps, cnt, lng, lnb,
             h_ref, xb_ref, z_ref, xn_ref,
             win_v, wout_v, y_scr, o_scr, q_ext, ub_ext, cu_ext, sh, p_scr, pl_scr) = refs
        i = pl.program_id(0)

        @pl.when(i == 0)
        def _():
            if comm is not None:
                comm["start"](*comm_refs)
            pltpu.sync_copy(win_hbm, win_v)
            pltpu.sync_copy(wout_hbm, wout_v)
            q_ext[0:SUBLANES, :] = jnp.zeros((SUBLANES, GW), F32)
            ub_ext[0:HALO, :] = jnp.zeros((HALO, GW), F32)
            cu_ext[0:16, :] = jnp.zeros((16, GW), F32)
            if last:
                acc2[...] = jnp.zeros_like(acc2)
                lacc[...] = jnp.zeros_like(lacc)

        xb_ref[...] = x_ref[...].astype(BF16)
        for j in range(D_IN // GW):
            h_ref[:, j * GW:(j + 1) * GW] = jnp.dot(
                xb_ref[...], win_v[:, j * GW:(j + 1) * GW], preferred_element_type=F32).astype(BF16)

        _fwd_mixers(h_ref, y_scr, q_ext, ub_ext, cu_ext, sh, p_scr, pl_scr,
                    (caw, cab, cbw, cbb, lbg, lbb, pw, pb, ps, cnt), tt, i * tt)

        o_scr[...] = jnp.dot(y_scr[...], wout_v[...], preferred_element_type=F32)

        def post(base):
            rows = pl.ds(base, RC)
            z = ALPHA * x_ref[rows, :] + o_scr[rows, :]
            xhat, rstd = _ln_stats(z)
            xn = xhat * lng[...] + lnb[...]
            if last:
                err = xn - tgt_ref[rows, :]
                lacc[...] += _fold8(err * err)
                dxn = err * (1.0 / D_MODEL)
                acc2[0] += _fold8(dxn * xhat)
                acc2[1] += _fold8(dxn)
                dz_ref[rows, :] = _ln_bwd(dxn, xhat, rstd, lng[...])
            else:
                z_ref[rows, :] = z
                xn_ref[rows, :] = xn
        _chunks(tt, post, unroll=2)

        if last:
            @pl.when(i == n_t - 1)
            def _():
                dln_ref[...] = jnp.sum(acc2[...], axis=1)
                loss_ref[...] = jnp.zeros((SUBLANES, 128), F32) + (0.5 / D_MODEL) * jnp.sum(lacc[...])
        if comm is not None:
            @pl.when(i == n_t - 1)
            def _():
                comm["wait"](*comm_refs)

    tile = lambda c: pl.BlockSpec((tt, c), lambda i: (i, 0))
    full = lambda a: pl.BlockSpec(a.shape, lambda i: (0,) * a.ndim)
    hbm = pl.BlockSpec(memory_space=pl.ANY)
    ins = [x, win_b, wout_b, *prm, ln_g, ln_b] + ([target] if last else [])
    in_specs = [tile(D_MODEL), hbm, hbm] + [full(a) for a in (*prm, ln_g, ln_b)] + ([tile(D_MODEL)] if last else [])
    out_shape = [jax.ShapeDtypeStruct((t_len, D_IN), BF16), jax.ShapeDtypeStruct((t_len, D_MODEL), BF16)]
    out_specs = [tile(D_IN), tile(D_MODEL)]
    if last:
        out_shape += [jax.ShapeDtypeStruct((t_len, D_MODEL), F32), jax.ShapeDtypeStruct((2, D_MODEL), F32),
                      jax.ShapeDtypeStruct((SUBLANES, 128), F32)]
        out_specs += [tile(D_MODEL), pl.BlockSpec((2, D_MODEL), lambda i: (0, 0)),
                      pl.BlockSpec((SUBLANES, 128), lambda i: (0, 0))]
    else:
        out_shape += [jax.ShapeDtypeStruct((t_len, D_MODEL), F32), jax.ShapeDtypeStruct((t_len, D_MODEL), F32)]
        out_specs += [tile(D_MODEL), tile(D_MODEL)]
    scratch = [
        pltpu.VMEM((D_MODEL, D_IN), BF16), pltpu.VMEM((D_MIX, D_MODEL), BF16),
        pltpu.VMEM((tt, D_MIX), BF16), pltpu.VMEM((tt, D_MODEL), F32),
        pltpu.VMEM((tt + SUBLANES, GW), F32), pltpu.VMEM((tt + HALO, GW), F32), pltpu.VMEM((tt + 16, GW), F32),
        pltpu.VMEM((SUBLANES, tt + HALO, GW), F32),
        pltpu.VMEM((tt, GW), BF16), pltpu.VMEM((tt, GW), F32),
    ]
    if last:
        scratch += [pltpu.VMEM((2, SUBLANES, D_MODEL), F32), pltpu.VMEM((SUBLANES, D_MODEL), F32)]
    n_in, n_out = len(ins), len(out_shape)
    ins, in_specs, out_shape, out_specs, scratch = _with_comm(comm, ins, in_specs, out_shape, out_specs, scratch)
    return pl.pallas_call(
        body, name=("fwd_last" if last else "fwd_layer") + ("" if comm is None else "_comm"), grid=(n_t,),
        in_specs=in_specs, out_specs=out_specs, out_shape=out_shape, scratch_shapes=scratch,
        compiler_params=pltpu.CompilerParams(dimension_semantics=("arbitrary",), vmem_limit_bytes=VMEM_LIMIT),
    )(*ins)


def _dsilu(z, sz):
    return sz * (1.0 + z * (1.0 - sz))


def _bwd_layer(dz, h, win_b, wout_b, prm, z_prev, lng_prev, *, tt, comm=None):
    t_len = dz.shape[0]
    n_t = t_len // tt
    has_prev = z_prev is not None
    hb = tt // HALO

    def body(*refs):
        refs, comm_refs = _split_comm(refs, comm, n_in, n_out)
        dz_ref, h_ref, halo_ref, win_hbm, wout_hbm, caw, cab, cbw, cbb, lbg, lbb, pw, pb, ps, cnt = refs[:15]
        k = 15
        if has_prev:
            zp_ref, lngp = refs[k:k + 2]
            k += 2
        dxo_ref, dh_ref, dwout_hbm, small_ref, dpw_ref = refs[k:k + 5]
        k += 5
        if has_prev:
            dlnp_ref = refs[k]
            k += 1
        (win_v, wout_v, dzb, dy_scr, y_scr, dx_scr, q_ext, ub_ext, cu_ext, dca_ext, dcb_ext, dpn_ext, sh,
         p_scr, pl_scr, dpl_scr, dp_scr, racc, dpw_acc, dwout_acc) = refs[k:k + 20]
        k += 20
        if has_prev:
            acc2 = refs[k]
        i = pl.program_id(0)
        ti = n_t - 1 - i
        t0 = ti * tt

        @pl.when(i == 0)
        def _():
            if comm is not None:
                comm["start"](*comm_refs)
            pltpu.sync_copy(win_hbm, win_v)
            pltpu.sync_copy(wout_hbm, wout_v)
            dca_ext[tt:tt + SUBLANES, :] = jnp.zeros((SUBLANES, GW), F32)
            dcb_ext[tt:tt + HALO, :] = jnp.zeros((HALO, GW), F32)
            dpn_ext[tt:tt + 16, :] = jnp.zeros((16, GW), F32)
            racc[...] = jnp.zeros_like(racc)
            dpw_acc[...] = jnp.zeros_like(dpw_acc)
            dwout_acc[...] = jnp.zeros_like(dwout_acc)
            if has_prev:
                acc2[...] = jnp.zeros_like(acc2)

        dzb[...] = dz_ref[...].astype(BF16)
        dy_scr[...] = lax.dot_general(dzb[...], wout_v[...], (((1,), (1,)), ((), ())), preferred_element_type=F32)

        live = (ti > 0).astype(F32)
        hh = lambda j, r0, r1: halo_ref[r0:r1, j * GW:(j + 1) * GW].astype(F32)
        q_ext[0:SUBLANES, :] = live * hh(1, 24, 32) * hh(2, 24, 32)
        ub_ext[0:HALO, :] = live * hh(4, 0, 32) * _sig(hh(5, 0, 32))
        cu_ext[0:16, :] = live * hh(7, 16, 32)

        def a1(base):
            q_ext[pl.ds(SUBLANES + base, RC), :] = _hcol(h_ref, 1, base) * _hcol(h_ref, 2, base)
        _chunks(tt, a1)
        _build_shifts(q_ext, sh, (6, 7), tt)

        def a2(base):
            rows = pl.ds(base, RC)
            q6, q7, q8 = _tap(q_ext, sh, 6, base), _tap(q_ext, sh, 7, base), _tap(q_ext, sh, 8, base)
            ca = cab[...] + caw[0:1, :] * q6 + caw[1:2, :] * q7 + caw[2:3, :] * q8
            bg, z = _hcol(h_ref, 0, base), _hcol(h_ref, 3, base)
            sz = _sig(z)
            sza = z * sz
            dya = dy_scr[rows, 0:GW]
            ya0 = bg * ca
            y_scr[rows, 0:GW] = (ya0 * sza).astype(BF16)
            dya0 = dya * sza
            dh_ref[rows, 3 * GW:4 * GW] = (dya * ya0 * _dsilu(z, sz)).astype(BF16)
            dh_ref[rows, 0:GW] = (dya0 * ca).astype(BF16)
            dca = dya0 * bg
            dca_ext[rows, :] = dca
            racc[R_DCAB] += _fold8(dca)
            racc[R_DWA + 0] += _fold8(dca * q6)
            racc[R_DWA + 1] += _fold8(dca * q7)
            racc[R_DWA + 2] += _fold8(dca * q8)
        _chunks(tt, a2)
        _build_shifts(dca_ext, sh, (1, 2), tt)

        def a3(base):
            rows = pl.ds(base, RC)
            dq = caw[0:1, :] * _tap(dca_ext, sh, 2, base) + caw[1:2, :] * _tap(dca_ext, sh, 1, base) \
                + caw[2:3, :] * _tap(dca_ext, sh, 0, base)
            dh_ref[rows, GW:2 * GW] = (dq * _hcol(h_ref, 2, base)).astype(BF16)
            dh_ref[rows, 2 * GW:3 * GW] = (dq * _hcol(h_ref, 1, base)).astype(BF16)
        _chunks(tt, a3)
        dca_ext[tt:tt + SUBLANES, :] = dca_ext[0:SUBLANES, :]

        def b1(base):
            ub_ext[pl.ds(HALO + base, RC), :] = _hcol(h_ref, 4, base) * _sig(_hcol(h_ref, 5, base))
        _chunks(tt, b1)
        _build_shifts(ub_ext, sh, range(1, 8), tt + HALO - SUBLANES)

        def b2(base):
            rows = pl.ds(base, RC)
            cbv = [cbb[...] + jnp.zeros((RC, GW), F32)]

            def tap(off, v):
                cbv[0] = cbv[0] + cbw[off - 2:off - 1, :] * v
            _for_taps(ub_ext, sh, base, range(2, 2 + KB), tap)
            xhat, rstd = _ln_stats(cbv[0])
            lnv = xhat * lbg[...] + lbb[...]
            sl = _sig(lnv)
            s = lnv * sl
            z = _hcol(h_ref, 6, base)
            sz = _sig(z)
            szb = z * sz
            y_scr[rows, GW:2 * GW] = (s * szb).astype(BF16)
            dyb = dy_scr[rows, GW:2 * GW]
            dh_ref[rows, 6 * GW:7 * GW] = (dyb * s * _dsilu(z, sz)).astype(BF16)
            dlnv = dyb * szb * _dsilu(lnv, sl)
            racc[R_DLBG] += _fold8(dlnv * xhat)
            racc[R_DLBB] += _fold8(dlnv)
            dcb = _ln_bwd(dlnv, xhat, rstd, lbg[...])
            dcb_ext[rows, :] = dcb
            racc[R_DCBB] += _fold8(dcb)

            def wtap(off, v):
                racc[R_DWB + off - 2] += _fold8(dcb * v)
            _for_taps(ub_ext, sh, base, range(2, 2 + KB), wtap)
        _chunks(tt, b2, unroll=2)
        _build_shifts(dcb_ext, sh, range(1, 8), tt + HALO - SUBLANES)

        def b3(base):
            rows = pl.ds(base, RC)
            dubv = [jnp.zeros((RC, GW), F32)]

            def tap(off, v):
                dubv[0] = dubv[0] + cbw[KB - 1 - off:KB - off, :] * v
            _for_taps(dcb_ext, sh, base, range(KB), tap)
            dub = dubv[0]
            v, gt = _hcol(h_ref, 4, base), _hcol(h_ref, 5, base)
            sg = _sig(gt)
            dh_ref[rows, 4 * GW:5 * GW] = (dub * sg).astype(BF16)
            dh_ref[rows, 5 * GW:6 * GW] = (dub * v * sg * (1.0 - sg)).astype(BF16)
        _chunks(tt, b3)
        dcb_ext[tt:tt + HALO, :] = dcb_ext[0:HALO, :]

        def c1(base):
            cu_ext[pl.ds(16 + base, RC), :] = _hcol(h_ref, 7, base)
        _chunks(tt, c1)
        _build_shifts(cu_ext, sh, range(1, 8), tt + SUBLANES)

        def c2(base):
            ic = _inv_count(cnt, base + t0 == 0)
            for g, w in enumerate(POOL_WINDOWS):
                lanes = slice(g * PGD, (g + 1) * PGD)
                acc = _tap(cu_ext, sh, 16, base, lanes)
                for j in range(1, w):
                    acc = acc + _tap(cu_ext, sh, 16 - j, base, lanes)
                p = acc * ic[:, lanes] - _tap(cu_ext, sh, 16, base, lanes)
                p_scr[pl.ds(base, RC), lanes] = p.astype(BF16)
        _chunks(tt, c2)
        for g in range(len(POOL_WINDOWS)):
            lanes = slice(g * PGD, (g + 1) * PGD)
            pl_scr[:, lanes] = jnp.dot(p_scr[:, lanes], pw[g], preferred_element_type=F32)

        def c3(base):
            rows = pl.ds(base, RC)
            z = _hcol(h_ref, 8, base)
            sz = _sig(z)
            szc = z * sz
            plb = pl_scr[rows, :] + pb[...]
            yc0 = plb * ps[...]
            y_scr[rows, 2 * GW:3 * GW] = (yc0 * szc).astype(BF16)
            dyc = dy_scr[rows, 2 * GW:3 * GW]
            dh_ref[rows, 8 * GW:9 * GW] = (dyc * yc0 * _dsilu(z, sz)).astype(BF16)
            dyc0 = dyc * szc
            racc[R_DPS] += _fold8(dyc0 * plb)
            dpl = dyc0 * ps[...]
            racc[R_DPB] += _fold8(dpl)
            dpl_scr[rows, :] = dpl.astype(BF16)
        _chunks(tt, c3)
        for g in range(len(POOL_WINDOWS)):
            lanes = slice(g * PGD, (g + 1) * PGD)
            dpw_acc[g] += lax.dot_general(p_scr[:, lanes], dpl_scr[:, lanes], (((0,), (0,)), ((), ())),
                                          preferred_element_type=F32)
            dp_scr[:, lanes] = lax.dot_general(dpl_scr[:, lanes], pw[g], (((1,), (1,)), ((), ())),
                                               preferred_element_type=F32)

        def c4(base):
            rows = pl.ds(base, RC)
            dpn_ext[rows, :] = dp_scr[rows, :] * _inv_count(cnt, base + t0 == 0)
        _chunks(tt, c4)
        _build_shifts(dpn_ext, sh, range(1, 8), tt + SUBLANES)

        def c5(base):
            rows = pl.ds(base, RC)
            for g, w in enumerate(POOL_WINDOWS):
                lanes = slice(g * PGD, (g + 1) * PGD)
                acc = _tap(dpn_ext, sh, 0, base, lanes)
                for j in range(1, w):
                    acc = acc + _tap(dpn_ext, sh, j, base, lanes)
                dh_ref[rows, 7 * GW + g * PGD:7 * GW + (g + 1) * PGD] = (acc - dp_scr[rows, lanes]).astype(BF16)
        _chunks(tt, c5)
        dpn_ext[tt:tt + 16, :] = dpn_ext[0:16, :]

        for r in range(D_MIX // GW):
            dwout_acc[r * GW:(r + 1) * GW, :] += lax.dot_general(
                y_scr[:, r * GW:(r + 1) * GW], dzb[...], (((0,), (0,)), ((), ())), preferred_element_type=F32)
        dx_scr[...] = lax.dot_general(dh_ref[...], win_v[...], (((1,), (1,)), ((), ())), preferred_element_type=F32)

        def post(base):
            rows = pl.ds(base, RC)
            dx = ALPHA * dz_ref[rows, :] + dx_scr[rows, :]
            if has_prev:
                xhat, rstd = _ln_stats(zp_ref[rows, :])
                acc2[0] += _fold8(dx * xhat)
                acc2[1] += _fold8(dx)
                dxo_ref[rows, :] = _ln_bwd(dx, xhat, rstd, lngp[...])
            else:
                dxo_ref[rows, :] = dx
        _chunks(tt, post, unroll=2)

        @pl.when(i == n_t - 1)
        def _():
            small_ref[...] = jnp.sum(racc[...], axis=1)
            dpw_ref[...] = dpw_acc[...]
            pltpu.sync_copy(dwout_acc, dwout_hbm)
            if has_prev:
                dlnp_ref[...] = jnp.sum(acc2[...], axis=1)
            if comm is not None:
                comm["wait"](*comm_refs)

    rtile = lambda c: pl.BlockSpec((tt, c), lambda i: (n_t - 1 - i, 0))
    full = lambda a: pl.BlockSpec(a.shape, lambda i: (0,) * a.ndim)
    const = lambda shp: pl.BlockSpec(shp, lambda i: (0,) * len(shp))
    hbm = pl.BlockSpec(memory_space=pl.ANY)
    halo_spec = pl.BlockSpec((HALO, D_IN), lambda i: (jnp.maximum((n_t - 1 - i) * hb - 1, 0), 0))
    ins = [dz, h, h, win_b, wout_b, *prm] + ([z_prev, lng_prev] if has_prev else [])
    in_specs = [rtile(D_MODEL), rtile(D_IN), halo_spec, hbm, hbm] + [full(a) for a in prm] \
        + ([rtile(D_MODEL), full(lng_prev)] if has_prev else [])
    out_shape = [jax.ShapeDtypeStruct((t_len, D_MODEL), F32), jax.ShapeDtypeStruct((t_len, D_IN), BF16),
                 jax.ShapeDtypeStruct((D_MIX, D_MODEL), F32), jax.ShapeDtypeStruct((N_RACC, GW), F32),
                 jax.ShapeDtypeStruct((len(POOL_WINDOWS), PGD, PGD), F32)]
    out_specs = [rtile(D_MODEL), rtile(D_IN), hbm, const((N_RACC, GW)), const((len(POOL_WINDOWS), PGD, PGD))]
    if has_prev:
        out_shape.append(jax.ShapeDtypeStruct((2, D_MODEL), F32))
        out_specs.append(const((2, D_MODEL)))
    scratch = [
        pltpu.VMEM((D_MODEL, D_IN), BF16), pltpu.VMEM((D_MIX, D_MODEL), BF16),
        pltpu.VMEM((tt, D_MODEL), BF16), pltpu.VMEM((tt, D_MIX), F32), pltpu.VMEM((tt, D_MIX), BF16),
        pltpu.VMEM((tt, D_MODEL), F32),
        pltpu.VMEM((tt + SUBLANES, GW), F32), pltpu.VMEM((tt + HALO, GW), F32), pltpu.VMEM((tt + 16, GW), F32),
        pltpu.VMEM((tt + SUBLANES, GW), F32), pltpu.VMEM((tt + HALO, GW), F32), pltpu.VMEM((tt + 16, GW), F32),
        pltpu.VMEM((SUBLANES, tt + HALO, GW), F32),
        pltpu.VMEM((tt, GW), BF16), pltpu.VMEM((tt, GW), F32), pltpu.VMEM((tt, GW), BF16), pltpu.VMEM((tt, GW), F32),
        pltpu.VMEM((N_RACC, SUBLANES, GW), F32), pltpu.VMEM((len(POOL_WINDOWS), PGD, PGD), F32),
        pltpu.VMEM((D_MIX, D_MODEL), F32),
    ]
    if has_prev:
        scratch.append(pltpu.VMEM((2, SUBLANES, D_MODEL), F32))
    n_in, n_out = len(ins), len(out_shape)
    ins, in_specs, out_shape, out_specs, scratch = _with_comm(comm, ins, in_specs, out_shape, out_specs, scratch)
    return pl.pallas_call(
        body, name=("bwd_layer_prev" if has_prev else "bwd_layer") + ("" if comm is None else "_comm"), grid=(n_t,),
        in_specs=in_specs, out_specs=out_specs, out_shape=out_shape, scratch_shapes=scratch,
        compiler_params=pltpu.CompilerParams(dimension_semantics=("arbitrary",), vmem_limit_bytes=VMEM_LIMIT),
    )(*ins)


def _wgrad_in(xb, dh, *, tk):
    t_len = xb.shape[0]
    n_k = t_len // tk

    def body(x_ref, dh_ref, o_ref):
        @pl.when(pl.program_id(1) == 0)
        def _():
            o_ref[...] = jnp.zeros_like(o_ref)
        o_ref[0] += lax.dot_general(x_ref[...], dh_ref[...], (((0,), (0,)), ((), ())), preferred_element_type=F32)

    return pl.pallas_call(
        body, name="wgrad_in", grid=(N_CHIP, n_k),
        in_specs=[pl.BlockSpec((tk, D_MODEL), lambda j, k: (k, 0)), pl.BlockSpec((tk, SHARD_IN), lambda j, k: (k, j))],
        out_specs=pl.BlockSpec((1, D_MODEL, SHARD_IN), lambda j, k: (j, 0, 0)),
        out_shape=jax.ShapeDtypeStruct((N_CHIP, D_MODEL, SHARD_IN), F32),
        compiler_params=pltpu.CompilerParams(dimension_semantics=("arbitrary", "arbitrary"), vmem_limit_bytes=VMEM_LIMIT),
    )(xb, dh)


MESH = pl.DeviceIdType.MESH
ANY = pl.BlockSpec(memory_space=pl.ANY)


def _place():
    x, y, c = lax.axis_index("x"), lax.axis_index("y"), lax.axis_index("c")
    others = [(1 - x, y), (x, 1 - y), (1 - x, 1 - y)]
    return x, y, c, 2 * x + y, [(ox, oy, 2 * ox + oy) for ox, oy in others]


def _rcopy(src, dst, send_sems, recv_sems, k, dev):
    return pltpu.make_async_remote_copy(src_ref=src, dst_ref=dst, send_sem=send_sems.at[k], recv_sem=recv_sems.at[k],
                                        device_id=dev, device_id_type=MESH)


def _gather_weights(w_in, w_out, cw):
    hi, ho = D_MODEL // 2, SHARD_OUT // 2

    def body(win_ref, wout_ref, cw_ref, owin, owout, ocw, bin_v, bout_v, send_sems, recv_sems, lsem):
        x, y, c, me, others = _place()
        for l in range(DEPTH):
            for r0 in range(0, D_MODEL, 256):
                bin_v[l, r0:r0 + 256, :] = win_ref[l, r0:r0 + 256, :].astype(BF16)
            bout_v[l] = wout_ref[l].astype(BF16)
        cin = pl.ds(pl.multiple_of(me * SHARD_IN, 128), SHARD_IN)
        rout = pl.ds(pl.multiple_of(me * SHARD_OUT, 128), SHARD_OUT)
        local = [pltpu.make_async_copy(bin_v.at[0], owin.at[:, cin], lsem.at[0]),
                 pltpu.make_async_copy(bout_v.at[0], owout.at[rout, :], lsem.at[1]),
                 pltpu.make_async_copy(cw_ref, ocw.at[me], lsem.at[2])]
        for cp in local:
            cp.start()

        def in_half(chip, core):
            return owin.at[pl.ds(pl.multiple_of(core * hi, 256), hi), pl.ds(pl.multiple_of(chip * SHARD_IN, 128), SHARD_IN)]

        def out_half(chip, core):
            return owout.at[pl.ds(pl.multiple_of(chip * SHARD_OUT + core * ho, 64), ho), :]

        first = []
        for k, (ox, oy, _) in enumerate(others):
            dev = (ox, oy, c)
            first.append(_rcopy(bin_v.at[0, pl.ds(pl.multiple_of(c * hi, 256), hi), :], in_half(me, c), send_sems, recv_sems, k, dev))
            first.append(_rcopy(bout_v.at[0, pl.ds(pl.multiple_of(c * ho, 64), ho), :], out_half(me, c), send_sems, recv_sems, 3 + k, dev))
            first.append(_rcopy(cw_ref, ocw.at[me], send_sems, recv_sems, 6 + k, dev))
        for cp in first:
            cp.start()
        sib = (x, y, 1 - c)
        passed = []
        for k, (ox, oy, oc) in enumerate(others):
            _rcopy(in_half(oc, c), in_half(oc, c), send_sems, recv_sems, k, sib).wait_recv()
            fwd_in = _rcopy(in_half(oc, c), in_half(oc, c), send_sems, recv_sems, 9 + k, sib)
            fwd_in.start()
            _rcopy(out_half(oc, c), out_half(oc, c), send_sems, recv_sems, 3 + k, sib).wait_recv()
            fwd_out = _rcopy(out_half(oc, c), out_half(oc, c), send_sems, recv_sems, 12 + k, sib)
            fwd_out.start()
            passed += [fwd_in, fwd_out]
        for k, (ox, oy, oc) in enumerate(others):
            _rcopy(cw_ref, ocw.at[oc], send_sems, recv_sems, 6 + k, sib).wait_recv()
            _rcopy(in_half(oc, 1 - c), in_half(oc, 1 - c), send_sems, recv_sems, 9 + k, sib).wait_recv()
            _rcopy(out_half(oc, 1 - c), out_half(oc, 1 - c), send_sems, recv_sems, 12 + k, sib).wait_recv()
        for cp in first + passed:
            cp.wait_send()
        for cp in local:
            cp.wait()

    vm = pl.BlockSpec(memory_space=pltpu.VMEM)
    return pl.pallas_call(
        body, name="gather_weights",
        in_specs=[vm, vm, vm], out_specs=[ANY, ANY, ANY, vm, vm],
        out_shape=[jax.ShapeDtypeStruct((D_MODEL, D_IN), BF16), jax.ShapeDtypeStruct((D_MIX, D_MODEL), BF16),
                   jax.ShapeDtypeStruct((N_CHIP,) + cw.shape, F32),
                   jax.ShapeDtypeStruct((DEPTH, D_MODEL, SHARD_IN), BF16), jax.ShapeDtypeStruct((DEPTH, SHARD_OUT, D_MODEL), BF16)],
        scratch_shapes=[pltpu.SemaphoreType.DMA((15,)), pltpu.SemaphoreType.DMA((15,)), pltpu.SemaphoreType.DMA((3,))],
        compiler_params=pltpu.CompilerParams(vmem_limit_bytes=VMEM_LIMIT),
    )(w_in, w_out, cw)


def _gather_starts(bsh_in, bsh_out, owin, owout, send_sems, recv_sems, lsem, layer):
    x, y, c, me, others = _place()
    hi, ho = D_MODEL // 2, SHARD_OUT // 2
    pltpu.make_async_copy(bsh_in.at[layer], owin.at[:, pl.ds(pl.multiple_of(me * SHARD_IN, 128), SHARD_IN)], lsem.at[0]).start()
    pltpu.make_async_copy(bsh_out.at[layer], owout.at[pl.ds(pl.multiple_of(me * SHARD_OUT, 128), SHARD_OUT), :], lsem.at[1]).start()
    for k, (ox, oy, _) in enumerate(others):
        for t in range(2):
            pltpu.make_async_remote_copy(
                src_ref=bsh_in.at[layer, pl.ds(pl.multiple_of(c * hi, 256), hi), :],
                dst_ref=owin.at[pl.ds(pl.multiple_of(c * hi, 256), hi), pl.ds(pl.multiple_of(me * SHARD_IN, 128), SHARD_IN)],
                send_sem=send_sems.at[2 * k + t], recv_sem=recv_sems.at[2 * k + c], device_id=(ox, oy, t), device_id_type=MESH).start()
            pltpu.make_async_remote_copy(
                src_ref=bsh_out.at[layer, pl.ds(pl.multiple_of(c * ho, 64), ho), :],
                dst_ref=owout.at[pl.ds(pl.multiple_of(me * SHARD_OUT + c * ho, 64), ho), :],
                send_sem=send_sems.at[6 + 2 * k + t], recv_sem=recv_sems.at[6 + 2 * k + c], device_id=(ox, oy, t), device_id_type=MESH).start()


def _gather_waits(bsh_in, bsh_out, owin, owout, send_sems, recv_sems, lsem, layer):
    x, y, c, me, others = _place()
    hi, ho = D_MODEL // 2, SHARD_OUT // 2
    src_in = bsh_in.at[layer, pl.ds(0, hi), :]
    src_out = bsh_out.at[layer, pl.ds(0, ho), :]
    for k, (ox, oy, oc) in enumerate(others):
        for t in range(2):
            dst_in = owin.at[pl.ds(t * hi, hi), pl.ds(pl.multiple_of(oc * SHARD_IN, 128), SHARD_IN)]
            dst_out = owout.at[pl.ds(pl.multiple_of(oc * SHARD_OUT + t * ho, 64), ho), :]
            a = pltpu.make_async_remote_copy(src_ref=src_in, dst_ref=dst_in, send_sem=send_sems.at[2 * k + t],
                                             recv_sem=recv_sems.at[2 * k + t], device_id=(ox, oy, t), device_id_type=MESH)
            b = pltpu.make_async_remote_copy(src_ref=src_out, dst_ref=dst_out, send_sem=send_sems.at[6 + 2 * k + t],
                                             recv_sem=recv_sems.at[6 + 2 * k + t], device_id=(ox, oy, t), device_id_type=MESH)
            a.wait_send()
            a.wait_recv()
            b.wait_send()
            b.wait_recv()
    pltpu.make_async_copy(bsh_in.at[layer], owin.at[:, pl.ds(pl.multiple_of(me * SHARD_IN, 128), SHARD_IN)], lsem.at[0]).wait()
    pltpu.make_async_copy(bsh_out.at[layer], owout.at[pl.ds(pl.multiple_of(me * SHARD_OUT, 128), SHARD_OUT), :], lsem.at[1]).wait()


def _gather_comm(bsh_in, bsh_out, layer):
    return dict(ins=[bsh_in, bsh_out],
                out_shape=[jax.ShapeDtypeStruct((D_MODEL, D_IN), BF16), jax.ShapeDtypeStruct((D_MIX, D_MODEL), BF16)],
                sems=[pltpu.SemaphoreType.DMA((12,)), pltpu.SemaphoreType.DMA((12,)), pltpu.SemaphoreType.DMA((2,))],
                start=lambda ins, outs, sems: _gather_starts(ins[0], ins[1], outs[0], outs[1], *sems, layer),
                wait=lambda ins, outs, sems: _gather_waits(ins[0], ins[1], outs[0], outs[1], *sems, layer))


def _exchange_halves(arrs, tag):
    n = len(arrs)

    def body(*refs):
        ins, outs, (send_sems, recv_sems) = refs[:n], refs[n:2 * n], refs[2 * n:]
        x, y, c, _, _ = _place()
        cps = []
        for m in range(n):
            half = ins[m].shape[1] // 2
            cps.append(_rcopy(ins[m].at[:, pl.ds(pl.multiple_of((1 - c) * half, SUBLANES), half), :], outs[m],
                              send_sems, recv_sems, m, (x, y, 1 - c)))
        for cp in cps:
            cp.start()
        for cp in cps:
            cp.wait()

    return pl.pallas_call(
        body, name="exchange_halves_" + tag, in_specs=[ANY] * n, out_specs=[ANY] * n,
        out_shape=[jax.ShapeDtypeStruct((a.shape[0], a.shape[1] // 2, a.shape[2]), F32) for a in arrs],
        scratch_shapes=[pltpu.SemaphoreType.DMA((n,)), pltpu.SemaphoreType.DMA((n,))],
    )(*arrs)


def _add_own_half(a, got, core, *, rb, dtype):
    nj, r, cdim = a.shape
    half = r // 2

    def body(core_ref, a_ref, g_ref, o_ref):
        o_ref[...] = (a_ref[0] + g_ref[...]).astype(dtype)

    return pl.pallas_call(
        body, name="add_own_half",
        grid_spec=pltpu.PrefetchScalarGridSpec(
            num_scalar_prefetch=1, grid=(nj, half // rb),
            in_specs=[pl.BlockSpec((1, 1, rb, cdim), lambda j, i, cr: (j, cr[0], i, 0)),
                      pl.BlockSpec((1, rb, cdim), lambda j, i, cr: (j, i, 0))],
            out_specs=pl.BlockSpec((1, rb, cdim), lambda j, i, cr: (j, i, 0))),
        out_shape=jax.ShapeDtypeStruct((nj, half, cdim), dtype),
    )(core, a.reshape(nj, 2, half, cdim), got)


def _owner_starts(ins, outs, sems):
    send_sems, recv_sems, lsem = sems
    x, y, c, me, others = _place()
    for m in range(len(ins)):
        pltpu.make_async_copy(ins[m].at[me], outs[m].at[me], lsem.at[m]).start()
        for k, (ox, oy, oc) in enumerate(others):
            _rcopy(ins[m].at[oc], outs[m].at[me], send_sems, recv_sems, 3 * m + k, (ox, oy, c)).start()


def _owner_waits(ins, outs, sems):
    send_sems, recv_sems, lsem = sems
    x, y, c, me, others = _place()
    for m in range(len(ins)):
        for k, (ox, oy, oc) in enumerate(others):
            _rcopy(ins[m].at[oc], outs[m].at[oc], send_sems, recv_sems, 3 * m + k, (ox, oy, c)).wait()
        pltpu.make_async_copy(ins[m].at[me], outs[m].at[me], lsem.at[m]).wait()


def _owner_comm(arrs):
    n = len(arrs)
    return dict(ins=arrs, out_shape=[jax.ShapeDtypeStruct(a.shape, a.dtype) for a in arrs],
                sems=[pltpu.SemaphoreType.DMA((3 * n,)), pltpu.SemaphoreType.DMA((3 * n,)), pltpu.SemaphoreType.DMA((n,))],
                start=_owner_starts, wait=_owner_waits)


def _send_to_owners(arrs):
    n = len(arrs)

    def body(*refs):
        ins, outs, sems = refs[:n], refs[n:2 * n], refs[2 * n:]
        _owner_starts(ins, outs, sems)
        _owner_waits(ins, outs, sems)

    job = _owner_comm(arrs)
    return pl.pallas_call(
        body, name="send_to_owners", in_specs=[ANY] * n, out_specs=[ANY] * n,
        out_shape=job["out_shape"], scratch_shapes=job["sems"],
    )(*arrs)


def _sum_chips(a, *, rb):
    nj, r, cdim = a.shape

    def body(a_ref, o_ref):
        f = lambda k: a_ref[k].astype(F32)
        o_ref[...] = ((f(0) + f(1)) + f(2)) + f(3)

    return pl.pallas_call(
        body, name="sum_chips", grid=(r // rb,),
        in_specs=[pl.BlockSpec((nj, rb, cdim), lambda i: (0, i, 0))],
        out_specs=pl.BlockSpec((rb, cdim), lambda i: (i, 0)),
        out_shape=jax.ShapeDtypeStruct((r, cdim), F32),
    )(a)


def _spread_reduced(red_in, red_out, red_small):
    hs = red_small.shape[0]

    def body(in0, in1, out0, out1, sm, fin, fout, fsm, gsm, send_sems, recv_sems, lsem):
        x, y, c, me, others = _place()
        sib = (x, y, 1 - c)
        hi, ho = D_MODEL // 2, SHARD_OUT // 2
        ri, ro = pl.ds(pl.multiple_of(c * hi, SUBLANES), hi), pl.ds(pl.multiple_of(c * ho, SUBLANES), ho)
        jobs = [(in0, fin.at[0, ri, :]), (in1, fin.at[1, ri, :]), (out0, fout.at[0, ro, :]), (out1, fout.at[1, ro, :])]
        local = [pltpu.make_async_copy(src, dst, lsem.at[m]) for m, (src, dst) in enumerate(jobs)]
        remote = [_rcopy(src, dst, send_sems, recv_sems, m, sib) for m, (src, dst) in enumerate(jobs)]
        own_small = pltpu.make_async_copy(sm, gsm.at[me], lsem.at[4])
        small = [_rcopy(sm, gsm.at[me], send_sems, recv_sems, 4 + k, (ox, oy, c)) for k, (ox, oy, _) in enumerate(others)]
        for cp in local + remote + [own_small] + small:
            cp.start()
        own_small.wait()
        for cp in small:
            cp.wait()
        mine = fsm.at[:, pl.ds(pl.multiple_of(c * hs, SUBLANES), hs), :]
        keep = pltpu.make_async_copy(gsm, mine, lsem.at[5])
        give = _rcopy(gsm, mine, send_sems, recv_sems, 7, sib)
        keep.start()
        give.start()
        for cp in remote + [give]:
            cp.wait()
        for cp in local + [keep]:
            cp.wait()

    return pl.pallas_call(
        body, name="spread_reduced", in_specs=[ANY] * 5, out_specs=[ANY] * 4,
        out_shape=[jax.ShapeDtypeStruct((DEPTH, D_MODEL, SHARD_IN), F32), jax.ShapeDtypeStruct((DEPTH, SHARD_OUT, D_MODEL), F32),
                   jax.ShapeDtypeStruct((N_CHIP, 2 * hs, GW), F32), jax.ShapeDtypeStruct((N_CHIP, hs, GW), F32)],
        scratch_shapes=[pltpu.SemaphoreType.DMA((8,)), pltpu.SemaphoreType.DMA((8,)), pltpu.SemaphoreType.DMA((6,))],
    )(red_in[0], red_in[1], red_out[0], red_out[1], red_small)[:3]


def _adamw_math(w, g, m, v):
    m = ADAM_B1 * m + (1.0 - ADAM_B1) * g
    v = ADAM_B2 * v + (1.0 - ADAM_B2) * (g * g)
    m_hat = m / (1.0 - ADAM_B1 ** ADAM_STEP)
    v_hat = v / (1.0 - ADAM_B2 ** ADAM_STEP)
    delta = -ADAM_LR * (m_hat / (jnp.sqrt(v_hat) + ADAM_EPS) + ADAM_WD * w)
    return delta, m, v


def _adamw_big(w, g, m, v, *, rb):
    r, cdim = w.shape

    def body(w_ref, g_ref, m_ref, v_ref, d_ref, nm_ref, nv_ref):
        d_ref[...], nm_ref[...], nv_ref[...] = _adamw_math(w_ref[...], g_ref[...], m_ref[...], v_ref[...])

    spec = pl.BlockSpec((rb, cdim), lambda i: (i, 0))
    return pl.pallas_call(
        body, name="adamw_big", grid=(r // rb,), in_specs=[spec] * 4, out_specs=[spec] * 3,
        out_shape=[jax.ShapeDtypeStruct((r, cdim), F32)] * 3,
    )(w, g, m, v)


def _adamw_small(ws, gs, ms, vs):
    n = len(ws)

    def body(*refs):
        w, g, m, v = refs[:n], refs[n:2 * n], refs[2 * n:3 * n], refs[3 * n:4 * n]
        d, nm, nv = refs[4 * n:5 * n], refs[5 * n:6 * n], refs[6 * n:7 * n]
        for k in range(n):
            d[k][...], nm[k][...], nv[k][...] = _adamw_math(w[k][...], g[k][...], m[k][...], v[k][...])

    shapes = [jax.ShapeDtypeStruct(a.shape, F32) for a in ws]
    outs = pl.pallas_call(body, name="adamw_small", out_shape=shapes * 3)(*ws, *gs, *ms, *vs)
    return outs[:n], outs[n:2 * n], outs[2 * n:]


TT = 256
TK = 512
CW_ROWS = 40
PACK_ROWS = 192


def _pack(rows):
    packed = jnp.concatenate(rows, axis=0)
    packed = jnp.pad(packed, ((0, PACK_ROWS - packed.shape[0]), (0, 0)))
    return packed.reshape(N_CHIP, PACK_ROWS // N_CHIP, GW)


def _reduce_to_owner_halves(parts, core1, tag):
    got = _exchange_halves(parts, tag)
    rbs = [256, SHARD_OUT // 2, PACK_ROWS // N_CHIP // 2]
    dts = [BF16, BF16, F32]
    return [_add_own_half(a, g, core1, rb=rb, dtype=dt) for a, g, rb, dt in zip(parts, got, rbs, dts)]


def kernel(x, w_in, conv_a_w, conv_a_b, conv_b_w, conv_b_b, ln_b_g, ln_b_b, pool_w, pool_b, pool_scale, w_out, ln_g, ln_b, loss_target, m_w_in, m_conv_a_w, m_conv_a_b, m_conv_b_w, m_conv_b_b, m_ln_b_g, m_ln_b_b, m_pool_w, m_pool_b, m_pool_scale, m_w_out, m_ln_g, m_ln_b, v_w_in, v_conv_a_w, v_conv_a_b, v_conv_b_w, v_conv_b_b, v_ln_b_g, v_ln_b_b, v_pool_w, v_pool_b, v_pool_scale, v_w_out, v_ln_g, v_ln_b):
    chip = 2 * lax.axis_index("x") + lax.axis_index("y")
    core1 = lax.axis_index("c").reshape(1).astype(jnp.int32)
    x2, tgt = x[0], loss_target[0]

    cw = jnp.zeros((DEPTH, CW_ROWS, PGD), F32).at[:, 0:KA].set(conv_a_w).at[:, 8:8 + KB].set(conv_b_w)
    win0_b, wout0_b, cw_all, bsh_in, bsh_out = _gather_weights(w_in, w_out, cw)
    cw_full = jnp.transpose(cw_all, (1, 2, 0, 3)).reshape(DEPTH, CW_ROWS, GW)
    row = lambda a, l: a[l].reshape(1, -1)
    cnt = _count_table()
    prm = [(cw_full[l, 0:KA], row(conv_a_b, l), cw_full[l, 8:8 + KB], row(conv_b_b, l), row(ln_b_g, l), row(ln_b_b, l),
            pool_w[l].astype(BF16), row(pool_b, l), row(pool_scale, l), cnt) for l in range(DEPTH)]

    h0, xb0, z0, x1, win1_b, wout1_b = _fwd_layer(x2, win0_b, wout0_b, prm[0], row(ln_g, 0), row(ln_b, 0), None, tt=TT, last=False,
                                                  comm=_gather_comm(bsh_in, bsh_out, 1))
    h1, xb1, dz1, dln1, loss8 = _fwd_layer(x1, win1_b, wout1_b, prm[1], row(ln_g, 1), row(ln_b, 1), tgt, tt=TT, last=True)

    dz0, dh1, dwout1, small1, dpw1, dln0 = _bwd_layer(dz1, h1, win1_b, wout1_b, prm[1], z0, row(ln_g, 0), tt=TT)
    dwin1 = _wgrad_in(xb1, dh1, tk=TK)
    pack1 = _pack([small1, dpw1.reshape(PGD, GW), dln1.reshape(4, GW), dln0.reshape(4, GW)])
    sums1 = _reduce_to_owner_halves([dwin1, dwout1.reshape(N_CHIP, SHARD_OUT, D_MODEL), pack1], core1, "1")
    gx, dh0, dwout0, small0, dpw0, *landed1 = _bwd_layer(dz0, h0, win0_b, wout0_b, prm[0], None, None, tt=TT, comm=_owner_comm(sums1))
    dwin0 = _wgrad_in(xb0, dh0, tk=TK)
    pack0 = _pack([small0, dpw0.reshape(PGD, GW)])
    sums0 = _reduce_to_owner_halves([dwin0, dwout0.reshape(N_CHIP, SHARD_OUT, D_MODEL), pack0], core1, "0")
    landed0 = _send_to_owners(sums0)

    rbs = [256, SHARD_OUT // 2, PACK_ROWS // N_CHIP // 2]
    red0 = [_sum_chips(a, rb=rb) for a, rb in zip(landed0, rbs)]
    red1 = [_sum_chips(a, rb=rb) for a, rb in zip(landed1, rbs)]
    g_in, g_out, g_small = _spread_reduced([red0[0], red1[0]], [red0[1], red1[1]], jnp.concatenate([red0[2], red1[2]], axis=0))

    flat = lambda a: a.reshape(-1, a.shape[-1])
    unflat = lambda a, like: a.reshape(like.shape)
    d_in, nm_in, nv_in = [unflat(a, w_in) for a in _adamw_big(flat(w_in), flat(g_in), flat(m_w_in), flat(v_w_in), rb=256)]
    d_out, nm_out, nv_out = [unflat(a, w_out) for a in _adamw_big(flat(w_out), flat(g_out), flat(m_w_out), flat(v_w_out), rb=SHARD_OUT)]

    hp = PACK_ROWS // N_CHIP // 2
    unpack = lambda o: jnp.concatenate([g_small[:, o:o + hp], g_small[:, 2 * hp + o:3 * hp + o]], axis=1).reshape(PACK_ROWS, GW)
    p0, p1 = unpack(0), unpack(hp)
    small = [p0[0:N_RACC], p1[0:N_RACC]]
    dpw = [p[N_RACC:N_RACC + PGD].reshape(len(POOL_WINDOWS), PGD, PGD) for p in (p0, p1)]
    o = N_RACC + PGD
    g_lng = jnp.stack([p1[o + 4:o + 8].reshape(2, D_MODEL)[0], p1[o:o + 4].reshape(2, D_MODEL)[0]])
    g_lnb = jnp.stack([p1[o + 4:o + 8].reshape(2, D_MODEL)[1], p1[o:o + 4].reshape(2, D_MODEL)[1]])
    mine = lambda a: lax.dynamic_slice_in_dim(a, chip * PGD, PGD, axis=-1)
    stack = lambda f: jnp.stack([f(0), f(1)])
    g_caw = stack(lambda l: mine(small[l][R_DWA:R_DWA + KA]))
    g_cab = stack(lambda l: small[l][R_DCAB])
    g_cbw = stack(lambda l: mine(small[l][R_DWB:R_DWB + KB]))
    g_cbb = stack(lambda l: small[l][R_DCBB])
    g_lbg = stack(lambda l: small[l][R_DLBG])
    g_lbb = stack(lambda l: small[l][R_DLBB])
    g_pw = stack(lambda l: dpw[l])
    g_pb = stack(lambda l: small[l][R_DPB].reshape(len(POOL_WINDOWS), PGD))
    g_ps = stack(lambda l: small[l][R_DPS])
    ws = [conv_a_w, conv_a_b, conv_b_w, conv_b_b, ln_b_g, ln_b_b, pool_w, pool_b, pool_scale, ln_g, ln_b]
    gs = [g_caw, g_cab, g_cbw, g_cbb, g_lbg, g_lbb, g_pw, g_pb, g_ps, g_lng, g_lnb]
    ms = [m_conv_a_w, m_conv_a_b, m_conv_b_w, m_conv_b_b, m_ln_b_g, m_ln_b_b, m_pool_w, m_pool_b, m_pool_scale, m_ln_g, m_ln_b]
    vs = [v_conv_a_w, v_conv_a_b, v_conv_b_w, v_conv_b_b, v_ln_b_g, v_ln_b_b, v_pool_w, v_pool_b, v_pool_scale, v_ln_g, v_ln_b]
    ds, nms, nvs = _adamw_small([flat(a) for a in ws], [flat(a) for a in gs], [flat(a) for a in ms], [flat(a) for a in vs])
    ds, nms, nvs = ([unflat(a, w) for a, w in zip(t, ws)] for t in (ds, nms, nvs))

    loss = lax.psum(loss8[0, 0], ("x", "y", "c"))

    def order(in_, small_, out_):
        return [in_, *small_[:9], out_, *small_[9:]]
    return (loss, gx[None], *order(g_in, gs, g_out), *order(d_in, ds, d_out), *order(nm_in, nms, nm_out), *order(nv_in, nvs, nv_out))
```

```python
import functools

import jax
import jax.numpy as jnp
import numpy as np
from jax import lax
from jax.experimental import pallas as pl
from jax.experimental.pallas import tpu as pltpu

F32 = jnp.float32
BF16 = jnp.bfloat16

D_MODEL = 1024
DEPTH = 2
GW = 512
D_IN = 9 * GW
D_MIX = 3 * GW
POOL_WINDOWS = (2, 4, 8, 16)
PGD = 128
KA = 3
KB = 31
ALPHA = (2.0 * DEPTH) ** 0.25
LN_EPS = 1e-5
ADAM_LR, ADAM_B1, ADAM_B2, ADAM_EPS, ADAM_WD, ADAM_STEP = 0.001, 0.9, 0.999, 1e-08, 0.01, 10

N_CHIP = 4
SHARD_IN = D_IN // N_CHIP
SHARD_OUT = D_MIX // N_CHIP

SUBLANES = 8
RC = 32
HALO = 32
VMEM_LIMIT = 60 * 1024 * 1024

R_DWA, R_DCAB, R_DWB, R_DCBB, R_DLBG, R_DLBB, R_DPB, R_DPS, N_RACC = 0, 3, 4, 35, 36, 37, 38, 39, 40


def _sig(v):
    return 0.5 * jnp.tanh(0.5 * v) + 0.5


def _chunks(n_rows, fn, unroll=1):
    def step(c, carry):
        fn(pl.multiple_of(c * RC, RC))
        return carry
    lax.fori_loop(0, n_rows // RC, step, 0, unroll=unroll)


def _fold8(v):
    return v.reshape(RC // SUBLANES, SUBLANES, v.shape[-1]).sum(axis=0)


def _build_shifts(ext_ref, sh_ref, shifts, n_rows):
    for r in shifts:
        for c0 in range(0, n_rows, RC):
            n = min(RC, n_rows - c0)
            sh_ref[r, pl.ds(c0, n), :] = ext_ref[pl.ds(c0 + r, n), :]


def _tap(ext_ref, sh_ref, off, base, lanes=None):
    a, r = divmod(off, SUBLANES)
    src = ext_ref if r == 0 else sh_ref.at[r]
    if lanes is None:
        return src[pl.ds(base + SUBLANES * a, RC), :]
    return src[pl.ds(base + SUBLANES * a, RC), lanes]


def _ln_stats(v):
    mu = jnp.mean(v, axis=-1, keepdims=True)
    vc = v - mu
    var = jnp.mean(vc * vc, axis=-1, keepdims=True)
    rstd = lax.rsqrt(var + LN_EPS)
    return vc * rstd, rstd


def _ln_bwd(dy, xhat, rstd, g):
    dxh = dy * g
    m1 = jnp.mean(dxh, axis=-1, keepdims=True)
    m2 = jnp.mean(dxh * xhat, axis=-1, keepdims=True)
    return rstd * (dxh - m1 - xhat * m2)


def _for_taps(ext_ref, sh_ref, base, offsets, fn):
    for r in range(SUBLANES):
        offs = [o for o in offsets if o % SUBLANES == r]
        if not offs:
            continue
        a0, a1 = min(offs) // SUBLANES, max(offs) // SUBLANES
        src = ext_ref if r == 0 else sh_ref.at[r]
        win = src[pl.ds(base + SUBLANES * a0, RC + SUBLANES * (a1 - a0)), :]
        for o in offs:
            a = o // SUBLANES - a0
            fn(o, win[SUBLANES * a:SUBLANES * a + RC])


def _count_table():
    t = np.arange(1, RC + 1, dtype=np.float64)[:, None]
    w = np.repeat(np.asarray(POOL_WINDOWS, np.float64), PGD)[None, :]
    return jnp.asarray(1.0 / np.minimum(t, w), F32)


def _inv_count(cnt_ref, first):
    return jnp.where(first, cnt_ref[...], cnt_ref[RC - 1:RC, :])


def _hcol(h_ref, j, base):
    return h_ref[pl.ds(base, RC), j * GW:(j + 1) * GW].astype(F32)


def _with_comm(comm, ins, in_specs, out_shape, out_specs, scratch):
    if comm is None:
        return ins, in_specs, out_shape, out_specs, scratch
    hbm = pl.BlockSpec(memory_space=pl.ANY)
    return (ins + list(comm["ins"]), in_specs + [hbm] * len(comm["ins"]), out_shape + list(comm["out_shape"]),
            out_specs + [hbm] * len(comm["out_shape"]), scratch + list(comm["sems"]))


def _split_comm(refs, comm, n_in, n_out):
    refs = list(refs)
    if comm is None:
        return refs, None
    ci, co, cs = len(comm["ins"]), len(comm["out_shape"]), len(comm["sems"])
    own = refs[:n_in] + refs[n_in + ci:n_in + ci + n_out] + refs[n_in + ci + n_out + co:len(refs) - cs]
    return own, (refs[n_in:n_in + ci], refs[n_in + ci + n_out:n_in + ci + n_out + co], refs[len(refs) - cs:])


def _fwd_mixers(h_ref, y_scr, q_ext, ub_ext, cu_ext, sh, p_scr, pl_scr, prm, tt, t0):
    caw, cab, cbw, cbb, lbg, lbb, pw, pb, ps, cnt = prm

    def a1(base):
        q_ext[pl.ds(SUBLANES + base, RC), :] = _hcol(h_ref, 1, base) * _hcol(h_ref, 2, base)
    _chunks(tt, a1)
    _build_shifts(q_ext, sh, (6, 7), tt)

    def a2(base):
        ca = cab[...] + caw[0:1, :] * _tap(q_ext, sh, 6, base) + caw[1:2, :] * _tap(q_ext, sh, 7, base) \
            + caw[2:3, :] * _tap(q_ext, sh, 8, base)
        z = _hcol(h_ref, 3, base)
        y_scr[pl.ds(base, RC), 0:GW] = (_hcol(h_ref, 0, base) * ca * (z * _sig(z))).astype(BF16)
    _chunks(tt, a2, unroll=2)
    q_ext[0:SUBLANES, :] = q_ext[tt:tt + SUBLANES, :]

    def b1(base):
        ub_ext[pl.ds(HALO + base, RC), :] = _hcol(h_ref, 4, base) * _sig(_hcol(h_ref, 5, base))
    _chunks(tt, b1)
    _build_shifts(ub_ext, sh, range(1, 8), tt + HALO - SUBLANES)

    def b2(base):
        cb = [cbb[...] + jnp.zeros((RC, GW), F32)]

        def tap(off, v):
            cb[0] = cb[0] + cbw[off - 2:off - 1, :] * v
        _for_taps(ub_ext, sh, base, range(2, 2 + KB), tap)
        xhat, _ = _ln_stats(cb[0])
        lnv = xhat * lbg[...] + lbb[...]
        z = _hcol(h_ref, 6, base)
        y_scr[pl.ds(base, RC), GW:2 * GW] = (lnv * _sig(lnv) * (z * _sig(z))).astype(BF16)
    _chunks(tt, b2, unroll=2)
    ub_ext[0:HALO, :] = ub_ext[tt:tt + HALO, :]

    def c1(base):
        cu_ext[pl.ds(16 + base, RC), :] = _hcol(h_ref, 7, base)
    _chunks(tt, c1)
    _build_shifts(cu_ext, sh, range(1, 8), tt + SUBLANES)

    def c2(base):
        ic = _inv_count(cnt, base + t0 == 0)
        for g, w in enumerate(POOL_WINDOWS):
            lanes = slice(g * PGD, (g + 1) * PGD)
            acc = _tap(cu_ext, sh, 16, base, lanes)
            for j in range(1, w):
                acc = acc + _tap(cu_ext, sh, 16 - j, base, lanes)
            p = acc * ic[:, lanes] - _tap(cu_ext, sh, 16, base, lanes)
            p_scr[pl.ds(base, RC), lanes] = p.astype(BF16)
    _chunks(tt, c2)
    cu_ext[0:16, :] = cu_ext[tt:tt + 16, :]
    for g in range(len(POOL_WINDOWS)):
        lanes = slice(g * PGD, (g + 1) * PGD)
        pl_scr[:, lanes] = jnp.dot(p_scr[:, lanes], pw[g], preferred_element_type=F32)

    def c3(base):
        z = _hcol(h_ref, 8, base)
        yc0 = (pl_scr[pl.ds(base, RC), :] + pb[...]) * ps[...]
        y_scr[pl.ds(base, RC), 2 * GW:3 * GW] = (yc0 * (z * _sig(z))).astype(BF16)
    _chunks(tt, c3, unroll=2)


def _fwd_layer(x, win_b, wout_b, prm, ln_g, ln_b, target, *, tt, last, comm=None):
    t_len = x.shape[0]
    n_t = t_len // tt

    def body(*refs):
        refs, comm_refs = _split_comm(refs, comm, n_in, n_out)
        if last:
            (x_ref, win_hbm, wout_hbm, caw, cab, cbw, cbb, lbg, lbb, pw, pb, ps, cnt, lng, lnb, tgt_ref,
             h_ref, xb_ref, dz_ref, dln_ref, loss_ref,
             win_v, wout_v, y_scr, o_scr, q_ext, ub_ext, cu_ext, sh, p_scr, pl_scr, acc2, lacc) = refs
        else:
            (x_ref, win_hbm, wout_hbm, caw, cab, cbw, cbb, lbg, lbb, pw, pb, ps, cnt, lng, lnb,
             h_ref, xb_ref, z_ref, xn_ref,
             win_v, wout_v, y_scr, o_scr, q_ext, ub_ext, cu_ext, sh, p_scr, pl_scr) = refs
        i = pl.program_id(0)

        @pl.when(i == 0)
        def _():
            if comm is not None:
                comm["start"](*comm_refs)
            pltpu.sync_copy(win_hbm, win_v)
            pltpu.sync_copy(wout_hbm, wout_v)
            q_ext[0:SUBLANES, :] = jnp.zeros((SUBLANES, GW), F32)
            ub_ext[0:HALO, :] = jnp.zeros((HALO, GW), F32)
            cu_ext[0:16, :] = jnp.zeros((16, GW), F32)
            if last:
                acc2[...] = jnp.zeros_like(acc2)
                lacc[...] = jnp.zeros_like(lacc)

        xb_ref[...] = x_ref[...].astype(BF16)
        for j in range(D_IN // GW):
            h_ref[:, j * GW:(j + 1) * GW] = jnp.dot(
                xb_ref[...], win_v[:, j * GW:(j + 1) * GW], preferred_element_type=F32).astype(BF16)

        _fwd_mixers(h_ref, y_scr, q_ext, ub_ext, cu_ext, sh, p_scr, pl_scr,
                    (caw, cab, cbw, cbb, lbg, lbb, pw, pb, ps, cnt), tt, i * tt)

        o_scr[...] = jnp.dot(y_scr[...], wout_v[...], preferred_element_type=F32)

        def post(base):
            rows = pl.ds(base, RC)
            z = ALPHA * x_ref[rows, :] + o_scr[rows, :]
            xhat, rstd = _ln_stats(z)
            xn = xhat * lng[...] + lnb[...]
            if last:
                err = xn - tgt_ref[rows, :]
                lacc[...] += _fold8(err * err)
                dxn = err * (1.0 / D_MODEL)
                acc2[0] += _fold8(dxn * xhat)
                acc2[1] += _fold8(dxn)
                dz_ref[rows, :] = _ln_bwd(dxn, xhat, rstd, lng[...])
            else:
                z_ref[rows, :] = z
                xn_ref[rows, :] = xn
        _chunks(tt, post, unroll=2)

        if last:
            @pl.when(i == n_t - 1)
            def _():
                dln_ref[...] = jnp.sum(acc2[...], axis=1)
                loss_ref[...] = jnp.zeros((SUBLANES, 128), F32) + (0.5 / D_MODEL) * jnp.sum(lacc[...])
        if comm is not None:
            @pl.when(i == n_t - 1)
            def _():
                comm["wait"](*comm_refs)

    tile = lambda c: pl.BlockSpec((tt, c), lambda i: (i, 0))
    full = lambda a: pl.BlockSpec(a.shape, lambda i: (0,) * a.ndim)
    hbm = pl.BlockSpec(memory_space=pl.ANY)
    ins = [x, win_b, wout_b, *prm, ln_g, ln_b] + ([target] if last else [])
    in_specs = [tile(D_MODEL), hbm, hbm] + [full(a) for a in (*prm, ln_g, ln_b)] + ([tile(D_MODEL)] if last else [])
    out_shape = [jax.ShapeDtypeStruct((t_len, D_IN), BF16), jax.ShapeDtypeStruct((t_len, D_MODEL), BF16)]
    out_specs = [tile(D_IN), tile(D_MODEL)]
    if last:
        out_shape += [jax.ShapeDtypeStruct((t_len, D_MODEL), F32), jax.ShapeDtypeStruct((2, D_MODEL), F32),
                      jax.ShapeDtypeStruct((SUBLANES, 128), F32)]
        out_specs += [tile(D_MODEL), pl.BlockSpec((2, D_MODEL), lambda i: (0, 0)),
                      pl.BlockSpec((SUBLANES, 128), lambda i: (0, 0))]
    else:
        out_shape += [jax.ShapeDtypeStruct((t_len, D_MODEL), F32), jax.ShapeDtypeStruct((t_len, D_MODEL), F32)]
        out_specs += [tile(D_MODEL), tile(D_MODEL)]
    scratch = [
        pltpu.VMEM((D_MODEL, D_IN), BF16), pltpu.VMEM((D_MIX, D_MODEL), BF16),
        pltpu.VMEM((tt, D_MIX), BF16), pltpu.VMEM((tt, D_MODEL), F32),
        pltpu.VMEM((tt + SUBLANES, GW), F32), pltpu.VMEM((tt + HALO, GW), F32), pltpu.VMEM((tt + 16, GW), F32),
        pltpu.VMEM((SUBLANES, tt + HALO, GW), F32),
        pltpu.VMEM((tt, GW), BF16), pltpu.VMEM((tt, GW), F32),
    ]
    if last:
        scratch += [pltpu.VMEM((2, SUBLANES, D_MODEL), F32), pltpu.VMEM((SUBLANES, D_MODEL), F32)]
    n_in, n_out = len(ins), len(out_shape)
    ins, in_specs, out_shape, out_specs, scratch = _with_comm(comm, ins, in_specs, out_shape, out_specs, scratch)
    return pl.pallas_call(
        body, name=("fwd_last" if last else "fwd_layer") + ("" if comm is None else "_comm"), grid=(n_t,),
        in_specs=in_specs, out_specs=out_specs, out_shape=out_shape, scratch_shapes=scratch,
        compiler_params=pltpu.CompilerParams(dimension_semantics=("arbitrary",), vmem_limit_bytes=VMEM_LIMIT),
    )(*ins)


def _dsilu(z, sz):
    return sz * (1.0 + z * (1.0 - sz))


def _bwd_layer(dz, h, win_b, wout_b, prm, z_prev, lng_prev, *, tt, comm=None):
    t_len = dz.shape[0]
    n_t = t_len // tt
    has_prev = z_prev is not None
    hb = tt // HALO

    def body(*refs):
        refs, comm_refs = _split_comm(refs, comm, n_in, n_out)
        dz_ref, h_ref, halo_ref, win_hbm, wout_hbm, caw, cab, cbw, cbb, lbg, lbb, pw, pb, ps, cnt = refs[:15]
        k = 15
        if has_prev:
            zp_ref, lngp = refs[k:k + 2]
            k += 2
        dxo_ref, dh_ref, dwout_hbm, small_ref, dpw_ref = refs[k:k + 5]
        k += 5
        if has_prev:
            dlnp_ref = refs[k]
            k += 1
        (win_v, wout_v, dzb, dy_scr, y_scr, dx_scr, q_ext, ub_ext, cu_ext, dca_ext, dcb_ext, dpn_ext, sh,
         p_scr, pl_scr, dpl_scr, dp_scr, racc, dpw_acc, dwout_acc) = refs[k:k + 20]
        k += 20
        if has_prev:
            acc2 = refs[k]
        i = pl.program_id(0)
        ti = n_t - 1 - i
        t0 = ti * tt

        @pl.when(i == 0)
        def _():
            if comm is not None:
                comm["start"](*comm_refs)
            pltpu.sync_copy(win_hbm, win_v)
            pltpu.sync_copy(wout_hbm, wout_v)
            dca_ext[tt:tt + SUBLANES, :] = jnp.zeros((SUBLANES, GW), F32)
            dcb_ext[tt:tt + HALO, :] = jnp.zeros((HALO, GW), F32)
            dpn_ext[tt:tt + 16, :] = jnp.zeros((16, GW), F32)
            racc[...] = jnp.zeros_like(racc)
            dpw_acc[...] = jnp.zeros_like(dpw_acc)
            dwout_acc[...] = jnp.zeros_like(dwout_acc)
            if has_prev:
                acc2[...] = jnp.zeros_like(acc2)

        dzb[...] = dz_ref[...].astype(BF16)
        dy_scr[...] = lax.dot_general(dzb[...], wout_v[...], (((1,), (1,)), ((), ())), preferred_element_type=F32)

        live = (ti > 0).astype(F32)
        hh = lambda j, r0, r1: halo_ref[r0:r1, j * GW:(j + 1) * GW].astype(F32)
        q_ext[0:SUBLANES, :] = live * hh(1, 24, 32) * hh(2, 24, 32)
        ub_ext[0:HALO, :] = live * hh(4, 0, 32) * _sig(hh(5, 0, 32))
        cu_ext[0:16, :] = live * hh(7, 16, 32)

        def a1(base):
            q_ext[pl.ds(SUBLANES + base, RC), :] = _hcol(h_ref, 1, base) * _hcol(h_ref, 2, base)
        _chunks(tt, a1)
        _build_shifts(q_ext, sh, (6, 7), tt)

        def a2(base):
            rows = pl.ds(base, RC)
            q6, q7, q8 = _tap(q_ext, sh, 6, base), _tap(q_ext, sh, 7, base), _tap(q_ext, sh, 8, base)
            ca = cab[...] + caw[0:1, :] * q6 + caw[1:2, :] * q7 + caw[2:3, :] * q8
            bg, z = _hcol(h_ref, 0, base), _hcol(h_ref, 3, base)
            sz = _sig(z)
            sza = z * sz
            dya = dy_scr[rows, 0:GW]
            ya0 = bg * ca
            y_scr[rows, 0:GW] = (ya0 * sza).astype(BF16)
            dya0 = dya * sza
            dh_ref[rows, 3 * GW:4 * GW] = (dya * ya0 * _dsilu(z, sz)).astype(BF16)
            dh_ref[rows, 0:GW] = (dya0 * ca).astype(BF16)
            dca = dya0 * bg
            dca_ext[rows, :] = dca
            racc[R_DCAB] += _fold8(dca)
            racc[R_DWA + 0] += _fold8(dca * q6)
            racc[R_DWA + 1] += _fold8(dca * q7)
            racc[R_DWA + 2] += _fold8(dca * q8)
        _chunks(tt, a2)
        _build_shifts(dca_ext, sh, (1, 2), tt)

        def a3(base):
            rows = pl.ds(base, RC)
            dq = caw[0:1, :] * _tap(dca_ext, sh, 2, base) + caw[1:2, :] * _tap(dca_ext, sh, 1, base) \
                + caw[2:3, :] * _tap(dca_ext, sh, 0, base)
            dh_ref[rows, GW:2 * GW] = (dq * _hcol(h_ref, 2, base)).astype(BF16)
            dh_ref[rows, 2 * GW:3 * GW] = (dq * _hcol(h_ref, 1, base)).astype(BF16)
        _chunks(tt, a3)
        dca_ext[tt:tt + SUBLANES, :] = dca_ext[0:SUBLANES, :]

        def b1(base):
            ub_ext[pl.ds(HALO + base, RC), :] = _hcol(h_ref, 4, base) * _sig(_hcol(h_ref, 5, base))
        _chunks(tt, b1)
        _build_shifts(ub_ext, sh, range(1, 8), tt + HALO - SUBLANES)

        def b2(base):
            rows = pl.ds(base, RC)
            cbv = [cbb[...] + jnp.zeros((RC, GW), F32)]

            def tap(off, v):
                cbv[0] = cbv[0] + cbw[off - 2:off - 1, :] * v
            _for_taps(ub_ext, sh, base, range(2, 2 + KB), tap)
            xhat, rstd = _ln_stats(cbv[0])
            lnv = xhat * lbg[...] + lbb[...]
            sl = _sig(lnv)
            s = lnv * sl
            z = _hcol(h_ref, 6, base)
            sz = _sig(z)
            szb = z * sz
            y_scr[rows, GW:2 * GW] = (s * szb).astype(BF16)
            dyb = dy_scr[rows, GW:2 * GW]
            dh_ref[rows, 6 * GW:7 * GW] = (dyb * s * _dsilu(z, sz)).astype(BF16)
            dlnv = dyb * szb * _dsilu(lnv, sl)
            racc[R_DLBG] += _fold8(dlnv * xhat)
            racc[R_DLBB] += _fold8(dlnv)
            dcb = _ln_bwd(dlnv, xhat, rstd, lbg[...])
            dcb_ext[rows, :] = dcb
            racc[R_DCBB] += _fold8(dcb)

            def wtap(off, v):
                racc[R_DWB + off - 2] += _fold8(dcb * v)
            _for_taps(ub_ext, sh, base, range(2, 2 + KB), wtap)
        _chunks(tt, b2, unroll=2)
        _build_shifts(dcb_ext, sh, range(1, 8), tt + HALO - SUBLANES)

        def b3(base):
            rows = pl.ds(base, RC)
            dubv = [jnp.zeros((RC, GW), F32)]

            def tap(off, v):
                dubv[0] = dubv[0] + cbw[KB - 1 - off:KB - off, :] * v
            _for_taps(dcb_ext, sh, base, range(KB), tap)
            dub = dubv[0]
            v, gt = _hcol(h_ref, 4, base), _hcol(h_ref, 5, base)
            sg = _sig(gt)
            dh_ref[rows, 4 * GW:5 * GW] = (dub * sg).astype(BF16)
            dh_ref[rows, 5 * GW:6 * GW] = (dub * v * sg * (1.0 - sg)).astype(BF16)
        _chunks(tt, b3)
        dcb_ext[tt:tt + HALO, :] = dcb_ext[0:HALO, :]

        def c1(base):
            cu_ext[pl.ds(16 + base, RC), :] = _hcol(h_ref, 7, base)
        _chunks(tt, c1)
        _build_shifts(cu_ext, sh, range(1, 8), tt + SUBLANES)

        def c2(base):
            ic = _inv_count(cnt, base + t0 == 0)
            for g, w in enumerate(POOL_WINDOWS):
                lanes = slice(g * PGD, (g + 1) * PGD)
                acc = _tap(cu_ext, sh, 16, base, lanes)
                for j in range(1, w):
                    acc = acc + _tap(cu_ext, sh, 16 - j, base, lanes)
                p = acc * ic[:, lanes] - _tap(cu_ext, sh, 16, base, lanes)
                p_scr[pl.ds(base, RC), lanes] = p.astype(BF16)
        _chunks(tt, c2)
        for g in range(len(POOL_WINDOWS)):
            lanes = slice(g * PGD, (g + 1) * PGD)
            pl_scr[:, lanes] = jnp.dot(p_scr[:, lanes], pw[g], preferred_element_type=F32)

        def c3(base):
            rows = pl.ds(base, RC)
            z = _hcol(h_ref, 8, base)
            sz = _sig(z)
            szc = z * sz
            plb = pl_scr[rows, :] + pb[...]
            yc0 = plb * ps[...]
            y_scr[rows, 2 * GW:3 * GW] = (yc0 * szc).astype(BF16)
            dyc = dy_scr[rows, 2 * GW:3 * GW]
            dh_ref[rows, 8 * GW:9 * GW] = (dyc * yc0 * _dsilu(z, sz)).astype(BF16)
            dyc0 = dyc * szc
            racc[R_DPS] += _fold8(dyc0 * plb)
            dpl = dyc0 * ps[...]
            racc[R_DPB] += _fold8(dpl)
            dpl_scr[rows, :] = dpl.astype(BF16)
        _chunks(tt, c3)
        for g in range(len(POOL_WINDOWS)):
            lanes = slice(g * PGD, (g + 1) * PGD)
            dpw_acc[g] += lax.dot_general(p_scr[:, lanes], dpl_scr[:, lanes], (((0,), (0,)), ((), ())),
                                          preferred_element_type=F32)
            dp_scr[:, lanes] = lax.dot_general(dpl_scr[:, lanes], pw[g], (((1,), (1,)), ((), ())),
                                               preferred_element_type=F32)

        def c4(base):
            rows = pl.ds(base, RC)
            dpn_ext[rows, :] = dp_scr[rows, :] * _inv_count(cnt, base + t0 == 0)
        _chunks(tt, c4)
        _build_shifts(dpn_ext, sh, range(1, 8), tt + SUBLANES)

        def c5(base):
            rows = pl.ds(base, RC)
            for g, w in enumerate(POOL_WINDOWS):
                lanes = slice(g * PGD, (g + 1) * PGD)
                acc = _tap(dpn_ext, sh, 0, base, lanes)
                for j in range(1, w):
                    acc = acc + _tap(dpn_ext, sh, j, base, lanes)
                dh_ref[rows, 7 * GW + g * PGD:7 * GW + (g + 1) * PGD] = (acc - dp_scr[rows, lanes]).astype(BF16)
        _chunks(tt, c5)
        dpn_ext[tt:tt + 16, :] = dpn_ext[0:16, :]

        for r in range(D_MIX // GW):
            dwout_acc[r * GW:(r + 1) * GW, :] += lax.dot_general(
                y_scr[:, r * GW:(r + 1) * GW], dzb[...], (((0,), (0,)), ((), ())), preferred_element_type=F32)
        dx_scr[...] = lax.dot_general(dh_ref[...], win_v[...], (((1,), (1,)), ((), ())), preferred_element_type=F32)

        def post(base):
            rows = pl.ds(base, RC)
            dx = ALPHA * dz_ref[rows, :] + dx_scr[rows, :]
            if has_prev:
                xhat, rstd = _ln_stats(zp_ref[rows, :])
                acc2[0] += _fold8(dx * xhat)
                acc2[1] += _fold8(dx)
                dxo_ref[rows, :] = _ln_bwd(dx, xhat, rstd, lngp[...])
            else:
                dxo_ref[rows, :] = dx
        _chunks(tt, post, unroll=2)

        @pl.when(i == n_t - 1)
        def _():
            small_ref[...] = jnp.sum(racc[...], axis=1)
            dpw_ref[...] = dpw_acc[...]
            pltpu.sync_copy(dwout_acc, dwout_hbm)
            if has_prev:
                dlnp_ref[...] = jnp.sum(acc2[...], axis=1)
            if comm is not None:
                comm["wait"](*comm_refs)

    rtile = lambda c: pl.BlockSpec((tt, c), lambda i: (n_t - 1 - i, 0))
    full = lambda a: pl.BlockSpec(a.shape, lambda i: (0,) * a.ndim)
    const = lambda shp: pl.BlockSpec(shp, lambda i: (0,) * len(shp))
    hbm = pl.BlockSpec(memory_space=pl.ANY)
    halo_spec = pl.BlockSpec((HALO, D_IN), lambda i: (jnp.maximum((n_t - 1 - i) * hb - 1, 0), 0))
    ins = [dz, h, h, win_b, wout_b, *prm] + ([z_prev, lng_prev] if has_prev else [])
    in_specs = [rtile(D_MODEL), rtile(D_IN), halo_spec, hbm, hbm] + [full(a) for a in prm] \
        + ([rtile(D_MODEL), full(lng_prev)] if has_prev else [])
    out_shape = [jax.ShapeDtypeStruct((t_len, D_MODEL), F32), jax.ShapeDtypeStruct((t_len, D_IN), BF16),
                 jax.ShapeDtypeStruct((D_MIX, D_MODEL), F32), jax.ShapeDtypeStruct((N_RACC, GW), F32),
                 jax.ShapeDtypeStruct((len(POOL_WINDOWS), PGD, PGD), F32)]
    out_specs = [rtile(D_MODEL), rtile(D_IN), hbm, const((N_RACC, GW)), const((len(POOL_WINDOWS), PGD, PGD))]
    if has_prev:
        out_shape.append(jax.ShapeDtypeStruct((2, D_MODEL), F32))
        out_specs.append(const((2, D_MODEL)))
    scratch = [
        pltpu.VMEM((D_MODEL, D_IN), BF16), pltpu.VMEM((D_MIX, D_MODEL), BF16),
        pltpu.VMEM((tt, D_MODEL), BF16), pltpu.VMEM((tt, D_MIX), F32), pltpu.VMEM((tt, D_MIX), BF16),
        pltpu.VMEM((tt, D_MODEL), F32),
        pltpu.VMEM((tt + SUBLANES, GW), F32), pltpu.VMEM((tt + HALO, GW), F32), pltpu.VMEM((tt + 16, GW), F32),
        pltpu.VMEM((tt + SUBLANES, GW), F32), pltpu.VMEM((tt + HALO, GW), F32), pltpu.VMEM((tt + 16, GW), F32),
        pltpu.VMEM((SUBLANES, tt + HALO, GW), F32),
        pltpu.VMEM((tt, GW), BF16), pltpu.VMEM((tt, GW), F32), pltpu.VMEM((tt, GW), BF16), pltpu.VMEM((tt, GW), F32),
        pltpu.VMEM((N_RACC, SUBLANES, GW), F32), pltpu.VMEM((len(POOL_WINDOWS), PGD, PGD), F32),
        pltpu.VMEM((D_MIX, D_MODEL), F32),
    ]
    if has_prev:
        scratch.append(pltpu.VMEM((2, SUBLANES, D_MODEL), F32))
    n_in, n_out = len(ins), len(out_shape)
    ins, in_specs, out_shape, out_specs, scratch = _with_comm(comm, ins, in_specs, out_shape, out_specs, scratch)
    return pl.pallas_call(
        body, name=("bwd_layer_prev" if has_prev else "bwd_layer") + ("" if comm is None else "_comm"), grid=(n_t,),
        in_specs=in_specs, out_specs=out_specs, out_shape=out_shape, scratch_shapes=scratch,
        compiler_params=pltpu.CompilerParams(dimension_semantics=("arbitrary",), vmem_limit_bytes=VMEM_LIMIT),
    )(*ins)


def _wgrad_in(xb, dh, *, tk):
    t_len = xb.shape[0]
    n_k = t_len // tk

    def body(x_ref, dh_ref, o_ref):
        @pl.when(pl.program_id(1) == 0)
        def _():
            o_ref[...] = jnp.zeros_like(o_ref)
        o_ref[0] += lax.dot_general(x_ref[...], dh_ref[...], (((0,), (0,)), ((), ())), preferred_element_type=F32)

    return pl.pallas_call(
        body, name="wgrad_in", grid=(N_CHIP, n_k),
        in_specs=[pl.BlockSpec((tk, D_MODEL), lambda j, k: (k, 0)), pl.BlockSpec((tk, SHARD_IN), lambda j, k: (k, j))],
        out_specs=pl.BlockSpec((1, D_MODEL, SHARD_IN), lambda j, k: (j, 0, 0)),
        out_shape=jax.ShapeDtypeStruct((N_CHIP, D_MODEL, SHARD_IN), F32),
        compiler_params=pltpu.CompilerParams(dimension_semantics=("arbitrary", "arbitrary"), vmem_limit_bytes=VMEM_LIMIT),
    )(xb, dh)


MESH = pl.DeviceIdType.MESH
ANY = pl.BlockSpec(memory_space=pl.ANY)


def _place():
    x, y, c = lax.axis_index("x"), lax.axis_index("y"), lax.axis_index("c")
    others = [(1 - x, y), (x, 1 - y), (1 - x, 1 - y)]
    return x, y, c, 2 * x + y, [(ox, oy, 2 * ox + oy) for ox, oy in others]


def _rcopy(src, dst, send_sems, recv_sems, k, dev):
    return pltpu.make_async_remote_copy(src_ref=src, dst_ref=dst, send_sem=send_sems.at[k], recv_sem=recv_sems.at[k],
                                        device_id=dev, device_id_type=MESH)


def _gather_weights(w_in, w_out, cw):
    hi, ho = D_MODEL // 2, SHARD_OUT // 2

    def body(win_ref, wout_ref, cw_ref, owin, owout, ocw, bin_v, bout_v, send_sems, recv_sems, lsem):
        x, y, c, me, others = _place()
        for l in range(DEPTH):
            for r0 in range(0, D_MODEL, 256):
                bin_v[l, r0:r0 + 256, :] = win_ref[l, r0:r0 + 256, :].astype(BF16)
            bout_v[l] = wout_ref[l].astype(BF16)
        cin = pl.ds(pl.multiple_of(me * SHARD_IN, 128), SHARD_IN)
        rout = pl.ds(pl.multiple_of(me * SHARD_OUT, 128), SHARD_OUT)
        local = [pltpu.make_async_copy(bin_v.at[0], owin.at[:, cin], lsem.at[0]),
                 pltpu.make_async_copy(bout_v.at[0], owout.at[rout, :], lsem.at[1]),
                 pltpu.make_async_copy(cw_ref, ocw.at[me], lsem.at[2])]
        for cp in local:
            cp.start()

        def in_half(chip, core):
            return owin.at[pl.ds(pl.multiple_of(core * hi, 256), hi), pl.ds(pl.multiple_of(chip * SHARD_IN, 128), SHARD_IN)]

        def out_half(chip, core):
            return owout.at[pl.ds(pl.multiple_of(chip * SHARD_OUT + core * ho, 64), ho), :]

        first = []
        for k, (ox, oy, _) in enumerate(others):
            dev = (ox, oy, c)
            first.append(_rcopy(bin_v.at[0, pl.ds(pl.multiple_of(c * hi, 256), hi), :], in_half(me, c), send_sems, recv_sems, k, dev))
            first.append(_rcopy(bout_v.at[0, pl.ds(pl.multiple_of(c * ho, 64), ho), :], out_half(me, c), send_sems, recv_sems, 3 + k, dev))
            first.append(_rcopy(cw_ref, ocw.at[me], send_sems, recv_sems, 6 + k, dev))
        for cp in first:
            cp.start()
        sib = (x, y, 1 - c)
        passed = []
        for k, (ox, oy, oc) in enumerate(others):
            _rcopy(in_half(oc, c), in_half(oc, c), send_sems, recv_sems, k, sib).wait_recv()
            fwd_in = _rcopy(in_half(oc, c), in_half(oc, c), send_sems, recv_sems, 9 + k, sib)
            fwd_in.start()
            _rcopy(out_half(oc, c), out_half(oc, c), send_sems, recv_sems, 3 + k, sib).wait_recv()
            fwd_out = _rcopy(out_half(oc, c), out_half(oc, c), send_sems, recv_sems, 12 + k, sib)
            fwd_out.start()
            passed += [fwd_in, fwd_out]
        for k, (ox, oy, oc) in enumerate(others):
            _rcopy(cw_ref, ocw.at[oc], send_sems, recv_sems, 6 + k, sib).wait_recv()
            _rcopy(in_half(oc, 1 - c), in_half(oc, 1 - c), send_sems, recv_sems, 9 + k, sib).wait_recv()
            _rcopy(out_half(oc, 1 - c), out_half(oc, 1 - c), send_sems, recv_sems, 12 + k, sib).wait_recv()
        for cp in first + passed:
            cp.wait_send()
        for cp in local:
            cp.wait()

    vm = pl.BlockSpec(memory_space=pltpu.VMEM)
    return pl.pallas_call(
        body, name="gather_weights",
        in_specs=[vm, vm, vm], out_specs=[ANY, ANY, ANY, vm, vm],
        out_shape=[jax.ShapeDtypeStruct((D_MODEL, D_IN), BF16), jax.ShapeDtypeStruct((D_MIX, D_MODEL), BF16),
                   jax.ShapeDtypeStruct((N_CHIP,) + cw.shape, F32),
                   jax.ShapeDtypeStruct((DEPTH, D_MODEL, SHARD_IN), BF16), jax.ShapeDtypeStruct((DEPTH, SHARD_OUT, D_MODEL), BF16)],
        scratch_shapes=[pltpu.SemaphoreType.DMA((15,)), pltpu.SemaphoreType.DMA((15,)), pltpu.SemaphoreType.DMA((3,))],
        compiler_params=pltpu.CompilerParams(vmem_limit_bytes=VMEM_LIMIT),
    )(w_in, w_out, cw)


def _gather_starts(bsh_in, bsh_out, owin, owout, send_sems, recv_sems, lsem, layer):
    x, y, c, me, others = _place()
    hi, ho = D_MODEL // 2, SHARD_OUT // 2
    pltpu.make_async_copy(bsh_in.at[layer], owin.at[:, pl.ds(pl.multiple_of(me * SHARD_IN, 128), SHARD_IN)], lsem.at[0]).start()
    pltpu.make_async_copy(bsh_out.at[layer], owout.at[pl.ds(pl.multiple_of(me * SHARD_OUT, 128), SHARD_OUT), :], lsem.at[1]).start()
    for k, (ox, oy, _) in enumerate(others):
        for t in range(2):
            pltpu.make_async_remote_copy(
                src_ref=bsh_in.at[layer, pl.ds(pl.multiple_of(c * hi, 256), hi), :],
                dst_ref=owin.at[pl.ds(pl.multiple_of(c * hi, 256), hi), pl.ds(pl.multiple_of(me * SHARD_IN, 128), SHARD_IN)],
                send_sem=send_sems.at[2 * k + t], recv_sem=recv_sems.at[2 * k + c], device_id=(ox, oy, t), device_id_type=MESH).start()
            pltpu.make_async_remote_copy(
                src_ref=bsh_out.at[layer, pl.ds(pl.multiple_of(c * ho, 64), ho), :],
                dst_ref=owout.at[pl.ds(pl.multiple_of(me * SHARD_OUT + c * ho, 64), ho), :],
                send_sem=send_sems.at[6 + 2 * k + t], recv_sem=recv_sems.at[6 + 2 * k + c], device_id=(ox, oy, t), device_id_type=MESH).start()


def _gather_waits(bsh_in, bsh_out, owin, owout, send_sems, recv_sems, lsem, layer):
    x, y, c, me, others = _place()
    hi, ho = D_MODEL // 2, SHARD_OUT // 2
    src_in = bsh_in.at[layer, pl.ds(0, hi), :]
    src_out = bsh_out.at[layer, pl.ds(0, ho), :]
    for k, (ox, oy, oc) in enumerate(others):
        for t in range(2):
            dst_in = owin.at[pl.ds(t * hi, hi), pl.ds(pl.multiple_of(oc * SHARD_IN, 128), SHARD_IN)]
            dst_out = owout.at[pl.ds(pl.multiple_of(oc * SHARD_OUT + t * ho, 64), ho), :]
            a = pltpu.make_async_remote_copy(src_ref=src_in, dst_ref=dst_in, send_sem=send_sems.at[2 * k + t],
                                             recv_sem=recv_sems.at[2 * k + t], device_id=(ox, oy, t), device_id_type=MESH)
            b = pltpu.make_async_remote_copy(src_ref=src_out, dst_ref=dst_out, send_sem=send_sems.at[6 + 2 * k + t],
                                             recv_sem=recv_sems.at[6 + 2 * k + t], device_id=(ox, oy, t), device_id_type=MESH)
            a.wait_send()
            a.wait_recv()
            b.wait_send()
            b.wait_recv()
    pltpu.make_async_copy(bsh_in.at[layer], owin.at[:, pl.ds(pl.multiple_of(me * SHARD_IN, 128), SHARD_IN)], lsem.at[0]).wait()
    pltpu.make_async_copy(bsh_out.at[layer], owout.at[pl.ds(pl.multiple_of(me * SHARD_OUT, 128), SHARD_OUT), :], lsem.at[1]).wait()


def _gather_comm(bsh_in, bsh_out, layer):
    return dict(ins=[bsh_in, bsh_out],
                out_shape=[jax.ShapeDtypeStruct((D_MODEL, D_IN), BF16), jax.ShapeDtypeStruct((D_MIX, D_MODEL), BF16)],
                sems=[pltpu.SemaphoreType.DMA((12,)), pltpu.SemaphoreType.DMA((12,)), pltpu.SemaphoreType.DMA((2,))],
                start=lambda ins, outs, sems: _gather_starts(ins[0], ins[1], outs[0], outs[1], *sems, layer),
                wait=lambda ins, outs, sems: _gather_waits(ins[0], ins[1], outs[0], outs[1], *sems, layer))


def _exchange_halves(arrs, tag):
    n = len(arrs)

    def body(*refs):
        ins, outs, (send_sems, recv_sems) = refs[:n], refs[n:2 * n], refs[2 * n:]
        x, y, c, _, _ = _place()
        cps = []
        for m in range(n):
            half = ins[m].shape[1] // 2
            cps.append(_rcopy(ins[m].at[:, pl.ds(pl.multiple_of((1 - c) * half, SUBLANES), half), :], outs[m],
                              send_sems, recv_sems, m, (x, y, 1 - c)))
        for cp in cps:
            cp.start()
        for cp in cps:
            cp.wait()

    return pl.pallas_call(
        body, name="exchange_halves_" + tag, in_specs=[ANY] * n, out_specs=[ANY] * n,
        out_shape=[jax.ShapeDtypeStruct((a.shape[0], a.shape[1] // 2, a.shape[2]), F32) for a in arrs],
        scratch_shapes=[pltpu.SemaphoreType.DMA((n,)), pltpu.SemaphoreType.DMA((n,))],
    )(*arrs)


def _add_own_half(a, got, core, *, rb, dtype):
    nj, r, cdim = a.shape
    half = r // 2

    def body(core_ref, a_ref, g_ref, o_ref):
        o_ref[...] = (a_ref[0] + g_ref[...]).astype(dtype)

    return pl.pallas_call(
        body, name="add_own_half",
        grid_spec=pltpu.PrefetchScalarGridSpec(
            num_scalar_prefetch=1, grid=(nj, half // rb),
            in_specs=[pl.BlockSpec((1, 1, rb, cdim), lambda j, i, cr: (j, cr[0], i, 0)),
                      pl.BlockSpec((1, rb, cdim), lambda j, i, cr: (j, i, 0))],
            out_specs=pl.BlockSpec((1, rb, cdim), lambda j, i, cr: (j, i, 0))),
        out_shape=jax.ShapeDtypeStruct((nj, half, cdim), dtype),
    )(core, a.reshape(nj, 2, half, cdim), got)


def _owner_starts(ins, outs, sems):
    send_sems, recv_sems, lsem = sems
    x, y, c, me, others = _place()
    for m in range(len(ins)):
        pltpu.make_async_copy(ins[m].at[me], outs[m].at[me], lsem.at[m]).start()
        for k, (ox, oy, oc) in enumerate(others):
            _rcopy(ins[m].at[oc], outs[m].at[me], send_sems, recv_sems, 3 * m + k, (ox, oy, c)).start()


def _owner_waits(ins, outs, sems):
    send_sems, recv_sems, lsem = sems
    x, y, c, me, others = _place()
    for m in range(len(ins)):
        for k, (ox, oy, oc) in enumerate(others):
            _rcopy(ins[m].at[oc], outs[m].at[oc], send_sems, recv_sems, 3 * m + k, (ox, oy, c)).wait()
        pltpu.make_async_copy(ins[m].at[me], outs[m].at[me], lsem.at[m]).wait()


def _owner_comm(arrs):
    n = len(arrs)
    return dict(ins=arrs, out_shape=[jax.ShapeDtypeStruct(a.shape, a.dtype) for a in arrs],
                sems=[pltpu.SemaphoreType.DMA((3 * n,)), pltpu.SemaphoreType.DMA((3 * n,)), pltpu.SemaphoreType.DMA((n,))],
                start=_owner_starts, wait=_owner_waits)


def _send_to_owners(arrs):
    n = len(arrs)

    def body(*refs):
        ins, outs, sems = refs[:n], refs[n:2 * n], refs[2 * n:]
        _owner_starts(ins, outs, sems)
        _owner_waits(ins, outs, sems)

    job = _owner_comm(arrs)
    return pl.pallas_call(
        body, name="send_to_owners", in_specs=[ANY] * n, out_specs=[ANY] * n,
        out_shape=job["out_shape"], scratch_shapes=job["sems"],
    )(*arrs)


def _sum_chips(a, *, rb):
    nj, r, cdim = a.shape

    def body(a_ref, o_ref):
        f = lambda k: a_ref[k].astype(F32)
        o_ref[...] = ((f(0) + f(1)) + f(2)) + f(3)

    return pl.pallas_call(
        body, name="sum_chips", grid=(r // rb,),
        in_specs=[pl.BlockSpec((nj, rb, cdim), lambda i: (0, i, 0))],
        out_specs=pl.BlockSpec((rb, cdim), lambda i: (i, 0)),
        out_shape=jax.ShapeDtypeStruct((r, cdim), F32),
    )(a)


def _sum_chips_into(a, dest, layer, core, *, rb):
    nj, half, cdim = a.shape
    nb = half // rb

    def body(*refs):
        a_ref, o_ref = refs[1], refs[-1]
        f = lambda k: a_ref[k].astype(F32)
        o_ref[0] = ((f(0) + f(1)) + f(2)) + f(3)

    grid_spec = pltpu.PrefetchScalarGridSpec(
        num_scalar_prefetch=1, grid=(nb,),
        in_specs=[pl.BlockSpec((nj, rb, cdim), lambda i, cr: (0, i, 0))] + ([] if dest is None else [ANY]),
        out_specs=pl.BlockSpec((1, rb, cdim), lambda i, cr: (layer, cr[0] * nb + i, 0)))
    return pl.pallas_call(
        body, name="sum_chips_into", grid_spec=grid_spec,
        out_shape=jax.ShapeDtypeStruct((DEPTH, 2 * half, cdim), F32),
        input_output_aliases={} if dest is None else {2: 0},
    )(*([core, a] if dest is None else [core, a, dest]))


def _spread_reduced(g_in, g_out, red_small):
    hs = red_small.shape[0]

    def body(gin_in, gout_in, sm, gin, gout, fsm, gsm, send_sems, recv_sems, lsem):
        x, y, c, me, others = _place()
        sib = (x, y, 1 - c)
        hi, ho = D_MODEL // 2, SHARD_OUT // 2
        ri, ro = pl.ds(pl.multiple_of(c * hi, SUBLANES), hi), pl.ds(pl.multiple_of(c * ho, SUBLANES), ho)
        remote = [_rcopy(gin.at[:, ri, :], gin.at[:, ri, :], send_sems, recv_sems, 0, sib),
                  _rcopy(gout.at[:, ro, :], gout.at[:, ro, :], send_sems, recv_sems, 1, sib)]
        own_small = pltpu.make_async_copy(sm, gsm.at[me], lsem.at[0])
        small = [_rcopy(sm, gsm.at[me], send_sems, recv_sems, 2 + k, (ox, oy, c)) for k, (ox, oy, _) in enumerate(others)]
        for cp in remote + [own_small] + small:
            cp.start()
        own_small.wait()
        for cp in small:
            cp.wait()
        mine = fsm.at[:, pl.ds(pl.multiple_of(c * hs, SUBLANES), hs), :]
        keep = pltpu.make_async_copy(gsm, mine, lsem.at[1])
        give = _rcopy(gsm, mine, send_sems, recv_sems, 5, sib)
        keep.start()
        give.start()
        for cp in remote + [give]:
            cp.wait()
        keep.wait()

    return pl.pallas_call(
        body, name="spread_reduced", in_specs=[ANY] * 3, out_specs=[ANY] * 4,
        out_shape=[jax.ShapeDtypeStruct(g_in.shape, F32), jax.ShapeDtypeStruct(g_out.shape, F32),
                   jax.ShapeDtypeStruct((N_CHIP, 2 * hs, GW), F32), jax.ShapeDtypeStruct((N_CHIP, hs, GW), F32)],
        input_output_aliases={0: 0, 1: 1},
        scratch_shapes=[pltpu.SemaphoreType.DMA((6,)), pltpu.SemaphoreType.DMA((6,)), pltpu.SemaphoreType.DMA((2,))],
    )(g_in, g_out, red_small)[:3]


def _adamw_math(w, g, m, v):
    m = ADAM_B1 * m + (1.0 - ADAM_B1) * g
    v = ADAM_B2 * v + (1.0 - ADAM_B2) * (g * g)
    m_hat = m / (1.0 - ADAM_B1 ** ADAM_STEP)
    v_hat = v / (1.0 - ADAM_B2 ** ADAM_STEP)
    delta = -ADAM_LR * (m_hat / (jnp.sqrt(v_hat) + ADAM_EPS) + ADAM_WD * w)
    return delta, m, v


def _adamw_big(w, g, m, v, *, rb):
    r, cdim = w.shape

    def body(w_ref, g_ref, m_ref, v_ref, d_ref, nm_ref, nv_ref):
        d_ref[...], nm_ref[...], nv_ref[...] = _adamw_math(w_ref[...], g_ref[...], m_ref[...], v_ref[...])

    spec = pl.BlockSpec((rb, cdim), lambda i: (i, 0))
    return pl.pallas_call(
        body, name="adamw_big", grid=(r // rb,), in_specs=[spec] * 4, out_specs=[spec] * 3,
        out_shape=[jax.ShapeDtypeStruct((r, cdim), F32)] * 3,
    )(w, g, m, v)


def _adamw_small(ws, gs, ms, vs):
    n = len(ws)

    def body(*refs):
        w, g, m, v = refs[:n], refs[n:2 * n], refs[2 * n:3 * n], refs[3 * n:4 * n]
        d, nm, nv = refs[4 * n:5 * n], refs[5 * n:6 * n], refs[6 * n:7 * n]
        for k in range(n):
            d[k][...], nm[k][...], nv[k][...] = _adamw_math(w[k][...], g[k][...], m[k][...], v[k][...])

    shapes = [jax.ShapeDtypeStruct(a.shape, F32) for a in ws]
    outs = pl.pallas_call(body, name="adamw_small", out_shape=shapes * 3)(*ws, *gs, *ms, *vs)
    return outs[:n], outs[n:2 * n], outs[2 * n:]


TT = 256
TK = 512
CW_ROWS = 40
PACK_ROWS = 192


def _pack(rows):
    packed = jnp.concatenate(rows, axis=0)
    packed = jnp.pad(packed, ((0, PACK_ROWS - packed.shape[0]), (0, 0)))
    return packed.reshape(N_CHIP, PACK_ROWS // N_CHIP, GW)


def _reduce_to_owner_halves(parts, core1, tag):
    got = _exchange_halves(parts, tag)
    rbs = [256, SHARD_OUT // 2, PACK_ROWS // N_CHIP // 2]
    dts = [BF16, BF16, F32]
    return [_add_own_half(a, g, core1, rb=rb, dtype=dt) for a, g, rb, dt in zip(parts, got, rbs, dts)]


def kernel(x, w_in, conv_a_w, conv_a_b, conv_b_w, conv_b_b, ln_b_g, ln_b_b, pool_w, pool_b, pool_scale, w_out, ln_g, ln_b, loss_target, m_w_in, m_conv_a_w, m_conv_a_b, m_conv_b_w, m_conv_b_b, m_ln_b_g, m_ln_b_b, m_pool_w, m_pool_b, m_pool_scale, m_w_out, m_ln_g, m_ln_b, v_w_in, v_conv_a_w, v_conv_a_b, v_conv_b_w, v_conv_b_b, v_ln_b_g, v_ln_b_b, v_pool_w, v_pool_b, v_pool_scale, v_w_out, v_ln_g, v_ln_b):
    chip = 2 * lax.axis_index("x") + lax.axis_index("y")
    core1 = lax.axis_index("c").reshape(1).astype(jnp.int32)
    x2, tgt = x[0], loss_target[0]

    cw = jnp.zeros((DEPTH, CW_ROWS, PGD), F32).at[:, 0:KA].set(conv_a_w).at[:, 8:8 + KB].set(conv_b_w)
    win0_b, wout0_b, cw_all, bsh_in, bsh_out = _gather_weights(w_in, w_out, cw)
    cw_full = jnp.transpose(cw_all, (1, 2, 0, 3)).reshape(DEPTH, CW_ROWS, GW)
    row = lambda a, l: a[l].reshape(1, -1)
    cnt = _count_table()
    prm = [(cw_full[l, 0:KA], row(conv_a_b, l), cw_full[l, 8:8 + KB], row(conv_b_b, l), row(ln_b_g, l), row(ln_b_b, l),
            pool_w[l].astype(BF16), row(pool_b, l), row(pool_scale, l), cnt) for l in range(DEPTH)]

    h0, xb0, z0, x1, win1_b, wout1_b = _fwd_layer(x2, win0_b, wout0_b, prm[0], row(ln_g, 0), row(ln_b, 0), None, tt=TT, last=False,
                                                  comm=_gather_comm(bsh_in, bsh_out, 1))
    h1, xb1, dz1, dln1, loss8 = _fwd_layer(x1, win1_b, wout1_b, prm[1], row(ln_g, 1), row(ln_b, 1), tgt, tt=TT, last=True)

    dz0, dh1, dwout1, small1, dpw1, dln0 = _bwd_layer(dz1, h1, win1_b, wout1_b, prm[1], z0, row(ln_g, 0), tt=TT)
    dwin1 = _wgrad_in(xb1, dh1, tk=TK)
    pack1 = _pack([small1, dpw1.reshape(PGD, GW), dln1.reshape(4, GW), dln0.reshape(4, GW)])
    sums1 = _reduce_to_owner_halves([dwin1, dwout1.reshape(N_CHIP, SHARD_OUT, D_MODEL), pack1], core1, "1")
    gx, dh0, dwout0, small0, dpw0, *landed1 = _bwd_layer(dz0, h0, win0_b, wout0_b, prm[0], None, None, tt=TT, comm=_owner_comm(sums1))
    dwin0 = _wgrad_in(xb0, dh0, tk=TK)
    pack0 = _pack([small0, dpw0.reshape(PGD, GW)])
    sums0 = _reduce_to_owner_halves([dwin0, dwout0.reshape(N_CHIP, SHARD_OUT, D_MODEL), pack0], core1, "0")
    landed0 = _send_to_owners(sums0)

    g_in = _sum_chips_into(landed0[0], _sum_chips_into(landed1[0], None, 1, core1, rb=256), 0, core1, rb=256)
    g_out = _sum_chips_into(landed0[1], _sum_chips_into(landed1[1], None, 1, core1, rb=SHARD_OUT // 2), 0, core1, rb=SHARD_OUT // 2)
    red_small = jnp.concatenate([_sum_chips(a, rb=PACK_ROWS // N_CHIP // 2) for a in (landed0[2], landed1[2])], axis=0)
    g_in, g_out, g_small = _spread_reduced(g_in, g_out, red_small)

    flat = lambda a: a.reshape(-1, a.shape[-1])
    unflat = lambda a, like: a.reshape(like.shape)
    d_in, nm_in, nv_in = [unflat(a, w_in) for a in _adamw_big(flat(w_in), flat(g_in), flat(m_w_in), flat(v_w_in), rb=256)]
    d_out, nm_out, nv_out = [unflat(a, w_out) for a in _adamw_big(flat(w_out), flat(g_out), flat(m_w_out), flat(v_w_out), rb=SHARD_OUT)]

    hp = PACK_ROWS // N_CHIP // 2
    unpack = lambda o: jnp.concatenate([g_small[:, o:o + hp], g_small[:, 2 * hp + o:3 * hp + o]], axis=1).reshape(PACK_ROWS, GW)
    p0, p1 = unpack(0), unpack(hp)
    small = [p0[0:N_RACC], p1[0:N_RACC]]
    dpw = [p[N_RACC:N_RACC + PGD].reshape(len(POOL_WINDOWS), PGD, PGD) for p in (p0, p1)]
    o = N_RACC + PGD
    g_lng = jnp.stack([p1[o + 4:o + 8].reshape(2, D_MODEL)[0], p1[o:o + 4].reshape(2, D_MODEL)[0]])
    g_lnb = jnp.stack([p1[o + 4:o + 8].reshape(2, D_MODEL)[1], p1[o:o + 4].reshape(2, D_MODEL)[1]])
    mine = lambda a: lax.dynamic_slice_in_dim(a, chip * PGD, PGD, axis=-1)
    stack = lambda f: jnp.stack([f(0), f(1)])
    g_caw = stack(lambda l: mine(small[l][R_DWA:R_DWA + KA]))
    g_cab = stack(lambda l: small[l][R_DCAB])
    g_cbw = stack(lambda l: mine(small[l][R_DWB:R_DWB + KB]))
    g_cbb = stack(lambda l: small[l][R_DCBB])
    g_lbg = stack(lambda l: small[l][R_DLBG])
    g_lbb = stack(lambda l: small[l][R_DLBB])
    g_pw = stack(lambda l: dpw[l])
    g_pb = stack(lambda l: small[l][R_DPB].reshape(len(POOL_WINDOWS), PGD))
    g_ps = stack(lambda l: small[l][R_DPS])
    ws = [conv_a_w, conv_a_b, conv_b_w, conv_b_b, ln_b_g, ln_b_b, pool_w, pool_b, pool_scale, ln_g, ln_b]
    gs = [g_caw, g_cab, g_cbw, g_cbb, g_lbg, g_lbb, g_pw, g_pb, g_ps, g_lng, g_lnb]
    ms = [m_conv_a_w, m_conv_a_b, m_conv_b_w, m_conv_b_b, m_ln_b_g, m_ln_b_b, m_pool_w, m_pool_b, m_pool_scale, m_ln_g, m_ln_b]
    vs = [v_conv_a_w, v_conv_a_b, v_conv_b_w, v_conv_b_b, v_ln_b_g, v_ln_b_b, v_pool_w, v_pool_b, v_pool_scale, v_ln_g, v_ln_b]
    ds, nms, nvs = _adamw_small([flat(a) for a in ws], [flat(a) for a in gs], [flat(a) for a in ms], [flat(a) for a in vs])
    ds, nms, nvs = ([unflat(a, w) for a, w in zip(t, ws)] for t in (ds, nms, nvs))

    loss = lax.psum(loss8[0, 0], ("x", "y", "c"))

    def order(in_, small_, out_):
        return [in_, *small_[:9], out_, *small_[9:]]
    return (loss, gx[None], *order(g_in, gs, g_out), *order(d_in, ds, d_out), *order(nm_in, nms, nm_out), *order(nv_in, nvs, nv_out))
```

```python
import functools

import jax
import jax.numpy as jnp
import numpy as np
from jax import lax
from jax.experimental import pallas as pl
from jax.experimental.pallas import tpu as pltpu

F32 = jnp.float32
BF16 = jnp.bfloat16

D_MODEL = 1024
DEPTH = 2
GW = 512
D_IN = 9 * GW
D_MIX = 3 * GW
NG = D_IN // GW
POOL_WINDOWS = (2, 4, 8, 16)
PGD = 128
KA = 3
KB = 31
ALPHA = (2.0 * DEPTH) ** 0.25
LN_EPS = 1e-5
ADAM_LR, ADAM_B1, ADAM_B2, ADAM_EPS, ADAM_WD, ADAM_STEP = 0.001, 0.9, 0.999, 1e-08, 0.01, 10

N_CHIP = 4
SHARD_IN = D_IN // N_CHIP
SHARD_OUT = D_MIX // N_CHIP

SUBLANES = 8
RC = 32
HALO = 32
VMEM_LIMIT = 60 * 1024 * 1024

R_DWA, R_DCAB, R_DWB, R_DCBB, R_DLBG, R_DLBB, R_DPB, R_DPS, N_RACC = 0, 3, 4, 35, 36, 37, 38, 39, 40


def _sig(v):
    return 0.5 * jnp.tanh(0.5 * v) + 0.5


def _chunks(n_rows, fn, unroll=1, extra=None):
    def step(m, carry):
        for u in range(unroll):
            fn(pl.multiple_of((m * unroll + u) * RC, RC))
        if extra is not None:
            extra(m)
        return carry
    lax.fori_loop(0, n_rows // (RC * unroll), step, 0)


def _fold8(v):
    return v.reshape(RC // SUBLANES, SUBLANES, v.shape[-1]).sum(axis=0)


def _build_shifts(ext_ref, sh_ref, shifts, n_rows):
    for r in shifts:
        for c0 in range(0, n_rows, RC):
            n = min(RC, n_rows - c0)
            sh_ref[r, pl.ds(c0, n), :] = ext_ref[pl.ds(c0 + r, n), :]


def _tap(ext_ref, sh_ref, off, base, lanes=None):
    a, r = divmod(off, SUBLANES)
    src = ext_ref if r == 0 else sh_ref.at[r]
    if lanes is None:
        return src[pl.ds(base + SUBLANES * a, RC), :]
    return src[pl.ds(base + SUBLANES * a, RC), lanes]


def _ln_stats(v):
    mu = jnp.mean(v, axis=-1, keepdims=True)
    vc = v - mu
    var = jnp.mean(vc * vc, axis=-1, keepdims=True)
    rstd = lax.rsqrt(var + LN_EPS)
    return vc * rstd, rstd


def _ln_bwd(dy, xhat, rstd, g):
    dxh = dy * g
    m1 = jnp.mean(dxh, axis=-1, keepdims=True)
    m2 = jnp.mean(dxh * xhat, axis=-1, keepdims=True)
    return rstd * (dxh - m1 - xhat * m2)


def _for_taps(ext_ref, sh_ref, base, offsets, fn):
    for r in range(SUBLANES):
        offs = [o for o in offsets if o % SUBLANES == r]
        if not offs:
            continue
        a0, a1 = min(offs) // SUBLANES, max(offs) // SUBLANES
        src = ext_ref if r == 0 else sh_ref.at[r]
        win = src[pl.ds(base + SUBLANES * a0, RC + SUBLANES * (a1 - a0)), :]
        for o in offs:
            a = o // SUBLANES - a0
            fn(o, win[SUBLANES * a:SUBLANES * a + RC])


def _count_table():
    t = np.arange(1, RC + 1, dtype=np.float64)[:, None]
    w = np.repeat(np.asarray(POOL_WINDOWS, np.float64), PGD)[None, :]
    return jnp.asarray(1.0 / np.minimum(t, w), F32)


def _inv_count(cnt_ref, first):
    return jnp.where(first, cnt_ref[...], cnt_ref[RC - 1:RC, :])


def _hcol(h_ref, j, base):
    if len(h_ref.shape) == 3:
        return h_ref[j, pl.ds(base, RC), :].astype(F32)
    return h_ref[pl.ds(base, RC), j * GW:(j + 1) * GW].astype(F32)


def _with_comm(comm, ins, in_specs, out_shape, out_specs, scratch):
    if comm is None:
        return ins, in_specs, out_shape, out_specs, scratch
    hbm = pl.BlockSpec(memory_space=pl.ANY)
    return (ins + list(comm["ins"]), in_specs + [hbm] * len(comm["ins"]), out_shape + list(comm["out_shape"]),
            out_specs + [hbm] * len(comm["out_shape"]), scratch + list(comm["sems"]))


def _split_comm(refs, comm, n_in, n_out):
    refs = list(refs)
    if comm is None:
        return refs, None
    ci, co, cs = len(comm["ins"]), len(comm["out_shape"]), len(comm["sems"])
    own = refs[:n_in] + refs[n_in + ci:n_in + ci + n_out] + refs[n_in + ci + n_out + co:len(refs) - cs]
    return own, (refs[n_in:n_in + ci], refs[n_in + ci + n_out:n_in + ci + n_out + co], refs[len(refs) - cs:])


def _fwd_mixers(h_ref, y_scr, q_ext, ub_ext, cu_ext, sh, p_scr, pl_scr, prm, tt, t0, extra_b2=None):
    caw, cab, cbw, cbb, lbg, lbb, pw, pb, ps, cnt = prm

    def a1(base):
        q_ext[pl.ds(SUBLANES + base, RC), :] = _hcol(h_ref, 1, base) * _hcol(h_ref, 2, base)
    _chunks(tt, a1)
    _build_shifts(q_ext, sh, (6, 7), tt)

    def a2(base):
        ca = cab[...] + caw[0:1, :] * _tap(q_ext, sh, 6, base) + caw[1:2, :] * _tap(q_ext, sh, 7, base) \
            + caw[2:3, :] * _tap(q_ext, sh, 8, base)
        z = _hcol(h_ref, 3, base)
        y_scr[pl.ds(base, RC), 0:GW] = (_hcol(h_ref, 0, base) * ca * (z * _sig(z))).astype(BF16)
    _chunks(tt, a2, unroll=2)
    q_ext[0:SUBLANES, :] = q_ext[tt:tt + SUBLANES, :]

    def b1(base):
        ub_ext[pl.ds(HALO + base, RC), :] = _hcol(h_ref, 4, base) * _sig(_hcol(h_ref, 5, base))
    _chunks(tt, b1)
    _build_shifts(ub_ext, sh, range(1, 8), tt + HALO - SUBLANES)

    def b2(base):
        cb = [cbb[...] + jnp.zeros((RC, GW), F32)]

        def tap(off, v):
            cb[0] = cb[0] + cbw[off - 2:off - 1, :] * v
        _for_taps(ub_ext, sh, base, range(2, 2 + KB), tap)
        xhat, _ = _ln_stats(cb[0])
        lnv = xhat * lbg[...] + lbb[...]
        z = _hcol(h_ref, 6, base)
        y_scr[pl.ds(base, RC), GW:2 * GW] = (lnv * _sig(lnv) * (z * _sig(z))).astype(BF16)
    _chunks(tt, b2, unroll=2, extra=extra_b2)
    ub_ext[0:HALO, :] = ub_ext[tt:tt + HALO, :]

    def c1(base):
        cu_ext[pl.ds(16 + base, RC), :] = _hcol(h_ref, 7, base)
    _chunks(tt, c1)
    _build_shifts(cu_ext, sh, range(1, 8), tt + SUBLANES)

    def c2(base):
        ic = _inv_count(cnt, base + t0 == 0)
        for g, w in enumerate(POOL_WINDOWS):
            lanes = slice(g * PGD, (g + 1) * PGD)
            acc = _tap(cu_ext, sh, 16, base, lanes)
            for j in range(1, w):
                acc = acc + _tap(cu_ext, sh, 16 - j, base, lanes)
            p = acc * ic[:, lanes] - _tap(cu_ext, sh, 16, base, lanes)
            p_scr[pl.ds(base, RC), lanes] = p.astype(BF16)
    _chunks(tt, c2)
    cu_ext[0:16, :] = cu_ext[tt:tt + 16, :]
    for g in range(len(POOL_WINDOWS)):
        lanes = slice(g * PGD, (g + 1) * PGD)
        pl_scr[:, lanes] = jnp.dot(p_scr[:, lanes], pw[g], preferred_element_type=F32)

    def c3(base):
        z = _hcol(h_ref, 8, base)
        yc0 = (pl_scr[pl.ds(base, RC), :] + pb[...]) * ps[...]
        y_scr[pl.ds(base, RC), 2 * GW:3 * GW] = (yc0 * (z * _sig(z))).astype(BF16)
    _chunks(tt, c3, unroll=2)


def _fwd_layer(x, win_b, wout_b, prm, ln_g, ln_b, target, *, tt, last, comm=None):
    t_len = x.shape[0]
    n_t = t_len // tt

    def body(*refs):
        refs, comm_refs = _split_comm(refs, comm, n_in, n_out)
        if last:
            (x_ref, xnext_ref, win_hbm, wout_hbm, caw, cab, cbw, cbb, lbg, lbb, pw, pb, ps, cnt, lng, lnb, tgt_ref,
             h_ref, xb_ref, dz_ref, dln_ref, loss_ref,
             win_v, wout_v, hb, xbn, y_scr, o_scr, q_ext, ub_ext, cu_ext, sh, p_scr, pl_scr, acc2, lacc) = refs
        else:
            (x_ref, xnext_ref, win_hbm, wout_hbm, caw, cab, cbw, cbb, lbg, lbb, pw, pb, ps, cnt, lng, lnb,
             h_ref, xb_ref, z_ref, xn_ref,
             win_v, wout_v, hb, xbn, y_scr, o_scr, q_ext, ub_ext, cu_ext, sh, p_scr, pl_scr) = refs
        i = pl.program_id(0)
        cur = i % 2
        h_cur, h_nxt = hb.at[cur], hb.at[1 - cur]

        def piece(j):
            h_nxt[j] = jnp.dot(xbn[...], win_v[j], preferred_element_type=F32).astype(BF16)

        n_iter = tt // (2 * RC)
        per_iter = 4 // n_iter

        def pieces(first):
            def run(m):
                for u in range(per_iter):
                    piece(first + m * per_iter + u)
            return run

        @pl.when(i == 0)
        def _():
            if comm is not None:
                comm["start"](*comm_refs)
            for j in range(NG):
                pltpu.sync_copy(win_hbm.at[:, j * GW:(j + 1) * GW], win_v.at[j])
            pltpu.sync_copy(wout_hbm, wout_v)
            xbn[...] = x_ref[...].astype(BF16)
            for j in range(NG):
                hb[0, j] = jnp.dot(xbn[...], win_v[j], preferred_element_type=F32).astype(BF16)
            q_ext[0:SUBLANES, :] = jnp.zeros((SUBLANES, GW), F32)
            ub_ext[0:HALO, :] = jnp.zeros((HALO, GW), F32)
            cu_ext[0:16, :] = jnp.zeros((16, GW), F32)
            if last:
                acc2[...] = jnp.zeros_like(acc2)
                lacc[...] = jnp.zeros_like(lacc)

        xb_ref[...] = x_ref[...].astype(BF16)
        xbn[...] = xnext_ref[...].astype(BF16)
        for j in range(NG):
            h_ref[:, j * GW:(j + 1) * GW] = h_cur[j]

        _fwd_mixers(h_cur, y_scr, q_ext, ub_ext, cu_ext, sh, p_scr, pl_scr,
                    (caw, cab, cbw, cbb, lbg, lbb, pw, pb, ps, cnt), tt, i * tt, extra_b2=pieces(0))

        piece(NG - 1)
        o_scr[...] = jnp.dot(y_scr[...], wout_v[...], preferred_element_type=F32)

        def post(base):
            rows = pl.ds(base, RC)
            z = ALPHA * x_ref[rows, :] + o_scr[rows, :]
            xhat, rstd = _ln_stats(z)
            xn = xhat * lng[...] + lnb[...]
            if last:
                err = xn - tgt_ref[rows, :]
                lacc[...] += _fold8(err * err)
                dxn = err * (1.0 / D_MODEL)
                acc2[0] += _fold8(dxn * xhat)
                acc2[1] += _fold8(dxn)
                dz_ref[rows, :] = _ln_bwd(dxn, xhat, rstd, lng[...])
            else:
                z_ref[rows, :] = z
                xn_ref[rows, :] = xn
        _chunks(tt, post, unroll=2, extra=pieces(4))

        if last:
            @pl.when(i == n_t - 1)
            def _():
                dln_ref[...] = jnp.sum(acc2[...], axis=1)
                loss_ref[...] = jnp.zeros((SUBLANES, 128), F32) + (0.5 / D_MODEL) * jnp.sum(lacc[...])
        if comm is not None:
            @pl.when(i == n_t - 1)
            def _():
                comm["wait"](*comm_refs)

    tile = lambda c: pl.BlockSpec((tt, c), lambda i: (i, 0))
    full = lambda a: pl.BlockSpec(a.shape, lambda i: (0,) * a.ndim)
    hbm = pl.BlockSpec(memory_space=pl.ANY)
    ins = [x, x, win_b, wout_b, *prm, ln_g, ln_b] + ([target] if last else [])
    nxt = pl.BlockSpec((tt, D_MODEL), lambda i: (jnp.minimum(i + 1, n_t - 1), 0))
    in_specs = [tile(D_MODEL), nxt, hbm, hbm] + [full(a) for a in (*prm, ln_g, ln_b)] + ([tile(D_MODEL)] if last else [])
    out_shape = [jax.ShapeDtypeStruct((t_len, D_IN), BF16), jax.ShapeDtypeStruct((t_len, D_MODEL), BF16)]
    out_specs = [tile(D_IN), tile(D_MODEL)]
    if last:
        out_shape += [jax.ShapeDtypeStruct((t_len, D_MODEL), F32), jax.ShapeDtypeStruct((2, D_MODEL), F32),
                      jax.ShapeDtypeStruct((SUBLANES, 128), F32)]
        out_specs += [tile(D_MODEL), pl.BlockSpec((2, D_MODEL), lambda i: (0, 0)),
                      pl.BlockSpec((SUBLANES, 128), lambda i: (0, 0))]
    else:
        out_shape += [jax.ShapeDtypeStruct((t_len, D_MODEL), F32), jax.ShapeDtypeStruct((t_len, D_MODEL), F32)]
        out_specs += [tile(D_MODEL), tile(D_MODEL)]
    scratch = [
        pltpu.VMEM((NG, D_MODEL, GW), BF16), pltpu.VMEM((D_MIX, D_MODEL), BF16),
        pltpu.VMEM((2, NG, tt, GW), BF16), pltpu.VMEM((tt, D_MODEL), BF16),
        pltpu.VMEM((tt, D_MIX), BF16), pltpu.VMEM((tt, D_MODEL), F32),
        pltpu.VMEM((tt + SUBLANES, GW), F32), pltpu.VMEM((tt + HALO, GW), F32), pltpu.VMEM((tt + 16, GW), F32),
        pltpu.VMEM((SUBLANES, tt + HALO, GW), F32),
        pltpu.VMEM((tt, GW), BF16), pltpu.VMEM((tt, GW), F32),
    ]
    if last:
        scratch += [pltpu.VMEM((2, SUBLANES, D_MODEL), F32), pltpu.VMEM((SUBLANES, D_MODEL), F32)]
    n_in, n_out = len(ins), len(out_shape)
    ins, in_specs, out_shape, out_specs, scratch = _with_comm(comm, ins, in_specs, out_shape, out_specs, scratch)
    return pl.pallas_call(
        body, name=("fwd_last" if last else "fwd_layer") + ("" if comm is None else "_comm"), grid=(n_t,),
        in_specs=in_specs, out_specs=out_specs, out_shape=out_shape, scratch_shapes=scratch,
        compiler_params=pltpu.CompilerParams(dimension_semantics=("arbitrary",), vmem_limit_bytes=VMEM_LIMIT),
    )(*ins)


def _dsilu(z, sz):
    return sz * (1.0 + z * (1.0 - sz))


def _bwd_layer(dz, h, win_b, wout_b, prm, z_prev, lng_prev, *, tt, comm=None):
    t_len = dz.shape[0]
    n_t = t_len // tt
    has_prev = z_prev is not None
    hb = tt // HALO

    def body(*refs):
        refs, comm_refs = _split_comm(refs, comm, n_in, n_out)
        dz_ref, h_ref, halo_ref, win_hbm, wout_hbm, caw, cab, cbw, cbb, lbg, lbb, pw, pb, ps, cnt = refs[:15]
        k = 15
        if has_prev:
            zp_ref, lngp = refs[k:k + 2]
            k += 2
        dxo_ref, dh_ref, dwout_hbm, small_ref, dpw_ref = refs[k:k + 5]
        k += 5
        if has_prev:
            dlnp_ref = refs[k]
            k += 1
        (win_v, wout_v, dzb, dy_scr, y_scr, dx_scr, q_ext, ub_ext, cu_ext, dca_ext, dcb_ext, dpn_ext, sh,
         p_scr, pl_scr, dpl_scr, dp_scr, racc, dpw_acc, dwout_acc) = refs[k:k + 20]
        k += 20
        if has_prev:
            acc2 = refs[k]
        i = pl.program_id(0)
        ti = n_t - 1 - i
        t0 = ti * tt

        @pl.when(i == 0)
        def _():
            if comm is not None:
                comm["start"](*comm_refs)
            pltpu.sync_copy(win_hbm, win_v)
            pltpu.sync_copy(wout_hbm, wout_v)
            dca_ext[tt:tt + SUBLANES, :] = jnp.zeros((SUBLANES, GW), F32)
            dcb_ext[tt:tt + HALO, :] = jnp.zeros((HALO, GW), F32)
            dpn_ext[tt:tt + 16, :] = jnp.zeros((16, GW), F32)
            racc[...] = jnp.zeros_like(racc)
            dpw_acc[...] = jnp.zeros_like(dpw_acc)
            dwout_acc[...] = jnp.zeros_like(dwout_acc)
            if has_prev:
                acc2[...] = jnp.zeros_like(acc2)

        dzb[...] = dz_ref[...].astype(BF16)
        dy_scr[...] = lax.dot_general(dzb[...], wout_v[...], (((1,), (1,)), ((), ())), preferred_element_type=F32)

        live = (ti > 0).astype(F32)
        hh = lambda j, r0, r1: halo_ref[r0:r1, j * GW:(j + 1) * GW].astype(F32)
        q_ext[0:SUBLANES, :] = live * hh(1, 24, 32) * hh(2, 24, 32)
        ub_ext[0:HALO, :] = live * hh(4, 0, 32) * _sig(hh(5, 0, 32))
        cu_ext[0:16, :] = live * hh(7, 16, 32)

        def a1(base):
            q_ext[pl.ds(SUBLANES + base, RC), :] = _hcol(h_ref, 1, base) * _hcol(h_ref, 2, base)
        _chunks(tt, a1)
        _build_shifts(q_ext, sh, (6, 7), tt)

        def a2(base):
            rows = pl.ds(base, RC)
            q6, q7, q8 = _tap(q_ext, sh, 6, base), _tap(q_ext, sh, 7, base), _tap(q_ext, sh, 8, base)
            ca = cab[...] + caw[0:1, :] * q6 + caw[1:2, :] * q7 + caw[2:3, :] * q8
            bg, z = _hcol(h_ref, 0, base), _hcol(h_ref, 3, base)
            sz = _sig(z)
            sza = z * sz
            dya = dy_scr[rows, 0:GW]
            ya0 = bg * ca
            y_scr[rows, 0:GW] = (ya0 * sza).astype(BF16)
            dya0 = dya * sza
            dh_ref[rows, 3 * GW:4 * GW] = (dya * ya0 * _dsilu(z, sz)).astype(BF16)
            dh_ref[rows, 0:GW] = (dya0 * ca).astype(BF16)
            dca = dya0 * bg
            dca_ext[rows, :] = dca
            racc[R_DCAB] += _fold8(dca)
            racc[R_DWA + 0] += _fold8(dca * q6)
            racc[R_DWA + 1] += _fold8(dca * q7)
            racc[R_DWA + 2] += _fold8(dca * q8)
        _chunks(tt, a2)
        _build_shifts(dca_ext, sh, (1, 2), tt)

        def a3(base):
            rows = pl.ds(base, RC)
            dq = caw[0:1, :] * _tap(dca_ext, sh, 2, base) + caw[1:2, :] * _tap(dca_ext, sh, 1, base) \
                + caw[2:3, :] * _tap(dca_ext, sh, 0, base)
            dh_ref[rows, GW:2 * GW] = (dq * _hcol(h_ref, 2, base)).astype(BF16)
            dh_ref[rows, 2 * GW:3 * GW] = (dq * _hcol(h_ref, 1, base)).astype(BF16)
        _chunks(tt, a3)
        dca_ext[tt:tt + SUBLANES, :] = dca_ext[0:SUBLANES, :]

        def b1(base):
            ub_ext[pl.ds(HALO + base, RC), :] = _hcol(h_ref, 4, base) * _sig(_hcol(h_ref, 5, base))
        _chunks(tt, b1)
        _build_shifts(ub_ext, sh, range(1, 8), tt + HALO - SUBLANES)

        def b2(base):
            rows = pl.ds(base, RC)
            cbv = [cbb[...] + jnp.zeros((RC, GW), F32)]

            def tap(off, v):
                cbv[0] = cbv[0] + cbw[off - 2:off - 1, :] * v
            _for_taps(ub_ext, sh, base, range(2, 2 + KB), tap)
            xhat, rstd = _ln_stats(cbv[0])
            lnv = xhat * lbg[...] + lbb[...]
            sl = _sig(lnv)
            s = lnv * sl
            z = _hcol(h_ref, 6, base)
            sz = _sig(z)
            szb = z * sz
            y_scr[rows, GW:2 * GW] = (s * szb).astype(BF16)
            dyb = dy_scr[rows, GW:2 * GW]
            dh_ref[rows, 6 * GW:7 * GW] = (dyb * s * _dsilu(z, sz)).astype(BF16)
            dlnv = dyb * szb * _dsilu(lnv, sl)
            racc[R_DLBG] += _fold8(dlnv * xhat)
            racc[R_DLBB] += _fold8(dlnv)
            dcb = _ln_bwd(dlnv, xhat, rstd, lbg[...])
            dcb_ext[rows, :] = dcb
            racc[R_DCBB] += _fold8(dcb)

            def wtap(off, v):
                racc[R_DWB + off - 2] += _fold8(dcb * v)
            _for_taps(ub_ext, sh, base, range(2, 2 + KB), wtap)
        _chunks(tt, b2, unroll=2)
        _build_shifts(dcb_ext, sh, range(1, 8), tt + HALO - SUBLANES)

        def b3(base):
            rows = pl.ds(base, RC)
            dubv = [jnp.zeros((RC, GW), F32)]

            def tap(off, v):
                dubv[0] = dubv[0] + cbw[KB - 1 - off:KB - off, :] * v
            _for_taps(dcb_ext, sh, base, range(KB), tap)
            dub = dubv[0]
            v, gt = _hcol(h_ref, 4, base), _hcol(h_ref, 5, base)
            sg = _sig(gt)
            dh_ref[rows, 4 * GW:5 * GW] = (dub * sg).astype(BF16)
            dh_ref[rows, 5 * GW:6 * GW] = (dub * v * sg * (1.0 - sg)).astype(BF16)
        _chunks(tt, b3)
        dcb_ext[tt:tt + HALO, :] = dcb_ext[0:HALO, :]

        def c1(base):
            cu_ext[pl.ds(16 + base, RC), :] = _hcol(h_ref, 7, base)
        _chunks(tt, c1)
        _build_shifts(cu_ext, sh, range(1, 8), tt + SUBLANES)

        def c2(base):
            ic = _inv_count(cnt, base + t0 == 0)
            for g, w in enumerate(POOL_WINDOWS):
                lanes = slice(g * PGD, (g + 1) * PGD)
                acc = _tap(cu_ext, sh, 16, base, lanes)
                for j in range(1, w):
                    acc = acc + _tap(cu_ext, sh, 16 - j, base, lanes)
                p = acc * ic[:, lanes] - _tap(cu_ext, sh, 16, base, lanes)
                p_scr[pl.ds(base, RC), lanes] = p.astype(BF16)
        _chunks(tt, c2)
        for g in range(len(POOL_WINDOWS)):
            lanes = slice(g * PGD, (g + 1) * PGD)
            pl_scr[:, lanes] = jnp.dot(p_scr[:, lanes], pw[g], preferred_element_type=F32)

        def c3(base):
            rows = pl.ds(base, RC)
            z = _hcol(h_ref, 8, base)
            sz = _sig(z)
            szc = z * sz
            plb = pl_scr[rows, :] + pb[...]
            yc0 = plb * ps[...]
            y_scr[rows, 2 * GW:3 * GW] = (yc0 * szc).astype(BF16)
            dyc = dy_scr[rows, 2 * GW:3 * GW]
            dh_ref[rows, 8 * GW:9 * GW] = (dyc * yc0 * _dsilu(z, sz)).astype(BF16)
            dyc0 = dyc * szc
            racc[R_DPS] += _fold8(dyc0 * plb)
            dpl = dyc0 * ps[...]
            racc[R_DPB] += _fold8(dpl)
            dpl_scr[rows, :] = dpl.astype(BF16)
        _chunks(tt, c3)
        for g in range(len(POOL_WINDOWS)):
            lanes = slice(g * PGD, (g + 1) * PGD)
            dpw_acc[g] += lax.dot_general(p_scr[:, lanes], dpl_scr[:, lanes], (((0,), (0,)), ((), ())),
                                          preferred_element_type=F32)
            dp_scr[:, lanes] = lax.dot_general(dpl_scr[:, lanes], pw[g], (((1,), (1,)), ((), ())),
                                               preferred_element_type=F32)

        def c4(base):
            rows = pl.ds(base, RC)
            dpn_ext[rows, :] = dp_scr[rows, :] * _inv_count(cnt, base + t0 == 0)
        _chunks(tt, c4)
        _build_shifts(dpn_ext, sh, range(1, 8), tt + SUBLANES)

        def c5(base):
            rows = pl.ds(base, RC)
            for g, w in enumerate(POOL_WINDOWS):
                lanes = slice(g * PGD, (g + 1) * PGD)
                acc = _tap(dpn_ext, sh, 0, base, lanes)
                for j in range(1, w):
                    acc = acc + _tap(dpn_ext, sh, j, base, lanes)
                dh_ref[rows, 7 * GW + g * PGD:7 * GW + (g + 1) * PGD] = (acc - dp_scr[rows, lanes]).astype(BF16)
        _chunks(tt, c5)
        dpn_ext[tt:tt + 16, :] = dpn_ext[0:16, :]

        for r in range(D_MIX // GW):
            dwout_acc[r * GW:(r + 1) * GW, :] += lax.dot_general(
                y_scr[:, r * GW:(r + 1) * GW], dzb[...], (((0,), (0,)), ((), ())), preferred_element_type=F32)
        dx_scr[...] = lax.dot_general(dh_ref[...], win_v[...], (((1,), (1,)), ((), ())), preferred_element_type=F32)

        def post(base):
            rows = pl.ds(base, RC)
            dx = ALPHA * dz_ref[rows, :] + dx_scr[rows, :]
            if has_prev:
                xhat, rstd = _ln_stats(zp_ref[rows, :])
                acc2[0] += _fold8(dx * xhat)
                acc2[1] += _fold8(dx)
                dxo_ref[rows, :] = _ln_bwd(dx, xhat, rstd, lngp[...])
            else:
                dxo_ref[rows, :] = dx
        _chunks(tt, post, unroll=2)

        @pl.when(i == n_t - 1)
        def _():
            small_ref[...] = jnp.sum(racc[...], axis=1)
            dpw_ref[...] = dpw_acc[...]
            pltpu.sync_copy(dwout_acc, dwout_hbm)
            if has_prev:
                dlnp_ref[...] = jnp.sum(acc2[...], axis=1)
            if comm is not None:
                comm["wait"](*comm_refs)

    rtile = lambda c: pl.BlockSpec((tt, c), lambda i: (n_t - 1 - i, 0))
    full = lambda a: pl.BlockSpec(a.shape, lambda i: (0,) * a.ndim)
    const = lambda shp: pl.BlockSpec(shp, lambda i: (0,) * len(shp))
    hbm = pl.BlockSpec(memory_space=pl.ANY)
    halo_spec = pl.BlockSpec((HALO, D_IN), lambda i: (jnp.maximum((n_t - 1 - i) * hb - 1, 0), 0))
    ins = [dz, h, h, win_b, wout_b, *prm] + ([z_prev, lng_prev] if has_prev else [])
    in_specs = [rtile(D_MODEL), rtile(D_IN), halo_spec, hbm, hbm] + [full(a) for a in prm] \
        + ([rtile(D_MODEL), full(lng_prev)] if has_prev else [])
    out_shape = [jax.ShapeDtypeStruct((t_len, D_MODEL), F32), jax.ShapeDtypeStruct((t_len, D_IN), BF16),
                 jax.ShapeDtypeStruct((D_MIX, D_MODEL), F32), jax.ShapeDtypeStruct((N_RACC, GW), F32),
                 jax.ShapeDtypeStruct((len(POOL_WINDOWS), PGD, PGD), F32)]
    out_specs = [rtile(D_MODEL), rtile(D_IN), hbm, const((N_RACC, GW)), const((len(POOL_WINDOWS), PGD, PGD))]
    if has_prev:
        out_shape.append(jax.ShapeDtypeStruct((2, D_MODEL), F32))
        out_specs.append(const((2, D_MODEL)))
    scratch = [
        pltpu.VMEM((D_MODEL, D_IN), BF16), pltpu.VMEM((D_MIX, D_MODEL), BF16),
        pltpu.VMEM((tt, D_MODEL), BF16), pltpu.VMEM((tt, D_MIX), F32), pltpu.VMEM((tt, D_MIX), BF16),
        pltpu.VMEM((tt, D_MODEL), F32),
        pltpu.VMEM((tt + SUBLANES, GW), F32), pltpu.VMEM((tt + HALO, GW), F32), pltpu.VMEM((tt + 16, GW), F32),
        pltpu.VMEM((tt + SUBLANES, GW), F32), pltpu.VMEM((tt + HALO, GW), F32), pltpu.VMEM((tt + 16, GW), F32),
        pltpu.VMEM((SUBLANES, tt + HALO, GW), F32),
        pltpu.VMEM((tt, GW), BF16), pltpu.VMEM((tt, GW), F32), pltpu.VMEM((tt, GW), BF16), pltpu.VMEM((tt, GW), F32),
        pltpu.VMEM((N_RACC, SUBLANES, GW), F32), pltpu.VMEM((len(POOL_WINDOWS), PGD, PGD), F32),
        pltpu.VMEM((D_MIX, D_MODEL), F32),
    ]
    if has_prev:
        scratch.append(pltpu.VMEM((2, SUBLANES, D_MODEL), F32))
    n_in, n_out = len(ins), len(out_shape)
    ins, in_specs, out_shape, out_specs, scratch = _with_comm(comm, ins, in_specs, out_shape, out_specs, scratch)
    return pl.pallas_call(
        body, name=("bwd_layer_prev" if has_prev else "bwd_layer") + ("" if comm is None else "_comm"), grid=(n_t,),
        in_specs=in_specs, out_specs=out_specs, out_shape=out_shape, scratch_shapes=scratch,
        compiler_params=pltpu.CompilerParams(dimension_semantics=("arbitrary",), vmem_limit_bytes=VMEM_LIMIT),
    )(*ins)


def _wgrad_in(xb, dh, *, tk, comm=None):
    t_len = xb.shape[0]
    n_k = t_len // tk

    def body(*refs):
        (x_ref, dh_ref, o_ref), comm_refs = _split_comm(refs, comm, 2, 1)
        j, k = pl.program_id(0), pl.program_id(1)

        @pl.when(k == 0)
        def _():
            o_ref[...] = jnp.zeros_like(o_ref)
        if comm is not None:
            @pl.when((j == 0) & (k == 0))
            def _():
                comm["start"](*comm_refs)
        o_ref[0] += lax.dot_general(x_ref[...], dh_ref[...], (((0,), (0,)), ((), ())), preferred_element_type=F32)
        if comm is not None:
            @pl.when((j == N_CHIP - 1) & (k == n_k - 1))
            def _():
                comm["wait"](*comm_refs)

    ins = [xb, dh]
    in_specs = [pl.BlockSpec((tk, D_MODEL), lambda j, k: (k, 0)), pl.BlockSpec((tk, SHARD_IN), lambda j, k: (k, j))]
    out_shape = [jax.ShapeDtypeStruct((N_CHIP, D_MODEL, SHARD_IN), F32)]
    out_specs = [pl.BlockSpec((1, D_MODEL, SHARD_IN), lambda j, k: (j, 0, 0))]
    ins, in_specs, out_shape, out_specs, scratch = _with_comm(comm, ins, in_specs, out_shape, out_specs, [])
    outs = pl.pallas_call(
        body, name="wgrad_in" + ("" if comm is None else "_comm"), grid=(N_CHIP, n_k),
        in_specs=in_specs, out_specs=out_specs, out_shape=out_shape, scratch_shapes=scratch,
        compiler_params=pltpu.CompilerParams(dimension_semantics=("arbitrary", "arbitrary"), vmem_limit_bytes=VMEM_LIMIT),
    )(*ins)
    return outs[0] if comm is None else outs


MESH = pl.DeviceIdType.MESH
ANY = pl.BlockSpec(memory_space=pl.ANY)


def _place():
    x, y, c = lax.axis_index("x"), lax.axis_index("y"), lax.axis_index("c")
    others = [(1 - x, y), (x, 1 - y), (1 - x, 1 - y)]
    return x, y, c, 2 * x + y, [(ox, oy, 2 * ox + oy) for ox, oy in others]


def _rcopy(src, dst, send_sems, recv_sems, k, dev):
    return pltpu.make_async_remote_copy(src_ref=src, dst_ref=dst, send_sem=send_sems.at[k], recv_sem=recv_sems.at[k],
                                        device_id=dev, device_id_type=MESH)


def _gather_weights(w_in, w_out, cw):
    hi, ho = D_MODEL // 2, SHARD_OUT // 2

    def body(win_ref, wout_ref, cw_ref, owin, owout, ocw, bin_v, bout_v, send_sems, recv_sems, lsem):
        x, y, c, me, others = _place()
        for l in range(DEPTH):
            for r0 in range(0, D_MODEL, 256):
                bin_v[l, r0:r0 + 256, :] = win_ref[l, r0:r0 + 256, :].astype(BF16)
            bout_v[l] = wout_ref[l].astype(BF16)
        cin = pl.ds(pl.multiple_of(me * SHARD_IN, 128), SHARD_IN)
        rout = pl.ds(pl.multiple_of(me * SHARD_OUT, 128), SHARD_OUT)
        local = [pltpu.make_async_copy(bin_v.at[0], owin.at[:, cin], lsem.at[0]),
                 pltpu.make_async_copy(bout_v.at[0], owout.at[rout, :], lsem.at[1]),
                 pltpu.make_async_copy(cw_ref, ocw.at[me], lsem.at[2])]
        for cp in local:
            cp.start()

        def in_half(chip, core):
            return owin.at[pl.ds(pl.multiple_of(core * hi, 256), hi), pl.ds(pl.multiple_of(chip * SHARD_IN, 128), SHARD_IN)]

        def out_half(chip, core):
            return owout.at[pl.ds(pl.multiple_of(chip * SHARD_OUT + core * ho, 64), ho), :]

        first = []
        for k, (ox, oy, _) in enumerate(others):
            dev = (ox, oy, c)
            first.append(_rcopy(bin_v.at[0, pl.ds(pl.multiple_of(c * hi, 256), hi), :], in_half(me, c), send_sems, recv_sems, k, dev))
            first.append(_rcopy(bout_v.at[0, pl.ds(pl.multiple_of(c * ho, 64), ho), :], out_half(me, c), send_sems, recv_sems, 3 + k, dev))
            first.append(_rcopy(cw_ref, ocw.at[me], send_sems, recv_sems, 6 + k, dev))
        for cp in first:
            cp.start()
        sib = (x, y, 1 - c)
        passed = []
        for k, (ox, oy, oc) in enumerate(others):
            _rcopy(in_half(oc, c), in_half(oc, c), send_sems, recv_sems, k, sib).wait_recv()
            fwd_in = _rcopy(in_half(oc, c), in_half(oc, c), send_sems, recv_sems, 9 + k, sib)
            fwd_in.start()
            _rcopy(out_half(oc, c), out_half(oc, c), send_sems, recv_sems, 3 + k, sib).wait_recv()
            fwd_out = _rcopy(out_half(oc, c), out_half(oc, c), send_sems, recv_sems, 12 + k, sib)
            fwd_out.start()
            passed += [fwd_in, fwd_out]
        for k, (ox, oy, oc) in enumerate(others):
            _rcopy(cw_ref, ocw.at[oc], send_sems, recv_sems, 6 + k, sib).wait_recv()
            _rcopy(in_half(oc, 1 - c), in_half(oc, 1 - c), send_sems, recv_sems, 9 + k, sib).wait_recv()
            _rcopy(out_half(oc, 1 - c), out_half(oc, 1 - c), send_sems, recv_sems, 12 + k, sib).wait_recv()
        for cp in first + passed:
            cp.wait_send()
        for cp in local:
            cp.wait()

    vm = pl.BlockSpec(memory_space=pltpu.VMEM)
    return pl.pallas_call(
        body, name="gather_weights",
        in_specs=[vm, vm, vm], out_specs=[ANY, ANY, ANY, vm, vm],
        out_shape=[jax.ShapeDtypeStruct((D_MODEL, D_IN), BF16), jax.ShapeDtypeStruct((D_MIX, D_MODEL), BF16),
                   jax.ShapeDtypeStruct((N_CHIP,) + cw.shape, F32),
                   jax.ShapeDtypeStruct((DEPTH, D_MODEL, SHARD_IN), BF16), jax.ShapeDtypeStruct((DEPTH, SHARD_OUT, D_MODEL), BF16)],
        scratch_shapes=[pltpu.SemaphoreType.DMA((15,)), pltpu.SemaphoreType.DMA((15,)), pltpu.SemaphoreType.DMA((3,))],
        compiler_params=pltpu.CompilerParams(vmem_limit_bytes=VMEM_LIMIT),
    )(w_in, w_out, cw)


def _gather_starts(bsh_in, bsh_out, owin, owout, send_sems, recv_sems, lsem, layer):
    x, y, c, me, others = _place()
    hi, ho = D_MODEL // 2, SHARD_OUT // 2
    pltpu.make_async_copy(bsh_in.at[layer], owin.at[:, pl.ds(pl.multiple_of(me * SHARD_IN, 128), SHARD_IN)], lsem.at[0]).start()
    pltpu.make_async_copy(bsh_out.at[layer], owout.at[pl.ds(pl.multiple_of(me * SHARD_OUT, 128), SHARD_OUT), :], lsem.at[1]).start()
    for k, (ox, oy, _) in enumerate(others):
        for t in range(2):
            pltpu.make_async_remote_copy(
                src_ref=bsh_in.at[layer, pl.ds(pl.multiple_of(c * hi, 256), hi), :],
                dst_ref=owin.at[pl.ds(pl.multiple_of(c * hi, 256), hi), pl.ds(pl.multiple_of(me * SHARD_IN, 128), SHARD_IN)],
                send_sem=send_sems.at[2 * k + t], recv_sem=recv_sems.at[2 * k + c], device_id=(ox, oy, t), device_id_type=MESH).start()
            pltpu.make_async_remote_copy(
                src_ref=bsh_out.at[layer, pl.ds(pl.multiple_of(c * ho, 64), ho), :],
                dst_ref=owout.at[pl.ds(pl.multiple_of(me * SHARD_OUT + c * ho, 64), ho), :],
                send_sem=send_sems.at[6 + 2 * k + t], recv_sem=recv_sems.at[6 + 2 * k + c], device_id=(ox, oy, t), device_id_type=MESH).start()


def _gather_waits(bsh_in, bsh_out, owin, owout, send_sems, recv_sems, lsem, layer):
    x, y, c, me, others = _place()
    hi, ho = D_MODEL // 2, SHARD_OUT // 2
    src_in = bsh_in.at[layer, pl.ds(0, hi), :]
    src_out = bsh_out.at[layer, pl.ds(0, ho), :]
    for k, (ox, oy, oc) in enumerate(others):
        for t in range(2):
            dst_in = owin.at[pl.ds(t * hi, hi), pl.ds(pl.multiple_of(oc * SHARD_IN, 128), SHARD_IN)]
            dst_out = owout.at[pl.ds(pl.multiple_of(oc * SHARD_OUT + t * ho, 64), ho), :]
            a = pltpu.make_async_remote_copy(src_ref=src_in, dst_ref=dst_in, send_sem=send_sems.at[2 * k + t],
                                             recv_sem=recv_sems.at[2 * k + t], device_id=(ox, oy, t), device_id_type=MESH)
            b = pltpu.make_async_remote_copy(src_ref=src_out, dst_ref=dst_out, send_sem=send_sems.at[6 + 2 * k + t],
                                             recv_sem=recv_sems.at[6 + 2 * k + t], device_id=(ox, oy, t), device_id_type=MESH)
            a.wait_send()
            a.wait_recv()
            b.wait_send()
            b.wait_recv()
    pltpu.make_async_copy(bsh_in.at[layer], owin.at[:, pl.ds(pl.multiple_of(me * SHARD_IN, 128), SHARD_IN)], lsem.at[0]).wait()
    pltpu.make_async_copy(bsh_out.at[layer], owout.at[pl.ds(pl.multiple_of(me * SHARD_OUT, 128), SHARD_OUT), :], lsem.at[1]).wait()


def _gather_comm(bsh_in, bsh_out, layer):
    return dict(ins=[bsh_in, bsh_out],
                out_shape=[jax.ShapeDtypeStruct((D_MODEL, D_IN), BF16), jax.ShapeDtypeStruct((D_MIX, D_MODEL), BF16)],
                sems=[pltpu.SemaphoreType.DMA((12,)), pltpu.SemaphoreType.DMA((12,)), pltpu.SemaphoreType.DMA((2,))],
                start=lambda ins, outs, sems: _gather_starts(ins[0], ins[1], outs[0], outs[1], *sems, layer),
                wait=lambda ins, outs, sems: _gather_waits(ins[0], ins[1], outs[0], outs[1], *sems, layer))


def _exchange_halves(arrs, tag):
    n = len(arrs)

    def body(*refs):
        ins, outs, (send_sems, recv_sems) = refs[:n], refs[n:2 * n], refs[2 * n:]
        x, y, c, _, _ = _place()
        cps = []
        for m in range(n):
            half = ins[m].shape[1] // 2
            cps.append(_rcopy(ins[m].at[:, pl.ds(pl.multiple_of((1 - c) * half, SUBLANES), half), :], outs[m],
                              send_sems, recv_sems, m, (x, y, 1 - c)))
        for cp in cps:
            cp.start()
        for cp in cps:
            cp.wait()

    return pl.pallas_call(
        body, name="exchange_halves_" + tag, in_specs=[ANY] * n, out_specs=[ANY] * n,
        out_shape=[jax.ShapeDtypeStruct((a.shape[0], a.shape[1] // 2, a.shape[2]), F32) for a in arrs],
        scratch_shapes=[pltpu.SemaphoreType.DMA((n,)), pltpu.SemaphoreType.DMA((n,))],
    )(*arrs)


def _add_own_half(a, got, core, *, rb, dtype):
    nj, r, cdim = a.shape
    half = r // 2

    def body(core_ref, a_ref, g_ref, o_ref):
        o_ref[...] = (a_ref[0] + g_ref[...]).astype(dtype)

    return pl.pallas_call(
        body, name="add_own_half",
        grid_spec=pltpu.PrefetchScalarGridSpec(
            num_scalar_prefetch=1, grid=(nj, half // rb),
            in_specs=[pl.BlockSpec((1, 1, rb, cdim), lambda j, i, cr: (j, cr[0], i, 0)),
                      pl.BlockSpec((1, rb, cdim), lambda j, i, cr: (j, i, 0))],
            out_specs=pl.BlockSpec((1, rb, cdim), lambda j, i, cr: (j, i, 0))),
        out_shape=jax.ShapeDtypeStruct((nj, half, cdim), dtype),
    )(core, a.reshape(nj, 2, half, cdim), got)


def _owner_starts(ins, outs, sems):
    send_sems, recv_sems, lsem = sems
    x, y, c, me, others = _place()
    for m in range(len(ins)):
        pltpu.make_async_copy(ins[m].at[me], outs[m].at[me], lsem.at[m]).start()
        for k, (ox, oy, oc) in enumerate(others):
            _rcopy(ins[m].at[oc], outs[m].at[me], send_sems, recv_sems, 3 * m + k, (ox, oy, c)).start()


def _owner_waits(ins, outs, sems):
    send_sems, recv_sems, lsem = sems
    x, y, c, me, others = _place()
    for m in range(len(ins)):
        for k, (ox, oy, oc) in enumerate(others):
            _rcopy(ins[m].at[oc], outs[m].at[oc], send_sems, recv_sems, 3 * m + k, (ox, oy, c)).wait()
        pltpu.make_async_copy(ins[m].at[me], outs[m].at[me], lsem.at[m]).wait()


def _owner_comm(arrs):
    n = len(arrs)
    return dict(ins=arrs, out_shape=[jax.ShapeDtypeStruct(a.shape, a.dtype) for a in arrs],
                sems=[pltpu.SemaphoreType.DMA((3 * n,)), pltpu.SemaphoreType.DMA((3 * n,)), pltpu.SemaphoreType.DMA((n,))],
                start=_owner_starts, wait=_owner_waits)


def _send_to_owners(arrs):
    n = len(arrs)

    def body(*refs):
        ins, outs, sems = refs[:n], refs[n:2 * n], refs[2 * n:]
        _owner_starts(ins, outs, sems)
        _owner_waits(ins, outs, sems)

    job = _owner_comm(arrs)
    return pl.pallas_call(
        body, name="send_to_owners", in_specs=[ANY] * n, out_specs=[ANY] * n,
        out_shape=job["out_shape"], scratch_shapes=job["sems"],
    )(*arrs)


def _sum_chips(a, *, rb):
    nj, r, cdim = a.shape

    def body(a_ref, o_ref):
        f = lambda k: a_ref[k].astype(F32)
        o_ref[...] = ((f(0) + f(1)) + f(2)) + f(3)

    return pl.pallas_call(
        body, name="sum_chips", grid=(r // rb,),
        in_specs=[pl.BlockSpec((nj, rb, cdim), lambda i: (0, i, 0))],
        out_specs=pl.BlockSpec((rb, cdim), lambda i: (i, 0)),
        out_shape=jax.ShapeDtypeStruct((r, cdim), F32),
    )(a)


def _sum_chips_into(a, dest, layer, core, *, rb):
    nj, half, cdim = a.shape
    nb = half // rb

    def body(*refs):
        a_ref, o_ref = refs[1], refs[-1]
        f = lambda k: a_ref[k].astype(F32)
        o_ref[0] = ((f(0) + f(1)) + f(2)) + f(3)

    grid_spec = pltpu.PrefetchScalarGridSpec(
        num_scalar_prefetch=1, grid=(nb,),
        in_specs=[pl.BlockSpec((nj, rb, cdim), lambda i, cr: (0, i, 0))] + ([] if dest is None else [ANY]),
        out_specs=pl.BlockSpec((1, rb, cdim), lambda i, cr: (layer, cr[0] * nb + i, 0)))
    return pl.pallas_call(
        body, name="sum_chips_into", grid_spec=grid_spec,
        out_shape=jax.ShapeDtypeStruct((DEPTH, 2 * half, cdim), F32),
        input_output_aliases={} if dest is None else {2: 0},
    )(*([core, a] if dest is None else [core, a, dest]))


def _spread_reduced(g_in, g_out, red_small):
    hs = red_small.shape[0]

    def body(gin_in, gout_in, sm, gin, gout, fsm, gsm, send_sems, recv_sems, lsem):
        x, y, c, me, others = _place()
        sib = (x, y, 1 - c)
        hi, ho = D_MODEL // 2, SHARD_OUT // 2
        ri, ro = pl.ds(pl.multiple_of(c * hi, SUBLANES), hi), pl.ds(pl.multiple_of(c * ho, SUBLANES), ho)
        remote = [_rcopy(gin.at[:, ri, :], gin.at[:, ri, :], send_sems, recv_sems, 0, sib),
                  _rcopy(gout.at[:, ro, :], gout.at[:, ro, :], send_sems, recv_sems, 1, sib)]
        own_small = pltpu.make_async_copy(sm, gsm.at[me], lsem.at[0])
        small = [_rcopy(sm, gsm.at[me], send_sems, recv_sems, 2 + k, (ox, oy, c)) for k, (ox, oy, _) in enumerate(others)]
        for cp in remote + [own_small] + small:
            cp.start()
        own_small.wait()
        for cp in small:
            cp.wait()
        mine = fsm.at[:, pl.ds(pl.multiple_of(c * hs, SUBLANES), hs), :]
        keep = pltpu.make_async_copy(gsm, mine, lsem.at[1])
        give = _rcopy(gsm, mine, send_sems, recv_sems, 5, sib)
        keep.start()
        give.start()
        for cp in remote + [give]:
            cp.wait()
        keep.wait()

    return pl.pallas_call(
        body, name="spread_reduced", in_specs=[ANY] * 3, out_specs=[ANY] * 4,
        out_shape=[jax.ShapeDtypeStruct(g_in.shape, F32), jax.ShapeDtypeStruct(g_out.shape, F32),
                   jax.ShapeDtypeStruct((N_CHIP, 2 * hs, GW), F32), jax.ShapeDtypeStruct((N_CHIP, hs, GW), F32)],
        input_output_aliases={0: 0, 1: 1},
        scratch_shapes=[pltpu.SemaphoreType.DMA((6,)), pltpu.SemaphoreType.DMA((6,)), pltpu.SemaphoreType.DMA((2,))],
    )(g_in, g_out, red_small)[:3]


def _adamw_math(w, g, m, v):
    m = ADAM_B1 * m + (1.0 - ADAM_B1) * g
    v = ADAM_B2 * v + (1.0 - ADAM_B2) * (g * g)
    m_hat = m / (1.0 - ADAM_B1 ** ADAM_STEP)
    v_hat = v / (1.0 - ADAM_B2 ** ADAM_STEP)
    delta = -ADAM_LR * (m_hat / (jnp.sqrt(v_hat) + ADAM_EPS) + ADAM_WD * w)
    return delta, m, v


def _adamw_big(w, g, m, v, *, rb):
    r, cdim = w.shape

    def body(w_ref, g_ref, m_ref, v_ref, d_ref, nm_ref, nv_ref):
        d_ref[...], nm_ref[...], nv_ref[...] = _adamw_math(w_ref[...], g_ref[...], m_ref[...], v_ref[...])

    spec = pl.BlockSpec((rb, cdim), lambda i: (i, 0))
    return pl.pallas_call(
        body, name="adamw_big", grid=(r // rb,), in_specs=[spec] * 4, out_specs=[spec] * 3,
        out_shape=[jax.ShapeDtypeStruct((r, cdim), F32)] * 3,
    )(w, g, m, v)


def _adamw_small(ws, gs, ms, vs):
    n = len(ws)

    def body(*refs):
        w, g, m, v = refs[:n], refs[n:2 * n], refs[2 * n:3 * n], refs[3 * n:4 * n]
        d, nm, nv = refs[4 * n:5 * n], refs[5 * n:6 * n], refs[6 * n:7 * n]
        for k in range(n):
            d[k][...], nm[k][...], nv[k][...] = _adamw_math(w[k][...], g[k][...], m[k][...], v[k][...])

    shapes = [jax.ShapeDtypeStruct(a.shape, F32) for a in ws]
    outs = pl.pallas_call(body, name="adamw_small", out_shape=shapes * 3)(*ws, *gs, *ms, *vs)
    return outs[:n], outs[n:2 * n], outs[2 * n:]


TT = 256
TK = 512
CW_ROWS = 40
PACK_ROWS = 192


def _pack(rows):
    packed = jnp.concatenate(rows, axis=0)
    packed = jnp.pad(packed, ((0, PACK_ROWS - packed.shape[0]), (0, 0)))
    return packed.reshape(N_CHIP, PACK_ROWS // N_CHIP, GW)


def _reduce_to_owner_halves(parts, core1, tag):
    got = _exchange_halves(parts, tag)
    rbs = [256, SHARD_OUT // 2, PACK_ROWS // N_CHIP // 2]
    dts = [BF16, BF16, F32]
    return [_add_own_half(a, g, core1, rb=rb, dtype=dt) for a, g, rb, dt in zip(parts, got, rbs, dts)]


def kernel(x, w_in, conv_a_w, conv_a_b, conv_b_w, conv_b_b, ln_b_g, ln_b_b, pool_w, pool_b, pool_scale, w_out, ln_g, ln_b, loss_target, m_w_in, m_conv_a_w, m_conv_a_b, m_conv_b_w, m_conv_b_b, m_ln_b_g, m_ln_b_b, m_pool_w, m_pool_b, m_pool_scale, m_w_out, m_ln_g, m_ln_b, v_w_in, v_conv_a_w, v_conv_a_b, v_conv_b_w, v_conv_b_b, v_ln_b_g, v_ln_b_b, v_pool_w, v_pool_b, v_pool_scale, v_w_out, v_ln_g, v_ln_b):
    chip = 2 * lax.axis_index("x") + lax.axis_index("y")
    core1 = lax.axis_index("c").reshape(1).astype(jnp.int32)
    x2, tgt = x[0], loss_target[0]

    cw = jnp.zeros((DEPTH, CW_ROWS, PGD), F32).at[:, 0:KA].set(conv_a_w).at[:, 8:8 + KB].set(conv_b_w)
    win0_b, wout0_b, cw_all, bsh_in, bsh_out = _gather_weights(w_in, w_out, cw)
    cw_full = jnp.transpose(cw_all, (1, 2, 0, 3)).reshape(DEPTH, CW_ROWS, GW)
    row = lambda a, l: a[l].reshape(1, -1)
    cnt = _count_table()
    prm = [(cw_full[l, 0:KA], row(conv_a_b, l), cw_full[l, 8:8 + KB], row(conv_b_b, l), row(ln_b_g, l), row(ln_b_b, l),
            pool_w[l].astype(BF16), row(pool_b, l), row(pool_scale, l), cnt) for l in range(DEPTH)]

    h0, xb0, z0, x1, win1_b, wout1_b = _fwd_layer(x2, win0_b, wout0_b, prm[0], row(ln_g, 0), row(ln_b, 0), None, tt=TT, last=False,
                                                  comm=_gather_comm(bsh_in, bsh_out, 1))
    h1, xb1, dz1, dln1, loss8 = _fwd_layer(x1, win1_b, wout1_b, prm[1], row(ln_g, 1), row(ln_b, 1), tgt, tt=TT, last=True)

    dz0, dh1, dwout1, small1, dpw1, dln0 = _bwd_layer(dz1, h1, win1_b, wout1_b, prm[1], z0, row(ln_g, 0), tt=TT)
    dwin1 = _wgrad_in(xb1, dh1, tk=TK)
    loss_row = jnp.pad(loss8, ((0, 0), (0, GW - loss8.shape[1])))
    pack1 = _pack([small1, dpw1.reshape(PGD, GW), dln1.reshape(4, GW), dln0.reshape(4, GW), loss_row])
    sums1 = _reduce_to_owner_halves([dwin1, dwout1.reshape(N_CHIP, SHARD_OUT, D_MODEL), pack1], core1, "1")
    gx, dh0, dwout0, small0, dpw0 = _bwd_layer(dz0, h0, win0_b, wout0_b, prm[0], None, None, tt=TT)
    dwin0, *landed1 = _wgrad_in(xb0, dh0, tk=TK, comm=_owner_comm(sums1))
    pack0 = _pack([small0, dpw0.reshape(PGD, GW)])
    sums0 = _reduce_to_owner_halves([dwin0, dwout0.reshape(N_CHIP, SHARD_OUT, D_MODEL), pack0], core1, "0")
    landed0 = _send_to_owners(sums0)

    g_in = _sum_chips_into(landed0[0], _sum_chips_into(landed1[0], None, 1, core1, rb=256), 0, core1, rb=256)
    g_out = _sum_chips_into(landed0[1], _sum_chips_into(landed1[1], None, 1, core1, rb=SHARD_OUT // 2), 0, core1, rb=SHARD_OUT // 2)
    red_small = jnp.concatenate([_sum_chips(a, rb=PACK_ROWS // N_CHIP // 2) for a in (landed0[2], landed1[2])], axis=0)
    g_in, g_out, g_small = _spread_reduced(g_in, g_out, red_small)

    flat = lambda a: a.reshape(-1, a.shape[-1])
    unflat = lambda a, like: a.reshape(like.shape)
    d_in, nm_in, nv_in = [unflat(a, w_in) for a in _adamw_big(flat(w_in), flat(g_in), flat(m_w_in), flat(v_w_in), rb=256)]
    d_out, nm_out, nv_out = [unflat(a, w_out) for a in _adamw_big(flat(w_out), flat(g_out), flat(m_w_out), flat(v_w_out), rb=SHARD_OUT)]

    hp = PACK_ROWS // N_CHIP // 2
    unpack = lambda o: jnp.concatenate([g_small[:, o:o + hp], g_small[:, 2 * hp + o:3 * hp + o]], axis=1).reshape(PACK_ROWS, GW)
    p0, p1 = unpack(0), unpack(hp)
    small = [p0[0:N_RACC], p1[0:N_RACC]]
    dpw = [p[N_RACC:N_RACC + PGD].reshape(len(POOL_WINDOWS), PGD, PGD) for p in (p0, p1)]
    o = N_RACC + PGD
    g_lng = jnp.stack([p1[o + 4:o + 8].reshape(2, D_MODEL)[0], p1[o:o + 4].reshape(2, D_MODEL)[0]])
    g_lnb = jnp.stack([p1[o + 4:o + 8].reshape(2, D_MODEL)[1], p1[o:o + 4].reshape(2, D_MODEL)[1]])
    mine = lambda a: lax.dynamic_slice_in_dim(a, chip * PGD, PGD, axis=-1)
    stack = lambda f: jnp.stack([f(0), f(1)])
    g_caw = stack(lambda l: mine(small[l][R_DWA:R_DWA + KA]))
    g_cab = stack(lambda l: small[l][R_DCAB])
    g_cbw = stack(lambda l: mine(small[l][R_DWB:R_DWB + KB]))
    g_cbb = stack(lambda l: small[l][R_DCBB])
    g_lbg = stack(lambda l: small[l][R_DLBG])
    g_lbb = stack(lambda l: small[l][R_DLBB])
    g_pw = stack(lambda l: dpw[l])
    g_pb = stack(lambda l: small[l][R_DPB].reshape(len(POOL_WINDOWS), PGD))
    g_ps = stack(lambda l: small[l][R_DPS])
    ws = [conv_a_w, conv_a_b, conv_b_w, conv_b_b, ln_b_g, ln_b_b, pool_w, pool_b, pool_scale, ln_g, ln_b]
    gs = [g_caw, g_cab, g_cbw, g_cbb, g_lbg, g_lbb, g_pw, g_pb, g_ps, g_lng, g_lnb]
    ms = [m_conv_a_w, m_conv_a_b, m_conv_b_w, m_conv_b_b, m_ln_b_g, m_ln_b_b, m_pool_w, m_pool_b, m_pool_scale, m_ln_g, m_ln_b]
    vs = [v_conv_a_w, v_conv_a_b, v_conv_b_w, v_conv_b_b, v_ln_b_g, v_ln_b_b, v_pool_w, v_pool_b, v_pool_scale, v_ln_g, v_ln_b]
    ds, nms, nvs = _adamw_small([flat(a) for a in ws], [flat(a) for a in gs], [flat(a) for a in ms], [flat(a) for a in vs])
    ds, nms, nvs = ([unflat(a, w) for a, w in zip(t, ws)] for t in (ds, nms, nvs))

    loss = p1[o + 8, 0]

    def order(in_, small_, out_):
        return [in_, *small_[:9], out_, *small_[9:]]
    return (loss, gx[None], *order(g_in, gs, g_out), *order(d_in, ds, d_out), *order(nm_in, nms, nm_out), *order(nv_in, nvs, nv_out))
```

```python
import functools

import jax
import jax.numpy as jnp
import numpy as np
from jax import lax
from jax.experimental import pallas as pl
from jax.experimental.pallas import tpu as pltpu

F32 = jnp.float32
BF16 = jnp.bfloat16

D_MODEL = 1024
DEPTH = 2
GW = 512
D_IN = 9 * GW
D_MIX = 3 * GW
NG = D_IN // GW
POOL_WINDOWS = (2, 4, 8, 16)
PGD = 128
KA = 3
KB = 31
ALPHA = (2.0 * DEPTH) ** 0.25
LN_EPS = 1e-5
ADAM_LR, ADAM_B1, ADAM_B2, ADAM_EPS, ADAM_WD, ADAM_STEP = 0.001, 0.9, 0.999, 1e-08, 0.01, 10

N_CHIP = 4
SHARD_IN = D_IN // N_CHIP
SHARD_OUT = D_MIX // N_CHIP

SUBLANES = 8
RC = 32
HALO = 32
VMEM_LIMIT = 60 * 1024 * 1024

R_DWA, R_DCAB, R_DWB, R_DCBB, R_DLBG, R_DLBB, R_DPB, R_DPS, N_RACC = 0, 3, 4, 35, 36, 37, 38, 39, 40


def _sig(v):
    return 0.5 * jnp.tanh(0.5 * v) + 0.5


def _chunks(n_rows, fn, unroll=1, extra=None):
    def step(m, carry):
        for u in range(unroll):
            fn(pl.multiple_of((m * unroll + u) * RC, RC))
        if extra is not None:
            extra(m)
        return carry
    lax.fori_loop(0, n_rows // (RC * unroll), step, 0)


def _fold8(v):
    return v.reshape(RC // SUBLANES, SUBLANES, v.shape[-1]).sum(axis=0)


def _build_shifts(ext_ref, sh_ref, shifts, n_rows):
    for r in shifts:
        for c0 in range(0, n_rows, RC):
            n = min(RC, n_rows - c0)
            sh_ref[r, pl.ds(c0, n), :] = ext_ref[pl.ds(c0 + r, n), :]


def _tap(ext_ref, sh_ref, off, base, lanes=None):
    a, r = divmod(off, SUBLANES)
    src = ext_ref if r == 0 else sh_ref.at[r]
    if lanes is None:
        return src[pl.ds(base + SUBLANES * a, RC), :]
    return src[pl.ds(base + SUBLANES * a, RC), lanes]


def _ln_stats(v):
    mu = jnp.mean(v, axis=-1, keepdims=True)
    vc = v - mu
    var = jnp.mean(vc * vc, axis=-1, keepdims=True)
    rstd = lax.rsqrt(var + LN_EPS)
    return vc * rstd, rstd


def _ln_bwd(dy, xhat, rstd, g):
    dxh = dy * g
    m1 = jnp.mean(dxh, axis=-1, keepdims=True)
    m2 = jnp.mean(dxh * xhat, axis=-1, keepdims=True)
    return rstd * (dxh - m1 - xhat * m2)


def _for_taps(ext_ref, sh_ref, base, offsets, fn):
    for r in range(SUBLANES):
        offs = [o for o in offsets if o % SUBLANES == r]
        if not offs:
            continue
        a0, a1 = min(offs) // SUBLANES, max(offs) // SUBLANES
        src = ext_ref if r == 0 else sh_ref.at[r]
        win = src[pl.ds(base + SUBLANES * a0, RC + SUBLANES * (a1 - a0)), :]
        for o in offs:
            a = o // SUBLANES - a0
            fn(o, win[SUBLANES * a:SUBLANES * a + RC])


def _count_table():
    t = np.arange(1, RC + 1, dtype=np.float64)[:, None]
    w = np.repeat(np.asarray(POOL_WINDOWS, np.float64), PGD)[None, :]
    return jnp.asarray(1.0 / np.minimum(t, w), F32)


def _inv_count(cnt_ref, first):
    return jnp.where(first, cnt_ref[...], cnt_ref[RC - 1:RC, :])


def _hcol(h_ref, j, base):
    if len(h_ref.shape) == 3:
        return h_ref[j, pl.ds(base, RC), :].astype(F32)
    return h_ref[pl.ds(base, RC), j * GW:(j + 1) * GW].astype(F32)


def _with_comm(comm, ins, in_specs, out_shape, out_specs, scratch):
    if comm is None:
        return ins, in_specs, out_shape, out_specs, scratch
    hbm = pl.BlockSpec(memory_space=pl.ANY)
    return (ins + list(comm["ins"]), in_specs + [hbm] * len(comm["ins"]), out_shape + list(comm["out_shape"]),
            out_specs + [hbm] * len(comm["out_shape"]), scratch + list(comm["sems"]))


def _split_comm(refs, comm, n_in, n_out):
    refs = list(refs)
    if comm is None:
        return refs, None
    ci, co, cs = len(comm["ins"]), len(comm["out_shape"]), len(comm["sems"])
    own = refs[:n_in] + refs[n_in + ci:n_in + ci + n_out] + refs[n_in + ci + n_out + co:len(refs) - cs]
    return own, (refs[n_in:n_in + ci], refs[n_in + ci + n_out:n_in + ci + n_out + co], refs[len(refs) - cs:])


def _fwd_mixers(h_ref, y_scr, q_ext, ub_ext, cu_ext, sh, p_scr, pl_scr, prm, tt, t0, extra_b2=None):
    caw, cab, cbw, cbb, lbg, lbb, pw, pb, ps, cnt = prm

    def a1(base):
        q_ext[pl.ds(SUBLANES + base, RC), :] = _hcol(h_ref, 1, base) * _hcol(h_ref, 2, base)
    _chunks(tt, a1)
    _build_shifts(q_ext, sh, (6, 7), tt)

    def a2(base):
        ca = cab[...] + caw[0:1, :] * _tap(q_ext, sh, 6, base) + caw[1:2, :] * _tap(q_ext, sh, 7, base) \
            + caw[2:3, :] * _tap(q_ext, sh, 8, base)
        z = _hcol(h_ref, 3, base)
        y_scr[pl.ds(base, RC), 0:GW] = (_hcol(h_ref, 0, base) * ca * (z * _sig(z))).astype(BF16)
    _chunks(tt, a2, unroll=2)
    q_ext[0:SUBLANES, :] = q_ext[tt:tt + SUBLANES, :]

    def b1(base):
        ub_ext[pl.ds(HALO + base, RC), :] = _hcol(h_ref, 4, base) * _sig(_hcol(h_ref, 5, base))
    _chunks(tt, b1)
    _build_shifts(ub_ext, sh, range(1, 8), tt + HALO - SUBLANES)

    def b2(base):
        cb = [cbb[...] + jnp.zeros((RC, GW), F32)]

        def tap(off, v):
            cb[0] = cb[0] + cbw[off - 2:off - 1, :] * v
        _for_taps(ub_ext, sh, base, range(2, 2 + KB), tap)
        xhat, _ = _ln_stats(cb[0])
        lnv = xhat * lbg[...] + lbb[...]
        z = _hcol(h_ref, 6, base)
        y_scr[pl.ds(base, RC), GW:2 * GW] = (lnv * _sig(lnv) * (z * _sig(z))).astype(BF16)
    _chunks(tt, b2, unroll=2, extra=extra_b2)
    ub_ext[0:HALO, :] = ub_ext[tt:tt + HALO, :]

    def c1(base):
        cu_ext[pl.ds(16 + base, RC), :] = _hcol(h_ref, 7, base)
    _chunks(tt, c1)
    _build_shifts(cu_ext, sh, range(1, 8), tt + SUBLANES)

    def c2(base):
        ic = _inv_count(cnt, base + t0 == 0)
        for g, w in enumerate(POOL_WINDOWS):
            lanes = slice(g * PGD, (g + 1) * PGD)
            acc = _tap(cu_ext, sh, 16, base, lanes)
            for j in range(1, w):
                acc = acc + _tap(cu_ext, sh, 16 - j, base, lanes)
            p = acc * ic[:, lanes] - _tap(cu_ext, sh, 16, base, lanes)
            p_scr[pl.ds(base, RC), lanes] = p.astype(BF16)
    _chunks(tt, c2)
    cu_ext[0:16, :] = cu_ext[tt:tt + 16, :]
    for g in range(len(POOL_WINDOWS)):
        lanes = slice(g * PGD, (g + 1) * PGD)
        pl_scr[:, lanes] = jnp.dot(p_scr[:, lanes], pw[g], preferred_element_type=F32)

    def c3(base):
        z = _hcol(h_ref, 8, base)
        yc0 = (pl_scr[pl.ds(base, RC), :] + pb[...]) * ps[...]
        y_scr[pl.ds(base, RC), 2 * GW:3 * GW] = (yc0 * (z * _sig(z))).astype(BF16)
    _chunks(tt, c3, unroll=2)


def _fwd_layer(x, win_b, wout_b, prm, ln_g, ln_b, target, *, tt, last, comm=None):
    t_len = x.shape[0]
    n_t = t_len // tt

    def body(*refs):
        refs, comm_refs = _split_comm(refs, comm, n_in, n_out)
        if last:
            (x_ref, win_hbm, wout_hbm, caw, cab, cbw, cbb, lbg, lbb, pw, pb, ps, cnt, lng, lnb, tgt_ref,
             h_ref, xb_ref, dz_ref, dln_ref, loss_ref,
             win_v, wout_v, y_scr, o_scr, q_ext, ub_ext, cu_ext, sh, p_scr, pl_scr, acc2, lacc) = refs
        else:
            (x_ref, win_hbm, wout_hbm, caw, cab, cbw, cbb, lbg, lbb, pw, pb, ps, cnt, lng, lnb,
             h_ref, xb_ref, z_ref, xn_ref,
             win_v, wout_v, y_scr, o_scr, q_ext, ub_ext, cu_ext, sh, p_scr, pl_scr) = refs
        i = pl.program_id(0)

        @pl.when(i == 0)
        def _():
            if comm is not None:
                comm["start"](*comm_refs)
            pltpu.sync_copy(win_hbm, win_v)
            pltpu.sync_copy(wout_hbm, wout_v)
            q_ext[0:SUBLANES, :] = jnp.zeros((SUBLANES, GW), F32)
            ub_ext[0:HALO, :] = jnp.zeros((HALO, GW), F32)
            cu_ext[0:16, :] = jnp.zeros((16, GW), F32)
            if last:
                acc2[...] = jnp.zeros_like(acc2)
                lacc[...] = jnp.zeros_like(lacc)

        xb_ref[...] = x_ref[...].astype(BF16)
        for j in range(NG):
            h_ref[:, j * GW:(j + 1) * GW] = jnp.dot(
                xb_ref[...], win_v[:, j * GW:(j + 1) * GW], preferred_element_type=F32).astype(BF16)

        _fwd_mixers(h_ref, y_scr, q_ext, ub_ext, cu_ext, sh, p_scr, pl_scr,
                    (caw, cab, cbw, cbb, lbg, lbb, pw, pb, ps, cnt), tt, i * tt)

        o_scr[...] = jnp.dot(y_scr[...], wout_v[...], preferred_element_type=F32)

        def post(base):
            rows = pl.ds(base, RC)
            z = ALPHA * x_ref[rows, :] + o_scr[rows, :]
            xhat, rstd = _ln_stats(z)
            xn = xhat * lng[...] + lnb[...]
            if last:
                err = xn - tgt_ref[rows, :]
                lacc[...] += _fold8(err * err)
                dxn = err * (1.0 / D_MODEL)
                acc2[0] += _fold8(dxn * xhat)
                acc2[1] += _fold8(dxn)
                dz_ref[rows, :] = _ln_bwd(dxn, xhat, rstd, lng[...])
            else:
                z_ref[rows, :] = z
                xn_ref[rows, :] = xn
        _chunks(tt, post, unroll=2)

        if last:
            @pl.when(i == n_t - 1)
            def _():
                dln_ref[...] = jnp.sum(acc2[...], axis=1)
                loss_ref[...] = jnp.zeros((SUBLANES, 128), F32) + (0.5 / D_MODEL) * jnp.sum(lacc[...])
        if comm is not None:
            @pl.when(i == n_t - 1)
            def _():
                comm["wait"](*comm_refs)

    tile = lambda c: pl.BlockSpec((tt, c), lambda i: (i, 0))
    full = lambda a: pl.BlockSpec(a.shape, lambda i: (0,) * a.ndim)
    hbm = pl.BlockSpec(memory_space=pl.ANY)
    ins = [x, win_b, wout_b, *prm, ln_g, ln_b] + ([target] if last else [])
    in_specs = [tile(D_MODEL), hbm, hbm] + [full(a) for a in (*prm, ln_g, ln_b)] + ([tile(D_MODEL)] if last else [])
    out_shape = [jax.ShapeDtypeStruct((t_len, D_IN), BF16), jax.ShapeDtypeStruct((t_len, D_MODEL), BF16)]
    out_specs = [tile(D_IN), tile(D_MODEL)]
    if last:
        out_shape += [jax.ShapeDtypeStruct((t_len, D_MODEL), F32), jax.ShapeDtypeStruct((2, D_MODEL), F32),
                      jax.ShapeDtypeStruct((SUBLANES, 128), F32)]
        out_specs += [tile(D_MODEL), pl.BlockSpec((2, D_MODEL), lambda i: (0, 0)),
                      pl.BlockSpec((SUBLANES, 128), lambda i: (0, 0))]
    else:
        out_shape += [jax.ShapeDtypeStruct((t_len, D_MODEL), F32), jax.ShapeDtypeStruct((t_len, D_MODEL), F32)]
        out_specs += [tile(D_MODEL), tile(D_MODEL)]
    scratch = [
        pltpu.VMEM((D_MODEL, D_IN), BF16), pltpu.VMEM((D_MIX, D_MODEL), BF16),
        pltpu.VMEM((tt, D_MIX), BF16), pltpu.VMEM((tt, D_MODEL), F32),
        pltpu.VMEM((tt + SUBLANES, GW), F32), pltpu.VMEM((tt + HALO, GW), F32), pltpu.VMEM((tt + 16, GW), F32),
        pltpu.VMEM((SUBLANES, tt + HALO, GW), F32),
        pltpu.VMEM((tt, GW), BF16), pltpu.VMEM((tt, GW), F32),
    ]
    if last:
        scratch += [pltpu.VMEM((2, SUBLANES, D_MODEL), F32), pltpu.VMEM((SUBLANES, D_MODEL), F32)]
    n_in, n_out = len(ins), len(out_shape)
    ins, in_specs, out_shape, out_specs, scratch = _with_comm(comm, ins, in_specs, out_shape, out_specs, scratch)
    return pl.pallas_call(
        body, name=("fwd_last" if last else "fwd_layer") + ("" if comm is None else "_comm"), grid=(n_t,),
        in_specs=in_specs, out_specs=out_specs, out_shape=out_shape, scratch_shapes=scratch,
        compiler_params=pltpu.CompilerParams(dimension_semantics=("arbitrary",), vmem_limit_bytes=VMEM_LIMIT),
    )(*ins)


def _dsilu(z, sz):
    return sz * (1.0 + z * (1.0 - sz))


def _bwd_layer(dz, h, win_b, wout_b, prm, z_prev, lng_prev, *, tt, comm=None):
    t_len = dz.shape[0]
    n_t = t_len // tt
    has_prev = z_prev is not None
    hb = tt // HALO

    def body(*refs):
        refs, comm_refs = _split_comm(refs, comm, n_in, n_out)
        dz_ref, h_ref, halo_ref, win_hbm, wout_hbm, caw, cab, cbw, cbb, lbg, lbb, pw, pb, ps, cnt = refs[:15]
        k = 15
        if has_prev:
            zp_ref, lngp = refs[k:k + 2]
            k += 2
        dxo_ref, dh_ref, dwout_hbm, small_ref, dpw_ref = refs[k:k + 5]
        k += 5
        if has_prev:
            dlnp_ref = refs[k]
            k += 1
        (win_v, wout_v, dzb, dy_scr, y_scr, dx_scr, q_ext, ub_ext, cu_ext, dca_ext, dcb_ext, dpn_ext, sh,
         p_scr, pl_scr, dpl_scr, dp_scr, racc, dpw_acc, dwout_acc) = refs[k:k + 20]
        k += 20
        if has_prev:
            acc2 = refs[k]
        i = pl.program_id(0)
        ti = n_t - 1 - i
        t0 = ti * tt

        @pl.when(i == 0)
        def _():
            if comm is not None:
                comm["start"](*comm_refs)
            pltpu.sync_copy(win_hbm, win_v)
            pltpu.sync_copy(wout_hbm, wout_v)
            dca_ext[tt:tt + SUBLANES, :] = jnp.zeros((SUBLANES, GW), F32)
            dcb_ext[tt:tt + HALO, :] = jnp.zeros((HALO, GW), F32)
            dpn_ext[tt:tt + 16, :] = jnp.zeros((16, GW), F32)
            racc[...] = jnp.zeros_like(racc)
            dpw_acc[...] = jnp.zeros_like(dpw_acc)
            dwout_acc[...] = jnp.zeros_like(dwout_acc)
            if has_prev:
                acc2[...] = jnp.zeros_like(acc2)

        dzb[...] = dz_ref[...].astype(BF16)
        dy_scr[...] = lax.dot_general(dzb[...], wout_v[...], (((1,), (1,)), ((), ())), preferred_element_type=F32)

        live = (ti > 0).astype(F32)
        hh = lambda j, r0, r1: halo_ref[r0:r1, j * GW:(j + 1) * GW].astype(F32)
        q_ext[0:SUBLANES, :] = live * hh(1, 24, 32) * hh(2, 24, 32)
        ub_ext[0:HALO, :] = live * hh(4, 0, 32) * _sig(hh(5, 0, 32))
        cu_ext[0:16, :] = live * hh(7, 16, 32)

        def a1(base):
            q_ext[pl.ds(SUBLANES + base, RC), :] = _hcol(h_ref, 1, base) * _hcol(h_ref, 2, base)
        _chunks(tt, a1)
        _build_shifts(q_ext, sh, (6, 7), tt)

        def a2(base):
            rows = pl.ds(base, RC)
            q6, q7, q8 = _tap(q_ext, sh, 6, base), _tap(q_ext, sh, 7, base), _tap(q_ext, sh, 8, base)
            ca = cab[...] + caw[0:1, :] * q6 + caw[1:2, :] * q7 + caw[2:3, :] * q8
            bg, z = _hcol(h_ref, 0, base), _hcol(h_ref, 3, base)
            sz = _sig(z)
            sza = z * sz
            dya = dy_scr[rows, 0:GW]
            ya0 = bg * ca
            y_scr[rows, 0:GW] = (ya0 * sza).astype(BF16)
            dya0 = dya * sza
            dh_ref[rows, 3 * GW:4 * GW] = (dya * ya0 * _dsilu(z, sz)).astype(BF16)
            dh_ref[rows, 0:GW] = (dya0 * ca).astype(BF16)
            dca = dya0 * bg
            dca_ext[rows, :] = dca
            racc[R_DCAB] += _fold8(dca)
            racc[R_DWA + 0] += _fold8(dca * q6)
            racc[R_DWA + 1] += _fold8(dca * q7)
            racc[R_DWA + 2] += _fold8(dca * q8)
        _chunks(tt, a2)
        _build_shifts(dca_ext, sh, (1, 2), tt)

        def a3(base):
            rows = pl.ds(base, RC)
            dq = caw[0:1, :] * _tap(dca_ext, sh, 2, base) + caw[1:2, :] * _tap(dca_ext, sh, 1, base) \
                + caw[2:3, :] * _tap(dca_ext, sh, 0, base)
            dh_ref[rows, GW:2 * GW] = (dq * _hcol(h_ref, 2, base)).astype(BF16)
            dh_ref[rows, 2 * GW:3 * GW] = (dq * _hcol(h_ref, 1, base)).astype(BF16)
        _chunks(tt, a3)
        dca_ext[tt:tt + SUBLANES, :] = dca_ext[0:SUBLANES, :]

        def b1(base):
            ub_ext[pl.ds(HALO + base, RC), :] = _hcol(h_ref, 4, base) * _sig(_hcol(h_ref, 5, base))
        _chunks(tt, b1)
        _build_shifts(ub_ext, sh, range(1, 8), tt + HALO - SUBLANES)

        def b2(base):
            rows = pl.ds(base, RC)
            cbv = [cbb[...] + jnp.zeros((RC, GW), F32)]

            def tap(off, v):
                cbv[0] = cbv[0] + cbw[off - 2:off - 1, :] * v
            _for_taps(ub_ext, sh, base, range(2, 2 + KB), tap)
            xhat, rstd = _ln_stats(cbv[0])
            lnv = xhat * lbg[...] + lbb[...]
            sl = _sig(lnv)
            s = lnv * sl
            z = _hcol(h_ref, 6, base)
            sz = _sig(z)
            szb = z * sz
            y_scr[rows, GW:2 * GW] = (s * szb).astype(BF16)
            dyb = dy_scr[rows, GW:2 * GW]
            dh_ref[rows, 6 * GW:7 * GW] = (dyb * s * _dsilu(z, sz)).astype(BF16)
            dlnv = dyb * szb * _dsilu(lnv, sl)
            racc[R_DLBG] += _fold8(dlnv * xhat)
            racc[R_DLBB] += _fold8(dlnv)
            dcb = _ln_bwd(dlnv, xhat, rstd, lbg[...])
            dcb_ext[rows, :] = dcb
            racc[R_DCBB] += _fold8(dcb)

            def wtap(off, v):
                racc[R_DWB + off - 2] += _fold8(dcb * v)
            _for_taps(ub_ext, sh, base, range(2, 2 + KB), wtap)
        _chunks(tt, b2, unroll=2)
        _build_shifts(dcb_ext, sh, range(1, 8), tt + HALO - SUBLANES)

        def b3(base):
            rows = pl.ds(base, RC)
            dubv = [jnp.zeros((RC, GW), F32)]

            def tap(off, v):
                dubv[0] = dubv[0] + cbw[KB - 1 - off:KB - off, :] * v
            _for_taps(dcb_ext, sh, base, range(KB), tap)
            dub = dubv[0]
            v, gt = _hcol(h_ref, 4, base), _hcol(h_ref, 5, base)
            sg = _sig(gt)
            dh_ref[rows, 4 * GW:5 * GW] = (dub * sg).astype(BF16)
            dh_ref[rows, 5 * GW:6 * GW] = (dub * v * sg * (1.0 - sg)).astype(BF16)
        _chunks(tt, b3)
        dcb_ext[tt:tt + HALO, :] = dcb_ext[0:HALO, :]

        def c1(base):
            cu_ext[pl.ds(16 + base, RC), :] = _hcol(h_ref, 7, base)
        _chunks(tt, c1)
        _build_shifts(cu_ext, sh, range(1, 8), tt + SUBLANES)

        def c2(base):
            ic = _inv_count(cnt, base + t0 == 0)
            for g, w in enumerate(POOL_WINDOWS):
                lanes = slice(g * PGD, (g + 1) * PGD)
                acc = _tap(cu_ext, sh, 16, base, lanes)
                for j in range(1, w):
                    acc = acc + _tap(cu_ext, sh, 16 - j, base, lanes)
                p = acc * ic[:, lanes] - _tap(cu_ext, sh, 16, base, lanes)
                p_scr[pl.ds(base, RC), lanes] = p.astype(BF16)
        _chunks(tt, c2)
        for g in range(len(POOL_WINDOWS)):
            lanes = slice(g * PGD, (g + 1) * PGD)
            pl_scr[:, lanes] = jnp.dot(p_scr[:, lanes], pw[g], preferred_element_type=F32)

        def c3(base):
            rows = pl.ds(base, RC)
            z = _hcol(h_ref, 8, base)
            sz = _sig(z)
            szc = z * sz
            plb = pl_scr[rows, :] + pb[...]
            yc0 = plb * ps[...]
            y_scr[rows, 2 * GW:3 * GW] = (yc0 * szc).astype(BF16)
            dyc = dy_scr[rows, 2 * GW:3 * GW]
            dh_ref[rows, 8 * GW:9 * GW] = (dyc * yc0 * _dsilu(z, sz)).astype(BF16)
            dyc0 = dyc * szc
            racc[R_DPS] += _fold8(dyc0 * plb)
            dpl = dyc0 * ps[...]
            racc[R_DPB] += _fold8(dpl)
            dpl_scr[rows, :] = dpl.astype(BF16)
        _chunks(tt, c3)
        for g in range(len(POOL_WINDOWS)):
            lanes = slice(g * PGD, (g + 1) * PGD)
            dpw_acc[g] += lax.dot_general(p_scr[:, lanes], dpl_scr[:, lanes], (((0,), (0,)), ((), ())),
                                          preferred_element_type=F32)
            dp_scr[:, lanes] = lax.dot_general(dpl_scr[:, lanes], pw[g], (((1,), (1,)), ((), ())),
                                               preferred_element_type=F32)

        def c4(base):
            rows = pl.ds(base, RC)
            dpn_ext[rows, :] = dp_scr[rows, :] * _inv_count(cnt, base + t0 == 0)
        _chunks(tt, c4)
        _build_shifts(dpn_ext, sh, range(1, 8), tt + SUBLANES)

        def c5(base):
            rows = pl.ds(base, RC)
            for g, w in enumerate(POOL_WINDOWS):
                lanes = slice(g * PGD, (g + 1) * PGD)
                acc = _tap(dpn_ext, sh, 0, base, lanes)
                for j in range(1, w):
                    acc = acc + _tap(dpn_ext, sh, j, base, lanes)
                dh_ref[rows, 7 * GW + g * PGD:7 * GW + (g + 1) * PGD] = (acc - dp_scr[rows, lanes]).astype(BF16)
        _chunks(tt, c5)
        dpn_ext[tt:tt + 16, :] = dpn_ext[0:16, :]

        for r in range(D_MIX // GW):
            dwout_acc[r * GW:(r + 1) * GW, :] += lax.dot_general(
                y_scr[:, r * GW:(r + 1) * GW], dzb[...], (((0,), (0,)), ((), ())), preferred_element_type=F32)
        dx_scr[...] = lax.dot_general(dh_ref[...], win_v[...], (((1,), (1,)), ((), ())), preferred_element_type=F32)

        def post(base):
            rows = pl.ds(base, RC)
            dx = ALPHA * dz_ref[rows, :] + dx_scr[rows, :]
            if has_prev:
                xhat, rstd = _ln_stats(zp_ref[rows, :])
                acc2[0] += _fold8(dx * xhat)
                acc2[1] += _fold8(dx)
                dxo_ref[rows, :] = _ln_bwd(dx, xhat, rstd, lngp[...])
            else:
                dxo_ref[rows, :] = dx
        _chunks(tt, post, unroll=2)

        @pl.when(i == n_t - 1)
        def _():
            small_ref[...] = jnp.sum(racc[...], axis=1)
            dpw_ref[...] = dpw_acc[...]
            pltpu.sync_copy(dwout_acc, dwout_hbm)
            if has_prev:
                dlnp_ref[...] = jnp.sum(acc2[...], axis=1)
            if comm is not None:
                comm["wait"](*comm_refs)

    rtile = lambda c: pl.BlockSpec((tt, c), lambda i: (n_t - 1 - i, 0))
    full = lambda a: pl.BlockSpec(a.shape, lambda i: (0,) * a.ndim)
    const = lambda shp: pl.BlockSpec(shp, lambda i: (0,) * len(shp))
    hbm = pl.BlockSpec(memory_space=pl.ANY)
    halo_spec = pl.BlockSpec((HALO, D_IN), lambda i: (jnp.maximum((n_t - 1 - i) * hb - 1, 0), 0))
    ins = [dz, h, h, win_b, wout_b, *prm] + ([z_prev, lng_prev] if has_prev else [])
    in_specs = [rtile(D_MODEL), rtile(D_IN), halo_spec, hbm, hbm] + [full(a) for a in prm] \
        + ([rtile(D_MODEL), full(lng_prev)] if has_prev else [])
    out_shape = [jax.ShapeDtypeStruct((t_len, D_MODEL), F32), jax.ShapeDtypeStruct((t_len, D_IN), BF16),
                 jax.ShapeDtypeStruct((D_MIX, D_MODEL), F32), jax.ShapeDtypeStruct((N_RACC, GW), F32),
                 jax.ShapeDtypeStruct((len(POOL_WINDOWS), PGD, PGD), F32)]
    out_specs = [rtile(D_MODEL), rtile(D_IN), hbm, const((N_RACC, GW)), const((len(POOL_WINDOWS), PGD, PGD))]
    if has_prev:
        out_shape.append(jax.ShapeDtypeStruct((2, D_MODEL), F32))
        out_specs.append(const((2, D_MODEL)))
    scratch = [
        pltpu.VMEM((D_MODEL, D_IN), BF16), pltpu.VMEM((D_MIX, D_MODEL), BF16),
        pltpu.VMEM((tt, D_MODEL), BF16), pltpu.VMEM((tt, D_MIX), F32), pltpu.VMEM((tt, D_MIX), BF16),
        pltpu.VMEM((tt, D_MODEL), F32),
        pltpu.VMEM((tt + SUBLANES, GW), F32), pltpu.VMEM((tt + HALO, GW), F32), pltpu.VMEM((tt + 16, GW), F32),
        pltpu.VMEM((tt + SUBLANES, GW), F32), pltpu.VMEM((tt + HALO, GW), F32), pltpu.VMEM((tt + 16, GW), F32),
        pltpu.VMEM((SUBLANES, tt + HALO, GW), F32),
        pltpu.VMEM((tt, GW), BF16), pltpu.VMEM((tt, GW), F32), pltpu.VMEM((tt, GW), BF16), pltpu.VMEM((tt, GW), F32),
        pltpu.VMEM((N_RACC, SUBLANES, GW), F32), pltpu.VMEM((len(POOL_WINDOWS), PGD, PGD), F32),
        pltpu.VMEM((D_MIX, D_MODEL), F32),
    ]
    if has_prev:
        scratch.append(pltpu.VMEM((2, SUBLANES, D_MODEL), F32))
    n_in, n_out = len(ins), len(out_shape)
    ins, in_specs, out_shape, out_specs, scratch = _with_comm(comm, ins, in_specs, out_shape, out_specs, scratch)
    return pl.pallas_call(
        body, name=("bwd_layer_prev" if has_prev else "bwd_layer") + ("" if comm is None else "_comm"), grid=(n_t,),
        in_specs=in_specs, out_specs=out_specs, out_shape=out_shape, scratch_shapes=scratch,
        compiler_params=pltpu.CompilerParams(dimension_semantics=("arbitrary",), vmem_limit_bytes=VMEM_LIMIT),
    )(*ins)


def _wgrad_in(xb, dh, *, tk, comm=None):
    t_len = xb.shape[0]
    n_k = t_len // tk

    def body(*refs):
        (x_ref, dh_ref, o_ref), comm_refs = _split_comm(refs, comm, 2, 1)
        j, k = pl.program_id(0), pl.program_id(1)

        @pl.when(k == 0)
        def _():
            o_ref[...] = jnp.zeros_like(o_ref)
        if comm is not None:
            @pl.when((j == 0) & (k == 0))
            def _():
                comm["start"](*comm_refs)
        o_ref[0] += lax.dot_general(x_ref[...], dh_ref[...], (((0,), (0,)), ((), ())), preferred_element_type=F32)
        if comm is not None:
            @pl.when((j == N_CHIP - 1) & (k == n_k - 1))
            def _():
                comm["wait"](*comm_refs)

    ins = [xb, dh]
    in_specs = [pl.BlockSpec((tk, D_MODEL), lambda j, k: (k, 0)), pl.BlockSpec((tk, SHARD_IN), lambda j, k: (k, j))]
    out_shape = [jax.ShapeDtypeStruct((N_CHIP, D_MODEL, SHARD_IN), F32)]
    out_specs = [pl.BlockSpec((1, D_MODEL, SHARD_IN), lambda j, k: (j, 0, 0))]
    ins, in_specs, out_shape, out_specs, scratch = _with_comm(comm, ins, in_specs, out_shape, out_specs, [])
    outs = pl.pallas_call(
        body, name="wgrad_in" + ("" if comm is None else "_comm"), grid=(N_CHIP, n_k),
        in_specs=in_specs, out_specs=out_specs, out_shape=out_shape, scratch_shapes=scratch,
        compiler_params=pltpu.CompilerParams(dimension_semantics=("arbitrary", "arbitrary"), vmem_limit_bytes=VMEM_LIMIT),
    )(*ins)
    return outs[0] if comm is None else outs


MESH = pl.DeviceIdType.MESH
ANY = pl.BlockSpec(memory_space=pl.ANY)


def _place():
    x, y, c = lax.axis_index("x"), lax.axis_index("y"), lax.axis_index("c")
    others = [(1 - x, y), (x, 1 - y), (1 - x, 1 - y)]
    return x, y, c, 2 * x + y, [(ox, oy, 2 * ox + oy) for ox, oy in others]


def _rcopy(src, dst, send_sems, recv_sems, k, dev):
    return pltpu.make_async_remote_copy(src_ref=src, dst_ref=dst, send_sem=send_sems.at[k], recv_sem=recv_sems.at[k],
                                        device_id=dev, device_id_type=MESH)


def _gather_weights(w_in, w_out, cw):
    hi, ho = D_MODEL // 2, SHARD_OUT // 2

    def body(win_ref, wout_ref, cw_ref, owin, owout, ocw, bin_v, bout_v, send_sems, recv_sems, lsem):
        x, y, c, me, others = _place()
        for l in range(DEPTH):
            for r0 in range(0, D_MODEL, 256):
                bin_v[l, r0:r0 + 256, :] = win_ref[l, r0:r0 + 256, :].astype(BF16)
            bout_v[l] = wout_ref[l].astype(BF16)
        cin = pl.ds(pl.multiple_of(me * SHARD_IN, 128), SHARD_IN)
        rout = pl.ds(pl.multiple_of(me * SHARD_OUT, 128), SHARD_OUT)
        local = [pltpu.make_async_copy(bin_v.at[0], owin.at[:, cin], lsem.at[0]),
                 pltpu.make_async_copy(bout_v.at[0], owout.at[rout, :], lsem.at[1]),
                 pltpu.make_async_copy(cw_ref, ocw.at[me], lsem.at[2])]
        for cp in local:
            cp.start()

        def in_half(chip, core):
            return owin.at[pl.ds(pl.multiple_of(core * hi, 256), hi), pl.ds(pl.multiple_of(chip * SHARD_IN, 128), SHARD_IN)]

        def out_half(chip, core):
            return owout.at[pl.ds(pl.multiple_of(chip * SHARD_OUT + core * ho, 64), ho), :]

        first = []
        for k, (ox, oy, _) in enumerate(others):
            dev = (ox, oy, c)
            first.append(_rcopy(bin_v.at[0, pl.ds(pl.multiple_of(c * hi, 256), hi), :], in_half(me, c), send_sems, recv_sems, k, dev))
            first.append(_rcopy(bout_v.at[0, pl.ds(pl.multiple_of(c * ho, 64), ho), :], out_half(me, c), send_sems, recv_sems, 3 + k, dev))
            first.append(_rcopy(cw_ref, ocw.at[me], send_sems, recv_sems, 6 + k, dev))
        for cp in first:
            cp.start()
        sib = (x, y, 1 - c)
        passed = []
        for k, (ox, oy, oc) in enumerate(others):
            _rcopy(in_half(oc, c), in_half(oc, c), send_sems, recv_sems, k, sib).wait_recv()
            fwd_in = _rcopy(in_half(oc, c), in_half(oc, c), send_sems, recv_sems, 9 + k, sib)
            fwd_in.start()
            _rcopy(out_half(oc, c), out_half(oc, c), send_sems, recv_sems, 3 + k, sib).wait_recv()
            fwd_out = _rcopy(out_half(oc, c), out_half(oc, c), send_sems, recv_sems, 12 + k, sib)
            fwd_out.start()
            passed += [fwd_in, fwd_out]
        for k, (ox, oy, oc) in enumerate(others):
            _rcopy(cw_ref, ocw.at[oc], send_sems, recv_sems, 6 + k, sib).wait_recv()
            _rcopy(in_half(oc, 1 - c), in_half(oc, 1 - c), send_sems, recv_sems, 9 + k, sib).wait_recv()
            _rcopy(out_half(oc, 1 - c), out_half(oc, 1 - c), send_sems, recv_sems, 12 + k, sib).wait_recv()
        for cp in first + passed:
            cp.wait_send()
        for cp in local:
            cp.wait()

    vm = pl.BlockSpec(memory_space=pltpu.VMEM)
    return pl.pallas_call(
        body, name="gather_weights",
        in_specs=[vm, vm, vm], out_specs=[ANY, ANY, ANY, vm, vm],
        out_shape=[jax.ShapeDtypeStruct((D_MODEL, D_IN), BF16), jax.ShapeDtypeStruct((D_MIX, D_MODEL), BF16),
                   jax.ShapeDtypeStruct((N_CHIP,) + cw.shape, F32),
                   jax.ShapeDtypeStruct((DEPTH, D_MODEL, SHARD_IN), BF16), jax.ShapeDtypeStruct((DEPTH, SHARD_OUT, D_MODEL), BF16)],
        scratch_shapes=[pltpu.SemaphoreType.DMA((15,)), pltpu.SemaphoreType.DMA((15,)), pltpu.SemaphoreType.DMA((3,))],
        compiler_params=pltpu.CompilerParams(vmem_limit_bytes=VMEM_LIMIT),
    )(w_in, w_out, cw)


def _gather_starts(bsh_in, bsh_out, owin, owout, send_sems, recv_sems, lsem, layer):
    x, y, c, me, others = _place()
    hi, ho = D_MODEL // 2, SHARD_OUT // 2
    pltpu.make_async_copy(bsh_in.at[layer], owin.at[:, pl.ds(pl.multiple_of(me * SHARD_IN, 128), SHARD_IN)], lsem.at[0]).start()
    pltpu.make_async_copy(bsh_out.at[layer], owout.at[pl.ds(pl.multiple_of(me * SHARD_OUT, 128), SHARD_OUT), :], lsem.at[1]).start()
    for k, (ox, oy, _) in enumerate(others):
        for t in range(2):
            pltpu.make_async_remote_copy(
                src_ref=bsh_in.at[layer, pl.ds(pl.multiple_of(c * hi, 256), hi), :],
                dst_ref=owin.at[pl.ds(pl.multiple_of(c * hi, 256), hi), pl.ds(pl.multiple_of(me * SHARD_IN, 128), SHARD_IN)],
                send_sem=send_sems.at[2 * k + t], recv_sem=recv_sems.at[2 * k + c], device_id=(ox, oy, t), device_id_type=MESH).start()
            pltpu.make_async_remote_copy(
                src_ref=bsh_out.at[layer, pl.ds(pl.multiple_of(c * ho, 64), ho), :],
                dst_ref=owout.at[pl.ds(pl.multiple_of(me * SHARD_OUT + c * ho, 64), ho), :],
                send_sem=send_sems.at[6 + 2 * k + t], recv_sem=recv_sems.at[6 + 2 * k + c], device_id=(ox, oy, t), device_id_type=MESH).start()


def _gather_waits(bsh_in, bsh_out, owin, owout, send_sems, recv_sems, lsem, layer):
    x, y, c, me, others = _place()
    hi, ho = D_MODEL // 2, SHARD_OUT // 2
    src_in = bsh_in.at[layer, pl.ds(0, hi), :]
    src_out = bsh_out.at[layer, pl.ds(0, ho), :]
    for k, (ox, oy, oc) in enumerate(others):
        for t in range(2):
            dst_in = owin.at[pl.ds(t * hi, hi), pl.ds(pl.multiple_of(oc * SHARD_IN, 128), SHARD_IN)]
            dst_out = owout.at[pl.ds(pl.multiple_of(oc * SHARD_OUT + t * ho, 64), ho), :]
            a = pltpu.make_async_remote_copy(src_ref=src_in, dst_ref=dst_in, send_sem=send_sems.at[2 * k + t],
                                             recv_sem=recv_sems.at[2 * k + t], device_id=(ox, oy, t), device_id_type=MESH)
            b = pltpu.make_async_remote_copy(src_ref=src_out, dst_ref=dst_out, send_sem=send_sems.at[6 + 2 * k + t],
                                             recv_sem=recv_sems.at[6 + 2 * k + t], device_id=(ox, oy, t), device_id_type=MESH)
            a.wait_send()
            a.wait_recv()
            b.wait_send()
            b.wait_recv()
    pltpu.make_async_copy(bsh_in.at[layer], owin.at[:, pl.ds(pl.multiple_of(me * SHARD_IN, 128), SHARD_IN)], lsem.at[0]).wait()
    pltpu.make_async_copy(bsh_out.at[layer], owout.at[pl.ds(pl.multiple_of(me * SHARD_OUT, 128), SHARD_OUT), :], lsem.at[1]).wait()


def _gather_comm(bsh_in, bsh_out, layer):
    return dict(ins=[bsh_in, bsh_out],
                out_shape=[jax.ShapeDtypeStruct((D_MODEL, D_IN), BF16), jax.ShapeDtypeStruct((D_MIX, D_MODEL), BF16)],
                sems=[pltpu.SemaphoreType.DMA((12,)), pltpu.SemaphoreType.DMA((12,)), pltpu.SemaphoreType.DMA((2,))],
                start=lambda ins, outs, sems: _gather_starts(ins[0], ins[1], outs[0], outs[1], *sems, layer),
                wait=lambda ins, outs, sems: _gather_waits(ins[0], ins[1], outs[0], outs[1], *sems, layer))


def _exchange_halves(arrs, tag):
    n = len(arrs)

    def body(*refs):
        ins, outs, (send_sems, recv_sems) = refs[:n], refs[n:2 * n], refs[2 * n:]
        x, y, c, _, _ = _place()
        cps = []
        for m in range(n):
            half = ins[m].shape[1] // 2
            cps.append(_rcopy(ins[m].at[:, pl.ds(pl.multiple_of((1 - c) * half, SUBLANES), half), :], outs[m],
                              send_sems, recv_sems, m, (x, y, 1 - c)))
        for cp in cps:
            cp.start()
        for cp in cps:
            cp.wait()

    return pl.pallas_call(
        body, name="exchange_halves_" + tag, in_specs=[ANY] * n, out_specs=[ANY] * n,
        out_shape=[jax.ShapeDtypeStruct((a.shape[0], a.shape[1] // 2, a.shape[2]), F32) for a in arrs],
        scratch_shapes=[pltpu.SemaphoreType.DMA((n,)), pltpu.SemaphoreType.DMA((n,))],
    )(*arrs)


def _add_own_half(a, got, core, *, rb, dtype):
    nj, r, cdim = a.shape
    half = r // 2

    def body(core_ref, a_ref, g_ref, o_ref):
        o_ref[...] = (a_ref[0] + g_ref[...]).astype(dtype)

    return pl.pallas_call(
        body, name="add_own_half",
        grid_spec=pltpu.PrefetchScalarGridSpec(
            num_scalar_prefetch=1, grid=(nj, half // rb),
            in_specs=[pl.BlockSpec((1, 1, rb, cdim), lambda j, i, cr: (j, cr[0], i, 0)),
                      pl.BlockSpec((1, rb, cdim), lambda j, i, cr: (j, i, 0))],
            out_specs=pl.BlockSpec((1, rb, cdim), lambda j, i, cr: (j, i, 0))),
        out_shape=jax.ShapeDtypeStruct((nj, half, cdim), dtype),
    )(core, a.reshape(nj, 2, half, cdim), got)


def _owner_starts(ins, outs, sems):
    send_sems, recv_sems, lsem = sems
    x, y, c, me, others = _place()
    for m in range(len(ins)):
        pltpu.make_async_copy(ins[m].at[me], outs[m].at[me], lsem.at[m]).start()
        for k, (ox, oy, oc) in enumerate(others):
            _rcopy(ins[m].at[oc], outs[m].at[me], send_sems, recv_sems, 3 * m + k, (ox, oy, c)).start()


def _owner_waits(ins, outs, sems):
    send_sems, recv_sems, lsem = sems
    x, y, c, me, others = _place()
    for m in range(len(ins)):
        for k, (ox, oy, oc) in enumerate(others):
            _rcopy(ins[m].at[oc], outs[m].at[oc], send_sems, recv_sems, 3 * m + k, (ox, oy, c)).wait()
        pltpu.make_async_copy(ins[m].at[me], outs[m].at[me], lsem.at[m]).wait()


def _owner_comm(arrs):
    n = len(arrs)
    return dict(ins=arrs, out_shape=[jax.ShapeDtypeStruct(a.shape, a.dtype) for a in arrs],
                sems=[pltpu.SemaphoreType.DMA((3 * n,)), pltpu.SemaphoreType.DMA((3 * n,)), pltpu.SemaphoreType.DMA((n,))],
                start=_owner_starts, wait=_owner_waits)


def _send_to_owners(arrs):
    n = len(arrs)

    def body(*refs):
        ins, outs, sems = refs[:n], refs[n:2 * n], refs[2 * n:]
        _owner_starts(ins, outs, sems)
        _owner_waits(ins, outs, sems)

    job = _owner_comm(arrs)
    return pl.pallas_call(
        body, name="send_to_owners", in_specs=[ANY] * n, out_specs=[ANY] * n,
        out_shape=job["out_shape"], scratch_shapes=job["sems"],
    )(*arrs)


def _sum_chips(a, *, rb):
    nj, r, cdim = a.shape

    def body(a_ref, o_ref):
        f = lambda k: a_ref[k].astype(F32)
        o_ref[...] = ((f(0) + f(1)) + f(2)) + f(3)

    return pl.pallas_call(
        body, name="sum_chips", grid=(r // rb,),
        in_specs=[pl.BlockSpec((nj, rb, cdim), lambda i: (0, i, 0))],
        out_specs=pl.BlockSpec((rb, cdim), lambda i: (i, 0)),
        out_shape=jax.ShapeDtypeStruct((r, cdim), F32),
    )(a)


def _sum_chips_into(a, dest, layer, core, *, rb):
    nj, half, cdim = a.shape
    nb = half // rb

    def body(*refs):
        a_ref, o_ref = refs[1], refs[-1]
        f = lambda k: a_ref[k].astype(F32)
        o_ref[0] = ((f(0) + f(1)) + f(2)) + f(3)

    grid_spec = pltpu.PrefetchScalarGridSpec(
        num_scalar_prefetch=1, grid=(nb,),
        in_specs=[pl.BlockSpec((nj, rb, cdim), lambda i, cr: (0, i, 0))] + ([] if dest is None else [ANY]),
        out_specs=pl.BlockSpec((1, rb, cdim), lambda i, cr: (layer, cr[0] * nb + i, 0)))
    return pl.pallas_call(
        body, name="sum_chips_into", grid_spec=grid_spec,
        out_shape=jax.ShapeDtypeStruct((DEPTH, 2 * half, cdim), F32),
        input_output_aliases={} if dest is None else {2: 0},
    )(*([core, a] if dest is None else [core, a, dest]))


def _spread_reduced(g_in, g_out, red_small):
    hs = red_small.shape[0]

    def body(gin_in, gout_in, sm, gin, gout, fsm, gsm, send_sems, recv_sems, lsem):
        x, y, c, me, others = _place()
        sib = (x, y, 1 - c)
        hi, ho = D_MODEL // 2, SHARD_OUT // 2
        ri, ro = pl.ds(pl.multiple_of(c * hi, SUBLANES), hi), pl.ds(pl.multiple_of(c * ho, SUBLANES), ho)
        remote = [_rcopy(gin.at[:, ri, :], gin.at[:, ri, :], send_sems, recv_sems, 0, sib),
                  _rcopy(gout.at[:, ro, :], gout.at[:, ro, :], send_sems, recv_sems, 1, sib)]
        own_small = pltpu.make_async_copy(sm, gsm.at[me], lsem.at[0])
        small = [_rcopy(sm, gsm.at[me], send_sems, recv_sems, 2 + k, (ox, oy, c)) for k, (ox, oy, _) in enumerate(others)]
        for cp in remote + [own_small] + small:
            cp.start()
        own_small.wait()
        for cp in small:
            cp.wait()
        mine = fsm.at[:, pl.ds(pl.multiple_of(c * hs, SUBLANES), hs), :]
        keep = pltpu.make_async_copy(gsm, mine, lsem.at[1])
        give = _rcopy(gsm, mine, send_sems, recv_sems, 5, sib)
        keep.start()
        give.start()
        for cp in remote + [give]:
            cp.wait()
        keep.wait()

    return pl.pallas_call(
        body, name="spread_reduced", in_specs=[ANY] * 3, out_specs=[ANY] * 4,
        out_shape=[jax.ShapeDtypeStruct(g_in.shape, F32), jax.ShapeDtypeStruct(g_out.shape, F32),
                   jax.ShapeDtypeStruct((N_CHIP, 2 * hs, GW), F32), jax.ShapeDtypeStruct((N_CHIP, hs, GW), F32)],
        input_output_aliases={0: 0, 1: 1},
        scratch_shapes=[pltpu.SemaphoreType.DMA((6,)), pltpu.SemaphoreType.DMA((6,)), pltpu.SemaphoreType.DMA((2,))],
    )(g_in, g_out, red_small)[:3]


def _adamw_math(w, g, m, v):
    m = ADAM_B1 * m + (1.0 - ADAM_B1) * g
    v = ADAM_B2 * v + (1.0 - ADAM_B2) * (g * g)
    m_hat = m / (1.0 - ADAM_B1 ** ADAM_STEP)
    v_hat = v / (1.0 - ADAM_B2 ** ADAM_STEP)
    delta = -ADAM_LR * (m_hat / (jnp.sqrt(v_hat) + ADAM_EPS) + ADAM_WD * w)
    return delta, m, v


def _adamw_big(w, g, m, v, *, rb):
    r, cdim = w.shape

    def body(w_ref, g_ref, m_ref, v_ref, d_ref, nm_ref, nv_ref):
        d_ref[...], nm_ref[...], nv_ref[...] = _adamw_math(w_ref[...], g_ref[...], m_ref[...], v_ref[...])

    spec = pl.BlockSpec((rb, cdim), lambda i: (i, 0))
    return pl.pallas_call(
        body, name="adamw_big", grid=(r // rb,), in_specs=[spec] * 4, out_specs=[spec] * 3,
        out_shape=[jax.ShapeDtypeStruct((r, cdim), F32)] * 3,
    )(w, g, m, v)


def _adamw_small(ws, gs, ms, vs):
    n = len(ws)

    def body(*refs):
        w, g, m, v = refs[:n], refs[n:2 * n], refs[2 * n:3 * n], refs[3 * n:4 * n]
        d, nm, nv = refs[4 * n:5 * n], refs[5 * n:6 * n], refs[6 * n:7 * n]
        for k in range(n):
            d[k][...], nm[k][...], nv[k][...] = _adamw_math(w[k][...], g[k][...], m[k][...], v[k][...])

    shapes = [jax.ShapeDtypeStruct(a.shape, F32) for a in ws]
    outs = pl.pallas_call(body, name="adamw_small", out_shape=shapes * 3)(*ws, *gs, *ms, *vs)
    return outs[:n], outs[n:2 * n], outs[2 * n:]


TT = 256
TK = 512
CW_ROWS = 40
PACK_ROWS = 192


def _pack(rows):
    packed = jnp.concatenate(rows, axis=0)
    packed = jnp.pad(packed, ((0, PACK_ROWS - packed.shape[0]), (0, 0)))
    return packed.reshape(N_CHIP, PACK_ROWS // N_CHIP, GW)


def _reduce_to_owner_halves(parts, core1, tag):
    got = _exchange_halves(parts, tag)
    rbs = {D_MODEL: 256, SHARD_OUT: SHARD_OUT // 2, PACK_ROWS // N_CHIP: PACK_ROWS // N_CHIP // 2}
    return [_add_own_half(a, g, core1, rb=rbs[a.shape[1]], dtype=F32 if a.shape[1] == PACK_ROWS // N_CHIP else BF16)
            for a, g in zip(parts, got)]


def kernel(x, w_in, conv_a_w, conv_a_b, conv_b_w, conv_b_b, ln_b_g, ln_b_b, pool_w, pool_b, pool_scale, w_out, ln_g, ln_b, loss_target, m_w_in, m_conv_a_w, m_conv_a_b, m_conv_b_w, m_conv_b_b, m_ln_b_g, m_ln_b_b, m_pool_w, m_pool_b, m_pool_scale, m_w_out, m_ln_g, m_ln_b, v_w_in, v_conv_a_w, v_conv_a_b, v_conv_b_w, v_conv_b_b, v_ln_b_g, v_ln_b_b, v_pool_w, v_pool_b, v_pool_scale, v_w_out, v_ln_g, v_ln_b):
    chip = 2 * lax.axis_index("x") + lax.axis_index("y")
    core1 = lax.axis_index("c").reshape(1).astype(jnp.int32)
    x2, tgt = x[0], loss_target[0]

    cw = jnp.zeros((DEPTH, CW_ROWS, PGD), F32).at[:, 0:KA].set(conv_a_w).at[:, 8:8 + KB].set(conv_b_w)
    win0_b, wout0_b, cw_all, bsh_in, bsh_out = _gather_weights(w_in, w_out, cw)
    cw_full = jnp.transpose(cw_all, (1, 2, 0, 3)).reshape(DEPTH, CW_ROWS, GW)
    row = lambda a, l: a[l].reshape(1, -1)
    cnt = _count_table()
    prm = [(cw_full[l, 0:KA], row(conv_a_b, l), cw_full[l, 8:8 + KB], row(conv_b_b, l), row(ln_b_g, l), row(ln_b_b, l),
            pool_w[l].astype(BF16), row(pool_b, l), row(pool_scale, l), cnt) for l in range(DEPTH)]

    h0, xb0, z0, x1, win1_b, wout1_b = _fwd_layer(x2, win0_b, wout0_b, prm[0], row(ln_g, 0), row(ln_b, 0), None, tt=TT, last=False,
                                                  comm=_gather_comm(bsh_in, bsh_out, 1))
    h1, xb1, dz1, dln1, loss8 = _fwd_layer(x1, win1_b, wout1_b, prm[1], row(ln_g, 1), row(ln_b, 1), tgt, tt=TT, last=True)

    dz0, dh1, dwout1, small1, dpw1, dln0 = _bwd_layer(dz1, h1, win1_b, wout1_b, prm[1], z0, row(ln_g, 0), tt=TT)
    dwin1 = _wgrad_in(xb1, dh1, tk=TK)
    loss_row = jnp.pad(loss8, ((0, 0), (0, GW - loss8.shape[1])))
    pack1 = _pack([small1, dpw1.reshape(PGD, GW), dln1.reshape(4, GW), dln0.reshape(4, GW), loss_row])
    sums1 = _reduce_to_owner_halves([dwin1, dwout1.reshape(N_CHIP, SHARD_OUT, D_MODEL), pack1], core1, "1")
    gx, dh0, dwout0, small0, dpw0 = _bwd_layer(dz0, h0, win0_b, wout0_b, prm[0], None, None, tt=TT)
    pack0 = _pack([small0, dpw0.reshape(PGD, GW)])
    sums0 = _reduce_to_owner_halves([dwout0.reshape(N_CHIP, SHARD_OUT, D_MODEL), pack0], core1, "0")
    dwin0, *landed = _wgrad_in(xb0, dh0, tk=TK, comm=_owner_comm(sums1 + sums0))
    landed1, landed0 = landed[:3], landed[3:]
    landed0 = list(_send_to_owners(_reduce_to_owner_halves([dwin0], core1, "in0"))) + list(landed0)

    g_in = _sum_chips_into(landed0[0], _sum_chips_into(landed1[0], None, 1, core1, rb=256), 0, core1, rb=256)
    g_out = _sum_chips_into(landed0[1], _sum_chips_into(landed1[1], None, 1, core1, rb=SHARD_OUT // 2), 0, core1, rb=SHARD_OUT // 2)
    red_small = jnp.concatenate([_sum_chips(a, rb=PACK_ROWS // N_CHIP // 2) for a in (landed0[2], landed1[2])], axis=0)
    g_in, g_out, g_small = _spread_reduced(g_in, g_out, red_small)

    flat = lambda a: a.reshape(-1, a.shape[-1])
    unflat = lambda a, like: a.reshape(like.shape)
    d_in, nm_in, nv_in = [unflat(a, w_in) for a in _adamw_big(flat(w_in), flat(g_in), flat(m_w_in), flat(v_w_in), rb=256)]
    d_out, nm_out, nv_out = [unflat(a, w_out) for a in _adamw_big(flat(w_out), flat(g_out), flat(m_w_out), flat(v_w_out), rb=SHARD_OUT)]

    hp = PACK_ROWS // N_CHIP // 2
    unpack = lambda o: jnp.concatenate([g_small[:, o:o + hp], g_small[:, 2 * hp + o:3 * hp + o]], axis=1).reshape(PACK_ROWS, GW)
    p0, p1 = unpack(0), unpack(hp)
    small = [p0[0:N_RACC], p1[0:N_RACC]]
    dpw = [p[N_RACC:N_RACC + PGD].reshape(len(POOL_WINDOWS), PGD, PGD) for p in (p0, p1)]
    o = N_RACC + PGD
    g_lng = jnp.stack([p1[o + 4:o + 8].reshape(2, D_MODEL)[0], p1[o:o + 4].reshape(2, D_MODEL)[0]])
    g_lnb = jnp.stack([p1[o + 4:o + 8].reshape(2, D_MODEL)[1], p1[o:o + 4].reshape(2, D_MODEL)[1]])
    mine = lambda a: lax.dynamic_slice_in_dim(a, chip * PGD, PGD, axis=-1)
    stack = lambda f: jnp.stack([f(0), f(1)])
    g_caw = stack(lambda l: mine(small[l][R_DWA:R_DWA + KA]))
    g_cab = stack(lambda l: small[l][R_DCAB])
    g_cbw = stack(lambda l: mine(small[l][R_DWB:R_DWB + KB]))
    g_cbb = stack(lambda l: small[l][R_DCBB])
    g_lbg = stack(lambda l: small[l][R_DLBG])
    g_lbb = stack(lambda l: small[l][R_DLBB])
    g_pw = stack(lambda l: dpw[l])
    g_pb = stack(lambda l: small[l][R_DPB].reshape(len(POOL_WINDOWS), PGD))
    g_ps = stack(lambda l: small[l][R_DPS])
    ws = [conv_a_w, conv_a_b, conv_b_w, conv_b_b, ln_b_g, ln_b_b, pool_w, pool_b, pool_scale, ln_g, ln_b]
    gs = [g_caw, g_cab, g_cbw, g_cbb, g_lbg, g_lbb, g_pw, g_pb, g_ps, g_lng, g_lnb]
    ms = [m_conv_a_w, m_conv_a_b, m_conv_b_w, m_conv_b_b, m_ln_b_g, m_ln_b_b, m_pool_w, m_pool_b, m_pool_scale, m_ln_g, m_ln_b]
    vs = [v_conv_a_w, v_conv_a_b, v_conv_b_w, v_conv_b_b, v_ln_b_g, v_ln_b_b, v_pool_w, v_pool_b, v_pool_scale, v_ln_g, v_ln_b]
    ds, nms, nvs = _adamw_small([flat(a) for a in ws], [flat(a) for a in gs], [flat(a) for a in ms], [flat(a) for a in vs])
    ds, nms, nvs = ([unflat(a, w) for a, w in zip(t, ws)] for t in (ds, nms, nvs))

    loss = p1[o + 8, 0]

    def order(in_, small_, out_):
        return [in_, *small_[:9], out_, *small_[9:]]
    return (loss, gx[None], *order(g_in, gs, g_out), *order(d_in, ds, d_out), *order(nm_in, nms, nm_out), *order(nv_in, nvs, nv_out))
```

```python
import functools

import jax
import jax.numpy as jnp
import numpy as np
from jax import lax
from jax.experimental import pallas as pl
from jax.experimental.pallas import tpu as pltpu

F32 = jnp.float32
BF16 = jnp.bfloat16

D_MODEL = 1024
DEPTH = 2
GW = 512
D_IN = 9 * GW
D_MIX = 3 * GW
NG = D_IN // GW
POOL_WINDOWS = (2, 4, 8, 16)
PGD = 128
KA = 3
KB = 31
ALPHA = (2.0 * DEPTH) ** 0.25
LN_EPS = 1e-5
ADAM_LR, ADAM_B1, ADAM_B2, ADAM_EPS, ADAM_WD, ADAM_STEP = 0.001, 0.9, 0.999, 1e-08, 0.01, 10

N_CHIP = 4
SHARD_IN = D_IN // N_CHIP
SHARD_OUT = D_MIX // N_CHIP

SUBLANES = 8
RC = 32
HALO = 32
VMEM_LIMIT = 60 * 1024 * 1024

R_DWA, R_DCAB, R_DWB, R_DCBB, R_DLBG, R_DLBB, R_DPB, R_DPS, N_RACC = 0, 3, 4, 35, 36, 37, 38, 39, 40


def _sig(v):
    return 0.5 * jnp.tanh(0.5 * v) + 0.5


def _chunks(n_rows, fn, unroll=1, extra=None):
    def step(m, carry):
        for u in range(unroll):
            fn(pl.multiple_of((m * unroll + u) * RC, RC))
        if extra is not None:
            extra(m)
        return carry
    lax.fori_loop(0, n_rows // (RC * unroll), step, 0)


def _fold8(v):
    return v.reshape(RC // SUBLANES, SUBLANES, v.shape[-1]).sum(axis=0)


def _build_shifts(ext_ref, sh_ref, shifts, n_rows):
    for r in shifts:
        for c0 in range(0, n_rows, RC):
            n = min(RC, n_rows - c0)
            sh_ref[r, pl.ds(c0, n), :] = ext_ref[pl.ds(c0 + r, n), :]


def _tap(ext_ref, sh_ref, off, base, lanes=None):
    a, r = divmod(off, SUBLANES)
    src = ext_ref if r == 0 else sh_ref.at[r]
    if lanes is None:
        return src[pl.ds(base + SUBLANES * a, RC), :]
    return src[pl.ds(base + SUBLANES * a, RC), lanes]


def _ln_stats(v):
    mu = jnp.mean(v, axis=-1, keepdims=True)
    vc = v - mu
    var = jnp.mean(vc * vc, axis=-1, keepdims=True)
    rstd = lax.rsqrt(var + LN_EPS)
    return vc * rstd, rstd


def _ln_bwd(dy, xhat, rstd, g):
    dxh = dy * g
    m1 = jnp.mean(dxh, axis=-1, keepdims=True)
    m2 = jnp.mean(dxh * xhat, axis=-1, keepdims=True)
    return rstd * (dxh - m1 - xhat * m2)


def _for_taps(ext_ref, sh_ref, base, offsets, fn):
    for r in range(SUBLANES):
        offs = [o for o in offsets if o % SUBLANES == r]
        if not offs:
            continue
        a0, a1 = min(offs) // SUBLANES, max(offs) // SUBLANES
        src = ext_ref if r == 0 else sh_ref.at[r]
        win = src[pl.ds(base + SUBLANES * a0, RC + SUBLANES * (a1 - a0)), :]
        for o in offs:
            a = o // SUBLANES - a0
            fn(o, win[SUBLANES * a:SUBLANES * a + RC])


def _count_table():
    t = np.arange(1, RC + 1, dtype=np.float64)[:, None]
    w = np.repeat(np.asarray(POOL_WINDOWS, np.float64), PGD)[None, :]
    return jnp.asarray(1.0 / np.minimum(t, w), F32)


def _inv_count(cnt_ref, first):
    return jnp.where(first, cnt_ref[...], cnt_ref[RC - 1:RC, :])


def _hcol(h_ref, j, base):
    if len(h_ref.shape) == 3:
        return h_ref[j, pl.ds(base, RC), :].astype(F32)
    return h_ref[pl.ds(base, RC), j * GW:(j + 1) * GW].astype(F32)


def _with_comm(comm, ins, in_specs, out_shape, out_specs, scratch):
    if comm is None:
        return ins, in_specs, out_shape, out_specs, scratch
    hbm = pl.BlockSpec(memory_space=pl.ANY)
    return (ins + list(comm["ins"]), in_specs + [hbm] * len(comm["ins"]), out_shape + list(comm["out_shape"]),
            out_specs + [hbm] * len(comm["out_shape"]), scratch + list(comm["sems"]))


def _split_comm(refs, comm, n_in, n_out):
    refs = list(refs)
    if comm is None:
        return refs, None
    ci, co, cs = len(comm["ins"]), len(comm["out_shape"]), len(comm["sems"])
    own = refs[:n_in] + refs[n_in + ci:n_in + ci + n_out] + refs[n_in + ci + n_out + co:len(refs) - cs]
    return own, (refs[n_in:n_in + ci], refs[n_in + ci + n_out:n_in + ci + n_out + co], refs[len(refs) - cs:])


def _fwd_mixers(h_ref, cb_ref, y_scr, q_ext, ub_ext, cu_ext, sh, p_scr, pl_scr, prm, tt, t0):
    caw, cab, cbw, cbb, lbg, lbb, pw, pb, ps, cnt = prm

    def a1(base):
        q_ext[pl.ds(SUBLANES + base, RC), :] = _hcol(h_ref, 1, base) * _hcol(h_ref, 2, base)
    _chunks(tt, a1)
    _build_shifts(q_ext, sh, (6, 7), tt)

    def a2(base):
        ca = cab[...] + caw[0:1, :] * _tap(q_ext, sh, 6, base) + caw[1:2, :] * _tap(q_ext, sh, 7, base) \
            + caw[2:3, :] * _tap(q_ext, sh, 8, base)
        z = _hcol(h_ref, 3, base)
        y_scr[pl.ds(base, RC), 0:GW] = (_hcol(h_ref, 0, base) * ca * (z * _sig(z))).astype(BF16)
    _chunks(tt, a2, unroll=2)
    q_ext[0:SUBLANES, :] = q_ext[tt:tt + SUBLANES, :]

    def b1(base):
        ub_ext[pl.ds(HALO + base, RC), :] = _hcol(h_ref, 4, base) * _sig(_hcol(h_ref, 5, base))
    _chunks(tt, b1)
    _build_shifts(ub_ext, sh, range(1, 8), tt + HALO - SUBLANES)

    def b2(base):
        cb = [cbb[...] + jnp.zeros((RC, GW), F32)]

        def tap(off, v):
            cb[0] = cb[0] + cbw[off - 2:off - 1, :] * v
        _for_taps(ub_ext, sh, base, range(2, 2 + KB), tap)
        cbr = cb[0].astype(BF16)
        cb_ref[pl.ds(base, RC), :] = cbr
        xhat, _ = _ln_stats(cbr.astype(F32))
        lnv = xhat * lbg[...] + lbb[...]
        z = _hcol(h_ref, 6, base)
        y_scr[pl.ds(base, RC), GW:2 * GW] = (lnv * _sig(lnv) * (z * _sig(z))).astype(BF16)
    _chunks(tt, b2, unroll=2)
    ub_ext[0:HALO, :] = ub_ext[tt:tt + HALO, :]

    def c1(base):
        cu_ext[pl.ds(16 + base, RC), :] = _hcol(h_ref, 7, base)
    _chunks(tt, c1)
    _build_shifts(cu_ext, sh, range(1, 8), tt + SUBLANES)

    def c2(base):
        ic = _inv_count(cnt, base + t0 == 0)
        for g, w in enumerate(POOL_WINDOWS):
            lanes = slice(g * PGD, (g + 1) * PGD)
            acc = _tap(cu_ext, sh, 16, base, lanes)
            for j in range(1, w):
                acc = acc + _tap(cu_ext, sh, 16 - j, base, lanes)
            p = acc * ic[:, lanes] - _tap(cu_ext, sh, 16, base, lanes)
            p_scr[pl.ds(base, RC), lanes] = p.astype(BF16)
    _chunks(tt, c2)
    cu_ext[0:16, :] = cu_ext[tt:tt + 16, :]
    for g in range(len(POOL_WINDOWS)):
        lanes = slice(g * PGD, (g + 1) * PGD)
        pl_scr[:, lanes] = jnp.dot(p_scr[:, lanes], pw[g], preferred_element_type=F32)

    def c3(base):
        z = _hcol(h_ref, 8, base)
        yc0 = (pl_scr[pl.ds(base, RC), :] + pb[...]) * ps[...]
        y_scr[pl.ds(base, RC), 2 * GW:3 * GW] = (yc0 * (z * _sig(z))).astype(BF16)
    _chunks(tt, c3, unroll=2)


def _fwd_layer(x, win_b, wout_b, prm, ln_g, ln_b, target, *, tt, last, comm=None):
    t_len = x.shape[0]
    n_t = t_len // tt

    def body(*refs):
        refs, comm_refs = _split_comm(refs, comm, n_in, n_out)
        if last:
            (x_ref, win_hbm, wout_hbm, caw, cab, cbw, cbb, lbg, lbb, pw, pb, ps, cnt, lng, lnb, tgt_ref,
             h_ref, xb_ref, cb_ref, dz_ref, dln_ref, loss_ref,
             win_v, wout_v, y_scr, o_scr, q_ext, ub_ext, cu_ext, sh, p_scr, pl_scr, acc2, lacc) = refs
        else:
            (x_ref, win_hbm, wout_hbm, caw, cab, cbw, cbb, lbg, lbb, pw, pb, ps, cnt, lng, lnb,
             h_ref, xb_ref, cb_ref, z_ref, xn_ref,
             win_v, wout_v, y_scr, o_scr, q_ext, ub_ext, cu_ext, sh, p_scr, pl_scr) = refs
        i = pl.program_id(0)

        @pl.when(i == 0)
        def _():
            if comm is not None:
                comm["start"](*comm_refs)
            pltpu.sync_copy(win_hbm, win_v)
            pltpu.sync_copy(wout_hbm, wout_v)
            q_ext[0:SUBLANES, :] = jnp.zeros((SUBLANES, GW), F32)
            ub_ext[0:HALO, :] = jnp.zeros((HALO, GW), F32)
            cu_ext[0:16, :] = jnp.zeros((16, GW), F32)
            if last:
                acc2[...] = jnp.zeros_like(acc2)
                lacc[...] = jnp.zeros_like(lacc)

        xb_ref[...] = x_ref[...].astype(BF16)
        for j in range(NG):
            h_ref[:, j * GW:(j + 1) * GW] = jnp.dot(
                xb_ref[...], win_v[:, j * GW:(j + 1) * GW], preferred_element_type=F32).astype(BF16)

        _fwd_mixers(h_ref, cb_ref, y_scr, q_ext, ub_ext, cu_ext, sh, p_scr, pl_scr,
                    (caw, cab, cbw, cbb, lbg, lbb, pw, pb, ps, cnt), tt, i * tt)

        o_scr[...] = jnp.dot(y_scr[...], wout_v[...], preferred_element_type=F32)

        def post(base):
            rows = pl.ds(base, RC)
            z = ALPHA * x_ref[rows, :] + o_scr[rows, :]
            xhat, rstd = _ln_stats(z)
            xn = xhat * lng[...] + lnb[...]
            if last:
                err = xn - tgt_ref[rows, :]
                lacc[...] += _fold8(err * err)
                dxn = err * (1.0 / D_MODEL)
                acc2[0] += _fold8(dxn * xhat)
                acc2[1] += _fold8(dxn)
                dz_ref[rows, :] = _ln_bwd(dxn, xhat, rstd, lng[...])
            else:
                z_ref[rows, :] = z
                xn_ref[rows, :] = xn
        _chunks(tt, post, unroll=2)

        if last:
            @pl.when(i == n_t - 1)
            def _():
                dln_ref[...] = jnp.sum(acc2[...], axis=1)
                loss_ref[...] = jnp.zeros((SUBLANES, 128), F32) + (0.5 / D_MODEL) * jnp.sum(lacc[...])
        if comm is not None:
            @pl.when(i == n_t - 1)
            def _():
                comm["wait"](*comm_refs)

    tile = lambda c: pl.BlockSpec((tt, c), lambda i: (i, 0))
    full = lambda a: pl.BlockSpec(a.shape, lambda i: (0,) * a.ndim)
    hbm = pl.BlockSpec(memory_space=pl.ANY)
    ins = [x, win_b, wout_b, *prm, ln_g, ln_b] + ([target] if last else [])
    in_specs = [tile(D_MODEL), hbm, hbm] + [full(a) for a in (*prm, ln_g, ln_b)] + ([tile(D_MODEL)] if last else [])
    out_shape = [jax.ShapeDtypeStruct((t_len, D_IN), BF16), jax.ShapeDtypeStruct((t_len, D_MODEL), BF16),
                 jax.ShapeDtypeStruct((t_len, GW), BF16)]
    out_specs = [tile(D_IN), tile(D_MODEL), tile(GW)]
    if last:
        out_shape += [jax.ShapeDtypeStruct((t_len, D_MODEL), F32), jax.ShapeDtypeStruct((2, D_MODEL), F32),
                      jax.ShapeDtypeStruct((SUBLANES, 128), F32)]
        out_specs += [tile(D_MODEL), pl.BlockSpec((2, D_MODEL), lambda i: (0, 0)),
                      pl.BlockSpec((SUBLANES, 128), lambda i: (0, 0))]
    else:
        out_shape += [jax.ShapeDtypeStruct((t_len, D_MODEL), F32), jax.ShapeDtypeStruct((t_len, D_MODEL), F32)]
        out_specs += [tile(D_MODEL), tile(D_MODEL)]
    scratch = [
        pltpu.VMEM((D_MODEL, D_IN), BF16), pltpu.VMEM((D_MIX, D_MODEL), BF16),
        pltpu.VMEM((tt, D_MIX), BF16), pltpu.VMEM((tt, D_MODEL), F32),
        pltpu.VMEM((tt + SUBLANES, GW), F32), pltpu.VMEM((tt + HALO, GW), F32), pltpu.VMEM((tt + 16, GW), F32),
        pltpu.VMEM((SUBLANES, tt + HALO, GW), F32),
        pltpu.VMEM((tt, GW), BF16), pltpu.VMEM((tt, GW), F32),
    ]
    if last:
        scratch += [pltpu.VMEM((2, SUBLANES, D_MODEL), F32), pltpu.VMEM((SUBLANES, D_MODEL), F32)]
    n_in, n_out = len(ins), len(out_shape)
    ins, in_specs, out_shape, out_specs, scratch = _with_comm(comm, ins, in_specs, out_shape, out_specs, scratch)
    return pl.pallas_call(
        body, name=("fwd_last" if last else "fwd_layer") + ("" if comm is None else "_comm"), grid=(n_t,),
        in_specs=in_specs, out_specs=out_specs, out_shape=out_shape, scratch_shapes=scratch,
        compiler_params=pltpu.CompilerParams(dimension_semantics=("arbitrary",), vmem_limit_bytes=VMEM_LIMIT),
    )(*ins)


def _dsilu(z, sz):
    return sz * (1.0 + z * (1.0 - sz))


def _bwd_layer(dz, h, cb, win_b, wout_b, prm, z_prev, lng_prev, *, tt, comm=None):
    t_len = dz.shape[0]
    n_t = t_len // tt
    has_prev = z_prev is not None
    hb = tt // HALO

    def body(*refs):
        refs, comm_refs = _split_comm(refs, comm, n_in, n_out)
        dz_ref, h_ref, halo_ref, cb_ref, win_hbm, wout_hbm, caw, cab, cbw, cbb, lbg, lbb, pw, pb, ps, cnt = refs[:16]
        k = 16
        if has_prev:
            zp_ref, lngp = refs[k:k + 2]
            k += 2
        dxo_ref, dh_ref, dwout_hbm, small_ref, dpw_ref = refs[k:k + 5]
        k += 5
        if has_prev:
            dlnp_ref = refs[k]
            k += 1
        (win_v, wout_v, dzb, dy_scr, y_scr, dx_scr, q_ext, ub_ext, cu_ext, dca_ext, dcb_ext, dpn_ext, sh,
         p_scr, pl_scr, dpl_scr, dp_scr, racc, dpw_acc, dwout_acc) = refs[k:k + 20]
        k += 20
        if has_prev:
            acc2 = refs[k]
        i = pl.program_id(0)
        ti = n_t - 1 - i
        t0 = ti * tt

        @pl.when(i == 0)
        def _():
            if comm is not None:
                comm["start"](*comm_refs)
            pltpu.sync_copy(win_hbm, win_v)
            pltpu.sync_copy(wout_hbm, wout_v)
            dca_ext[tt:tt + SUBLANES, :] = jnp.zeros((SUBLANES, GW), F32)
            dcb_ext[tt:tt + HALO, :] = jnp.zeros((HALO, GW), F32)
            dpn_ext[tt:tt + 16, :] = jnp.zeros((16, GW), F32)
            racc[...] = jnp.zeros_like(racc)
            dpw_acc[...] = jnp.zeros_like(dpw_acc)
            dwout_acc[...] = jnp.zeros_like(dwout_acc)
            if has_prev:
                acc2[...] = jnp.zeros_like(acc2)

        dzb[...] = dz_ref[...].astype(BF16)
        dy_scr[...] = lax.dot_general(dzb[...], wout_v[...], (((1,), (1,)), ((), ())), preferred_element_type=F32)

        live = (ti > 0).astype(F32)
        hh = lambda j, r0, r1: halo_ref[r0:r1, j * GW:(j + 1) * GW].astype(F32)
        q_ext[0:SUBLANES, :] = live * hh(1, 24, 32) * hh(2, 24, 32)
        ub_ext[0:HALO, :] = live * hh(4, 0, 32) * _sig(hh(5, 0, 32))
        cu_ext[0:16, :] = live * hh(7, 16, 32)

        def a1(base):
            q_ext[pl.ds(SUBLANES + base, RC), :] = _hcol(h_ref, 1, base) * _hcol(h_ref, 2, base)
        _chunks(tt, a1)
        _build_shifts(q_ext, sh, (6, 7), tt)

        def a2(base):
            rows = pl.ds(base, RC)
            q6, q7, q8 = _tap(q_ext, sh, 6, base), _tap(q_ext, sh, 7, base), _tap(q_ext, sh, 8, base)
            ca = cab[...] + caw[0:1, :] * q6 + caw[1:2, :] * q7 + caw[2:3, :] * q8
            bg, z = _hcol(h_ref, 0, base), _hcol(h_ref, 3, base)
            sz = _sig(z)
            sza = z * sz
            dya = dy_scr[rows, 0:GW]
            ya0 = bg * ca
            y_scr[rows, 0:GW] = (ya0 * sza).astype(BF16)
            dya0 = dya * sza
            dh_ref[rows, 3 * GW:4 * GW] = (dya * ya0 * _dsilu(z, sz)).astype(BF16)
            dh_ref[rows, 0:GW] = (dya0 * ca).astype(BF16)
            dca = dya0 * bg
            dca_ext[rows, :] = dca
            racc[R_DCAB] += _fold8(dca)
            racc[R_DWA + 0] += _fold8(dca * q6)
            racc[R_DWA + 1] += _fold8(dca * q7)
            racc[R_DWA + 2] += _fold8(dca * q8)
        _chunks(tt, a2)
        _build_shifts(dca_ext, sh, (1, 2), tt)

        def a3(base):
            rows = pl.ds(base, RC)
            dq = caw[0:1, :] * _tap(dca_ext, sh, 2, base) + caw[1:2, :] * _tap(dca_ext, sh, 1, base) \
                + caw[2:3, :] * _tap(dca_ext, sh, 0, base)
            dh_ref[rows, GW:2 * GW] = (dq * _hcol(h_ref, 2, base)).astype(BF16)
            dh_ref[rows, 2 * GW:3 * GW] = (dq * _hcol(h_ref, 1, base)).astype(BF16)
        _chunks(tt, a3)
        dca_ext[tt:tt + SUBLANES, :] = dca_ext[0:SUBLANES, :]

        def b1(base):
            ub_ext[pl.ds(HALO + base, RC), :] = _hcol(h_ref, 4, base) * _sig(_hcol(h_ref, 5, base))
        _chunks(tt, b1)
        _build_shifts(ub_ext, sh, range(1, 8), tt + HALO - SUBLANES)

        def b2(base):
            rows = pl.ds(base, RC)
            xhat, rstd = _ln_stats(cb_ref[rows, :].astype(F32))
            lnv = xhat * lbg[...] + lbb[...]
            sl = _sig(lnv)
            s = lnv * sl
            z = _hcol(h_ref, 6, base)
            sz = _sig(z)
            szb = z * sz
            y_scr[rows, GW:2 * GW] = (s * szb).astype(BF16)
            dyb = dy_scr[rows, GW:2 * GW]
            dh_ref[rows, 6 * GW:7 * GW] = (dyb * s * _dsilu(z, sz)).astype(BF16)
            dlnv = dyb * szb * _dsilu(lnv, sl)
            racc[R_DLBG] += _fold8(dlnv * xhat)
            racc[R_DLBB] += _fold8(dlnv)
            dcb = _ln_bwd(dlnv, xhat, rstd, lbg[...])
            dcb_ext[rows, :] = dcb
            racc[R_DCBB] += _fold8(dcb)

            def wtap(off, v):
                racc[R_DWB + off - 2] += _fold8(dcb * v)
            _for_taps(ub_ext, sh, base, range(2, 2 + KB), wtap)
        _chunks(tt, b2, unroll=2)
        _build_shifts(dcb_ext, sh, range(1, 8), tt + HALO - SUBLANES)

        def b3(base):
            rows = pl.ds(base, RC)
            dubv = [jnp.zeros((RC, GW), F32)]

            def tap(off, v):
                dubv[0] = dubv[0] + cbw[KB - 1 - off:KB - off, :] * v
            _for_taps(dcb_ext, sh, base, range(KB), tap)
            dub = dubv[0]
            v, gt = _hcol(h_ref, 4, base), _hcol(h_ref, 5, base)
            sg = _sig(gt)
            dh_ref[rows, 4 * GW:5 * GW] = (dub * sg).astype(BF16)
            dh_ref[rows, 5 * GW:6 * GW] = (dub * v * sg * (1.0 - sg)).astype(BF16)
        _chunks(tt, b3)
        dcb_ext[tt:tt + HALO, :] = dcb_ext[0:HALO, :]

        def c1(base):
            cu_ext[pl.ds(16 + base, RC), :] = _hcol(h_ref, 7, base)
        _chunks(tt, c1)
        _build_shifts(cu_ext, sh, range(1, 8), tt + SUBLANES)

        def c2(base):
            ic = _inv_count(cnt, base + t0 == 0)
            for g, w in enumerate(POOL_WINDOWS):
                lanes = slice(g * PGD, (g + 1) * PGD)
                acc = _tap(cu_ext, sh, 16, base, lanes)
                for j in range(1, w):
                    acc = acc + _tap(cu_ext, sh, 16 - j, base, lanes)
                p = acc * ic[:, lanes] - _tap(cu_ext, sh, 16, base, lanes)
                p_scr[pl.ds(base, RC), lanes] = p.astype(BF16)
        _chunks(tt, c2)
        for g in range(len(POOL_WINDOWS)):
            lanes = slice(g * PGD, (g + 1) * PGD)
            pl_scr[:, lanes] = jnp.dot(p_scr[:, lanes], pw[g], preferred_element_type=F32)

        def c3(base):
            rows = pl.ds(base, RC)
            z = _hcol(h_ref, 8, base)
            sz = _sig(z)
            szc = z * sz
            plb = pl_scr[rows, :] + pb[...]
            yc0 = plb * ps[...]
            y_scr[rows, 2 * GW:3 * GW] = (yc0 * szc).astype(BF16)
            dyc = dy_scr[rows, 2 * GW:3 * GW]
            dh_ref[rows, 8 * GW:9 * GW] = (dyc * yc0 * _dsilu(z, sz)).astype(BF16)
            dyc0 = dyc * szc
            racc[R_DPS] += _fold8(dyc0 * plb)
            dpl = dyc0 * ps[...]
            racc[R_DPB] += _fold8(dpl)
            dpl_scr[rows, :] = dpl.astype(BF16)
        _chunks(tt, c3)
        for g in range(len(POOL_WINDOWS)):
            lanes = slice(g * PGD, (g + 1) * PGD)
            dpw_acc[g] += lax.dot_general(p_scr[:, lanes], dpl_scr[:, lanes], (((0,), (0,)), ((), ())),
                                          preferred_element_type=F32)
            dp_scr[:, lanes] = lax.dot_general(dpl_scr[:, lanes], pw[g], (((1,), (1,)), ((), ())),
                                               preferred_element_type=F32)

        def c4(base):
            rows = pl.ds(base, RC)
            dpn_ext[rows, :] = dp_scr[rows, :] * _inv_count(cnt, base + t0 == 0)
        _chunks(tt, c4)
        _build_shifts(dpn_ext, sh, range(1, 8), tt + SUBLANES)

        def c5(base):
            rows = pl.ds(base, RC)
            for g, w in enumerate(POOL_WINDOWS):
                lanes = slice(g * PGD, (g + 1) * PGD)
                acc = _tap(dpn_ext, sh, 0, base, lanes)
                for j in range(1, w):
                    acc = acc + _tap(dpn_ext, sh, j, base, lanes)
                dh_ref[rows, 7 * GW + g * PGD:7 * GW + (g + 1) * PGD] = (acc - dp_scr[rows, lanes]).astype(BF16)
        _chunks(tt, c5)
        dpn_ext[tt:tt + 16, :] = dpn_ext[0:16, :]

        for r in range(D_MIX // GW):
            dwout_acc[r * GW:(r + 1) * GW, :] += lax.dot_general(
                y_scr[:, r * GW:(r + 1) * GW], dzb[...], (((0,), (0,)), ((), ())), preferred_element_type=F32)
        dx_scr[...] = lax.dot_general(dh_ref[...], win_v[...], (((1,), (1,)), ((), ())), preferred_element_type=F32)

        def post(base):
            rows = pl.ds(base, RC)
            dx = ALPHA * dz_ref[rows, :] + dx_scr[rows, :]
            if has_prev:
                xhat, rstd = _ln_stats(zp_ref[rows, :])
                acc2[0] += _fold8(dx * xhat)
                acc2[1] += _fold8(dx)
                dxo_ref[rows, :] = _ln_bwd(dx, xhat, rstd, lngp[...])
            else:
                dxo_ref[rows, :] = dx
        _chunks(tt, post, unroll=2)

        @pl.when(i == n_t - 1)
        def _():
            small_ref[...] = jnp.sum(racc[...], axis=1)
            dpw_ref[...] = dpw_acc[...]
            pltpu.sync_copy(dwout_acc, dwout_hbm)
            if has_prev:
                dlnp_ref[...] = jnp.sum(acc2[...], axis=1)
            if comm is not None:
                comm["wait"](*comm_refs)

    rtile = lambda c: pl.BlockSpec((tt, c), lambda i: (n_t - 1 - i, 0))
    full = lambda a: pl.BlockSpec(a.shape, lambda i: (0,) * a.ndim)
    const = lambda shp: pl.BlockSpec(shp, lambda i: (0,) * len(shp))
    hbm = pl.BlockSpec(memory_space=pl.ANY)
    halo_spec = pl.BlockSpec((HALO, D_IN), lambda i: (jnp.maximum((n_t - 1 - i) * hb - 1, 0), 0))
    ins = [dz, h, h, cb, win_b, wout_b, *prm] + ([z_prev, lng_prev] if has_prev else [])
    in_specs = [rtile(D_MODEL), rtile(D_IN), halo_spec, rtile(GW), hbm, hbm] + [full(a) for a in prm] \
        + ([rtile(D_MODEL), full(lng_prev)] if has_prev else [])
    out_shape = [jax.ShapeDtypeStruct((t_len, D_MODEL), F32), jax.ShapeDtypeStruct((t_len, D_IN), BF16),
                 jax.ShapeDtypeStruct((D_MIX, D_MODEL), F32), jax.ShapeDtypeStruct((N_RACC, GW), F32),
                 jax.ShapeDtypeStruct((len(POOL_WINDOWS), PGD, PGD), F32)]
    out_specs = [rtile(D_MODEL), rtile(D_IN), hbm, const((N_RACC, GW)), const((len(POOL_WINDOWS), PGD, PGD))]
    if has_prev:
        out_shape.append(jax.ShapeDtypeStruct((2, D_MODEL), F32))
        out_specs.append(const((2, D_MODEL)))
    scratch = [
        pltpu.VMEM((D_MODEL, D_IN), BF16), pltpu.VMEM((D_MIX, D_MODEL), BF16),
        pltpu.VMEM((tt, D_MODEL), BF16), pltpu.VMEM((tt, D_MIX), F32), pltpu.VMEM((tt, D_MIX), BF16),
        pltpu.VMEM((tt, D_MODEL), F32),
        pltpu.VMEM((tt + SUBLANES, GW), F32), pltpu.VMEM((tt + HALO, GW), F32), pltpu.VMEM((tt + 16, GW), F32),
        pltpu.VMEM((tt + SUBLANES, GW), F32), pltpu.VMEM((tt + HALO, GW), F32), pltpu.VMEM((tt + 16, GW), F32),
        pltpu.VMEM((SUBLANES, tt + HALO, GW), F32),
        pltpu.VMEM((tt, GW), BF16), pltpu.VMEM((tt, GW), F32), pltpu.VMEM((tt, GW), BF16), pltpu.VMEM((tt, GW), F32),
        pltpu.VMEM((N_RACC, SUBLANES, GW), F32), pltpu.VMEM((len(POOL_WINDOWS), PGD, PGD), F32),
        pltpu.VMEM((D_MIX, D_MODEL), F32),
    ]
    if has_prev:
        scratch.append(pltpu.VMEM((2, SUBLANES, D_MODEL), F32))
    n_in, n_out = len(ins), len(out_shape)
    ins, in_specs, out_shape, out_specs, scratch = _with_comm(comm, ins, in_specs, out_shape, out_specs, scratch)
    return pl.pallas_call(
        body, name=("bwd_layer_prev" if has_prev else "bwd_layer") + ("" if comm is None else "_comm"), grid=(n_t,),
        in_specs=in_specs, out_specs=out_specs, out_shape=out_shape, scratch_shapes=scratch,
        compiler_params=pltpu.CompilerParams(dimension_semantics=("arbitrary",), vmem_limit_bytes=VMEM_LIMIT),
    )(*ins)


def _wgrad_in(xb, dh, *, tk, comm=None):
    t_len = xb.shape[0]
    n_k = t_len // tk

    def body(*refs):
        (x_ref, dh_ref, o_ref), comm_refs = _split_comm(refs, comm, 2, 1)
        j, k = pl.program_id(0), pl.program_id(1)

        @pl.when(k == 0)
        def _():
            o_ref[...] = jnp.zeros_like(o_ref)
        if comm is not None:
            @pl.when((j == 0) & (k == 0))
            def _():
                comm["start"](*comm_refs)
        o_ref[0] += lax.dot_general(x_ref[...], dh_ref[...], (((0,), (0,)), ((), ())), preferred_element_type=F32)
        if comm is not None:
            @pl.when((j == N_CHIP - 1) & (k == n_k - 1))
            def _():
                comm["wait"](*comm_refs)

    ins = [xb, dh]
    in_specs = [pl.BlockSpec((tk, D_MODEL), lambda j, k: (k, 0)), pl.BlockSpec((tk, SHARD_IN), lambda j, k: (k, j))]
    out_shape = [jax.ShapeDtypeStruct((N_CHIP, D_MODEL, SHARD_IN), F32)]
    out_specs = [pl.BlockSpec((1, D_MODEL, SHARD_IN), lambda j, k: (j, 0, 0))]
    ins, in_specs, out_shape, out_specs, scratch = _with_comm(comm, ins, in_specs, out_shape, out_specs, [])
    outs = pl.pallas_call(
        body, name="wgrad_in" + ("" if comm is None else "_comm"), grid=(N_CHIP, n_k),
        in_specs=in_specs, out_specs=out_specs, out_shape=out_shape, scratch_shapes=scratch,
        compiler_params=pltpu.CompilerParams(dimension_semantics=("arbitrary", "arbitrary"), vmem_limit_bytes=VMEM_LIMIT),
    )(*ins)
    return outs[0] if comm is None else outs


MESH = pl.DeviceIdType.MESH
ANY = pl.BlockSpec(memory_space=pl.ANY)


def _place():
    x, y, c = lax.axis_index("x"), lax.axis_index("y"), lax.axis_index("c")
    others = [(1 - x, y), (x, 1 - y), (1 - x, 1 - y)]
    return x, y, c, 2 * x + y, [(ox, oy, 2 * ox + oy) for ox, oy in others]


def _rcopy(src, dst, send_sems, recv_sems, k, dev):
    return pltpu.make_async_remote_copy(src_ref=src, dst_ref=dst, send_sem=send_sems.at[k], recv_sem=recv_sems.at[k],
                                        device_id=dev, device_id_type=MESH)


def _gather_weights(w_in, w_out, cw):
    hi, ho = D_MODEL // 2, SHARD_OUT // 2

    def body(win_ref, wout_ref, cw_ref, owin, owout, ocw, bin_v, bout_v, send_sems, recv_sems, lsem):
        x, y, c, me, others = _place()
        for l in range(DEPTH):
            for r0 in range(0, D_MODEL, 256):
                bin_v[l, r0:r0 + 256, :] = win_ref[l, r0:r0 + 256, :].astype(BF16)
            bout_v[l] = wout_ref[l].astype(BF16)
        cin = pl.ds(pl.multiple_of(me * SHARD_IN, 128), SHARD_IN)
        rout = pl.ds(pl.multiple_of(me * SHARD_OUT, 128), SHARD_OUT)
        local = [pltpu.make_async_copy(bin_v.at[0], owin.at[:, cin], lsem.at[0]),
                 pltpu.make_async_copy(bout_v.at[0], owout.at[rout, :], lsem.at[1]),
                 pltpu.make_async_copy(cw_ref, ocw.at[me], lsem.at[2])]
        for cp in local:
            cp.start()

        def in_half(chip, core):
            return owin.at[pl.ds(pl.multiple_of(core * hi, 256), hi), pl.ds(pl.multiple_of(chip * SHARD_IN, 128), SHARD_IN)]

        def out_half(chip, core):
            return owout.at[pl.ds(pl.multiple_of(chip * SHARD_OUT + core * ho, 64), ho), :]

        first = []
        for k, (ox, oy, _) in enumerate(others):
            dev = (ox, oy, c)
            first.append(_rcopy(bin_v.at[0, pl.ds(pl.multiple_of(c * hi, 256), hi), :], in_half(me, c), send_sems, recv_sems, k, dev))
            first.append(_rcopy(bout_v.at[0, pl.ds(pl.multiple_of(c * ho, 64), ho), :], out_half(me, c), send_sems, recv_sems, 3 + k, dev))
            first.append(_rcopy(cw_ref, ocw.at[me], send_sems, recv_sems, 6 + k, dev))
        for cp in first:
            cp.start()
        sib = (x, y, 1 - c)
        passed = []
        for k, (ox, oy, oc) in enumerate(others):
            _rcopy(in_half(oc, c), in_half(oc, c), send_sems, recv_sems, k, sib).wait_recv()
            fwd_in = _rcopy(in_half(oc, c), in_half(oc, c), send_sems, recv_sems, 9 + k, sib)
            fwd_in.start()
            _rcopy(out_half(oc, c), out_half(oc, c), send_sems, recv_sems, 3 + k, sib).wait_recv()
            fwd_out = _rcopy(out_half(oc, c), out_half(oc, c), send_sems, recv_sems, 12 + k, sib)
            fwd_out.start()
            passed += [fwd_in, fwd_out]
        for k, (ox, oy, oc) in enumerate(others):
            _rcopy(cw_ref, ocw.at[oc], send_sems, recv_sems, 6 + k, sib).wait_recv()
            _rcopy(in_half(oc, 1 - c), in_half(oc, 1 - c), send_sems, recv_sems, 9 + k, sib).wait_recv()
            _rcopy(out_half(oc, 1 - c), out_half(oc, 1 - c), send_sems, recv_sems, 12 + k, sib).wait_recv()
        for cp in first + passed:
            cp.wait_send()
        for cp in local:
            cp.wait()

    vm = pl.BlockSpec(memory_space=pltpu.VMEM)
    return pl.pallas_call(
        body, name="gather_weights",
        in_specs=[vm, vm, vm], out_specs=[ANY, ANY, ANY, vm, vm],
        out_shape=[jax.ShapeDtypeStruct((D_MODEL, D_IN), BF16), jax.ShapeDtypeStruct((D_MIX, D_MODEL), BF16),
                   jax.ShapeDtypeStruct((N_CHIP,) + cw.shape, F32),
                   jax.ShapeDtypeStruct((DEPTH, D_MODEL, SHARD_IN), BF16), jax.ShapeDtypeStruct((DEPTH, SHARD_OUT, D_MODEL), BF16)],
        scratch_shapes=[pltpu.SemaphoreType.DMA((15,)), pltpu.SemaphoreType.DMA((15,)), pltpu.SemaphoreType.DMA((3,))],
        compiler_params=pltpu.CompilerParams(vmem_limit_bytes=VMEM_LIMIT),
    )(w_in, w_out, cw)


def _gather_starts(bsh_in, bsh_out, owin, owout, send_sems, recv_sems, lsem, layer):
    x, y, c, me, others = _place()
    hi, ho = D_MODEL // 2, SHARD_OUT // 2
    pltpu.make_async_copy(bsh_in.at[layer], owin.at[:, pl.ds(pl.multiple_of(me * SHARD_IN, 128), SHARD_IN)], lsem.at[0]).start()
    pltpu.make_async_copy(bsh_out.at[layer], owout.at[pl.ds(pl.multiple_of(me * SHARD_OUT, 128), SHARD_OUT), :], lsem.at[1]).start()
    for k, (ox, oy, _) in enumerate(others):
        for t in range(2):
            pltpu.make_async_remote_copy(
                src_ref=bsh_in.at[layer, pl.ds(pl.multiple_of(c * hi, 256), hi), :],
                dst_ref=owin.at[pl.ds(pl.multiple_of(c * hi, 256), hi), pl.ds(pl.multiple_of(me * SHARD_IN, 128), SHARD_IN)],
                send_sem=send_sems.at[2 * k + t], recv_sem=recv_sems.at[2 * k + c], device_id=(ox, oy, t), device_id_type=MESH).start()
            pltpu.make_async_remote_copy(
                src_ref=bsh_out.at[layer, pl.ds(pl.multiple_of(c * ho, 64), ho), :],
                dst_ref=owout.at[pl.ds(pl.multiple_of(me * SHARD_OUT + c * ho, 64), ho), :],
                send_sem=send_sems.at[6 + 2 * k + t], recv_sem=recv_sems.at[6 + 2 * k + c], device_id=(ox, oy, t), device_id_type=MESH).start()


def _gather_waits(bsh_in, bsh_out, owin, owout, send_sems, recv_sems, lsem, layer):
    x, y, c, me, others = _place()
    hi, ho = D_MODEL // 2, SHARD_OUT // 2
    src_in = bsh_in.at[layer, pl.ds(0, hi), :]
    src_out = bsh_out.at[layer, pl.ds(0, ho), :]
    for k, (ox, oy, oc) in enumerate(others):
        for t in range(2):
            dst_in = owin.at[pl.ds(t * hi, hi), pl.ds(pl.multiple_of(oc * SHARD_IN, 128), SHARD_IN)]
            dst_out = owout.at[pl.ds(pl.multiple_of(oc * SHARD_OUT + t * ho, 64), ho), :]
            a = pltpu.make_async_remote_copy(src_ref=src_in, dst_ref=dst_in, send_sem=send_sems.at[2 * k + t],
                                             recv_sem=recv_sems.at[2 * k + t], device_id=(ox, oy, t), device_id_type=MESH)
            b = pltpu.make_async_remote_copy(src_ref=src_out, dst_ref=dst_out, send_sem=send_sems.at[6 + 2 * k + t],
                                             recv_sem=recv_sems.at[6 + 2 * k + t], device_id=(ox, oy, t), device_id_type=MESH)
            a.wait_send()
            a.wait_recv()
            b.wait_send()
            b.wait_recv()
    pltpu.make_async_copy(bsh_in.at[layer], owin.at[:, pl.ds(pl.multiple_of(me * SHARD_IN, 128), SHARD_IN)], lsem.at[0]).wait()
    pltpu.make_async_copy(bsh_out.at[layer], owout.at[pl.ds(pl.multiple_of(me * SHARD_OUT, 128), SHARD_OUT), :], lsem.at[1]).wait()


def _gather_comm(bsh_in, bsh_out, layer):
    return dict(ins=[bsh_in, bsh_out],
                out_shape=[jax.ShapeDtypeStruct((D_MODEL, D_IN), BF16), jax.ShapeDtypeStruct((D_MIX, D_MODEL), BF16)],
                sems=[pltpu.SemaphoreType.DMA((12,)), pltpu.SemaphoreType.DMA((12,)), pltpu.SemaphoreType.DMA((2,))],
                start=lambda ins, outs, sems: _gather_starts(ins[0], ins[1], outs[0], outs[1], *sems, layer),
                wait=lambda ins, outs, sems: _gather_waits(ins[0], ins[1], outs[0], outs[1], *sems, layer))


def _exchange_halves(arrs, tag):
    n = len(arrs)

    def body(*refs):
        ins, outs, (send_sems, recv_sems) = refs[:n], refs[n:2 * n], refs[2 * n:]
        x, y, c, _, _ = _place()
        cps = []
        for m in range(n):
            half = ins[m].shape[1] // 2
            cps.append(_rcopy(ins[m].at[:, pl.ds(pl.multiple_of((1 - c) * half, SUBLANES), half), :], outs[m],
                              send_sems, recv_sems, m, (x, y, 1 - c)))
        for cp in cps:
            cp.start()
        for cp in cps:
            cp.wait()

    return pl.pallas_call(
        body, name="exchange_halves_" + tag, in_specs=[ANY] * n, out_specs=[ANY] * n,
        out_shape=[jax.ShapeDtypeStruct((a.shape[0], a.shape[1] // 2, a.shape[2]), F32) for a in arrs],
        scratch_shapes=[pltpu.SemaphoreType.DMA((n,)), pltpu.SemaphoreType.DMA((n,))],
    )(*arrs)


def _add_own_half(a, got, core, *, rb, dtype):
    nj, r, cdim = a.shape
    half = r // 2

    def body(core_ref, a_ref, g_ref, o_ref):
        o_ref[...] = (a_ref[0] + g_ref[...]).astype(dtype)

    return pl.pallas_call(
        body, name="add_own_half",
        grid_spec=pltpu.PrefetchScalarGridSpec(
            num_scalar_prefetch=1, grid=(nj, half // rb),
            in_specs=[pl.BlockSpec((1, 1, rb, cdim), lambda j, i, cr: (j, cr[0], i, 0)),
                      pl.BlockSpec((1, rb, cdim), lambda j, i, cr: (j, i, 0))],
            out_specs=pl.BlockSpec((1, rb, cdim), lambda j, i, cr: (j, i, 0))),
        out_shape=jax.ShapeDtypeStruct((nj, half, cdim), dtype),
    )(core, a.reshape(nj, 2, half, cdim), got)


def _owner_starts(ins, outs, sems):
    send_sems, recv_sems, lsem = sems
    x, y, c, me, others = _place()
    for m in range(len(ins)):
        pltpu.make_async_copy(ins[m].at[me], outs[m].at[me], lsem.at[m]).start()
        for k, (ox, oy, oc) in enumerate(others):
            _rcopy(ins[m].at[oc], outs[m].at[me], send_sems, recv_sems, 3 * m + k, (ox, oy, c)).start()


def _owner_waits(ins, outs, sems):
    send_sems, recv_sems, lsem = sems
    x, y, c, me, others = _place()
    for m in range(len(ins)):
        for k, (ox, oy, oc) in enumerate(others):
            _rcopy(ins[m].at[oc], outs[m].at[oc], send_sems, recv_sems, 3 * m + k, (ox, oy, c)).wait()
        pltpu.make_async_copy(ins[m].at[me], outs[m].at[me], lsem.at[m]).wait()


def _owner_comm(arrs):
    n = len(arrs)
    return dict(ins=arrs, out_shape=[jax.ShapeDtypeStruct(a.shape, a.dtype) for a in arrs],
                sems=[pltpu.SemaphoreType.DMA((3 * n,)), pltpu.SemaphoreType.DMA((3 * n,)), pltpu.SemaphoreType.DMA((n,))],
                start=_owner_starts, wait=_owner_waits)


def _send_to_owners(arrs):
    n = len(arrs)

    def body(*refs):
        ins, outs, sems = refs[:n], refs[n:2 * n], refs[2 * n:]
        _owner_starts(ins, outs, sems)
        _owner_waits(ins, outs, sems)

    job = _owner_comm(arrs)
    return pl.pallas_call(
        body, name="send_to_owners", in_specs=[ANY] * n, out_specs=[ANY] * n,
        out_shape=job["out_shape"], scratch_shapes=job["sems"],
    )(*arrs)


def _sum_chips(a, *, rb):
    nj, r, cdim = a.shape

    def body(a_ref, o_ref):
        f = lambda k: a_ref[k].astype(F32)
        o_ref[...] = ((f(0) + f(1)) + f(2)) + f(3)

    return pl.pallas_call(
        body, name="sum_chips", grid=(r // rb,),
        in_specs=[pl.BlockSpec((nj, rb, cdim), lambda i: (0, i, 0))],
        out_specs=pl.BlockSpec((rb, cdim), lambda i: (i, 0)),
        out_shape=jax.ShapeDtypeStruct((r, cdim), F32),
    )(a)


def _sum_chips_into(a, dest, layer, core, *, rb):
    nj, half, cdim = a.shape
    nb = half // rb

    def body(*refs):
        a_ref, o_ref = refs[1], refs[-1]
        f = lambda k: a_ref[k].astype(F32)
        o_ref[0] = ((f(0) + f(1)) + f(2)) + f(3)

    grid_spec = pltpu.PrefetchScalarGridSpec(
        num_scalar_prefetch=1, grid=(nb,),
        in_specs=[pl.BlockSpec((nj, rb, cdim), lambda i, cr: (0, i, 0))] + ([] if dest is None else [ANY]),
        out_specs=pl.BlockSpec((1, rb, cdim), lambda i, cr: (layer, cr[0] * nb + i, 0)))
    return pl.pallas_call(
        body, name="sum_chips_into", grid_spec=grid_spec,
        out_shape=jax.ShapeDtypeStruct((DEPTH, 2 * half, cdim), F32),
        input_output_aliases={} if dest is None else {2: 0},
    )(*([core, a] if dest is None else [core, a, dest]))


def _spread_reduced(g_in, g_out, red_small):
    hs = red_small.shape[0]

    def body(gin_in, gout_in, sm, gin, gout, fsm, gsm, send_sems, recv_sems, lsem):
        x, y, c, me, others = _place()
        sib = (x, y, 1 - c)
        hi, ho = D_MODEL // 2, SHARD_OUT // 2
        ri, ro = pl.ds(pl.multiple_of(c * hi, SUBLANES), hi), pl.ds(pl.multiple_of(c * ho, SUBLANES), ho)
        remote = [_rcopy(gin.at[:, ri, :], gin.at[:, ri, :], send_sems, recv_sems, 0, sib),
                  _rcopy(gout.at[:, ro, :], gout.at[:, ro, :], send_sems, recv_sems, 1, sib)]
        own_small = pltpu.make_async_copy(sm, gsm.at[me], lsem.at[0])
        small = [_rcopy(sm, gsm.at[me], send_sems, recv_sems, 2 + k, (ox, oy, c)) for k, (ox, oy, _) in enumerate(others)]
        for cp in remote + [own_small] + small:
            cp.start()
        own_small.wait()
        for cp in small:
            cp.wait()
        mine = fsm.at[:, pl.ds(pl.multiple_of(c * hs, SUBLANES), hs), :]
        keep = pltpu.make_async_copy(gsm, mine, lsem.at[1])
        give = _rcopy(gsm, mine, send_sems, recv_sems, 5, sib)
        keep.start()
        give.start()
        for cp in remote + [give]:
            cp.wait()
        keep.wait()

    return pl.pallas_call(
        body, name="spread_reduced", in_specs=[ANY] * 3, out_specs=[ANY] * 4,
        out_shape=[jax.ShapeDtypeStruct(g_in.shape, F32), jax.ShapeDtypeStruct(g_out.shape, F32),
                   jax.ShapeDtypeStruct((N_CHIP, 2 * hs, GW), F32), jax.ShapeDtypeStruct((N_CHIP, hs, GW), F32)],
        input_output_aliases={0: 0, 1: 1},
        scratch_shapes=[pltpu.SemaphoreType.DMA((6,)), pltpu.SemaphoreType.DMA((6,)), pltpu.SemaphoreType.DMA((2,))],
    )(g_in, g_out, red_small)[:3]


def _adamw_math(w, g, m, v):
    m = ADAM_B1 * m + (1.0 - ADAM_B1) * g
    v = ADAM_B2 * v + (1.0 - ADAM_B2) * (g * g)
    m_hat = m / (1.0 - ADAM_B1 ** ADAM_STEP)
    v_hat = v / (1.0 - ADAM_B2 ** ADAM_STEP)
    delta = -ADAM_LR * (m_hat / (jnp.sqrt(v_hat) + ADAM_EPS) + ADAM_WD * w)
    return delta, m, v


def _adamw_big(w, g, m, v, *, rb):
    r, cdim = w.shape

    def body(w_ref, g_ref, m_ref, v_ref, d_ref, nm_ref, nv_ref):
        d_ref[...], nm_ref[...], nv_ref[...] = _adamw_math(w_ref[...], g_ref[...], m_ref[...], v_ref[...])

    spec = pl.BlockSpec((rb, cdim), lambda i: (i, 0))
    return pl.pallas_call(
        body, name="adamw_big", grid=(r // rb,), in_specs=[spec] * 4, out_specs=[spec] * 3,
        out_shape=[jax.ShapeDtypeStruct((r, cdim), F32)] * 3,
    )(w, g, m, v)


def _adamw_small(ws, gs, ms, vs):
    n = len(ws)

    def body(*refs):
        w, g, m, v = refs[:n], refs[n:2 * n], refs[2 * n:3 * n], refs[3 * n:4 * n]
        d, nm, nv = refs[4 * n:5 * n], refs[5 * n:6 * n], refs[6 * n:7 * n]
        for k in range(n):
            d[k][...], nm[k][...], nv[k][...] = _adamw_math(w[k][...], g[k][...], m[k][...], v[k][...])

    shapes = [jax.ShapeDtypeStruct(a.shape, F32) for a in ws]
    outs = pl.pallas_call(body, name="adamw_small", out_shape=shapes * 3)(*ws, *gs, *ms, *vs)
    return outs[:n], outs[n:2 * n], outs[2 * n:]


TT = 256
TK = 512
CW_ROWS = 40
PACK_ROWS = 192


def _pack(rows):
    packed = jnp.concatenate(rows, axis=0)
    packed = jnp.pad(packed, ((0, PACK_ROWS - packed.shape[0]), (0, 0)))
    return packed.reshape(N_CHIP, PACK_ROWS // N_CHIP, GW)


def _reduce_to_owner_halves(parts, core1, tag):
    got = _exchange_halves(parts, tag)
    rbs = {D_MODEL: 256, SHARD_OUT: SHARD_OUT // 2, PACK_ROWS // N_CHIP: PACK_ROWS // N_CHIP // 2}
    return [_add_own_half(a, g, core1, rb=rbs[a.shape[1]], dtype=F32 if a.shape[1] == PACK_ROWS // N_CHIP else BF16)
            for a, g in zip(parts, got)]


def kernel(x, w_in, conv_a_w, conv_a_b, conv_b_w, conv_b_b, ln_b_g, ln_b_b, pool_w, pool_b, pool_scale, w_out, ln_g, ln_b, loss_target, m_w_in, m_conv_a_w, m_conv_a_b, m_conv_b_w, m_conv_b_b, m_ln_b_g, m_ln_b_b, m_pool_w, m_pool_b, m_pool_scale, m_w_out, m_ln_g, m_ln_b, v_w_in, v_conv_a_w, v_conv_a_b, v_conv_b_w, v_conv_b_b, v_ln_b_g, v_ln_b_b, v_pool_w, v_pool_b, v_pool_scale, v_w_out, v_ln_g, v_ln_b):
    chip = 2 * lax.axis_index("x") + lax.axis_index("y")
    core1 = lax.axis_index("c").reshape(1).astype(jnp.int32)
    x2, tgt = x[0], loss_target[0]

    cw = jnp.zeros((DEPTH, CW_ROWS, PGD), F32).at[:, 0:KA].set(conv_a_w).at[:, 8:8 + KB].set(conv_b_w)
    win0_b, wout0_b, cw_all, bsh_in, bsh_out = _gather_weights(w_in, w_out, cw)
    cw_full = jnp.transpose(cw_all, (1, 2, 0, 3)).reshape(DEPTH, CW_ROWS, GW)
    row = lambda a, l: a[l].reshape(1, -1)
    cnt = _count_table()
    prm = [(cw_full[l, 0:KA], row(conv_a_b, l), cw_full[l, 8:8 + KB], row(conv_b_b, l), row(ln_b_g, l), row(ln_b_b, l),
            pool_w[l].astype(BF16), row(pool_b, l), row(pool_scale, l), cnt) for l in range(DEPTH)]

    h0, xb0, cb0, z0, x1, win1_b, wout1_b = _fwd_layer(x2, win0_b, wout0_b, prm[0], row(ln_g, 0), row(ln_b, 0), None, tt=TT, last=False,
                                                  comm=_gather_comm(bsh_in, bsh_out, 1))
    h1, xb1, cb1, dz1, dln1, loss8 = _fwd_layer(x1, win1_b, wout1_b, prm[1], row(ln_g, 1), row(ln_b, 1), tgt, tt=TT, last=True)

    dz0, dh1, dwout1, small1, dpw1, dln0 = _bwd_layer(dz1, h1, cb1, win1_b, wout1_b, prm[1], z0, row(ln_g, 0), tt=TT)
    dwin1 = _wgrad_in(xb1, dh1, tk=TK)
    loss_row = jnp.pad(loss8, ((0, 0), (0, GW - loss8.shape[1])))
    pack1 = _pack([small1, dpw1.reshape(PGD, GW), dln1.reshape(4, GW), dln0.reshape(4, GW), loss_row])
    sums1 = _reduce_to_owner_halves([dwin1, dwout1.reshape(N_CHIP, SHARD_OUT, D_MODEL), pack1], core1, "1")
    gx, dh0, dwout0, small0, dpw0 = _bwd_layer(dz0, h0, cb0, win0_b, wout0_b, prm[0], None, None, tt=TT)
    pack0 = _pack([small0, dpw0.reshape(PGD, GW)])
    sums0 = _reduce_to_owner_halves([dwout0.reshape(N_CHIP, SHARD_OUT, D_MODEL), pack0], core1, "0")
    dwin0, *landed = _wgrad_in(xb0, dh0, tk=TK, comm=_owner_comm(sums1 + sums0))
    landed1, landed0 = landed[:3], landed[3:]
    landed0 = list(_send_to_owners(_reduce_to_owner_halves([dwin0], core1, "in0"))) + list(landed0)

    g_in = _sum_chips_into(landed0[0], _sum_chips_into(landed1[0], None, 1, core1, rb=256), 0, core1, rb=256)
    g_out = _sum_chips_into(landed0[1], _sum_chips_into(landed1[1], None, 1, core1, rb=SHARD_OUT // 2), 0, core1, rb=SHARD_OUT // 2)
    red_small = jnp.concatenate([_sum_chips(a, rb=PACK_ROWS // N_CHIP // 2) for a in (landed0[2], landed1[2])], axis=0)
    g_in, g_out, g_small = _spread_reduced(g_in, g_out, red_small)

    flat = lambda a: a.reshape(-1, a.shape[-1])
    unflat = lambda a, like: a.reshape(like.shape)
    d_in, nm_in, nv_in = [unflat(a, w_in) for a in _adamw_big(flat(w_in), flat(g_in), flat(m_w_in), flat(v_w_in), rb=256)]
    d_out, nm_out, nv_out = [unflat(a, w_out) for a in _adamw_big(flat(w_out), flat(g_out), flat(m_w_out), flat(v_w_out), rb=SHARD_OUT)]

    hp = PACK_ROWS // N_CHIP // 2
    unpack = lambda o: jnp.concatenate([g_small[:, o:o + hp], g_small[:, 2 * hp + o:3 * hp + o]], axis=1).reshape(PACK_ROWS, GW)
    p0, p1 = unpack(0), unpack(hp)
    small = [p0[0:N_RACC], p1[0:N_RACC]]
    dpw = [p[N_RACC:N_RACC + PGD].reshape(len(POOL_WINDOWS), PGD, PGD) for p in (p0, p1)]
    o = N_RACC + PGD
    g_lng = jnp.stack([p1[o + 4:o + 8].reshape(2, D_MODEL)[0], p1[o:o + 4].reshape(2, D_MODEL)[0]])
    g_lnb = jnp.stack([p1[o + 4:o + 8].reshape(2, D_MODEL)[1], p1[o:o + 4].reshape(2, D_MODEL)[1]])
    mine = lambda a: lax.dynamic_slice_in_dim(a, chip * PGD, PGD, axis=-1)
    stack = lambda f: jnp.stack([f(0), f(1)])
    g_caw = stack(lambda l: mine(small[l][R_DWA:R_DWA + KA]))
    g_cab = stack(lambda l: small[l][R_DCAB])
    g_cbw = stack(lambda l: mine(small[l][R_DWB:R_DWB + KB]))
    g_cbb = stack(lambda l: small[l][R_DCBB])
    g_lbg = stack(lambda l: small[l][R_DLBG])
    g_lbb = stack(lambda l: small[l][R_DLBB])
    g_pw = stack(lambda l: dpw[l])
    g_pb = stack(lambda l: small[l][R_DPB].reshape(len(POOL_WINDOWS), PGD))
    g_ps = stack(lambda l: small[l][R_DPS])
    ws = [conv_a_w, conv_a_b, conv_b_w, conv_b_b, ln_b_g, ln_b_b, pool_w, pool_b, pool_scale, ln_g, ln_b]
    gs = [g_caw, g_cab, g_cbw, g_cbb, g_lbg, g_lbb, g_pw, g_pb, g_ps, g_lng, g_lnb]
    ms = [m_conv_a_w, m_conv_a_b, m_conv_b_w, m_conv_b_b, m_ln_b_g, m_ln_b_b, m_pool_w, m_pool_b, m_pool_scale, m_ln_g, m_ln_b]
    vs = [v_conv_a_w, v_conv_a_b, v_conv_b_w, v_conv_b_b, v_ln_b_g, v_ln_b_b, v_pool_w, v_pool_b, v_pool_scale, v_ln_g, v_ln_b]
    ds, nms, nvs = _adamw_small([flat(a) for a in ws], [flat(a) for a in gs], [flat(a) for a in ms], [flat(a) for a in vs])
    ds, nms, nvs = ([unflat(a, w) for a, w in zip(t, ws)] for t in (ds, nms, nvs))

    loss = p1[o + 8, 0]

    def order(in_, small_, out_):
        return [in_, *small_[:9], out_, *small_[9:]]
    return (loss, gx[None], *order(g_in, gs, g_out), *order(d_in, ds, d_out), *order(nm_in, nms, nm_out), *order(nv_in, nvs, nv_out))
```

```python
import functools

import jax
import jax.numpy as jnp
import numpy as np
from jax import lax
from jax.experimental import pallas as pl
from jax.experimental.pallas import tpu as pltpu

F32 = jnp.float32
BF16 = jnp.bfloat16

D_MODEL = 1024
DEPTH = 2
GW = 512
D_IN = 9 * GW
D_MIX = 3 * GW
NG = D_IN // GW
POOL_WINDOWS = (2, 4, 8, 16)
PGD = 128
KA = 3
KB = 31
ALPHA = (2.0 * DEPTH) ** 0.25
LN_EPS = 1e-5
ADAM_LR, ADAM_B1, ADAM_B2, ADAM_EPS, ADAM_WD, ADAM_STEP = 0.001, 0.9, 0.999, 1e-08, 0.01, 10

N_CHIP = 4
SHARD_IN = D_IN // N_CHIP
SHARD_OUT = D_MIX // N_CHIP

SUBLANES = 8
RC = 32
HALO = 32
VMEM_LIMIT = 60 * 1024 * 1024
WOUT_GROUP = 4

R_DWA, R_DCAB, R_DWB, R_DCBB, R_DLBG, R_DLBB, R_DPB, R_DPS, N_RACC = 0, 3, 4, 35, 36, 37, 38, 39, 40


def _sig(v):
    return 0.5 * jnp.tanh(0.5 * v) + 0.5


def _chunks(n_rows, fn, unroll=1, extra=None):
    def step(m, carry):
        for u in range(unroll):
            fn(pl.multiple_of((m * unroll + u) * RC, RC))
        if extra is not None:
            extra(m)
        return carry
    lax.fori_loop(0, n_rows // (RC * unroll), step, 0)


def _fold8(v):
    return v.reshape(RC // SUBLANES, SUBLANES, v.shape[-1]).sum(axis=0)


def _build_shifts(ext_ref, sh_ref, shifts, n_rows):
    for r in shifts:
        for c0 in range(0, n_rows, RC):
            n = min(RC, n_rows - c0)
            sh_ref[r, pl.ds(c0, n), :] = ext_ref[pl.ds(c0 + r, n), :]


def _tap(ext_ref, sh_ref, off, base, lanes=None):
    a, r = divmod(off, SUBLANES)
    src = ext_ref if r == 0 else sh_ref.at[r]
    if lanes is None:
        return src[pl.ds(base + SUBLANES * a, RC), :]
    return src[pl.ds(base + SUBLANES * a, RC), lanes]


def _ln_stats(v):
    mu = jnp.mean(v, axis=-1, keepdims=True)
    vc = v - mu
    var = jnp.mean(vc * vc, axis=-1, keepdims=True)
    rstd = lax.rsqrt(var + LN_EPS)
    return vc * rstd, rstd


def _ln_bwd(dy, xhat, rstd, g):
    dxh = dy * g
    m1 = jnp.mean(dxh, axis=-1, keepdims=True)
    m2 = jnp.mean(dxh * xhat, axis=-1, keepdims=True)
    return rstd * (dxh - m1 - xhat * m2)


def _for_taps(ext_ref, sh_ref, base, offsets, fn):
    for r in range(SUBLANES):
        offs = [o for o in offsets if o % SUBLANES == r]
        if not offs:
            continue
        a0, a1 = min(offs) // SUBLANES, max(offs) // SUBLANES
        src = ext_ref if r == 0 else sh_ref.at[r]
        win = src[pl.ds(base + SUBLANES * a0, RC + SUBLANES * (a1 - a0)), :]
        for o in offs:
            a = o // SUBLANES - a0
            fn(o, win[SUBLANES * a:SUBLANES * a + RC])


def _count_table():
    t = np.arange(1, RC + 1, dtype=np.float64)[:, None]
    w = np.repeat(np.asarray(POOL_WINDOWS, np.float64), PGD)[None, :]
    return jnp.asarray(1.0 / np.minimum(t, w), F32)


def _inv_count(cnt_ref, first):
    return jnp.where(first, cnt_ref[...], cnt_ref[RC - 1:RC, :])


def _hcol(h_ref, j, base):
    if len(h_ref.shape) == 3:
        return h_ref[j, pl.ds(base, RC), :].astype(F32)
    return h_ref[pl.ds(base, RC), j * GW:(j + 1) * GW].astype(F32)


def _with_comm(comm, ins, in_specs, out_shape, out_specs, scratch):
    if comm is None:
        return ins, in_specs, out_shape, out_specs, scratch
    hbm = pl.BlockSpec(memory_space=pl.ANY)
    return (ins + list(comm["ins"]), in_specs + [hbm] * len(comm["ins"]), out_shape + list(comm["out_shape"]),
            out_specs + [hbm] * len(comm["out_shape"]), scratch + list(comm["sems"]))


def _split_comm(refs, comm, n_in, n_out):
    refs = list(refs)
    if comm is None:
        return refs, None
    ci, co, cs = len(comm["ins"]), len(comm["out_shape"]), len(comm["sems"])
    own = refs[:n_in] + refs[n_in + ci:n_in + ci + n_out] + refs[n_in + ci + n_out + co:len(refs) - cs]
    return own, (refs[n_in:n_in + ci], refs[n_in + ci + n_out:n_in + ci + n_out + co], refs[len(refs) - cs:])


def _fwd_mixers(h_ref, cb_ref, y_scr, q_ext, ub_ext, cu_ext, sh, p_scr, pl_scr, prm, tt, t0):
    caw, cab, cbw, cbb, lbg, lbb, pw, pb, ps, cnt = prm

    def a1(base):
        q_ext[pl.ds(SUBLANES + base, RC), :] = _hcol(h_ref, 1, base) * _hcol(h_ref, 2, base)
    _chunks(tt, a1)
    _build_shifts(q_ext, sh, (6, 7), tt)

    def a2(base):
        ca = cab[...] + caw[0:1, :] * _tap(q_ext, sh, 6, base) + caw[1:2, :] * _tap(q_ext, sh, 7, base) \
            + caw[2:3, :] * _tap(q_ext, sh, 8, base)
        z = _hcol(h_ref, 3, base)
        y_scr[pl.ds(base, RC), 0:GW] = (_hcol(h_ref, 0, base) * ca * (z * _sig(z))).astype(BF16)
    _chunks(tt, a2, unroll=2)
    q_ext[0:SUBLANES, :] = q_ext[tt:tt + SUBLANES, :]

    def b1(base):
        ub_ext[pl.ds(HALO + base, RC), :] = _hcol(h_ref, 4, base) * _sig(_hcol(h_ref, 5, base))
    _chunks(tt, b1)
    _build_shifts(ub_ext, sh, range(1, 8), tt + HALO - SUBLANES)

    def b2(base):
        cb = [cbb[...] + jnp.zeros((RC, GW), F32)]

        def tap(off, v):
            cb[0] = cb[0] + cbw[off - 2:off - 1, :] * v
        _for_taps(ub_ext, sh, base, range(2, 2 + KB), tap)
        cbr = cb[0].astype(BF16)
        cb_ref[pl.ds(base, RC), :] = cbr
        xhat, _ = _ln_stats(cbr.astype(F32))
        lnv = xhat * lbg[...] + lbb[...]
        z = _hcol(h_ref, 6, base)
        y_scr[pl.ds(base, RC), GW:2 * GW] = (lnv * _sig(lnv) * (z * _sig(z))).astype(BF16)
    _chunks(tt, b2, unroll=2)
    ub_ext[0:HALO, :] = ub_ext[tt:tt + HALO, :]

    def c1(base):
        cu_ext[pl.ds(16 + base, RC), :] = _hcol(h_ref, 7, base)
    _chunks(tt, c1)
    _build_shifts(cu_ext, sh, range(1, 8), tt + SUBLANES)

    def c2(base):
        ic = _inv_count(cnt, base + t0 == 0)
        for g, w in enumerate(POOL_WINDOWS):
            lanes = slice(g * PGD, (g + 1) * PGD)
            acc = _tap(cu_ext, sh, 16, base, lanes)
            for j in range(1, w):
                acc = acc + _tap(cu_ext, sh, 16 - j, base, lanes)
            p = acc * ic[:, lanes] - _tap(cu_ext, sh, 16, base, lanes)
            p_scr[pl.ds(base, RC), lanes] = p.astype(BF16)
    _chunks(tt, c2)
    cu_ext[0:16, :] = cu_ext[tt:tt + 16, :]
    for g in range(len(POOL_WINDOWS)):
        lanes = slice(g * PGD, (g + 1) * PGD)
        pl_scr[:, lanes] = jnp.dot(p_scr[:, lanes], pw[g], preferred_element_type=F32)

    def c3(base):
        z = _hcol(h_ref, 8, base)
        yc0 = (pl_scr[pl.ds(base, RC), :] + pb[...]) * ps[...]
        y_scr[pl.ds(base, RC), 2 * GW:3 * GW] = (yc0 * (z * _sig(z))).astype(BF16)
    _chunks(tt, c3, unroll=2)


def _fwd_layer(x, win_b, wout_b, prm, ln_g, ln_b, target, *, tt, last, comm=None):
    t_len = x.shape[0]
    n_t = t_len // tt

    def body(*refs):
        refs, comm_refs = _split_comm(refs, comm, n_in, n_out)
        if last:
            (x_ref, win_hbm, wout_hbm, caw, cab, cbw, cbb, lbg, lbb, pw, pb, ps, cnt, lng, lnb, tgt_ref,
             h_ref, xb_ref, cb_ref, p_scr, dz_ref, dln_ref, loss_ref,
             win_v, wout_v, y_scr, o_scr, q_ext, ub_ext, cu_ext, sh, pl_scr, acc2, lacc) = refs
        else:
            (x_ref, win_hbm, wout_hbm, caw, cab, cbw, cbb, lbg, lbb, pw, pb, ps, cnt, lng, lnb,
             h_ref, xb_ref, cb_ref, p_scr, z_ref, xn_ref,
             win_v, wout_v, y_scr, o_scr, q_ext, ub_ext, cu_ext, sh, pl_scr) = refs
        i = pl.program_id(0)

        @pl.when(i == 0)
        def _():
            if comm is not None:
                comm["start"](*comm_refs)
            pltpu.sync_copy(win_hbm, win_v)
            pltpu.sync_copy(wout_hbm, wout_v)
            q_ext[0:SUBLANES, :] = jnp.zeros((SUBLANES, GW), F32)
            ub_ext[0:HALO, :] = jnp.zeros((HALO, GW), F32)
            cu_ext[0:16, :] = jnp.zeros((16, GW), F32)
            if last:
                acc2[...] = jnp.zeros_like(acc2)
                lacc[...] = jnp.zeros_like(lacc)

        xb_ref[...] = x_ref[...].astype(BF16)
        for j in range(NG):
            h_ref[:, j * GW:(j + 1) * GW] = jnp.dot(
                xb_ref[...], win_v[:, j * GW:(j + 1) * GW], preferred_element_type=F32).astype(BF16)

        _fwd_mixers(h_ref, cb_ref, y_scr, q_ext, ub_ext, cu_ext, sh, p_scr, pl_scr,
                    (caw, cab, cbw, cbb, lbg, lbb, pw, pb, ps, cnt), tt, i * tt)

        o_scr[...] = jnp.dot(y_scr[...], wout_v[...], preferred_element_type=F32)

        def post(base):
            rows = pl.ds(base, RC)
            z = ALPHA * x_ref[rows, :] + o_scr[rows, :]
            xhat, rstd = _ln_stats(z)
            xn = xhat * lng[...] + lnb[...]
            if last:
                err = xn - tgt_ref[rows, :]
                lacc[...] += _fold8(err * err)
                dxn = err * (1.0 / D_MODEL)
                acc2[0] += _fold8(dxn * xhat)
                acc2[1] += _fold8(dxn)
                dz_ref[rows, :] = _ln_bwd(dxn, xhat, rstd, lng[...])
            else:
                z_ref[rows, :] = z
                xn_ref[rows, :] = xn
        _chunks(tt, post, unroll=2)

        if last:
            @pl.when(i == n_t - 1)
            def _():
                dln_ref[...] = jnp.sum(acc2[...], axis=1)
                loss_ref[...] = jnp.zeros((SUBLANES, 128), F32) + (0.5 / D_MODEL) * jnp.sum(lacc[...])
        if comm is not None:
            @pl.when(i == n_t - 1)
            def _():
                comm["wait"](*comm_refs)

    tile = lambda c: pl.BlockSpec((tt, c), lambda i: (i, 0))
    full = lambda a: pl.BlockSpec(a.shape, lambda i: (0,) * a.ndim)
    hbm = pl.BlockSpec(memory_space=pl.ANY)
    ins = [x, win_b, wout_b, *prm, ln_g, ln_b] + ([target] if last else [])
    in_specs = [tile(D_MODEL), hbm, hbm] + [full(a) for a in (*prm, ln_g, ln_b)] + ([tile(D_MODEL)] if last else [])
    out_shape = [jax.ShapeDtypeStruct((t_len, D_IN), BF16), jax.ShapeDtypeStruct((t_len, D_MODEL), BF16),
                 jax.ShapeDtypeStruct((t_len, GW), BF16), jax.ShapeDtypeStruct((t_len, GW), BF16)]
    out_specs = [tile(D_IN), tile(D_MODEL), tile(GW), tile(GW)]
    if last:
        out_shape += [jax.ShapeDtypeStruct((t_len, D_MODEL), F32), jax.ShapeDtypeStruct((2, D_MODEL), F32),
                      jax.ShapeDtypeStruct((SUBLANES, 128), F32)]
        out_specs += [tile(D_MODEL), pl.BlockSpec((2, D_MODEL), lambda i: (0, 0)),
                      pl.BlockSpec((SUBLANES, 128), lambda i: (0, 0))]
    else:
        out_shape += [jax.ShapeDtypeStruct((t_len, D_MODEL), F32), jax.ShapeDtypeStruct((t_len, D_MODEL), F32)]
        out_specs += [tile(D_MODEL), tile(D_MODEL)]
    scratch = [
        pltpu.VMEM((D_MODEL, D_IN), BF16), pltpu.VMEM((D_MIX, D_MODEL), BF16),
        pltpu.VMEM((tt, D_MIX), BF16), pltpu.VMEM((tt, D_MODEL), F32),
        pltpu.VMEM((tt + SUBLANES, GW), F32), pltpu.VMEM((tt + HALO, GW), F32), pltpu.VMEM((tt + 16, GW), F32),
        pltpu.VMEM((SUBLANES, tt + HALO, GW), F32),
        pltpu.VMEM((tt, GW), F32),
    ]
    if last:
        scratch += [pltpu.VMEM((2, SUBLANES, D_MODEL), F32), pltpu.VMEM((SUBLANES, D_MODEL), F32)]
    n_in, n_out = len(ins), len(out_shape)
    ins, in_specs, out_shape, out_specs, scratch = _with_comm(comm, ins, in_specs, out_shape, out_specs, scratch)
    return pl.pallas_call(
        body, name=("fwd_last" if last else "fwd_layer") + ("" if comm is None else "_comm"), grid=(n_t,),
        in_specs=in_specs, out_specs=out_specs, out_shape=out_shape, scratch_shapes=scratch,
        compiler_params=pltpu.CompilerParams(dimension_semantics=("arbitrary",), vmem_limit_bytes=VMEM_LIMIT),
    )(*ins)


def _dsilu(z, sz):
    return sz * (1.0 + z * (1.0 - sz))


def _bwd_layer(dz, h, cb, p, win_b, wout_b, prm, z_prev, lng_prev, *, tt, comm=None):
    t_len = dz.shape[0]
    n_t = t_len // tt
    has_prev = z_prev is not None
    hb = tt // HALO
    group = min(WOUT_GROUP, n_t)
    assert n_t % group == 0

    def body(*refs):
        refs, comm_refs = _split_comm(refs, comm, n_in, n_out)
        dz_ref, h_ref, halo_ref, cb_ref, p_scr, win_hbm, wout_hbm, caw, cab, cbw, cbb, lbg, lbb, pw, pb, ps, cnt = refs[:17]
        k = 17
        if has_prev:
            zp_ref, lngp = refs[k:k + 2]
            k += 2
        dxo_ref, dh_ref, dwout_hbm, small_ref, dpw_ref = refs[k:k + 5]
        k += 5
        if has_prev:
            dlnp_ref = refs[k]
            k += 1
        (win_v, wout_v, dzb_all, dy_scr, y_all, dx_scr, q_ext, ub_ext, dca_ext, dcb_ext, dpn_ext, sh,
         pl_scr, dpl_scr, dp_scr, racc, dpw_acc, dwout_acc) = refs[k:k + 18]
        k += 18
        if has_prev:
            acc2 = refs[k]
        i = pl.program_id(0)
        ti = n_t - 1 - i
        t0 = ti * tt
        slot = i % group
        slot_rows = pl.ds(pl.multiple_of(slot * tt, tt), tt)
        dzb, y_scr = dzb_all.at[slot_rows], y_all.at[slot_rows]

        @pl.when(i == 0)
        def _():
            if comm is not None:
                comm["start"](*comm_refs)
            pltpu.sync_copy(win_hbm, win_v)
            pltpu.sync_copy(wout_hbm, wout_v)
            dca_ext[tt:tt + SUBLANES, :] = jnp.zeros((SUBLANES, GW), F32)
            dcb_ext[tt:tt + HALO, :] = jnp.zeros((HALO, GW), F32)
            dpn_ext[tt:tt + 16, :] = jnp.zeros((16, GW), F32)
            racc[...] = jnp.zeros_like(racc)
            dpw_acc[...] = jnp.zeros_like(dpw_acc)
            dwout_acc[...] = jnp.zeros_like(dwout_acc)
            if has_prev:
                acc2[...] = jnp.zeros_like(acc2)

        dzb[...] = dz_ref[...].astype(BF16)
        dy_scr[...] = lax.dot_general(dzb[...], wout_v[...], (((1,), (1,)), ((), ())), preferred_element_type=F32)

        live = (ti > 0).astype(F32)
        hh = lambda j, r0, r1: halo_ref[r0:r1, j * GW:(j + 1) * GW].astype(F32)
        q_ext[0:SUBLANES, :] = live * hh(1, 24, 32) * hh(2, 24, 32)
        ub_ext[0:HALO, :] = live * hh(4, 0, 32) * _sig(hh(5, 0, 32))

        def a1(base):
            q_ext[pl.ds(SUBLANES + base, RC), :] = _hcol(h_ref, 1, base) * _hcol(h_ref, 2, base)
        _chunks(tt, a1)
        _build_shifts(q_ext, sh, (6, 7), tt)

        def a2(base):
            rows = pl.ds(base, RC)
            q6, q7, q8 = _tap(q_ext, sh, 6, base), _tap(q_ext, sh, 7, base), _tap(q_ext, sh, 8, base)
            ca = cab[...] + caw[0:1, :] * q6 + caw[1:2, :] * q7 + caw[2:3, :] * q8
            bg, z = _hcol(h_ref, 0, base), _hcol(h_ref, 3, base)
            sz = _sig(z)
            sza = z * sz
            dya = dy_scr[rows, 0:GW]
            ya0 = bg * ca
            y_scr[rows, 0:GW] = (ya0 * sza).astype(BF16)
            dya0 = dya * sza
            dh_ref[rows, 3 * GW:4 * GW] = (dya * ya0 * _dsilu(z, sz)).astype(BF16)
            dh_ref[rows, 0:GW] = (dya0 * ca).astype(BF16)
            dca = dya0 * bg
            dca_ext[rows, :] = dca
            racc[R_DCAB] += _fold8(dca)
            racc[R_DWA + 0] += _fold8(dca * q6)
            racc[R_DWA + 1] += _fold8(dca * q7)
            racc[R_DWA + 2] += _fold8(dca * q8)
        _chunks(tt, a2)
        _build_shifts(dca_ext, sh, (1, 2), tt)

        def a3(base):
            rows = pl.ds(base, RC)
            dq = caw[0:1, :] * _tap(dca_ext, sh, 2, base) + caw[1:2, :] * _tap(dca_ext, sh, 1, base) \
                + caw[2:3, :] * _tap(dca_ext, sh, 0, base)
            dh_ref[rows, GW:2 * GW] = (dq * _hcol(h_ref, 2, base)).astype(BF16)
            dh_ref[rows, 2 * GW:3 * GW] = (dq * _hcol(h_ref, 1, base)).astype(BF16)
        _chunks(tt, a3)
        dca_ext[tt:tt + SUBLANES, :] = dca_ext[0:SUBLANES, :]

        def b1(base):
            ub_ext[pl.ds(HALO + base, RC), :] = _hcol(h_ref, 4, base) * _sig(_hcol(h_ref, 5, base))
        _chunks(tt, b1)
        _build_shifts(ub_ext, sh, range(1, 8), tt + HALO - SUBLANES)

        def b2(base):
            rows = pl.ds(base, RC)
            xhat, rstd = _ln_stats(cb_ref[rows, :].astype(F32))
            lnv = xhat * lbg[...] + lbb[...]
            sl = _sig(lnv)
            s = lnv * sl
            z = _hcol(h_ref, 6, base)
            sz = _sig(z)
            szb = z * sz
            y_scr[rows, GW:2 * GW] = (s * szb).astype(BF16)
            dyb = dy_scr[rows, GW:2 * GW]
            dh_ref[rows, 6 * GW:7 * GW] = (dyb * s * _dsilu(z, sz)).astype(BF16)
            dlnv = dyb * szb * _dsilu(lnv, sl)
            racc[R_DLBG] += _fold8(dlnv * xhat)
            racc[R_DLBB] += _fold8(dlnv)
            dcb = _ln_bwd(dlnv, xhat, rstd, lbg[...])
            dcb_ext[rows, :] = dcb
            racc[R_DCBB] += _fold8(dcb)

            def wtap(off, v):
                racc[R_DWB + off - 2] += _fold8(dcb * v)
            _for_taps(ub_ext, sh, base, range(2, 2 + KB), wtap)
        _chunks(tt, b2, unroll=2)
        _build_shifts(dcb_ext, sh, range(1, 8), tt + HALO - SUBLANES)

        def b3(base):
            rows = pl.ds(base, RC)
            dubv = [jnp.zeros((RC, GW), F32)]

            def tap(off, v):
                dubv[0] = dubv[0] + cbw[KB - 1 - off:KB - off, :] * v
            _for_taps(dcb_ext, sh, base, range(KB), tap)
            dub = dubv[0]
            v, gt = _hcol(h_ref, 4, base), _hcol(h_ref, 5, base)
            sg = _sig(gt)
            dh_ref[rows, 4 * GW:5 * GW] = (dub * sg).astype(BF16)
            dh_ref[rows, 5 * GW:6 * GW] = (dub * v * sg * (1.0 - sg)).astype(BF16)
        _chunks(tt, b3)
        dcb_ext[tt:tt + HALO, :] = dcb_ext[0:HALO, :]

        for g in range(len(POOL_WINDOWS)):
            lanes = slice(g * PGD, (g + 1) * PGD)
            pl_scr[:, lanes] = jnp.dot(p_scr[:, lanes], pw[g], preferred_element_type=F32)

        def c3(base):
            rows = pl.ds(base, RC)
            z = _hcol(h_ref, 8, base)
            sz = _sig(z)
            szc = z * sz
            plb = pl_scr[rows, :] + pb[...]
            yc0 = plb * ps[...]
            y_scr[rows, 2 * GW:3 * GW] = (yc0 * szc).astype(BF16)
            dyc = dy_scr[rows, 2 * GW:3 * GW]
            dh_ref[rows, 8 * GW:9 * GW] = (dyc * yc0 * _dsilu(z, sz)).astype(BF16)
            dyc0 = dyc * szc
            racc[R_DPS] += _fold8(dyc0 * plb)
            dpl = dyc0 * ps[...]
            racc[R_DPB] += _fold8(dpl)
            dpl_scr[rows, :] = dpl.astype(BF16)
        _chunks(tt, c3)
        for g in range(len(POOL_WINDOWS)):
            lanes = slice(g * PGD, (g + 1) * PGD)
            dpw_acc[g] += lax.dot_general(p_scr[:, lanes], dpl_scr[:, lanes], (((0,), (0,)), ((), ())),
                                          preferred_element_type=F32)
            dp_scr[:, lanes] = lax.dot_general(dpl_scr[:, lanes], pw[g], (((1,), (1,)), ((), ())),
                                               preferred_element_type=F32)

        def c4(base):
            rows = pl.ds(base, RC)
            dpn_ext[rows, :] = dp_scr[rows, :] * _inv_count(cnt, base + t0 == 0)
        _chunks(tt, c4)
        _build_shifts(dpn_ext, sh, range(1, 8), tt + SUBLANES)

        def c5(base):
            rows = pl.ds(base, RC)
            for g, w in enumerate(POOL_WINDOWS):
                lanes = slice(g * PGD, (g + 1) * PGD)
                acc = _tap(dpn_ext, sh, 0, base, lanes)
                for j in range(1, w):
                    acc = acc + _tap(dpn_ext, sh, j, base, lanes)
                dh_ref[rows, 7 * GW + g * PGD:7 * GW + (g + 1) * PGD] = (acc - dp_scr[rows, lanes]).astype(BF16)
        _chunks(tt, c5)
        dpn_ext[tt:tt + 16, :] = dpn_ext[0:16, :]

        @pl.when(slot == group - 1)
        def _():
            for r in range(D_MIX // GW):
                dwout_acc[r * GW:(r + 1) * GW, :] += lax.dot_general(
                    y_all[:, r * GW:(r + 1) * GW], dzb_all[...], (((0,), (0,)), ((), ())), preferred_element_type=F32)
        dx_scr[...] = lax.dot_general(dh_ref[...], win_v[...], (((1,), (1,)), ((), ())), preferred_element_type=F32)

        def post(base):
            rows = pl.ds(base, RC)
            dx = ALPHA * dz_ref[rows, :] + dx_scr[rows, :]
            if has_prev:
                xhat, rstd = _ln_stats(zp_ref[rows, :])
                acc2[0] += _fold8(dx * xhat)
                acc2[1] += _fold8(dx)
                dxo_ref[rows, :] = _ln_bwd(dx, xhat, rstd, lngp[...])
            else:
                dxo_ref[rows, :] = dx
        _chunks(tt, post, unroll=2)

        @pl.when(i == n_t - 1)
        def _():
            small_ref[...] = jnp.sum(racc[...], axis=1)
            dpw_ref[...] = dpw_acc[...]
            pltpu.sync_copy(dwout_acc, dwout_hbm)
            if has_prev:
                dlnp_ref[...] = jnp.sum(acc2[...], axis=1)
            if comm is not None:
                comm["wait"](*comm_refs)

    rtile = lambda c: pl.BlockSpec((tt, c), lambda i: (n_t - 1 - i, 0))
    full = lambda a: pl.BlockSpec(a.shape, lambda i: (0,) * a.ndim)
    const = lambda shp: pl.BlockSpec(shp, lambda i: (0,) * len(shp))
    hbm = pl.BlockSpec(memory_space=pl.ANY)
    halo_spec = pl.BlockSpec((HALO, D_IN), lambda i: (jnp.maximum((n_t - 1 - i) * hb - 1, 0), 0))
    ins = [dz, h, h, cb, p, win_b, wout_b, *prm] + ([z_prev, lng_prev] if has_prev else [])
    in_specs = [rtile(D_MODEL), rtile(D_IN), halo_spec, rtile(GW), rtile(GW), hbm, hbm] + [full(a) for a in prm] \
        + ([rtile(D_MODEL), full(lng_prev)] if has_prev else [])
    out_shape = [jax.ShapeDtypeStruct((t_len, D_MODEL), F32), jax.ShapeDtypeStruct((t_len, D_IN), BF16),
                 jax.ShapeDtypeStruct((D_MIX, D_MODEL), F32), jax.ShapeDtypeStruct((N_RACC, GW), F32),
                 jax.ShapeDtypeStruct((len(POOL_WINDOWS), PGD, PGD), F32)]
    out_specs = [rtile(D_MODEL), rtile(D_IN), hbm, const((N_RACC, GW)), const((len(POOL_WINDOWS), PGD, PGD))]
    if has_prev:
        out_shape.append(jax.ShapeDtypeStruct((2, D_MODEL), F32))
        out_specs.append(const((2, D_MODEL)))
    scratch = [
        pltpu.VMEM((D_MODEL, D_IN), BF16), pltpu.VMEM((D_MIX, D_MODEL), BF16),
        pltpu.VMEM((group * tt, D_MODEL), BF16), pltpu.VMEM((tt, D_MIX), F32), pltpu.VMEM((group * tt, D_MIX), BF16),
        pltpu.VMEM((tt, D_MODEL), F32),
        pltpu.VMEM((tt + SUBLANES, GW), F32), pltpu.VMEM((tt + HALO, GW), F32),
        pltpu.VMEM((tt + SUBLANES, GW), F32), pltpu.VMEM((tt + HALO, GW), F32), pltpu.VMEM((tt + 16, GW), F32),
        pltpu.VMEM((SUBLANES, tt + HALO, GW), F32),
        pltpu.VMEM((tt, GW), F32), pltpu.VMEM((tt, GW), BF16), pltpu.VMEM((tt, GW), F32),
        pltpu.VMEM((N_RACC, SUBLANES, GW), F32), pltpu.VMEM((len(POOL_WINDOWS), PGD, PGD), F32),
        pltpu.VMEM((D_MIX, D_MODEL), F32),
    ]
    if has_prev:
        scratch.append(pltpu.VMEM((2, SUBLANES, D_MODEL), F32))
    n_in, n_out = len(ins), len(out_shape)
    ins, in_specs, out_shape, out_specs, scratch = _with_comm(comm, ins, in_specs, out_shape, out_specs, scratch)
    return pl.pallas_call(
        body, name=("bwd_layer_prev" if has_prev else "bwd_layer") + ("" if comm is None else "_comm"), grid=(n_t,),
        in_specs=in_specs, out_specs=out_specs, out_shape=out_shape, scratch_shapes=scratch,
        compiler_params=pltpu.CompilerParams(dimension_semantics=("arbitrary",), vmem_limit_bytes=VMEM_LIMIT),
    )(*ins)


def _wgrad_in(xb, dh, *, tk, comm=None):
    t_len = xb.shape[0]
    tk = min(tk, t_len)
    n_k = t_len // tk

    def body(*refs):
        (x_ref, dh_ref, o_ref), comm_refs = _split_comm(refs, comm, 2, 1)
        j, k = pl.program_id(0), pl.program_id(1)

        @pl.when(k == 0)
        def _():
            o_ref[...] = jnp.zeros_like(o_ref)
        if comm is not None:
            @pl.when((j == 0) & (k == 0))
            def _():
                comm["start"](*comm_refs)
        o_ref[0] += lax.dot_general(x_ref[...], dh_ref[...], (((0,), (0,)), ((), ())), preferred_element_type=F32)
        if comm is not None:
            @pl.when((j == N_CHIP - 1) & (k == n_k - 1))
            def _():
                comm["wait"](*comm_refs)

    ins = [xb, dh]
    in_specs = [pl.BlockSpec((tk, D_MODEL), lambda j, k: (k, 0)), pl.BlockSpec((tk, SHARD_IN), lambda j, k: (k, j))]
    out_shape = [jax.ShapeDtypeStruct((N_CHIP, D_MODEL, SHARD_IN), F32)]
    out_specs = [pl.BlockSpec((1, D_MODEL, SHARD_IN), lambda j, k: (j, 0, 0))]
    ins, in_specs, out_shape, out_specs, scratch = _with_comm(comm, ins, in_specs, out_shape, out_specs, [])
    outs = pl.pallas_call(
        body, name="wgrad_in" + ("" if comm is None else "_comm"), grid=(N_CHIP, n_k),
        in_specs=in_specs, out_specs=out_specs, out_shape=out_shape, scratch_shapes=scratch,
        compiler_params=pltpu.CompilerParams(dimension_semantics=("arbitrary", "arbitrary"), vmem_limit_bytes=VMEM_LIMIT),
    )(*ins)
    return outs[0] if comm is None else outs


MESH = pl.DeviceIdType.MESH
ANY = pl.BlockSpec(memory_space=pl.ANY)


def _place():
    x, y, c = lax.axis_index("x"), lax.axis_index("y"), lax.axis_index("c")
    others = [(1 - x, y), (x, 1 - y), (1 - x, 1 - y)]
    return x, y, c, 2 * x + y, [(ox, oy, 2 * ox + oy) for ox, oy in others]


def _rcopy(src, dst, send_sems, recv_sems, k, dev):
    return pltpu.make_async_remote_copy(src_ref=src, dst_ref=dst, send_sem=send_sems.at[k], recv_sem=recv_sems.at[k],
                                        device_id=dev, device_id_type=MESH)


def _gather_weights(w_in, w_out, cw):
    hi, ho = D_MODEL // 2, SHARD_OUT // 2

    def body(win_ref, wout_ref, cw_ref, owin, owout, ocw, bin_v, bout_v, send_sems, recv_sems, lsem):
        x, y, c, me, others = _place()
        for l in range(DEPTH):
            for r0 in range(0, D_MODEL, 256):
                bin_v[l, r0:r0 + 256, :] = win_ref[l, r0:r0 + 256, :].astype(BF16)
            bout_v[l] = wout_ref[l].astype(BF16)
        cin = pl.ds(pl.multiple_of(me * SHARD_IN, 128), SHARD_IN)
        rout = pl.ds(pl.multiple_of(me * SHARD_OUT, 128), SHARD_OUT)
        local = [pltpu.make_async_copy(bin_v.at[0], owin.at[:, cin], lsem.at[0]),
                 pltpu.make_async_copy(bout_v.at[0], owout.at[rout, :], lsem.at[1]),
                 pltpu.make_async_copy(cw_ref, ocw.at[me], lsem.at[2])]
        for cp in local:
            cp.start()

        def in_half(chip, core):
            return owin.at[pl.ds(pl.multiple_of(core * hi, 256), hi), pl.ds(pl.multiple_of(chip * SHARD_IN, 128), SHARD_IN)]

        def out_half(chip, core):
            return owout.at[pl.ds(pl.multiple_of(chip * SHARD_OUT + core * ho, 64), ho), :]

        first = []
        for k, (ox, oy, _) in enumerate(others):
            dev = (ox, oy, c)
            first.append(_rcopy(bin_v.at[0, pl.ds(pl.multiple_of(c * hi, 256), hi), :], in_half(me, c), send_sems, recv_sems, k, dev))
            first.append(_rcopy(bout_v.at[0, pl.ds(pl.multiple_of(c * ho, 64), ho), :], out_half(me, c), send_sems, recv_sems, 3 + k, dev))
            first.append(_rcopy(cw_ref, ocw.at[me], send_sems, recv_sems, 6 + k, dev))
        for cp in first:
            cp.start()
        sib = (x, y, 1 - c)
        passed = []
        for k, (ox, oy, oc) in enumerate(others):
            _rcopy(in_half(oc, c), in_half(oc, c), send_sems, recv_sems, k, sib).wait_recv()
            fwd_in = _rcopy(in_half(oc, c), in_half(oc, c), send_sems, recv_sems, 9 + k, sib)
            fwd_in.start()
            _rcopy(out_half(oc, c), out_half(oc, c), send_sems, recv_sems, 3 + k, sib).wait_recv()
            fwd_out = _rcopy(out_half(oc, c), out_half(oc, c), send_sems, recv_sems, 12 + k, sib)
            fwd_out.start()
            passed += [fwd_in, fwd_out]
        for k, (ox, oy, oc) in enumerate(others):
            _rcopy(cw_ref, ocw.at[oc], send_sems, recv_sems, 6 + k, sib).wait_recv()
            _rcopy(in_half(oc, 1 - c), in_half(oc, 1 - c), send_sems, recv_sems, 9 + k, sib).wait_recv()
            _rcopy(out_half(oc, 1 - c), out_half(oc, 1 - c), send_sems, recv_sems, 12 + k, sib).wait_recv()
        for cp in first + passed:
            cp.wait_send()
        for cp in local:
            cp.wait()

    vm = pl.BlockSpec(memory_space=pltpu.VMEM)
    return pl.pallas_call(
        body, name="gather_weights",
        in_specs=[vm, vm, vm], out_specs=[ANY, ANY, ANY, vm, vm],
        out_shape=[jax.ShapeDtypeStruct((D_MODEL, D_IN), BF16), jax.ShapeDtypeStruct((D_MIX, D_MODEL), BF16),
                   jax.ShapeDtypeStruct((N_CHIP,) + cw.shape, F32),
                   jax.ShapeDtypeStruct((DEPTH, D_MODEL, SHARD_IN), BF16), jax.ShapeDtypeStruct((DEPTH, SHARD_OUT, D_MODEL), BF16)],
        scratch_shapes=[pltpu.SemaphoreType.DMA((15,)), pltpu.SemaphoreType.DMA((15,)), pltpu.SemaphoreType.DMA((3,))],
        compiler_params=pltpu.CompilerParams(vmem_limit_bytes=VMEM_LIMIT),
    )(w_in, w_out, cw)


def _gather_starts(bsh_in, bsh_out, owin, owout, send_sems, recv_sems, lsem, layer):
    x, y, c, me, others = _place()
    hi, ho = D_MODEL // 2, SHARD_OUT // 2
    pltpu.make_async_copy(bsh_in.at[layer], owin.at[:, pl.ds(pl.multiple_of(me * SHARD_IN, 128), SHARD_IN)], lsem.at[0]).start()
    pltpu.make_async_copy(bsh_out.at[layer], owout.at[pl.ds(pl.multiple_of(me * SHARD_OUT, 128), SHARD_OUT), :], lsem.at[1]).start()
    for k, (ox, oy, _) in enumerate(others):
        for t in range(2):
            pltpu.make_async_remote_copy(
                src_ref=bsh_in.at[layer, pl.ds(pl.multiple_of(c * hi, 256), hi), :],
                dst_ref=owin.at[pl.ds(pl.multiple_of(c * hi, 256), hi), pl.ds(pl.multiple_of(me * SHARD_IN, 128), SHARD_IN)],
                send_sem=send_sems.at[2 * k + t], recv_sem=recv_sems.at[2 * k + c], device_id=(ox, oy, t), device_id_type=MESH).start()
            pltpu.make_async_remote_copy(
                src_ref=bsh_out.at[layer, pl.ds(pl.multiple_of(c * ho, 64), ho), :],
                dst_ref=owout.at[pl.ds(pl.multiple_of(me * SHARD_OUT + c * ho, 64), ho), :],
                send_sem=send_sems.at[6 + 2 * k + t], recv_sem=recv_sems.at[6 + 2 * k + c], device_id=(ox, oy, t), device_id_type=MESH).start()


def _gather_waits(bsh_in, bsh_out, owin, owout, send_sems, recv_sems, lsem, layer):
    x, y, c, me, others = _place()
    hi, ho = D_MODEL // 2, SHARD_OUT // 2
    src_in = bsh_in.at[layer, pl.ds(0, hi), :]
    src_out = bsh_out.at[layer, pl.ds(0, ho), :]
    for k, (ox, oy, oc) in enumerate(others):
        for t in range(2):
            dst_in = owin.at[pl.ds(t * hi, hi), pl.ds(pl.multiple_of(oc * SHARD_IN, 128), SHARD_IN)]
            dst_out = owout.at[pl.ds(pl.multiple_of(oc * SHARD_OUT + t * ho, 64), ho), :]
            a = pltpu.make_async_remote_copy(src_ref=src_in, dst_ref=dst_in, send_sem=send_sems.at[2 * k + t],
                                             recv_sem=recv_sems.at[2 * k + t], device_id=(ox, oy, t), device_id_type=MESH)
            b = pltpu.make_async_remote_copy(src_ref=src_out, dst_ref=dst_out, send_sem=send_sems.at[6 + 2 * k + t],
                                             recv_sem=recv_sems.at[6 + 2 * k + t], device_id=(ox, oy, t), device_id_type=MESH)
            a.wait_send()
            a.wait_recv()
            b.wait_send()
            b.wait_recv()
    pltpu.make_async_copy(bsh_in.at[layer], owin.at[:, pl.ds(pl.multiple_of(me * SHARD_IN, 128), SHARD_IN)], lsem.at[0]).wait()
    pltpu.make_async_copy(bsh_out.at[layer], owout.at[pl.ds(pl.multiple_of(me * SHARD_OUT, 128), SHARD_OUT), :], lsem.at[1]).wait()


def _gather_comm(bsh_in, bsh_out, layer):
    return dict(ins=[bsh_in, bsh_out],
                out_shape=[jax.ShapeDtypeStruct((D_MODEL, D_IN), BF16), jax.ShapeDtypeStruct((D_MIX, D_MODEL), BF16)],
                sems=[pltpu.SemaphoreType.DMA((12,)), pltpu.SemaphoreType.DMA((12,)), pltpu.SemaphoreType.DMA((2,))],
                start=lambda ins, outs, sems: _gather_starts(ins[0], ins[1], outs[0], outs[1], *sems, layer),
                wait=lambda ins, outs, sems: _gather_waits(ins[0], ins[1], outs[0], outs[1], *sems, layer))


def _exchange_halves(arrs, tag):
    n = len(arrs)

    def body(*refs):
        ins, outs, (send_sems, recv_sems) = refs[:n], refs[n:2 * n], refs[2 * n:]
        x, y, c, _, _ = _place()
        cps = []
        for m in range(n):
            half = ins[m].shape[1] // 2
            cps.append(_rcopy(ins[m].at[:, pl.ds(pl.multiple_of((1 - c) * half, SUBLANES), half), :], outs[m],
                              send_sems, recv_sems, m, (x, y, 1 - c)))
        for cp in cps:
            cp.start()
        for cp in cps:
            cp.wait()

    return pl.pallas_call(
        body, name="exchange_halves_" + tag, in_specs=[ANY] * n, out_specs=[ANY] * n,
        out_shape=[jax.ShapeDtypeStruct((a.shape[0], a.shape[1] // 2, a.shape[2]), F32) for a in arrs],
        scratch_shapes=[pltpu.SemaphoreType.DMA((n,)), pltpu.SemaphoreType.DMA((n,))],
    )(*arrs)


def _add_own_half(a, got, core, *, rb, dtype):
    nj, r, cdim = a.shape
    half = r // 2

    def body(core_ref, a_ref, g_ref, o_ref):
        o_ref[...] = (a_ref[0] + g_ref[...]).astype(dtype)

    return pl.pallas_call(
        body, name="add_own_half",
        grid_spec=pltpu.PrefetchScalarGridSpec(
            num_scalar_prefetch=1, grid=(nj, half // rb),
            in_specs=[pl.BlockSpec((1, 1, rb, cdim), lambda j, i, cr: (j, cr[0], i, 0)),
                      pl.BlockSpec((1, rb, cdim), lambda j, i, cr: (j, i, 0))],
            out_specs=pl.BlockSpec((1, rb, cdim), lambda j, i, cr: (j, i, 0))),
        out_shape=jax.ShapeDtypeStruct((nj, half, cdim), dtype),
    )(core, a.reshape(nj, 2, half, cdim), got)


def _owner_starts(ins, outs, sems):
    send_sems, recv_sems, lsem = sems
    x, y, c, me, others = _place()
    for m in range(len(ins)):
        pltpu.make_async_copy(ins[m].at[me], outs[m].at[me], lsem.at[m]).start()
        for k, (ox, oy, oc) in enumerate(others):
            _rcopy(ins[m].at[oc], outs[m].at[me], send_sems, recv_sems, 3 * m + k, (ox, oy, c)).start()


def _owner_waits(ins, outs, sems):
    send_sems, recv_sems, lsem = sems
    x, y, c, me, others = _place()
    for m in range(len(ins)):
        for k, (ox, oy, oc) in enumerate(others):
            _rcopy(ins[m].at[oc], outs[m].at[oc], send_sems, recv_sems, 3 * m + k, (ox, oy, c)).wait()
        pltpu.make_async_copy(ins[m].at[me], outs[m].at[me], lsem.at[m]).wait()


def _owner_comm(arrs):
    n = len(arrs)
    return dict(ins=arrs, out_shape=[jax.ShapeDtypeStruct(a.shape, a.dtype) for a in arrs],
                sems=[pltpu.SemaphoreType.DMA((3 * n,)), pltpu.SemaphoreType.DMA((3 * n,)), pltpu.SemaphoreType.DMA((n,))],
                start=_owner_starts, wait=_owner_waits)


def _send_to_owners(arrs):
    n = len(arrs)

    def body(*refs):
        ins, outs, sems = refs[:n], refs[n:2 * n], refs[2 * n:]
        _owner_starts(ins, outs, sems)
        _owner_waits(ins, outs, sems)

    job = _owner_comm(arrs)
    return pl.pallas_call(
        body, name="send_to_owners", in_specs=[ANY] * n, out_specs=[ANY] * n,
        out_shape=job["out_shape"], scratch_shapes=job["sems"],
    )(*arrs)


def _sum_chips(a, *, rb):
    nj, r, cdim = a.shape

    def body(a_ref, o_ref):
        f = lambda k: a_ref[k].astype(F32)
        o_ref[...] = ((f(0) + f(1)) + f(2)) + f(3)

    return pl.pallas_call(
        body, name="sum_chips", grid=(r // rb,),
        in_specs=[pl.BlockSpec((nj, rb, cdim), lambda i: (0, i, 0))],
        out_specs=pl.BlockSpec((rb, cdim), lambda i: (i, 0)),
        out_shape=jax.ShapeDtypeStruct((r, cdim), F32),
    )(a)


def _sum_chips_into(a, dest, layer, core, *, rb):
    nj, half, cdim = a.shape
    nb = half // rb

    def body(*refs):
        a_ref, o_ref = refs[1], refs[-1]
        f = lambda k: a_ref[k].astype(F32)
        o_ref[0] = ((f(0) + f(1)) + f(2)) + f(3)

    grid_spec = pltpu.PrefetchScalarGridSpec(
        num_scalar_prefetch=1, grid=(nb,),
        in_specs=[pl.BlockSpec((nj, rb, cdim), lambda i, cr: (0, i, 0))] + ([] if dest is None else [ANY]),
        out_specs=pl.BlockSpec((1, rb, cdim), lambda i, cr: (layer, cr[0] * nb + i, 0)))
    return pl.pallas_call(
        body, name="sum_chips_into", grid_spec=grid_spec,
        out_shape=jax.ShapeDtypeStruct((DEPTH, 2 * half, cdim), F32),
        input_output_aliases={} if dest is None else {2: 0},
    )(*([core, a] if dest is None else [core, a, dest]))


def _spread_reduced(g_in, g_out, red_small):
    hs = red_small.shape[0]

    def body(gin_in, gout_in, sm, gin, gout, fsm, gsm, send_sems, recv_sems, lsem):
        x, y, c, me, others = _place()
        sib = (x, y, 1 - c)
        hi, ho = D_MODEL // 2, SHARD_OUT // 2
        ri, ro = pl.ds(pl.multiple_of(c * hi, SUBLANES), hi), pl.ds(pl.multiple_of(c * ho, SUBLANES), ho)
        remote = [_rcopy(gin.at[:, ri, :], gin.at[:, ri, :], send_sems, recv_sems, 0, sib),
                  _rcopy(gout.at[:, ro, :], gout.at[:, ro, :], send_sems, recv_sems, 1, sib)]
        own_small = pltpu.make_async_copy(sm, gsm.at[me], lsem.at[0])
        small = [_rcopy(sm, gsm.at[me], send_sems, recv_sems, 2 + k, (ox, oy, c)) for k, (ox, oy, _) in enumerate(others)]
        for cp in remote + [own_small] + small:
            cp.start()
        own_small.wait()
        for cp in small:
            cp.wait()
        mine = fsm.at[:, pl.ds(pl.multiple_of(c * hs, SUBLANES), hs), :]
        keep = pltpu.make_async_copy(gsm, mine, lsem.at[1])
        give = _rcopy(gsm, mine, send_sems, recv_sems, 5, sib)
        keep.start()
        give.start()
        for cp in remote + [give]:
            cp.wait()
        keep.wait()

    return pl.pallas_call(
        body, name="spread_reduced", in_specs=[ANY] * 3, out_specs=[ANY] * 4,
        out_shape=[jax.ShapeDtypeStruct(g_in.shape, F32), jax.ShapeDtypeStruct(g_out.shape, F32),
                   jax.ShapeDtypeStruct((N_CHIP, 2 * hs, GW), F32), jax.ShapeDtypeStruct((N_CHIP, hs, GW), F32)],
        input_output_aliases={0: 0, 1: 1},
        scratch_shapes=[pltpu.SemaphoreType.DMA((6,)), pltpu.SemaphoreType.DMA((6,)), pltpu.SemaphoreType.DMA((2,))],
    )(g_in, g_out, red_small)[:3]


def _adamw_math(w, g, m, v):
    m = ADAM_B1 * m + (1.0 - ADAM_B1) * g
    v = ADAM_B2 * v + (1.0 - ADAM_B2) * (g * g)
    m_hat = m / (1.0 - ADAM_B1 ** ADAM_STEP)
    v_hat = v / (1.0 - ADAM_B2 ** ADAM_STEP)
    delta = -ADAM_LR * (m_hat / (jnp.sqrt(v_hat) + ADAM_EPS) + ADAM_WD * w)
    return delta, m, v


def _adamw_big(w, g, m, v, *, rb):
    r, cdim = w.shape

    def body(w_ref, g_ref, m_ref, v_ref, d_ref, nm_ref, nv_ref):
        d_ref[...], nm_ref[...], nv_ref[...] = _adamw_math(w_ref[...], g_ref[...], m_ref[...], v_ref[...])

    spec = pl.BlockSpec((rb, cdim), lambda i: (i, 0))
    return pl.pallas_call(
        body, name="adamw_big", grid=(r // rb,), in_specs=[spec] * 4, out_specs=[spec] * 3,
        out_shape=[jax.ShapeDtypeStruct((r, cdim), F32)] * 3,
    )(w, g, m, v)


def _adamw_small(ws, gs, ms, vs):
    n = len(ws)

    def body(*refs):
        w, g, m, v = refs[:n], refs[n:2 * n], refs[2 * n:3 * n], refs[3 * n:4 * n]
        d, nm, nv = refs[4 * n:5 * n], refs[5 * n:6 * n], refs[6 * n:7 * n]
        for k in range(n):
            d[k][...], nm[k][...], nv[k][...] = _adamw_math(w[k][...], g[k][...], m[k][...], v[k][...])

    shapes = [jax.ShapeDtypeStruct(a.shape, F32) for a in ws]
    outs = pl.pallas_call(body, name="adamw_small", out_shape=shapes * 3)(*ws, *gs, *ms, *vs)
    return outs[:n], outs[n:2 * n], outs[2 * n:]


TT = 256
TK = 2048
CW_ROWS = 40
PACK_ROWS = 192


def _pack(rows):
    packed = jnp.concatenate(rows, axis=0)
    packed = jnp.pad(packed, ((0, PACK_ROWS - packed.shape[0]), (0, 0)))
    return packed.reshape(N_CHIP, PACK_ROWS // N_CHIP, GW)


def _reduce_to_owner_halves(parts, core1, tag):
    got = _exchange_halves(parts, tag)
    rbs = {D_MODEL: 256, SHARD_OUT: SHARD_OUT // 2, PACK_ROWS // N_CHIP: PACK_ROWS // N_CHIP // 2}
    return [_add_own_half(a, g, core1, rb=rbs[a.shape[1]], dtype=F32 if a.shape[1] == PACK_ROWS // N_CHIP else BF16)
            for a, g in zip(parts, got)]


def kernel(x, w_in, conv_a_w, conv_a_b, conv_b_w, conv_b_b, ln_b_g, ln_b_b, pool_w, pool_b, pool_scale, w_out, ln_g, ln_b, loss_target, m_w_in, m_conv_a_w, m_conv_a_b, m_conv_b_w, m_conv_b_b, m_ln_b_g, m_ln_b_b, m_pool_w, m_pool_b, m_pool_scale, m_w_out, m_ln_g, m_ln_b, v_w_in, v_conv_a_w, v_conv_a_b, v_conv_b_w, v_conv_b_b, v_ln_b_g, v_ln_b_b, v_pool_w, v_pool_b, v_pool_scale, v_w_out, v_ln_g, v_ln_b):
    chip = 2 * lax.axis_index("x") + lax.axis_index("y")
    core1 = lax.axis_index("c").reshape(1).astype(jnp.int32)
    x2, tgt = x[0], loss_target[0]

    cw = jnp.zeros((DEPTH, CW_ROWS, PGD), F32).at[:, 0:KA].set(conv_a_w).at[:, 8:8 + KB].set(conv_b_w)
    win0_b, wout0_b, cw_all, bsh_in, bsh_out = _gather_weights(w_in, w_out, cw)
    cw_full = jnp.transpose(cw_all, (1, 2, 0, 3)).reshape(DEPTH, CW_ROWS, GW)
    row = lambda a, l: a[l].reshape(1, -1)
    cnt = _count_table()
    prm = [(cw_full[l, 0:KA], row(conv_a_b, l), cw_full[l, 8:8 + KB], row(conv_b_b, l), row(ln_b_g, l), row(ln_b_b, l),
            pool_w[l].astype(BF16), row(pool_b, l), row(pool_scale, l), cnt) for l in range(DEPTH)]

    h0, xb0, cb0, pool0, z0, x1, win1_b, wout1_b = _fwd_layer(x2, win0_b, wout0_b, prm[0], row(ln_g, 0), row(ln_b, 0), None, tt=TT, last=False,
                                                  comm=_gather_comm(bsh_in, bsh_out, 1))
    h1, xb1, cb1, pool1, dz1, dln1, loss8 = _fwd_layer(x1, win1_b, wout1_b, prm[1], row(ln_g, 1), row(ln_b, 1), tgt, tt=TT, last=True)

    dz0, dh1, dwout1, small1, dpw1, dln0 = _bwd_layer(dz1, h1, cb1, pool1, win1_b, wout1_b, prm[1], z0, row(ln_g, 0), tt=TT)
    dwin1 = _wgrad_in(xb1, dh1, tk=TK)
    loss_row = jnp.pad(loss8, ((0, 0), (0, GW - loss8.shape[1])))
    pack1 = _pack([small1, dpw1.reshape(PGD, GW), dln1.reshape(4, GW), dln0.reshape(4, GW), loss_row])
    sums1 = _reduce_to_owner_halves([dwin1, dwout1.reshape(N_CHIP, SHARD_OUT, D_MODEL), pack1], core1, "1")
    gx, dh0, dwout0, small0, dpw0 = _bwd_layer(dz0, h0, cb0, pool0, win0_b, wout0_b, prm[0], None, None, tt=TT)
    pack0 = _pack([small0, dpw0.reshape(PGD, GW)])
    sums0 = _reduce_to_owner_halves([dwout0.reshape(N_CHIP, SHARD_OUT, D_MODEL), pack0], core1, "0")
    dwin0, *landed = _wgrad_in(xb0, dh0, tk=TK, comm=_owner_comm(sums1 + sums0))
    landed1, landed0 = landed[:3], landed[3:]
    landed0 = list(_send_to_owners(_reduce_to_owner_halves([dwin0], core1, "in0"))) + list(landed0)

    g_in = _sum_chips_into(landed0[0], _sum_chips_into(landed1[0], None, 1, core1, rb=256), 0, core1, rb=256)
    g_out = _sum_chips_into(landed0[1], _sum_chips_into(landed1[1], None, 1, core1, rb=SHARD_OUT // 2), 0, core1, rb=SHARD_OUT // 2)
    red_small = jnp.concatenate([_sum_chips(a, rb=PACK_ROWS // N_CHIP // 2) for a in (landed0[2], landed1[2])], axis=0)
    g_in, g_out, g_small = _spread_reduced(g_in, g_out, red_small)

    flat = lambda a: a.reshape(-1, a.shape[-1])
    unflat = lambda a, like: a.reshape(like.shape)
    d_in, nm_in, nv_in = [unflat(a, w_in) for a in _adamw_big(flat(w_in), flat(g_in), flat(m_w_in), flat(v_w_in), rb=256)]
    d_out, nm_out, nv_out = [unflat(a, w_out) for a in _adamw_big(flat(w_out), flat(g_out), flat(m_w_out), flat(v_w_out), rb=SHARD_OUT)]

    hp = PACK_ROWS // N_CHIP // 2
    unpack = lambda o: jnp.concatenate([g_small[:, o:o + hp], g_small[:, 2 * hp + o:3 * hp + o]], axis=1).reshape(PACK_ROWS, GW)
    p0, p1 = unpack(0), unpack(hp)
    small = [p0[0:N_RACC], p1[0:N_RACC]]
    dpw = [p[N_RACC:N_RACC + PGD].reshape(len(POOL_WINDOWS), PGD, PGD) for p in (p0, p1)]
    o = N_RACC + PGD
    g_lng = jnp.stack([p1[o + 4:o + 8].reshape(2, D_MODEL)[0], p1[o:o + 4].reshape(2, D_MODEL)[0]])
    g_lnb = jnp.stack([p1[o + 4:o + 8].reshape(2, D_MODEL)[1], p1[o:o + 4].reshape(2, D_MODEL)[1]])
    mine = lambda a: lax.dynamic_slice_in_dim(a, chip * PGD, PGD, axis=-1)
    stack = lambda f: jnp.stack([f(0), f(1)])
    g_caw = stack(lambda l: mine(small[l][R_DWA:R_DWA + KA]))
    g_cab = stack(lambda l: small[l][R_DCAB])
    g_cbw = stack(lambda l: mine(small[l][R_DWB:R_DWB + KB]))
    g_cbb = stack(lambda l: small[l][R_DCBB])
    g_lbg = stack(lambda l: small[l][R_DLBG])
    g_lbb = stack(lambda l: small[l][R_DLBB])
    g_pw = stack(lambda l: dpw[l])
    g_pb = stack(lambda l: small[l][R_DPB].reshape(len(POOL_WINDOWS), PGD))
    g_ps = stack(lambda l: small[l][R_DPS])
    ws = [conv_a_w, conv_a_b, conv_b_w, conv_b_b, ln_b_g, ln_b_b, pool_w, pool_b, pool_scale, ln_g, ln_b]
    gs = [g_caw, g_cab, g_cbw, g_cbb, g_lbg, g_lbb, g_pw, g_pb, g_ps, g_lng, g_lnb]
    ms = [m_conv_a_w, m_conv_a_b, m_conv_b_w, m_conv_b_b, m_ln_b_g, m_ln_b_b, m_pool_w, m_pool_b, m_pool_scale, m_ln_g, m_ln_b]
    vs = [v_conv_a_w, v_conv_a_b, v_conv_b_w, v_conv_b_b, v_ln_b_g, v_ln_b_b, v_pool_w, v_pool_b, v_pool_scale, v_ln_g, v_ln_b]
    ds, nms, nvs = _adamw_small([flat(a) for a in ws], [flat(a) for a in gs], [flat(a) for a in ms], [flat(a) for a in vs])
    ds, nms, nvs = ([unflat(a, w) for a, w in zip(t, ws)] for t in (ds, nms, nvs))

    loss = p1[o + 8, 0]

    def order(in_, small_, out_):
        return [in_, *small_[:9], out_, *small_[9:]]
    return (loss, gx[None], *order(g_in, gs, g_out), *order(d_in, ds, d_out), *order(nm_in, nms, nm_out), *order(nv_in, nvs, nv_out))
```

```python
import functools

import jax
import jax.numpy as jnp
import numpy as np
from jax import lax
from jax.experimental import pallas as pl
from jax.experimental.pallas import tpu as pltpu

F32 = jnp.float32
BF16 = jnp.bfloat16

D_MODEL = 1024
DEPTH = 2
GW = 512
D_IN = 9 * GW
D_MIX = 3 * GW
NG = D_IN // GW
POOL_WINDOWS = (2, 4, 8, 16)
PGD = 128
KA = 3
KB = 31
ALPHA = (2.0 * DEPTH) ** 0.25
LN_EPS = 1e-5
ADAM_LR, ADAM_B1, ADAM_B2, ADAM_EPS, ADAM_WD, ADAM_STEP = 0.001, 0.9, 0.999, 1e-08, 0.01, 10

N_CHIP = 4
SHARD_IN = D_IN // N_CHIP
SHARD_OUT = D_MIX // N_CHIP

SUBLANES = 8
RC = 32
HALO = 32
VMEM_LIMIT = 60 * 1024 * 1024
WOUT_GROUP = 4

R_DWA, R_DCAB, R_DWB, R_DCBB, R_DLBG, R_DLBB, R_DPB, R_DPS, N_RACC = 0, 3, 4, 35, 36, 37, 38, 39, 40


def _sig(v):
    return 0.5 * jnp.tanh(0.5 * v) + 0.5


def _chunks(n_rows, fn, unroll=1, extra=None):
    unroll = min(unroll, n_rows // RC)

    def step(m, carry):
        for u in range(unroll):
            fn(pl.multiple_of((m * unroll + u) * RC, RC))
        if extra is not None:
            extra(m)
        return carry
    lax.fori_loop(0, n_rows // (RC * unroll), step, 0)


def _fold8(v):
    return v.reshape(RC // SUBLANES, SUBLANES, v.shape[-1]).sum(axis=0)


def _build_shifts(ext_ref, sh_ref, shifts, n_rows):
    for r in shifts:
        for c0 in range(0, n_rows, RC):
            n = min(RC, n_rows - c0)
            sh_ref[r, pl.ds(c0, n), :] = ext_ref[pl.ds(c0 + r, n), :]


def _tap(ext_ref, sh_ref, off, base, lanes=None):
    a, r = divmod(off, SUBLANES)
    src = ext_ref if r == 0 else sh_ref.at[r]
    if lanes is None:
        return src[pl.ds(base + SUBLANES * a, RC), :]
    return src[pl.ds(base + SUBLANES * a, RC), lanes]


def _ln_stats(v):
    mu = jnp.mean(v, axis=-1, keepdims=True)
    vc = v - mu
    var = jnp.mean(vc * vc, axis=-1, keepdims=True)
    rstd = lax.rsqrt(var + LN_EPS)
    return vc * rstd, rstd


def _ln_bwd(dy, xhat, rstd, g):
    dxh = dy * g
    m1 = jnp.mean(dxh, axis=-1, keepdims=True)
    m2 = jnp.mean(dxh * xhat, axis=-1, keepdims=True)
    return rstd * (dxh - m1 - xhat * m2)


def _for_taps(ext_ref, sh_ref, base, offsets, fn):
    for r in range(SUBLANES):
        offs = [o for o in offsets if o % SUBLANES == r]
        if not offs:
            continue
        a0, a1 = min(offs) // SUBLANES, max(offs) // SUBLANES
        src = ext_ref if r == 0 else sh_ref.at[r]
        win = src[pl.ds(base + SUBLANES * a0, RC + SUBLANES * (a1 - a0)), :]
        for o in offs:
            a = o // SUBLANES - a0
            fn(o, win[SUBLANES * a:SUBLANES * a + RC])


def _count_table():
    t = np.arange(1, RC + 1, dtype=np.float64)[:, None]
    w = np.repeat(np.asarray(POOL_WINDOWS, np.float64), PGD)[None, :]
    return jnp.asarray(1.0 / np.minimum(t, w), F32)


def _inv_count(cnt_ref, first):
    return jnp.where(first, cnt_ref[...], cnt_ref[RC - 1:RC, :])


def _hcol(h_ref, j, base):
    if len(h_ref.shape) == 3:
        return h_ref[j, pl.ds(base, RC), :].astype(F32)
    return h_ref[pl.ds(base, RC), j * GW:(j + 1) * GW].astype(F32)


def _with_comm(comm, ins, in_specs, out_shape, out_specs, scratch):
    if comm is None:
        return ins, in_specs, out_shape, out_specs, scratch
    hbm = pl.BlockSpec(memory_space=pl.ANY)
    return (ins + list(comm["ins"]), in_specs + [hbm] * len(comm["ins"]), out_shape + list(comm["out_shape"]),
            out_specs + [hbm] * len(comm["out_shape"]), scratch + list(comm["sems"]))


def _split_comm(refs, comm, n_in, n_out):
    refs = list(refs)
    if comm is None:
        return refs, None
    ci, co, cs = len(comm["ins"]), len(comm["out_shape"]), len(comm["sems"])
    own = refs[:n_in] + refs[n_in + ci:n_in + ci + n_out] + refs[n_in + ci + n_out + co:len(refs) - cs]
    return own, (refs[n_in:n_in + ci], refs[n_in + ci + n_out:n_in + ci + n_out + co], refs[len(refs) - cs:])


def _fwd_mixers(h_ref, cb_ref, y_scr, q_ext, ub_ext, cu_ext, sh, p_scr, pl_scr, prm, tt, t0):
    caw, cab, cbw, cbb, lbg, lbb, pw, pb, ps, cnt = prm

    def a1(base):
        q_ext[pl.ds(SUBLANES + base, RC), :] = _hcol(h_ref, 1, base) * _hcol(h_ref, 2, base)
    _chunks(tt, a1)
    _build_shifts(q_ext, sh, (6, 7), tt)

    def a2(base):
        ca = cab[...] + caw[0:1, :] * _tap(q_ext, sh, 6, base) + caw[1:2, :] * _tap(q_ext, sh, 7, base) \
            + caw[2:3, :] * _tap(q_ext, sh, 8, base)
        z = _hcol(h_ref, 3, base)
        y_scr[pl.ds(base, RC), 0:GW] = (_hcol(h_ref, 0, base) * ca * (z * _sig(z))).astype(BF16)
    _chunks(tt, a2, unroll=2)
    q_ext[0:SUBLANES, :] = q_ext[tt:tt + SUBLANES, :]

    def b1(base):
        ub_ext[pl.ds(HALO + base, RC), :] = _hcol(h_ref, 4, base) * _sig(_hcol(h_ref, 5, base))
    _chunks(tt, b1)
    _build_shifts(ub_ext, sh, range(1, 8), tt + HALO - SUBLANES)

    def b2(base):
        cb = [cbb[...] + jnp.zeros((RC, GW), F32)]

        def tap(off, v):
            cb[0] = cb[0] + cbw[off - 2:off - 1, :] * v
        _for_taps(ub_ext, sh, base, range(2, 2 + KB), tap)
        cbr = cb[0].astype(BF16)
        cb_ref[pl.ds(base, RC), :] = cbr
        xhat, _ = _ln_stats(cbr.astype(F32))
        lnv = xhat * lbg[...] + lbb[...]
        z = _hcol(h_ref, 6, base)
        y_scr[pl.ds(base, RC), GW:2 * GW] = (lnv * _sig(lnv) * (z * _sig(z))).astype(BF16)
    _chunks(tt, b2, unroll=4)
    ub_ext[0:HALO, :] = ub_ext[tt:tt + HALO, :]

    def c1(base):
        cu_ext[pl.ds(16 + base, RC), :] = _hcol(h_ref, 7, base)
    _chunks(tt, c1)
    _build_shifts(cu_ext, sh, range(1, 8), tt + SUBLANES)

    def c2(base):
        ic = _inv_count(cnt, base + t0 == 0)
        for g, w in enumerate(POOL_WINDOWS):
            lanes = slice(g * PGD, (g + 1) * PGD)
            acc = _tap(cu_ext, sh, 16, base, lanes)
            for j in range(1, w):
                acc = acc + _tap(cu_ext, sh, 16 - j, base, lanes)
            p = acc * ic[:, lanes] - _tap(cu_ext, sh, 16, base, lanes)
            p_scr[pl.ds(base, RC), lanes] = p.astype(BF16)
    _chunks(tt, c2)
    cu_ext[0:16, :] = cu_ext[tt:tt + 16, :]
    for g in range(len(POOL_WINDOWS)):
        lanes = slice(g * PGD, (g + 1) * PGD)
        pl_scr[:, lanes] = jnp.dot(p_scr[:, lanes], pw[g], preferred_element_type=F32)

    def c3(base):
        z = _hcol(h_ref, 8, base)
        yc0 = (pl_scr[pl.ds(base, RC), :] + pb[...]) * ps[...]
        y_scr[pl.ds(base, RC), 2 * GW:3 * GW] = (yc0 * (z * _sig(z))).astype(BF16)
    _chunks(tt, c3, unroll=2)


def _fwd_layer(x, win_b, wout_b, prm, ln_g, ln_b, target, *, tt, last, comm=None):
    t_len = x.shape[0]
    n_t = t_len // tt

    def body(*refs):
        refs, comm_refs = _split_comm(refs, comm, n_in, n_out)
        if last:
            (x_ref, win_hbm, wout_hbm, caw, cab, cbw, cbb, lbg, lbb, pw, pb, ps, cnt, lng, lnb, tgt_ref,
             h_ref, xb_ref, cb_ref, p_scr, dz_ref, dln_ref, loss_ref,
             win_v, wout_v, y_scr, o_scr, q_ext, ub_ext, cu_ext, sh, pl_scr, acc2, lacc) = refs
        else:
            (x_ref, win_hbm, wout_hbm, caw, cab, cbw, cbb, lbg, lbb, pw, pb, ps, cnt, lng, lnb,
             h_ref, xb_ref, cb_ref, p_scr, z_ref, xn_ref,
             win_v, wout_v, y_scr, o_scr, q_ext, ub_ext, cu_ext, sh, pl_scr) = refs
        i = pl.program_id(0)

        @pl.when(i == 0)
        def _():
            if comm is not None:
                comm["start"](*comm_refs)
            pltpu.sync_copy(win_hbm, win_v)
            pltpu.sync_copy(wout_hbm, wout_v)
            q_ext[0:SUBLANES, :] = jnp.zeros((SUBLANES, GW), F32)
            ub_ext[0:HALO, :] = jnp.zeros((HALO, GW), F32)
            cu_ext[0:16, :] = jnp.zeros((16, GW), F32)
            if last:
                acc2[...] = jnp.zeros_like(acc2)
                lacc[...] = jnp.zeros_like(lacc)

        xb_ref[...] = x_ref[...].astype(BF16)
        for j in range(NG):
            h_ref[:, j * GW:(j + 1) * GW] = jnp.dot(
                xb_ref[...], win_v[:, j * GW:(j + 1) * GW], preferred_element_type=F32).astype(BF16)

        _fwd_mixers(h_ref, cb_ref, y_scr, q_ext, ub_ext, cu_ext, sh, p_scr, pl_scr,
                    (caw, cab, cbw, cbb, lbg, lbb, pw, pb, ps, cnt), tt, i * tt)

        o_scr[...] = jnp.dot(y_scr[...], wout_v[...], preferred_element_type=F32)

        def post(base):
            rows = pl.ds(base, RC)
            z = ALPHA * x_ref[rows, :] + o_scr[rows, :]
            xhat, rstd = _ln_stats(z)
            xn = xhat * lng[...] + lnb[...]
            if last:
                err = xn - tgt_ref[rows, :]
                lacc[...] += _fold8(err * err)
                dxn = err * (1.0 / D_MODEL)
                acc2[0] += _fold8(dxn * xhat)
                acc2[1] += _fold8(dxn)
                dz_ref[rows, :] = _ln_bwd(dxn, xhat, rstd, lng[...])
            else:
                z_ref[rows, :] = z
                xn_ref[rows, :] = xn
        _chunks(tt, post, unroll=8)

        if last:
            @pl.when(i == n_t - 1)
            def _():
                dln_ref[...] = jnp.sum(acc2[...], axis=1)
                loss_ref[...] = jnp.zeros((SUBLANES, 128), F32) + (0.5 / D_MODEL) * jnp.sum(lacc[...])
        if comm is not None:
            @pl.when(i == n_t - 1)
            def _():
                comm["wait"](*comm_refs)

    tile = lambda c: pl.BlockSpec((tt, c), lambda i: (i, 0))
    full = lambda a: pl.BlockSpec(a.shape, lambda i: (0,) * a.ndim)
    hbm = pl.BlockSpec(memory_space=pl.ANY)
    ins = [x, win_b, wout_b, *prm, ln_g, ln_b] + ([target] if last else [])
    in_specs = [tile(D_MODEL), hbm, hbm] + [full(a) for a in (*prm, ln_g, ln_b)] + ([tile(D_MODEL)] if last else [])
    out_shape = [jax.ShapeDtypeStruct((t_len, D_IN), BF16), jax.ShapeDtypeStruct((t_len, D_MODEL), BF16),
                 jax.ShapeDtypeStruct((t_len, GW), BF16), jax.ShapeDtypeStruct((t_len, GW), BF16)]
    out_specs = [tile(D_IN), tile(D_MODEL), tile(GW), tile(GW)]
    if last:
        out_shape += [jax.ShapeDtypeStruct((t_len, D_MODEL), F32), jax.ShapeDtypeStruct((2, D_MODEL), F32),
                      jax.ShapeDtypeStruct((SUBLANES, 128), F32)]
        out_specs += [tile(D_MODEL), pl.BlockSpec((2, D_MODEL), lambda i: (0, 0)),
                      pl.BlockSpec((SUBLANES, 128), lambda i: (0, 0))]
    else:
        out_shape += [jax.ShapeDtypeStruct((t_len, D_MODEL), F32), jax.ShapeDtypeStruct((t_len, D_MODEL), F32)]
        out_specs += [tile(D_MODEL), tile(D_MODEL)]
    scratch = [
        pltpu.VMEM((D_MODEL, D_IN), BF16), pltpu.VMEM((D_MIX, D_MODEL), BF16),
        pltpu.VMEM((tt, D_MIX), BF16), pltpu.VMEM((tt, D_MODEL), F32),
        pltpu.VMEM((tt + SUBLANES, GW), F32), pltpu.VMEM((tt + HALO, GW), F32), pltpu.VMEM((tt + 16, GW), F32),
        pltpu.VMEM((SUBLANES, tt + HALO, GW), F32),
        pltpu.VMEM((tt, GW), F32),
    ]
    if last:
        scratch += [pltpu.VMEM((2, SUBLANES, D_MODEL), F32), pltpu.VMEM((SUBLANES, D_MODEL), F32)]
    n_in, n_out = len(ins), len(out_shape)
    ins, in_specs, out_shape, out_specs, scratch = _with_comm(comm, ins, in_specs, out_shape, out_specs, scratch)
    return pl.pallas_call(
        body, name=("fwd_last" if last else "fwd_layer") + ("" if comm is None else "_comm"), grid=(n_t,),
        in_specs=in_specs, out_specs=out_specs, out_shape=out_shape, scratch_shapes=scratch,
        compiler_params=pltpu.CompilerParams(dimension_semantics=("arbitrary",), vmem_limit_bytes=VMEM_LIMIT),
    )(*ins)


def _dsilu(z, sz):
    return sz * (1.0 + z * (1.0 - sz))


def _bwd_layer(dz, h, cb, p, win_b, wout_b, prm, z_prev, lng_prev, *, tt, comm=None):
    t_len = dz.shape[0]
    n_t = t_len // tt
    has_prev = z_prev is not None
    hb = tt // HALO
    group = min(WOUT_GROUP, n_t)
    assert n_t % group == 0

    def body(*refs):
        refs, comm_refs = _split_comm(refs, comm, n_in, n_out)
        dz_ref, h_ref, halo_ref, cb_ref, p_scr, win_hbm, wout_hbm, caw, cab, cbw, cbb, lbg, lbb, pw, pb, ps, cnt = refs[:17]
        k = 17
        if has_prev:
            zp_ref, lngp = refs[k:k + 2]
            k += 2
        dxo_ref, dh_ref, dwout_hbm, small_ref, dpw_ref = refs[k:k + 5]
        k += 5
        if has_prev:
            dlnp_ref = refs[k]
            k += 1
        (win_v, wout_v, dzb_all, dy_scr, y_all, dx_scr, q_ext, ub_ext, dca_ext, dcb_ext, dpn_ext, sh,
         pl_scr, dpl_scr, dp_scr, racc, dpw_acc, dwout_acc) = refs[k:k + 18]
        k += 18
        if has_prev:
            acc2 = refs[k]
        i = pl.program_id(0)
        ti = n_t - 1 - i
        t0 = ti * tt
        slot = i % group
        slot_rows = pl.ds(pl.multiple_of(slot * tt, tt), tt)
        dzb, y_scr = dzb_all.at[slot_rows], y_all.at[slot_rows]

        @pl.when(i == 0)
        def _():
            if comm is not None:
                comm["start"](*comm_refs)
            pltpu.sync_copy(win_hbm, win_v)
            pltpu.sync_copy(wout_hbm, wout_v)
            dca_ext[tt:tt + SUBLANES, :] = jnp.zeros((SUBLANES, GW), F32)
            dcb_ext[tt:tt + HALO, :] = jnp.zeros((HALO, GW), F32)
            dpn_ext[tt:tt + 16, :] = jnp.zeros((16, GW), F32)
            racc[...] = jnp.zeros_like(racc)
            dpw_acc[...] = jnp.zeros_like(dpw_acc)
            dwout_acc[...] = jnp.zeros_like(dwout_acc)
            if has_prev:
                acc2[...] = jnp.zeros_like(acc2)

        dzb[...] = dz_ref[...].astype(BF16)
        dy_scr[...] = lax.dot_general(dzb[...], wout_v[...], (((1,), (1,)), ((), ())), preferred_element_type=F32)

        live = (ti > 0).astype(F32)
        hh = lambda j, r0, r1: halo_ref[r0:r1, j * GW:(j + 1) * GW].astype(F32)
        q_ext[0:SUBLANES, :] = live * hh(1, 24, 32) * hh(2, 24, 32)
        ub_ext[0:HALO, :] = live * hh(4, 0, 32) * _sig(hh(5, 0, 32))

        def a1(base):
            q_ext[pl.ds(SUBLANES + base, RC), :] = _hcol(h_ref, 1, base) * _hcol(h_ref, 2, base)
        _chunks(tt, a1)
        _build_shifts(q_ext, sh, (6, 7), tt)

        def a2(base):
            rows = pl.ds(base, RC)
            q6, q7, q8 = _tap(q_ext, sh, 6, base), _tap(q_ext, sh, 7, base), _tap(q_ext, sh, 8, base)
            ca = cab[...] + caw[0:1, :] * q6 + caw[1:2, :] * q7 + caw[2:3, :] * q8
            bg, z = _hcol(h_ref, 0, base), _hcol(h_ref, 3, base)
            sz = _sig(z)
            sza = z * sz
            dya = dy_scr[rows, 0:GW]
            ya0 = bg * ca
            y_scr[rows, 0:GW] = (ya0 * sza).astype(BF16)
            dya0 = dya * sza
            dh_ref[rows, 3 * GW:4 * GW] = (dya * ya0 * _dsilu(z, sz)).astype(BF16)
            dh_ref[rows, 0:GW] = (dya0 * ca).astype(BF16)
            dca = dya0 * bg
            dca_ext[rows, :] = dca
            racc[R_DCAB] += _fold8(dca)
            racc[R_DWA + 0] += _fold8(dca * q6)
            racc[R_DWA + 1] += _fold8(dca * q7)
            racc[R_DWA + 2] += _fold8(dca * q8)
        _chunks(tt, a2)
        _build_shifts(dca_ext, sh, (1, 2), tt)

        def a3(base):
            rows = pl.ds(base, RC)
            dq = caw[0:1, :] * _tap(dca_ext, sh, 2, base) + caw[1:2, :] * _tap(dca_ext, sh, 1, base) \
                + caw[2:3, :] * _tap(dca_ext, sh, 0, base)
            dh_ref[rows, GW:2 * GW] = (dq * _hcol(h_ref, 2, base)).astype(BF16)
            dh_ref[rows, 2 * GW:3 * GW] = (dq * _hcol(h_ref, 1, base)).astype(BF16)
        _chunks(tt, a3)
        dca_ext[tt:tt + SUBLANES, :] = dca_ext[0:SUBLANES, :]

        def b1(base):
            ub_ext[pl.ds(HALO + base, RC), :] = _hcol(h_ref, 4, base) * _sig(_hcol(h_ref, 5, base))
        _chunks(tt, b1)
        _build_shifts(ub_ext, sh, range(1, 8), tt + HALO - SUBLANES)

        def b2(base):
            rows = pl.ds(base, RC)
            xhat, rstd = _ln_stats(cb_ref[rows, :].astype(F32))
            lnv = xhat * lbg[...] + lbb[...]
            sl = _sig(lnv)
            s = lnv * sl
            z = _hcol(h_ref, 6, base)
            sz = _sig(z)
            szb = z * sz
            y_scr[rows, GW:2 * GW] = (s * szb).astype(BF16)
            dyb = dy_scr[rows, GW:2 * GW]
            dh_ref[rows, 6 * GW:7 * GW] = (dyb * s * _dsilu(z, sz)).astype(BF16)
            dlnv = dyb * szb * _dsilu(lnv, sl)
            racc[R_DLBG] += _fold8(dlnv * xhat)
            racc[R_DLBB] += _fold8(dlnv)
            dcb = _ln_bwd(dlnv, xhat, rstd, lbg[...])
            dcb_ext[rows, :] = dcb
            racc[R_DCBB] += _fold8(dcb)

            def wtap(off, v):
                racc[R_DWB + off - 2] += _fold8(dcb * v)
            _for_taps(ub_ext, sh, base, range(2, 2 + KB), wtap)
        _chunks(tt, b2, unroll=4)
        _build_shifts(dcb_ext, sh, range(1, 8), tt + HALO - SUBLANES)

        def b3(base):
            rows = pl.ds(base, RC)
            dubv = [jnp.zeros((RC, GW), F32)]

            def tap(off, v):
                dubv[0] = dubv[0] + cbw[KB - 1 - off:KB - off, :] * v
            _for_taps(dcb_ext, sh, base, range(KB), tap)
            dub = dubv[0]
            v, gt = _hcol(h_ref, 4, base), _hcol(h_ref, 5, base)
            sg = _sig(gt)
            dh_ref[rows, 4 * GW:5 * GW] = (dub * sg).astype(BF16)
            dh_ref[rows, 5 * GW:6 * GW] = (dub * v * sg * (1.0 - sg)).astype(BF16)
        _chunks(tt, b3)
        dcb_ext[tt:tt + HALO, :] = dcb_ext[0:HALO, :]

        for g in range(len(POOL_WINDOWS)):
            lanes = slice(g * PGD, (g + 1) * PGD)
            pl_scr[:, lanes] = jnp.dot(p_scr[:, lanes], pw[g], preferred_element_type=F32)

        def c3(base):
            rows = pl.ds(base, RC)
            z = _hcol(h_ref, 8, base)
            sz = _sig(z)
            szc = z * sz
            plb = pl_scr[rows, :] + pb[...]
            yc0 = plb * ps[...]
            y_scr[rows, 2 * GW:3 * GW] = (yc0 * szc).astype(BF16)
            dyc = dy_scr[rows, 2 * GW:3 * GW]
            dh_ref[rows, 8 * GW:9 * GW] = (dyc * yc0 * _dsilu(z, sz)).astype(BF16)
            dyc0 = dyc * szc
            racc[R_DPS] += _fold8(dyc0 * plb)
            dpl = dyc0 * ps[...]
            racc[R_DPB] += _fold8(dpl)
            dpl_scr[rows, :] = dpl.astype(BF16)
        _chunks(tt, c3)
        for g in range(len(POOL_WINDOWS)):
            lanes = slice(g * PGD, (g + 1) * PGD)
            dpw_acc[g] += lax.dot_general(p_scr[:, lanes], dpl_scr[:, lanes], (((0,), (0,)), ((), ())),
                                          preferred_element_type=F32)
            dp_scr[:, lanes] = lax.dot_general(dpl_scr[:, lanes], pw[g], (((1,), (1,)), ((), ())),
                                               preferred_element_type=F32)

        def c4(base):
            rows = pl.ds(base, RC)
            dpn_ext[rows, :] = dp_scr[rows, :] * _inv_count(cnt, base + t0 == 0)
        _chunks(tt, c4)
        _build_shifts(dpn_ext, sh, range(1, 8), tt + SUBLANES)

        def c5(base):
            rows = pl.ds(base, RC)
            for g, w in enumerate(POOL_WINDOWS):
                lanes = slice(g * PGD, (g + 1) * PGD)
                acc = _tap(dpn_ext, sh, 0, base, lanes)
                for j in range(1, w):
                    acc = acc + _tap(dpn_ext, sh, j, base, lanes)
                dh_ref[rows, 7 * GW + g * PGD:7 * GW + (g + 1) * PGD] = (acc - dp_scr[rows, lanes]).astype(BF16)
        _chunks(tt, c5)
        dpn_ext[tt:tt + 16, :] = dpn_ext[0:16, :]

        @pl.when(slot == group - 1)
        def _():
            for r in range(D_MIX // GW):
                dwout_acc[r * GW:(r + 1) * GW, :] += lax.dot_general(
                    y_all[:, r * GW:(r + 1) * GW], dzb_all[...], (((0,), (0,)), ((), ())), preferred_element_type=F32)
        dx_scr[...] = lax.dot_general(dh_ref[...], win_v[...], (((1,), (1,)), ((), ())), preferred_element_type=F32)

        def post(base):
            rows = pl.ds(base, RC)
            dx = ALPHA * dz_ref[rows, :] + dx_scr[rows, :]
            if has_prev:
                xhat, rstd = _ln_stats(zp_ref[rows, :])
                acc2[0] += _fold8(dx * xhat)
                acc2[1] += _fold8(dx)
                dxo_ref[rows, :] = _ln_bwd(dx, xhat, rstd, lngp[...])
            else:
                dxo_ref[rows, :] = dx
        _chunks(tt, post, unroll=8)

        @pl.when(i == n_t - 1)
        def _():
            small_ref[...] = jnp.sum(racc[...], axis=1)
            dpw_ref[...] = dpw_acc[...]
            pltpu.sync_copy(dwout_acc, dwout_hbm)
            if has_prev:
                dlnp_ref[...] = jnp.sum(acc2[...], axis=1)
            if comm is not None:
                comm["wait"](*comm_refs)

    rtile = lambda c: pl.BlockSpec((tt, c), lambda i: (n_t - 1 - i, 0))
    full = lambda a: pl.BlockSpec(a.shape, lambda i: (0,) * a.ndim)
    const = lambda shp: pl.BlockSpec(shp, lambda i: (0,) * len(shp))
    hbm = pl.BlockSpec(memory_space=pl.ANY)
    halo_spec = pl.BlockSpec((HALO, D_IN), lambda i: (jnp.maximum((n_t - 1 - i) * hb - 1, 0), 0))
    ins = [dz, h, h, cb, p, win_b, wout_b, *prm] + ([z_prev, lng_prev] if has_prev else [])
    in_specs = [rtile(D_MODEL), rtile(D_IN), halo_spec, rtile(GW), rtile(GW), hbm, hbm] + [full(a) for a in prm] \
        + ([rtile(D_MODEL), full(lng_prev)] if has_prev else [])
    out_shape = [jax.ShapeDtypeStruct((t_len, D_MODEL), F32), jax.ShapeDtypeStruct((t_len, D_IN), BF16),
                 jax.ShapeDtypeStruct((D_MIX, D_MODEL), F32), jax.ShapeDtypeStruct((N_RACC, GW), F32),
                 jax.ShapeDtypeStruct((len(POOL_WINDOWS), PGD, PGD), F32)]
    out_specs = [rtile(D_MODEL), rtile(D_IN), hbm, const((N_RACC, GW)), const((len(POOL_WINDOWS), PGD, PGD))]
    if has_prev:
        out_shape.append(jax.ShapeDtypeStruct((2, D_MODEL), F32))
        out_specs.append(const((2, D_MODEL)))
    scratch = [
        pltpu.VMEM((D_MODEL, D_IN), BF16), pltpu.VMEM((D_MIX, D_MODEL), BF16),
        pltpu.VMEM((group * tt, D_MODEL), BF16), pltpu.VMEM((tt, D_MIX), F32), pltpu.VMEM((group * tt, D_MIX), BF16),
        pltpu.VMEM((tt, D_MODEL), F32),
        pltpu.VMEM((tt + SUBLANES, GW), F32), pltpu.VMEM((tt + HALO, GW), F32),
        pltpu.VMEM((tt + SUBLANES, GW), F32), pltpu.VMEM((tt + HALO, GW), F32), pltpu.VMEM((tt + 16, GW), F32),
        pltpu.VMEM((SUBLANES, tt + HALO, GW), F32),
        pltpu.VMEM((tt, GW), F32), pltpu.VMEM((tt, GW), BF16), pltpu.VMEM((tt, GW), F32),
        pltpu.VMEM((N_RACC, SUBLANES, GW), F32), pltpu.VMEM((len(POOL_WINDOWS), PGD, PGD), F32),
        pltpu.VMEM((D_MIX, D_MODEL), F32),
    ]
    if has_prev:
        scratch.append(pltpu.VMEM((2, SUBLANES, D_MODEL), F32))
    n_in, n_out = len(ins), len(out_shape)
    ins, in_specs, out_shape, out_specs, scratch = _with_comm(comm, ins, in_specs, out_shape, out_specs, scratch)
    return pl.pallas_call(
        body, name=("bwd_layer_prev" if has_prev else "bwd_layer") + ("" if comm is None else "_comm"), grid=(n_t,),
        in_specs=in_specs, out_specs=out_specs, out_shape=out_shape, scratch_shapes=scratch,
        compiler_params=pltpu.CompilerParams(dimension_semantics=("arbitrary",), vmem_limit_bytes=VMEM_LIMIT),
    )(*ins)


def _wgrad_in(xb, dh, *, tk, comm=None):
    t_len = xb.shape[0]
    tk = min(tk, t_len)
    n_k = t_len // tk

    def body(*refs):
        (x_ref, dh_ref, o_ref), comm_refs = _split_comm(refs, comm, 2, 1)
        j, k = pl.program_id(0), pl.program_id(1)

        @pl.when(k == 0)
        def _():
            o_ref[...] = jnp.zeros_like(o_ref)
        if comm is not None:
            @pl.when((j == 0) & (k == 0))
            def _():
                comm["start"](*comm_refs)
        o_ref[0] += lax.dot_general(x_ref[...], dh_ref[...], (((0,), (0,)), ((), ())), preferred_element_type=F32)
        if comm is not None:
            @pl.when((j == N_CHIP - 1) & (k == n_k - 1))
            def _():
                comm["wait"](*comm_refs)

    ins = [xb, dh]
    in_specs = [pl.BlockSpec((tk, D_MODEL), lambda j, k: (k, 0)), pl.BlockSpec((tk, SHARD_IN), lambda j, k: (k, j))]
    out_shape = [jax.ShapeDtypeStruct((N_CHIP, D_MODEL, SHARD_IN), F32)]
    out_specs = [pl.BlockSpec((1, D_MODEL, SHARD_IN), lambda j, k: (j, 0, 0))]
    ins, in_specs, out_shape, out_specs, scratch = _with_comm(comm, ins, in_specs, out_shape, out_specs, [])
    outs = pl.pallas_call(
        body, name="wgrad_in" + ("" if comm is None else "_comm"), grid=(N_CHIP, n_k),
        in_specs=in_specs, out_specs=out_specs, out_shape=out_shape, scratch_shapes=scratch,
        compiler_params=pltpu.CompilerParams(dimension_semantics=("arbitrary", "arbitrary"), vmem_limit_bytes=VMEM_LIMIT),
    )(*ins)
    return outs[0] if comm is None else outs


MESH = pl.DeviceIdType.MESH
ANY = pl.BlockSpec(memory_space=pl.ANY)


def _place():
    x, y, c = lax.axis_index("x"), lax.axis_index("y"), lax.axis_index("c")
    others = [(1 - x, y), (x, 1 - y), (1 - x, 1 - y)]
    return x, y, c, 2 * x + y, [(ox, oy, 2 * ox + oy) for ox, oy in others]


def _rcopy(src, dst, send_sems, recv_sems, k, dev):
    return pltpu.make_async_remote_copy(src_ref=src, dst_ref=dst, send_sem=send_sems.at[k], recv_sem=recv_sems.at[k],
                                        device_id=dev, device_id_type=MESH)


def _gather_weights(w_in, w_out, cw):
    hi, ho = D_MODEL // 2, SHARD_OUT // 2

    def body(win_ref, wout_ref, cw_ref, owin, owout, ocw, bin_v, bout_v, send_sems, recv_sems, lsem):
        x, y, c, me, others = _place()
        for l in range(DEPTH):
            for r0 in range(0, D_MODEL, 256):
                bin_v[l, r0:r0 + 256, :] = win_ref[l, r0:r0 + 256, :].astype(BF16)
            bout_v[l] = wout_ref[l].astype(BF16)
        cin = pl.ds(pl.multiple_of(me * SHARD_IN, 128), SHARD_IN)
        rout = pl.ds(pl.multiple_of(me * SHARD_OUT, 128), SHARD_OUT)
        local = [pltpu.make_async_copy(bin_v.at[0], owin.at[:, cin], lsem.at[0]),
                 pltpu.make_async_copy(bout_v.at[0], owout.at[rout, :], lsem.at[1]),
                 pltpu.make_async_copy(cw_ref, ocw.at[me], lsem.at[2])]
        for cp in local:
            cp.start()

        def in_half(chip, core):
            return owin.at[pl.ds(pl.multiple_of(core * hi, 256), hi), pl.ds(pl.multiple_of(chip * SHARD_IN, 128), SHARD_IN)]

        def out_half(chip, core):
            return owout.at[pl.ds(pl.multiple_of(chip * SHARD_OUT + core * ho, 64), ho), :]

        first = []
        for k, (ox, oy, _) in enumerate(others):
            dev = (ox, oy, c)
            first.append(_rcopy(bin_v.at[0, pl.ds(pl.multiple_of(c * hi, 256), hi), :], in_half(me, c), send_sems, recv_sems, k, dev))
            first.append(_rcopy(bout_v.at[0, pl.ds(pl.multiple_of(c * ho, 64), ho), :], out_half(me, c), send_sems, recv_sems, 3 + k, dev))
            first.append(_rcopy(cw_ref, ocw.at[me], send_sems, recv_sems, 6 + k, dev))
        for cp in first:
            cp.start()
        sib = (x, y, 1 - c)
        passed = []
        for k, (ox, oy, oc) in enumerate(others):
            _rcopy(in_half(oc, c), in_half(oc, c), send_sems, recv_sems, k, sib).wait_recv()
            fwd_in = _rcopy(in_half(oc, c), in_half(oc, c), send_sems, recv_sems, 9 + k, sib)
            fwd_in.start()
            _rcopy(out_half(oc, c), out_half(oc, c), send_sems, recv_sems, 3 + k, sib).wait_recv()
            fwd_out = _rcopy(out_half(oc, c), out_half(oc, c), send_sems, recv_sems, 12 + k, sib)
            fwd_out.start()
            passed += [fwd_in, fwd_out]
        for k, (ox, oy, oc) in enumerate(others):
            _rcopy(cw_ref, ocw.at[oc], send_sems, recv_sems, 6 + k, sib).wait_recv()
            _rcopy(in_half(oc, 1 - c), in_half(oc, 1 - c), send_sems, recv_sems, 9 + k, sib).wait_recv()
            _rcopy(out_half(oc, 1 - c), out_half(oc, 1 - c), send_sems, recv_sems, 12 + k, sib).wait_recv()
        for cp in first + passed:
            cp.wait_send()
        for cp in local:
            cp.wait()

    vm = pl.BlockSpec(memory_space=pltpu.VMEM)
    return pl.pallas_call(
        body, name="gather_weights",
        in_specs=[vm, vm, vm], out_specs=[ANY, ANY, ANY, vm, vm],
        out_shape=[jax.ShapeDtypeStruct((D_MODEL, D_IN), BF16), jax.ShapeDtypeStruct((D_MIX, D_MODEL), BF16),
                   jax.ShapeDtypeStruct((N_CHIP,) + cw.shape, F32),
                   jax.ShapeDtypeStruct((DEPTH, D_MODEL, SHARD_IN), BF16), jax.ShapeDtypeStruct((DEPTH, SHARD_OUT, D_MODEL), BF16)],
        scratch_shapes=[pltpu.SemaphoreType.DMA((15,)), pltpu.SemaphoreType.DMA((15,)), pltpu.SemaphoreType.DMA((3,))],
        compiler_params=pltpu.CompilerParams(vmem_limit_bytes=VMEM_LIMIT),
    )(w_in, w_out, cw)


def _gather_starts(bsh_in, bsh_out, owin, owout, send_sems, recv_sems, lsem, layer):
    x, y, c, me, others = _place()
    hi, ho = D_MODEL // 2, SHARD_OUT // 2
    pltpu.make_async_copy(bsh_in.at[layer], owin.at[:, pl.ds(pl.multiple_of(me * SHARD_IN, 128), SHARD_IN)], lsem.at[0]).start()
    pltpu.make_async_copy(bsh_out.at[layer], owout.at[pl.ds(pl.multiple_of(me * SHARD_OUT, 128), SHARD_OUT), :], lsem.at[1]).start()
    for k, (ox, oy, _) in enumerate(others):
        for t in range(2):
            pltpu.make_async_remote_copy(
                src_ref=bsh_in.at[layer, pl.ds(pl.multiple_of(c * hi, 256), hi), :],
                dst_ref=owin.at[pl.ds(pl.multiple_of(c * hi, 256), hi), pl.ds(pl.multiple_of(me * SHARD_IN, 128), SHARD_IN)],
                send_sem=send_sems.at[2 * k + t], recv_sem=recv_sems.at[2 * k + c], device_id=(ox, oy, t), device_id_type=MESH).start()
            pltpu.make_async_remote_copy(
                src_ref=bsh_out.at[layer, pl.ds(pl.multiple_of(c * ho, 64), ho), :],
                dst_ref=owout.at[pl.ds(pl.multiple_of(me * SHARD_OUT + c * ho, 64), ho), :],
                send_sem=send_sems.at[6 + 2 * k + t], recv_sem=recv_sems.at[6 + 2 * k + c], device_id=(ox, oy, t), device_id_type=MESH).start()


def _gather_waits(bsh_in, bsh_out, owin, owout, send_sems, recv_sems, lsem, layer):
    x, y, c, me, others = _place()
    hi, ho = D_MODEL // 2, SHARD_OUT // 2
    src_in = bsh_in.at[layer, pl.ds(0, hi), :]
    src_out = bsh_out.at[layer, pl.ds(0, ho), :]
    for k, (ox, oy, oc) in enumerate(others):
        for t in range(2):
            dst_in = owin.at[pl.ds(t * hi, hi), pl.ds(pl.multiple_of(oc * SHARD_IN, 128), SHARD_IN)]
            dst_out = owout.at[pl.ds(pl.multiple_of(oc * SHARD_OUT + t * ho, 64), ho), :]
            a = pltpu.make_async_remote_copy(src_ref=src_in, dst_ref=dst_in, send_sem=send_sems.at[2 * k + t],
                                             recv_sem=recv_sems.at[2 * k + t], device_id=(ox, oy, t), device_id_type=MESH)
            b = pltpu.make_async_remote_copy(src_ref=src_out, dst_ref=dst_out, send_sem=send_sems.at[6 + 2 * k + t],
                                             recv_sem=recv_sems.at[6 + 2 * k + t], device_id=(ox, oy, t), device_id_type=MESH)
            a.wait_send()
            a.wait_recv()
            b.wait_send()
            b.wait_recv()
    pltpu.make_async_copy(bsh_in.at[layer], owin.at[:, pl.ds(pl.multiple_of(me * SHARD_IN, 128), SHARD_IN)], lsem.at[0]).wait()
    pltpu.make_async_copy(bsh_out.at[layer], owout.at[pl.ds(pl.multiple_of(me * SHARD_OUT, 128), SHARD_OUT), :], lsem.at[1]).wait()


def _gather_comm(bsh_in, bsh_out, layer):
    return dict(ins=[bsh_in, bsh_out],
                out_shape=[jax.ShapeDtypeStruct((D_MODEL, D_IN), BF16), jax.ShapeDtypeStruct((D_MIX, D_MODEL), BF16)],
                sems=[pltpu.SemaphoreType.DMA((12,)), pltpu.SemaphoreType.DMA((12,)), pltpu.SemaphoreType.DMA((2,))],
                start=lambda ins, outs, sems: _gather_starts(ins[0], ins[1], outs[0], outs[1], *sems, layer),
                wait=lambda ins, outs, sems: _gather_waits(ins[0], ins[1], outs[0], outs[1], *sems, layer))


def _exchange_halves(arrs, tag):
    n = len(arrs)

    def body(*refs):
        ins, outs, (send_sems, recv_sems) = refs[:n], refs[n:2 * n], refs[2 * n:]
        x, y, c, _, _ = _place()
        cps = []
        for m in range(n):
            half = ins[m].shape[1] // 2
            cps.append(_rcopy(ins[m].at[:, pl.ds(pl.multiple_of((1 - c) * half, SUBLANES), half), :], outs[m],
                              send_sems, recv_sems, m, (x, y, 1 - c)))
        for cp in cps:
            cp.start()
        for cp in cps:
            cp.wait()

    return pl.pallas_call(
        body, name="exchange_halves_" + tag, in_specs=[ANY] * n, out_specs=[ANY] * n,
        out_shape=[jax.ShapeDtypeStruct((a.shape[0], a.shape[1] // 2, a.shape[2]), F32) for a in arrs],
        scratch_shapes=[pltpu.SemaphoreType.DMA((n,)), pltpu.SemaphoreType.DMA((n,))],
    )(*arrs)


def _add_own_half(a, got, core, *, rb, dtype):
    nj, r, cdim = a.shape
    half = r // 2

    def body(core_ref, a_ref, g_ref, o_ref):
        o_ref[...] = (a_ref[0] + g_ref[...]).astype(dtype)

    return pl.pallas_call(
        body, name="add_own_half",
        grid_spec=pltpu.PrefetchScalarGridSpec(
            num_scalar_prefetch=1, grid=(nj, half // rb),
            in_specs=[pl.BlockSpec((1, 1, rb, cdim), lambda j, i, cr: (j, cr[0], i, 0)),
                      pl.BlockSpec((1, rb, cdim), lambda j, i, cr: (j, i, 0))],
            out_specs=pl.BlockSpec((1, rb, cdim), lambda j, i, cr: (j, i, 0))),
        out_shape=jax.ShapeDtypeStruct((nj, half, cdim), dtype),
    )(core, a.reshape(nj, 2, half, cdim), got)


def _owner_starts(ins, outs, sems):
    send_sems, recv_sems, lsem = sems
    x, y, c, me, others = _place()
    for m in range(len(ins)):
        pltpu.make_async_copy(ins[m].at[me], outs[m].at[me], lsem.at[m]).start()
        for k, (ox, oy, oc) in enumerate(others):
            _rcopy(ins[m].at[oc], outs[m].at[me], send_sems, recv_sems, 3 * m + k, (ox, oy, c)).start()


def _owner_waits(ins, outs, sems):
    send_sems, recv_sems, lsem = sems
    x, y, c, me, others = _place()
    for m in range(len(ins)):
        for k, (ox, oy, oc) in enumerate(others):
            _rcopy(ins[m].at[oc], outs[m].at[oc], send_sems, recv_sems, 3 * m + k, (ox, oy, c)).wait()
        pltpu.make_async_copy(ins[m].at[me], outs[m].at[me], lsem.at[m]).wait()


def _owner_comm(arrs):
    n = len(arrs)
    return dict(ins=arrs, out_shape=[jax.ShapeDtypeStruct(a.shape, a.dtype) for a in arrs],
                sems=[pltpu.SemaphoreType.DMA((3 * n,)), pltpu.SemaphoreType.DMA((3 * n,)), pltpu.SemaphoreType.DMA((n,))],
                start=_owner_starts, wait=_owner_waits)


def _send_to_owners(arrs):
    n = len(arrs)

    def body(*refs):
        ins, outs, sems = refs[:n], refs[n:2 * n], refs[2 * n:]
        _owner_starts(ins, outs, sems)
        _owner_waits(ins, outs, sems)

    job = _owner_comm(arrs)
    return pl.pallas_call(
        body, name="send_to_owners", in_specs=[ANY] * n, out_specs=[ANY] * n,
        out_shape=job["out_shape"], scratch_shapes=job["sems"],
    )(*arrs)


def _sum_chips(a, *, rb):
    nj, r, cdim = a.shape

    def body(a_ref, o_ref):
        f = lambda k: a_ref[k].astype(F32)
        o_ref[...] = ((f(0) + f(1)) + f(2)) + f(3)

    return pl.pallas_call(
        body, name="sum_chips", grid=(r // rb,),
        in_specs=[pl.BlockSpec((nj, rb, cdim), lambda i: (0, i, 0))],
        out_specs=pl.BlockSpec((rb, cdim), lambda i: (i, 0)),
        out_shape=jax.ShapeDtypeStruct((r, cdim), F32),
    )(a)


def _sum_chips_into(a, dest, layer, core, *, rb):
    nj, half, cdim = a.shape
    nb = half // rb

    def body(*refs):
        a_ref, o_ref = refs[1], refs[-1]
        f = lambda k: a_ref[k].astype(F32)
        o_ref[0] = ((f(0) + f(1)) + f(2)) + f(3)

    grid_spec = pltpu.PrefetchScalarGridSpec(
        num_scalar_prefetch=1, grid=(nb,),
        in_specs=[pl.BlockSpec((nj, rb, cdim), lambda i, cr: (0, i, 0))] + ([] if dest is None else [ANY]),
        out_specs=pl.BlockSpec((1, rb, cdim), lambda i, cr: (layer, cr[0] * nb + i, 0)))
    return pl.pallas_call(
        body, name="sum_chips_into", grid_spec=grid_spec,
        out_shape=jax.ShapeDtypeStruct((DEPTH, 2 * half, cdim), F32),
        input_output_aliases={} if dest is None else {2: 0},
    )(*([core, a] if dest is None else [core, a, dest]))


def _spread_reduced(g_in, g_out, red_small):
    hs = red_small.shape[0]

    def body(gin_in, gout_in, sm, gin, gout, fsm, gsm, send_sems, recv_sems, lsem):
        x, y, c, me, others = _place()
        sib = (x, y, 1 - c)
        hi, ho = D_MODEL // 2, SHARD_OUT // 2
        ri, ro = pl.ds(pl.multiple_of(c * hi, SUBLANES), hi), pl.ds(pl.multiple_of(c * ho, SUBLANES), ho)
        remote = [_rcopy(gin.at[:, ri, :], gin.at[:, ri, :], send_sems, recv_sems, 0, sib),
                  _rcopy(gout.at[:, ro, :], gout.at[:, ro, :], send_sems, recv_sems, 1, sib)]
        own_small = pltpu.make_async_copy(sm, gsm.at[me], lsem.at[0])
        small = [_rcopy(sm, gsm.at[me], send_sems, recv_sems, 2 + k, (ox, oy, c)) for k, (ox, oy, _) in enumerate(others)]
        for cp in remote + [own_small] + small:
            cp.start()
        own_small.wait()
        for cp in small:
            cp.wait()
        mine = fsm.at[:, pl.ds(pl.multiple_of(c * hs, SUBLANES), hs), :]
        keep = pltpu.make_async_copy(gsm, mine, lsem.at[1])
        give = _rcopy(gsm, mine, send_sems, recv_sems, 5, sib)
        keep.start()
        give.start()
        for cp in remote + [give]:
            cp.wait()
        keep.wait()

    return pl.pallas_call(
        body, name="spread_reduced", in_specs=[ANY] * 3, out_specs=[ANY] * 4,
        out_shape=[jax.ShapeDtypeStruct(g_in.shape, F32), jax.ShapeDtypeStruct(g_out.shape, F32),
                   jax.ShapeDtypeStruct((N_CHIP, 2 * hs, GW), F32), jax.ShapeDtypeStruct((N_CHIP, hs, GW), F32)],
        input_output_aliases={0: 0, 1: 1},
        scratch_shapes=[pltpu.SemaphoreType.DMA((6,)), pltpu.SemaphoreType.DMA((6,)), pltpu.SemaphoreType.DMA((2,))],
    )(g_in, g_out, red_small)[:3]


def _adamw_math(w, g, m, v):
    m = ADAM_B1 * m + (1.0 - ADAM_B1) * g
    v = ADAM_B2 * v + (1.0 - ADAM_B2) * (g * g)
    m_hat = m / (1.0 - ADAM_B1 ** ADAM_STEP)
    v_hat = v / (1.0 - ADAM_B2 ** ADAM_STEP)
    delta = -ADAM_LR * (m_hat / (jnp.sqrt(v_hat) + ADAM_EPS) + ADAM_WD * w)
    return delta, m, v


def _adamw_big(w, g, m, v, *, rb):
    r, cdim = w.shape

    def body(w_ref, g_ref, m_ref, v_ref, d_ref, nm_ref, nv_ref):
        d_ref[...], nm_ref[...], nv_ref[...] = _adamw_math(w_ref[...], g_ref[...], m_ref[...], v_ref[...])

    spec = pl.BlockSpec((rb, cdim), lambda i: (i, 0))
    return pl.pallas_call(
        body, name="adamw_big", grid=(r // rb,), in_specs=[spec] * 4, out_specs=[spec] * 3,
        out_shape=[jax.ShapeDtypeStruct((r, cdim), F32)] * 3,
    )(w, g, m, v)


def _adamw_small(ws, gs, ms, vs):
    n = len(ws)

    def body(*refs):
        w, g, m, v = refs[:n], refs[n:2 * n], refs[2 * n:3 * n], refs[3 * n:4 * n]
        d, nm, nv = refs[4 * n:5 * n], refs[5 * n:6 * n], refs[6 * n:7 * n]
        for k in range(n):
            d[k][...], nm[k][...], nv[k][...] = _adamw_math(w[k][...], g[k][...], m[k][...], v[k][...])

    shapes = [jax.ShapeDtypeStruct(a.shape, F32) for a in ws]
    outs = pl.pallas_call(body, name="adamw_small", out_shape=shapes * 3)(*ws, *gs, *ms, *vs)
    return outs[:n], outs[n:2 * n], outs[2 * n:]


TT = 256
TK = 2048
CW_ROWS = 40
PACK_ROWS = 192


def _pack(rows):
    packed = jnp.concatenate(rows, axis=0)
    packed = jnp.pad(packed, ((0, PACK_ROWS - packed.shape[0]), (0, 0)))
    return packed.reshape(N_CHIP, PACK_ROWS // N_CHIP, GW)


def _reduce_to_owner_halves(parts, core1, tag):
    got = _exchange_halves(parts, tag)
    rbs = {D_MODEL: 256, SHARD_OUT: SHARD_OUT // 2, PACK_ROWS // N_CHIP: PACK_ROWS // N_CHIP // 2}
    return [_add_own_half(a, g, core1, rb=rbs[a.shape[1]], dtype=F32 if a.shape[1] == PACK_ROWS // N_CHIP else BF16)
            for a, g in zip(parts, got)]


def kernel(x, w_in, conv_a_w, conv_a_b, conv_b_w, conv_b_b, ln_b_g, ln_b_b, pool_w, pool_b, pool_scale, w_out, ln_g, ln_b, loss_target, m_w_in, m_conv_a_w, m_conv_a_b, m_conv_b_w, m_conv_b_b, m_ln_b_g, m_ln_b_b, m_pool_w, m_pool_b, m_pool_scale, m_w_out, m_ln_g, m_ln_b, v_w_in, v_conv_a_w, v_conv_a_b, v_conv_b_w, v_conv_b_b, v_ln_b_g, v_ln_b_b, v_pool_w, v_pool_b, v_pool_scale, v_w_out, v_ln_g, v_ln_b):
    chip = 2 * lax.axis_index("x") + lax.axis_index("y")
    core1 = lax.axis_index("c").reshape(1).astype(jnp.int32)
    x2, tgt = x[0], loss_target[0]

    cw = jnp.zeros((DEPTH, CW_ROWS, PGD), F32).at[:, 0:KA].set(conv_a_w).at[:, 8:8 + KB].set(conv_b_w)
    win0_b, wout0_b, cw_all, bsh_in, bsh_out = _gather_weights(w_in, w_out, cw)
    cw_full = jnp.transpose(cw_all, (1, 2, 0, 3)).reshape(DEPTH, CW_ROWS, GW)
    row = lambda a, l: a[l].reshape(1, -1)
    cnt = _count_table()
    prm = [(cw_full[l, 0:KA], row(conv_a_b, l), cw_full[l, 8:8 + KB], row(conv_b_b, l), row(ln_b_g, l), row(ln_b_b, l),
            pool_w[l].astype(BF16), row(pool_b, l), row(pool_scale, l), cnt) for l in range(DEPTH)]

    h0, xb0, cb0, pool0, z0, x1, win1_b, wout1_b = _fwd_layer(x2, win0_b, wout0_b, prm[0], row(ln_g, 0), row(ln_b, 0), None, tt=TT, last=False,
                                                  comm=_gather_comm(bsh_in, bsh_out, 1))
    h1, xb1, cb1, pool1, dz1, dln1, loss8 = _fwd_layer(x1, win1_b, wout1_b, prm[1], row(ln_g, 1), row(ln_b, 1), tgt, tt=TT, last=True)

    dz0, dh1, dwout1, small1, dpw1, dln0 = _bwd_layer(dz1, h1, cb1, pool1, win1_b, wout1_b, prm[1], z0, row(ln_g, 0), tt=TT)
    dwin1 = _wgrad_in(xb1, dh1, tk=TK)
    loss_row = jnp.pad(loss8, ((0, 0), (0, GW - loss8.shape[1])))
    pack1 = _pack([small1, dpw1.reshape(PGD, GW), dln1.reshape(4, GW), dln0.reshape(4, GW), loss_row])
    sums1 = _reduce_to_owner_halves([dwin1, dwout1.reshape(N_CHIP, SHARD_OUT, D_MODEL), pack1], core1, "1")
    gx, dh0, dwout0, small0, dpw0 = _bwd_layer(dz0, h0, cb0, pool0, win0_b, wout0_b, prm[0], None, None, tt=TT)
    pack0 = _pack([small0, dpw0.reshape(PGD, GW)])
    sums0 = _reduce_to_owner_halves([dwout0.reshape(N_CHIP, SHARD_OUT, D_MODEL), pack0], core1, "0")
    dwin0, *landed = _wgrad_in(xb0, dh0, tk=TK, comm=_owner_comm(sums1 + sums0))
    landed1, landed0 = landed[:3], landed[3:]
    landed0 = list(_send_to_owners(_reduce_to_owner_halves([dwin0], core1, "in0"))) + list(landed0)

    g_in = _sum_chips_into(landed0[0], _sum_chips_into(landed1[0], None, 1, core1, rb=256), 0, core1, rb=256)
    g_out = _sum_chips_into(landed0[1], _sum_chips_into(landed1[1], None, 1, core1, rb=SHARD_OUT // 2), 0, core1, rb=SHARD_OUT // 2)
    red_small = jnp.concatenate([_sum_chips(a, rb=PACK_ROWS // N_CHIP // 2) for a in (landed0[2], landed1[2])], axis=0)
    g_in, g_out, g_small = _spread_reduced(g_in, g_out, red_small)

    flat = lambda a: a.reshape(-1, a.shape[-1])
    unflat = lambda a, like: a.reshape(like.shape)
    d_in, nm_in, nv_in = [unflat(a, w_in) for a in _adamw_big(flat(w_in), flat(g_in), flat(m_w_in), flat(v_w_in), rb=256)]
    d_out, nm_out, nv_out = [unflat(a, w_out) for a in _adamw_big(flat(w_out), flat(g_out), flat(m_w_out), flat(v_w_out), rb=SHARD_OUT)]

    hp = PACK_ROWS // N_CHIP // 2
    unpack = lambda o: jnp.concatenate([g_small[:, o:o + hp], g_small[:, 2 * hp + o:3 * hp + o]], axis=1).reshape(PACK_ROWS, GW)
    p0, p1 = unpack(0), unpack(hp)
    small = [p0[0:N_RACC], p1[0:N_RACC]]
    dpw = [p[N_RACC:N_RACC + PGD].reshape(len(POOL_WINDOWS), PGD, PGD) for p in (p0, p1)]
    o = N_RACC + PGD
    g_lng = jnp.stack([p1[o + 4:o + 8].reshape(2, D_MODEL)[0], p1[o:o + 4].reshape(2, D_MODEL)[0]])
    g_lnb = jnp.stack([p1[o + 4:o + 8].reshape(2, D_MODEL)[1], p1[o:o + 4].reshape(2, D_MODEL)[1]])
    mine = lambda a: lax.dynamic_slice_in_dim(a, chip * PGD, PGD, axis=-1)
    stack = lambda f: jnp.stack([f(0), f(1)])
    g_caw = stack(lambda l: mine(small[l][R_DWA:R_DWA + KA]))
    g_cab = stack(lambda l: small[l][R_DCAB])
    g_cbw = stack(lambda l: mine(small[l][R_DWB:R_DWB + KB]))
    g_cbb = stack(lambda l: small[l][R_DCBB])
    g_lbg = stack(lambda l: small[l][R_DLBG])
    g_lbb = stack(lambda l: small[l][R_DLBB])
    g_pw = stack(lambda l: dpw[l])
    g_pb = stack(lambda l: small[l][R_DPB].reshape(len(POOL_WINDOWS), PGD))
    g_ps = stack(lambda l: small[l][R_DPS])
    ws = [conv_a_w, conv_a_b, conv_b_w, conv_b_b, ln_b_g, ln_b_b, pool_w, pool_b, pool_scale, ln_g, ln_b]
    gs = [g_caw, g_cab, g_cbw, g_cbb, g_lbg, g_lbb, g_pw, g_pb, g_ps, g_lng, g_lnb]
    ms = [m_conv_a_w, m_conv_a_b, m_conv_b_w, m_conv_b_b, m_ln_b_g, m_ln_b_b, m_pool_w, m_pool_b, m_pool_scale, m_ln_g, m_ln_b]
    vs = [v_conv_a_w, v_conv_a_b, v_conv_b_w, v_conv_b_b, v_ln_b_g, v_ln_b_b, v_pool_w, v_pool_b, v_pool_scale, v_ln_g, v_ln_b]
    ds, nms, nvs = _adamw_small([flat(a) for a in ws], [flat(a) for a in gs], [flat(a) for a in ms], [flat(a) for a in vs])
    ds, nms, nvs = ([unflat(a, w) for a, w in zip(t, ws)] for t in (ds, nms, nvs))

    loss = p1[o + 8, 0]

    def order(in_, small_, out_):
        return [in_, *small_[:9], out_, *small_[9:]]
    return (loss, gx[None], *order(g_in, gs, g_out), *order(d_in, ds, d_out), *order(nm_in, nms, nm_out), *order(nv_in, nvs, nv_out))
```

```python
import functools

import jax
import jax.numpy as jnp
import numpy as np
from jax import lax
from jax.experimental import pallas as pl
from jax.experimental.pallas import tpu as pltpu

F32 = jnp.float32
BF16 = jnp.bfloat16

D_MODEL = 1024
DEPTH = 2
GW = 512
D_IN = 9 * GW
D_MIX = 3 * GW
NG = D_IN // GW
POOL_WINDOWS = (2, 4, 8, 16)
PGD = 128
KA = 3
KB = 31
ALPHA = (2.0 * DEPTH) ** 0.25
LN_EPS = 1e-5
ADAM_LR, ADAM_B1, ADAM_B2, ADAM_EPS, ADAM_WD, ADAM_STEP = 0.001, 0.9, 0.999, 1e-08, 0.01, 10

N_CHIP = 4
SHARD_IN = D_IN // N_CHIP
SHARD_OUT = D_MIX // N_CHIP

SUBLANES = 8
RC = 32
HALO = 32
VMEM_LIMIT = 60 * 1024 * 1024
WOUT_GROUP = 4

R_DWA, R_DCAB, R_DWB, R_DCBB, R_DLBG, R_DLBB, R_DPB, R_DPS, N_RACC = 0, 3, 4, 35, 36, 37, 38, 39, 40


def _sig(v):
    return 0.5 * jnp.tanh(0.5 * v) + 0.5


def _chunks(n_rows, fn, unroll=1, extra=None):
    unroll = min(unroll, n_rows // RC)

    def step(m, carry):
        for u in range(unroll):
            fn(pl.multiple_of((m * unroll + u) * RC, RC))
        if extra is not None:
            extra(m)
        return carry
    lax.fori_loop(0, n_rows // (RC * unroll), step, 0)


def _fold8(v):
    return v.reshape(RC // SUBLANES, SUBLANES, v.shape[-1]).sum(axis=0)


def _build_shifts(ext_ref, sh_ref, shifts, n_rows):
    for r in shifts:
        for c0 in range(0, n_rows, RC):
            n = min(RC, n_rows - c0)
            sh_ref[r, pl.ds(c0, n), :] = ext_ref[pl.ds(c0 + r, n), :]


def _tap(ext_ref, sh_ref, off, base, lanes=None):
    a, r = divmod(off, SUBLANES)
    src = ext_ref if r == 0 else sh_ref.at[r]
    if lanes is None:
        return src[pl.ds(base + SUBLANES * a, RC), :]
    return src[pl.ds(base + SUBLANES * a, RC), lanes]


def _ln_stats(v):
    mu = jnp.mean(v, axis=-1, keepdims=True)
    vc = v - mu
    var = jnp.mean(vc * vc, axis=-1, keepdims=True)
    rstd = lax.rsqrt(var + LN_EPS)
    return vc * rstd, rstd


def _ln_bwd(dy, xhat, rstd, g):
    dxh = dy * g
    m1 = jnp.mean(dxh, axis=-1, keepdims=True)
    m2 = jnp.mean(dxh * xhat, axis=-1, keepdims=True)
    return rstd * (dxh - m1 - xhat * m2)


def _for_taps(ext_ref, sh_ref, base, offsets, fn):
    for r in range(SUBLANES):
        offs = [o for o in offsets if o % SUBLANES == r]
        if not offs:
            continue
        a0, a1 = min(offs) // SUBLANES, max(offs) // SUBLANES
        src = ext_ref if r == 0 else sh_ref.at[r]
        win = src[pl.ds(base + SUBLANES * a0, RC + SUBLANES * (a1 - a0)), :]
        for o in offs:
            a = o // SUBLANES - a0
            fn(o, win[SUBLANES * a:SUBLANES * a + RC])


def _count_table():
    t = np.arange(1, RC + 1, dtype=np.float64)[:, None]
    w = np.repeat(np.asarray(POOL_WINDOWS, np.float64), PGD)[None, :]
    return jnp.asarray(1.0 / np.minimum(t, w), F32)


def _inv_count(cnt_ref, first):
    return jnp.where(first, cnt_ref[...], cnt_ref[RC - 1:RC, :])


def _hcol(h_ref, j, base):
    if len(h_ref.shape) == 3:
        return h_ref[j, pl.ds(base, RC), :].astype(F32)
    return h_ref[pl.ds(base, RC), j * GW:(j + 1) * GW].astype(F32)


def _with_comm(comm, ins, in_specs, out_shape, out_specs, scratch):
    if comm is None:
        return ins, in_specs, out_shape, out_specs, scratch
    hbm = pl.BlockSpec(memory_space=pl.ANY)
    return (ins + list(comm["ins"]), in_specs + [hbm] * len(comm["ins"]), out_shape + list(comm["out_shape"]),
            out_specs + [hbm] * len(comm["out_shape"]), scratch + list(comm["sems"]))


def _split_comm(refs, comm, n_in, n_out):
    refs = list(refs)
    if comm is None:
        return refs, None
    ci, co, cs = len(comm["ins"]), len(comm["out_shape"]), len(comm["sems"])
    own = refs[:n_in] + refs[n_in + ci:n_in + ci + n_out] + refs[n_in + ci + n_out + co:len(refs) - cs]
    return own, (refs[n_in:n_in + ci], refs[n_in + ci + n_out:n_in + ci + n_out + co], refs[len(refs) - cs:])


def _fwd_mixers(h_ref, cb_ref, y_scr, q_ext, ub_ext, cu_ext, sh, p_scr, pl_scr, prm, tt, t0):
    caw, cab, cbw, cbb, lbg, lbb, pw, pb, ps, cnt = prm

    def a1(base):
        q_ext[pl.ds(SUBLANES + base, RC), :] = _hcol(h_ref, 1, base) * _hcol(h_ref, 2, base)
    _chunks(tt, a1)
    _build_shifts(q_ext, sh, (6, 7), tt)

    def a2(base):
        ca = cab[...] + caw[0:1, :] * _tap(q_ext, sh, 6, base) + caw[1:2, :] * _tap(q_ext, sh, 7, base) \
            + caw[2:3, :] * _tap(q_ext, sh, 8, base)
        z = _hcol(h_ref, 3, base)
        y_scr[pl.ds(base, RC), 0:GW] = (_hcol(h_ref, 0, base) * ca * (z * _sig(z))).astype(BF16)
    _chunks(tt, a2, unroll=2)
    q_ext[0:SUBLANES, :] = q_ext[tt:tt + SUBLANES, :]

    def b1(base):
        ub_ext[pl.ds(HALO + base, RC), :] = _hcol(h_ref, 4, base) * _sig(_hcol(h_ref, 5, base))
    _chunks(tt, b1)
    _build_shifts(ub_ext, sh, range(1, 8), tt + HALO - SUBLANES)

    def b2(base):
        cb = [cbb[...] + jnp.zeros((RC, GW), F32)]

        def tap(off, v):
            cb[0] = cb[0] + cbw[off - 2:off - 1, :] * v
        _for_taps(ub_ext, sh, base, range(2, 2 + KB), tap)
        cbr = cb[0].astype(BF16)
        cb_ref[pl.ds(base, RC), :] = cbr
        xhat, _ = _ln_stats(cbr.astype(F32))
        lnv = xhat * lbg[...] + lbb[...]
        z = _hcol(h_ref, 6, base)
        y_scr[pl.ds(base, RC), GW:2 * GW] = (lnv * _sig(lnv) * (z * _sig(z))).astype(BF16)
    _chunks(tt, b2, unroll=4)
    ub_ext[0:HALO, :] = ub_ext[tt:tt + HALO, :]

    def c1(base):
        cu_ext[pl.ds(16 + base, RC), :] = _hcol(h_ref, 7, base)
    _chunks(tt, c1)
    _build_shifts(cu_ext, sh, range(1, 8), tt + SUBLANES)

    def c2(base):
        ic = _inv_count(cnt, base + t0 == 0)
        for g, w in enumerate(POOL_WINDOWS):
            lanes = slice(g * PGD, (g + 1) * PGD)
            acc = _tap(cu_ext, sh, 16, base, lanes)
            for j in range(1, w):
                acc = acc + _tap(cu_ext, sh, 16 - j, base, lanes)
            p = acc * ic[:, lanes] - _tap(cu_ext, sh, 16, base, lanes)
            p_scr[pl.ds(base, RC), lanes] = p.astype(BF16)
    _chunks(tt, c2, unroll=4)
    cu_ext[0:16, :] = cu_ext[tt:tt + 16, :]
    for g in range(len(POOL_WINDOWS)):
        lanes = slice(g * PGD, (g + 1) * PGD)
        pl_scr[:, lanes] = jnp.dot(p_scr[:, lanes], pw[g], preferred_element_type=F32)

    def c3(base):
        z = _hcol(h_ref, 8, base)
        yc0 = (pl_scr[pl.ds(base, RC), :] + pb[...]) * ps[...]
        y_scr[pl.ds(base, RC), 2 * GW:3 * GW] = (yc0 * (z * _sig(z))).astype(BF16)
    _chunks(tt, c3, unroll=2)


def _fwd_layer(x, win_b, wout_b, prm, ln_g, ln_b, target, *, tt, last, comm=None):
    t_len = x.shape[0]
    n_t = t_len // tt

    def body(*refs):
        refs, comm_refs = _split_comm(refs, comm, n_in, n_out)
        if last:
            (x_ref, win_hbm, wout_hbm, caw, cab, cbw, cbb, lbg, lbb, pw, pb, ps, cnt, lng, lnb, tgt_ref,
             h_ref, xb_ref, cb_ref, p_scr, dz_ref, dln_ref, loss_ref,
             win_v, wout_v, y_scr, o_scr, q_ext, ub_ext, cu_ext, sh, pl_scr, acc2, lacc) = refs
        else:
            (x_ref, win_hbm, wout_hbm, caw, cab, cbw, cbb, lbg, lbb, pw, pb, ps, cnt, lng, lnb,
             h_ref, xb_ref, cb_ref, p_scr, z_ref, xn_ref,
             win_v, wout_v, y_scr, o_scr, q_ext, ub_ext, cu_ext, sh, pl_scr) = refs
        i = pl.program_id(0)

        @pl.when(i == 0)
        def _():
            if comm is not None:
                comm["start"](*comm_refs)
            pltpu.sync_copy(win_hbm, win_v)
            pltpu.sync_copy(wout_hbm, wout_v)
            q_ext[0:SUBLANES, :] = jnp.zeros((SUBLANES, GW), F32)
            ub_ext[0:HALO, :] = jnp.zeros((HALO, GW), F32)
            cu_ext[0:16, :] = jnp.zeros((16, GW), F32)
            if last:
                acc2[...] = jnp.zeros_like(acc2)
                lacc[...] = jnp.zeros_like(lacc)

        xb_ref[...] = x_ref[...].astype(BF16)
        for j in range(NG):
            h_ref[:, j * GW:(j + 1) * GW] = jnp.dot(
                xb_ref[...], win_v[:, j * GW:(j + 1) * GW], preferred_element_type=F32).astype(BF16)

        _fwd_mixers(h_ref, cb_ref, y_scr, q_ext, ub_ext, cu_ext, sh, p_scr, pl_scr,
                    (caw, cab, cbw, cbb, lbg, lbb, pw, pb, ps, cnt), tt, i * tt)

        o_scr[...] = jnp.dot(y_scr[...], wout_v[...], preferred_element_type=F32)

        def post(base):
            rows = pl.ds(base, RC)
            z = ALPHA * x_ref[rows, :] + o_scr[rows, :]
            xhat, rstd = _ln_stats(z)
            xn = xhat * lng[...] + lnb[...]
            if last:
                err = xn - tgt_ref[rows, :]
                lacc[...] += _fold8(err * err)
                dxn = err * (1.0 / D_MODEL)
                acc2[0] += _fold8(dxn * xhat)
                acc2[1] += _fold8(dxn)
                dz_ref[rows, :] = _ln_bwd(dxn, xhat, rstd, lng[...])
            else:
                z_ref[rows, :] = z
                xn_ref[rows, :] = xn
        _chunks(tt, post, unroll=8)

        if last:
            @pl.when(i == n_t - 1)
            def _():
                dln_ref[...] = jnp.sum(acc2[...], axis=1)
                loss_ref[...] = jnp.zeros((SUBLANES, 128), F32) + (0.5 / D_MODEL) * jnp.sum(lacc[...])
        if comm is not None:
            @pl.when(i == n_t - 1)
            def _():
                comm["wait"](*comm_refs)

    tile = lambda c: pl.BlockSpec((tt, c), lambda i: (i, 0))
    full = lambda a: pl.BlockSpec(a.shape, lambda i: (0,) * a.ndim)
    hbm = pl.BlockSpec(memory_space=pl.ANY)
    ins = [x, win_b, wout_b, *prm, ln_g, ln_b] + ([target] if last else [])
    in_specs = [tile(D_MODEL), hbm, hbm] + [full(a) for a in (*prm, ln_g, ln_b)] + ([tile(D_MODEL)] if last else [])
    out_shape = [jax.ShapeDtypeStruct((t_len, D_IN), BF16), jax.ShapeDtypeStruct((t_len, D_MODEL), BF16),
                 jax.ShapeDtypeStruct((t_len, GW), BF16), jax.ShapeDtypeStruct((t_len, GW), BF16)]
    out_specs = [tile(D_IN), tile(D_MODEL), tile(GW), tile(GW)]
    if last:
        out_shape += [jax.ShapeDtypeStruct((t_len, D_MODEL), F32), jax.ShapeDtypeStruct((2, D_MODEL), F32),
                      jax.ShapeDtypeStruct((SUBLANES, 128), F32)]
        out_specs += [tile(D_MODEL), pl.BlockSpec((2, D_MODEL), lambda i: (0, 0)),
                      pl.BlockSpec((SUBLANES, 128), lambda i: (0, 0))]
    else:
        out_shape += [jax.ShapeDtypeStruct((t_len, D_MODEL), F32), jax.ShapeDtypeStruct((t_len, D_MODEL), F32)]
        out_specs += [tile(D_MODEL), tile(D_MODEL)]
    scratch = [
        pltpu.VMEM((D_MODEL, D_IN), BF16), pltpu.VMEM((D_MIX, D_MODEL), BF16),
        pltpu.VMEM((tt, D_MIX), BF16), pltpu.VMEM((tt, D_MODEL), F32),
        pltpu.VMEM((tt + SUBLANES, GW), F32), pltpu.VMEM((tt + HALO, GW), F32), pltpu.VMEM((tt + 16, GW), F32),
        pltpu.VMEM((SUBLANES, tt + HALO, GW), F32),
        pltpu.VMEM((tt, GW), F32),
    ]
    if last:
        scratch += [pltpu.VMEM((2, SUBLANES, D_MODEL), F32), pltpu.VMEM((SUBLANES, D_MODEL), F32)]
    n_in, n_out = len(ins), len(out_shape)
    ins, in_specs, out_shape, out_specs, scratch = _with_comm(comm, ins, in_specs, out_shape, out_specs, scratch)
    return pl.pallas_call(
        body, name=("fwd_last" if last else "fwd_layer") + ("" if comm is None else "_comm"), grid=(n_t,),
        in_specs=in_specs, out_specs=out_specs, out_shape=out_shape, scratch_shapes=scratch,
        compiler_params=pltpu.CompilerParams(dimension_semantics=("arbitrary",), vmem_limit_bytes=VMEM_LIMIT),
    )(*ins)


def _dsilu(z, sz):
    return sz * (1.0 + z * (1.0 - sz))


def _bwd_layer(dz, h, cb, p, win_b, wout_b, prm, z_prev, lng_prev, *, tt, comm=None):
    t_len = dz.shape[0]
    n_t = t_len // tt
    has_prev = z_prev is not None
    hb = tt // HALO
    group = min(WOUT_GROUP, n_t)
    assert n_t % group == 0

    def body(*refs):
        refs, comm_refs = _split_comm(refs, comm, n_in, n_out)
        dz_ref, h_ref, halo_ref, cb_ref, p_scr, win_hbm, wout_hbm, caw, cab, cbw, cbb, lbg, lbb, pw, pb, ps, cnt = refs[:17]
        k = 17
        if has_prev:
            zp_ref, lngp = refs[k:k + 2]
            k += 2
        dxo_ref, dh_ref, dwout_hbm, small_ref, dpw_ref = refs[k:k + 5]
        k += 5
        if has_prev:
            dlnp_ref = refs[k]
            k += 1
        (win_v, wout_v, dzb_all, dy_scr, y_all, dx_scr, q_ext, ub_ext, dca_ext, dcb_ext, dpn_ext, sh,
         pl_scr, dpl_scr, dp_scr, racc, dpw_acc, dwout_acc) = refs[k:k + 18]
        k += 18
        if has_prev:
            acc2 = refs[k]
        i = pl.program_id(0)
        ti = n_t - 1 - i
        t0 = ti * tt
        slot = i % group
        slot_rows = pl.ds(pl.multiple_of(slot * tt, tt), tt)
        dzb, y_scr = dzb_all.at[slot_rows], y_all.at[slot_rows]

        @pl.when(i == 0)
        def _():
            if comm is not None:
                comm["start"](*comm_refs)
            pltpu.sync_copy(win_hbm, win_v)
            pltpu.sync_copy(wout_hbm, wout_v)
            dca_ext[tt:tt + SUBLANES, :] = jnp.zeros((SUBLANES, GW), F32)
            dcb_ext[tt:tt + HALO, :] = jnp.zeros((HALO, GW), F32)
            dpn_ext[tt:tt + 16, :] = jnp.zeros((16, GW), F32)
            racc[...] = jnp.zeros_like(racc)
            dpw_acc[...] = jnp.zeros_like(dpw_acc)
            dwout_acc[...] = jnp.zeros_like(dwout_acc)
            if has_prev:
                acc2[...] = jnp.zeros_like(acc2)

        dzb[...] = dz_ref[...].astype(BF16)
        dy_scr[...] = lax.dot_general(dzb[...], wout_v[...], (((1,), (1,)), ((), ())), preferred_element_type=F32)

        live = (ti > 0).astype(F32)
        hh = lambda j, r0, r1: halo_ref[r0:r1, j * GW:(j + 1) * GW].astype(F32)
        q_ext[0:SUBLANES, :] = live * hh(1, 24, 32) * hh(2, 24, 32)
        ub_ext[0:HALO, :] = live * hh(4, 0, 32) * _sig(hh(5, 0, 32))

        def a1(base):
            q_ext[pl.ds(SUBLANES + base, RC), :] = _hcol(h_ref, 1, base) * _hcol(h_ref, 2, base)
        _chunks(tt, a1)
        _build_shifts(q_ext, sh, (6, 7), tt)

        def a2(base):
            rows = pl.ds(base, RC)
            q6, q7, q8 = _tap(q_ext, sh, 6, base), _tap(q_ext, sh, 7, base), _tap(q_ext, sh, 8, base)
            ca = cab[...] + caw[0:1, :] * q6 + caw[1:2, :] * q7 + caw[2:3, :] * q8
            bg, z = _hcol(h_ref, 0, base), _hcol(h_ref, 3, base)
            sz = _sig(z)
            sza = z * sz
            dya = dy_scr[rows, 0:GW]
            ya0 = bg * ca
            y_scr[rows, 0:GW] = (ya0 * sza).astype(BF16)
            dya0 = dya * sza
            dh_ref[rows, 3 * GW:4 * GW] = (dya * ya0 * _dsilu(z, sz)).astype(BF16)
            dh_ref[rows, 0:GW] = (dya0 * ca).astype(BF16)
            dca = dya0 * bg
            dca_ext[rows, :] = dca
            racc[R_DCAB] += _fold8(dca)
            racc[R_DWA + 0] += _fold8(dca * q6)
            racc[R_DWA + 1] += _fold8(dca * q7)
            racc[R_DWA + 2] += _fold8(dca * q8)
        _chunks(tt, a2)
        _build_shifts(dca_ext, sh, (1, 2), tt)

        def a3(base):
            rows = pl.ds(base, RC)
            dq = caw[0:1, :] * _tap(dca_ext, sh, 2, base) + caw[1:2, :] * _tap(dca_ext, sh, 1, base) \
                + caw[2:3, :] * _tap(dca_ext, sh, 0, base)
            dh_ref[rows, GW:2 * GW] = (dq * _hcol(h_ref, 2, base)).astype(BF16)
            dh_ref[rows, 2 * GW:3 * GW] = (dq * _hcol(h_ref, 1, base)).astype(BF16)
        _chunks(tt, a3)
        dca_ext[tt:tt + SUBLANES, :] = dca_ext[0:SUBLANES, :]

        def b1(base):
            ub_ext[pl.ds(HALO + base, RC), :] = _hcol(h_ref, 4, base) * _sig(_hcol(h_ref, 5, base))
        _chunks(tt, b1)
        _build_shifts(ub_ext, sh, range(1, 8), tt + HALO - SUBLANES)

        def b2(base):
            rows = pl.ds(base, RC)
            xhat, rstd = _ln_stats(cb_ref[rows, :].astype(F32))
            lnv = xhat * lbg[...] + lbb[...]
            sl = _sig(lnv)
            s = lnv * sl
            z = _hcol(h_ref, 6, base)
            sz = _sig(z)
            szb = z * sz
            y_scr[rows, GW:2 * GW] = (s * szb).astype(BF16)
            dyb = dy_scr[rows, GW:2 * GW]
            dh_ref[rows, 6 * GW:7 * GW] = (dyb * s * _dsilu(z, sz)).astype(BF16)
            dlnv = dyb * szb * _dsilu(lnv, sl)
            racc[R_DLBG] += _fold8(dlnv * xhat)
            racc[R_DLBB] += _fold8(dlnv)
            dcb = _ln_bwd(dlnv, xhat, rstd, lbg[...])
            dcb_ext[rows, :] = dcb
            racc[R_DCBB] += _fold8(dcb)

            def wtap(off, v):
                racc[R_DWB + off - 2] += _fold8(dcb * v)
            _for_taps(ub_ext, sh, base, range(2, 2 + KB), wtap)
        _chunks(tt, b2, unroll=4)
        _build_shifts(dcb_ext, sh, range(1, 8), tt + HALO - SUBLANES)

        def b3(base):
            rows = pl.ds(base, RC)
            dubv = [jnp.zeros((RC, GW), F32)]

            def tap(off, v):
                dubv[0] = dubv[0] + cbw[KB - 1 - off:KB - off, :] * v
            _for_taps(dcb_ext, sh, base, range(KB), tap)
            dub = dubv[0]
            v, gt = _hcol(h_ref, 4, base), _hcol(h_ref, 5, base)
            sg = _sig(gt)
            dh_ref[rows, 4 * GW:5 * GW] = (dub * sg).astype(BF16)
            dh_ref[rows, 5 * GW:6 * GW] = (dub * v * sg * (1.0 - sg)).astype(BF16)
        _chunks(tt, b3, unroll=4)
        dcb_ext[tt:tt + HALO, :] = dcb_ext[0:HALO, :]

        for g in range(len(POOL_WINDOWS)):
            lanes = slice(g * PGD, (g + 1) * PGD)
            pl_scr[:, lanes] = jnp.dot(p_scr[:, lanes], pw[g], preferred_element_type=F32)

        def c3(base):
            rows = pl.ds(base, RC)
            z = _hcol(h_ref, 8, base)
            sz = _sig(z)
            szc = z * sz
            plb = pl_scr[rows, :] + pb[...]
            yc0 = plb * ps[...]
            y_scr[rows, 2 * GW:3 * GW] = (yc0 * szc).astype(BF16)
            dyc = dy_scr[rows, 2 * GW:3 * GW]
            dh_ref[rows, 8 * GW:9 * GW] = (dyc * yc0 * _dsilu(z, sz)).astype(BF16)
            dyc0 = dyc * szc
            racc[R_DPS] += _fold8(dyc0 * plb)
            dpl = dyc0 * ps[...]
            racc[R_DPB] += _fold8(dpl)
            dpl_scr[rows, :] = dpl.astype(BF16)
        _chunks(tt, c3, unroll=4)
        for g in range(len(POOL_WINDOWS)):
            lanes = slice(g * PGD, (g + 1) * PGD)
            dpw_acc[g] += lax.dot_general(p_scr[:, lanes], dpl_scr[:, lanes], (((0,), (0,)), ((), ())),
                                          preferred_element_type=F32)
            dp_scr[:, lanes] = lax.dot_general(dpl_scr[:, lanes], pw[g], (((1,), (1,)), ((), ())),
                                               preferred_element_type=F32)

        def c4(base):
            rows = pl.ds(base, RC)
            dpn_ext[rows, :] = dp_scr[rows, :] * _inv_count(cnt, base + t0 == 0)
        _chunks(tt, c4)
        _build_shifts(dpn_ext, sh, range(1, 8), tt + SUBLANES)

        def c5(base):
            rows = pl.ds(base, RC)
            for g, w in enumerate(POOL_WINDOWS):
                lanes = slice(g * PGD, (g + 1) * PGD)
                acc = _tap(dpn_ext, sh, 0, base, lanes)
                for j in range(1, w):
                    acc = acc + _tap(dpn_ext, sh, j, base, lanes)
                dh_ref[rows, 7 * GW + g * PGD:7 * GW + (g + 1) * PGD] = (acc - dp_scr[rows, lanes]).astype(BF16)
        _chunks(tt, c5, unroll=4)
        dpn_ext[tt:tt + 16, :] = dpn_ext[0:16, :]

        @pl.when(slot == group - 1)
        def _():
            for r in range(D_MIX // GW):
                dwout_acc[r * GW:(r + 1) * GW, :] += lax.dot_general(
                    y_all[:, r * GW:(r + 1) * GW], dzb_all[...], (((0,), (0,)), ((), ())), preferred_element_type=F32)
        dx_scr[...] = lax.dot_general(dh_ref[...], win_v[...], (((1,), (1,)), ((), ())), preferred_element_type=F32)

        def post(base):
            rows = pl.ds(base, RC)
            dx = ALPHA * dz_ref[rows, :] + dx_scr[rows, :]
            if has_prev:
                xhat, rstd = _ln_stats(zp_ref[rows, :])
                acc2[0] += _fold8(dx * xhat)
                acc2[1] += _fold8(dx)
                dxo_ref[rows, :] = _ln_bwd(dx, xhat, rstd, lngp[...])
            else:
                dxo_ref[rows, :] = dx
        _chunks(tt, post, unroll=8)

        @pl.when(i == n_t - 1)
        def _():
            small_ref[...] = jnp.sum(racc[...], axis=1)
            dpw_ref[...] = dpw_acc[...]
            pltpu.sync_copy(dwout_acc, dwout_hbm)
            if has_prev:
                dlnp_ref[...] = jnp.sum(acc2[...], axis=1)
            if comm is not None:
                comm["wait"](*comm_refs)

    rtile = lambda c: pl.BlockSpec((tt, c), lambda i: (n_t - 1 - i, 0))
    full = lambda a: pl.BlockSpec(a.shape, lambda i: (0,) * a.ndim)
    const = lambda shp: pl.BlockSpec(shp, lambda i: (0,) * len(shp))
    hbm = pl.BlockSpec(memory_space=pl.ANY)
    halo_spec = pl.BlockSpec((HALO, D_IN), lambda i: (jnp.maximum((n_t - 1 - i) * hb - 1, 0), 0))
    ins = [dz, h, h, cb, p, win_b, wout_b, *prm] + ([z_prev, lng_prev] if has_prev else [])
    in_specs = [rtile(D_MODEL), rtile(D_IN), halo_spec, rtile(GW), rtile(GW), hbm, hbm] + [full(a) for a in prm] \
        + ([rtile(D_MODEL), full(lng_prev)] if has_prev else [])
    out_shape = [jax.ShapeDtypeStruct((t_len, D_MODEL), F32), jax.ShapeDtypeStruct((t_len, D_IN), BF16),
                 jax.ShapeDtypeStruct((D_MIX, D_MODEL), F32), jax.ShapeDtypeStruct((N_RACC, GW), F32),
                 jax.ShapeDtypeStruct((len(POOL_WINDOWS), PGD, PGD), F32)]
    out_specs = [rtile(D_MODEL), rtile(D_IN), hbm, const((N_RACC, GW)), const((len(POOL_WINDOWS), PGD, PGD))]
    if has_prev:
        out_shape.append(jax.ShapeDtypeStruct((2, D_MODEL), F32))
        out_specs.append(const((2, D_MODEL)))
    scratch = [
        pltpu.VMEM((D_MODEL, D_IN), BF16), pltpu.VMEM((D_MIX, D_MODEL), BF16),
        pltpu.VMEM((group * tt, D_MODEL), BF16), pltpu.VMEM((tt, D_MIX), F32), pltpu.VMEM((group * tt, D_MIX), BF16),
        pltpu.VMEM((tt, D_MODEL), F32),
        pltpu.VMEM((tt + SUBLANES, GW), F32), pltpu.VMEM((tt + HALO, GW), F32),
        pltpu.VMEM((tt + SUBLANES, GW), F32), pltpu.VMEM((tt + HALO, GW), F32), pltpu.VMEM((tt + 16, GW), F32),
        pltpu.VMEM((SUBLANES, tt + HALO, GW), F32),
        pltpu.VMEM((tt, GW), F32), pltpu.VMEM((tt, GW), BF16), pltpu.VMEM((tt, GW), F32),
        pltpu.VMEM((N_RACC, SUBLANES, GW), F32), pltpu.VMEM((len(POOL_WINDOWS), PGD, PGD), F32),
        pltpu.VMEM((D_MIX, D_MODEL), F32),
    ]
    if has_prev:
        scratch.append(pltpu.VMEM((2, SUBLANES, D_MODEL), F32))
    n_in, n_out = len(ins), len(out_shape)
    ins, in_specs, out_shape, out_specs, scratch = _with_comm(comm, ins, in_specs, out_shape, out_specs, scratch)
    return pl.pallas_call(
        body, name=("bwd_layer_prev" if has_prev else "bwd_layer") + ("" if comm is None else "_comm"), grid=(n_t,),
        in_specs=in_specs, out_specs=out_specs, out_shape=out_shape, scratch_shapes=scratch,
        compiler_params=pltpu.CompilerParams(dimension_semantics=("arbitrary",), vmem_limit_bytes=VMEM_LIMIT),
    )(*ins)


def _wgrad_in(xb, dh, *, tk, comm=None):
    t_len = xb.shape[0]
    tk = min(tk, t_len)
    n_k = t_len // tk

    def body(*refs):
        (x_ref, dh_ref, o_ref), comm_refs = _split_comm(refs, comm, 2, 1)
        j, k = pl.program_id(0), pl.program_id(1)

        @pl.when(k == 0)
        def _():
            o_ref[...] = jnp.zeros_like(o_ref)
        if comm is not None:
            @pl.when((j == 0) & (k == 0))
            def _():
                comm["start"](*comm_refs)
        o_ref[0] += lax.dot_general(x_ref[...], dh_ref[...], (((0,), (0,)), ((), ())), preferred_element_type=F32)
        if comm is not None:
            @pl.when((j == N_CHIP - 1) & (k == n_k - 1))
            def _():
                comm["wait"](*comm_refs)

    ins = [xb, dh]
    in_specs = [pl.BlockSpec((tk, D_MODEL), lambda j, k: (k, 0)), pl.BlockSpec((tk, SHARD_IN), lambda j, k: (k, j))]
    out_shape = [jax.ShapeDtypeStruct((N_CHIP, D_MODEL, SHARD_IN), F32)]
    out_specs = [pl.BlockSpec((1, D_MODEL, SHARD_IN), lambda j, k: (j, 0, 0))]
    ins, in_specs, out_shape, out_specs, scratch = _with_comm(comm, ins, in_specs, out_shape, out_specs, [])
    outs = pl.pallas_call(
        body, name="wgrad_in" + ("" if comm is None else "_comm"), grid=(N_CHIP, n_k),
        in_specs=in_specs, out_specs=out_specs, out_shape=out_shape, scratch_shapes=scratch,
        compiler_params=pltpu.CompilerParams(dimension_semantics=("arbitrary", "arbitrary"), vmem_limit_bytes=VMEM_LIMIT),
    )(*ins)
    return outs[0] if comm is None else outs


MESH = pl.DeviceIdType.MESH
ANY = pl.BlockSpec(memory_space=pl.ANY)


def _place():
    x, y, c = lax.axis_index("x"), lax.axis_index("y"), lax.axis_index("c")
    others = [(1 - x, y), (x, 1 - y), (1 - x, 1 - y)]
    return x, y, c, 2 * x + y, [(ox, oy, 2 * ox + oy) for ox, oy in others]


def _rcopy(src, dst, send_sems, recv_sems, k, dev):
    return pltpu.make_async_remote_copy(src_ref=src, dst_ref=dst, send_sem=send_sems.at[k], recv_sem=recv_sems.at[k],
                                        device_id=dev, device_id_type=MESH)


def _gather_weights(w_in, w_out, cw):
    hi, ho = D_MODEL // 2, SHARD_OUT // 2

    def body(win_ref, wout_ref, cw_ref, owin, owout, ocw, bin_v, bout_v, send_sems, recv_sems, lsem):
        x, y, c, me, others = _place()
        for l in range(DEPTH):
            for r0 in range(0, D_MODEL, 256):
                bin_v[l, r0:r0 + 256, :] = win_ref[l, r0:r0 + 256, :].astype(BF16)
            bout_v[l] = wout_ref[l].astype(BF16)
        cin = pl.ds(pl.multiple_of(me * SHARD_IN, 128), SHARD_IN)
        rout = pl.ds(pl.multiple_of(me * SHARD_OUT, 128), SHARD_OUT)
        local = [pltpu.make_async_copy(bin_v.at[0], owin.at[:, cin], lsem.at[0]),
                 pltpu.make_async_copy(bout_v.at[0], owout.at[rout, :], lsem.at[1]),
                 pltpu.make_async_copy(cw_ref, ocw.at[me], lsem.at[2])]
        for cp in local:
            cp.start()

        def in_half(chip, core):
            return owin.at[pl.ds(pl.multiple_of(core * hi, 256), hi), pl.ds(pl.multiple_of(chip * SHARD_IN, 128), SHARD_IN)]

        def out_half(chip, core):
            return owout.at[pl.ds(pl.multiple_of(chip * SHARD_OUT + core * ho, 64), ho), :]

        first = []
        for k, (ox, oy, _) in enumerate(others):
            dev = (ox, oy, c)
            first.append(_rcopy(bin_v.at[0, pl.ds(pl.multiple_of(c * hi, 256), hi), :], in_half(me, c), send_sems, recv_sems, k, dev))
            first.append(_rcopy(bout_v.at[0, pl.ds(pl.multiple_of(c * ho, 64), ho), :], out_half(me, c), send_sems, recv_sems, 3 + k, dev))
            first.append(_rcopy(cw_ref, ocw.at[me], send_sems, recv_sems, 6 + k, dev))
        for cp in first:
            cp.start()
        sib = (x, y, 1 - c)
        passed = []
        for k, (ox, oy, oc) in enumerate(others):
            _rcopy(in_half(oc, c), in_half(oc, c), send_sems, recv_sems, k, sib).wait_recv()
            fwd_in = _rcopy(in_half(oc, c), in_half(oc, c), send_sems, recv_sems, 9 + k, sib)
            fwd_in.start()
            _rcopy(out_half(oc, c), out_half(oc, c), send_sems, recv_sems, 3 + k, sib).wait_recv()
            fwd_out = _rcopy(out_half(oc, c), out_half(oc, c), send_sems, recv_sems, 12 + k, sib)
            fwd_out.start()
            passed += [fwd_in, fwd_out]
        for k, (ox, oy, oc) in enumerate(others):
            _rcopy(cw_ref, ocw.at[oc], send_sems, recv_sems, 6 + k, sib).wait_recv()
            _rcopy(in_half(oc, 1 - c), in_half(oc, 1 - c), send_sems, recv_sems, 9 + k, sib).wait_recv()
            _rcopy(out_half(oc, 1 - c), out_half(oc, 1 - c), send_sems, recv_sems, 12 + k, sib).wait_recv()
        for cp in first + passed:
            cp.wait_send()
        for cp in local:
            cp.wait()

    vm = pl.BlockSpec(memory_space=pltpu.VMEM)
    return pl.pallas_call(
        body, name="gather_weights",
        in_specs=[vm, vm, vm], out_specs=[ANY, ANY, ANY, vm, vm],
        out_shape=[jax.ShapeDtypeStruct((D_MODEL, D_IN), BF16), jax.ShapeDtypeStruct((D_MIX, D_MODEL), BF16),
                   jax.ShapeDtypeStruct((N_CHIP,) + cw.shape, F32),
                   jax.ShapeDtypeStruct((DEPTH, D_MODEL, SHARD_IN), BF16), jax.ShapeDtypeStruct((DEPTH, SHARD_OUT, D_MODEL), BF16)],
        scratch_shapes=[pltpu.SemaphoreType.DMA((15,)), pltpu.SemaphoreType.DMA((15,)), pltpu.SemaphoreType.DMA((3,))],
        compiler_params=pltpu.CompilerParams(vmem_limit_bytes=VMEM_LIMIT),
    )(w_in, w_out, cw)


def _gather_starts(bsh_in, bsh_out, owin, owout, send_sems, recv_sems, lsem, layer):
    x, y, c, me, others = _place()
    hi, ho = D_MODEL // 2, SHARD_OUT // 2
    pltpu.make_async_copy(bsh_in.at[layer], owin.at[:, pl.ds(pl.multiple_of(me * SHARD_IN, 128), SHARD_IN)], lsem.at[0]).start()
    pltpu.make_async_copy(bsh_out.at[layer], owout.at[pl.ds(pl.multiple_of(me * SHARD_OUT, 128), SHARD_OUT), :], lsem.at[1]).start()
    for k, (ox, oy, _) in enumerate(others):
        for t in range(2):
            pltpu.make_async_remote_copy(
                src_ref=bsh_in.at[layer, pl.ds(pl.multiple_of(c * hi, 256), hi), :],
                dst_ref=owin.at[pl.ds(pl.multiple_of(c * hi, 256), hi), pl.ds(pl.multiple_of(me * SHARD_IN, 128), SHARD_IN)],
                send_sem=send_sems.at[2 * k + t], recv_sem=recv_sems.at[2 * k + c], device_id=(ox, oy, t), device_id_type=MESH).start()
            pltpu.make_async_remote_copy(
                src_ref=bsh_out.at[layer, pl.ds(pl.multiple_of(c * ho, 64), ho), :],
                dst_ref=owout.at[pl.ds(pl.multiple_of(me * SHARD_OUT + c * ho, 64), ho), :],
                send_sem=send_sems.at[6 + 2 * k + t], recv_sem=recv_sems.at[6 + 2 * k + c], device_id=(ox, oy, t), device_id_type=MESH).start()


def _gather_waits(bsh_in, bsh_out, owin, owout, send_sems, recv_sems, lsem, layer):
    x, y, c, me, others = _place()
    hi, ho = D_MODEL // 2, SHARD_OUT // 2
    src_in = bsh_in.at[layer, pl.ds(0, hi), :]
    src_out = bsh_out.at[layer, pl.ds(0, ho), :]
    for k, (ox, oy, oc) in enumerate(others):
        for t in range(2):
            dst_in = owin.at[pl.ds(t * hi, hi), pl.ds(pl.multiple_of(oc * SHARD_IN, 128), SHARD_IN)]
            dst_out = owout.at[pl.ds(pl.multiple_of(oc * SHARD_OUT + t * ho, 64), ho), :]
            a = pltpu.make_async_remote_copy(src_ref=src_in, dst_ref=dst_in, send_sem=send_sems.at[2 * k + t],
                                             recv_sem=recv_sems.at[2 * k + t], device_id=(ox, oy, t), device_id_type=MESH)
            b = pltpu.make_async_remote_copy(src_ref=src_out, dst_ref=dst_out, send_sem=send_sems.at[6 + 2 * k + t],
                                             recv_sem=recv_sems.at[6 + 2 * k + t], device_id=(ox, oy, t), device_id_type=MESH)
            a.wait_send()
            a.wait_recv()
            b.wait_send()
            b.wait_recv()
    pltpu.make_async_copy(bsh_in.at[layer], owin.at[:, pl.ds(pl.multiple_of(me * SHARD_IN, 128), SHARD_IN)], lsem.at[0]).wait()
    pltpu.make_async_copy(bsh_out.at[layer], owout.at[pl.ds(pl.multiple_of(me * SHARD_OUT, 128), SHARD_OUT), :], lsem.at[1]).wait()


def _gather_comm(bsh_in, bsh_out, layer):
    return dict(ins=[bsh_in, bsh_out],
                out_shape=[jax.ShapeDtypeStruct((D_MODEL, D_IN), BF16), jax.ShapeDtypeStruct((D_MIX, D_MODEL), BF16)],
                sems=[pltpu.SemaphoreType.DMA((12,)), pltpu.SemaphoreType.DMA((12,)), pltpu.SemaphoreType.DMA((2,))],
                start=lambda ins, outs, sems: _gather_starts(ins[0], ins[1], outs[0], outs[1], *sems, layer),
                wait=lambda ins, outs, sems: _gather_waits(ins[0], ins[1], outs[0], outs[1], *sems, layer))


def _exchange_halves(arrs, tag):
    n = len(arrs)

    def body(*refs):
        ins, outs, (send_sems, recv_sems) = refs[:n], refs[n:2 * n], refs[2 * n:]
        x, y, c, _, _ = _place()
        cps = []
        for m in range(n):
            half = ins[m].shape[1] // 2
            cps.append(_rcopy(ins[m].at[:, pl.ds(pl.multiple_of((1 - c) * half, SUBLANES), half), :], outs[m],
                              send_sems, recv_sems, m, (x, y, 1 - c)))
        for cp in cps:
            cp.start()
        for cp in cps:
            cp.wait()

    return pl.pallas_call(
        body, name="exchange_halves_" + tag, in_specs=[ANY] * n, out_specs=[ANY] * n,
        out_shape=[jax.ShapeDtypeStruct((a.shape[0], a.shape[1] // 2, a.shape[2]), F32) for a in arrs],
        scratch_shapes=[pltpu.SemaphoreType.DMA((n,)), pltpu.SemaphoreType.DMA((n,))],
    )(*arrs)


def _add_own_half(a, got, core, *, rb, dtype):
    nj, r, cdim = a.shape
    half = r // 2

    def body(core_ref, a_ref, g_ref, o_ref):
        o_ref[...] = (a_ref[0] + g_ref[...]).astype(dtype)

    return pl.pallas_call(
        body, name="add_own_half",
        grid_spec=pltpu.PrefetchScalarGridSpec(
            num_scalar_prefetch=1, grid=(nj, half // rb),
            in_specs=[pl.BlockSpec((1, 1, rb, cdim), lambda j, i, cr: (j, cr[0], i, 0)),
                      pl.BlockSpec((1, rb, cdim), lambda j, i, cr: (j, i, 0))],
            out_specs=pl.BlockSpec((1, rb, cdim), lambda j, i, cr: (j, i, 0))),
        out_shape=jax.ShapeDtypeStruct((nj, half, cdim), dtype),
    )(core, a.reshape(nj, 2, half, cdim), got)


def _owner_starts(ins, outs, sems):
    send_sems, recv_sems, lsem = sems
    x, y, c, me, others = _place()
    for m in range(len(ins)):
        pltpu.make_async_copy(ins[m].at[me], outs[m].at[me], lsem.at[m]).start()
        for k, (ox, oy, oc) in enumerate(others):
            _rcopy(ins[m].at[oc], outs[m].at[me], send_sems, recv_sems, 3 * m + k, (ox, oy, c)).start()


def _owner_waits(ins, outs, sems):
    send_sems, recv_sems, lsem = sems
    x, y, c, me, others = _place()
    for m in range(len(ins)):
        for k, (ox, oy, oc) in enumerate(others):
            _rcopy(ins[m].at[oc], outs[m].at[oc], send_sems, recv_sems, 3 * m + k, (ox, oy, c)).wait()
        pltpu.make_async_copy(ins[m].at[me], outs[m].at[me], lsem.at[m]).wait()


def _owner_comm(arrs):
    n = len(arrs)
    return dict(ins=arrs, out_shape=[jax.ShapeDtypeStruct(a.shape, a.dtype) for a in arrs],
                sems=[pltpu.SemaphoreType.DMA((3 * n,)), pltpu.SemaphoreType.DMA((3 * n,)), pltpu.SemaphoreType.DMA((n,))],
                start=_owner_starts, wait=_owner_waits)


def _send_to_owners(arrs):
    n = len(arrs)

    def body(*refs):
        ins, outs, sems = refs[:n], refs[n:2 * n], refs[2 * n:]
        _owner_starts(ins, outs, sems)
        _owner_waits(ins, outs, sems)

    job = _owner_comm(arrs)
    return pl.pallas_call(
        body, name="send_to_owners", in_specs=[ANY] * n, out_specs=[ANY] * n,
        out_shape=job["out_shape"], scratch_shapes=job["sems"],
    )(*arrs)


def _sum_chips(a, *, rb):
    nj, r, cdim = a.shape

    def body(a_ref, o_ref):
        f = lambda k: a_ref[k].astype(F32)
        o_ref[...] = ((f(0) + f(1)) + f(2)) + f(3)

    return pl.pallas_call(
        body, name="sum_chips", grid=(r // rb,),
        in_specs=[pl.BlockSpec((nj, rb, cdim), lambda i: (0, i, 0))],
        out_specs=pl.BlockSpec((rb, cdim), lambda i: (i, 0)),
        out_shape=jax.ShapeDtypeStruct((r, cdim), F32),
    )(a)


def _sum_chips_into(a, dest, layer, core, *, rb):
    nj, half, cdim = a.shape
    nb = half // rb

    def body(*refs):
        a_ref, o_ref = refs[1], refs[-1]
        f = lambda k: a_ref[k].astype(F32)
        o_ref[0] = ((f(0) + f(1)) + f(2)) + f(3)

    grid_spec = pltpu.PrefetchScalarGridSpec(
        num_scalar_prefetch=1, grid=(nb,),
        in_specs=[pl.BlockSpec((nj, rb, cdim), lambda i, cr: (0, i, 0))] + ([] if dest is None else [ANY]),
        out_specs=pl.BlockSpec((1, rb, cdim), lambda i, cr: (layer, cr[0] * nb + i, 0)))
    return pl.pallas_call(
        body, name="sum_chips_into", grid_spec=grid_spec,
        out_shape=jax.ShapeDtypeStruct((DEPTH, 2 * half, cdim), F32),
        input_output_aliases={} if dest is None else {2: 0},
    )(*([core, a] if dest is None else [core, a, dest]))


def _spread_reduced(g_in, g_out, red_small):
    hs = red_small.shape[0]

    def body(gin_in, gout_in, sm, gin, gout, fsm, gsm, send_sems, recv_sems, lsem):
        x, y, c, me, others = _place()
        sib = (x, y, 1 - c)
        hi, ho = D_MODEL // 2, SHARD_OUT // 2
        ri, ro = pl.ds(pl.multiple_of(c * hi, SUBLANES), hi), pl.ds(pl.multiple_of(c * ho, SUBLANES), ho)
        remote = [_rcopy(gin.at[:, ri, :], gin.at[:, ri, :], send_sems, recv_sems, 0, sib),
                  _rcopy(gout.at[:, ro, :], gout.at[:, ro, :], send_sems, recv_sems, 1, sib)]
        own_small = pltpu.make_async_copy(sm, gsm.at[me], lsem.at[0])
        small = [_rcopy(sm, gsm.at[me], send_sems, recv_sems, 2 + k, (ox, oy, c)) for k, (ox, oy, _) in enumerate(others)]
        for cp in remote + [own_small] + small:
            cp.start()
        own_small.wait()
        for cp in small:
            cp.wait()
        mine = fsm.at[:, pl.ds(pl.multiple_of(c * hs, SUBLANES), hs), :]
        keep = pltpu.make_async_copy(gsm, mine, lsem.at[1])
        give = _rcopy(gsm, mine, send_sems, recv_sems, 5, sib)
        keep.start()
        give.start()
        for cp in remote + [give]:
            cp.wait()
        keep.wait()

    return pl.pallas_call(
        body, name="spread_reduced", in_specs=[ANY] * 3, out_specs=[ANY] * 4,
        out_shape=[jax.ShapeDtypeStruct(g_in.shape, F32), jax.ShapeDtypeStruct(g_out.shape, F32),
                   jax.ShapeDtypeStruct((N_CHIP, 2 * hs, GW), F32), jax.ShapeDtypeStruct((N_CHIP, hs, GW), F32)],
        input_output_aliases={0: 0, 1: 1},
        scratch_shapes=[pltpu.SemaphoreType.DMA((6,)), pltpu.SemaphoreType.DMA((6,)), pltpu.SemaphoreType.DMA((2,))],
    )(g_in, g_out, red_small)[:3]


def _adamw_math(w, g, m, v):
    m = ADAM_B1 * m + (1.0 - ADAM_B1) * g
    v = ADAM_B2 * v + (1.0 - ADAM_B2) * (g * g)
    m_hat = m / (1.0 - ADAM_B1 ** ADAM_STEP)
    v_hat = v / (1.0 - ADAM_B2 ** ADAM_STEP)
    delta = -ADAM_LR * (m_hat / (jnp.sqrt(v_hat) + ADAM_EPS) + ADAM_WD * w)
    return delta, m, v


def _adamw_big(w, g, m, v, *, rb):
    r, cdim = w.shape

    def body(w_ref, g_ref, m_ref, v_ref, d_ref, nm_ref, nv_ref):
        d_ref[...], nm_ref[...], nv_ref[...] = _adamw_math(w_ref[...], g_ref[...], m_ref[...], v_ref[...])

    spec = pl.BlockSpec((rb, cdim), lambda i: (i, 0))
    return pl.pallas_call(
        body, name="adamw_big", grid=(r // rb,), in_specs=[spec] * 4, out_specs=[spec] * 3,
        out_shape=[jax.ShapeDtypeStruct((r, cdim), F32)] * 3,
    )(w, g, m, v)


def _adamw_small(ws, gs, ms, vs):
    n = len(ws)

    def body(*refs):
        w, g, m, v = refs[:n], refs[n:2 * n], refs[2 * n:3 * n], refs[3 * n:4 * n]
        d, nm, nv = refs[4 * n:5 * n], refs[5 * n:6 * n], refs[6 * n:7 * n]
        for k in range(n):
            d[k][...], nm[k][...], nv[k][...] = _adamw_math(w[k][...], g[k][...], m[k][...], v[k][...])

    shapes = [jax.ShapeDtypeStruct(a.shape, F32) for a in ws]
    outs = pl.pallas_call(body, name="adamw_small", out_shape=shapes * 3)(*ws, *gs, *ms, *vs)
    return outs[:n], outs[n:2 * n], outs[2 * n:]


TT = 256
TK = 2048
CW_ROWS = 40
PACK_ROWS = 192


def _pack(rows):
    packed = jnp.concatenate(rows, axis=0)
    packed = jnp.pad(packed, ((0, PACK_ROWS - packed.shape[0]), (0, 0)))
    return packed.reshape(N_CHIP, PACK_ROWS // N_CHIP, GW)


def _reduce_to_owner_halves(parts, core1, tag):
    got = _exchange_halves(parts, tag)
    rbs = {D_MODEL: 256, SHARD_OUT: SHARD_OUT // 2, PACK_ROWS // N_CHIP: PACK_ROWS // N_CHIP // 2}
    return [_add_own_half(a, g, core1, rb=rbs[a.shape[1]], dtype=F32 if a.shape[1] == PACK_ROWS // N_CHIP else BF16)
            for a, g in zip(parts, got)]


def kernel(x, w_in, conv_a_w, conv_a_b, conv_b_w, conv_b_b, ln_b_g, ln_b_b, pool_w, pool_b, pool_scale, w_out, ln_g, ln_b, loss_target, m_w_in, m_conv_a_w, m_conv_a_b, m_conv_b_w, m_conv_b_b, m_ln_b_g, m_ln_b_b, m_pool_w, m_pool_b, m_pool_scale, m_w_out, m_ln_g, m_ln_b, v_w_in, v_conv_a_w, v_conv_a_b, v_conv_b_w, v_conv_b_b, v_ln_b_g, v_ln_b_b, v_pool_w, v_pool_b, v_pool_scale, v_w_out, v_ln_g, v_ln_b):
    chip = 2 * lax.axis_index("x") + lax.axis_index("y")
    core1 = lax.axis_index("c").reshape(1).astype(jnp.int32)
    x2, tgt = x[0], loss_target[0]

    cw = jnp.zeros((DEPTH, CW_ROWS, PGD), F32).at[:, 0:KA].set(conv_a_w).at[:, 8:8 + KB].set(conv_b_w)
    win0_b, wout0_b, cw_all, bsh_in, bsh_out = _gather_weights(w_in, w_out, cw)
    cw_full = jnp.transpose(cw_all, (1, 2, 0, 3)).reshape(DEPTH, CW_ROWS, GW)
    row = lambda a, l: a[l].reshape(1, -1)
    cnt = _count_table()
    prm = [(cw_full[l, 0:KA], row(conv_a_b, l), cw_full[l, 8:8 + KB], row(conv_b_b, l), row(ln_b_g, l), row(ln_b_b, l),
            pool_w[l].astype(BF16), row(pool_b, l), row(pool_scale, l), cnt) for l in range(DEPTH)]

    h0, xb0, cb0, pool0, z0, x1, win1_b, wout1_b = _fwd_layer(x2, win0_b, wout0_b, prm[0], row(ln_g, 0), row(ln_b, 0), None, tt=TT, last=False,
                                                  comm=_gather_comm(bsh_in, bsh_out, 1))
    h1, xb1, cb1, pool1, dz1, dln1, loss8 = _fwd_layer(x1, win1_b, wout1_b, prm[1], row(ln_g, 1), row(ln_b, 1), tgt, tt=TT, last=True)

    dz0, dh1, dwout1, small1, dpw1, dln0 = _bwd_layer(dz1, h1, cb1, pool1, win1_b, wout1_b, prm[1], z0, row(ln_g, 0), tt=TT)
    dwin1 = _wgrad_in(xb1, dh1, tk=TK)
    loss_row = jnp.pad(loss8, ((0, 0), (0, GW - loss8.shape[1])))
    pack1 = _pack([small1, dpw1.reshape(PGD, GW), dln1.reshape(4, GW), dln0.reshape(4, GW), loss_row])
    sums1 = _reduce_to_owner_halves([dwin1, dwout1.reshape(N_CHIP, SHARD_OUT, D_MODEL), pack1], core1, "1")
    gx, dh0, dwout0, small0, dpw0 = _bwd_layer(dz0, h0, cb0, pool0, win0_b, wout0_b, prm[0], None, None, tt=TT)
    pack0 = _pack([small0, dpw0.reshape(PGD, GW)])
    sums0 = _reduce_to_owner_halves([dwout0.reshape(N_CHIP, SHARD_OUT, D_MODEL), pack0], core1, "0")
    dwin0, *landed = _wgrad_in(xb0, dh0, tk=TK, comm=_owner_comm(sums1 + sums0))
    landed1, landed0 = landed[:3], landed[3:]
    landed0 = list(_send_to_owners(_reduce_to_owner_halves([dwin0], core1, "in0"))) + list(landed0)

    g_in = _sum_chips_into(landed0[0], _sum_chips_into(landed1[0], None, 1, core1, rb=256), 0, core1, rb=256)
    g_out = _sum_chips_into(landed0[1], _sum_chips_into(landed1[1], None, 1, core1, rb=SHARD_OUT // 2), 0, core1, rb=SHARD_OUT // 2)
    red_small = jnp.concatenate([_sum_chips(a, rb=PACK_ROWS // N_CHIP // 2) for a in (landed0[2], landed1[2])], axis=0)
    g_in, g_out, g_small = _spread_reduced(g_in, g_out, red_small)

    flat = lambda a: a.reshape(-1, a.shape[-1])
    unflat = lambda a, like: a.reshape(like.shape)
    d_in, nm_in, nv_in = [unflat(a, w_in) for a in _adamw_big(flat(w_in), flat(g_in), flat(m_w_in), flat(v_w_in), rb=256)]
    d_out, nm_out, nv_out = [unflat(a, w_out) for a in _adamw_big(flat(w_out), flat(g_out), flat(m_w_out), flat(v_w_out), rb=SHARD_OUT)]

    hp = PACK_ROWS // N_CHIP // 2
    unpack = lambda o: jnp.concatenate([g_small[:, o:o + hp], g_small[:, 2 * hp + o:3 * hp + o]], axis=1).reshape(PACK_ROWS, GW)
    p0, p1 = unpack(0), unpack(hp)
    small = [p0[0:N_RACC], p1[0:N_RACC]]
    dpw = [p[N_RACC:N_RACC + PGD].reshape(len(POOL_WINDOWS), PGD, PGD) for p in (p0, p1)]
    o = N_RACC + PGD
    g_lng = jnp.stack([p1[o + 4:o + 8].reshape(2, D_MODEL)[0], p1[o:o + 4].reshape(2, D_MODEL)[0]])
    g_lnb = jnp.stack([p1[o + 4:o + 8].reshape(2, D_MODEL)[1], p1[o:o + 4].reshape(2, D_MODEL)[1]])
    mine = lambda a: lax.dynamic_slice_in_dim(a, chip * PGD, PGD, axis=-1)
    stack = lambda f: jnp.stack([f(0), f(1)])
    g_caw = stack(lambda l: mine(small[l][R_DWA:R_DWA + KA]))
    g_cab = stack(lambda l: small[l][R_DCAB])
    g_cbw = stack(lambda l: mine(small[l][R_DWB:R_DWB + KB]))
    g_cbb = stack(lambda l: small[l][R_DCBB])
    g_lbg = stack(lambda l: small[l][R_DLBG])
    g_lbb = stack(lambda l: small[l][R_DLBB])
    g_pw = stack(lambda l: dpw[l])
    g_pb = stack(lambda l: small[l][R_DPB].reshape(len(POOL_WINDOWS), PGD))
    g_ps = stack(lambda l: small[l][R_DPS])
    ws = [conv_a_w, conv_a_b, conv_b_w, conv_b_b, ln_b_g, ln_b_b, pool_w, pool_b, pool_scale, ln_g, ln_b]
    gs = [g_caw, g_cab, g_cbw, g_cbb, g_lbg, g_lbb, g_pw, g_pb, g_ps, g_lng, g_lnb]
    ms = [m_conv_a_w, m_conv_a_b, m_conv_b_w, m_conv_b_b, m_ln_b_g, m_ln_b_b, m_pool_w, m_pool_b, m_pool_scale, m_ln_g, m_ln_b]
    vs = [v_conv_a_w, v_conv_a_b, v_conv_b_w, v_conv_b_b, v_ln_b_g, v_ln_b_b, v_pool_w, v_pool_b, v_pool_scale, v_ln_g, v_ln_b]
    ds, nms, nvs = _adamw_small(ws, gs, ms, vs)

    loss = p1[o + 8, 0]

    def order(in_, small_, out_):
        return [in_, *small_[:9], out_, *small_[9:]]
    return (loss, gx[None], *order(g_in, gs, g_out), *order(d_in, ds, d_out), *order(nm_in, nms, nm_out), *order(nv_in, nvs, nv_out))
```

```python
import functools

import jax
import jax.numpy as jnp
import numpy as np
from jax import lax
from jax.experimental import pallas as pl
from jax.experimental.pallas import tpu as pltpu

F32 = jnp.float32
BF16 = jnp.bfloat16

D_MODEL = 1024
DEPTH = 2
GW = 512
D_IN = 9 * GW
D_MIX = 3 * GW
NG = D_IN // GW
POOL_WINDOWS = (2, 4, 8, 16)
PGD = 128
KA = 3
KB = 31
ALPHA = (2.0 * DEPTH) ** 0.25
LN_EPS = 1e-5
ADAM_LR, ADAM_B1, ADAM_B2, ADAM_EPS, ADAM_WD, ADAM_STEP = 0.001, 0.9, 0.999, 1e-08, 0.01, 10

N_CHIP = 4
SHARD_IN = D_IN // N_CHIP
SHARD_OUT = D_MIX // N_CHIP

SUBLANES = 8
RC = 32
HALO = 32
VMEM_LIMIT = 60 * 1024 * 1024
WOUT_GROUP = 4

R_DWA, R_DCAB, R_DWB, R_DCBB, R_DLBG, R_DLBB, R_DPB, R_DPS, N_RACC = 0, 3, 4, 35, 36, 37, 38, 39, 40


def _sig(v):
    return 0.5 * jnp.tanh(0.5 * v) + 0.5


def _chunks(n_rows, fn, unroll=1, extra=None):
    unroll = min(unroll, n_rows // RC)

    def step(m, carry):
        for u in range(unroll):
            fn(pl.multiple_of((m * unroll + u) * RC, RC))
        if extra is not None:
            extra(m)
        return carry
    lax.fori_loop(0, n_rows // (RC * unroll), step, 0)


def _fold8(v):
    return v.reshape(RC // SUBLANES, SUBLANES, v.shape[-1]).sum(axis=0)


def _build_shifts(ext_ref, sh_ref, shifts, n_rows):
    for r in shifts:
        for c0 in range(0, n_rows, RC):
            n = min(RC, n_rows - c0)
            sh_ref[r, pl.ds(c0, n), :] = ext_ref[pl.ds(c0 + r, n), :]


def _tap(ext_ref, sh_ref, off, base, lanes=None):
    a, r = divmod(off, SUBLANES)
    src = ext_ref if r == 0 else sh_ref.at[r]
    if lanes is None:
        return src[pl.ds(base + SUBLANES * a, RC), :]
    return src[pl.ds(base + SUBLANES * a, RC), lanes]


def _ln_stats(v):
    mu = jnp.mean(v, axis=-1, keepdims=True)
    vc = v - mu
    var = jnp.mean(vc * vc, axis=-1, keepdims=True)
    rstd = lax.rsqrt(var + LN_EPS)
    return vc * rstd, rstd


def _ln_bwd(dy, xhat, rstd, g):
    dxh = dy * g
    m1 = jnp.mean(dxh, axis=-1, keepdims=True)
    m2 = jnp.mean(dxh * xhat, axis=-1, keepdims=True)
    return rstd * (dxh - m1 - xhat * m2)


def _for_taps(ext_ref, sh_ref, base, offsets, fn):
    for r in range(SUBLANES):
        offs = [o for o in offsets if o % SUBLANES == r]
        if not offs:
            continue
        a0, a1 = min(offs) // SUBLANES, max(offs) // SUBLANES
        src = ext_ref if r == 0 else sh_ref.at[r]
        win = src[pl.ds(base + SUBLANES * a0, RC + SUBLANES * (a1 - a0)), :]
        for o in offs:
            a = o // SUBLANES - a0
            fn(o, win[SUBLANES * a:SUBLANES * a + RC])


def _count_table():
    t = np.arange(1, RC + 1, dtype=np.float64)[:, None]
    w = np.repeat(np.asarray(POOL_WINDOWS, np.float64), PGD)[None, :]
    return jnp.asarray(1.0 / np.minimum(t, w), F32)


def _inv_count(cnt_ref, first):
    return jnp.where(first, cnt_ref[...], cnt_ref[RC - 1:RC, :])


def _hcol(h_ref, j, base):
    if len(h_ref.shape) == 3:
        return h_ref[j, pl.ds(base, RC), :].astype(F32)
    return h_ref[pl.ds(base, RC), j * GW:(j + 1) * GW].astype(F32)


def _with_comm(comm, ins, in_specs, out_shape, out_specs, scratch):
    if comm is None:
        return ins, in_specs, out_shape, out_specs, scratch
    hbm = pl.BlockSpec(memory_space=pl.ANY)
    return (ins + list(comm["ins"]), in_specs + [hbm] * len(comm["ins"]), out_shape + list(comm["out_shape"]),
            out_specs + [hbm] * len(comm["out_shape"]), scratch + list(comm["sems"]))


def _split_comm(refs, comm, n_in, n_out):
    refs = list(refs)
    if comm is None:
        return refs, None
    ci, co, cs = len(comm["ins"]), len(comm["out_shape"]), len(comm["sems"])
    own = refs[:n_in] + refs[n_in + ci:n_in + ci + n_out] + refs[n_in + ci + n_out + co:len(refs) - cs]
    return own, (refs[n_in:n_in + ci], refs[n_in + ci + n_out:n_in + ci + n_out + co], refs[len(refs) - cs:])


def _fwd_mixers(h_ref, cb_ref, y_scr, q_ext, ub_ext, cu_ext, sh, p_scr, pl_scr, prm, tt, t0):
    caw, cab, cbw, cbb, lbg, lbb, pw, pb, ps, cnt = prm

    def a1(base):
        q_ext[pl.ds(SUBLANES + base, RC), :] = _hcol(h_ref, 1, base) * _hcol(h_ref, 2, base)
    _chunks(tt, a1)
    _build_shifts(q_ext, sh, (6, 7), tt)

    def a2(base):
        ca = cab[...] + caw[0:1, :] * _tap(q_ext, sh, 6, base) + caw[1:2, :] * _tap(q_ext, sh, 7, base) \
            + caw[2:3, :] * _tap(q_ext, sh, 8, base)
        z = _hcol(h_ref, 3, base)
        y_scr[pl.ds(base, RC), 0:GW] = (_hcol(h_ref, 0, base) * ca * (z * _sig(z))).astype(BF16)
    _chunks(tt, a2, unroll=2)
    q_ext[0:SUBLANES, :] = q_ext[tt:tt + SUBLANES, :]

    def b1(base):
        ub_ext[pl.ds(HALO + base, RC), :] = _hcol(h_ref, 4, base) * _sig(_hcol(h_ref, 5, base))
    _chunks(tt, b1)
    _build_shifts(ub_ext, sh, range(1, 8), tt + HALO - SUBLANES)

    def b2(base):
        cb = [cbb[...] + jnp.zeros((RC, GW), F32)]

        def tap(off, v):
            cb[0] = cb[0] + cbw[off - 2:off - 1, :] * v
        _for_taps(ub_ext, sh, base, range(2, 2 + KB), tap)
        cbr = cb[0].astype(BF16)
        cb_ref[pl.ds(base, RC), :] = cbr
        xhat, _ = _ln_stats(cbr.astype(F32))
        lnv = xhat * lbg[...] + lbb[...]
        z = _hcol(h_ref, 6, base)
        y_scr[pl.ds(base, RC), GW:2 * GW] = (lnv * _sig(lnv) * (z * _sig(z))).astype(BF16)
    _chunks(tt, b2, unroll=4)
    ub_ext[0:HALO, :] = ub_ext[tt:tt + HALO, :]

    def c1(base):
        cu_ext[pl.ds(16 + base, RC), :] = _hcol(h_ref, 7, base)
    _chunks(tt, c1)
    _build_shifts(cu_ext, sh, range(1, 8), tt + SUBLANES)

    def c2(base):
        ic = _inv_count(cnt, base + t0 == 0)
        for g, w in enumerate(POOL_WINDOWS):
            lanes = slice(g * PGD, (g + 1) * PGD)
            acc = _tap(cu_ext, sh, 16, base, lanes)
            for j in range(1, w):
                acc = acc + _tap(cu_ext, sh, 16 - j, base, lanes)
            p = acc * ic[:, lanes] - _tap(cu_ext, sh, 16, base, lanes)
            p_scr[pl.ds(base, RC), lanes] = p.astype(BF16)
    _chunks(tt, c2, unroll=4)
    cu_ext[0:16, :] = cu_ext[tt:tt + 16, :]
    for g in range(len(POOL_WINDOWS)):
        lanes = slice(g * PGD, (g + 1) * PGD)
        pl_scr[:, lanes] = jnp.dot(p_scr[:, lanes], pw[g], preferred_element_type=F32)

    def c3(base):
        z = _hcol(h_ref, 8, base)
        yc0 = (pl_scr[pl.ds(base, RC), :] + pb[...]) * ps[...]
        y_scr[pl.ds(base, RC), 2 * GW:3 * GW] = (yc0 * (z * _sig(z))).astype(BF16)
    _chunks(tt, c3, unroll=2)


def _fwd_layer(x, win_b, wout_b, prm, ln_g, ln_b, target, *, tt, last, comm=None):
    t_len = x.shape[0]
    n_t = t_len // tt

    def body(*refs):
        refs, comm_refs = _split_comm(refs, comm, n_in, n_out)
        if last:
            (x_ref, win_hbm, wout_hbm, caw, cab, cbw, cbb, lbg, lbb, pw, pb, ps, cnt, lng, lnb, tgt_ref,
             h_ref, xb_ref, cb_ref, p_scr, dz_ref, dln_ref, loss_ref,
             win_v, wout_v, y_scr, o_scr, q_ext, ub_ext, cu_ext, sh, pl_scr, acc2, lacc) = refs
        else:
            (x_ref, win_hbm, wout_hbm, caw, cab, cbw, cbb, lbg, lbb, pw, pb, ps, cnt, lng, lnb,
             h_ref, xb_ref, cb_ref, p_scr, z_ref, xn_ref,
             win_v, wout_v, y_scr, o_scr, q_ext, ub_ext, cu_ext, sh, pl_scr) = refs
        i = pl.program_id(0)

        @pl.when(i == 0)
        def _():
            if comm is not None:
                comm["start"](*comm_refs)
            pltpu.sync_copy(win_hbm, win_v)
            pltpu.sync_copy(wout_hbm, wout_v)
            q_ext[0:SUBLANES, :] = jnp.zeros((SUBLANES, GW), F32)
            ub_ext[0:HALO, :] = jnp.zeros((HALO, GW), F32)
            cu_ext[0:16, :] = jnp.zeros((16, GW), F32)
            if last:
                acc2[...] = jnp.zeros_like(acc2)
                lacc[...] = jnp.zeros_like(lacc)

        xb_ref[...] = x_ref[...].astype(BF16)
        for j in range(NG):
            h_ref[:, j * GW:(j + 1) * GW] = jnp.dot(
                xb_ref[...], win_v[:, j * GW:(j + 1) * GW], preferred_element_type=F32).astype(BF16)

        _fwd_mixers(h_ref, cb_ref, y_scr, q_ext, ub_ext, cu_ext, sh, p_scr, pl_scr,
                    (caw, cab, cbw, cbb, lbg, lbb, pw, pb, ps, cnt), tt, i * tt)

        o_scr[...] = jnp.dot(y_scr[...], wout_v[...], preferred_element_type=F32)

        def post(base):
            rows = pl.ds(base, RC)
            z = ALPHA * x_ref[rows, :] + o_scr[rows, :]
            xhat, rstd = _ln_stats(z)
            xn = xhat * lng[...] + lnb[...]
            if last:
                err = xn - tgt_ref[rows, :]
                lacc[...] += _fold8(err * err)
                dxn = err * (1.0 / D_MODEL)
                acc2[0] += _fold8(dxn * xhat)
                acc2[1] += _fold8(dxn)
                dz_ref[rows, :] = _ln_bwd(dxn, xhat, rstd, lng[...])
            else:
                z_ref[rows, :] = z
                xn_ref[rows, :] = xn
        _chunks(tt, post, unroll=8)

        if last:
            @pl.when(i == n_t - 1)
            def _():
                dln_ref[...] = jnp.sum(acc2[...], axis=1)
                loss_ref[...] = jnp.zeros((SUBLANES, 128), F32) + (0.5 / D_MODEL) * jnp.sum(lacc[...])
        if comm is not None:
            @pl.when(i == n_t - 1)
            def _():
                comm["wait"](*comm_refs)

    tile = lambda c: pl.BlockSpec((tt, c), lambda i: (i, 0))
    full = lambda a: pl.BlockSpec(a.shape, lambda i: (0,) * a.ndim)
    hbm = pl.BlockSpec(memory_space=pl.ANY)
    ins = [x, win_b, wout_b, *prm, ln_g, ln_b] + ([target] if last else [])
    in_specs = [tile(D_MODEL), hbm, hbm] + [full(a) for a in (*prm, ln_g, ln_b)] + ([tile(D_MODEL)] if last else [])
    out_shape = [jax.ShapeDtypeStruct((t_len, D_IN), BF16), jax.ShapeDtypeStruct((t_len, D_MODEL), BF16),
                 jax.ShapeDtypeStruct((t_len, GW), BF16), jax.ShapeDtypeStruct((t_len, GW), BF16)]
    out_specs = [tile(D_IN), tile(D_MODEL), tile(GW), tile(GW)]
    if last:
        out_shape += [jax.ShapeDtypeStruct((t_len, D_MODEL), F32), jax.ShapeDtypeStruct((2, D_MODEL), F32),
                      jax.ShapeDtypeStruct((SUBLANES, 128), F32)]
        out_specs += [tile(D_MODEL), pl.BlockSpec((2, D_MODEL), lambda i: (0, 0)),
                      pl.BlockSpec((SUBLANES, 128), lambda i: (0, 0))]
    else:
        out_shape += [jax.ShapeDtypeStruct((t_len, D_MODEL), F32), jax.ShapeDtypeStruct((t_len, D_MODEL), F32)]
        out_specs += [tile(D_MODEL), tile(D_MODEL)]
    scratch = [
        pltpu.VMEM((D_MODEL, D_IN), BF16), pltpu.VMEM((D_MIX, D_MODEL), BF16),
        pltpu.VMEM((tt, D_MIX), BF16), pltpu.VMEM((tt, D_MODEL), F32),
        pltpu.VMEM((tt + SUBLANES, GW), F32), pltpu.VMEM((tt + HALO, GW), F32), pltpu.VMEM((tt + 16, GW), F32),
        pltpu.VMEM((SUBLANES, tt + HALO, GW), F32),
        pltpu.VMEM((tt, GW), F32),
    ]
    if last:
        scratch += [pltpu.VMEM((2, SUBLANES, D_MODEL), F32), pltpu.VMEM((SUBLANES, D_MODEL), F32)]
    n_in, n_out = len(ins), len(out_shape)
    ins, in_specs, out_shape, out_specs, scratch = _with_comm(comm, ins, in_specs, out_shape, out_specs, scratch)
    return pl.pallas_call(
        body, name=("fwd_last" if last else "fwd_layer") + ("" if comm is None else "_comm"), grid=(n_t,),
        in_specs=in_specs, out_specs=out_specs, out_shape=out_shape, scratch_shapes=scratch,
        compiler_params=pltpu.CompilerParams(dimension_semantics=("arbitrary",), vmem_limit_bytes=VMEM_LIMIT),
    )(*ins)


def _dsilu(z, sz):
    return sz * (1.0 + z * (1.0 - sz))


def _bwd_layer(dz, h, cb, p, win_b, wout_b, prm, z_prev, lng_prev, *, tt, comm=None):
    t_len = dz.shape[0]
    n_t = t_len // tt
    has_prev = z_prev is not None
    hb = tt // HALO
    group = min(WOUT_GROUP, n_t)
    assert n_t % group == 0

    def body(*refs):
        refs, comm_refs = _split_comm(refs, comm, n_in, n_out)
        dz_ref, h_ref, halo_ref, cb_ref, p_scr, win_hbm, wout_hbm, caw, cab, cbw, cbb, lbg, lbb, pw, pb, ps, cnt = refs[:17]
        k = 17
        if has_prev:
            zp_ref, lngp = refs[k:k + 2]
            k += 2
        dxo_ref, dh_ref, dwout_hbm, small_ref, dpw_ref = refs[k:k + 5]
        k += 5
        if has_prev:
            dlnp_ref = refs[k]
            k += 1
        (win_v, wout_v, dzb_all, dy_scr, y_all, dx_scr, q_ext, dca_ext, dcb_ext, dpn_ext, sh,
         pl_scr, dpl_scr, dp_scr, racc, dpw_acc, dwout_acc) = refs[k:k + 17]
        k += 17
        if has_prev:
            acc2 = refs[k]
        i = pl.program_id(0)
        ti = n_t - 1 - i
        t0 = ti * tt
        slot = i % group
        slot_rows = pl.ds(pl.multiple_of(slot * tt, tt), tt)
        dzb, y_scr = dzb_all.at[slot_rows], y_all.at[slot_rows]

        @pl.when(i == 0)
        def _():
            if comm is not None:
                comm["start"](*comm_refs)
            pltpu.sync_copy(win_hbm, win_v)
            pltpu.sync_copy(wout_hbm, wout_v)
            dca_ext[tt:tt + SUBLANES, :] = jnp.zeros((SUBLANES, GW), F32)
            dcb_ext[tt:tt + HALO, :] = jnp.zeros((HALO, GW), F32)
            dpn_ext[tt:tt + 16, :] = jnp.zeros((16, GW), F32)
            racc[...] = jnp.zeros_like(racc)
            dpw_acc[...] = jnp.zeros_like(dpw_acc)
            dwout_acc[...] = jnp.zeros_like(dwout_acc)
            if has_prev:
                acc2[...] = jnp.zeros_like(acc2)

        dzb[...] = dz_ref[...].astype(BF16)
        dy_scr[...] = lax.dot_general(dzb[...], wout_v[...], (((1,), (1,)), ((), ())), preferred_element_type=F32)

        live = (ti > 0).astype(F32)
        hh = lambda j, r0, r1: halo_ref[r0:r1, j * GW:(j + 1) * GW].astype(F32)
        q_ext[0:SUBLANES, :] = live * hh(1, 24, 32) * hh(2, 24, 32)

        def a1(base):
            q_ext[pl.ds(SUBLANES + base, RC), :] = _hcol(h_ref, 1, base) * _hcol(h_ref, 2, base)
        _chunks(tt, a1)
        _build_shifts(q_ext, sh, (6, 7), tt)

        def a2(base):
            rows = pl.ds(base, RC)
            q6, q7, q8 = _tap(q_ext, sh, 6, base), _tap(q_ext, sh, 7, base), _tap(q_ext, sh, 8, base)
            ca = cab[...] + caw[0:1, :] * q6 + caw[1:2, :] * q7 + caw[2:3, :] * q8
            bg, z = _hcol(h_ref, 0, base), _hcol(h_ref, 3, base)
            sz = _sig(z)
            sza = z * sz
            dya = dy_scr[rows, 0:GW]
            ya0 = bg * ca
            y_scr[rows, 0:GW] = (ya0 * sza).astype(BF16)
            dya0 = dya * sza
            dh_ref[rows, 3 * GW:4 * GW] = (dya * ya0 * _dsilu(z, sz)).astype(BF16)
            dh_ref[rows, 0:GW] = (dya0 * ca).astype(BF16)
            dca = dya0 * bg
            dca_ext[rows, :] = dca
            racc[R_DCAB] += _fold8(dca)
            racc[R_DWA + 0] += _fold8(dca * q6)
            racc[R_DWA + 1] += _fold8(dca * q7)
            racc[R_DWA + 2] += _fold8(dca * q8)
        _chunks(tt, a2)
        _build_shifts(dca_ext, sh, (1, 2), tt)

        def a3(base):
            rows = pl.ds(base, RC)
            dq = caw[0:1, :] * _tap(dca_ext, sh, 2, base) + caw[1:2, :] * _tap(dca_ext, sh, 1, base) \
                + caw[2:3, :] * _tap(dca_ext, sh, 0, base)
            dh_ref[rows, GW:2 * GW] = (dq * _hcol(h_ref, 2, base)).astype(BF16)
            dh_ref[rows, 2 * GW:3 * GW] = (dq * _hcol(h_ref, 1, base)).astype(BF16)
        _chunks(tt, a3)
        dca_ext[tt:tt + SUBLANES, :] = dca_ext[0:SUBLANES, :]

        def b2(base):
            rows = pl.ds(base, RC)
            xhat, rstd = _ln_stats(cb_ref[rows, :].astype(F32))
            lnv = xhat * lbg[...] + lbb[...]
            sl = _sig(lnv)
            s = lnv * sl
            z = _hcol(h_ref, 6, base)
            sz = _sig(z)
            szb = z * sz
            y_scr[rows, GW:2 * GW] = (s * szb).astype(BF16)
            dyb = dy_scr[rows, GW:2 * GW]
            dh_ref[rows, 6 * GW:7 * GW] = (dyb * s * _dsilu(z, sz)).astype(BF16)
            dlnv = dyb * szb * _dsilu(lnv, sl)
            racc[R_DLBG] += _fold8(dlnv * xhat)
            racc[R_DLBB] += _fold8(dlnv)
            dcb = _ln_bwd(dlnv, xhat, rstd, lbg[...])
            dcb_ext[rows, :] = dcb
            racc[R_DCBB] += _fold8(dcb)
        _chunks(tt, b2, unroll=4)
        _build_shifts(dcb_ext, sh, range(1, 8), tt + HALO - SUBLANES)

        def b3(base):
            rows = pl.ds(base, RC)
            v, gt = _hcol(h_ref, 4, base), _hcol(h_ref, 5, base)
            sg = _sig(gt)
            ub = v * sg
            dubv = [jnp.zeros((RC, GW), F32)]

            def tap(off, win):
                dubv[0] = dubv[0] + cbw[KB - 1 - off:KB - off, :] * win
                racc[R_DWB + KB - 1 - off] += _fold8(ub * win)
            _for_taps(dcb_ext, sh, base, range(KB), tap)
            dub = dubv[0]
            dh_ref[rows, 4 * GW:5 * GW] = (dub * sg).astype(BF16)
            dh_ref[rows, 5 * GW:6 * GW] = (dub * v * sg * (1.0 - sg)).astype(BF16)
        _chunks(tt, b3, unroll=4)
        dcb_ext[tt:tt + HALO, :] = dcb_ext[0:HALO, :]

        for g in range(len(POOL_WINDOWS)):
            lanes = slice(g * PGD, (g + 1) * PGD)
            pl_scr[:, lanes] = jnp.dot(p_scr[:, lanes], pw[g], preferred_element_type=F32)

        def c3(base):
            rows = pl.ds(base, RC)
            z = _hcol(h_ref, 8, base)
            sz = _sig(z)
            szc = z * sz
            plb = pl_scr[rows, :] + pb[...]
            yc0 = plb * ps[...]
            y_scr[rows, 2 * GW:3 * GW] = (yc0 * szc).astype(BF16)
            dyc = dy_scr[rows, 2 * GW:3 * GW]
            dh_ref[rows, 8 * GW:9 * GW] = (dyc * yc0 * _dsilu(z, sz)).astype(BF16)
            dyc0 = dyc * szc
            racc[R_DPS] += _fold8(dyc0 * plb)
            dpl = dyc0 * ps[...]
            racc[R_DPB] += _fold8(dpl)
            dpl_scr[rows, :] = dpl.astype(BF16)
        _chunks(tt, c3, unroll=4)
        for g in range(len(POOL_WINDOWS)):
            lanes = slice(g * PGD, (g + 1) * PGD)
            dpw_acc[g] += lax.dot_general(p_scr[:, lanes], dpl_scr[:, lanes], (((0,), (0,)), ((), ())),
                                          preferred_element_type=F32)
            dp_scr[:, lanes] = lax.dot_general(dpl_scr[:, lanes], pw[g], (((1,), (1,)), ((), ())),
                                               preferred_element_type=F32)

        def c4(base):
            rows = pl.ds(base, RC)
            dpn_ext[rows, :] = dp_scr[rows, :] * _inv_count(cnt, base + t0 == 0)
        _chunks(tt, c4)
        _build_shifts(dpn_ext, sh, range(1, 8), tt + SUBLANES)

        def c5(base):
            rows = pl.ds(base, RC)
            for g, w in enumerate(POOL_WINDOWS):
                lanes = slice(g * PGD, (g + 1) * PGD)
                acc = _tap(dpn_ext, sh, 0, base, lanes)
                for j in range(1, w):
                    acc = acc + _tap(dpn_ext, sh, j, base, lanes)
                dh_ref[rows, 7 * GW + g * PGD:7 * GW + (g + 1) * PGD] = (acc - dp_scr[rows, lanes]).astype(BF16)
        _chunks(tt, c5, unroll=4)
        dpn_ext[tt:tt + 16, :] = dpn_ext[0:16, :]

        @pl.when(slot == group - 1)
        def _():
            for r in range(D_MIX // GW):
                dwout_acc[r * GW:(r + 1) * GW, :] += lax.dot_general(
                    y_all[:, r * GW:(r + 1) * GW], dzb_all[...], (((0,), (0,)), ((), ())), preferred_element_type=F32)
        dx_scr[...] = lax.dot_general(dh_ref[...], win_v[...], (((1,), (1,)), ((), ())), preferred_element_type=F32)

        def post(base):
            rows = pl.ds(base, RC)
            dx = ALPHA * dz_ref[rows, :] + dx_scr[rows, :]
            if has_prev:
                xhat, rstd = _ln_stats(zp_ref[rows, :])
                acc2[0] += _fold8(dx * xhat)
                acc2[1] += _fold8(dx)
                dxo_ref[rows, :] = _ln_bwd(dx, xhat, rstd, lngp[...])
            else:
                dxo_ref[rows, :] = dx
        _chunks(tt, post, unroll=8)

        @pl.when(i == n_t - 1)
        def _():
            small_ref[...] = jnp.sum(racc[...], axis=1)
            dpw_ref[...] = dpw_acc[...]
            pltpu.sync_copy(dwout_acc, dwout_hbm)
            if has_prev:
                dlnp_ref[...] = jnp.sum(acc2[...], axis=1)
            if comm is not None:
                comm["wait"](*comm_refs)

    rtile = lambda c: pl.BlockSpec((tt, c), lambda i: (n_t - 1 - i, 0))
    full = lambda a: pl.BlockSpec(a.shape, lambda i: (0,) * a.ndim)
    const = lambda shp: pl.BlockSpec(shp, lambda i: (0,) * len(shp))
    hbm = pl.BlockSpec(memory_space=pl.ANY)
    halo_spec = pl.BlockSpec((HALO, D_IN), lambda i: (jnp.maximum((n_t - 1 - i) * hb - 1, 0), 0))
    ins = [dz, h, h, cb, p, win_b, wout_b, *prm] + ([z_prev, lng_prev] if has_prev else [])
    in_specs = [rtile(D_MODEL), rtile(D_IN), halo_spec, rtile(GW), rtile(GW), hbm, hbm] + [full(a) for a in prm] \
        + ([rtile(D_MODEL), full(lng_prev)] if has_prev else [])
    out_shape = [jax.ShapeDtypeStruct((t_len, D_MODEL), F32), jax.ShapeDtypeStruct((t_len, D_IN), BF16),
                 jax.ShapeDtypeStruct((D_MIX, D_MODEL), F32), jax.ShapeDtypeStruct((N_RACC, GW), F32),
                 jax.ShapeDtypeStruct((len(POOL_WINDOWS), PGD, PGD), F32)]
    out_specs = [rtile(D_MODEL), rtile(D_IN), hbm, const((N_RACC, GW)), const((len(POOL_WINDOWS), PGD, PGD))]
    if has_prev:
        out_shape.append(jax.ShapeDtypeStruct((2, D_MODEL), F32))
        out_specs.append(const((2, D_MODEL)))
    scratch = [
        pltpu.VMEM((D_MODEL, D_IN), BF16), pltpu.VMEM((D_MIX, D_MODEL), BF16),
        pltpu.VMEM((group * tt, D_MODEL), BF16), pltpu.VMEM((tt, D_MIX), F32), pltpu.VMEM((group * tt, D_MIX), BF16),
        pltpu.VMEM((tt, D_MODEL), F32),
        pltpu.VMEM((tt + SUBLANES, GW), F32),
        pltpu.VMEM((tt + SUBLANES, GW), F32), pltpu.VMEM((tt + HALO, GW), F32), pltpu.VMEM((tt + 16, GW), F32),
        pltpu.VMEM((SUBLANES, tt + HALO, GW), F32),
        pltpu.VMEM((tt, GW), F32), pltpu.VMEM((tt, GW), BF16), pltpu.VMEM((tt, GW), F32),
        pltpu.VMEM((N_RACC, SUBLANES, GW), F32), pltpu.VMEM((len(POOL_WINDOWS), PGD, PGD), F32),
        pltpu.VMEM((D_MIX, D_MODEL), F32),
    ]
    if has_prev:
        scratch.append(pltpu.VMEM((2, SUBLANES, D_MODEL), F32))
    n_in, n_out = len(ins), len(out_shape)
    ins, in_specs, out_shape, out_specs, scratch = _with_comm(comm, ins, in_specs, out_shape, out_specs, scratch)
    return pl.pallas_call(
        body, name=("bwd_layer_prev" if has_prev else "bwd_layer") + ("" if comm is None else "_comm"), grid=(n_t,),
        in_specs=in_specs, out_specs=out_specs, out_shape=out_shape, scratch_shapes=scratch,
        compiler_params=pltpu.CompilerParams(dimension_semantics=("arbitrary",), vmem_limit_bytes=VMEM_LIMIT),
    )(*ins)


def _wgrad_in(xb, dh, *, tk, comm=None):
    t_len = xb.shape[0]
    tk = min(tk, t_len)
    n_k = t_len // tk

    def body(*refs):
        (x_ref, dh_ref, o_ref), comm_refs = _split_comm(refs, comm, 2, 1)
        j, k = pl.program_id(0), pl.program_id(1)

        @pl.when(k == 0)
        def _():
            o_ref[...] = jnp.zeros_like(o_ref)
        if comm is not None:
            @pl.when((j == 0) & (k == 0))
            def _():
                comm["start"](*comm_refs)
        o_ref[0] += lax.dot_general(x_ref[...], dh_ref[...], (((0,), (0,)), ((), ())), preferred_element_type=F32)
        if comm is not None:
            @pl.when((j == N_CHIP - 1) & (k == n_k - 1))
            def _():
                comm["wait"](*comm_refs)

    ins = [xb, dh]
    in_specs = [pl.BlockSpec((tk, D_MODEL), lambda j, k: (k, 0)), pl.BlockSpec((tk, SHARD_IN), lambda j, k: (k, j))]
    out_shape = [jax.ShapeDtypeStruct((N_CHIP, D_MODEL, SHARD_IN), F32)]
    out_specs = [pl.BlockSpec((1, D_MODEL, SHARD_IN), lambda j, k: (j, 0, 0))]
    ins, in_specs, out_shape, out_specs, scratch = _with_comm(comm, ins, in_specs, out_shape, out_specs, [])
    outs = pl.pallas_call(
        body, name="wgrad_in" + ("" if comm is None else "_comm"), grid=(N_CHIP, n_k),
        in_specs=in_specs, out_specs=out_specs, out_shape=out_shape, scratch_shapes=scratch,
        compiler_params=pltpu.CompilerParams(dimension_semantics=("arbitrary", "arbitrary"), vmem_limit_bytes=VMEM_LIMIT),
    )(*ins)
    return outs[0] if comm is None else outs


MESH = pl.DeviceIdType.MESH
ANY = pl.BlockSpec(memory_space=pl.ANY)


def _place():
    x, y, c = lax.axis_index("x"), lax.axis_index("y"), lax.axis_index("c")
    others = [(1 - x, y), (x, 1 - y), (1 - x, 1 - y)]
    return x, y, c, 2 * x + y, [(ox, oy, 2 * ox + oy) for ox, oy in others]


def _rcopy(src, dst, send_sems, recv_sems, k, dev):
    return pltpu.make_async_remote_copy(src_ref=src, dst_ref=dst, send_sem=send_sems.at[k], recv_sem=recv_sems.at[k],
                                        device_id=dev, device_id_type=MESH)


def _gather_weights(w_in, w_out, cw):
    hi, ho = D_MODEL // 2, SHARD_OUT // 2

    def body(win_ref, wout_ref, cw_ref, owin, owout, ocw, bin_v, bout_v, send_sems, recv_sems, lsem):
        x, y, c, me, others = _place()
        for l in range(DEPTH):
            for r0 in range(0, D_MODEL, 256):
                bin_v[l, r0:r0 + 256, :] = win_ref[l, r0:r0 + 256, :].astype(BF16)
            bout_v[l] = wout_ref[l].astype(BF16)
        cin = pl.ds(pl.multiple_of(me * SHARD_IN, 128), SHARD_IN)
        rout = pl.ds(pl.multiple_of(me * SHARD_OUT, 128), SHARD_OUT)
        local = [pltpu.make_async_copy(bin_v.at[0], owin.at[:, cin], lsem.at[0]),
                 pltpu.make_async_copy(bout_v.at[0], owout.at[rout, :], lsem.at[1]),
                 pltpu.make_async_copy(cw_ref, ocw.at[me], lsem.at[2])]
        for cp in local:
            cp.start()

        def in_half(chip, core):
            return owin.at[pl.ds(pl.multiple_of(core * hi, 256), hi), pl.ds(pl.multiple_of(chip * SHARD_IN, 128), SHARD_IN)]

        def out_half(chip, core):
            return owout.at[pl.ds(pl.multiple_of(chip * SHARD_OUT + core * ho, 64), ho), :]

        first = []
        for k, (ox, oy, _) in enumerate(others):
            dev = (ox, oy, c)
            first.append(_rcopy(bin_v.at[0, pl.ds(pl.multiple_of(c * hi, 256), hi), :], in_half(me, c), send_sems, recv_sems, k, dev))
            first.append(_rcopy(bout_v.at[0, pl.ds(pl.multiple_of(c * ho, 64), ho), :], out_half(me, c), send_sems, recv_sems, 3 + k, dev))
            first.append(_rcopy(cw_ref, ocw.at[me], send_sems, recv_sems, 6 + k, dev))
        for cp in first:
            cp.start()
        sib = (x, y, 1 - c)
        passed = []
        for k, (ox, oy, oc) in enumerate(others):
            _rcopy(in_half(oc, c), in_half(oc, c), send_sems, recv_sems, k, sib).wait_recv()
            fwd_in = _rcopy(in_half(oc, c), in_half(oc, c), send_sems, recv_sems, 9 + k, sib)
            fwd_in.start()
            _rcopy(out_half(oc, c), out_half(oc, c), send_sems, recv_sems, 3 + k, sib).wait_recv()
            fwd_out = _rcopy(out_half(oc, c), out_half(oc, c), send_sems, recv_sems, 12 + k, sib)
            fwd_out.start()
            passed += [fwd_in, fwd_out]
        for k, (ox, oy, oc) in enumerate(others):
            _rcopy(cw_ref, ocw.at[oc], send_sems, recv_sems, 6 + k, sib).wait_recv()
            _rcopy(in_half(oc, 1 - c), in_half(oc, 1 - c), send_sems, recv_sems, 9 + k, sib).wait_recv()
            _rcopy(out_half(oc, 1 - c), out_half(oc, 1 - c), send_sems, recv_sems, 12 + k, sib).wait_recv()
        for cp in first + passed:
            cp.wait_send()
        for cp in local:
            cp.wait()

    vm = pl.BlockSpec(memory_space=pltpu.VMEM)
    return pl.pallas_call(
        body, name="gather_weights",
        in_specs=[vm, vm, vm], out_specs=[ANY, ANY, ANY, vm, vm],
        out_shape=[jax.ShapeDtypeStruct((D_MODEL, D_IN), BF16), jax.ShapeDtypeStruct((D_MIX, D_MODEL), BF16),
                   jax.ShapeDtypeStruct((N_CHIP,) + cw.shape, F32),
                   jax.ShapeDtypeStruct((DEPTH, D_MODEL, SHARD_IN), BF16), jax.ShapeDtypeStruct((DEPTH, SHARD_OUT, D_MODEL), BF16)],
        scratch_shapes=[pltpu.SemaphoreType.DMA((15,)), pltpu.SemaphoreType.DMA((15,)), pltpu.SemaphoreType.DMA((3,))],
        compiler_params=pltpu.CompilerParams(vmem_limit_bytes=VMEM_LIMIT),
    )(w_in, w_out, cw)


def _gather_starts(bsh_in, bsh_out, owin, owout, send_sems, recv_sems, lsem, layer):
    x, y, c, me, others = _place()
    hi, ho = D_MODEL // 2, SHARD_OUT // 2
    pltpu.make_async_copy(bsh_in.at[layer], owin.at[:, pl.ds(pl.multiple_of(me * SHARD_IN, 128), SHARD_IN)], lsem.at[0]).start()
    pltpu.make_async_copy(bsh_out.at[layer], owout.at[pl.ds(pl.multiple_of(me * SHARD_OUT, 128), SHARD_OUT), :], lsem.at[1]).start()
    for k, (ox, oy, _) in enumerate(others):
        for t in range(2):
            pltpu.make_async_remote_copy(
                src_ref=bsh_in.at[layer, pl.ds(pl.multiple_of(c * hi, 256), hi), :],
                dst_ref=owin.at[pl.ds(pl.multiple_of(c * hi, 256), hi), pl.ds(pl.multiple_of(me * SHARD_IN, 128), SHARD_IN)],
                send_sem=send_sems.at[2 * k + t], recv_sem=recv_sems.at[2 * k + c], device_id=(ox, oy, t), device_id_type=MESH).start()
            pltpu.make_async_remote_copy(
                src_ref=bsh_out.at[layer, pl.ds(pl.multiple_of(c * ho, 64), ho), :],
                dst_ref=owout.at[pl.ds(pl.multiple_of(me * SHARD_OUT + c * ho, 64), ho), :],
                send_sem=send_sems.at[6 + 2 * k + t], recv_sem=recv_sems.at[6 + 2 * k + c], device_id=(ox, oy, t), device_id_type=MESH).start()


def _gather_waits(bsh_in, bsh_out, owin, owout, send_sems, recv_sems, lsem, layer):
    x, y, c, me, others = _place()
    hi, ho = D_MODEL // 2, SHARD_OUT // 2
    src_in = bsh_in.at[layer, pl.ds(0, hi), :]
    src_out = bsh_out.at[layer, pl.ds(0, ho), :]
    for k, (ox, oy, oc) in enumerate(others):
        for t in range(2):
            dst_in = owin.at[pl.ds(t * hi, hi), pl.ds(pl.multiple_of(oc * SHARD_IN, 128), SHARD_IN)]
            dst_out = owout.at[pl.ds(pl.multiple_of(oc * SHARD_OUT + t * ho, 64), ho), :]
            a = pltpu.make_async_remote_copy(src_ref=src_in, dst_ref=dst_in, send_sem=send_sems.at[2 * k + t],
                                             recv_sem=recv_sems.at[2 * k + t], device_id=(ox, oy, t), device_id_type=MESH)
            b = pltpu.make_async_remote_copy(src_ref=src_out, dst_ref=dst_out, send_sem=send_sems.at[6 + 2 * k + t],
                                             recv_sem=recv_sems.at[6 + 2 * k + t], device_id=(ox, oy, t), device_id_type=MESH)
            a.wait_send()
            a.wait_recv()
            b.wait_send()
            b.wait_recv()
    pltpu.make_async_copy(bsh_in.at[layer], owin.at[:, pl.ds(pl.multiple_of(me * SHARD_IN, 128), SHARD_IN)], lsem.at[0]).wait()
    pltpu.make_async_copy(bsh_out.at[layer], owout.at[pl.ds(pl.multiple_of(me * SHARD_OUT, 128), SHARD_OUT), :], lsem.at[1]).wait()


def _gather_comm(bsh_in, bsh_out, layer):
    return dict(ins=[bsh_in, bsh_out],
                out_shape=[jax.ShapeDtypeStruct((D_MODEL, D_IN), BF16), jax.ShapeDtypeStruct((D_MIX, D_MODEL), BF16)],
                sems=[pltpu.SemaphoreType.DMA((12,)), pltpu.SemaphoreType.DMA((12,)), pltpu.SemaphoreType.DMA((2,))],
                start=lambda ins, outs, sems: _gather_starts(ins[0], ins[1], outs[0], outs[1], *sems, layer),
                wait=lambda ins, outs, sems: _gather_waits(ins[0], ins[1], outs[0], outs[1], *sems, layer))


def _exchange_halves(arrs, tag):
    n = len(arrs)

    def body(*refs):
        ins, outs, (send_sems, recv_sems) = refs[:n], refs[n:2 * n], refs[2 * n:]
        x, y, c, _, _ = _place()
        cps = []
        for m in range(n):
            half = ins[m].shape[1] // 2
            cps.append(_rcopy(ins[m].at[:, pl.ds(pl.multiple_of((1 - c) * half, SUBLANES), half), :], outs[m],
                              send_sems, recv_sems, m, (x, y, 1 - c)))
        for cp in cps:
            cp.start()
        for cp in cps:
            cp.wait()

    return pl.pallas_call(
        body, name="exchange_halves_" + tag, in_specs=[ANY] * n, out_specs=[ANY] * n,
        out_shape=[jax.ShapeDtypeStruct((a.shape[0], a.shape[1] // 2, a.shape[2]), F32) for a in arrs],
        scratch_shapes=[pltpu.SemaphoreType.DMA((n,)), pltpu.SemaphoreType.DMA((n,))],
    )(*arrs)


def _add_own_half(a, got, core, *, rb, dtype):
    nj, r, cdim = a.shape
    half = r // 2

    def body(core_ref, a_ref, g_ref, o_ref):
        o_ref[...] = (a_ref[0] + g_ref[...]).astype(dtype)

    return pl.pallas_call(
        body, name="add_own_half",
        grid_spec=pltpu.PrefetchScalarGridSpec(
            num_scalar_prefetch=1, grid=(nj, half // rb),
            in_specs=[pl.BlockSpec((1, 1, rb, cdim), lambda j, i, cr: (j, cr[0], i, 0)),
                      pl.BlockSpec((1, rb, cdim), lambda j, i, cr: (j, i, 0))],
            out_specs=pl.BlockSpec((1, rb, cdim), lambda j, i, cr: (j, i, 0))),
        out_shape=jax.ShapeDtypeStruct((nj, half, cdim), dtype),
    )(core, a.reshape(nj, 2, half, cdim), got)


def _owner_starts(ins, outs, sems):
    send_sems, recv_sems, lsem = sems
    x, y, c, me, others = _place()
    for m in range(len(ins)):
        pltpu.make_async_copy(ins[m].at[me], outs[m].at[me], lsem.at[m]).start()
        for k, (ox, oy, oc) in enumerate(others):
            _rcopy(ins[m].at[oc], outs[m].at[me], send_sems, recv_sems, 3 * m + k, (ox, oy, c)).start()


def _owner_waits(ins, outs, sems):
    send_sems, recv_sems, lsem = sems
    x, y, c, me, others = _place()
    for m in range(len(ins)):
        for k, (ox, oy, oc) in enumerate(others):
            _rcopy(ins[m].at[oc], outs[m].at[oc], send_sems, recv_sems, 3 * m + k, (ox, oy, c)).wait()
        pltpu.make_async_copy(ins[m].at[me], outs[m].at[me], lsem.at[m]).wait()


def _owner_comm(arrs):
    n = len(arrs)
    return dict(ins=arrs, out_shape=[jax.ShapeDtypeStruct(a.shape, a.dtype) for a in arrs],
                sems=[pltpu.SemaphoreType.DMA((3 * n,)), pltpu.SemaphoreType.DMA((3 * n,)), pltpu.SemaphoreType.DMA((n,))],
                start=_owner_starts, wait=_owner_waits)


def _send_to_owners(arrs):
    n = len(arrs)

    def body(*refs):
        ins, outs, sems = refs[:n], refs[n:2 * n], refs[2 * n:]
        _owner_starts(ins, outs, sems)
        _owner_waits(ins, outs, sems)

    job = _owner_comm(arrs)
    return pl.pallas_call(
        body, name="send_to_owners", in_specs=[ANY] * n, out_specs=[ANY] * n,
        out_shape=job["out_shape"], scratch_shapes=job["sems"],
    )(*arrs)


def _sum_chips(a, *, rb):
    nj, r, cdim = a.shape

    def body(a_ref, o_ref):
        f = lambda k: a_ref[k].astype(F32)
        o_ref[...] = ((f(0) + f(1)) + f(2)) + f(3)

    return pl.pallas_call(
        body, name="sum_chips", grid=(r // rb,),
        in_specs=[pl.BlockSpec((nj, rb, cdim), lambda i: (0, i, 0))],
        out_specs=pl.BlockSpec((rb, cdim), lambda i: (i, 0)),
        out_shape=jax.ShapeDtypeStruct((r, cdim), F32),
    )(a)


def _sum_chips_into(a, dest, layer, core, *, rb):
    nj, half, cdim = a.shape
    nb = half // rb

    def body(*refs):
        a_ref, o_ref = refs[1], refs[-1]
        f = lambda k: a_ref[k].astype(F32)
        o_ref[0] = ((f(0) + f(1)) + f(2)) + f(3)

    grid_spec = pltpu.PrefetchScalarGridSpec(
        num_scalar_prefetch=1, grid=(nb,),
        in_specs=[pl.BlockSpec((nj, rb, cdim), lambda i, cr: (0, i, 0))] + ([] if dest is None else [ANY]),
        out_specs=pl.BlockSpec((1, rb, cdim), lambda i, cr: (layer, cr[0] * nb + i, 0)))
    return pl.pallas_call(
        body, name="sum_chips_into", grid_spec=grid_spec,
        out_shape=jax.ShapeDtypeStruct((DEPTH, 2 * half, cdim), F32),
        input_output_aliases={} if dest is None else {2: 0},
    )(*([core, a] if dest is None else [core, a, dest]))


def _spread_reduced(g_in, g_out, red_small):
    hs = red_small.shape[0]

    def body(gin_in, gout_in, sm, gin, gout, fsm, gsm, send_sems, recv_sems, lsem):
        x, y, c, me, others = _place()
        sib = (x, y, 1 - c)
        hi, ho = D_MODEL // 2, SHARD_OUT // 2
        ri, ro = pl.ds(pl.multiple_of(c * hi, SUBLANES), hi), pl.ds(pl.multiple_of(c * ho, SUBLANES), ho)
        remote = [_rcopy(gin.at[:, ri, :], gin.at[:, ri, :], send_sems, recv_sems, 0, sib),
                  _rcopy(gout.at[:, ro, :], gout.at[:, ro, :], send_sems, recv_sems, 1, sib)]
        own_small = pltpu.make_async_copy(sm, gsm.at[me], lsem.at[0])
        small = [_rcopy(sm, gsm.at[me], send_sems, recv_sems, 2 + k, (ox, oy, c)) for k, (ox, oy, _) in enumerate(others)]
        for cp in remote + [own_small] + small:
            cp.start()
        own_small.wait()
        for cp in small:
            cp.wait()
        mine = fsm.at[:, pl.ds(pl.multiple_of(c * hs, SUBLANES), hs), :]
        keep = pltpu.make_async_copy(gsm, mine, lsem.at[1])
        give = _rcopy(gsm, mine, send_sems, recv_sems, 5, sib)
        keep.start()
        give.start()
        for cp in remote + [give]:
            cp.wait()
        keep.wait()

    return pl.pallas_call(
        body, name="spread_reduced", in_specs=[ANY] * 3, out_specs=[ANY] * 4,
        out_shape=[jax.ShapeDtypeStruct(g_in.shape, F32), jax.ShapeDtypeStruct(g_out.shape, F32),
                   jax.ShapeDtypeStruct((N_CHIP, 2 * hs, GW), F32), jax.ShapeDtypeStruct((N_CHIP, hs, GW), F32)],
        input_output_aliases={0: 0, 1: 1},
        scratch_shapes=[pltpu.SemaphoreType.DMA((6,)), pltpu.SemaphoreType.DMA((6,)), pltpu.SemaphoreType.DMA((2,))],
    )(g_in, g_out, red_small)[:3]


def _adamw_math(w, g, m, v):
    m = ADAM_B1 * m + (1.0 - ADAM_B1) * g
    v = ADAM_B2 * v + (1.0 - ADAM_B2) * (g * g)
    m_hat = m / (1.0 - ADAM_B1 ** ADAM_STEP)
    v_hat = v / (1.0 - ADAM_B2 ** ADAM_STEP)
    delta = -ADAM_LR * (m_hat / (jnp.sqrt(v_hat) + ADAM_EPS) + ADAM_WD * w)
    return delta, m, v


def _adamw_big(w, g, m, v, *, rb):
    r, cdim = w.shape

    def body(w_ref, g_ref, m_ref, v_ref, d_ref, nm_ref, nv_ref):
        d_ref[...], nm_ref[...], nv_ref[...] = _adamw_math(w_ref[...], g_ref[...], m_ref[...], v_ref[...])

    spec = pl.BlockSpec((rb, cdim), lambda i: (i, 0))
    return pl.pallas_call(
        body, name="adamw_big", grid=(r // rb,), in_specs=[spec] * 4, out_specs=[spec] * 3,
        out_shape=[jax.ShapeDtypeStruct((r, cdim), F32)] * 3,
    )(w, g, m, v)


def _adamw_small(ws, gs, ms, vs):
    n = len(ws)

    def body(*refs):
        w, g, m, v = refs[:n], refs[n:2 * n], refs[2 * n:3 * n], refs[3 * n:4 * n]
        d, nm, nv = refs[4 * n:5 * n], refs[5 * n:6 * n], refs[6 * n:7 * n]
        for k in range(n):
            d[k][...], nm[k][...], nv[k][...] = _adamw_math(w[k][...], g[k][...], m[k][...], v[k][...])

    shapes = [jax.ShapeDtypeStruct(a.shape, F32) for a in ws]
    outs = pl.pallas_call(body, name="adamw_small", out_shape=shapes * 3)(*ws, *gs, *ms, *vs)
    return outs[:n], outs[n:2 * n], outs[2 * n:]


TT = 256
TK = 2048
CW_ROWS = 40
PACK_ROWS = 192


def _pack(rows):
    packed = jnp.concatenate(rows, axis=0)
    packed = jnp.pad(packed, ((0, PACK_ROWS - packed.shape[0]), (0, 0)))
    return packed.reshape(N_CHIP, PACK_ROWS // N_CHIP, GW)


def _reduce_to_owner_halves(parts, core1, tag):
    got = _exchange_halves(parts, tag)
    rbs = {D_MODEL: 256, SHARD_OUT: SHARD_OUT // 2, PACK_ROWS // N_CHIP: PACK_ROWS // N_CHIP // 2}
    return [_add_own_half(a, g, core1, rb=rbs[a.shape[1]], dtype=F32 if a.shape[1] == PACK_ROWS // N_CHIP else BF16)
            for a, g in zip(parts, got)]


def kernel(x, w_in, conv_a_w, conv_a_b, conv_b_w, conv_b_b, ln_b_g, ln_b_b, pool_w, pool_b, pool_scale, w_out, ln_g, ln_b, loss_target, m_w_in, m_conv_a_w, m_conv_a_b, m_conv_b_w, m_conv_b_b, m_ln_b_g, m_ln_b_b, m_pool_w, m_pool_b, m_pool_scale, m_w_out, m_ln_g, m_ln_b, v_w_in, v_conv_a_w, v_conv_a_b, v_conv_b_w, v_conv_b_b, v_ln_b_g, v_ln_b_b, v_pool_w, v_pool_b, v_pool_scale, v_w_out, v_ln_g, v_ln_b):
    chip = 2 * lax.axis_index("x") + lax.axis_index("y")
    core1 = lax.axis_index("c").reshape(1).astype(jnp.int32)
    x2, tgt = x[0], loss_target[0]

    cw = jnp.zeros((DEPTH, CW_ROWS, PGD), F32).at[:, 0:KA].set(conv_a_w).at[:, 8:8 + KB].set(conv_b_w)
    win0_b, wout0_b, cw_all, bsh_in, bsh_out = _gather_weights(w_in, w_out, cw)
    cw_full = jnp.transpose(cw_all, (1, 2, 0, 3)).reshape(DEPTH, CW_ROWS, GW)
    row = lambda a, l: a[l].reshape(1, -1)
    cnt = _count_table()
    prm = [(cw_full[l, 0:KA], row(conv_a_b, l), cw_full[l, 8:8 + KB], row(conv_b_b, l), row(ln_b_g, l), row(ln_b_b, l),
            pool_w[l].astype(BF16), row(pool_b, l), row(pool_scale, l), cnt) for l in range(DEPTH)]

    h0, xb0, cb0, pool0, z0, x1, win1_b, wout1_b = _fwd_layer(x2, win0_b, wout0_b, prm[0], row(ln_g, 0), row(ln_b, 0), None, tt=TT, last=False,
                                                  comm=_gather_comm(bsh_in, bsh_out, 1))
    h1, xb1, cb1, pool1, dz1, dln1, loss8 = _fwd_layer(x1, win1_b, wout1_b, prm[1], row(ln_g, 1), row(ln_b, 1), tgt, tt=TT, last=True)

    dz0, dh1, dwout1, small1, dpw1, dln0 = _bwd_layer(dz1, h1, cb1, pool1, win1_b, wout1_b, prm[1], z0, row(ln_g, 0), tt=TT)
    dwin1 = _wgrad_in(xb1, dh1, tk=TK)
    loss_row = jnp.pad(loss8, ((0, 0), (0, GW - loss8.shape[1])))
    pack1 = _pack([small1, dpw1.reshape(PGD, GW), dln1.reshape(4, GW), dln0.reshape(4, GW), loss_row])
    sums1 = _reduce_to_owner_halves([dwin1, dwout1.reshape(N_CHIP, SHARD_OUT, D_MODEL), pack1], core1, "1")
    gx, dh0, dwout0, small0, dpw0 = _bwd_layer(dz0, h0, cb0, pool0, win0_b, wout0_b, prm[0], None, None, tt=TT)
    pack0 = _pack([small0, dpw0.reshape(PGD, GW)])
    sums0 = _reduce_to_owner_halves([dwout0.reshape(N_CHIP, SHARD_OUT, D_MODEL), pack0], core1, "0")
    dwin0, *landed = _wgrad_in(xb0, dh0, tk=TK, comm=_owner_comm(sums1 + sums0))
    landed1, landed0 = landed[:3], landed[3:]
    landed0 = list(_send_to_owners(_reduce_to_owner_halves([dwin0], core1, "in0"))) + list(landed0)

    g_in = _sum_chips_into(landed0[0], _sum_chips_into(landed1[0], None, 1, core1, rb=256), 0, core1, rb=256)
    g_out = _sum_chips_into(landed0[1], _sum_chips_into(landed1[1], None, 1, core1, rb=SHARD_OUT // 2), 0, core1, rb=SHARD_OUT // 2)
    red_small = jnp.concatenate([_sum_chips(a, rb=PACK_ROWS // N_CHIP // 2) for a in (landed0[2], landed1[2])], axis=0)
    g_in, g_out, g_small = _spread_reduced(g_in, g_out, red_small)

    flat = lambda a: a.reshape(-1, a.shape[-1])
    unflat = lambda a, like: a.reshape(like.shape)
    d_in, nm_in, nv_in = [unflat(a, w_in) for a in _adamw_big(flat(w_in), flat(g_in), flat(m_w_in), flat(v_w_in), rb=256)]
    d_out, nm_out, nv_out = [unflat(a, w_out) for a in _adamw_big(flat(w_out), flat(g_out), flat(m_w_out), flat(v_w_out), rb=SHARD_OUT)]

    hp = PACK_ROWS // N_CHIP // 2
    unpack = lambda o: jnp.concatenate([g_small[:, o:o + hp], g_small[:, 2 * hp + o:3 * hp + o]], axis=1).reshape(PACK_ROWS, GW)
    p0, p1 = unpack(0), unpack(hp)
    small = [p0[0:N_RACC], p1[0:N_RACC]]
    dpw = [p[N_RACC:N_RACC + PGD].reshape(len(POOL_WINDOWS), PGD, PGD) for p in (p0, p1)]
    o = N_RACC + PGD
    g_lng = jnp.stack([p1[o + 4:o + 8].reshape(2, D_MODEL)[0], p1[o:o + 4].reshape(2, D_MODEL)[0]])
    g_lnb = jnp.stack([p1[o + 4:o + 8].reshape(2, D_MODEL)[1], p1[o:o + 4].reshape(2, D_MODEL)[1]])
    mine = lambda a: lax.dynamic_slice_in_dim(a, chip * PGD, PGD, axis=-1)
    stack = lambda f: jnp.stack([f(0), f(1)])
    g_caw = stack(lambda l: mine(small[l][R_DWA:R_DWA + KA]))
    g_cab = stack(lambda l: small[l][R_DCAB])
    g_cbw = stack(lambda l: mine(small[l][R_DWB:R_DWB + KB]))
    g_cbb = stack(lambda l: small[l][R_DCBB])
    g_lbg = stack(lambda l: small[l][R_DLBG])
    g_lbb = stack(lambda l: small[l][R_DLBB])
    g_pw = stack(lambda l: dpw[l])
    g_pb = stack(lambda l: small[l][R_DPB].reshape(len(POOL_WINDOWS), PGD))
    g_ps = stack(lambda l: small[l][R_DPS])
    ws = [conv_a_w, conv_a_b, conv_b_w, conv_b_b, ln_b_g, ln_b_b, pool_w, pool_b, pool_scale, ln_g, ln_b]
    gs = [g_caw, g_cab, g_cbw, g_cbb, g_lbg, g_lbb, g_pw, g_pb, g_ps, g_lng, g_lnb]
    ms = [m_conv_a_w, m_conv_a_b, m_conv_b_w, m_conv_b_b, m_ln_b_g, m_ln_b_b, m_pool_w, m_pool_b, m_pool_scale, m_ln_g, m_ln_b]
    vs = [v_conv_a_w, v_conv_a_b, v_conv_b_w, v_conv_b_b, v_ln_b_g, v_ln_b_b, v_pool_w, v_pool_b, v_pool_scale, v_ln_g, v_ln_b]
    ds, nms, nvs = _adamw_small(ws, gs, ms, vs)

    loss = p1[o + 8, 0]

    def order(in_, small_, out_):
        return [in_, *small_[:9], out_, *small_[9:]]
    return (loss, gx[None], *order(g_in, gs, g_out), *order(d_in, ds, d_out), *order(nm_in, nms, nm_out), *order(nv_in, nvs, nv_out))
```

```python
import functools

import jax
import jax.numpy as jnp
import numpy as np
from jax import lax
from jax.experimental import pallas as pl
from jax.experimental.pallas import tpu as pltpu

F32 = jnp.float32
BF16 = jnp.bfloat16

D_MODEL = 1024
DEPTH = 2
GW = 512
D_IN = 9 * GW
D_MIX = 3 * GW
NG = D_IN // GW
POOL_WINDOWS = (2, 4, 8, 16)
PGD = 128
KA = 3
KB = 31
ALPHA = (2.0 * DEPTH) ** 0.25
LN_EPS = 1e-5
ADAM_LR, ADAM_B1, ADAM_B2, ADAM_EPS, ADAM_WD, ADAM_STEP = 0.001, 0.9, 0.999, 1e-08, 0.01, 10

N_CHIP = 4
SHARD_IN = D_IN // N_CHIP
SHARD_OUT = D_MIX // N_CHIP

SUBLANES = 8
RC = 32
HALO = 32
VMEM_LIMIT = 60 * 1024 * 1024
WOUT_GROUP = 4

R_DWA, R_DCAB, R_DWB, R_DCBB, R_DLBG, R_DLBB, R_DPB, R_DPS, N_RACC = 0, 3, 4, 35, 36, 37, 38, 39, 40


def _sig(v):
    return 0.5 * jnp.tanh(0.5 * v) + 0.5


def _chunks(n_rows, fn, unroll=1, extra=None):
    unroll = min(unroll, n_rows // RC)

    def step(m, carry):
        for u in range(unroll):
            fn(pl.multiple_of((m * unroll + u) * RC, RC))
        if extra is not None:
            extra(m)
        return carry
    lax.fori_loop(0, n_rows // (RC * unroll), step, 0)


def _fold8(v):
    return v.reshape(RC // SUBLANES, SUBLANES, v.shape[-1]).sum(axis=0)


def _build_shifts(ext_ref, sh_ref, shifts, n_rows, first_lane=None):
    for r in shifts:
        lanes = slice(0, ext_ref.shape[1]) if first_lane is None else slice(first_lane(r), ext_ref.shape[1])
        for c0 in range(0, n_rows, RC):
            n = min(RC, n_rows - c0)
            sh_ref[r, pl.ds(c0, n), lanes] = ext_ref[pl.ds(c0 + r, n), lanes]


def _tap(ext_ref, sh_ref, off, base, lanes=None):
    a, r = divmod(off, SUBLANES)
    src = ext_ref if r == 0 else sh_ref.at[r]
    if lanes is None:
        return src[pl.ds(base + SUBLANES * a, RC), :]
    return src[pl.ds(base + SUBLANES * a, RC), lanes]


def _ln_stats(v):
    mu = jnp.mean(v, axis=-1, keepdims=True)
    vc = v - mu
    var = jnp.mean(vc * vc, axis=-1, keepdims=True)
    rstd = lax.rsqrt(var + LN_EPS)
    return vc * rstd, rstd


def _ln_bwd(dy, xhat, rstd, g):
    dxh = dy * g
    m1 = jnp.mean(dxh, axis=-1, keepdims=True)
    m2 = jnp.mean(dxh * xhat, axis=-1, keepdims=True)
    return rstd * (dxh - m1 - xhat * m2)


def _for_taps(ext_ref, sh_ref, base, offsets, fn):
    for r in range(SUBLANES):
        offs = [o for o in offsets if o % SUBLANES == r]
        if not offs:
            continue
        a0, a1 = min(offs) // SUBLANES, max(offs) // SUBLANES
        src = ext_ref if r == 0 else sh_ref.at[r]
        win = src[pl.ds(base + SUBLANES * a0, RC + SUBLANES * (a1 - a0)), :]
        for o in offs:
            a = o // SUBLANES - a0
            fn(o, win[SUBLANES * a:SUBLANES * a + RC])


def _count_table():
    t = np.arange(1, RC + 1, dtype=np.float64)[:, None]
    w = np.repeat(np.asarray(POOL_WINDOWS, np.float64), PGD)[None, :]
    return jnp.asarray(1.0 / np.minimum(t, w), F32)


def _inv_count(cnt_ref, first):
    return jnp.where(first, cnt_ref[...], cnt_ref[RC - 1:RC, :])


def _hcol(h_ref, j, base):
    if len(h_ref.shape) == 3:
        return h_ref[j, pl.ds(base, RC), :].astype(F32)
    return h_ref[pl.ds(base, RC), j * GW:(j + 1) * GW].astype(F32)


def _with_comm(comm, ins, in_specs, out_shape, out_specs, scratch):
    if comm is None:
        return ins, in_specs, out_shape, out_specs, scratch
    hbm = pl.BlockSpec(memory_space=pl.ANY)
    return (ins + list(comm["ins"]), in_specs + [hbm] * len(comm["ins"]), out_shape + list(comm["out_shape"]),
            out_specs + [hbm] * len(comm["out_shape"]), scratch + list(comm["sems"]))


def _split_comm(refs, comm, n_in, n_out):
    refs = list(refs)
    if comm is None:
        return refs, None
    ci, co, cs = len(comm["ins"]), len(comm["out_shape"]), len(comm["sems"])
    own = refs[:n_in] + refs[n_in + ci:n_in + ci + n_out] + refs[n_in + ci + n_out + co:len(refs) - cs]
    return own, (refs[n_in:n_in + ci], refs[n_in + ci + n_out:n_in + ci + n_out + co], refs[len(refs) - cs:])


def _fwd_mixers(h_ref, cb_ref, y_scr, q_ext, ub_ext, cu_ext, sh, p_scr, pl_scr, prm, tt, t0):
    caw, cab, cbw, cbb, lbg, lbb, pw, pb, ps, cnt = prm

    def a1(base):
        q_ext[pl.ds(SUBLANES + base, RC), :] = _hcol(h_ref, 1, base) * _hcol(h_ref, 2, base)
    _chunks(tt, a1)
    _build_shifts(q_ext, sh, (6, 7), tt)

    def a2(base):
        ca = cab[...] + caw[0:1, :] * _tap(q_ext, sh, 6, base) + caw[1:2, :] * _tap(q_ext, sh, 7, base) \
            + caw[2:3, :] * _tap(q_ext, sh, 8, base)
        z = _hcol(h_ref, 3, base)
        y_scr[pl.ds(base, RC), 0:GW] = (_hcol(h_ref, 0, base) * ca * (z * _sig(z))).astype(BF16)
    _chunks(tt, a2, unroll=2)
    q_ext[0:SUBLANES, :] = q_ext[tt:tt + SUBLANES, :]

    def b1(base):
        ub_ext[pl.ds(HALO + base, RC), :] = _hcol(h_ref, 4, base) * _sig(_hcol(h_ref, 5, base))
    _chunks(tt, b1)
    _build_shifts(ub_ext, sh, range(1, 8), tt + HALO - SUBLANES)

    def b2(base):
        cb = [cbb[...] + jnp.zeros((RC, GW), F32)]

        def tap(off, v):
            cb[0] = cb[0] + cbw[off - 2:off - 1, :] * v
        _for_taps(ub_ext, sh, base, range(2, 2 + KB), tap)
        cbr = cb[0].astype(BF16)
        cb_ref[pl.ds(base, RC), :] = cbr
        xhat, _ = _ln_stats(cbr.astype(F32))
        lnv = xhat * lbg[...] + lbb[...]
        z = _hcol(h_ref, 6, base)
        y_scr[pl.ds(base, RC), GW:2 * GW] = (lnv * _sig(lnv) * (z * _sig(z))).astype(BF16)
    _chunks(tt, b2, unroll=4)
    ub_ext[0:HALO, :] = ub_ext[tt:tt + HALO, :]

    def c1(base):
        cu_ext[pl.ds(16 + base, RC), :] = _hcol(h_ref, 7, base)
    _chunks(tt, c1)
    _build_shifts(cu_ext, sh, range(1, 8), tt + SUBLANES, first_lane=lambda r: 0 if r == 7 else PGD if r >= 5 else 2 * PGD)

    def c2(base):
        ic = _inv_count(cnt, base + t0 == 0)
        for g, w in enumerate(POOL_WINDOWS):
            lanes = slice(g * PGD, (g + 1) * PGD)
            acc = _tap(cu_ext, sh, 16, base, lanes)
            for j in range(1, w):
                acc = acc + _tap(cu_ext, sh, 16 - j, base, lanes)
            p = acc * ic[:, lanes] - _tap(cu_ext, sh, 16, base, lanes)
            p_scr[pl.ds(base, RC), lanes] = p.astype(BF16)
    _chunks(tt, c2, unroll=4)
    cu_ext[0:16, :] = cu_ext[tt:tt + 16, :]
    for g in range(len(POOL_WINDOWS)):
        lanes = slice(g * PGD, (g + 1) * PGD)
        pl_scr[:, lanes] = jnp.dot(p_scr[:, lanes], pw[g], preferred_element_type=F32)

    def c3(base):
        z = _hcol(h_ref, 8, base)
        yc0 = (pl_scr[pl.ds(base, RC), :] + pb[...]) * ps[...]
        y_scr[pl.ds(base, RC), 2 * GW:3 * GW] = (yc0 * (z * _sig(z))).astype(BF16)
    _chunks(tt, c3, unroll=2)


def _fwd_layer(x, win_b, wout_b, prm, ln_g, ln_b, target, *, tt, last, comm=None):
    t_len = x.shape[0]
    n_t = t_len // tt

    def body(*refs):
        refs, comm_refs = _split_comm(refs, comm, n_in, n_out)
        if last:
            (x_ref, win_hbm, wout_hbm, caw, cab, cbw, cbb, lbg, lbb, pw, pb, ps, cnt, lng, lnb, tgt_ref,
             h_ref, xb_ref, cb_ref, p_scr, dz_ref, dln_ref, loss_ref,
             win_v, wout_v, y_scr, o_scr, q_ext, ub_ext, cu_ext, sh, pl_scr, acc2, lacc) = refs
        else:
            (x_ref, win_hbm, wout_hbm, caw, cab, cbw, cbb, lbg, lbb, pw, pb, ps, cnt, lng, lnb,
             h_ref, xb_ref, cb_ref, p_scr, z_ref, xn_ref,
             win_v, wout_v, y_scr, o_scr, q_ext, ub_ext, cu_ext, sh, pl_scr) = refs
        i = pl.program_id(0)

        @pl.when(i == 0)
        def _():
            if comm is not None:
                comm["start"](*comm_refs)
            pltpu.sync_copy(win_hbm, win_v)
            pltpu.sync_copy(wout_hbm, wout_v)
            q_ext[0:SUBLANES, :] = jnp.zeros((SUBLANES, GW), F32)
            ub_ext[0:HALO, :] = jnp.zeros((HALO, GW), F32)
            cu_ext[0:16, :] = jnp.zeros((16, GW), F32)
            if last:
                acc2[...] = jnp.zeros_like(acc2)
                lacc[...] = jnp.zeros_like(lacc)

        xb_ref[...] = x_ref[...].astype(BF16)
        for j in range(NG):
            h_ref[:, j * GW:(j + 1) * GW] = jnp.dot(
                xb_ref[...], win_v[:, j * GW:(j + 1) * GW], preferred_element_type=F32).astype(BF16)

        _fwd_mixers(h_ref, cb_ref, y_scr, q_ext, ub_ext, cu_ext, sh, p_scr, pl_scr,
                    (caw, cab, cbw, cbb, lbg, lbb, pw, pb, ps, cnt), tt, i * tt)

        o_scr[...] = jnp.dot(y_scr[...], wout_v[...], preferred_element_type=F32)

        def post(base):
            rows = pl.ds(base, RC)
            z = ALPHA * x_ref[rows, :] + o_scr[rows, :]
            xhat, rstd = _ln_stats(z)
            xn = xhat * lng[...] + lnb[...]
            if last:
                err = xn - tgt_ref[rows, :]
                lacc[...] += _fold8(err * err)
                dxn = err * (1.0 / D_MODEL)
                acc2[0] += _fold8(dxn * xhat)
                acc2[1] += _fold8(dxn)
                dz_ref[rows, :] = _ln_bwd(dxn, xhat, rstd, lng[...])
            else:
                z_ref[rows, :] = z
                xn_ref[rows, :] = xn
        _chunks(tt, post, unroll=8)

        if last:
            @pl.when(i == n_t - 1)
            def _():
                dln_ref[...] = jnp.sum(acc2[...], axis=1)
                loss_ref[...] = jnp.zeros((SUBLANES, 128), F32) + (0.5 / D_MODEL) * jnp.sum(lacc[...])
        if comm is not None:
            @pl.when(i == n_t - 1)
            def _():
                comm["wait"](*comm_refs)

    tile = lambda c: pl.BlockSpec((tt, c), lambda i: (i, 0))
    full = lambda a: pl.BlockSpec(a.shape, lambda i: (0,) * a.ndim)
    hbm = pl.BlockSpec(memory_space=pl.ANY)
    ins = [x, win_b, wout_b, *prm, ln_g, ln_b] + ([target] if last else [])
    in_specs = [tile(D_MODEL), hbm, hbm] + [full(a) for a in (*prm, ln_g, ln_b)] + ([tile(D_MODEL)] if last else [])
    out_shape = [jax.ShapeDtypeStruct((t_len, D_IN), BF16), jax.ShapeDtypeStruct((t_len, D_MODEL), BF16),
                 jax.ShapeDtypeStruct((t_len, GW), BF16), jax.ShapeDtypeStruct((t_len, GW), BF16)]
    out_specs = [tile(D_IN), tile(D_MODEL), tile(GW), tile(GW)]
    if last:
        out_shape += [jax.ShapeDtypeStruct((t_len, D_MODEL), F32), jax.ShapeDtypeStruct((2, D_MODEL), F32),
                      jax.ShapeDtypeStruct((SUBLANES, 128), F32)]
        out_specs += [tile(D_MODEL), pl.BlockSpec((2, D_MODEL), lambda i: (0, 0)),
                      pl.BlockSpec((SUBLANES, 128), lambda i: (0, 0))]
    else:
        out_shape += [jax.ShapeDtypeStruct((t_len, D_MODEL), F32), jax.ShapeDtypeStruct((t_len, D_MODEL), F32)]
        out_specs += [tile(D_MODEL), tile(D_MODEL)]
    scratch = [
        pltpu.VMEM((D_MODEL, D_IN), BF16), pltpu.VMEM((D_MIX, D_MODEL), BF16),
        pltpu.VMEM((tt, D_MIX), BF16), pltpu.VMEM((tt, D_MODEL), F32),
        pltpu.VMEM((tt + SUBLANES, GW), F32), pltpu.VMEM((tt + HALO, GW), F32), pltpu.VMEM((tt + 16, GW), F32),
        pltpu.VMEM((SUBLANES, tt + HALO, GW), F32),
        pltpu.VMEM((tt, GW), F32),
    ]
    if last:
        scratch += [pltpu.VMEM((2, SUBLANES, D_MODEL), F32), pltpu.VMEM((SUBLANES, D_MODEL), F32)]
    n_in, n_out = len(ins), len(out_shape)
    ins, in_specs, out_shape, out_specs, scratch = _with_comm(comm, ins, in_specs, out_shape, out_specs, scratch)
    return pl.pallas_call(
        body, name=("fwd_last" if last else "fwd_layer") + ("" if comm is None else "_comm"), grid=(n_t,),
        in_specs=in_specs, out_specs=out_specs, out_shape=out_shape, scratch_shapes=scratch,
        compiler_params=pltpu.CompilerParams(dimension_semantics=("arbitrary",), vmem_limit_bytes=VMEM_LIMIT),
    )(*ins)


def _dsilu(z, sz):
    return sz * (1.0 + z * (1.0 - sz))


def _bwd_layer(dz, h, cb, p, win_b, wout_b, prm, z_prev, lng_prev, *, tt, comm=None):
    t_len = dz.shape[0]
    n_t = t_len // tt
    has_prev = z_prev is not None
    hb = tt // HALO
    group = min(WOUT_GROUP, n_t)
    assert n_t % group == 0

    def body(*refs):
        refs, comm_refs = _split_comm(refs, comm, n_in, n_out)
        dz_ref, h_ref, halo_ref, cb_ref, p_scr, win_hbm, wout_hbm, caw, cab, cbw, cbb, lbg, lbb, pw, pb, ps, cnt = refs[:17]
        k = 17
        if has_prev:
            zp_ref, lngp = refs[k:k + 2]
            k += 2
        dxo_ref, dh_ref, dwout_hbm, small_ref, dpw_ref = refs[k:k + 5]
        k += 5
        if has_prev:
            dlnp_ref = refs[k]
            k += 1
        (win_v, wout_v, dzb_all, dy_scr, y_all, dx_scr, q_ext, dca_ext, dcb_ext, dpn_ext, sh,
         pl_scr, dpl_scr, dp_scr, racc, dpw_acc, dwout_acc) = refs[k:k + 17]
        k += 17
        if has_prev:
            acc2 = refs[k]
        i = pl.program_id(0)
        ti = n_t - 1 - i
        t0 = ti * tt
        slot = i % group
        slot_rows = pl.ds(pl.multiple_of(slot * tt, tt), tt)
        dzb, y_scr = dzb_all.at[slot_rows], y_all.at[slot_rows]

        @pl.when(i == 0)
        def _():
            if comm is not None:
                comm["start"](*comm_refs)
            pltpu.sync_copy(win_hbm, win_v)
            pltpu.sync_copy(wout_hbm, wout_v)
            dca_ext[tt:tt + SUBLANES, :] = jnp.zeros((SUBLANES, GW), F32)
            dcb_ext[tt:tt + HALO, :] = jnp.zeros((HALO, GW), F32)
            dpn_ext[tt:tt + 16, :] = jnp.zeros((16, GW), F32)
            racc[...] = jnp.zeros_like(racc)
            dpw_acc[...] = jnp.zeros_like(dpw_acc)
            dwout_acc[...] = jnp.zeros_like(dwout_acc)
            if has_prev:
                acc2[...] = jnp.zeros_like(acc2)

        dzb[...] = dz_ref[...].astype(BF16)
        dy_scr[...] = lax.dot_general(dzb[...], wout_v[...], (((1,), (1,)), ((), ())), preferred_element_type=F32)

        live = (ti > 0).astype(F32)
        hh = lambda j, r0, r1: halo_ref[r0:r1, j * GW:(j + 1) * GW].astype(F32)
        q_ext[0:SUBLANES, :] = live * hh(1, 24, 32) * hh(2, 24, 32)

        def a1(base):
            q_ext[pl.ds(SUBLANES + base, RC), :] = _hcol(h_ref, 1, base) * _hcol(h_ref, 2, base)
        _chunks(tt, a1)
        _build_shifts(q_ext, sh, (6, 7), tt)

        def a2(base):
            rows = pl.ds(base, RC)
            q6, q7, q8 = _tap(q_ext, sh, 6, base), _tap(q_ext, sh, 7, base), _tap(q_ext, sh, 8, base)
            ca = cab[...] + caw[0:1, :] * q6 + caw[1:2, :] * q7 + caw[2:3, :] * q8
            bg, z = _hcol(h_ref, 0, base), _hcol(h_ref, 3, base)
            sz = _sig(z)
            sza = z * sz
            dya = dy_scr[rows, 0:GW]
            ya0 = bg * ca
            y_scr[rows, 0:GW] = (ya0 * sza).astype(BF16)
            dya0 = dya * sza
            dh_ref[rows, 3 * GW:4 * GW] = (dya * ya0 * _dsilu(z, sz)).astype(BF16)
            dh_ref[rows, 0:GW] = (dya0 * ca).astype(BF16)
            dca = dya0 * bg
            dca_ext[rows, :] = dca
            racc[R_DCAB] += _fold8(dca)
            racc[R_DWA + 0] += _fold8(dca * q6)
            racc[R_DWA + 1] += _fold8(dca * q7)
            racc[R_DWA + 2] += _fold8(dca * q8)
        _chunks(tt, a2)
        _build_shifts(dca_ext, sh, (1, 2), tt)

        def a3(base):
            rows = pl.ds(base, RC)
            dq = caw[0:1, :] * _tap(dca_ext, sh, 2, base) + caw[1:2, :] * _tap(dca_ext, sh, 1, base) \
                + caw[2:3, :] * _tap(dca_ext, sh, 0, base)
            dh_ref[rows, GW:2 * GW] = (dq * _hcol(h_ref, 2, base)).astype(BF16)
            dh_ref[rows, 2 * GW:3 * GW] = (dq * _hcol(h_ref, 1, base)).astype(BF16)
        _chunks(tt, a3)
        dca_ext[tt:tt + SUBLANES, :] = dca_ext[0:SUBLANES, :]

        def b2(base):
            rows = pl.ds(base, RC)
            xhat, rstd = _ln_stats(cb_ref[rows, :].astype(F32))
            lnv = xhat * lbg[...] + lbb[...]
            sl = _sig(lnv)
            s = lnv * sl
            z = _hcol(h_ref, 6, base)
            sz = _sig(z)
            szb = z * sz
            y_scr[rows, GW:2 * GW] = (s * szb).astype(BF16)
            dyb = dy_scr[rows, GW:2 * GW]
            dh_ref[rows, 6 * GW:7 * GW] = (dyb * s * _dsilu(z, sz)).astype(BF16)
            dlnv = dyb * szb * _dsilu(lnv, sl)
            racc[R_DLBG] += _fold8(dlnv * xhat)
            racc[R_DLBB] += _fold8(dlnv)
            dcb = _ln_bwd(dlnv, xhat, rstd, lbg[...])
            dcb_ext[rows, :] = dcb
            racc[R_DCBB] += _fold8(dcb)
        _chunks(tt, b2, unroll=4)
        _build_shifts(dcb_ext, sh, range(1, 8), tt + HALO - SUBLANES)

        def b3(base):
            rows = pl.ds(base, RC)
            v, gt = _hcol(h_ref, 4, base), _hcol(h_ref, 5, base)
            sg = _sig(gt)
            ub = v * sg
            dubv = [jnp.zeros((RC, GW), F32)]

            def tap(off, win):
                dubv[0] = dubv[0] + cbw[KB - 1 - off:KB - off, :] * win
                racc[R_DWB + KB - 1 - off] += _fold8(ub * win)
            _for_taps(dcb_ext, sh, base, range(KB), tap)
            dub = dubv[0]
            dh_ref[rows, 4 * GW:5 * GW] = (dub * sg).astype(BF16)
            dh_ref[rows, 5 * GW:6 * GW] = (dub * v * sg * (1.0 - sg)).astype(BF16)
        _chunks(tt, b3, unroll=4)
        dcb_ext[tt:tt + HALO, :] = dcb_ext[0:HALO, :]

        for g in range(len(POOL_WINDOWS)):
            lanes = slice(g * PGD, (g + 1) * PGD)
            pl_scr[:, lanes] = jnp.dot(p_scr[:, lanes], pw[g], preferred_element_type=F32)

        def c3(base):
            rows = pl.ds(base, RC)
            z = _hcol(h_ref, 8, base)
            sz = _sig(z)
            szc = z * sz
            plb = pl_scr[rows, :] + pb[...]
            yc0 = plb * ps[...]
            y_scr[rows, 2 * GW:3 * GW] = (yc0 * szc).astype(BF16)
            dyc = dy_scr[rows, 2 * GW:3 * GW]
            dh_ref[rows, 8 * GW:9 * GW] = (dyc * yc0 * _dsilu(z, sz)).astype(BF16)
            dyc0 = dyc * szc
            racc[R_DPS] += _fold8(dyc0 * plb)
            dpl = dyc0 * ps[...]
            racc[R_DPB] += _fold8(dpl)
            dpl_scr[rows, :] = dpl.astype(BF16)
        _chunks(tt, c3, unroll=4)
        for g in range(len(POOL_WINDOWS)):
            lanes = slice(g * PGD, (g + 1) * PGD)
            dpw_acc[g] += lax.dot_general(p_scr[:, lanes], dpl_scr[:, lanes], (((0,), (0,)), ((), ())),
                                          preferred_element_type=F32)
            dp_scr[:, lanes] = lax.dot_general(dpl_scr[:, lanes], pw[g], (((1,), (1,)), ((), ())),
                                               preferred_element_type=F32)

        def c4(base):
            rows = pl.ds(base, RC)
            dpn_ext[rows, :] = dp_scr[rows, :] * _inv_count(cnt, base + t0 == 0)
        _chunks(tt, c4)
        _build_shifts(dpn_ext, sh, range(1, 8), tt + SUBLANES, first_lane=lambda r: 0 if r == 1 else PGD if r <= 3 else 2 * PGD)

        def c5(base):
            rows = pl.ds(base, RC)
            for g, w in enumerate(POOL_WINDOWS):
                lanes = slice(g * PGD, (g + 1) * PGD)
                acc = _tap(dpn_ext, sh, 0, base, lanes)
                for j in range(1, w):
                    acc = acc + _tap(dpn_ext, sh, j, base, lanes)
                dh_ref[rows, 7 * GW + g * PGD:7 * GW + (g + 1) * PGD] = (acc - dp_scr[rows, lanes]).astype(BF16)
        _chunks(tt, c5, unroll=4)
        dpn_ext[tt:tt + 16, :] = dpn_ext[0:16, :]

        @pl.when(slot == group - 1)
        def _():
            for r in range(D_MIX // GW):
                dwout_acc[r * GW:(r + 1) * GW, :] += lax.dot_general(
                    y_all[:, r * GW:(r + 1) * GW], dzb_all[...], (((0,), (0,)), ((), ())), preferred_element_type=F32)
        dx_scr[...] = lax.dot_general(dh_ref[...], win_v[...], (((1,), (1,)), ((), ())), preferred_element_type=F32)

        def post(base):
            rows = pl.ds(base, RC)
            dx = ALPHA * dz_ref[rows, :] + dx_scr[rows, :]
            if has_prev:
                xhat, rstd = _ln_stats(zp_ref[rows, :])
                acc2[0] += _fold8(dx * xhat)
                acc2[1] += _fold8(dx)
                dxo_ref[rows, :] = _ln_bwd(dx, xhat, rstd, lngp[...])
            else:
                dxo_ref[rows, :] = dx
        _chunks(tt, post, unroll=8)

        @pl.when(i == n_t - 1)
        def _():
            small_ref[...] = jnp.sum(racc[...], axis=1)
            dpw_ref[...] = dpw_acc[...]
            pltpu.sync_copy(dwout_acc, dwout_hbm)
            if has_prev:
                dlnp_ref[...] = jnp.sum(acc2[...], axis=1)
            if comm is not None:
                comm["wait"](*comm_refs)

    rtile = lambda c: pl.BlockSpec((tt, c), lambda i: (n_t - 1 - i, 0))
    full = lambda a: pl.BlockSpec(a.shape, lambda i: (0,) * a.ndim)
    const = lambda shp: pl.BlockSpec(shp, lambda i: (0,) * len(shp))
    hbm = pl.BlockSpec(memory_space=pl.ANY)
    halo_spec = pl.BlockSpec((HALO, D_IN), lambda i: (jnp.maximum((n_t - 1 - i) * hb - 1, 0), 0))
    ins = [dz, h, h, cb, p, win_b, wout_b, *prm] + ([z_prev, lng_prev] if has_prev else [])
    in_specs = [rtile(D_MODEL), rtile(D_IN), halo_spec, rtile(GW), rtile(GW), hbm, hbm] + [full(a) for a in prm] \
        + ([rtile(D_MODEL), full(lng_prev)] if has_prev else [])
    out_shape = [jax.ShapeDtypeStruct((t_len, D_MODEL), F32), jax.ShapeDtypeStruct((t_len, D_IN), BF16),
                 jax.ShapeDtypeStruct((D_MIX, D_MODEL), F32), jax.ShapeDtypeStruct((N_RACC, GW), F32),
                 jax.ShapeDtypeStruct((len(POOL_WINDOWS), PGD, PGD), F32)]
    out_specs = [rtile(D_MODEL), rtile(D_IN), hbm, const((N_RACC, GW)), const((len(POOL_WINDOWS), PGD, PGD))]
    if has_prev:
        out_shape.append(jax.ShapeDtypeStruct((2, D_MODEL), F32))
        out_specs.append(const((2, D_MODEL)))
    scratch = [
        pltpu.VMEM((D_MODEL, D_IN), BF16), pltpu.VMEM((D_MIX, D_MODEL), BF16),
        pltpu.VMEM((group * tt, D_MODEL), BF16), pltpu.VMEM((tt, D_MIX), F32), pltpu.VMEM((group * tt, D_MIX), BF16),
        pltpu.VMEM((tt, D_MODEL), F32),
        pltpu.VMEM((tt + SUBLANES, GW), F32),
        pltpu.VMEM((tt + SUBLANES, GW), F32), pltpu.VMEM((tt + HALO, GW), F32), pltpu.VMEM((tt + 16, GW), F32),
        pltpu.VMEM((SUBLANES, tt + HALO, GW), F32),
        pltpu.VMEM((tt, GW), F32), pltpu.VMEM((tt, GW), BF16), pltpu.VMEM((tt, GW), F32),
        pltpu.VMEM((N_RACC, SUBLANES, GW), F32), pltpu.VMEM((len(POOL_WINDOWS), PGD, PGD), F32),
        pltpu.VMEM((D_MIX, D_MODEL), F32),
    ]
    if has_prev:
        scratch.append(pltpu.VMEM((2, SUBLANES, D_MODEL), F32))
    n_in, n_out = len(ins), len(out_shape)
    ins, in_specs, out_shape, out_specs, scratch = _with_comm(comm, ins, in_specs, out_shape, out_specs, scratch)
    return pl.pallas_call(
        body, name=("bwd_layer_prev" if has_prev else "bwd_layer") + ("" if comm is None else "_comm"), grid=(n_t,),
        in_specs=in_specs, out_specs=out_specs, out_shape=out_shape, scratch_shapes=scratch,
        compiler_params=pltpu.CompilerParams(dimension_semantics=("arbitrary",), vmem_limit_bytes=VMEM_LIMIT),
    )(*ins)


def _wgrad_in(xb, dh, *, tk, comm=None):
    t_len = xb.shape[0]
    tk = min(tk, t_len)
    n_k = t_len // tk

    def body(*refs):
        (x_ref, dh_ref, o_ref), comm_refs = _split_comm(refs, comm, 2, 1)
        j, k = pl.program_id(0), pl.program_id(1)

        @pl.when(k == 0)
        def _():
            o_ref[...] = jnp.zeros_like(o_ref)
        if comm is not None:
            @pl.when((j == 0) & (k == 0))
            def _():
                comm["start"](*comm_refs)
        o_ref[0] += lax.dot_general(x_ref[...], dh_ref[...], (((0,), (0,)), ((), ())), preferred_element_type=F32)
        if comm is not None:
            @pl.when((j == N_CHIP - 1) & (k == n_k - 1))
            def _():
                comm["wait"](*comm_refs)

    ins = [xb, dh]
    in_specs = [pl.BlockSpec((tk, D_MODEL), lambda j, k: (k, 0)), pl.BlockSpec((tk, SHARD_IN), lambda j, k: (k, j))]
    out_shape = [jax.ShapeDtypeStruct((N_CHIP, D_MODEL, SHARD_IN), F32)]
    out_specs = [pl.BlockSpec((1, D_MODEL, SHARD_IN), lambda j, k: (j, 0, 0))]
    ins, in_specs, out_shape, out_specs, scratch = _with_comm(comm, ins, in_specs, out_shape, out_specs, [])
    outs = pl.pallas_call(
        body, name="wgrad_in" + ("" if comm is None else "_comm"), grid=(N_CHIP, n_k),
        in_specs=in_specs, out_specs=out_specs, out_shape=out_shape, scratch_shapes=scratch,
        compiler_params=pltpu.CompilerParams(dimension_semantics=("arbitrary", "arbitrary"), vmem_limit_bytes=VMEM_LIMIT),
    )(*ins)
    return outs[0] if comm is None else outs


MESH = pl.DeviceIdType.MESH
ANY = pl.BlockSpec(memory_space=pl.ANY)


def _place():
    x, y, c = lax.axis_index("x"), lax.axis_index("y"), lax.axis_index("c")
    others = [(1 - x, y), (x, 1 - y), (1 - x, 1 - y)]
    return x, y, c, 2 * x + y, [(ox, oy, 2 * ox + oy) for ox, oy in others]


def _rcopy(src, dst, send_sems, recv_sems, k, dev):
    return pltpu.make_async_remote_copy(src_ref=src, dst_ref=dst, send_sem=send_sems.at[k], recv_sem=recv_sems.at[k],
                                        device_id=dev, device_id_type=MESH)


def _gather_weights(w_in, w_out, cw):
    hi, ho = D_MODEL // 2, SHARD_OUT // 2

    def body(win_ref, wout_ref, cw_ref, owin, owout, ocw, bin_v, bout_v, send_sems, recv_sems, lsem):
        x, y, c, me, others = _place()
        for l in range(DEPTH):
            for r0 in range(0, D_MODEL, 256):
                bin_v[l, r0:r0 + 256, :] = win_ref[l, r0:r0 + 256, :].astype(BF16)
            bout_v[l] = wout_ref[l].astype(BF16)
        cin = pl.ds(pl.multiple_of(me * SHARD_IN, 128), SHARD_IN)
        rout = pl.ds(pl.multiple_of(me * SHARD_OUT, 128), SHARD_OUT)
        local = [pltpu.make_async_copy(bin_v.at[0], owin.at[:, cin], lsem.at[0]),
                 pltpu.make_async_copy(bout_v.at[0], owout.at[rout, :], lsem.at[1]),
                 pltpu.make_async_copy(cw_ref, ocw.at[me], lsem.at[2])]
        for cp in local:
            cp.start()

        def in_half(chip, core):
            return owin.at[pl.ds(pl.multiple_of(core * hi, 256), hi), pl.ds(pl.multiple_of(chip * SHARD_IN, 128), SHARD_IN)]

        def out_half(chip, core):
            return owout.at[pl.ds(pl.multiple_of(chip * SHARD_OUT + core * ho, 64), ho), :]

        first = []
        for k, (ox, oy, _) in enumerate(others):
            dev = (ox, oy, c)
            first.append(_rcopy(bin_v.at[0, pl.ds(pl.multiple_of(c * hi, 256), hi), :], in_half(me, c), send_sems, recv_sems, k, dev))
            first.append(_rcopy(bout_v.at[0, pl.ds(pl.multiple_of(c * ho, 64), ho), :], out_half(me, c), send_sems, recv_sems, 3 + k, dev))
            first.append(_rcopy(cw_ref, ocw.at[me], send_sems, recv_sems, 6 + k, dev))
        for cp in first:
            cp.start()
        sib = (x, y, 1 - c)
        passed = []
        for k, (ox, oy, oc) in enumerate(others):
            _rcopy(in_half(oc, c), in_half(oc, c), send_sems, recv_sems, k, sib).wait_recv()
            fwd_in = _rcopy(in_half(oc, c), in_half(oc, c), send_sems, recv_sems, 9 + k, sib)
            fwd_in.start()
            _rcopy(out_half(oc, c), out_half(oc, c), send_sems, recv_sems, 3 + k, sib).wait_recv()
            fwd_out = _rcopy(out_half(oc, c), out_half(oc, c), send_sems, recv_sems, 12 + k, sib)
            fwd_out.start()
            passed += [fwd_in, fwd_out]
        for k, (ox, oy, oc) in enumerate(others):
            _rcopy(cw_ref, ocw.at[oc], send_sems, recv_sems, 6 + k, sib).wait_recv()
            _rcopy(in_half(oc, 1 - c), in_half(oc, 1 - c), send_sems, recv_sems, 9 + k, sib).wait_recv()
            _rcopy(out_half(oc, 1 - c), out_half(oc, 1 - c), send_sems, recv_sems, 12 + k, sib).wait_recv()
        for cp in first + passed:
            cp.wait_send()
        for cp in local:
            cp.wait()

    vm = pl.BlockSpec(memory_space=pltpu.VMEM)
    return pl.pallas_call(
        body, name="gather_weights",
        in_specs=[vm, vm, vm], out_specs=[ANY, ANY, ANY, vm, vm],
        out_shape=[jax.ShapeDtypeStruct((D_MODEL, D_IN), BF16), jax.ShapeDtypeStruct((D_MIX, D_MODEL), BF16),
                   jax.ShapeDtypeStruct((N_CHIP,) + cw.shape, F32),
                   jax.ShapeDtypeStruct((DEPTH, D_MODEL, SHARD_IN), BF16), jax.ShapeDtypeStruct((DEPTH, SHARD_OUT, D_MODEL), BF16)],
        scratch_shapes=[pltpu.SemaphoreType.DMA((15,)), pltpu.SemaphoreType.DMA((15,)), pltpu.SemaphoreType.DMA((3,))],
        compiler_params=pltpu.CompilerParams(vmem_limit_bytes=VMEM_LIMIT),
    )(w_in, w_out, cw)


def _gather_starts(bsh_in, bsh_out, owin, owout, send_sems, recv_sems, lsem, layer):
    x, y, c, me, others = _place()
    hi, ho = D_MODEL // 2, SHARD_OUT // 2
    pltpu.make_async_copy(bsh_in.at[layer], owin.at[:, pl.ds(pl.multiple_of(me * SHARD_IN, 128), SHARD_IN)], lsem.at[0]).start()
    pltpu.make_async_copy(bsh_out.at[layer], owout.at[pl.ds(pl.multiple_of(me * SHARD_OUT, 128), SHARD_OUT), :], lsem.at[1]).start()
    for k, (ox, oy, _) in enumerate(others):
        for t in range(2):
            pltpu.make_async_remote_copy(
                src_ref=bsh_in.at[layer, pl.ds(pl.multiple_of(c * hi, 256), hi), :],
                dst_ref=owin.at[pl.ds(pl.multiple_of(c * hi, 256), hi), pl.ds(pl.multiple_of(me * SHARD_IN, 128), SHARD_IN)],
                send_sem=send_sems.at[2 * k + t], recv_sem=recv_sems.at[2 * k + c], device_id=(ox, oy, t), device_id_type=MESH).start()
            pltpu.make_async_remote_copy(
                src_ref=bsh_out.at[layer, pl.ds(pl.multiple_of(c * ho, 64), ho), :],
                dst_ref=owout.at[pl.ds(pl.multiple_of(me * SHARD_OUT + c * ho, 64), ho), :],
                send_sem=send_sems.at[6 + 2 * k + t], recv_sem=recv_sems.at[6 + 2 * k + c], device_id=(ox, oy, t), device_id_type=MESH).start()


def _gather_waits(bsh_in, bsh_out, owin, owout, send_sems, recv_sems, lsem, layer):
    x, y, c, me, others = _place()
    hi, ho = D_MODEL // 2, SHARD_OUT // 2
    src_in = bsh_in.at[layer, pl.ds(0, hi), :]
    src_out = bsh_out.at[layer, pl.ds(0, ho), :]
    for k, (ox, oy, oc) in enumerate(others):
        for t in range(2):
            dst_in = owin.at[pl.ds(t * hi, hi), pl.ds(pl.multiple_of(oc * SHARD_IN, 128), SHARD_IN)]
            dst_out = owout.at[pl.ds(pl.multiple_of(oc * SHARD_OUT + t * ho, 64), ho), :]
            a = pltpu.make_async_remote_copy(src_ref=src_in, dst_ref=dst_in, send_sem=send_sems.at[2 * k + t],
                                             recv_sem=recv_sems.at[2 * k + t], device_id=(ox, oy, t), device_id_type=MESH)
            b = pltpu.make_async_remote_copy(src_ref=src_out, dst_ref=dst_out, send_sem=send_sems.at[6 + 2 * k + t],
                                             recv_sem=recv_sems.at[6 + 2 * k + t], device_id=(ox, oy, t), device_id_type=MESH)
            a.wait_send()
            a.wait_recv()
            b.wait_send()
            b.wait_recv()
    pltpu.make_async_copy(bsh_in.at[layer], owin.at[:, pl.ds(pl.multiple_of(me * SHARD_IN, 128), SHARD_IN)], lsem.at[0]).wait()
    pltpu.make_async_copy(bsh_out.at[layer], owout.at[pl.ds(pl.multiple_of(me * SHARD_OUT, 128), SHARD_OUT), :], lsem.at[1]).wait()


def _gather_comm(bsh_in, bsh_out, layer):
    return dict(ins=[bsh_in, bsh_out],
                out_shape=[jax.ShapeDtypeStruct((D_MODEL, D_IN), BF16), jax.ShapeDtypeStruct((D_MIX, D_MODEL), BF16)],
                sems=[pltpu.SemaphoreType.DMA((12,)), pltpu.SemaphoreType.DMA((12,)), pltpu.SemaphoreType.DMA((2,))],
                start=lambda ins, outs, sems: _gather_starts(ins[0], ins[1], outs[0], outs[1], *sems, layer),
                wait=lambda ins, outs, sems: _gather_waits(ins[0], ins[1], outs[0], outs[1], *sems, layer))


def _exchange_halves(arrs, tag):
    n = len(arrs)

    def body(*refs):
        ins, outs, (send_sems, recv_sems) = refs[:n], refs[n:2 * n], refs[2 * n:]
        x, y, c, _, _ = _place()
        cps = []
        for m in range(n):
            half = ins[m].shape[1] // 2
            cps.append(_rcopy(ins[m].at[:, pl.ds(pl.multiple_of((1 - c) * half, SUBLANES), half), :], outs[m],
                              send_sems, recv_sems, m, (x, y, 1 - c)))
        for cp in cps:
            cp.start()
        for cp in cps:
            cp.wait()

    return pl.pallas_call(
        body, name="exchange_halves_" + tag, in_specs=[ANY] * n, out_specs=[ANY] * n,
        out_shape=[jax.ShapeDtypeStruct((a.shape[0], a.shape[1] // 2, a.shape[2]), F32) for a in arrs],
        scratch_shapes=[pltpu.SemaphoreType.DMA((n,)), pltpu.SemaphoreType.DMA((n,))],
    )(*arrs)


def _add_own_half(a, got, core, *, rb, dtype):
    nj, r, cdim = a.shape
    half = r // 2

    def body(core_ref, a_ref, g_ref, o_ref):
        o_ref[...] = (a_ref[0] + g_ref[...]).astype(dtype)

    return pl.pallas_call(
        body, name="add_own_half",
        grid_spec=pltpu.PrefetchScalarGridSpec(
            num_scalar_prefetch=1, grid=(nj, half // rb),
            in_specs=[pl.BlockSpec((1, 1, rb, cdim), lambda j, i, cr: (j, cr[0], i, 0)),
                      pl.BlockSpec((1, rb, cdim), lambda j, i, cr: (j, i, 0))],
            out_specs=pl.BlockSpec((1, rb, cdim), lambda j, i, cr: (j, i, 0))),
        out_shape=jax.ShapeDtypeStruct((nj, half, cdim), dtype),
    )(core, a.reshape(nj, 2, half, cdim), got)


def _owner_starts(ins, outs, sems):
    send_sems, recv_sems, lsem = sems
    x, y, c, me, others = _place()
    for m in range(len(ins)):
        pltpu.make_async_copy(ins[m].at[me], outs[m].at[me], lsem.at[m]).start()
        for k, (ox, oy, oc) in enumerate(others):
            _rcopy(ins[m].at[oc], outs[m].at[me], send_sems, recv_sems, 3 * m + k, (ox, oy, c)).start()


def _owner_waits(ins, outs, sems):
    send_sems, recv_sems, lsem = sems
    x, y, c, me, others = _place()
    for m in range(len(ins)):
        for k, (ox, oy, oc) in enumerate(others):
            _rcopy(ins[m].at[oc], outs[m].at[oc], send_sems, recv_sems, 3 * m + k, (ox, oy, c)).wait()
        pltpu.make_async_copy(ins[m].at[me], outs[m].at[me], lsem.at[m]).wait()


def _owner_comm(arrs):
    n = len(arrs)
    return dict(ins=arrs, out_shape=[jax.ShapeDtypeStruct(a.shape, a.dtype) for a in arrs],
                sems=[pltpu.SemaphoreType.DMA((3 * n,)), pltpu.SemaphoreType.DMA((3 * n,)), pltpu.SemaphoreType.DMA((n,))],
                start=_owner_starts, wait=_owner_waits)


def _send_to_owners(arrs):
    n = len(arrs)

    def body(*refs):
        ins, outs, sems = refs[:n], refs[n:2 * n], refs[2 * n:]
        _owner_starts(ins, outs, sems)
        _owner_waits(ins, outs, sems)

    job = _owner_comm(arrs)
    return pl.pallas_call(
        body, name="send_to_owners", in_specs=[ANY] * n, out_specs=[ANY] * n,
        out_shape=job["out_shape"], scratch_shapes=job["sems"],
    )(*arrs)


def _sum_chips(a, *, rb):
    nj, r, cdim = a.shape

    def body(a_ref, o_ref):
        f = lambda k: a_ref[k].astype(F32)
        o_ref[...] = ((f(0) + f(1)) + f(2)) + f(3)

    return pl.pallas_call(
        body, name="sum_chips", grid=(r // rb,),
        in_specs=[pl.BlockSpec((nj, rb, cdim), lambda i: (0, i, 0))],
        out_specs=pl.BlockSpec((rb, cdim), lambda i: (i, 0)),
        out_shape=jax.ShapeDtypeStruct((r, cdim), F32),
    )(a)


def _sum_chips_into(a, dest, layer, core, *, rb):
    nj, half, cdim = a.shape
    nb = half // rb

    def body(*refs):
        a_ref, o_ref = refs[1], refs[-1]
        f = lambda k: a_ref[k].astype(F32)
        o_ref[0] = ((f(0) + f(1)) + f(2)) + f(3)

    grid_spec = pltpu.PrefetchScalarGridSpec(
        num_scalar_prefetch=1, grid=(nb,),
        in_specs=[pl.BlockSpec((nj, rb, cdim), lambda i, cr: (0, i, 0))] + ([] if dest is None else [ANY]),
        out_specs=pl.BlockSpec((1, rb, cdim), lambda i, cr: (layer, cr[0] * nb + i, 0)))
    return pl.pallas_call(
        body, name="sum_chips_into", grid_spec=grid_spec,
        out_shape=jax.ShapeDtypeStruct((DEPTH, 2 * half, cdim), F32),
        input_output_aliases={} if dest is None else {2: 0},
    )(*([core, a] if dest is None else [core, a, dest]))


def _spread_reduced(g_in, g_out, red_small):
    hs = red_small.shape[0]

    def body(gin_in, gout_in, sm, gin, gout, fsm, gsm, send_sems, recv_sems, lsem):
        x, y, c, me, others = _place()
        sib = (x, y, 1 - c)
        hi, ho = D_MODEL // 2, SHARD_OUT // 2
        ri, ro = pl.ds(pl.multiple_of(c * hi, SUBLANES), hi), pl.ds(pl.multiple_of(c * ho, SUBLANES), ho)
        remote = [_rcopy(gin.at[:, ri, :], gin.at[:, ri, :], send_sems, recv_sems, 0, sib),
                  _rcopy(gout.at[:, ro, :], gout.at[:, ro, :], send_sems, recv_sems, 1, sib)]
        own_small = pltpu.make_async_copy(sm, gsm.at[me], lsem.at[0])
        small = [_rcopy(sm, gsm.at[me], send_sems, recv_sems, 2 + k, (ox, oy, c)) for k, (ox, oy, _) in enumerate(others)]
        for cp in remote + [own_small] + small:
            cp.start()
        own_small.wait()
        for cp in small:
            cp.wait()
        mine = fsm.at[:, pl.ds(pl.multiple_of(c * hs, SUBLANES), hs), :]
        keep = pltpu.make_async_copy(gsm, mine, lsem.at[1])
        give = _rcopy(gsm, mine, send_sems, recv_sems, 5, sib)
        keep.start()
        give.start()
        for cp in remote + [give]:
            cp.wait()
        keep.wait()

    return pl.pallas_call(
        body, name="spread_reduced", in_specs=[ANY] * 3, out_specs=[ANY] * 4,
        out_shape=[jax.ShapeDtypeStruct(g_in.shape, F32), jax.ShapeDtypeStruct(g_out.shape, F32),
                   jax.ShapeDtypeStruct((N_CHIP, 2 * hs, GW), F32), jax.ShapeDtypeStruct((N_CHIP, hs, GW), F32)],
        input_output_aliases={0: 0, 1: 1},
        scratch_shapes=[pltpu.SemaphoreType.DMA((6,)), pltpu.SemaphoreType.DMA((6,)), pltpu.SemaphoreType.DMA((2,))],
    )(g_in, g_out, red_small)[:3]


def _adamw_math(w, g, m, v):
    m = ADAM_B1 * m + (1.0 - ADAM_B1) * g
    v = ADAM_B2 * v + (1.0 - ADAM_B2) * (g * g)
    m_hat = m / (1.0 - ADAM_B1 ** ADAM_STEP)
    v_hat = v / (1.0 - ADAM_B2 ** ADAM_STEP)
    delta = -ADAM_LR * (m_hat / (jnp.sqrt(v_hat) + ADAM_EPS) + ADAM_WD * w)
    return delta, m, v


def _adamw_big(w, g, m, v, *, rb):
    r, cdim = w.shape

    def body(w_ref, g_ref, m_ref, v_ref, d_ref, nm_ref, nv_ref):
        d_ref[...], nm_ref[...], nv_ref[...] = _adamw_math(w_ref[...], g_ref[...], m_ref[...], v_ref[...])

    spec = pl.BlockSpec((rb, cdim), lambda i: (i, 0))
    return pl.pallas_call(
        body, name="adamw_big", grid=(r // rb,), in_specs=[spec] * 4, out_specs=[spec] * 3,
        out_shape=[jax.ShapeDtypeStruct((r, cdim), F32)] * 3,
    )(w, g, m, v)


def _adamw_small(ws, gs, ms, vs):
    n = len(ws)

    def body(*refs):
        w, g, m, v = refs[:n], refs[n:2 * n], refs[2 * n:3 * n], refs[3 * n:4 * n]
        d, nm, nv = refs[4 * n:5 * n], refs[5 * n:6 * n], refs[6 * n:7 * n]
        for k in range(n):
            d[k][...], nm[k][...], nv[k][...] = _adamw_math(w[k][...], g[k][...], m[k][...], v[k][...])

    shapes = [jax.ShapeDtypeStruct(a.shape, F32) for a in ws]
    outs = pl.pallas_call(body, name="adamw_small", out_shape=shapes * 3)(*ws, *gs, *ms, *vs)
    return outs[:n], outs[n:2 * n], outs[2 * n:]


TT = 256
TK = 2048
CW_ROWS = 40
PACK_ROWS = 192


def _pack(rows):
    packed = jnp.concatenate(rows, axis=0)
    packed = jnp.pad(packed, ((0, PACK_ROWS - packed.shape[0]), (0, 0)))
    return packed.reshape(N_CHIP, PACK_ROWS // N_CHIP, GW)


def _reduce_to_owner_halves(parts, core1, tag):
    got = _exchange_halves(parts, tag)
    rbs = {D_MODEL: 256, SHARD_OUT: SHARD_OUT // 2, PACK_ROWS // N_CHIP: PACK_ROWS // N_CHIP // 2}
    return [_add_own_half(a, g, core1, rb=rbs[a.shape[1]], dtype=F32 if a.shape[1] == PACK_ROWS // N_CHIP else BF16)
            for a, g in zip(parts, got)]


def kernel(x, w_in, conv_a_w, conv_a_b, conv_b_w, conv_b_b, ln_b_g, ln_b_b, pool_w, pool_b, pool_scale, w_out, ln_g, ln_b, loss_target, m_w_in, m_conv_a_w, m_conv_a_b, m_conv_b_w, m_conv_b_b, m_ln_b_g, m_ln_b_b, m_pool_w, m_pool_b, m_pool_scale, m_w_out, m_ln_g, m_ln_b, v_w_in, v_conv_a_w, v_conv_a_b, v_conv_b_w, v_conv_b_b, v_ln_b_g, v_ln_b_b, v_pool_w, v_pool_b, v_pool_scale, v_w_out, v_ln_g, v_ln_b):
    chip = 2 * lax.axis_index("x") + lax.axis_index("y")
    core1 = lax.axis_index("c").reshape(1).astype(jnp.int32)
    x2, tgt = x[0], loss_target[0]

    cw = jnp.zeros((DEPTH, CW_ROWS, PGD), F32).at[:, 0:KA].set(conv_a_w).at[:, 8:8 + KB].set(conv_b_w)
    win0_b, wout0_b, cw_all, bsh_in, bsh_out = _gather_weights(w_in, w_out, cw)
    cw_full = jnp.transpose(cw_all, (1, 2, 0, 3)).reshape(DEPTH, CW_ROWS, GW)
    row = lambda a, l: a[l].reshape(1, -1)
    cnt = _count_table()
    prm = [(cw_full[l, 0:KA], row(conv_a_b, l), cw_full[l, 8:8 + KB], row(conv_b_b, l), row(ln_b_g, l), row(ln_b_b, l),
            pool_w[l].astype(BF16), row(pool_b, l), row(pool_scale, l), cnt) for l in range(DEPTH)]

    h0, xb0, cb0, pool0, z0, x1, win1_b, wout1_b = _fwd_layer(x2, win0_b, wout0_b, prm[0], row(ln_g, 0), row(ln_b, 0), None, tt=TT, last=False,
                                                  comm=_gather_comm(bsh_in, bsh_out, 1))
    h1, xb1, cb1, pool1, dz1, dln1, loss8 = _fwd_layer(x1, win1_b, wout1_b, prm[1], row(ln_g, 1), row(ln_b, 1), tgt, tt=TT, last=True)

    dz0, dh1, dwout1, small1, dpw1, dln0 = _bwd_layer(dz1, h1, cb1, pool1, win1_b, wout1_b, prm[1], z0, row(ln_g, 0), tt=TT)
    dwin1 = _wgrad_in(xb1, dh1, tk=TK)
    loss_row = jnp.pad(loss8, ((0, 0), (0, GW - loss8.shape[1])))
    pack1 = _pack([small1, dpw1.reshape(PGD, GW), dln1.reshape(4, GW), dln0.reshape(4, GW), loss_row])
    sums1 = _reduce_to_owner_halves([dwin1, dwout1.reshape(N_CHIP, SHARD_OUT, D_MODEL), pack1], core1, "1")
    gx, dh0, dwout0, small0, dpw0 = _bwd_layer(dz0, h0, cb0, pool0, win0_b, wout0_b, prm[0], None, None, tt=TT)
    pack0 = _pack([small0, dpw0.reshape(PGD, GW)])
    sums0 = _reduce_to_owner_halves([dwout0.reshape(N_CHIP, SHARD_OUT, D_MODEL), pack0], core1, "0")
    dwin0, *landed = _wgrad_in(xb0, dh0, tk=TK, comm=_owner_comm(sums1 + sums0))
    landed1, landed0 = landed[:3], landed[3:]
    landed0 = list(_send_to_owners(_reduce_to_owner_halves([dwin0], core1, "in0"))) + list(landed0)

    g_in = _sum_chips_into(landed0[0], _sum_chips_into(landed1[0], None, 1, core1, rb=256), 0, core1, rb=256)
    g_out = _sum_chips_into(landed0[1], _sum_chips_into(landed1[1], None, 1, core1, rb=SHARD_OUT // 2), 0, core1, rb=SHARD_OUT // 2)
    red_small = jnp.concatenate([_sum_chips(a, rb=PACK_ROWS // N_CHIP // 2) for a in (landed0[2], landed1[2])], axis=0)
    g_in, g_out, g_small = _spread_reduced(g_in, g_out, red_small)

    flat = lambda a: a.reshape(-1, a.shape[-1])
    unflat = lambda a, like: a.reshape(like.shape)
    d_in, nm_in, nv_in = [unflat(a, w_in) for a in _adamw_big(flat(w_in), flat(g_in), flat(m_w_in), flat(v_w_in), rb=256)]
    d_out, nm_out, nv_out = [unflat(a, w_out) for a in _adamw_big(flat(w_out), flat(g_out), flat(m_w_out), flat(v_w_out), rb=SHARD_OUT)]

    hp = PACK_ROWS // N_CHIP // 2
    unpack = lambda o: jnp.concatenate([g_small[:, o:o + hp], g_small[:, 2 * hp + o:3 * hp + o]], axis=1).reshape(PACK_ROWS, GW)
    p0, p1 = unpack(0), unpack(hp)
    small = [p0[0:N_RACC], p1[0:N_RACC]]
    dpw = [p[N_RACC:N_RACC + PGD].reshape(len(POOL_WINDOWS), PGD, PGD) for p in (p0, p1)]
    o = N_RACC + PGD
    g_lng = jnp.stack([p1[o + 4:o + 8].reshape(2, D_MODEL)[0], p1[o:o + 4].reshape(2, D_MODEL)[0]])
    g_lnb = jnp.stack([p1[o + 4:o + 8].reshape(2, D_MODEL)[1], p1[o:o + 4].reshape(2, D_MODEL)[1]])
    mine = lambda a: lax.dynamic_slice_in_dim(a, chip * PGD, PGD, axis=-1)
    stack = lambda f: jnp.stack([f(0), f(1)])
    g_caw = stack(lambda l: mine(small[l][R_DWA:R_DWA + KA]))
    g_cab = stack(lambda l: small[l][R_DCAB])
    g_cbw = stack(lambda l: mine(small[l][R_DWB:R_DWB + KB]))
    g_cbb = stack(lambda l: small[l][R_DCBB])
    g_lbg = stack(lambda l: small[l][R_DLBG])
    g_lbb = stack(lambda l: small[l][R_DLBB])
    g_pw = stack(lambda l: dpw[l])
    g_pb = stack(lambda l: small[l][R_DPB].reshape(len(POOL_WINDOWS), PGD))
    g_ps = stack(lambda l: small[l][R_DPS])
    ws = [conv_a_w, conv_a_b, conv_b_w, conv_b_b, ln_b_g, ln_b_b, pool_w, pool_b, pool_scale, ln_g, ln_b]
    gs = [g_caw, g_cab, g_cbw, g_cbb, g_lbg, g_lbb, g_pw, g_pb, g_ps, g_lng, g_lnb]
    ms = [m_conv_a_w, m_conv_a_b, m_conv_b_w, m_conv_b_b, m_ln_b_g, m_ln_b_b, m_pool_w, m_pool_b, m_pool_scale, m_ln_g, m_ln_b]
    vs = [v_conv_a_w, v_conv_a_b, v_conv_b_w, v_conv_b_b, v_ln_b_g, v_ln_b_b, v_pool_w, v_pool_b, v_pool_scale, v_ln_g, v_ln_b]
    ds, nms, nvs = _adamw_small(ws, gs, ms, vs)

    loss = p1[o + 8, 0]

    def order(in_, small_, out_):
        return [in_, *small_[:9], out_, *small_[9:]]
    return (loss, gx[None], *order(g_in, gs, g_out), *order(d_in, ds, d_out), *order(nm_in, nms, nm_out), *order(nv_in, nvs, nv_out))
```

```python
import functools

import jax
import jax.numpy as jnp
import numpy as np
from jax import lax
from jax.experimental import pallas as pl
from jax.experimental.pallas import tpu as pltpu

F32 = jnp.float32
BF16 = jnp.bfloat16

D_MODEL = 1024
DEPTH = 2
GW = 512
D_IN = 9 * GW
D_MIX = 3 * GW
NG = D_IN // GW
POOL_WINDOWS = (2, 4, 8, 16)
PGD = 128
KA = 3
KB = 31
ALPHA = (2.0 * DEPTH) ** 0.25
LN_EPS = 1e-5
ADAM_LR, ADAM_B1, ADAM_B2, ADAM_EPS, ADAM_WD, ADAM_STEP = 0.001, 0.9, 0.999, 1e-08, 0.01, 10

N_CHIP = 4
SHARD_IN = D_IN // N_CHIP
SHARD_OUT = D_MIX // N_CHIP

SUBLANES = 8
RC = 32
HALO = 32
VMEM_LIMIT = 60 * 1024 * 1024
WOUT_GROUP = 4

R_DWA, R_DCAB, R_DWB, R_DCBB, R_DLBG, R_DLBB, R_DPB, R_DPS, N_RACC = 0, 3, 4, 35, 36, 37, 38, 39, 40


def _sig(v):
    return 0.5 * jnp.tanh(0.5 * v) + 0.5


def _chunks(n_rows, fn, unroll=1, extra=None):
    unroll = min(unroll, n_rows // RC)

    def step(m, carry):
        for u in range(unroll):
            fn(pl.multiple_of((m * unroll + u) * RC, RC))
        if extra is not None:
            extra(m)
        return carry
    lax.fori_loop(0, n_rows // (RC * unroll), step, 0)


def _fold8(v):
    return v.reshape(RC // SUBLANES, SUBLANES, v.shape[-1]).sum(axis=0)


def _build_shifts(ext_ref, sh_ref, shifts, n_rows, first_lane=None):
    for r in shifts:
        lanes = slice(0, ext_ref.shape[1]) if first_lane is None else slice(first_lane(r), ext_ref.shape[1])
        for c0 in range(0, n_rows, RC):
            n = min(RC, n_rows - c0)
            sh_ref[r, pl.ds(c0, n), lanes] = ext_ref[pl.ds(c0 + r, n), lanes]


def _tap(ext_ref, sh_ref, off, base, lanes=None):
    a, r = divmod(off, SUBLANES)
    src = ext_ref if r == 0 else sh_ref.at[r]
    if lanes is None:
        return src[pl.ds(base + SUBLANES * a, RC), :]
    return src[pl.ds(base + SUBLANES * a, RC), lanes]


def _ln_stats(v):
    mu = jnp.mean(v, axis=-1, keepdims=True)
    vc = v - mu
    var = jnp.mean(vc * vc, axis=-1, keepdims=True)
    rstd = lax.rsqrt(var + LN_EPS)
    return vc * rstd, rstd


def _ln_bwd(dy, xhat, rstd, g):
    dxh = dy * g
    m1 = jnp.mean(dxh, axis=-1, keepdims=True)
    m2 = jnp.mean(dxh * xhat, axis=-1, keepdims=True)
    return rstd * (dxh - m1 - xhat * m2)


def _for_taps(ext_ref, sh_ref, base, offsets, fn):
    for r in range(SUBLANES):
        offs = [o for o in offsets if o % SUBLANES == r]
        if not offs:
            continue
        a0, a1 = min(offs) // SUBLANES, max(offs) // SUBLANES
        src = ext_ref if r == 0 else sh_ref.at[r]
        win = src[pl.ds(base + SUBLANES * a0, RC + SUBLANES * (a1 - a0)), :]
        for o in offs:
            a = o // SUBLANES - a0
            fn(o, win[SUBLANES * a:SUBLANES * a + RC])


def _count_table():
    t = np.arange(1, RC + 1, dtype=np.float64)[:, None]
    w = np.repeat(np.asarray(POOL_WINDOWS, np.float64), PGD)[None, :]
    return jnp.asarray(1.0 / np.minimum(t, w), F32)


def _inv_count(cnt_ref, first):
    return jnp.where(first, cnt_ref[...], cnt_ref[RC - 1:RC, :])


def _hcol(h_ref, j, base):
    if len(h_ref.shape) == 3:
        return h_ref[j, pl.ds(base, RC), :].astype(F32)
    return h_ref[pl.ds(base, RC), j * GW:(j + 1) * GW].astype(F32)


def _with_comm(comm, ins, in_specs, out_shape, out_specs, scratch):
    if comm is None:
        return ins, in_specs, out_shape, out_specs, scratch
    hbm = pl.BlockSpec(memory_space=pl.ANY)
    return (ins + list(comm["ins"]), in_specs + [hbm] * len(comm["ins"]), out_shape + list(comm["out_shape"]),
            out_specs + [hbm] * len(comm["out_shape"]), scratch + list(comm["sems"]))


def _split_comm(refs, comm, n_in, n_out):
    refs = list(refs)
    if comm is None:
        return refs, None
    ci, co, cs = len(comm["ins"]), len(comm["out_shape"]), len(comm["sems"])
    own = refs[:n_in] + refs[n_in + ci:n_in + ci + n_out] + refs[n_in + ci + n_out + co:len(refs) - cs]
    return own, (refs[n_in:n_in + ci], refs[n_in + ci + n_out:n_in + ci + n_out + co], refs[len(refs) - cs:])


def _fwd_mixers(h_ref, cb_ref, ca_ref, y_scr, q_ext, ub_ext, cu_ext, sh, p_scr, pl_scr, prm, tt, t0):
    caw, cab, cbw, cbb, lbg, lbb, pw, pb, ps, cnt = prm

    def a1(base):
        q_ext[pl.ds(SUBLANES + base, RC), :] = _hcol(h_ref, 1, base) * _hcol(h_ref, 2, base)
    _chunks(tt, a1)
    _build_shifts(q_ext, sh, (6, 7), tt)

    def a2(base):
        ca = cab[...] + caw[0:1, :] * _tap(q_ext, sh, 6, base) + caw[1:2, :] * _tap(q_ext, sh, 7, base) \
            + caw[2:3, :] * _tap(q_ext, sh, 8, base)
        car = ca.astype(BF16)
        ca_ref[pl.ds(base, RC), :] = car
        z = _hcol(h_ref, 3, base)
        y_scr[pl.ds(base, RC), 0:GW] = (_hcol(h_ref, 0, base) * car.astype(F32) * (z * _sig(z))).astype(BF16)
    _chunks(tt, a2, unroll=2)
    q_ext[0:SUBLANES, :] = q_ext[tt:tt + SUBLANES, :]

    def b1(base):
        ub_ext[pl.ds(HALO + base, RC), :] = _hcol(h_ref, 4, base) * _sig(_hcol(h_ref, 5, base))
    _chunks(tt, b1)
    _build_shifts(ub_ext, sh, range(1, 8), tt + HALO - SUBLANES)

    def b2(base):
        cb = [cbb[...] + jnp.zeros((RC, GW), F32)]

        def tap(off, v):
            cb[0] = cb[0] + cbw[off - 2:off - 1, :] * v
        _for_taps(ub_ext, sh, base, range(2, 2 + KB), tap)
        cbr = cb[0].astype(BF16)
        cb_ref[pl.ds(base, RC), :] = cbr
        xhat, _ = _ln_stats(cbr.astype(F32))
        lnv = xhat * lbg[...] + lbb[...]
        z = _hcol(h_ref, 6, base)
        y_scr[pl.ds(base, RC), GW:2 * GW] = (lnv * _sig(lnv) * (z * _sig(z))).astype(BF16)
    _chunks(tt, b2, unroll=4)
    ub_ext[0:HALO, :] = ub_ext[tt:tt + HALO, :]

    def c1(base):
        cu_ext[pl.ds(16 + base, RC), :] = _hcol(h_ref, 7, base)
    _chunks(tt, c1)
    _build_shifts(cu_ext, sh, range(1, 8), tt + SUBLANES, first_lane=lambda r: 0 if r == 7 else PGD if r >= 5 else 2 * PGD)

    def c2(base):
        ic = _inv_count(cnt, base + t0 == 0)
        for g, w in enumerate(POOL_WINDOWS):
            lanes = slice(g * PGD, (g + 1) * PGD)
            acc = _tap(cu_ext, sh, 16, base, lanes)
            for j in range(1, w):
                acc = acc + _tap(cu_ext, sh, 16 - j, base, lanes)
            p = acc * ic[:, lanes] - _tap(cu_ext, sh, 16, base, lanes)
            p_scr[pl.ds(base, RC), lanes] = p.astype(BF16)
    _chunks(tt, c2, unroll=4)
    cu_ext[0:16, :] = cu_ext[tt:tt + 16, :]
    for g in range(len(POOL_WINDOWS)):
        lanes = slice(g * PGD, (g + 1) * PGD)
        pl_scr[:, lanes] = jnp.dot(p_scr[:, lanes], pw[g], preferred_element_type=F32)

    def c3(base):
        z = _hcol(h_ref, 8, base)
        yc0 = (pl_scr[pl.ds(base, RC), :] + pb[...]) * ps[...]
        y_scr[pl.ds(base, RC), 2 * GW:3 * GW] = (yc0 * (z * _sig(z))).astype(BF16)
    _chunks(tt, c3, unroll=2)


def _fwd_layer(x, win_b, wout_b, prm, ln_g, ln_b, target, *, tt, last, comm=None):
    t_len = x.shape[0]
    n_t = t_len // tt

    def body(*refs):
        refs, comm_refs = _split_comm(refs, comm, n_in, n_out)
        if last:
            (x_ref, win_hbm, wout_hbm, caw, cab, cbw, cbb, lbg, lbb, pw, pb, ps, cnt, lng, lnb, tgt_ref,
             h_ref, xb_ref, cb_ref, p_scr, ca_ref, dz_ref, dln_ref, loss_ref,
             win_v, wout_v, y_scr, o_scr, q_ext, ub_ext, cu_ext, sh, pl_scr, acc2, lacc) = refs
        else:
            (x_ref, win_hbm, wout_hbm, caw, cab, cbw, cbb, lbg, lbb, pw, pb, ps, cnt, lng, lnb,
             h_ref, xb_ref, cb_ref, p_scr, ca_ref, z_ref, xn_ref,
             win_v, wout_v, y_scr, o_scr, q_ext, ub_ext, cu_ext, sh, pl_scr) = refs
        i = pl.program_id(0)

        @pl.when(i == 0)
        def _():
            if comm is not None:
                comm["start"](*comm_refs)
            pltpu.sync_copy(win_hbm, win_v)
            pltpu.sync_copy(wout_hbm, wout_v)
            q_ext[0:SUBLANES, :] = jnp.zeros((SUBLANES, GW), F32)
            ub_ext[0:HALO, :] = jnp.zeros((HALO, GW), F32)
            cu_ext[0:16, :] = jnp.zeros((16, GW), F32)
            if last:
                acc2[...] = jnp.zeros_like(acc2)
                lacc[...] = jnp.zeros_like(lacc)

        xb_ref[...] = x_ref[...].astype(BF16)
        for j in range(NG):
            h_ref[:, j * GW:(j + 1) * GW] = jnp.dot(
                xb_ref[...], win_v[:, j * GW:(j + 1) * GW], preferred_element_type=F32).astype(BF16)

        _fwd_mixers(h_ref, cb_ref, ca_ref, y_scr, q_ext, ub_ext, cu_ext, sh, p_scr, pl_scr,
                    (caw, cab, cbw, cbb, lbg, lbb, pw, pb, ps, cnt), tt, i * tt)

        o_scr[...] = jnp.dot(y_scr[...], wout_v[...], preferred_element_type=F32)

        def post(base):
            rows = pl.ds(base, RC)
            z = ALPHA * x_ref[rows, :] + o_scr[rows, :]
            xhat, rstd = _ln_stats(z)
            xn = xhat * lng[...] + lnb[...]
            if last:
                err = xn - tgt_ref[rows, :]
                lacc[...] += _fold8(err * err)
                dxn = err * (1.0 / D_MODEL)
                acc2[0] += _fold8(dxn * xhat)
                acc2[1] += _fold8(dxn)
                dz_ref[rows, :] = _ln_bwd(dxn, xhat, rstd, lng[...])
            else:
                z_ref[rows, :] = z
                xn_ref[rows, :] = xn
        _chunks(tt, post, unroll=8)

        if last:
            @pl.when(i == n_t - 1)
            def _():
                dln_ref[...] = jnp.sum(acc2[...], axis=1)
                loss_ref[...] = jnp.zeros((SUBLANES, 128), F32) + (0.5 / D_MODEL) * jnp.sum(lacc[...])
        if comm is not None:
            @pl.when(i == n_t - 1)
            def _():
                comm["wait"](*comm_refs)

    tile = lambda c: pl.BlockSpec((tt, c), lambda i: (i, 0))
    full = lambda a: pl.BlockSpec(a.shape, lambda i: (0,) * a.ndim)
    hbm = pl.BlockSpec(memory_space=pl.ANY)
    ins = [x, win_b, wout_b, *prm, ln_g, ln_b] + ([target] if last else [])
    in_specs = [tile(D_MODEL), hbm, hbm] + [full(a) for a in (*prm, ln_g, ln_b)] + ([tile(D_MODEL)] if last else [])
    out_shape = [jax.ShapeDtypeStruct((t_len, D_IN), BF16), jax.ShapeDtypeStruct((t_len, D_MODEL), BF16),
                 jax.ShapeDtypeStruct((t_len, GW), BF16), jax.ShapeDtypeStruct((t_len, GW), BF16),
                 jax.ShapeDtypeStruct((t_len, GW), BF16)]
    out_specs = [tile(D_IN), tile(D_MODEL), tile(GW), tile(GW), tile(GW)]
    if last:
        out_shape += [jax.ShapeDtypeStruct((t_len, D_MODEL), F32), jax.ShapeDtypeStruct((2, D_MODEL), F32),
                      jax.ShapeDtypeStruct((SUBLANES, 128), F32)]
        out_specs += [tile(D_MODEL), pl.BlockSpec((2, D_MODEL), lambda i: (0, 0)),
                      pl.BlockSpec((SUBLANES, 128), lambda i: (0, 0))]
    else:
        out_shape += [jax.ShapeDtypeStruct((t_len, D_MODEL), F32), jax.ShapeDtypeStruct((t_len, D_MODEL), F32)]
        out_specs += [tile(D_MODEL), tile(D_MODEL)]
    scratch = [
        pltpu.VMEM((D_MODEL, D_IN), BF16), pltpu.VMEM((D_MIX, D_MODEL), BF16),
        pltpu.VMEM((tt, D_MIX), BF16), pltpu.VMEM((tt, D_MODEL), F32),
        pltpu.VMEM((tt + SUBLANES, GW), F32), pltpu.VMEM((tt + HALO, GW), F32), pltpu.VMEM((tt + 16, GW), F32),
        pltpu.VMEM((SUBLANES, tt + HALO, GW), F32),
        pltpu.VMEM((tt, GW), F32),
    ]
    if last:
        scratch += [pltpu.VMEM((2, SUBLANES, D_MODEL), F32), pltpu.VMEM((SUBLANES, D_MODEL), F32)]
    n_in, n_out = len(ins), len(out_shape)
    ins, in_specs, out_shape, out_specs, scratch = _with_comm(comm, ins, in_specs, out_shape, out_specs, scratch)
    return pl.pallas_call(
        body, name=("fwd_last" if last else "fwd_layer") + ("" if comm is None else "_comm"), grid=(n_t,),
        in_specs=in_specs, out_specs=out_specs, out_shape=out_shape, scratch_shapes=scratch,
        compiler_params=pltpu.CompilerParams(dimension_semantics=("arbitrary",), vmem_limit_bytes=VMEM_LIMIT),
    )(*ins)


def _dsilu(z, sz):
    return sz * (1.0 + z * (1.0 - sz))


def _bwd_layer(dz, h, cb, p, ca, win_b, wout_b, prm, z_prev, lng_prev, *, tt, comm=None):
    t_len = dz.shape[0]
    n_t = t_len // tt
    has_prev = z_prev is not None
    group = min(WOUT_GROUP, n_t)
    assert n_t % group == 0

    def body(*refs):
        refs, comm_refs = _split_comm(refs, comm, n_in, n_out)
        dz_ref, h_ref, cb_ref, p_scr, ca_ref, win_hbm, wout_hbm, caw, cab, cbw, cbb, lbg, lbb, pw, pb, ps, cnt = refs[:17]
        k = 17
        if has_prev:
            zp_ref, lngp = refs[k:k + 2]
            k += 2
        dxo_ref, dh_ref, dwout_hbm, small_ref, dpw_ref = refs[k:k + 5]
        k += 5
        if has_prev:
            dlnp_ref = refs[k]
            k += 1
        (win_v, wout_v, dzb_all, dy_scr, y_all, dx_scr, dca_ext, dcb_ext, dpn_ext, sh,
         pl_scr, dpl_scr, dp_scr, racc, dpw_acc, dwout_acc) = refs[k:k + 16]
        k += 16
        if has_prev:
            acc2 = refs[k]
        i = pl.program_id(0)
        ti = n_t - 1 - i
        t0 = ti * tt
        slot = i % group
        slot_rows = pl.ds(pl.multiple_of(slot * tt, tt), tt)
        dzb, y_scr = dzb_all.at[slot_rows], y_all.at[slot_rows]

        @pl.when(i == 0)
        def _():
            if comm is not None:
                comm["start"](*comm_refs)
            pltpu.sync_copy(win_hbm, win_v)
            pltpu.sync_copy(wout_hbm, wout_v)
            dca_ext[tt:tt + SUBLANES, :] = jnp.zeros((SUBLANES, GW), F32)
            dcb_ext[tt:tt + HALO, :] = jnp.zeros((HALO, GW), F32)
            dpn_ext[tt:tt + 16, :] = jnp.zeros((16, GW), F32)
            racc[...] = jnp.zeros_like(racc)
            dpw_acc[...] = jnp.zeros_like(dpw_acc)
            dwout_acc[...] = jnp.zeros_like(dwout_acc)
            if has_prev:
                acc2[...] = jnp.zeros_like(acc2)

        dzb[...] = dz_ref[...].astype(BF16)
        dy_scr[...] = lax.dot_general(dzb[...], wout_v[...], (((1,), (1,)), ((), ())), preferred_element_type=F32)

        def a2(base):
            rows = pl.ds(base, RC)
            ca = ca_ref[rows, :].astype(F32)
            bg, z = _hcol(h_ref, 0, base), _hcol(h_ref, 3, base)
            sz = _sig(z)
            sza = z * sz
            dya = dy_scr[rows, 0:GW]
            ya0 = bg * ca
            y_scr[rows, 0:GW] = (ya0 * sza).astype(BF16)
            dya0 = dya * sza
            dh_ref[rows, 3 * GW:4 * GW] = (dya * ya0 * _dsilu(z, sz)).astype(BF16)
            dh_ref[rows, 0:GW] = (dya0 * ca).astype(BF16)
            dca = dya0 * bg
            dca_ext[rows, :] = dca
            racc[R_DCAB] += _fold8(dca)
        _chunks(tt, a2)
        _build_shifts(dca_ext, sh, (1, 2), tt)

        def a3(base):
            rows = pl.ds(base, RC)
            wins = [_tap(dca_ext, sh, 2 - kk, base) for kk in range(KA)]
            cg, v = _hcol(h_ref, 1, base), _hcol(h_ref, 2, base)
            q = cg * v
            dq = caw[0:1, :] * wins[0] + caw[1:2, :] * wins[1] + caw[2:3, :] * wins[2]
            for kk in range(KA):
                racc[R_DWA + kk] += _fold8(q * wins[kk])
            dh_ref[rows, GW:2 * GW] = (dq * v).astype(BF16)
            dh_ref[rows, 2 * GW:3 * GW] = (dq * cg).astype(BF16)
        _chunks(tt, a3)
        dca_ext[tt:tt + SUBLANES, :] = dca_ext[0:SUBLANES, :]

        def b2(base):
            rows = pl.ds(base, RC)
            xhat, rstd = _ln_stats(cb_ref[rows, :].astype(F32))
            lnv = xhat * lbg[...] + lbb[...]
            sl = _sig(lnv)
            s = lnv * sl
            z = _hcol(h_ref, 6, base)
            sz = _sig(z)
            szb = z * sz
            y_scr[rows, GW:2 * GW] = (s * szb).astype(BF16)
            dyb = dy_scr[rows, GW:2 * GW]
            dh_ref[rows, 6 * GW:7 * GW] = (dyb * s * _dsilu(z, sz)).astype(BF16)
            dlnv = dyb * szb * _dsilu(lnv, sl)
            racc[R_DLBG] += _fold8(dlnv * xhat)
            racc[R_DLBB] += _fold8(dlnv)
            dcb = _ln_bwd(dlnv, xhat, rstd, lbg[...])
            dcb_ext[rows, :] = dcb
            racc[R_DCBB] += _fold8(dcb)
        _chunks(tt, b2, unroll=4)
        _build_shifts(dcb_ext, sh, range(1, 8), tt + HALO - SUBLANES)

        def b3(base):
            rows = pl.ds(base, RC)
            v, gt = _hcol(h_ref, 4, base), _hcol(h_ref, 5, base)
            sg = _sig(gt)
            ub = v * sg
            dubv = [jnp.zeros((RC, GW), F32)]

            def tap(off, win):
                dubv[0] = dubv[0] + cbw[KB - 1 - off:KB - off, :] * win
                racc[R_DWB + KB - 1 - off] += _fold8(ub * win)
            _for_taps(dcb_ext, sh, base, range(KB), tap)
            dub = dubv[0]
            dh_ref[rows, 4 * GW:5 * GW] = (dub * sg).astype(BF16)
            dh_ref[rows, 5 * GW:6 * GW] = (dub * v * sg * (1.0 - sg)).astype(BF16)
        _chunks(tt, b3, unroll=4)
        dcb_ext[tt:tt + HALO, :] = dcb_ext[0:HALO, :]

        for g in range(len(POOL_WINDOWS)):
            lanes = slice(g * PGD, (g + 1) * PGD)
            pl_scr[:, lanes] = jnp.dot(p_scr[:, lanes], pw[g], preferred_element_type=F32)

        def c3(base):
            rows = pl.ds(base, RC)
            z = _hcol(h_ref, 8, base)
            sz = _sig(z)
            szc = z * sz
            plb = pl_scr[rows, :] + pb[...]
            yc0 = plb * ps[...]
            y_scr[rows, 2 * GW:3 * GW] = (yc0 * szc).astype(BF16)
            dyc = dy_scr[rows, 2 * GW:3 * GW]
            dh_ref[rows, 8 * GW:9 * GW] = (dyc * yc0 * _dsilu(z, sz)).astype(BF16)
            dyc0 = dyc * szc
            racc[R_DPS] += _fold8(dyc0 * plb)
            dpl = dyc0 * ps[...]
            racc[R_DPB] += _fold8(dpl)
            dpl_scr[rows, :] = dpl.astype(BF16)
        _chunks(tt, c3, unroll=4)
        for g in range(len(POOL_WINDOWS)):
            lanes = slice(g * PGD, (g + 1) * PGD)
            dpw_acc[g] += lax.dot_general(p_scr[:, lanes], dpl_scr[:, lanes], (((0,), (0,)), ((), ())),
                                          preferred_element_type=F32)
            dp_scr[:, lanes] = lax.dot_general(dpl_scr[:, lanes], pw[g], (((1,), (1,)), ((), ())),
                                               preferred_element_type=F32)

        def c4(base):
            rows = pl.ds(base, RC)
            dpn_ext[rows, :] = dp_scr[rows, :] * _inv_count(cnt, base + t0 == 0)
        _chunks(tt, c4)
        _build_shifts(dpn_ext, sh, range(1, 8), tt + SUBLANES, first_lane=lambda r: 0 if r == 1 else PGD if r <= 3 else 2 * PGD)

        def c5(base):
            rows = pl.ds(base, RC)
            for g, w in enumerate(POOL_WINDOWS):
                lanes = slice(g * PGD, (g + 1) * PGD)
                acc = _tap(dpn_ext, sh, 0, base, lanes)
                for j in range(1, w):
                    acc = acc + _tap(dpn_ext, sh, j, base, lanes)
                dh_ref[rows, 7 * GW + g * PGD:7 * GW + (g + 1) * PGD] = (acc - dp_scr[rows, lanes]).astype(BF16)
        _chunks(tt, c5, unroll=4)
        dpn_ext[tt:tt + 16, :] = dpn_ext[0:16, :]

        @pl.when(slot == group - 1)
        def _():
            for r in range(D_MIX // GW):
                dwout_acc[r * GW:(r + 1) * GW, :] += lax.dot_general(
                    y_all[:, r * GW:(r + 1) * GW], dzb_all[...], (((0,), (0,)), ((), ())), preferred_element_type=F32)
        dx_scr[...] = lax.dot_general(dh_ref[...], win_v[...], (((1,), (1,)), ((), ())), preferred_element_type=F32)

        def post(base):
            rows = pl.ds(base, RC)
            dx = ALPHA * dz_ref[rows, :] + dx_scr[rows, :]
            if has_prev:
                xhat, rstd = _ln_stats(zp_ref[rows, :])
                acc2[0] += _fold8(dx * xhat)
                acc2[1] += _fold8(dx)
                dxo_ref[rows, :] = _ln_bwd(dx, xhat, rstd, lngp[...])
            else:
                dxo_ref[rows, :] = dx
        _chunks(tt, post, unroll=8)

        @pl.when(i == n_t - 1)
        def _():
            small_ref[...] = jnp.sum(racc[...], axis=1)
            dpw_ref[...] = dpw_acc[...]
            pltpu.sync_copy(dwout_acc, dwout_hbm)
            if has_prev:
                dlnp_ref[...] = jnp.sum(acc2[...], axis=1)
            if comm is not None:
                comm["wait"](*comm_refs)

    rtile = lambda c: pl.BlockSpec((tt, c), lambda i: (n_t - 1 - i, 0))
    full = lambda a: pl.BlockSpec(a.shape, lambda i: (0,) * a.ndim)
    const = lambda shp: pl.BlockSpec(shp, lambda i: (0,) * len(shp))
    hbm = pl.BlockSpec(memory_space=pl.ANY)
    ins = [dz, h, cb, p, ca, win_b, wout_b, *prm] + ([z_prev, lng_prev] if has_prev else [])
    in_specs = [rtile(D_MODEL), rtile(D_IN), rtile(GW), rtile(GW), rtile(GW), hbm, hbm] + [full(a) for a in prm] \
        + ([rtile(D_MODEL), full(lng_prev)] if has_prev else [])
    out_shape = [jax.ShapeDtypeStruct((t_len, D_MODEL), F32), jax.ShapeDtypeStruct((t_len, D_IN), BF16),
                 jax.ShapeDtypeStruct((D_MIX, D_MODEL), F32), jax.ShapeDtypeStruct((N_RACC, GW), F32),
                 jax.ShapeDtypeStruct((len(POOL_WINDOWS), PGD, PGD), F32)]
    out_specs = [rtile(D_MODEL), rtile(D_IN), hbm, const((N_RACC, GW)), const((len(POOL_WINDOWS), PGD, PGD))]
    if has_prev:
        out_shape.append(jax.ShapeDtypeStruct((2, D_MODEL), F32))
        out_specs.append(const((2, D_MODEL)))
    scratch = [
        pltpu.VMEM((D_MODEL, D_IN), BF16), pltpu.VMEM((D_MIX, D_MODEL), BF16),
        pltpu.VMEM((group * tt, D_MODEL), BF16), pltpu.VMEM((tt, D_MIX), F32), pltpu.VMEM((group * tt, D_MIX), BF16),
        pltpu.VMEM((tt, D_MODEL), F32),
        pltpu.VMEM((tt + SUBLANES, GW), F32), pltpu.VMEM((tt + HALO, GW), F32), pltpu.VMEM((tt + 16, GW), F32),
        pltpu.VMEM((SUBLANES, tt + HALO, GW), F32),
        pltpu.VMEM((tt, GW), F32), pltpu.VMEM((tt, GW), BF16), pltpu.VMEM((tt, GW), F32),
        pltpu.VMEM((N_RACC, SUBLANES, GW), F32), pltpu.VMEM((len(POOL_WINDOWS), PGD, PGD), F32),
        pltpu.VMEM((D_MIX, D_MODEL), F32),
    ]
    if has_prev:
        scratch.append(pltpu.VMEM((2, SUBLANES, D_MODEL), F32))
    n_in, n_out = len(ins), len(out_shape)
    ins, in_specs, out_shape, out_specs, scratch = _with_comm(comm, ins, in_specs, out_shape, out_specs, scratch)
    return pl.pallas_call(
        body, name=("bwd_layer_prev" if has_prev else "bwd_layer") + ("" if comm is None else "_comm"), grid=(n_t,),
        in_specs=in_specs, out_specs=out_specs, out_shape=out_shape, scratch_shapes=scratch,
        compiler_params=pltpu.CompilerParams(dimension_semantics=("arbitrary",), vmem_limit_bytes=VMEM_LIMIT),
    )(*ins)


def _wgrad_in(xb, dh, *, tk, comm=None):
    t_len = xb.shape[0]
    tk = min(tk, t_len)
    n_k = t_len // tk

    def body(*refs):
        (x_ref, dh_ref, o_ref), comm_refs = _split_comm(refs, comm, 2, 1)
        j, k = pl.program_id(0), pl.program_id(1)

        @pl.when(k == 0)
        def _():
            o_ref[...] = jnp.zeros_like(o_ref)
        if comm is not None:
            @pl.when((j == 0) & (k == 0))
            def _():
                comm["start"](*comm_refs)
        o_ref[0] += lax.dot_general(x_ref[...], dh_ref[...], (((0,), (0,)), ((), ())), preferred_element_type=F32)
        if comm is not None:
            @pl.when((j == N_CHIP - 1) & (k == n_k - 1))
            def _():
                comm["wait"](*comm_refs)

    ins = [xb, dh]
    in_specs = [pl.BlockSpec((tk, D_MODEL), lambda j, k: (k, 0)), pl.BlockSpec((tk, SHARD_IN), lambda j, k: (k, j))]
    out_shape = [jax.ShapeDtypeStruct((N_CHIP, D_MODEL, SHARD_IN), F32)]
    out_specs = [pl.BlockSpec((1, D_MODEL, SHARD_IN), lambda j, k: (j, 0, 0))]
    ins, in_specs, out_shape, out_specs, scratch = _with_comm(comm, ins, in_specs, out_shape, out_specs, [])
    outs = pl.pallas_call(
        body, name="wgrad_in" + ("" if comm is None else "_comm"), grid=(N_CHIP, n_k),
        in_specs=in_specs, out_specs=out_specs, out_shape=out_shape, scratch_shapes=scratch,
        compiler_params=pltpu.CompilerParams(dimension_semantics=("arbitrary", "arbitrary"), vmem_limit_bytes=VMEM_LIMIT),
    )(*ins)
    return outs[0] if comm is None else outs


MESH = pl.DeviceIdType.MESH
ANY = pl.BlockSpec(memory_space=pl.ANY)


def _place():
    x, y, c = lax.axis_index("x"), lax.axis_index("y"), lax.axis_index("c")
    others = [(1 - x, y), (x, 1 - y), (1 - x, 1 - y)]
    return x, y, c, 2 * x + y, [(ox, oy, 2 * ox + oy) for ox, oy in others]


def _rcopy(src, dst, send_sems, recv_sems, k, dev):
    return pltpu.make_async_remote_copy(src_ref=src, dst_ref=dst, send_sem=send_sems.at[k], recv_sem=recv_sems.at[k],
                                        device_id=dev, device_id_type=MESH)


def _gather_weights(w_in, w_out, cw):
    hi, ho = D_MODEL // 2, SHARD_OUT // 2

    def body(win_ref, wout_ref, cw_ref, owin, owout, ocw, bin_v, bout_v, send_sems, recv_sems, lsem):
        x, y, c, me, others = _place()
        for l in range(DEPTH):
            for r0 in range(0, D_MODEL, 256):
                bin_v[l, r0:r0 + 256, :] = win_ref[l, r0:r0 + 256, :].astype(BF16)
            bout_v[l] = wout_ref[l].astype(BF16)
        cin = pl.ds(pl.multiple_of(me * SHARD_IN, 128), SHARD_IN)
        rout = pl.ds(pl.multiple_of(me * SHARD_OUT, 128), SHARD_OUT)
        local = [pltpu.make_async_copy(bin_v.at[0], owin.at[:, cin], lsem.at[0]),
                 pltpu.make_async_copy(bout_v.at[0], owout.at[rout, :], lsem.at[1]),
                 pltpu.make_async_copy(cw_ref, ocw.at[me], lsem.at[2])]
        for cp in local:
            cp.start()

        def in_half(chip, core):
            return owin.at[pl.ds(pl.multiple_of(core * hi, 256), hi), pl.ds(pl.multiple_of(chip * SHARD_IN, 128), SHARD_IN)]

        def out_half(chip, core):
            return owout.at[pl.ds(pl.multiple_of(chip * SHARD_OUT + core * ho, 64), ho), :]

        first = []
        for k, (ox, oy, _) in enumerate(others):
            dev = (ox, oy, c)
            first.append(_rcopy(bin_v.at[0, pl.ds(pl.multiple_of(c * hi, 256), hi), :], in_half(me, c), send_sems, recv_sems, k, dev))
            first.append(_rcopy(bout_v.at[0, pl.ds(pl.multiple_of(c * ho, 64), ho), :], out_half(me, c), send_sems, recv_sems, 3 + k, dev))
            first.append(_rcopy(cw_ref, ocw.at[me], send_sems, recv_sems, 6 + k, dev))
        for cp in first:
            cp.start()
        sib = (x, y, 1 - c)
        passed = []
        for k, (ox, oy, oc) in enumerate(others):
            _rcopy(in_half(oc, c), in_half(oc, c), send_sems, recv_sems, k, sib).wait_recv()
            fwd_in = _rcopy(in_half(oc, c), in_half(oc, c), send_sems, recv_sems, 9 + k, sib)
            fwd_in.start()
            _rcopy(out_half(oc, c), out_half(oc, c), send_sems, recv_sems, 3 + k, sib).wait_recv()
            fwd_out = _rcopy(out_half(oc, c), out_half(oc, c), send_sems, recv_sems, 12 + k, sib)
            fwd_out.start()
            passed += [fwd_in, fwd_out]
        for k, (ox, oy, oc) in enumerate(others):
            _rcopy(cw_ref, ocw.at[oc], send_sems, recv_sems, 6 + k, sib).wait_recv()
            _rcopy(in_half(oc, 1 - c), in_half(oc, 1 - c), send_sems, recv_sems, 9 + k, sib).wait_recv()
            _rcopy(out_half(oc, 1 - c), out_half(oc, 1 - c), send_sems, recv_sems, 12 + k, sib).wait_recv()
        for cp in first + passed:
            cp.wait_send()
        for cp in local:
            cp.wait()

    vm = pl.BlockSpec(memory_space=pltpu.VMEM)
    return pl.pallas_call(
        body, name="gather_weights",
        in_specs=[vm, vm, vm], out_specs=[ANY, ANY, ANY, vm, vm],
        out_shape=[jax.ShapeDtypeStruct((D_MODEL, D_IN), BF16), jax.ShapeDtypeStruct((D_MIX, D_MODEL), BF16),
                   jax.ShapeDtypeStruct((N_CHIP,) + cw.shape, F32),
                   jax.ShapeDtypeStruct((DEPTH, D_MODEL, SHARD_IN), BF16), jax.ShapeDtypeStruct((DEPTH, SHARD_OUT, D_MODEL), BF16)],
        scratch_shapes=[pltpu.SemaphoreType.DMA((15,)), pltpu.SemaphoreType.DMA((15,)), pltpu.SemaphoreType.DMA((3,))],
        compiler_params=pltpu.CompilerParams(vmem_limit_bytes=VMEM_LIMIT),
    )(w_in, w_out, cw)


def _gather_starts(bsh_in, bsh_out, owin, owout, send_sems, recv_sems, lsem, layer):
    x, y, c, me, others = _place()
    hi, ho = D_MODEL // 2, SHARD_OUT // 2
    pltpu.make_async_copy(bsh_in.at[layer], owin.at[:, pl.ds(pl.multiple_of(me * SHARD_IN, 128), SHARD_IN)], lsem.at[0]).start()
    pltpu.make_async_copy(bsh_out.at[layer], owout.at[pl.ds(pl.multiple_of(me * SHARD_OUT, 128), SHARD_OUT), :], lsem.at[1]).start()
    for k, (ox, oy, _) in enumerate(others):
        for t in range(2):
            pltpu.make_async_remote_copy(
                src_ref=bsh_in.at[layer, pl.ds(pl.multiple_of(c * hi, 256), hi), :],
                dst_ref=owin.at[pl.ds(pl.multiple_of(c * hi, 256), hi), pl.ds(pl.multiple_of(me * SHARD_IN, 128), SHARD_IN)],
                send_sem=send_sems.at[2 * k + t], recv_sem=recv_sems.at[2 * k + c], device_id=(ox, oy, t), device_id_type=MESH).start()
            pltpu.make_async_remote_copy(
                src_ref=bsh_out.at[layer, pl.ds(pl.multiple_of(c * ho, 64), ho), :],
                dst_ref=owout.at[pl.ds(pl.multiple_of(me * SHARD_OUT + c * ho, 64), ho), :],
                send_sem=send_sems.at[6 + 2 * k + t], recv_sem=recv_sems.at[6 + 2 * k + c], device_id=(ox, oy, t), device_id_type=MESH).start()


def _gather_waits(bsh_in, bsh_out, owin, owout, send_sems, recv_sems, lsem, layer):
    x, y, c, me, others = _place()
    hi, ho = D_MODEL // 2, SHARD_OUT // 2
    src_in = bsh_in.at[layer, pl.ds(0, hi), :]
    src_out = bsh_out.at[layer, pl.ds(0, ho), :]
    for k, (ox, oy, oc) in enumerate(others):
        for t in range(2):
            dst_in = owin.at[pl.ds(t * hi, hi), pl.ds(pl.multiple_of(oc * SHARD_IN, 128), SHARD_IN)]
            dst_out = owout.at[pl.ds(pl.multiple_of(oc * SHARD_OUT + t * ho, 64), ho), :]
            a = pltpu.make_async_remote_copy(src_ref=src_in, dst_ref=dst_in, send_sem=send_sems.at[2 * k + t],
                                             recv_sem=recv_sems.at[2 * k + t], device_id=(ox, oy, t), device_id_type=MESH)
            b = pltpu.make_async_remote_copy(src_ref=src_out, dst_ref=dst_out, send_sem=send_sems.at[6 + 2 * k + t],
                                             recv_sem=recv_sems.at[6 + 2 * k + t], device_id=(ox, oy, t), device_id_type=MESH)
            a.wait_send()
            a.wait_recv()
            b.wait_send()
            b.wait_recv()
    pltpu.make_async_copy(bsh_in.at[layer], owin.at[:, pl.ds(pl.multiple_of(me * SHARD_IN, 128), SHARD_IN)], lsem.at[0]).wait()
    pltpu.make_async_copy(bsh_out.at[layer], owout.at[pl.ds(pl.multiple_of(me * SHARD_OUT, 128), SHARD_OUT), :], lsem.at[1]).wait()


def _gather_comm(bsh_in, bsh_out, layer):
    return dict(ins=[bsh_in, bsh_out],
                out_shape=[jax.ShapeDtypeStruct((D_MODEL, D_IN), BF16), jax.ShapeDtypeStruct((D_MIX, D_MODEL), BF16)],
                sems=[pltpu.SemaphoreType.DMA((12,)), pltpu.SemaphoreType.DMA((12,)), pltpu.SemaphoreType.DMA((2,))],
                start=lambda ins, outs, sems: _gather_starts(ins[0], ins[1], outs[0], outs[1], *sems, layer),
                wait=lambda ins, outs, sems: _gather_waits(ins[0], ins[1], outs[0], outs[1], *sems, layer))


def _exchange_halves(arrs, tag):
    n = len(arrs)

    def body(*refs):
        ins, outs, (send_sems, recv_sems) = refs[:n], refs[n:2 * n], refs[2 * n:]
        x, y, c, _, _ = _place()
        cps = []
        for m in range(n):
            half = ins[m].shape[1] // 2
            cps.append(_rcopy(ins[m].at[:, pl.ds(pl.multiple_of((1 - c) * half, SUBLANES), half), :], outs[m],
                              send_sems, recv_sems, m, (x, y, 1 - c)))
        for cp in cps:
            cp.start()
        for cp in cps:
            cp.wait()

    return pl.pallas_call(
        body, name="exchange_halves_" + tag, in_specs=[ANY] * n, out_specs=[ANY] * n,
        out_shape=[jax.ShapeDtypeStruct((a.shape[0], a.shape[1] // 2, a.shape[2]), F32) for a in arrs],
        scratch_shapes=[pltpu.SemaphoreType.DMA((n,)), pltpu.SemaphoreType.DMA((n,))],
    )(*arrs)


def _add_own_half(a, got, core, *, rb, dtype):
    nj, r, cdim = a.shape
    half = r // 2

    def body(core_ref, a_ref, g_ref, o_ref):
        o_ref[...] = (a_ref[0] + g_ref[...]).astype(dtype)

    return pl.pallas_call(
        body, name="add_own_half",
        grid_spec=pltpu.PrefetchScalarGridSpec(
            num_scalar_prefetch=1, grid=(nj, half // rb),
            in_specs=[pl.BlockSpec((1, 1, rb, cdim), lambda j, i, cr: (j, cr[0], i, 0)),
                      pl.BlockSpec((1, rb, cdim), lambda j, i, cr: (j, i, 0))],
            out_specs=pl.BlockSpec((1, rb, cdim), lambda j, i, cr: (j, i, 0))),
        out_shape=jax.ShapeDtypeStruct((nj, half, cdim), dtype),
    )(core, a.reshape(nj, 2, half, cdim), got)


def _owner_starts(ins, outs, sems):
    send_sems, recv_sems, lsem = sems
    x, y, c, me, others = _place()
    for m in range(len(ins)):
        pltpu.make_async_copy(ins[m].at[me], outs[m].at[me], lsem.at[m]).start()
        for k, (ox, oy, oc) in enumerate(others):
            _rcopy(ins[m].at[oc], outs[m].at[me], send_sems, recv_sems, 3 * m + k, (ox, oy, c)).start()


def _owner_waits(ins, outs, sems):
    send_sems, recv_sems, lsem = sems
    x, y, c, me, others = _place()
    for m in range(len(ins)):
        for k, (ox, oy, oc) in enumerate(others):
            _rcopy(ins[m].at[oc], outs[m].at[oc], send_sems, recv_sems, 3 * m + k, (ox, oy, c)).wait()
        pltpu.make_async_copy(ins[m].at[me], outs[m].at[me], lsem.at[m]).wait()


def _owner_comm(arrs):
    n = len(arrs)
    return dict(ins=arrs, out_shape=[jax.ShapeDtypeStruct(a.shape, a.dtype) for a in arrs],
                sems=[pltpu.SemaphoreType.DMA((3 * n,)), pltpu.SemaphoreType.DMA((3 * n,)), pltpu.SemaphoreType.DMA((n,))],
                start=_owner_starts, wait=_owner_waits)


def _send_to_owners(arrs):
    n = len(arrs)

    def body(*refs):
        ins, outs, sems = refs[:n], refs[n:2 * n], refs[2 * n:]
        _owner_starts(ins, outs, sems)
        _owner_waits(ins, outs, sems)

    job = _owner_comm(arrs)
    return pl.pallas_call(
        body, name="send_to_owners", in_specs=[ANY] * n, out_specs=[ANY] * n,
        out_shape=job["out_shape"], scratch_shapes=job["sems"],
    )(*arrs)


def _sum_chips(a, *, rb):
    nj, r, cdim = a.shape

    def body(a_ref, o_ref):
        f = lambda k: a_ref[k].astype(F32)
        o_ref[...] = ((f(0) + f(1)) + f(2)) + f(3)

    return pl.pallas_call(
        body, name="sum_chips", grid=(r // rb,),
        in_specs=[pl.BlockSpec((nj, rb, cdim), lambda i: (0, i, 0))],
        out_specs=pl.BlockSpec((rb, cdim), lambda i: (i, 0)),
        out_shape=jax.ShapeDtypeStruct((r, cdim), F32),
    )(a)


def _sum_chips_into(a, dest, layer, core, *, rb):
    nj, half, cdim = a.shape
    nb = half // rb

    def body(*refs):
        a_ref, o_ref = refs[1], refs[-1]
        f = lambda k: a_ref[k].astype(F32)
        o_ref[0] = ((f(0) + f(1)) + f(2)) + f(3)

    grid_spec = pltpu.PrefetchScalarGridSpec(
        num_scalar_prefetch=1, grid=(nb,),
        in_specs=[pl.BlockSpec((nj, rb, cdim), lambda i, cr: (0, i, 0))] + ([] if dest is None else [ANY]),
        out_specs=pl.BlockSpec((1, rb, cdim), lambda i, cr: (layer, cr[0] * nb + i, 0)))
    return pl.pallas_call(
        body, name="sum_chips_into", grid_spec=grid_spec,
        out_shape=jax.ShapeDtypeStruct((DEPTH, 2 * half, cdim), F32),
        input_output_aliases={} if dest is None else {2: 0},
    )(*([core, a] if dest is None else [core, a, dest]))


def _spread_reduced(g_in, g_out, red_small):
    hs = red_small.shape[0]

    def body(gin_in, gout_in, sm, gin, gout, fsm, gsm, send_sems, recv_sems, lsem):
        x, y, c, me, others = _place()
        sib = (x, y, 1 - c)
        hi, ho = D_MODEL // 2, SHARD_OUT // 2
        ri, ro = pl.ds(pl.multiple_of(c * hi, SUBLANES), hi), pl.ds(pl.multiple_of(c * ho, SUBLANES), ho)
        remote = [_rcopy(gin.at[:, ri, :], gin.at[:, ri, :], send_sems, recv_sems, 0, sib),
                  _rcopy(gout.at[:, ro, :], gout.at[:, ro, :], send_sems, recv_sems, 1, sib)]
        own_small = pltpu.make_async_copy(sm, gsm.at[me], lsem.at[0])
        small = [_rcopy(sm, gsm.at[me], send_sems, recv_sems, 2 + k, (ox, oy, c)) for k, (ox, oy, _) in enumerate(others)]
        for cp in remote + [own_small] + small:
            cp.start()
        own_small.wait()
        for cp in small:
            cp.wait()
        mine = fsm.at[:, pl.ds(pl.multiple_of(c * hs, SUBLANES), hs), :]
        keep = pltpu.make_async_copy(gsm, mine, lsem.at[1])
        give = _rcopy(gsm, mine, send_sems, recv_sems, 5, sib)
        keep.start()
        give.start()
        for cp in remote + [give]:
            cp.wait()
        keep.wait()

    return pl.pallas_call(
        body, name="spread_reduced", in_specs=[ANY] * 3, out_specs=[ANY] * 4,
        out_shape=[jax.ShapeDtypeStruct(g_in.shape, F32), jax.ShapeDtypeStruct(g_out.shape, F32),
                   jax.ShapeDtypeStruct((N_CHIP, 2 * hs, GW), F32), jax.ShapeDtypeStruct((N_CHIP, hs, GW), F32)],
        input_output_aliases={0: 0, 1: 1},
        scratch_shapes=[pltpu.SemaphoreType.DMA((6,)), pltpu.SemaphoreType.DMA((6,)), pltpu.SemaphoreType.DMA((2,))],
    )(g_in, g_out, red_small)[:3]


def _adamw_math(w, g, m, v):
    m = ADAM_B1 * m + (1.0 - ADAM_B1) * g
    v = ADAM_B2 * v + (1.0 - ADAM_B2) * (g * g)
    m_hat = m / (1.0 - ADAM_B1 ** ADAM_STEP)
    v_hat = v / (1.0 - ADAM_B2 ** ADAM_STEP)
    delta = -ADAM_LR * (m_hat / (jnp.sqrt(v_hat) + ADAM_EPS) + ADAM_WD * w)
    return delta, m, v


def _adamw_big(w, g, m, v, *, rb):
    r, cdim = w.shape

    def body(w_ref, g_ref, m_ref, v_ref, d_ref, nm_ref, nv_ref):
        d_ref[...], nm_ref[...], nv_ref[...] = _adamw_math(w_ref[...], g_ref[...], m_ref[...], v_ref[...])

    spec = pl.BlockSpec((rb, cdim), lambda i: (i, 0))
    return pl.pallas_call(
        body, name="adamw_big", grid=(r // rb,), in_specs=[spec] * 4, out_specs=[spec] * 3,
        out_shape=[jax.ShapeDtypeStruct((r, cdim), F32)] * 3,
    )(w, g, m, v)


def _adamw_small(ws, gs, ms, vs):
    n = len(ws)

    def body(*refs):
        w, g, m, v = refs[:n], refs[n:2 * n], refs[2 * n:3 * n], refs[3 * n:4 * n]
        d, nm, nv = refs[4 * n:5 * n], refs[5 * n:6 * n], refs[6 * n:7 * n]
        for k in range(n):
            d[k][...], nm[k][...], nv[k][...] = _adamw_math(w[k][...], g[k][...], m[k][...], v[k][...])

    shapes = [jax.ShapeDtypeStruct(a.shape, F32) for a in ws]
    outs = pl.pallas_call(body, name="adamw_small", out_shape=shapes * 3)(*ws, *gs, *ms, *vs)
    return outs[:n], outs[n:2 * n], outs[2 * n:]


TT = 256
TK = 2048
CW_ROWS = 40
PACK_ROWS = 192


def _pack(rows):
    packed = jnp.concatenate(rows, axis=0)
    packed = jnp.pad(packed, ((0, PACK_ROWS - packed.shape[0]), (0, 0)))
    return packed.reshape(N_CHIP, PACK_ROWS // N_CHIP, GW)


def _reduce_to_owner_halves(parts, core1, tag):
    got = _exchange_halves(parts, tag)
    rbs = {D_MODEL: 256, SHARD_OUT: SHARD_OUT // 2, PACK_ROWS // N_CHIP: PACK_ROWS // N_CHIP // 2}
    return [_add_own_half(a, g, core1, rb=rbs[a.shape[1]], dtype=F32 if a.shape[1] == PACK_ROWS // N_CHIP else BF16)
            for a, g in zip(parts, got)]


def kernel(x, w_in, conv_a_w, conv_a_b, conv_b_w, conv_b_b, ln_b_g, ln_b_b, pool_w, pool_b, pool_scale, w_out, ln_g, ln_b, loss_target, m_w_in, m_conv_a_w, m_conv_a_b, m_conv_b_w, m_conv_b_b, m_ln_b_g, m_ln_b_b, m_pool_w, m_pool_b, m_pool_scale, m_w_out, m_ln_g, m_ln_b, v_w_in, v_conv_a_w, v_conv_a_b, v_conv_b_w, v_conv_b_b, v_ln_b_g, v_ln_b_b, v_pool_w, v_pool_b, v_pool_scale, v_w_out, v_ln_g, v_ln_b):
    chip = 2 * lax.axis_index("x") + lax.axis_index("y")
    core1 = lax.axis_index("c").reshape(1).astype(jnp.int32)
    x2, tgt = x[0], loss_target[0]

    cw = jnp.zeros((DEPTH, CW_ROWS, PGD), F32).at[:, 0:KA].set(conv_a_w).at[:, 8:8 + KB].set(conv_b_w)
    win0_b, wout0_b, cw_all, bsh_in, bsh_out = _gather_weights(w_in, w_out, cw)
    cw_full = jnp.transpose(cw_all, (1, 2, 0, 3)).reshape(DEPTH, CW_ROWS, GW)
    row = lambda a, l: a[l].reshape(1, -1)
    cnt = _count_table()
    prm = [(cw_full[l, 0:KA], row(conv_a_b, l), cw_full[l, 8:8 + KB], row(conv_b_b, l), row(ln_b_g, l), row(ln_b_b, l),
            pool_w[l].astype(BF16), row(pool_b, l), row(pool_scale, l), cnt) for l in range(DEPTH)]

    h0, xb0, cb0, pool0, ca0, z0, x1, win1_b, wout1_b = _fwd_layer(x2, win0_b, wout0_b, prm[0], row(ln_g, 0), row(ln_b, 0), None, tt=TT, last=False,
                                                  comm=_gather_comm(bsh_in, bsh_out, 1))
    h1, xb1, cb1, pool1, ca1, dz1, dln1, loss8 = _fwd_layer(x1, win1_b, wout1_b, prm[1], row(ln_g, 1), row(ln_b, 1), tgt, tt=TT, last=True)

    dz0, dh1, dwout1, small1, dpw1, dln0 = _bwd_layer(dz1, h1, cb1, pool1, ca1, win1_b, wout1_b, prm[1], z0, row(ln_g, 0), tt=TT)
    dwin1 = _wgrad_in(xb1, dh1, tk=TK)
    loss_row = jnp.pad(loss8, ((0, 0), (0, GW - loss8.shape[1])))
    pack1 = _pack([small1, dpw1.reshape(PGD, GW), dln1.reshape(4, GW), dln0.reshape(4, GW), loss_row])
    sums1 = _reduce_to_owner_halves([dwin1, dwout1.reshape(N_CHIP, SHARD_OUT, D_MODEL), pack1], core1, "1")
    gx, dh0, dwout0, small0, dpw0 = _bwd_layer(dz0, h0, cb0, pool0, ca0, win0_b, wout0_b, prm[0], None, None, tt=TT)
    pack0 = _pack([small0, dpw0.reshape(PGD, GW)])
    sums0 = _reduce_to_owner_halves([dwout0.reshape(N_CHIP, SHARD_OUT, D_MODEL), pack0], core1, "0")
    dwin0, *landed = _wgrad_in(xb0, dh0, tk=TK, comm=_owner_comm(sums1 + sums0))
    landed1, landed0 = landed[:3], landed[3:]
    landed0 = list(_send_to_owners(_reduce_to_owner_halves([dwin0], core1, "in0"))) + list(landed0)

    g_in = _sum_chips_into(landed0[0], _sum_chips_into(landed1[0], None, 1, core1, rb=256), 0, core1, rb=256)
    g_out = _sum_chips_into(landed0[1], _sum_chips_into(landed1[1], None, 1, core1, rb=SHARD_OUT // 2), 0, core1, rb=SHARD_OUT // 2)
    red_small = jnp.concatenate([_sum_chips(a, rb=PACK_ROWS // N_CHIP // 2) for a in (landed0[2], landed1[2])], axis=0)
    g_in, g_out, g_small = _spread_reduced(g_in, g_out, red_small)

    flat = lambda a: a.reshape(-1, a.shape[-1])
    unflat = lambda a, like: a.reshape(like.shape)
    d_in, nm_in, nv_in = [unflat(a, w_in) for a in _adamw_big(flat(w_in), flat(g_in), flat(m_w_in), flat(v_w_in), rb=256)]
    d_out, nm_out, nv_out = [unflat(a, w_out) for a in _adamw_big(flat(w_out), flat(g_out), flat(m_w_out), flat(v_w_out), rb=SHARD_OUT)]

    hp = PACK_ROWS // N_CHIP // 2
    unpack = lambda o: jnp.concatenate([g_small[:, o:o + hp], g_small[:, 2 * hp + o:3 * hp + o]], axis=1).reshape(PACK_ROWS, GW)
    p0, p1 = unpack(0), unpack(hp)
    small = [p0[0:N_RACC], p1[0:N_RACC]]
    dpw = [p[N_RACC:N_RACC + PGD].reshape(len(POOL_WINDOWS), PGD, PGD) for p in (p0, p1)]
    o = N_RACC + PGD
    g_lng = jnp.stack([p1[o + 4:o + 8].reshape(2, D_MODEL)[0], p1[o:o + 4].reshape(2, D_MODEL)[0]])
    g_lnb = jnp.stack([p1[o + 4:o + 8].reshape(2, D_MODEL)[1], p1[o:o + 4].reshape(2, D_MODEL)[1]])
    mine = lambda a: lax.dynamic_slice_in_dim(a, chip * PGD, PGD, axis=-1)
    stack = lambda f: jnp.stack([f(0), f(1)])
    g_caw = stack(lambda l: mine(small[l][R_DWA:R_DWA + KA]))
    g_cab = stack(lambda l: small[l][R_DCAB])
    g_cbw = stack(lambda l: mine(small[l][R_DWB:R_DWB + KB]))
    g_cbb = stack(lambda l: small[l][R_DCBB])
    g_lbg = stack(lambda l: small[l][R_DLBG])
    g_lbb = stack(lambda l: small[l][R_DLBB])
    g_pw = stack(lambda l: dpw[l])
    g_pb = stack(lambda l: small[l][R_DPB].reshape(len(POOL_WINDOWS), PGD))
    g_ps = stack(lambda l: small[l][R_DPS])
    ws = [conv_a_w, conv_a_b, conv_b_w, conv_b_b, ln_b_g, ln_b_b, pool_w, pool_b, pool_scale, ln_g, ln_b]
    gs = [g_caw, g_cab, g_cbw, g_cbb, g_lbg, g_lbb, g_pw, g_pb, g_ps, g_lng, g_lnb]
    ms = [m_conv_a_w, m_conv_a_b, m_conv_b_w, m_conv_b_b, m_ln_b_g, m_ln_b_b, m_pool_w, m_pool_b, m_pool_scale, m_ln_g, m_ln_b]
    vs = [v_conv_a_w, v_conv_a_b, v_conv_b_w, v_conv_b_b, v_ln_b_g, v_ln_b_b, v_pool_w, v_pool_b, v_pool_scale, v_ln_g, v_ln_b]
    ds, nms, nvs = _adamw_small(ws, gs, ms, vs)

    loss = p1[o + 8, 0]

    def order(in_, small_, out_):
        return [in_, *small_[:9], out_, *small_[9:]]
    return (loss, gx[None], *order(g_in, gs, g_out), *order(d_in, ds, d_out), *order(nm_in, nms, nm_out), *order(nv_in, nvs, nv_out))
```

```python
import functools

import jax
import jax.numpy as jnp
import numpy as np
from jax import lax
from jax.experimental import pallas as pl
from jax.experimental.pallas import tpu as pltpu

F32 = jnp.float32
BF16 = jnp.bfloat16

D_MODEL = 1024
DEPTH = 2
GW = 512
D_IN = 9 * GW
D_MIX = 3 * GW
NG = D_IN // GW
POOL_WINDOWS = (2, 4, 8, 16)
PGD = 128
KA = 3
KB = 31
ALPHA = (2.0 * DEPTH) ** 0.25
LN_EPS = 1e-5
ADAM_LR, ADAM_B1, ADAM_B2, ADAM_EPS, ADAM_WD, ADAM_STEP = 0.001, 0.9, 0.999, 1e-08, 0.01, 10

N_CHIP = 4
SHARD_IN = D_IN // N_CHIP
SHARD_OUT = D_MIX // N_CHIP

SUBLANES = 8
RC = 32
HALO = 32
VMEM_LIMIT = 60 * 1024 * 1024
WOUT_GROUP = 4

R_DWA, R_DCAB, R_DWB, R_DCBB, R_DLBG, R_DLBB, R_DPB, R_DPS, N_RACC = 0, 3, 4, 35, 36, 37, 38, 39, 40


def _sig(v):
    return 0.5 * jnp.tanh(0.5 * v) + 0.5


def _chunks(n_rows, fn, unroll=1, extra=None):
    unroll = min(unroll, n_rows // RC)

    def step(m, carry):
        for u in range(unroll):
            fn(pl.multiple_of((m * unroll + u) * RC, RC))
        if extra is not None:
            extra(m)
        return carry
    lax.fori_loop(0, n_rows // (RC * unroll), step, 0)


def _fold8(v):
    return v.reshape(RC // SUBLANES, SUBLANES, v.shape[-1]).sum(axis=0)


def _build_shifts(ext_ref, sh_ref, shifts, n_rows, first_lane=None):
    for r in shifts:
        lanes = slice(0, ext_ref.shape[1]) if first_lane is None else slice(first_lane(r), ext_ref.shape[1])
        for c0 in range(0, n_rows, RC):
            n = min(RC, n_rows - c0)
            sh_ref[r, pl.ds(c0, n), lanes] = ext_ref[pl.ds(c0 + r, n), lanes]


def _tap(ext_ref, sh_ref, off, base, lanes=None):
    a, r = divmod(off, SUBLANES)
    src = ext_ref if r == 0 else sh_ref.at[r]
    if lanes is None:
        return src[pl.ds(base + SUBLANES * a, RC), :]
    return src[pl.ds(base + SUBLANES * a, RC), lanes]


def _ln_stats(v):
    mu = jnp.mean(v, axis=-1, keepdims=True)
    vc = v - mu
    var = jnp.mean(vc * vc, axis=-1, keepdims=True)
    rstd = lax.rsqrt(var + LN_EPS)
    return vc * rstd, rstd


def _ln_bwd(dy, xhat, rstd, g):
    dxh = dy * g
    m1 = jnp.mean(dxh, axis=-1, keepdims=True)
    m2 = jnp.mean(dxh * xhat, axis=-1, keepdims=True)
    return rstd * (dxh - m1 - xhat * m2)


def _for_taps(ext_ref, sh_ref, base, offsets, fn):
    for r in range(SUBLANES):
        offs = [o for o in offsets if o % SUBLANES == r]
        if not offs:
            continue
        a0, a1 = min(offs) // SUBLANES, max(offs) // SUBLANES
        src = ext_ref if r == 0 else sh_ref.at[r]
        win = src[pl.ds(base + SUBLANES * a0, RC + SUBLANES * (a1 - a0)), :]
        for o in offs:
            a = o // SUBLANES - a0
            fn(o, win[SUBLANES * a:SUBLANES * a + RC])


def _count_table():
    t = np.arange(1, RC + 1, dtype=np.float64)[:, None]
    w = np.repeat(np.asarray(POOL_WINDOWS, np.float64), PGD)[None, :]
    return jnp.asarray(1.0 / np.minimum(t, w), F32)


def _inv_count(cnt_ref, first):
    return jnp.where(first, cnt_ref[...], cnt_ref[RC - 1:RC, :])


def _hcol(h_ref, j, base):
    if len(h_ref.shape) == 3:
        return h_ref[j, pl.ds(base, RC), :].astype(F32)
    return h_ref[pl.ds(base, RC), j * GW:(j + 1) * GW].astype(F32)


def _with_comm(comm, ins, in_specs, out_shape, out_specs, scratch):
    if comm is None:
        return ins, in_specs, out_shape, out_specs, scratch
    hbm = pl.BlockSpec(memory_space=pl.ANY)
    return (ins + list(comm["ins"]), in_specs + [hbm] * len(comm["ins"]), out_shape + list(comm["out_shape"]),
            out_specs + [hbm] * len(comm["out_shape"]), scratch + list(comm["sems"]))


def _split_comm(refs, comm, n_in, n_out):
    refs = list(refs)
    if comm is None:
        return refs, None
    ci, co, cs = len(comm["ins"]), len(comm["out_shape"]), len(comm["sems"])
    own = refs[:n_in] + refs[n_in + ci:n_in + ci + n_out] + refs[n_in + ci + n_out + co:len(refs) - cs]
    return own, (refs[n_in:n_in + ci], refs[n_in + ci + n_out:n_in + ci + n_out + co], refs[len(refs) - cs:])


def _fwd_mixers(h_ref, cb_ref, ca_ref, y_scr, q_ext, ub_ext, cu_ext, sh, p_scr, pl_scr, prm, tt, t0):
    caw, cab, cbw, cbb, lbg, lbb, pw, pb, ps, cnt = prm

    def a1(base):
        q_ext[pl.ds(SUBLANES + base, RC), :] = _hcol(h_ref, 1, base) * _hcol(h_ref, 2, base)
    _chunks(tt, a1)
    _build_shifts(q_ext, sh, (6, 7), tt)

    def a2(base):
        ca = cab[...] + caw[0:1, :] * _tap(q_ext, sh, 6, base) + caw[1:2, :] * _tap(q_ext, sh, 7, base) \
            + caw[2:3, :] * _tap(q_ext, sh, 8, base)
        car = ca.astype(BF16)
        ca_ref[pl.ds(base, RC), :] = car
        z = _hcol(h_ref, 3, base)
        y_scr[pl.ds(base, RC), 0:GW] = (_hcol(h_ref, 0, base) * car.astype(F32) * (z * _sig(z))).astype(BF16)
    _chunks(tt, a2, unroll=2)
    q_ext[0:SUBLANES, :] = q_ext[tt:tt + SUBLANES, :]

    def b1(base):
        ub_ext[pl.ds(HALO + base, RC), :] = _hcol(h_ref, 4, base) * _sig(_hcol(h_ref, 5, base))
    _chunks(tt, b1)
    _build_shifts(ub_ext, sh, range(1, 8), tt + HALO - SUBLANES)

    def b2(base):
        cb = [cbb[...] + jnp.zeros((RC, GW), F32)]

        def tap(off, v):
            cb[0] = cb[0] + cbw[off - 2:off - 1, :] * v
        _for_taps(ub_ext, sh, base, range(2, 2 + KB), tap)
        cbr = cb[0].astype(BF16)
        cb_ref[pl.ds(base, RC), :] = cbr
        xhat, _ = _ln_stats(cbr.astype(F32))
        lnv = xhat * lbg[...] + lbb[...]
        z = _hcol(h_ref, 6, base)
        y_scr[pl.ds(base, RC), GW:2 * GW] = (lnv * _sig(lnv) * (z * _sig(z))).astype(BF16)
    _chunks(tt, b2, unroll=4)
    ub_ext[0:HALO, :] = ub_ext[tt:tt + HALO, :]

    def c1(base):
        cu_ext[pl.ds(16 + base, RC), :] = _hcol(h_ref, 7, base)
    _chunks(tt, c1)
    _build_shifts(cu_ext, sh, range(1, 8), tt + SUBLANES, first_lane=lambda r: 0 if r == 7 else PGD if r >= 5 else 2 * PGD)

    def c2(base):
        ic = _inv_count(cnt, base + t0 == 0)
        for g, w in enumerate(POOL_WINDOWS):
            lanes = slice(g * PGD, (g + 1) * PGD)
            acc = _tap(cu_ext, sh, 16, base, lanes)
            for j in range(1, w):
                acc = acc + _tap(cu_ext, sh, 16 - j, base, lanes)
            p = acc * ic[:, lanes] - _tap(cu_ext, sh, 16, base, lanes)
            p_scr[pl.ds(base, RC), lanes] = p.astype(BF16)
    _chunks(tt, c2, unroll=4)
    cu_ext[0:16, :] = cu_ext[tt:tt + 16, :]
    for g in range(len(POOL_WINDOWS)):
        lanes = slice(g * PGD, (g + 1) * PGD)
        pl_scr[:, lanes] = jnp.dot(p_scr[:, lanes], pw[g], preferred_element_type=F32)

    def c3(base):
        z = _hcol(h_ref, 8, base)
        yc0 = (pl_scr[pl.ds(base, RC), :] + pb[...]) * ps[...]
        y_scr[pl.ds(base, RC), 2 * GW:3 * GW] = (yc0 * (z * _sig(z))).astype(BF16)
    _chunks(tt, c3, unroll=2)


def _fwd_layer(x, win_b, wout_b, prm, ln_g, ln_b, target, *, tt, last, comm=None):
    t_len = x.shape[0]
    n_t = t_len // tt

    def body(*refs):
        refs, comm_refs = _split_comm(refs, comm, n_in, n_out)
        if last:
            (x_ref, win_hbm, wout_hbm, caw, cab, cbw, cbb, lbg, lbb, pw, pb, ps, cnt, lng, lnb, tgt_ref,
             h_ref, xb_ref, cb_ref, p_scr, ca_ref, dz_ref, dln_ref, loss_ref,
             win_v, wout_v, y_scr, o_scr, q_ext, ub_ext, cu_ext, sh, pl_scr, acc2, lacc) = refs
        else:
            (x_ref, win_hbm, wout_hbm, caw, cab, cbw, cbb, lbg, lbb, pw, pb, ps, cnt, lng, lnb,
             h_ref, xb_ref, cb_ref, p_scr, ca_ref, z_ref, xn_ref,
             win_v, wout_v, y_scr, o_scr, q_ext, ub_ext, cu_ext, sh, pl_scr) = refs
        i = pl.program_id(0)

        @pl.when(i == 0)
        def _():
            if comm is not None:
                comm["start"](*comm_refs)
            pltpu.sync_copy(win_hbm, win_v)
            pltpu.sync_copy(wout_hbm, wout_v)
            q_ext[0:SUBLANES, :] = jnp.zeros((SUBLANES, GW), F32)
            ub_ext[0:HALO, :] = jnp.zeros((HALO, GW), F32)
            cu_ext[0:16, :] = jnp.zeros((16, GW), F32)
            if last:
                acc2[...] = jnp.zeros_like(acc2)
                lacc[...] = jnp.zeros_like(lacc)

        xb_ref[...] = x_ref[...].astype(BF16)
        for j in range(NG):
            h_ref[:, j * GW:(j + 1) * GW] = jnp.dot(
                xb_ref[...], win_v[:, j * GW:(j + 1) * GW], preferred_element_type=F32).astype(BF16)

        _fwd_mixers(h_ref, cb_ref, ca_ref, y_scr, q_ext, ub_ext, cu_ext, sh, p_scr, pl_scr,
                    (caw, cab, cbw, cbb, lbg, lbb, pw, pb, ps, cnt), tt, i * tt)

        o_scr[...] = jnp.dot(y_scr[...], wout_v[...], preferred_element_type=F32)

        def post(base):
            rows = pl.ds(base, RC)
            z = ALPHA * x_ref[rows, :] + o_scr[rows, :]
            xhat, rstd = _ln_stats(z)
            xn = xhat * lng[...] + lnb[...]
            if last:
                err = xn - tgt_ref[rows, :]
                lacc[...] += _fold8(err * err)
                dxn = err * (1.0 / D_MODEL)
                acc2[0] += _fold8(dxn * xhat)
                acc2[1] += _fold8(dxn)
                dz_ref[rows, :] = _ln_bwd(dxn, xhat, rstd, lng[...])
            else:
                z_ref[rows, :] = z
                xn_ref[rows, :] = xn
        _chunks(tt, post, unroll=8)

        if last:
            @pl.when(i == n_t - 1)
            def _():
                dln_ref[...] = jnp.sum(acc2[...], axis=1)
                loss_ref[...] = jnp.zeros((SUBLANES, 128), F32) + (0.5 / D_MODEL) * jnp.sum(lacc[...])
        if comm is not None:
            @pl.when(i == n_t - 1)
            def _():
                comm["wait"](*comm_refs)

    tile = lambda c: pl.BlockSpec((tt, c), lambda i: (i, 0))
    full = lambda a: pl.BlockSpec(a.shape, lambda i: (0,) * a.ndim)
    hbm = pl.BlockSpec(memory_space=pl.ANY)
    ins = [x, win_b, wout_b, *prm, ln_g, ln_b] + ([target] if last else [])
    in_specs = [tile(D_MODEL), hbm, hbm] + [full(a) for a in (*prm, ln_g, ln_b)] + ([tile(D_MODEL)] if last else [])
    out_shape = [jax.ShapeDtypeStruct((t_len, D_IN), BF16), jax.ShapeDtypeStruct((t_len, D_MODEL), BF16),
                 jax.ShapeDtypeStruct((t_len, GW), BF16), jax.ShapeDtypeStruct((t_len, GW), BF16),
                 jax.ShapeDtypeStruct((t_len, GW), BF16)]
    out_specs = [tile(D_IN), tile(D_MODEL), tile(GW), tile(GW), tile(GW)]
    if last:
        out_shape += [jax.ShapeDtypeStruct((t_len, D_MODEL), F32), jax.ShapeDtypeStruct((2, D_MODEL), F32),
                      jax.ShapeDtypeStruct((SUBLANES, 128), F32)]
        out_specs += [tile(D_MODEL), pl.BlockSpec((2, D_MODEL), lambda i: (0, 0)),
                      pl.BlockSpec((SUBLANES, 128), lambda i: (0, 0))]
    else:
        out_shape += [jax.ShapeDtypeStruct((t_len, D_MODEL), F32), jax.ShapeDtypeStruct((t_len, D_MODEL), F32)]
        out_specs += [tile(D_MODEL), tile(D_MODEL)]
    scratch = [
        pltpu.VMEM((D_MODEL, D_IN), BF16), pltpu.VMEM((D_MIX, D_MODEL), BF16),
        pltpu.VMEM((tt, D_MIX), BF16), pltpu.VMEM((tt, D_MODEL), F32),
        pltpu.VMEM((tt + SUBLANES, GW), F32), pltpu.VMEM((tt + HALO, GW), F32), pltpu.VMEM((tt + 16, GW), F32),
        pltpu.VMEM((SUBLANES, tt + HALO, GW), F32),
        pltpu.VMEM((tt, GW), F32),
    ]
    if last:
        scratch += [pltpu.VMEM((2, SUBLANES, D_MODEL), F32), pltpu.VMEM((SUBLANES, D_MODEL), F32)]
    n_in, n_out = len(ins), len(out_shape)
    ins, in_specs, out_shape, out_specs, scratch = _with_comm(comm, ins, in_specs, out_shape, out_specs, scratch)
    return pl.pallas_call(
        body, name=("fwd_last" if last else "fwd_layer") + ("" if comm is None else "_comm"), grid=(n_t,),
        in_specs=in_specs, out_specs=out_specs, out_shape=out_shape, scratch_shapes=scratch,
        compiler_params=pltpu.CompilerParams(dimension_semantics=("arbitrary",), vmem_limit_bytes=VMEM_LIMIT),
    )(*ins)


def _dsilu(z, sz):
    return sz * (1.0 + z * (1.0 - sz))


def _bwd_layer(dz, h, cb, p, ca, win_b, wout_b, prm, z_prev, lng_prev, *, tt, comm=None):
    t_len = dz.shape[0]
    n_t = t_len // tt
    has_prev = z_prev is not None
    group = min(WOUT_GROUP, n_t)
    assert n_t % group == 0

    def body(*refs):
        refs, comm_refs = _split_comm(refs, comm, n_in, n_out)
        dz_ref, h_ref, cb_ref, p_scr, ca_ref, win_hbm, wout_hbm, caw, cab, cbw, cbb, lbg, lbb, pw, pb, ps, cnt = refs[:17]
        k = 17
        if has_prev:
            zp_ref, lngp = refs[k:k + 2]
            k += 2
        dxo_ref, dh_ref, dwout_hbm, small_ref, dpw_ref = refs[k:k + 5]
        k += 5
        if has_prev:
            dlnp_ref = refs[k]
            k += 1
        (win_v, wout_v, dzb_all, dy_scr, y_all, dx_scr, dca_ext, dcb_ext, dpn_ext, sh,
         pl_scr, dpl_scr, dp_scr, racc, dpw_acc, dwout_acc) = refs[k:k + 16]
        k += 16
        if has_prev:
            acc2 = refs[k]
        i = pl.program_id(0)
        ti = n_t - 1 - i
        t0 = ti * tt
        slot = i % group
        slot_rows = pl.ds(pl.multiple_of(slot * tt, tt), tt)
        dzb, y_scr = dzb_all.at[slot_rows], y_all.at[slot_rows]

        @pl.when(i == 0)
        def _():
            if comm is not None:
                comm["start"](*comm_refs)
            pltpu.sync_copy(win_hbm, win_v)
            pltpu.sync_copy(wout_hbm, wout_v)
            dca_ext[tt:tt + SUBLANES, :] = jnp.zeros((SUBLANES, GW), F32)
            dcb_ext[tt:tt + HALO, :] = jnp.zeros((HALO, GW), F32)
            dpn_ext[tt:tt + 16, :] = jnp.zeros((16, GW), F32)
            racc[...] = jnp.zeros_like(racc)
            dpw_acc[...] = jnp.zeros_like(dpw_acc)
            dwout_acc[...] = jnp.zeros_like(dwout_acc)
            if has_prev:
                acc2[...] = jnp.zeros_like(acc2)

        dzb[...] = dz_ref[...].astype(BF16)
        dy_scr[...] = lax.dot_general(dzb[...], wout_v[...], (((1,), (1,)), ((), ())), preferred_element_type=F32)

        def a2(base):
            rows = pl.ds(base, RC)
            ca = ca_ref[rows, :].astype(F32)
            bg, z = _hcol(h_ref, 0, base), _hcol(h_ref, 3, base)
            sz = _sig(z)
            sza = z * sz
            dya = dy_scr[rows, 0:GW]
            ya0 = bg * ca
            y_scr[rows, 0:GW] = (ya0 * sza).astype(BF16)
            dya0 = dya * sza
            dh_ref[rows, 3 * GW:4 * GW] = (dya * ya0 * _dsilu(z, sz)).astype(BF16)
            dh_ref[rows, 0:GW] = (dya0 * ca).astype(BF16)
            dca = dya0 * bg
            dca_ext[rows, :] = dca
            racc[R_DCAB] += _fold8(dca)
        _chunks(tt, a2)
        _build_shifts(dca_ext, sh, (1, 2), tt)

        def a3(base):
            rows = pl.ds(base, RC)
            wins = [_tap(dca_ext, sh, 2 - kk, base) for kk in range(KA)]
            cg, v = _hcol(h_ref, 1, base), _hcol(h_ref, 2, base)
            q = cg * v
            dq = caw[0:1, :] * wins[0] + caw[1:2, :] * wins[1] + caw[2:3, :] * wins[2]
            for kk in range(KA):
                racc[R_DWA + kk] += _fold8(q * wins[kk])
            dh_ref[rows, GW:2 * GW] = (dq * v).astype(BF16)
            dh_ref[rows, 2 * GW:3 * GW] = (dq * cg).astype(BF16)
        _chunks(tt, a3)
        dca_ext[tt:tt + SUBLANES, :] = dca_ext[0:SUBLANES, :]

        def b2(base):
            rows = pl.ds(base, RC)
            xhat, rstd = _ln_stats(cb_ref[rows, :].astype(F32))
            lnv = xhat * lbg[...] + lbb[...]
            sl = _sig(lnv)
            s = lnv * sl
            z = _hcol(h_ref, 6, base)
            sz = _sig(z)
            szb = z * sz
            y_scr[rows, GW:2 * GW] = (s * szb).astype(BF16)
            dyb = dy_scr[rows, GW:2 * GW]
            dh_ref[rows, 6 * GW:7 * GW] = (dyb * s * _dsilu(z, sz)).astype(BF16)
            dlnv = dyb * szb * _dsilu(lnv, sl)
            racc[R_DLBG] += _fold8(dlnv * xhat)
            racc[R_DLBB] += _fold8(dlnv)
            dcb = _ln_bwd(dlnv, xhat, rstd, lbg[...])
            dcb_ext[rows, :] = dcb
            racc[R_DCBB] += _fold8(dcb)
        _chunks(tt, b2, unroll=4)
        _build_shifts(dcb_ext, sh, range(1, 8), tt + HALO - SUBLANES)

        def b3(base):
            rows = pl.ds(base, RC)
            v, gt = _hcol(h_ref, 4, base), _hcol(h_ref, 5, base)
            sg = _sig(gt)
            ub = v * sg
            dubv = [jnp.zeros((RC, GW), F32)]

            def tap(off, win):
                dubv[0] = dubv[0] + cbw[KB - 1 - off:KB - off, :] * win
                racc[R_DWB + KB - 1 - off] += _fold8(ub * win)
            _for_taps(dcb_ext, sh, base, range(KB), tap)
            dub = dubv[0]
            dh_ref[rows, 4 * GW:5 * GW] = (dub * sg).astype(BF16)
            dh_ref[rows, 5 * GW:6 * GW] = (dub * v * sg * (1.0 - sg)).astype(BF16)
        _chunks(tt, b3, unroll=4)
        dcb_ext[tt:tt + HALO, :] = dcb_ext[0:HALO, :]

        for g in range(len(POOL_WINDOWS)):
            lanes = slice(g * PGD, (g + 1) * PGD)
            pl_scr[:, lanes] = jnp.dot(p_scr[:, lanes], pw[g], preferred_element_type=F32)

        def c3(base):
            rows = pl.ds(base, RC)
            z = _hcol(h_ref, 8, base)
            sz = _sig(z)
            szc = z * sz
            plb = pl_scr[rows, :] + pb[...]
            yc0 = plb * ps[...]
            y_scr[rows, 2 * GW:3 * GW] = (yc0 * szc).astype(BF16)
            dyc = dy_scr[rows, 2 * GW:3 * GW]
            dh_ref[rows, 8 * GW:9 * GW] = (dyc * yc0 * _dsilu(z, sz)).astype(BF16)
            dyc0 = dyc * szc
            racc[R_DPS] += _fold8(dyc0 * plb)
            dpl = dyc0 * ps[...]
            racc[R_DPB] += _fold8(dpl)
            dpl_scr[rows, :] = dpl.astype(BF16)
        _chunks(tt, c3, unroll=4)
        for g in range(len(POOL_WINDOWS)):
            lanes = slice(g * PGD, (g + 1) * PGD)
            dpw_acc[g] += lax.dot_general(p_scr[:, lanes], dpl_scr[:, lanes], (((0,), (0,)), ((), ())),
                                          preferred_element_type=F32)
            dp_scr[:, lanes] = lax.dot_general(dpl_scr[:, lanes], pw[g], (((1,), (1,)), ((), ())),
                                               preferred_element_type=F32)

        def c4(base):
            rows = pl.ds(base, RC)
            dpn_ext[rows, :] = dp_scr[rows, :] * _inv_count(cnt, base + t0 == 0)
        _chunks(tt, c4)
        _build_shifts(dpn_ext, sh, range(1, 8), tt + SUBLANES, first_lane=lambda r: 0 if r == 1 else PGD if r <= 3 else 2 * PGD)

        def c5(base):
            rows = pl.ds(base, RC)
            for g, w in enumerate(POOL_WINDOWS):
                lanes = slice(g * PGD, (g + 1) * PGD)
                acc = _tap(dpn_ext, sh, 0, base, lanes)
                for j in range(1, w):
                    acc = acc + _tap(dpn_ext, sh, j, base, lanes)
                dh_ref[rows, 7 * GW + g * PGD:7 * GW + (g + 1) * PGD] = (acc - dp_scr[rows, lanes]).astype(BF16)
        _chunks(tt, c5, unroll=4)
        dpn_ext[tt:tt + 16, :] = dpn_ext[0:16, :]

        @pl.when(slot == group - 1)
        def _():
            for r in range(D_MIX // GW):
                dwout_acc[r * GW:(r + 1) * GW, :] += lax.dot_general(
                    y_all[:, r * GW:(r + 1) * GW], dzb_all[...], (((0,), (0,)), ((), ())), preferred_element_type=F32)
        dx_scr[...] = lax.dot_general(dh_ref[...], win_v[...], (((1,), (1,)), ((), ())), preferred_element_type=F32)

        def post(base):
            rows = pl.ds(base, RC)
            dx = ALPHA * dz_ref[rows, :] + dx_scr[rows, :]
            if has_prev:
                xhat, rstd = _ln_stats(zp_ref[rows, :])
                acc2[0] += _fold8(dx * xhat)
                acc2[1] += _fold8(dx)
                dxo_ref[rows, :] = _ln_bwd(dx, xhat, rstd, lngp[...])
            else:
                dxo_ref[rows, :] = dx
        _chunks(tt, post, unroll=8)

        @pl.when(i == n_t - 1)
        def _():
            small_ref[...] = jnp.sum(racc[...], axis=1)
            dpw_ref[...] = dpw_acc[...]
            pltpu.sync_copy(dwout_acc, dwout_hbm)
            if has_prev:
                dlnp_ref[...] = jnp.sum(acc2[...], axis=1)
            if comm is not None:
                comm["wait"](*comm_refs)

    rtile = lambda c: pl.BlockSpec((tt, c), lambda i: (n_t - 1 - i, 0))
    full = lambda a: pl.BlockSpec(a.shape, lambda i: (0,) * a.ndim)
    const = lambda shp: pl.BlockSpec(shp, lambda i: (0,) * len(shp))
    hbm = pl.BlockSpec(memory_space=pl.ANY)
    ins = [dz, h, cb, p, ca, win_b, wout_b, *prm] + ([z_prev, lng_prev] if has_prev else [])
    in_specs = [rtile(D_MODEL), rtile(D_IN), rtile(GW), rtile(GW), rtile(GW), hbm, hbm] + [full(a) for a in prm] \
        + ([rtile(D_MODEL), full(lng_prev)] if has_prev else [])
    out_shape = [jax.ShapeDtypeStruct((t_len, D_MODEL), F32), jax.ShapeDtypeStruct((t_len, D_IN), BF16),
                 jax.ShapeDtypeStruct((D_MIX, D_MODEL), F32), jax.ShapeDtypeStruct((N_RACC, GW), F32),
                 jax.ShapeDtypeStruct((len(POOL_WINDOWS), PGD, PGD), F32)]
    out_specs = [rtile(D_MODEL), rtile(D_IN), hbm, const((N_RACC, GW)), const((len(POOL_WINDOWS), PGD, PGD))]
    if has_prev:
        out_shape.append(jax.ShapeDtypeStruct((2, D_MODEL), F32))
        out_specs.append(const((2, D_MODEL)))
    scratch = [
        pltpu.VMEM((D_MODEL, D_IN), BF16), pltpu.VMEM((D_MIX, D_MODEL), BF16),
        pltpu.VMEM((group * tt, D_MODEL), BF16), pltpu.VMEM((tt, D_MIX), F32), pltpu.VMEM((group * tt, D_MIX), BF16),
        pltpu.VMEM((tt, D_MODEL), F32),
        pltpu.VMEM((tt + SUBLANES, GW), F32), pltpu.VMEM((tt + HALO, GW), F32), pltpu.VMEM((tt + 16, GW), F32),
        pltpu.VMEM((SUBLANES, tt + HALO, GW), F32),
        pltpu.VMEM((tt, GW), F32), pltpu.VMEM((tt, GW), BF16), pltpu.VMEM((tt, GW), F32),
        pltpu.VMEM((N_RACC, SUBLANES, GW), F32), pltpu.VMEM((len(POOL_WINDOWS), PGD, PGD), F32),
        pltpu.VMEM((D_MIX, D_MODEL), F32),
    ]
    if has_prev:
        scratch.append(pltpu.VMEM((2, SUBLANES, D_MODEL), F32))
    n_in, n_out = len(ins), len(out_shape)
    ins, in_specs, out_shape, out_specs, scratch = _with_comm(comm, ins, in_specs, out_shape, out_specs, scratch)
    return pl.pallas_call(
        body, name=("bwd_layer_prev" if has_prev else "bwd_layer") + ("" if comm is None else "_comm"), grid=(n_t,),
        in_specs=in_specs, out_specs=out_specs, out_shape=out_shape, scratch_shapes=scratch,
        compiler_params=pltpu.CompilerParams(dimension_semantics=("arbitrary",), vmem_limit_bytes=VMEM_LIMIT),
    )(*ins)


def _wgrad_in(xb, dh, *, tk, comm=None):
    t_len = xb.shape[0]
    tk = min(tk, t_len)
    n_k = t_len // tk

    def body(*refs):
        (x_ref, dh_ref, o_ref), comm_refs = _split_comm(refs, comm, 2, 1)
        j, k = pl.program_id(0), pl.program_id(1)

        @pl.when(k == 0)
        def _():
            o_ref[...] = jnp.zeros_like(o_ref)
        if comm is not None:
            @pl.when((j == 0) & (k == 0))
            def _():
                comm["start"](*comm_refs)
        o_ref[0] += lax.dot_general(x_ref[...], dh_ref[...], (((0,), (0,)), ((), ())), preferred_element_type=F32)
        if comm is not None:
            @pl.when((j == N_CHIP - 1) & (k == n_k - 1))
            def _():
                comm["wait"](*comm_refs)

    ins = [xb, dh]
    in_specs = [pl.BlockSpec((tk, D_MODEL), lambda j, k: (k, 0)), pl.BlockSpec((tk, SHARD_IN), lambda j, k: (k, j))]
    out_shape = [jax.ShapeDtypeStruct((N_CHIP, D_MODEL, SHARD_IN), F32)]
    out_specs = [pl.BlockSpec((1, D_MODEL, SHARD_IN), lambda j, k: (j, 0, 0))]
    ins, in_specs, out_shape, out_specs, scratch = _with_comm(comm, ins, in_specs, out_shape, out_specs, [])
    outs = pl.pallas_call(
        body, name="wgrad_in" + ("" if comm is None else "_comm"), grid=(N_CHIP, n_k),
        in_specs=in_specs, out_specs=out_specs, out_shape=out_shape, scratch_shapes=scratch,
        compiler_params=pltpu.CompilerParams(dimension_semantics=("arbitrary", "arbitrary"), vmem_limit_bytes=VMEM_LIMIT),
    )(*ins)
    return outs[0] if comm is None else outs


MESH = pl.DeviceIdType.MESH
ANY = pl.BlockSpec(memory_space=pl.ANY)


def _place():
    x, y, c = lax.axis_index("x"), lax.axis_index("y"), lax.axis_index("c")
    others = [(1 - x, y), (x, 1 - y), (1 - x, 1 - y)]
    return x, y, c, 2 * x + y, [(ox, oy, 2 * ox + oy) for ox, oy in others]


def _rcopy(src, dst, send_sems, recv_sems, k, dev):
    return pltpu.make_async_remote_copy(src_ref=src, dst_ref=dst, send_sem=send_sems.at[k], recv_sem=recv_sems.at[k],
                                        device_id=dev, device_id_type=MESH)


def _gather_weights(w_in, w_out, cw):
    hi, ho = D_MODEL // 2, SHARD_OUT // 2

    def body(win_ref, wout_ref, cw_ref, owin, owout, ocw, bin_v, bout_v, send_sems, recv_sems, lsem):
        x, y, c, me, others = _place()
        for l in range(DEPTH):
            for r0 in range(0, D_MODEL, 256):
                bin_v[l, r0:r0 + 256, :] = win_ref[l, r0:r0 + 256, :].astype(BF16)
            bout_v[l] = wout_ref[l].astype(BF16)
        cin = pl.ds(pl.multiple_of(me * SHARD_IN, 128), SHARD_IN)
        rout = pl.ds(pl.multiple_of(me * SHARD_OUT, 128), SHARD_OUT)
        local = [pltpu.make_async_copy(bin_v.at[0], owin.at[:, cin], lsem.at[0]),
                 pltpu.make_async_copy(bout_v.at[0], owout.at[rout, :], lsem.at[1]),
                 pltpu.make_async_copy(cw_ref, ocw.at[me], lsem.at[2])]
        for cp in local:
            cp.start()

        def in_half(chip, core):
            return owin.at[pl.ds(pl.multiple_of(core * hi, 256), hi), pl.ds(pl.multiple_of(chip * SHARD_IN, 128), SHARD_IN)]

        def out_half(chip, core):
            return owout.at[pl.ds(pl.multiple_of(chip * SHARD_OUT + core * ho, 64), ho), :]

        first = []
        for k, (ox, oy, _) in enumerate(others):
            dev = (ox, oy, c)
            first.append(_rcopy(bin_v.at[0, pl.ds(pl.multiple_of(c * hi, 256), hi), :], in_half(me, c), send_sems, recv_sems, k, dev))
            first.append(_rcopy(bout_v.at[0, pl.ds(pl.multiple_of(c * ho, 64), ho), :], out_half(me, c), send_sems, recv_sems, 3 + k, dev))
            first.append(_rcopy(cw_ref, ocw.at[me], send_sems, recv_sems, 6 + k, dev))
        for cp in first:
            cp.start()
        sib = (x, y, 1 - c)
        passed = []
        for k, (ox, oy, oc) in enumerate(others):
            _rcopy(in_half(oc, c), in_half(oc, c), send_sems, recv_sems, k, sib).wait_recv()
            fwd_in = _rcopy(in_half(oc, c), in_half(oc, c), send_sems, recv_sems, 9 + k, sib)
            fwd_in.start()
            _rcopy(out_half(oc, c), out_half(oc, c), send_sems, recv_sems, 3 + k, sib).wait_recv()
            fwd_out = _rcopy(out_half(oc, c), out_half(oc, c), send_sems, recv_sems, 12 + k, sib)
            fwd_out.start()
            passed += [fwd_in, fwd_out]
        for k, (ox, oy, oc) in enumerate(others):
            _rcopy(cw_ref, ocw.at[oc], send_sems, recv_sems, 6 + k, sib).wait_recv()
            _rcopy(in_half(oc, 1 - c), in_half(oc, 1 - c), send_sems, recv_sems, 9 + k, sib).wait_recv()
            _rcopy(out_half(oc, 1 - c), out_half(oc, 1 - c), send_sems, recv_sems, 12 + k, sib).wait_recv()
        for cp in first + passed:
            cp.wait_send()
        for cp in local:
            cp.wait()

    vm = pl.BlockSpec(memory_space=pltpu.VMEM)
    return pl.pallas_call(
        body, name="gather_weights",
        in_specs=[vm, vm, vm], out_specs=[ANY, ANY, ANY, vm, vm],
        out_shape=[jax.ShapeDtypeStruct((D_MODEL, D_IN), BF16), jax.ShapeDtypeStruct((D_MIX, D_MODEL), BF16),
                   jax.ShapeDtypeStruct((N_CHIP,) + cw.shape, F32),
                   jax.ShapeDtypeStruct((DEPTH, D_MODEL, SHARD_IN), BF16), jax.ShapeDtypeStruct((DEPTH, SHARD_OUT, D_MODEL), BF16)],
        scratch_shapes=[pltpu.SemaphoreType.DMA((15,)), pltpu.SemaphoreType.DMA((15,)), pltpu.SemaphoreType.DMA((3,))],
        compiler_params=pltpu.CompilerParams(vmem_limit_bytes=VMEM_LIMIT),
    )(w_in, w_out, cw)


def _gather_starts(bsh_in, bsh_out, owin, owout, send_sems, recv_sems, lsem, layer):
    x, y, c, me, others = _place()
    hi, ho = D_MODEL // 2, SHARD_OUT // 2
    pltpu.make_async_copy(bsh_in.at[layer], owin.at[:, pl.ds(pl.multiple_of(me * SHARD_IN, 128), SHARD_IN)], lsem.at[0]).start()
    pltpu.make_async_copy(bsh_out.at[layer], owout.at[pl.ds(pl.multiple_of(me * SHARD_OUT, 128), SHARD_OUT), :], lsem.at[1]).start()
    for k, (ox, oy, _) in enumerate(others):
        for t in range(2):
            pltpu.make_async_remote_copy(
                src_ref=bsh_in.at[layer, pl.ds(pl.multiple_of(c * hi, 256), hi), :],
                dst_ref=owin.at[pl.ds(pl.multiple_of(c * hi, 256), hi), pl.ds(pl.multiple_of(me * SHARD_IN, 128), SHARD_IN)],
                send_sem=send_sems.at[2 * k + t], recv_sem=recv_sems.at[2 * k + c], device_id=(ox, oy, t), device_id_type=MESH).start()
            pltpu.make_async_remote_copy(
                src_ref=bsh_out.at[layer, pl.ds(pl.multiple_of(c * ho, 64), ho), :],
                dst_ref=owout.at[pl.ds(pl.multiple_of(me * SHARD_OUT + c * ho, 64), ho), :],
                send_sem=send_sems.at[6 + 2 * k + t], recv_sem=recv_sems.at[6 + 2 * k + c], device_id=(ox, oy, t), device_id_type=MESH).start()


def _gather_waits(bsh_in, bsh_out, owin, owout, send_sems, recv_sems, lsem, layer):
    x, y, c, me, others = _place()
    hi, ho = D_MODEL // 2, SHARD_OUT // 2
    src_in = bsh_in.at[layer, pl.ds(0, hi), :]
    src_out = bsh_out.at[layer, pl.ds(0, ho), :]
    for k, (ox, oy, oc) in enumerate(others):
        for t in range(2):
            dst_in = owin.at[pl.ds(t * hi, hi), pl.ds(pl.multiple_of(oc * SHARD_IN, 128), SHARD_IN)]
            dst_out = owout.at[pl.ds(pl.multiple_of(oc * SHARD_OUT + t * ho, 64), ho), :]
            a = pltpu.make_async_remote_copy(src_ref=src_in, dst_ref=dst_in, send_sem=send_sems.at[2 * k + t],
                                             recv_sem=recv_sems.at[2 * k + t], device_id=(ox, oy, t), device_id_type=MESH)
            b = pltpu.make_async_remote_copy(src_ref=src_out, dst_ref=dst_out, send_sem=send_sems.at[6 + 2 * k + t],
                                             recv_sem=recv_sems.at[6 + 2 * k + t], device_id=(ox, oy, t), device_id_type=MESH)
            a.wait_send()
            a.wait_recv()
            b.wait_send()
            b.wait_recv()
    pltpu.make_async_copy(bsh_in.at[layer], owin.at[:, pl.ds(pl.multiple_of(me * SHARD_IN, 128), SHARD_IN)], lsem.at[0]).wait()
    pltpu.make_async_copy(bsh_out.at[layer], owout.at[pl.ds(pl.multiple_of(me * SHARD_OUT, 128), SHARD_OUT), :], lsem.at[1]).wait()


def _gather_comm(bsh_in, bsh_out, layer):
    return dict(ins=[bsh_in, bsh_out],
                out_shape=[jax.ShapeDtypeStruct((D_MODEL, D_IN), BF16), jax.ShapeDtypeStruct((D_MIX, D_MODEL), BF16)],
                sems=[pltpu.SemaphoreType.DMA((12,)), pltpu.SemaphoreType.DMA((12,)), pltpu.SemaphoreType.DMA((2,))],
                start=lambda ins, outs, sems: _gather_starts(ins[0], ins[1], outs[0], outs[1], *sems, layer),
                wait=lambda ins, outs, sems: _gather_waits(ins[0], ins[1], outs[0], outs[1], *sems, layer))


def _exchange_halves(arrs, tag):
    n = len(arrs)

    def body(*refs):
        ins, outs, (send_sems, recv_sems) = refs[:n], refs[n:2 * n], refs[2 * n:]
        x, y, c, _, _ = _place()
        cps = []
        for m in range(n):
            half = ins[m].shape[1] // 2
            cps.append(_rcopy(ins[m].at[:, pl.ds(pl.multiple_of((1 - c) * half, SUBLANES), half), :], outs[m],
                              send_sems, recv_sems, m, (x, y, 1 - c)))
        for cp in cps:
            cp.start()
        for cp in cps:
            cp.wait()

    return pl.pallas_call(
        body, name="exchange_halves_" + tag, in_specs=[ANY] * n, out_specs=[ANY] * n,
        out_shape=[jax.ShapeDtypeStruct((a.shape[0], a.shape[1] // 2, a.shape[2]), F32) for a in arrs],
        scratch_shapes=[pltpu.SemaphoreType.DMA((n,)), pltpu.SemaphoreType.DMA((n,))],
    )(*arrs)


def _add_own_half(a, got, core, *, rb, dtype):
    nj, r, cdim = a.shape
    half = r // 2

    def body(core_ref, a_ref, g_ref, o_ref):
        o_ref[...] = (a_ref[0] + g_ref[...]).astype(dtype)

    return pl.pallas_call(
        body, name="add_own_half",
        grid_spec=pltpu.PrefetchScalarGridSpec(
            num_scalar_prefetch=1, grid=(nj, half // rb),
            in_specs=[pl.BlockSpec((1, 1, rb, cdim), lambda j, i, cr: (j, cr[0], i, 0)),
                      pl.BlockSpec((1, rb, cdim), lambda j, i, cr: (j, i, 0))],
            out_specs=pl.BlockSpec((1, rb, cdim), lambda j, i, cr: (j, i, 0))),
        out_shape=jax.ShapeDtypeStruct((nj, half, cdim), dtype),
    )(core, a.reshape(nj, 2, half, cdim), got)


def _owner_starts(ins, outs, sems):
    send_sems, recv_sems, lsem = sems
    x, y, c, me, others = _place()
    for m in range(len(ins)):
        pltpu.make_async_copy(ins[m].at[me], outs[m].at[me], lsem.at[m]).start()
        for k, (ox, oy, oc) in enumerate(others):
            _rcopy(ins[m].at[oc], outs[m].at[me], send_sems, recv_sems, 3 * m + k, (ox, oy, c)).start()


def _owner_waits(ins, outs, sems):
    send_sems, recv_sems, lsem = sems
    x, y, c, me, others = _place()
    for m in range(len(ins)):
        for k, (ox, oy, oc) in enumerate(others):
            _rcopy(ins[m].at[oc], outs[m].at[oc], send_sems, recv_sems, 3 * m + k, (ox, oy, c)).wait()
        pltpu.make_async_copy(ins[m].at[me], outs[m].at[me], lsem.at[m]).wait()


def _owner_comm(arrs):
    n = len(arrs)
    return dict(ins=arrs, out_shape=[jax.ShapeDtypeStruct(a.shape, a.dtype) for a in arrs],
                sems=[pltpu.SemaphoreType.DMA((3 * n,)), pltpu.SemaphoreType.DMA((3 * n,)), pltpu.SemaphoreType.DMA((n,))],
                start=_owner_starts, wait=_owner_waits)


def _send_to_owners(arrs):
    n = len(arrs)

    def body(*refs):
        ins, outs, sems = refs[:n], refs[n:2 * n], refs[2 * n:]
        _owner_starts(ins, outs, sems)
        _owner_waits(ins, outs, sems)

    job = _owner_comm(arrs)
    return pl.pallas_call(
        body, name="send_to_owners", in_specs=[ANY] * n, out_specs=[ANY] * n,
        out_shape=job["out_shape"], scratch_shapes=job["sems"],
    )(*arrs)


def _sum_chips(a, *, rb):
    nj, r, cdim = a.shape

    def body(a_ref, o_ref):
        f = lambda k: a_ref[k].astype(F32)
        o_ref[...] = ((f(0) + f(1)) + f(2)) + f(3)

    return pl.pallas_call(
        body, name="sum_chips", grid=(r // rb,),
        in_specs=[pl.BlockSpec((nj, rb, cdim), lambda i: (0, i, 0))],
        out_specs=pl.BlockSpec((rb, cdim), lambda i: (i, 0)),
        out_shape=jax.ShapeDtypeStruct((r, cdim), F32),
    )(a)


def _sum_chips_into(a, dest, layer, core, *, rb):
    nj, half, cdim = a.shape
    nb = half // rb

    def body(*refs):
        a_ref, o_ref = refs[1], refs[-1]
        f = lambda k: a_ref[k].astype(F32)
        o_ref[0] = ((f(0) + f(1)) + f(2)) + f(3)

    grid_spec = pltpu.PrefetchScalarGridSpec(
        num_scalar_prefetch=1, grid=(nb,),
        in_specs=[pl.BlockSpec((nj, rb, cdim), lambda i, cr: (0, i, 0))] + ([] if dest is None else [ANY]),
        out_specs=pl.BlockSpec((1, rb, cdim), lambda i, cr: (layer, cr[0] * nb + i, 0)))
    return pl.pallas_call(
        body, name="sum_chips_into", grid_spec=grid_spec,
        out_shape=jax.ShapeDtypeStruct((DEPTH, 2 * half, cdim), F32),
        input_output_aliases={} if dest is None else {2: 0},
    )(*([core, a] if dest is None else [core, a, dest]))


def _spread_reduced(g_in, g_out, red_small):
    hs = red_small.shape[0]

    def body(gin_in, gout_in, sm, gin, gout, fsm, gsm, send_sems, recv_sems, lsem):
        x, y, c, me, others = _place()
        sib = (x, y, 1 - c)
        hi, ho = D_MODEL // 2, SHARD_OUT // 2
        ri, ro = pl.ds(pl.multiple_of(c * hi, SUBLANES), hi), pl.ds(pl.multiple_of(c * ho, SUBLANES), ho)
        remote = [_rcopy(gin.at[:, ri, :], gin.at[:, ri, :], send_sems, recv_sems, 0, sib),
                  _rcopy(gout.at[:, ro, :], gout.at[:, ro, :], send_sems, recv_sems, 1, sib)]
        own_small = pltpu.make_async_copy(sm, gsm.at[me], lsem.at[0])
        small = [_rcopy(sm, gsm.at[me], send_sems, recv_sems, 2 + k, (ox, oy, c)) for k, (ox, oy, _) in enumerate(others)]
        for cp in remote + [own_small] + small:
            cp.start()
        own_small.wait()
        for cp in small:
            cp.wait()
        mine = fsm.at[:, pl.ds(pl.multiple_of(c * hs, SUBLANES), hs), :]
        keep = pltpu.make_async_copy(gsm, mine, lsem.at[1])
        give = _rcopy(gsm, mine, send_sems, recv_sems, 5, sib)
        keep.start()
        give.start()
        for cp in remote + [give]:
            cp.wait()
        keep.wait()

    return pl.pallas_call(
        body, name="spread_reduced", in_specs=[ANY] * 3, out_specs=[ANY] * 4,
        out_shape=[jax.ShapeDtypeStruct(g_in.shape, F32), jax.ShapeDtypeStruct(g_out.shape, F32),
                   jax.ShapeDtypeStruct((N_CHIP, 2 * hs, GW), F32), jax.ShapeDtypeStruct((N_CHIP, hs, GW), F32)],
        input_output_aliases={0: 0, 1: 1},
        scratch_shapes=[pltpu.SemaphoreType.DMA((6,)), pltpu.SemaphoreType.DMA((6,)), pltpu.SemaphoreType.DMA((2,))],
    )(g_in, g_out, red_small)[:3]


def _adamw_math(w, g, m, v):
    m = ADAM_B1 * m + (1.0 - ADAM_B1) * g
    v = ADAM_B2 * v + (1.0 - ADAM_B2) * (g * g)
    m_hat = m / (1.0 - ADAM_B1 ** ADAM_STEP)
    v_hat = v / (1.0 - ADAM_B2 ** ADAM_STEP)
    delta = -ADAM_LR * (m_hat / (jnp.sqrt(v_hat) + ADAM_EPS) + ADAM_WD * w)
    return delta, m, v


def _adamw_big(w, g, m, v, *, rb):
    r, cdim = w.shape

    def body(w_ref, g_ref, m_ref, v_ref, d_ref, nm_ref, nv_ref, go_ref):
        g = g_ref[...]
        d_ref[...], nm_ref[...], nv_ref[...] = _adamw_math(w_ref[...], g, m_ref[...], v_ref[...])
        go_ref[...] = g

    spec = pl.BlockSpec((rb, cdim), lambda i: (i, 0))
    return pl.pallas_call(
        body, name="adamw_big", grid=(r // rb,), in_specs=[spec] * 4, out_specs=[spec] * 4,
        out_shape=[jax.ShapeDtypeStruct((r, cdim), F32)] * 4,
    )(w, g, m, v)


def _adamw_small(ws, gs, ms, vs):
    n = len(ws)

    def body(*refs):
        w, g, m, v = refs[:n], refs[n:2 * n], refs[2 * n:3 * n], refs[3 * n:4 * n]
        d, nm, nv = refs[4 * n:5 * n], refs[5 * n:6 * n], refs[6 * n:7 * n]
        for k in range(n):
            d[k][...], nm[k][...], nv[k][...] = _adamw_math(w[k][...], g[k][...], m[k][...], v[k][...])

    shapes = [jax.ShapeDtypeStruct(a.shape, F32) for a in ws]
    outs = pl.pallas_call(body, name="adamw_small", out_shape=shapes * 3)(*ws, *gs, *ms, *vs)
    return outs[:n], outs[n:2 * n], outs[2 * n:]


TT = 256
TK = 4096
CW_ROWS = 40
PACK_ROWS = 192


def _pack(rows):
    packed = jnp.concatenate(rows, axis=0)
    packed = jnp.pad(packed, ((0, PACK_ROWS - packed.shape[0]), (0, 0)))
    return packed.reshape(N_CHIP, PACK_ROWS // N_CHIP, GW)


def _reduce_to_owner_halves(parts, core1, tag):
    got = _exchange_halves(parts, tag)
    rbs = {D_MODEL: 256, SHARD_OUT: SHARD_OUT // 2, PACK_ROWS // N_CHIP: PACK_ROWS // N_CHIP // 2}
    return [_add_own_half(a, g, core1, rb=rbs[a.shape[1]], dtype=F32 if a.shape[1] == PACK_ROWS // N_CHIP else BF16)
            for a, g in zip(parts, got)]


def kernel(x, w_in, conv_a_w, conv_a_b, conv_b_w, conv_b_b, ln_b_g, ln_b_b, pool_w, pool_b, pool_scale, w_out, ln_g, ln_b, loss_target, m_w_in, m_conv_a_w, m_conv_a_b, m_conv_b_w, m_conv_b_b, m_ln_b_g, m_ln_b_b, m_pool_w, m_pool_b, m_pool_scale, m_w_out, m_ln_g, m_ln_b, v_w_in, v_conv_a_w, v_conv_a_b, v_conv_b_w, v_conv_b_b, v_ln_b_g, v_ln_b_b, v_pool_w, v_pool_b, v_pool_scale, v_w_out, v_ln_g, v_ln_b):
    chip = 2 * lax.axis_index("x") + lax.axis_index("y")
    core1 = lax.axis_index("c").reshape(1).astype(jnp.int32)
    x2, tgt = x[0], loss_target[0]

    cw = jnp.zeros((DEPTH, CW_ROWS, PGD), F32).at[:, 0:KA].set(conv_a_w).at[:, 8:8 + KB].set(conv_b_w)
    win0_b, wout0_b, cw_all, bsh_in, bsh_out = _gather_weights(w_in, w_out, cw)
    cw_full = jnp.transpose(cw_all, (1, 2, 0, 3)).reshape(DEPTH, CW_ROWS, GW)
    row = lambda a, l: a[l].reshape(1, -1)
    cnt = _count_table()
    prm = [(cw_full[l, 0:KA], row(conv_a_b, l), cw_full[l, 8:8 + KB], row(conv_b_b, l), row(ln_b_g, l), row(ln_b_b, l),
            pool_w[l].astype(BF16), row(pool_b, l), row(pool_scale, l), cnt) for l in range(DEPTH)]

    h0, xb0, cb0, pool0, ca0, z0, x1, win1_b, wout1_b = _fwd_layer(x2, win0_b, wout0_b, prm[0], row(ln_g, 0), row(ln_b, 0), None, tt=TT, last=False,
                                                  comm=_gather_comm(bsh_in, bsh_out, 1))
    h1, xb1, cb1, pool1, ca1, dz1, dln1, loss8 = _fwd_layer(x1, win1_b, wout1_b, prm[1], row(ln_g, 1), row(ln_b, 1), tgt, tt=TT, last=True)

    dz0, dh1, dwout1, small1, dpw1, dln0 = _bwd_layer(dz1, h1, cb1, pool1, ca1, win1_b, wout1_b, prm[1], z0, row(ln_g, 0), tt=TT)
    dwin1 = _wgrad_in(xb1, dh1, tk=TK)
    loss_row = jnp.pad(loss8, ((0, 0), (0, GW - loss8.shape[1])))
    pack1 = _pack([small1, dpw1.reshape(PGD, GW), dln1.reshape(4, GW), dln0.reshape(4, GW), loss_row])
    sums1 = _reduce_to_owner_halves([dwin1, dwout1.reshape(N_CHIP, SHARD_OUT, D_MODEL), pack1], core1, "1")
    gx, dh0, dwout0, small0, dpw0 = _bwd_layer(dz0, h0, cb0, pool0, ca0, win0_b, wout0_b, prm[0], None, None, tt=TT)
    pack0 = _pack([small0, dpw0.reshape(PGD, GW)])
    sums0 = _reduce_to_owner_halves([dwout0.reshape(N_CHIP, SHARD_OUT, D_MODEL), pack0], core1, "0")
    dwin0, *landed = _wgrad_in(xb0, dh0, tk=TK, comm=_owner_comm(sums1 + sums0))
    landed1, landed0 = landed[:3], landed[3:]
    landed0 = list(_send_to_owners(_reduce_to_owner_halves([dwin0], core1, "in0"))) + list(landed0)

    g_in = _sum_chips_into(landed0[0], _sum_chips_into(landed1[0], None, 1, core1, rb=256), 0, core1, rb=256)
    g_out = _sum_chips_into(landed0[1], _sum_chips_into(landed1[1], None, 1, core1, rb=SHARD_OUT // 2), 0, core1, rb=SHARD_OUT // 2)
    red_small = jnp.concatenate([_sum_chips(a, rb=PACK_ROWS // N_CHIP // 2) for a in (landed0[2], landed1[2])], axis=0)
    g_in, g_out, g_small = _spread_reduced(g_in, g_out, red_small)

    flat = lambda a: a.reshape(-1, a.shape[-1])
    unflat = lambda a, like: a.reshape(like.shape)
    d_in, nm_in, nv_in, g_in = [unflat(a, w_in) for a in _adamw_big(flat(w_in), flat(g_in), flat(m_w_in), flat(v_w_in), rb=256)]
    d_out, nm_out, nv_out, g_out = [unflat(a, w_out) for a in _adamw_big(flat(w_out), flat(g_out), flat(m_w_out), flat(v_w_out), rb=SHARD_OUT)]

    hp = PACK_ROWS // N_CHIP // 2
    unpack = lambda o: jnp.concatenate([g_small[:, o:o + hp], g_small[:, 2 * hp + o:3 * hp + o]], axis=1).reshape(PACK_ROWS, GW)
    p0, p1 = unpack(0), unpack(hp)
    small = [p0[0:N_RACC], p1[0:N_RACC]]
    dpw = [p[N_RACC:N_RACC + PGD].reshape(len(POOL_WINDOWS), PGD, PGD) for p in (p0, p1)]
    o = N_RACC + PGD
    g_lng = jnp.stack([p1[o + 4:o + 8].reshape(2, D_MODEL)[0], p1[o:o + 4].reshape(2, D_MODEL)[0]])
    g_lnb = jnp.stack([p1[o + 4:o + 8].reshape(2, D_MODEL)[1], p1[o:o + 4].reshape(2, D_MODEL)[1]])
    mine = lambda a: lax.dynamic_slice_in_dim(a, chip * PGD, PGD, axis=-1)
    stack = lambda f: jnp.stack([f(0), f(1)])
    g_caw = stack(lambda l: mine(small[l][R_DWA:R_DWA + KA]))
    g_cab = stack(lambda l: small[l][R_DCAB])
    g_cbw = stack(lambda l: mine(small[l][R_DWB:R_DWB + KB]))
    g_cbb = stack(lambda l: small[l][R_DCBB])
    g_lbg = stack(lambda l: small[l][R_DLBG])
    g_lbb = stack(lambda l: small[l][R_DLBB])
    g_pw = stack(lambda l: dpw[l])
    g_pb = stack(lambda l: small[l][R_DPB].reshape(len(POOL_WINDOWS), PGD))
    g_ps = stack(lambda l: small[l][R_DPS])
    ws = [conv_a_w, conv_a_b, conv_b_w, conv_b_b, ln_b_g, ln_b_b, pool_w, pool_b, pool_scale, ln_g, ln_b]
    gs = [g_caw, g_cab, g_cbw, g_cbb, g_lbg, g_lbb, g_pw, g_pb, g_ps, g_lng, g_lnb]
    ms = [m_conv_a_w, m_conv_a_b, m_conv_b_w, m_conv_b_b, m_ln_b_g, m_ln_b_b, m_pool_w, m_pool_b, m_pool_scale, m_ln_g, m_ln_b]
    vs = [v_conv_a_w, v_conv_a_b, v_conv_b_w, v_conv_b_b, v_ln_b_g, v_ln_b_b, v_pool_w, v_pool_b, v_pool_scale, v_ln_g, v_ln_b]
    ds, nms, nvs = _adamw_small(ws, gs, ms, vs)

    loss = p1[o + 8, 0]

    def order(in_, small_, out_):
        return [in_, *small_[:9], out_, *small_[9:]]
    return (loss, gx[None], *order(g_in, gs, g_out), *order(d_in, ds, d_out), *order(nm_in, nms, nm_out), *order(nv_in, nvs, nv_out))
```

```python
import functools

import jax
import jax.numpy as jnp
import numpy as np
from jax import lax
from jax.experimental import pallas as pl
from jax.experimental.pallas import tpu as pltpu

F32 = jnp.float32
BF16 = jnp.bfloat16

D_MODEL = 1024
DEPTH = 2
GW = 512
D_IN = 9 * GW
D_MIX = 3 * GW
NG = D_IN // GW
POOL_WINDOWS = (2, 4, 8, 16)
PGD = 128
KA = 3
KB = 31
ALPHA = (2.0 * DEPTH) ** 0.25
LN_EPS = 1e-5
ADAM_LR, ADAM_B1, ADAM_B2, ADAM_EPS, ADAM_WD, ADAM_STEP = 0.001, 0.9, 0.999, 1e-08, 0.01, 10

N_CHIP = 4
SHARD_IN = D_IN // N_CHIP
SHARD_OUT = D_MIX // N_CHIP

SUBLANES = 8
RC = 32
HALO = 32
VMEM_LIMIT = 60 * 1024 * 1024
WOUT_GROUP = 4

R_DWA, R_DCAB, R_DWB, R_DCBB, R_DLBG, R_DLBB, R_DPB, R_DPS, N_RACC = 0, 3, 4, 35, 36, 37, 38, 39, 40


def _sig(v):
    return 0.5 * jnp.tanh(0.5 * v) + 0.5


def _chunks(n_rows, fn, unroll=1, extra=None):
    unroll = min(unroll, n_rows // RC)

    def step(m, carry):
        for u in range(unroll):
            fn(pl.multiple_of((m * unroll + u) * RC, RC))
        if extra is not None:
            extra(m)
        return carry
    lax.fori_loop(0, n_rows // (RC * unroll), step, 0)


def _fold8(v):
    return v.reshape(RC // SUBLANES, SUBLANES, v.shape[-1]).sum(axis=0)


def _build_shifts(ext_ref, sh_ref, shifts, n_rows, first_lane=None):
    for r in shifts:
        lanes = slice(0, ext_ref.shape[1]) if first_lane is None else slice(first_lane(r), ext_ref.shape[1])
        for c0 in range(0, n_rows, RC):
            n = min(RC, n_rows - c0)
            sh_ref[r, pl.ds(c0, n), lanes] = ext_ref[pl.ds(c0 + r, n), lanes]


def _tap(ext_ref, sh_ref, off, base, lanes=None):
    a, r = divmod(off, SUBLANES)
    src = ext_ref if r == 0 else sh_ref.at[r]
    if lanes is None:
        return src[pl.ds(base + SUBLANES * a, RC), :]
    return src[pl.ds(base + SUBLANES * a, RC), lanes]


def _ln_stats(v):
    mu = jnp.mean(v, axis=-1, keepdims=True)
    vc = v - mu
    var = jnp.mean(vc * vc, axis=-1, keepdims=True)
    rstd = lax.rsqrt(var + LN_EPS)
    return vc * rstd, rstd


def _ln_bwd(dy, xhat, rstd, g):
    dxh = dy * g
    m1 = jnp.mean(dxh, axis=-1, keepdims=True)
    m2 = jnp.mean(dxh * xhat, axis=-1, keepdims=True)
    return rstd * (dxh - m1 - xhat * m2)


def _for_taps(ext_ref, sh_ref, base, offsets, fn):
    for r in range(SUBLANES):
        offs = [o for o in offsets if o % SUBLANES == r]
        if not offs:
            continue
        a0, a1 = min(offs) // SUBLANES, max(offs) // SUBLANES
        src = ext_ref if r == 0 else sh_ref.at[r]
        win = src[pl.ds(base + SUBLANES * a0, RC + SUBLANES * (a1 - a0)), :]
        for o in offs:
            a = o // SUBLANES - a0
            fn(o, win[SUBLANES * a:SUBLANES * a + RC])


def _count_table():
    t = np.arange(1, RC + 1, dtype=np.float64)[:, None]
    w = np.repeat(np.asarray(POOL_WINDOWS, np.float64), PGD)[None, :]
    return jnp.asarray(1.0 / np.minimum(t, w), F32)


def _inv_count(cnt_ref, first):
    return jnp.where(first, cnt_ref[...], cnt_ref[RC - 1:RC, :])


def _hcol(h_ref, j, base):
    if len(h_ref.shape) == 3:
        return h_ref[j, pl.ds(base, RC), :].astype(F32)
    return h_ref[pl.ds(base, RC), j * GW:(j + 1) * GW].astype(F32)


def _with_comm(comm, ins, in_specs, out_shape, out_specs, scratch):
    if comm is None:
        return ins, in_specs, out_shape, out_specs, scratch
    hbm = pl.BlockSpec(memory_space=pl.ANY)
    return (ins + list(comm["ins"]), in_specs + [hbm] * len(comm["ins"]), out_shape + list(comm["out_shape"]),
            out_specs + [hbm] * len(comm["out_shape"]), scratch + list(comm["sems"]))


def _split_comm(refs, comm, n_in, n_out):
    refs = list(refs)
    if comm is None:
        return refs, None
    ci, co, cs = len(comm["ins"]), len(comm["out_shape"]), len(comm["sems"])
    own = refs[:n_in] + refs[n_in + ci:n_in + ci + n_out] + refs[n_in + ci + n_out + co:len(refs) - cs]
    return own, (refs[n_in:n_in + ci], refs[n_in + ci + n_out:n_in + ci + n_out + co], refs[len(refs) - cs:])


def _fwd_mixers(h_ref, cb_ref, ca_ref, y_scr, q_ext, ub_ext, cu_ext, sh, p_scr, pl_scr, prm, tt, t0):
    caw, cab, cbw, cbb, lbg, lbb, pw, pb, ps, cnt = prm

    def a1(base):
        q_ext[pl.ds(SUBLANES + base, RC), :] = _hcol(h_ref, 1, base) * _hcol(h_ref, 2, base)
    _chunks(tt, a1)
    _build_shifts(q_ext, sh, (6, 7), tt)

    def a2(base):
        ca = cab[...] + caw[0:1, :] * _tap(q_ext, sh, 6, base) + caw[1:2, :] * _tap(q_ext, sh, 7, base) \
            + caw[2:3, :] * _tap(q_ext, sh, 8, base)
        car = ca.astype(BF16)
        ca_ref[pl.ds(base, RC), :] = car
        z = _hcol(h_ref, 3, base)
        y_scr[pl.ds(base, RC), 0:GW] = (_hcol(h_ref, 0, base) * car.astype(F32) * (z * _sig(z))).astype(BF16)
    _chunks(tt, a2, unroll=2)
    q_ext[0:SUBLANES, :] = q_ext[tt:tt + SUBLANES, :]

    def b1(base):
        ub_ext[pl.ds(HALO + base, RC), :] = _hcol(h_ref, 4, base) * _sig(_hcol(h_ref, 5, base))
    _chunks(tt, b1)
    _build_shifts(ub_ext, sh, range(1, 8), tt + HALO - SUBLANES)

    def b2(base):
        cb = [cbb[...] + jnp.zeros((RC, GW), F32)]

        def tap(off, v):
            cb[0] = cb[0] + cbw[off - 2:off - 1, :] * v
        _for_taps(ub_ext, sh, base, range(2, 2 + KB), tap)
        cbr = cb[0].astype(BF16)
        cb_ref[pl.ds(base, RC), :] = cbr
        xhat, _ = _ln_stats(cbr.astype(F32))
        lnv = xhat * lbg[...] + lbb[...]
        z = _hcol(h_ref, 6, base)
        y_scr[pl.ds(base, RC), GW:2 * GW] = (lnv * _sig(lnv) * (z * _sig(z))).astype(BF16)
    _chunks(tt, b2, unroll=4)
    ub_ext[0:HALO, :] = ub_ext[tt:tt + HALO, :]

    def c1(base):
        cu_ext[pl.ds(16 + base, RC), :] = _hcol(h_ref, 7, base)
    _chunks(tt, c1)
    _build_shifts(cu_ext, sh, range(1, 8), tt + SUBLANES, first_lane=lambda r: 0 if r == 7 else PGD if r >= 5 else 2 * PGD)

    def c2(base):
        ic = _inv_count(cnt, base + t0 == 0)
        for g, w in enumerate(POOL_WINDOWS):
            lanes = slice(g * PGD, (g + 1) * PGD)
            acc = _tap(cu_ext, sh, 16, base, lanes)
            for j in range(1, w):
                acc = acc + _tap(cu_ext, sh, 16 - j, base, lanes)
            p = acc * ic[:, lanes] - _tap(cu_ext, sh, 16, base, lanes)
            p_scr[pl.ds(base, RC), lanes] = p.astype(BF16)
    _chunks(tt, c2, unroll=4)
    cu_ext[0:16, :] = cu_ext[tt:tt + 16, :]
    for g in range(len(POOL_WINDOWS)):
        lanes = slice(g * PGD, (g + 1) * PGD)
        pl_scr[:, lanes] = jnp.dot(p_scr[:, lanes], pw[g], preferred_element_type=F32)

    def c3(base):
        z = _hcol(h_ref, 8, base)
        yc0 = (pl_scr[pl.ds(base, RC), :] + pb[...]) * ps[...]
        y_scr[pl.ds(base, RC), 2 * GW:3 * GW] = (yc0 * (z * _sig(z))).astype(BF16)
    _chunks(tt, c3, unroll=2)


def _fwd_layer(x, win_b, wout_b, prm, ln_g, ln_b, target, *, tt, last, comm=None):
    t_len = x.shape[0]
    n_t = t_len // tt

    def body(*refs):
        refs, comm_refs = _split_comm(refs, comm, n_in, n_out)
        if last:
            (x_ref, win_hbm, wout_hbm, caw, cab, cbw, cbb, lbg, lbb, pw, pb, ps, cnt, lng, lnb, tgt_ref,
             h_ref, xb_ref, cb_ref, p_scr, ca_ref, dz_ref, dln_ref, loss_ref,
             win_v, wout_v, y_scr, o_scr, q_ext, ub_ext, cu_ext, sh, pl_scr, acc2, lacc) = refs
        else:
            (x_ref, win_hbm, wout_hbm, caw, cab, cbw, cbb, lbg, lbb, pw, pb, ps, cnt, lng, lnb,
             h_ref, xb_ref, cb_ref, p_scr, ca_ref, z_ref, xn_ref,
             win_v, wout_v, y_scr, o_scr, q_ext, ub_ext, cu_ext, sh, pl_scr) = refs
        i = pl.program_id(0)

        @pl.when(i == 0)
        def _():
            if comm is not None:
                comm["start"](*comm_refs)
            pltpu.sync_copy(win_hbm, win_v)
            pltpu.sync_copy(wout_hbm, wout_v)
            q_ext[0:SUBLANES, :] = jnp.zeros((SUBLANES, GW), F32)
            ub_ext[0:HALO, :] = jnp.zeros((HALO, GW), F32)
            cu_ext[0:16, :] = jnp.zeros((16, GW), F32)
            if last:
                acc2[...] = jnp.zeros_like(acc2)
                lacc[...] = jnp.zeros_like(lacc)

        xb_ref[...] = x_ref[...].astype(BF16)
        for j in range(NG):
            h_ref[:, j * GW:(j + 1) * GW] = jnp.dot(
                xb_ref[...], win_v[:, j * GW:(j + 1) * GW], preferred_element_type=F32).astype(BF16)

        _fwd_mixers(h_ref, cb_ref, ca_ref, y_scr, q_ext, ub_ext, cu_ext, sh, p_scr, pl_scr,
                    (caw, cab, cbw, cbb, lbg, lbb, pw, pb, ps, cnt), tt, i * tt)

        o_scr[...] = jnp.dot(y_scr[...], wout_v[...], preferred_element_type=F32)

        def post(base):
            rows = pl.ds(base, RC)
            z = ALPHA * x_ref[rows, :] + o_scr[rows, :]
            xhat, rstd = _ln_stats(z)
            xn = xhat * lng[...] + lnb[...]
            if last:
                err = xn - tgt_ref[rows, :]
                lacc[...] += _fold8(err * err)
                dxn = err * (1.0 / D_MODEL)
                acc2[0] += _fold8(dxn * xhat)
                acc2[1] += _fold8(dxn)
                dz_ref[rows, :] = _ln_bwd(dxn, xhat, rstd, lng[...])
            else:
                z_ref[rows, :] = z
                xn_ref[rows, :] = xn
        _chunks(tt, post, unroll=8)

        if last:
            @pl.when(i == n_t - 1)
            def _():
                dln_ref[...] = jnp.sum(acc2[...], axis=1)
                loss_ref[...] = jnp.zeros((SUBLANES, 128), F32) + (0.5 / D_MODEL) * jnp.sum(lacc[...])
        if comm is not None:
            @pl.when(i == n_t - 1)
            def _():
                comm["wait"](*comm_refs)

    tile = lambda c: pl.BlockSpec((tt, c), lambda i: (i, 0))
    full = lambda a: pl.BlockSpec(a.shape, lambda i: (0,) * a.ndim)
    hbm = pl.BlockSpec(memory_space=pl.ANY)
    ins = [x, win_b, wout_b, *prm, ln_g, ln_b] + ([target] if last else [])
    in_specs = [tile(D_MODEL), hbm, hbm] + [full(a) for a in (*prm, ln_g, ln_b)] + ([tile(D_MODEL)] if last else [])
    out_shape = [jax.ShapeDtypeStruct((t_len, D_IN), BF16), jax.ShapeDtypeStruct((t_len, D_MODEL), BF16),
                 jax.ShapeDtypeStruct((t_len, GW), BF16), jax.ShapeDtypeStruct((t_len, GW), BF16),
                 jax.ShapeDtypeStruct((t_len, GW), BF16)]
    out_specs = [tile(D_IN), tile(D_MODEL), tile(GW), tile(GW), tile(GW)]
    if last:
        out_shape += [jax.ShapeDtypeStruct((t_len, D_MODEL), F32), jax.ShapeDtypeStruct((2, D_MODEL), F32),
                      jax.ShapeDtypeStruct((SUBLANES, 128), F32)]
        out_specs += [tile(D_MODEL), pl.BlockSpec((2, D_MODEL), lambda i: (0, 0)),
                      pl.BlockSpec((SUBLANES, 128), lambda i: (0, 0))]
    else:
        out_shape += [jax.ShapeDtypeStruct((t_len, D_MODEL), F32), jax.ShapeDtypeStruct((t_len, D_MODEL), F32)]
        out_specs += [tile(D_MODEL), tile(D_MODEL)]
    scratch = [
        pltpu.VMEM((D_MODEL, D_IN), BF16), pltpu.VMEM((D_MIX, D_MODEL), BF16),
        pltpu.VMEM((tt, D_MIX), BF16), pltpu.VMEM((tt, D_MODEL), F32),
        pltpu.VMEM((tt + SUBLANES, GW), F32), pltpu.VMEM((tt + HALO, GW), F32), pltpu.VMEM((tt + 16, GW), F32),
        pltpu.VMEM((SUBLANES, tt + HALO, GW), F32),
        pltpu.VMEM((tt, GW), F32),
    ]
    if last:
        scratch += [pltpu.VMEM((2, SUBLANES, D_MODEL), F32), pltpu.VMEM((SUBLANES, D_MODEL), F32)]
    n_in, n_out = len(ins), len(out_shape)
    ins, in_specs, out_shape, out_specs, scratch = _with_comm(comm, ins, in_specs, out_shape, out_specs, scratch)
    return pl.pallas_call(
        body, name=("fwd_last" if last else "fwd_layer") + ("" if comm is None else "_comm"), grid=(n_t,),
        in_specs=in_specs, out_specs=out_specs, out_shape=out_shape, scratch_shapes=scratch,
        compiler_params=pltpu.CompilerParams(dimension_semantics=("arbitrary",), vmem_limit_bytes=VMEM_LIMIT),
    )(*ins)


def _dsilu(z, sz):
    return sz * (1.0 + z * (1.0 - sz))


def _bwd_layer(dz, h, cb, p, ca, win_b, wout_b, prm, z_prev, lng_prev, *, tt, comm=None):
    t_len = dz.shape[0]
    n_t = t_len // tt
    has_prev = z_prev is not None
    group = min(WOUT_GROUP, n_t)
    assert n_t % group == 0

    def body(*refs):
        refs, comm_refs = _split_comm(refs, comm, n_in, n_out)
        dz_ref, h_ref, cb_ref, p_scr, ca_ref, win_hbm, wout_hbm, caw, cab, cbw, cbb, lbg, lbb, pw, pb, ps, cnt = refs[:17]
        k = 17
        if has_prev:
            zp_ref, lngp = refs[k:k + 2]
            k += 2
        dxo_ref, dh_ref, dwout_hbm, small_ref, dpw_ref = refs[k:k + 5]
        k += 5
        if has_prev:
            dlnp_ref = refs[k]
            k += 1
        (win_v, wout_v, dzb_all, dy_scr, y_all, dx_scr, dca_ext, dcb_ext, dpn_ext, sh,
         pl_scr, dpl_scr, dp_scr, racc, dpw_acc, dwout_acc) = refs[k:k + 16]
        k += 16
        if has_prev:
            acc2 = refs[k]
        i = pl.program_id(0)
        ti = n_t - 1 - i
        t0 = ti * tt
        slot = i % group
        slot_rows = pl.ds(pl.multiple_of(slot * tt, tt), tt)
        dzb, y_scr = dzb_all.at[slot_rows], y_all.at[slot_rows]

        @pl.when(i == 0)
        def _():
            if comm is not None:
                comm["start"](*comm_refs)
            pltpu.sync_copy(win_hbm, win_v)
            pltpu.sync_copy(wout_hbm, wout_v)
            dca_ext[tt:tt + SUBLANES, :] = jnp.zeros((SUBLANES, GW), F32)
            dcb_ext[tt:tt + HALO, :] = jnp.zeros((HALO, GW), F32)
            dpn_ext[tt:tt + 16, :] = jnp.zeros((16, GW), F32)
            racc[...] = jnp.zeros_like(racc)
            dpw_acc[...] = jnp.zeros_like(dpw_acc)
            dwout_acc[...] = jnp.zeros_like(dwout_acc)
            if has_prev:
                acc2[...] = jnp.zeros_like(acc2)

        dzb[...] = dz_ref[...].astype(BF16)
        dy_scr[...] = lax.dot_general(dzb[...], wout_v[...], (((1,), (1,)), ((), ())), preferred_element_type=F32)

        def a2(base):
            rows = pl.ds(base, RC)
            ca = ca_ref[rows, :].astype(F32)
            bg, z = _hcol(h_ref, 0, base), _hcol(h_ref, 3, base)
            sz = _sig(z)
            sza = z * sz
            dya = dy_scr[rows, 0:GW]
            ya0 = bg * ca
            y_scr[rows, 0:GW] = (ya0 * sza).astype(BF16)
            dya0 = dya * sza
            dh_ref[rows, 3 * GW:4 * GW] = (dya * ya0 * _dsilu(z, sz)).astype(BF16)
            dh_ref[rows, 0:GW] = (dya0 * ca).astype(BF16)
            dca = dya0 * bg
            dca_ext[rows, :] = dca
            racc[R_DCAB] += _fold8(dca)
        _chunks(tt, a2)
        _build_shifts(dca_ext, sh, (1, 2), tt)

        def a3(base):
            rows = pl.ds(base, RC)
            wins = [_tap(dca_ext, sh, 2 - kk, base) for kk in range(KA)]
            cg, v = _hcol(h_ref, 1, base), _hcol(h_ref, 2, base)
            q = cg * v
            dq = caw[0:1, :] * wins[0] + caw[1:2, :] * wins[1] + caw[2:3, :] * wins[2]
            for kk in range(KA):
                racc[R_DWA + kk] += _fold8(q * wins[kk])
            dh_ref[rows, GW:2 * GW] = (dq * v).astype(BF16)
            dh_ref[rows, 2 * GW:3 * GW] = (dq * cg).astype(BF16)
        _chunks(tt, a3)
        dca_ext[tt:tt + SUBLANES, :] = dca_ext[0:SUBLANES, :]

        def b2(base):
            rows = pl.ds(base, RC)
            xhat, rstd = _ln_stats(cb_ref[rows, :].astype(F32))
            lnv = xhat * lbg[...] + lbb[...]
            sl = _sig(lnv)
            s = lnv * sl
            z = _hcol(h_ref, 6, base)
            sz = _sig(z)
            szb = z * sz
            y_scr[rows, GW:2 * GW] = (s * szb).astype(BF16)
            dyb = dy_scr[rows, GW:2 * GW]
            dh_ref[rows, 6 * GW:7 * GW] = (dyb * s * _dsilu(z, sz)).astype(BF16)
            dlnv = dyb * szb * _dsilu(lnv, sl)
            racc[R_DLBG] += _fold8(dlnv * xhat)
            racc[R_DLBB] += _fold8(dlnv)
            dcb = _ln_bwd(dlnv, xhat, rstd, lbg[...])
            dcb_ext[rows, :] = dcb
            racc[R_DCBB] += _fold8(dcb)
        _chunks(tt, b2, unroll=4)
        _build_shifts(dcb_ext, sh, range(1, 8), tt + HALO - SUBLANES)

        def b3(base):
            rows = pl.ds(base, RC)
            v, gt = _hcol(h_ref, 4, base), _hcol(h_ref, 5, base)
            sg = _sig(gt)
            ub = v * sg
            dubv = [jnp.zeros((RC, GW), F32)]

            def tap(off, win):
                dubv[0] = dubv[0] + cbw[KB - 1 - off:KB - off, :] * win
                racc[R_DWB + KB - 1 - off] += _fold8(ub * win)
            _for_taps(dcb_ext, sh, base, range(KB), tap)
            dub = dubv[0]
            dh_ref[rows, 4 * GW:5 * GW] = (dub * sg).astype(BF16)
            dh_ref[rows, 5 * GW:6 * GW] = (dub * v * sg * (1.0 - sg)).astype(BF16)
        _chunks(tt, b3, unroll=4)
        dcb_ext[tt:tt + HALO, :] = dcb_ext[0:HALO, :]

        for g in range(len(POOL_WINDOWS)):
            lanes = slice(g * PGD, (g + 1) * PGD)
            pl_scr[:, lanes] = jnp.dot(p_scr[:, lanes], pw[g], preferred_element_type=F32)

        def c3(base):
            rows = pl.ds(base, RC)
            z = _hcol(h_ref, 8, base)
            sz = _sig(z)
            szc = z * sz
            plb = pl_scr[rows, :] + pb[...]
            yc0 = plb * ps[...]
            y_scr[rows, 2 * GW:3 * GW] = (yc0 * szc).astype(BF16)
            dyc = dy_scr[rows, 2 * GW:3 * GW]
            dh_ref[rows, 8 * GW:9 * GW] = (dyc * yc0 * _dsilu(z, sz)).astype(BF16)
            dyc0 = dyc * szc
            racc[R_DPS] += _fold8(dyc0 * plb)
            dpl = dyc0 * ps[...]
            racc[R_DPB] += _fold8(dpl)
            dpl_scr[rows, :] = dpl.astype(BF16)
        _chunks(tt, c3, unroll=4)
        for g in range(len(POOL_WINDOWS)):
            lanes = slice(g * PGD, (g + 1) * PGD)
            dpw_acc[g] += lax.dot_general(p_scr[:, lanes], dpl_scr[:, lanes], (((0,), (0,)), ((), ())),
                                          preferred_element_type=F32)
            dp_scr[:, lanes] = lax.dot_general(dpl_scr[:, lanes], pw[g], (((1,), (1,)), ((), ())),
                                               preferred_element_type=F32)

        def c4(base):
            rows = pl.ds(base, RC)
            dpn_ext[rows, :] = dp_scr[rows, :] * _inv_count(cnt, base + t0 == 0)
        _chunks(tt, c4)
        _build_shifts(dpn_ext, sh, range(1, 8), tt + SUBLANES, first_lane=lambda r: 0 if r == 1 else PGD if r <= 3 else 2 * PGD)

        def c5(base):
            rows = pl.ds(base, RC)
            for g, w in enumerate(POOL_WINDOWS):
                lanes = slice(g * PGD, (g + 1) * PGD)
                acc = _tap(dpn_ext, sh, 0, base, lanes)
                for j in range(1, w):
                    acc = acc + _tap(dpn_ext, sh, j, base, lanes)
                dh_ref[rows, 7 * GW + g * PGD:7 * GW + (g + 1) * PGD] = (acc - dp_scr[rows, lanes]).astype(BF16)
        _chunks(tt, c5, unroll=4)
        dpn_ext[tt:tt + 16, :] = dpn_ext[0:16, :]

        @pl.when(slot == group - 1)
        def _():
            for r in range(D_MIX // GW):
                dwout_acc[r * GW:(r + 1) * GW, :] += lax.dot_general(
                    y_all[:, r * GW:(r + 1) * GW], dzb_all[...], (((0,), (0,)), ((), ())), preferred_element_type=F32)
        dx_scr[...] = lax.dot_general(dh_ref[...], win_v[...], (((1,), (1,)), ((), ())), preferred_element_type=F32)

        def post(base):
            rows = pl.ds(base, RC)
            dx = ALPHA * dz_ref[rows, :] + dx_scr[rows, :]
            if has_prev:
                xhat, rstd = _ln_stats(zp_ref[rows, :])
                acc2[0] += _fold8(dx * xhat)
                acc2[1] += _fold8(dx)
                dxo_ref[rows, :] = _ln_bwd(dx, xhat, rstd, lngp[...])
            else:
                dxo_ref[rows, :] = dx
        _chunks(tt, post, unroll=8)

        @pl.when(i == n_t - 1)
        def _():
            small_ref[...] = jnp.sum(racc[...], axis=1)
            dpw_ref[...] = dpw_acc[...]
            pltpu.sync_copy(dwout_acc, dwout_hbm)
            if has_prev:
                dlnp_ref[...] = jnp.sum(acc2[...], axis=1)
            if comm is not None:
                comm["wait"](*comm_refs)

    rtile = lambda c: pl.BlockSpec((tt, c), lambda i: (n_t - 1 - i, 0))
    full = lambda a: pl.BlockSpec(a.shape, lambda i: (0,) * a.ndim)
    const = lambda shp: pl.BlockSpec(shp, lambda i: (0,) * len(shp))
    hbm = pl.BlockSpec(memory_space=pl.ANY)
    ins = [dz, h, cb, p, ca, win_b, wout_b, *prm] + ([z_prev, lng_prev] if has_prev else [])
    in_specs = [rtile(D_MODEL), rtile(D_IN), rtile(GW), rtile(GW), rtile(GW), hbm, hbm] + [full(a) for a in prm] \
        + ([rtile(D_MODEL), full(lng_prev)] if has_prev else [])
    out_shape = [jax.ShapeDtypeStruct((t_len, D_MODEL), F32), jax.ShapeDtypeStruct((t_len, D_IN), BF16),
                 jax.ShapeDtypeStruct((D_MIX, D_MODEL), F32), jax.ShapeDtypeStruct((N_RACC, GW), F32),
                 jax.ShapeDtypeStruct((len(POOL_WINDOWS), PGD, PGD), F32)]
    out_specs = [rtile(D_MODEL), rtile(D_IN), hbm, const((N_RACC, GW)), const((len(POOL_WINDOWS), PGD, PGD))]
    if has_prev:
        out_shape.append(jax.ShapeDtypeStruct((2, D_MODEL), F32))
        out_specs.append(const((2, D_MODEL)))
    scratch = [
        pltpu.VMEM((D_MODEL, D_IN), BF16), pltpu.VMEM((D_MIX, D_MODEL), BF16),
        pltpu.VMEM((group * tt, D_MODEL), BF16), pltpu.VMEM((tt, D_MIX), F32), pltpu.VMEM((group * tt, D_MIX), BF16),
        pltpu.VMEM((tt, D_MODEL), F32),
        pltpu.VMEM((tt + SUBLANES, GW), F32), pltpu.VMEM((tt + HALO, GW), F32), pltpu.VMEM((tt + 16, GW), F32),
        pltpu.VMEM((SUBLANES, tt + HALO, GW), F32),
        pltpu.VMEM((tt, GW), F32), pltpu.VMEM((tt, GW), BF16), pltpu.VMEM((tt, GW), F32),
        pltpu.VMEM((N_RACC, SUBLANES, GW), F32), pltpu.VMEM((len(POOL_WINDOWS), PGD, PGD), F32),
        pltpu.VMEM((D_MIX, D_MODEL), F32),
    ]
    if has_prev:
        scratch.append(pltpu.VMEM((2, SUBLANES, D_MODEL), F32))
    n_in, n_out = len(ins), len(out_shape)
    ins, in_specs, out_shape, out_specs, scratch = _with_comm(comm, ins, in_specs, out_shape, out_specs, scratch)
    return pl.pallas_call(
        body, name=("bwd_layer_prev" if has_prev else "bwd_layer") + ("" if comm is None else "_comm"), grid=(n_t,),
        in_specs=in_specs, out_specs=out_specs, out_shape=out_shape, scratch_shapes=scratch,
        compiler_params=pltpu.CompilerParams(dimension_semantics=("arbitrary",), vmem_limit_bytes=VMEM_LIMIT),
    )(*ins)


def _wgrad_in(xb, dh, *, tk, comm=None):
    t_len = xb.shape[0]
    tk = min(tk, t_len)
    n_k = t_len // tk

    def body(*refs):
        (x_ref, dh_ref, o_ref), comm_refs = _split_comm(refs, comm, 2, 1)
        j, k = pl.program_id(0), pl.program_id(1)

        @pl.when(k == 0)
        def _():
            o_ref[...] = jnp.zeros_like(o_ref)
        if comm is not None:
            @pl.when((j == 0) & (k == 0))
            def _():
                comm["start"](*comm_refs)
        o_ref[0] += lax.dot_general(x_ref[...], dh_ref[...], (((0,), (0,)), ((), ())), preferred_element_type=F32)
        if comm is not None:
            @pl.when((j == N_CHIP - 1) & (k == n_k - 1))
            def _():
                comm["wait"](*comm_refs)

    ins = [xb, dh]
    in_specs = [pl.BlockSpec((tk, D_MODEL), lambda j, k: (k, 0)), pl.BlockSpec((tk, SHARD_IN), lambda j, k: (k, j))]
    out_shape = [jax.ShapeDtypeStruct((N_CHIP, D_MODEL, SHARD_IN), F32)]
    out_specs = [pl.BlockSpec((1, D_MODEL, SHARD_IN), lambda j, k: (j, 0, 0))]
    ins, in_specs, out_shape, out_specs, scratch = _with_comm(comm, ins, in_specs, out_shape, out_specs, [])
    outs = pl.pallas_call(
        body, name="wgrad_in" + ("" if comm is None else "_comm"), grid=(N_CHIP, n_k),
        in_specs=in_specs, out_specs=out_specs, out_shape=out_shape, scratch_shapes=scratch,
        compiler_params=pltpu.CompilerParams(dimension_semantics=("arbitrary", "arbitrary"), vmem_limit_bytes=VMEM_LIMIT),
    )(*ins)
    return outs[0] if comm is None else outs


MESH = pl.DeviceIdType.MESH
ANY = pl.BlockSpec(memory_space=pl.ANY)


def _place():
    x, y, c = lax.axis_index("x"), lax.axis_index("y"), lax.axis_index("c")
    others = [(1 - x, y), (x, 1 - y), (1 - x, 1 - y)]
    return x, y, c, 2 * x + y, [(ox, oy, 2 * ox + oy) for ox, oy in others]


def _rcopy(src, dst, send_sems, recv_sems, k, dev):
    return pltpu.make_async_remote_copy(src_ref=src, dst_ref=dst, send_sem=send_sems.at[k], recv_sem=recv_sems.at[k],
                                        device_id=dev, device_id_type=MESH)


def _gather_weights(w_in, w_out, cw):
    hi, ho = D_MODEL // 2, SHARD_OUT // 2

    def body(win_ref, wout_ref, cw_ref, owin, owout, ocw, bin_v, bout_v, send_sems, recv_sems, lsem):
        x, y, c, me, others = _place()

        def cast(l):
            for r0 in range(0, D_MODEL, 256):
                bin_v[l, r0:r0 + 256, :] = win_ref[l, r0:r0 + 256, :].astype(BF16)
            bout_v[l] = wout_ref[l].astype(BF16)
        cast(0)
        cin = pl.ds(pl.multiple_of(me * SHARD_IN, 128), SHARD_IN)
        rout = pl.ds(pl.multiple_of(me * SHARD_OUT, 128), SHARD_OUT)
        local = [pltpu.make_async_copy(bin_v.at[0], owin.at[:, cin], lsem.at[0]),
                 pltpu.make_async_copy(bout_v.at[0], owout.at[rout, :], lsem.at[1]),
                 pltpu.make_async_copy(cw_ref, ocw.at[me], lsem.at[2])]
        for cp in local:
            cp.start()

        def in_half(chip, core):
            return owin.at[pl.ds(pl.multiple_of(core * hi, 256), hi), pl.ds(pl.multiple_of(chip * SHARD_IN, 128), SHARD_IN)]

        def out_half(chip, core):
            return owout.at[pl.ds(pl.multiple_of(chip * SHARD_OUT + core * ho, 64), ho), :]

        first = []
        for k, (ox, oy, _) in enumerate(others):
            dev = (ox, oy, c)
            first.append(_rcopy(bin_v.at[0, pl.ds(pl.multiple_of(c * hi, 256), hi), :], in_half(me, c), send_sems, recv_sems, k, dev))
            first.append(_rcopy(bout_v.at[0, pl.ds(pl.multiple_of(c * ho, 64), ho), :], out_half(me, c), send_sems, recv_sems, 3 + k, dev))
            first.append(_rcopy(cw_ref, ocw.at[me], send_sems, recv_sems, 6 + k, dev))
        for cp in first:
            cp.start()
        cast(1)
        sib = (x, y, 1 - c)
        passed = []
        for k, (ox, oy, oc) in enumerate(others):
            _rcopy(in_half(oc, c), in_half(oc, c), send_sems, recv_sems, k, sib).wait_recv()
            fwd_in = _rcopy(in_half(oc, c), in_half(oc, c), send_sems, recv_sems, 9 + k, sib)
            fwd_in.start()
            _rcopy(out_half(oc, c), out_half(oc, c), send_sems, recv_sems, 3 + k, sib).wait_recv()
            fwd_out = _rcopy(out_half(oc, c), out_half(oc, c), send_sems, recv_sems, 12 + k, sib)
            fwd_out.start()
            passed += [fwd_in, fwd_out]
        for k, (ox, oy, oc) in enumerate(others):
            _rcopy(cw_ref, ocw.at[oc], send_sems, recv_sems, 6 + k, sib).wait_recv()
            _rcopy(in_half(oc, 1 - c), in_half(oc, 1 - c), send_sems, recv_sems, 9 + k, sib).wait_recv()
            _rcopy(out_half(oc, 1 - c), out_half(oc, 1 - c), send_sems, recv_sems, 12 + k, sib).wait_recv()
        for cp in first + passed:
            cp.wait_send()
        for cp in local:
            cp.wait()

    vm = pl.BlockSpec(memory_space=pltpu.VMEM)
    return pl.pallas_call(
        body, name="gather_weights",
        in_specs=[vm, vm, vm], out_specs=[ANY, ANY, ANY, vm, vm],
        out_shape=[jax.ShapeDtypeStruct((D_MODEL, D_IN), BF16), jax.ShapeDtypeStruct((D_MIX, D_MODEL), BF16),
                   jax.ShapeDtypeStruct((N_CHIP,) + cw.shape, F32),
                   jax.ShapeDtypeStruct((DEPTH, D_MODEL, SHARD_IN), BF16), jax.ShapeDtypeStruct((DEPTH, SHARD_OUT, D_MODEL), BF16)],
        scratch_shapes=[pltpu.SemaphoreType.DMA((15,)), pltpu.SemaphoreType.DMA((15,)), pltpu.SemaphoreType.DMA((3,))],
        compiler_params=pltpu.CompilerParams(vmem_limit_bytes=VMEM_LIMIT),
    )(w_in, w_out, cw)


def _gather_starts(bsh_in, bsh_out, owin, owout, send_sems, recv_sems, lsem, layer):
    x, y, c, me, others = _place()
    hi, ho = D_MODEL // 2, SHARD_OUT // 2
    pltpu.make_async_copy(bsh_in.at[layer], owin.at[:, pl.ds(pl.multiple_of(me * SHARD_IN, 128), SHARD_IN)], lsem.at[0]).start()
    pltpu.make_async_copy(bsh_out.at[layer], owout.at[pl.ds(pl.multiple_of(me * SHARD_OUT, 128), SHARD_OUT), :], lsem.at[1]).start()
    for k, (ox, oy, _) in enumerate(others):
        for t in range(2):
            pltpu.make_async_remote_copy(
                src_ref=bsh_in.at[layer, pl.ds(pl.multiple_of(c * hi, 256), hi), :],
                dst_ref=owin.at[pl.ds(pl.multiple_of(c * hi, 256), hi), pl.ds(pl.multiple_of(me * SHARD_IN, 128), SHARD_IN)],
                send_sem=send_sems.at[2 * k + t], recv_sem=recv_sems.at[2 * k + c], device_id=(ox, oy, t), device_id_type=MESH).start()
            pltpu.make_async_remote_copy(
                src_ref=bsh_out.at[layer, pl.ds(pl.multiple_of(c * ho, 64), ho), :],
                dst_ref=owout.at[pl.ds(pl.multiple_of(me * SHARD_OUT + c * ho, 64), ho), :],
                send_sem=send_sems.at[6 + 2 * k + t], recv_sem=recv_sems.at[6 + 2 * k + c], device_id=(ox, oy, t), device_id_type=MESH).start()


def _gather_waits(bsh_in, bsh_out, owin, owout, send_sems, recv_sems, lsem, layer):
    x, y, c, me, others = _place()
    hi, ho = D_MODEL // 2, SHARD_OUT // 2
    src_in = bsh_in.at[layer, pl.ds(0, hi), :]
    src_out = bsh_out.at[layer, pl.ds(0, ho), :]
    for k, (ox, oy, oc) in enumerate(others):
        for t in range(2):
            dst_in = owin.at[pl.ds(t * hi, hi), pl.ds(pl.multiple_of(oc * SHARD_IN, 128), SHARD_IN)]
            dst_out = owout.at[pl.ds(pl.multiple_of(oc * SHARD_OUT + t * ho, 64), ho), :]
            a = pltpu.make_async_remote_copy(src_ref=src_in, dst_ref=dst_in, send_sem=send_sems.at[2 * k + t],
                                             recv_sem=recv_sems.at[2 * k + t], device_id=(ox, oy, t), device_id_type=MESH)
            b = pltpu.make_async_remote_copy(src_ref=src_out, dst_ref=dst_out, send_sem=send_sems.at[6 + 2 * k + t],
                                             recv_sem=recv_sems.at[6 + 2 * k + t], device_id=(ox, oy, t), device_id_type=MESH)
            a.wait_send()
            a.wait_recv()
            b.wait_send()
            b.wait_recv()
    pltpu.make_async_copy(bsh_in.at[layer], owin.at[:, pl.ds(pl.multiple_of(me * SHARD_IN, 128), SHARD_IN)], lsem.at[0]).wait()
    pltpu.make_async_copy(bsh_out.at[layer], owout.at[pl.ds(pl.multiple_of(me * SHARD_OUT, 128), SHARD_OUT), :], lsem.at[1]).wait()


def _gather_comm(bsh_in, bsh_out, layer):
    return dict(ins=[bsh_in, bsh_out],
                out_shape=[jax.ShapeDtypeStruct((D_MODEL, D_IN), BF16), jax.ShapeDtypeStruct((D_MIX, D_MODEL), BF16)],
                sems=[pltpu.SemaphoreType.DMA((12,)), pltpu.SemaphoreType.DMA((12,)), pltpu.SemaphoreType.DMA((2,))],
                start=lambda ins, outs, sems: _gather_starts(ins[0], ins[1], outs[0], outs[1], *sems, layer),
                wait=lambda ins, outs, sems: _gather_waits(ins[0], ins[1], outs[0], outs[1], *sems, layer))


def _exchange_halves(arrs, tag):
    n = len(arrs)

    def body(*refs):
        ins, outs, (send_sems, recv_sems) = refs[:n], refs[n:2 * n], refs[2 * n:]
        x, y, c, _, _ = _place()
        cps = []
        for m in range(n):
            half = ins[m].shape[1] // 2
            cps.append(_rcopy(ins[m].at[:, pl.ds(pl.multiple_of((1 - c) * half, SUBLANES), half), :], outs[m],
                              send_sems, recv_sems, m, (x, y, 1 - c)))
        for cp in cps:
            cp.start()
        for cp in cps:
            cp.wait()

    return pl.pallas_call(
        body, name="exchange_halves_" + tag, in_specs=[ANY] * n, out_specs=[ANY] * n,
        out_shape=[jax.ShapeDtypeStruct((a.shape[0], a.shape[1] // 2, a.shape[2]), F32) for a in arrs],
        scratch_shapes=[pltpu.SemaphoreType.DMA((n,)), pltpu.SemaphoreType.DMA((n,))],
    )(*arrs)


def _add_own_half(a, got, core, *, rb, dtype):
    nj, r, cdim = a.shape
    half = r // 2

    def body(core_ref, a_ref, g_ref, o_ref):
        o_ref[...] = (a_ref[0] + g_ref[...]).astype(dtype)

    return pl.pallas_call(
        body, name="add_own_half",
        grid_spec=pltpu.PrefetchScalarGridSpec(
            num_scalar_prefetch=1, grid=(nj, half // rb),
            in_specs=[pl.BlockSpec((1, 1, rb, cdim), lambda j, i, cr: (j, cr[0], i, 0)),
                      pl.BlockSpec((1, rb, cdim), lambda j, i, cr: (j, i, 0))],
            out_specs=pl.BlockSpec((1, rb, cdim), lambda j, i, cr: (j, i, 0))),
        out_shape=jax.ShapeDtypeStruct((nj, half, cdim), dtype),
    )(core, a.reshape(nj, 2, half, cdim), got)


def _owner_starts(ins, outs, sems):
    send_sems, recv_sems, lsem = sems
    x, y, c, me, others = _place()
    for m in range(len(ins)):
        pltpu.make_async_copy(ins[m].at[me], outs[m].at[me], lsem.at[m]).start()
        for k, (ox, oy, oc) in enumerate(others):
            _rcopy(ins[m].at[oc], outs[m].at[me], send_sems, recv_sems, 3 * m + k, (ox, oy, c)).start()


def _owner_waits(ins, outs, sems):
    send_sems, recv_sems, lsem = sems
    x, y, c, me, others = _place()
    for m in range(len(ins)):
        for k, (ox, oy, oc) in enumerate(others):
            _rcopy(ins[m].at[oc], outs[m].at[oc], send_sems, recv_sems, 3 * m + k, (ox, oy, c)).wait()
        pltpu.make_async_copy(ins[m].at[me], outs[m].at[me], lsem.at[m]).wait()


def _owner_comm(arrs):
    n = len(arrs)
    return dict(ins=arrs, out_shape=[jax.ShapeDtypeStruct(a.shape, a.dtype) for a in arrs],
                sems=[pltpu.SemaphoreType.DMA((3 * n,)), pltpu.SemaphoreType.DMA((3 * n,)), pltpu.SemaphoreType.DMA((n,))],
                start=_owner_starts, wait=_owner_waits)


def _send_to_owners(arrs):
    n = len(arrs)

    def body(*refs):
        ins, outs, sems = refs[:n], refs[n:2 * n], refs[2 * n:]
        _owner_starts(ins, outs, sems)
        _owner_waits(ins, outs, sems)

    job = _owner_comm(arrs)
    return pl.pallas_call(
        body, name="send_to_owners", in_specs=[ANY] * n, out_specs=[ANY] * n,
        out_shape=job["out_shape"], scratch_shapes=job["sems"],
    )(*arrs)


def _sum_chips(a, *, rb):
    nj, r, cdim = a.shape

    def body(a_ref, o_ref):
        f = lambda k: a_ref[k].astype(F32)
        o_ref[...] = ((f(0) + f(1)) + f(2)) + f(3)

    return pl.pallas_call(
        body, name="sum_chips", grid=(r // rb,),
        in_specs=[pl.BlockSpec((nj, rb, cdim), lambda i: (0, i, 0))],
        out_specs=pl.BlockSpec((rb, cdim), lambda i: (i, 0)),
        out_shape=jax.ShapeDtypeStruct((r, cdim), F32),
    )(a)


def _sum_chips_into(a, dest, layer, core, *, rb):
    nj, half, cdim = a.shape
    nb = half // rb

    def body(*refs):
        a_ref, o_ref = refs[1], refs[-1]
        f = lambda k: a_ref[k].astype(F32)
        o_ref[0] = ((f(0) + f(1)) + f(2)) + f(3)

    grid_spec = pltpu.PrefetchScalarGridSpec(
        num_scalar_prefetch=1, grid=(nb,),
        in_specs=[pl.BlockSpec((nj, rb, cdim), lambda i, cr: (0, i, 0))] + ([] if dest is None else [ANY]),
        out_specs=pl.BlockSpec((1, rb, cdim), lambda i, cr: (layer, cr[0] * nb + i, 0)))
    return pl.pallas_call(
        body, name="sum_chips_into", grid_spec=grid_spec,
        out_shape=jax.ShapeDtypeStruct((DEPTH, 2 * half, cdim), F32),
        input_output_aliases={} if dest is None else {2: 0},
    )(*([core, a] if dest is None else [core, a, dest]))


def _spread_reduced(g_in, g_out, red_small):
    hs = red_small.shape[0]

    def body(gin_in, gout_in, sm, gin, gout, fsm, gsm, send_sems, recv_sems, lsem):
        x, y, c, me, others = _place()
        sib = (x, y, 1 - c)
        hi, ho = D_MODEL // 2, SHARD_OUT // 2
        ri, ro = pl.ds(pl.multiple_of(c * hi, SUBLANES), hi), pl.ds(pl.multiple_of(c * ho, SUBLANES), ho)
        remote = [_rcopy(gin.at[:, ri, :], gin.at[:, ri, :], send_sems, recv_sems, 0, sib),
                  _rcopy(gout.at[:, ro, :], gout.at[:, ro, :], send_sems, recv_sems, 1, sib)]
        own_small = pltpu.make_async_copy(sm, gsm.at[me], lsem.at[0])
        small = [_rcopy(sm, gsm.at[me], send_sems, recv_sems, 2 + k, (ox, oy, c)) for k, (ox, oy, _) in enumerate(others)]
        for cp in remote + [own_small] + small:
            cp.start()
        own_small.wait()
        for cp in small:
            cp.wait()
        mine = fsm.at[:, pl.ds(pl.multiple_of(c * hs, SUBLANES), hs), :]
        keep = pltpu.make_async_copy(gsm, mine, lsem.at[1])
        give = _rcopy(gsm, mine, send_sems, recv_sems, 5, sib)
        keep.start()
        give.start()
        for cp in remote + [give]:
            cp.wait()
        keep.wait()

    return pl.pallas_call(
        body, name="spread_reduced", in_specs=[ANY] * 3, out_specs=[ANY] * 4,
        out_shape=[jax.ShapeDtypeStruct(g_in.shape, F32), jax.ShapeDtypeStruct(g_out.shape, F32),
                   jax.ShapeDtypeStruct((N_CHIP, 2 * hs, GW), F32), jax.ShapeDtypeStruct((N_CHIP, hs, GW), F32)],
        input_output_aliases={0: 0, 1: 1},
        scratch_shapes=[pltpu.SemaphoreType.DMA((6,)), pltpu.SemaphoreType.DMA((6,)), pltpu.SemaphoreType.DMA((2,))],
    )(g_in, g_out, red_small)[:3]


def _adamw_math(w, g, m, v):
    m = ADAM_B1 * m + (1.0 - ADAM_B1) * g
    v = ADAM_B2 * v + (1.0 - ADAM_B2) * (g * g)
    m_hat = m / (1.0 - ADAM_B1 ** ADAM_STEP)
    v_hat = v / (1.0 - ADAM_B2 ** ADAM_STEP)
    delta = -ADAM_LR * (m_hat / (jnp.sqrt(v_hat) + ADAM_EPS) + ADAM_WD * w)
    return delta, m, v


def _adamw_big(w, g, m, v, *, rb):
    r, cdim = w.shape

    def body(w_ref, g_ref, m_ref, v_ref, d_ref, nm_ref, nv_ref, go_ref):
        g = g_ref[...]
        d_ref[...], nm_ref[...], nv_ref[...] = _adamw_math(w_ref[...], g, m_ref[...], v_ref[...])
        go_ref[...] = g

    spec = pl.BlockSpec((rb, cdim), lambda i: (i, 0))
    return pl.pallas_call(
        body, name="adamw_big", grid=(r // rb,), in_specs=[spec] * 4, out_specs=[spec] * 4,
        out_shape=[jax.ShapeDtypeStruct((r, cdim), F32)] * 4,
    )(w, g, m, v)


def _adamw_small(ws, gs, ms, vs):
    n = len(ws)

    def body(*refs):
        w, g, m, v = refs[:n], refs[n:2 * n], refs[2 * n:3 * n], refs[3 * n:4 * n]
        d, nm, nv = refs[4 * n:5 * n], refs[5 * n:6 * n], refs[6 * n:7 * n]
        for k in range(n):
            d[k][...], nm[k][...], nv[k][...] = _adamw_math(w[k][...], g[k][...], m[k][...], v[k][...])

    shapes = [jax.ShapeDtypeStruct(a.shape, F32) for a in ws]
    outs = pl.pallas_call(body, name="adamw_small", out_shape=shapes * 3)(*ws, *gs, *ms, *vs)
    return outs[:n], outs[n:2 * n], outs[2 * n:]


TT = 256
TK = 4096
CW_ROWS = 40
PACK_ROWS = 192


def _pack(rows):
    packed = jnp.concatenate(rows, axis=0)
    packed = jnp.pad(packed, ((0, PACK_ROWS - packed.shape[0]), (0, 0)))
    return packed.reshape(N_CHIP, PACK_ROWS // N_CHIP, GW)


def _reduce_to_owner_halves(parts, core1, tag):
    got = _exchange_halves(parts, tag)
    rbs = {D_MODEL: 256, SHARD_OUT: SHARD_OUT // 2, PACK_ROWS // N_CHIP: PACK_ROWS // N_CHIP // 2}
    return [_add_own_half(a, g, core1, rb=rbs[a.shape[1]], dtype=F32 if a.shape[1] == PACK_ROWS // N_CHIP else BF16)
            for a, g in zip(parts, got)]


def kernel(x, w_in, conv_a_w, conv_a_b, conv_b_w, conv_b_b, ln_b_g, ln_b_b, pool_w, pool_b, pool_scale, w_out, ln_g, ln_b, loss_target, m_w_in, m_conv_a_w, m_conv_a_b, m_conv_b_w, m_conv_b_b, m_ln_b_g, m_ln_b_b, m_pool_w, m_pool_b, m_pool_scale, m_w_out, m_ln_g, m_ln_b, v_w_in, v_conv_a_w, v_conv_a_b, v_conv_b_w, v_conv_b_b, v_ln_b_g, v_ln_b_b, v_pool_w, v_pool_b, v_pool_scale, v_w_out, v_ln_g, v_ln_b):
    chip = 2 * lax.axis_index("x") + lax.axis_index("y")
    core1 = lax.axis_index("c").reshape(1).astype(jnp.int32)
    x2, tgt = x[0], loss_target[0]

    cw = jnp.zeros((DEPTH, CW_ROWS, PGD), F32).at[:, 0:KA].set(conv_a_w).at[:, 8:8 + KB].set(conv_b_w)
    win0_b, wout0_b, cw_all, bsh_in, bsh_out = _gather_weights(w_in, w_out, cw)
    cw_full = jnp.transpose(cw_all, (1, 2, 0, 3)).reshape(DEPTH, CW_ROWS, GW)
    row = lambda a, l: a[l].reshape(1, -1)
    cnt = _count_table()
    prm = [(cw_full[l, 0:KA], row(conv_a_b, l), cw_full[l, 8:8 + KB], row(conv_b_b, l), row(ln_b_g, l), row(ln_b_b, l),
            pool_w[l].astype(BF16), row(pool_b, l), row(pool_scale, l), cnt) for l in range(DEPTH)]

    h0, xb0, cb0, pool0, ca0, z0, x1, win1_b, wout1_b = _fwd_layer(x2, win0_b, wout0_b, prm[0], row(ln_g, 0), row(ln_b, 0), None, tt=TT, last=False,
                                                  comm=_gather_comm(bsh_in, bsh_out, 1))
    h1, xb1, cb1, pool1, ca1, dz1, dln1, loss8 = _fwd_layer(x1, win1_b, wout1_b, prm[1], row(ln_g, 1), row(ln_b, 1), tgt, tt=TT, last=True)

    dz0, dh1, dwout1, small1, dpw1, dln0 = _bwd_layer(dz1, h1, cb1, pool1, ca1, win1_b, wout1_b, prm[1], z0, row(ln_g, 0), tt=TT)
    dwin1 = _wgrad_in(xb1, dh1, tk=TK)
    loss_row = jnp.pad(loss8, ((0, 0), (0, GW - loss8.shape[1])))
    pack1 = _pack([small1, dpw1.reshape(PGD, GW), dln1.reshape(4, GW), dln0.reshape(4, GW), loss_row])
    sums1 = _reduce_to_owner_halves([dwin1, dwout1.reshape(N_CHIP, SHARD_OUT, D_MODEL), pack1], core1, "1")
    gx, dh0, dwout0, small0, dpw0 = _bwd_layer(dz0, h0, cb0, pool0, ca0, win0_b, wout0_b, prm[0], None, None, tt=TT)
    pack0 = _pack([small0, dpw0.reshape(PGD, GW)])
    sums0 = _reduce_to_owner_halves([dwout0.reshape(N_CHIP, SHARD_OUT, D_MODEL), pack0], core1, "0")
    dwin0, *landed = _wgrad_in(xb0, dh0, tk=TK, comm=_owner_comm(sums1 + sums0))
    landed1, landed0 = landed[:3], landed[3:]
    landed0 = list(_send_to_owners(_reduce_to_owner_halves([dwin0], core1, "in0"))) + list(landed0)

    g_in = _sum_chips_into(landed0[0], _sum_chips_into(landed1[0], None, 1, core1, rb=256), 0, core1, rb=256)
    g_out = _sum_chips_into(landed0[1], _sum_chips_into(landed1[1], None, 1, core1, rb=SHARD_OUT // 2), 0, core1, rb=SHARD_OUT // 2)
    red_small = jnp.concatenate([_sum_chips(a, rb=PACK_ROWS // N_CHIP // 2) for a in (landed0[2], landed1[2])], axis=0)
    g_in, g_out, g_small = _spread_reduced(g_in, g_out, red_small)

    flat = lambda a: a.reshape(-1, a.shape[-1])
    unflat = lambda a, like: a.reshape(like.shape)
    d_in, nm_in, nv_in, g_in = [unflat(a, w_in) for a in _adamw_big(flat(w_in), flat(g_in), flat(m_w_in), flat(v_w_in), rb=256)]
    d_out, nm_out, nv_out, g_out = [unflat(a, w_out) for a in _adamw_big(flat(w_out), flat(g_out), flat(m_w_out), flat(v_w_out), rb=SHARD_OUT)]

    hp = PACK_ROWS // N_CHIP // 2
    unpack = lambda o: jnp.concatenate([g_small[:, o:o + hp], g_small[:, 2 * hp + o:3 * hp + o]], axis=1).reshape(PACK_ROWS, GW)
    p0, p1 = unpack(0), unpack(hp)
    small = [p0[0:N_RACC], p1[0:N_RACC]]
    dpw = [p[N_RACC:N_RACC + PGD].reshape(len(POOL_WINDOWS), PGD, PGD) for p in (p0, p1)]
    o = N_RACC + PGD
    g_lng = jnp.stack([p1[o + 4:o + 8].reshape(2, D_MODEL)[0], p1[o:o + 4].reshape(2, D_MODEL)[0]])
    g_lnb = jnp.stack([p1[o + 4:o + 8].reshape(2, D_MODEL)[1], p1[o:o + 4].reshape(2, D_MODEL)[1]])
    mine = lambda a: lax.dynamic_slice_in_dim(a, chip * PGD, PGD, axis=-1)
    stack = lambda f: jnp.stack([f(0), f(1)])
    g_caw = stack(lambda l: mine(small[l][R_DWA:R_DWA + KA]))
    g_cab = stack(lambda l: small[l][R_DCAB])
    g_cbw = stack(lambda l: mine(small[l][R_DWB:R_DWB + KB]))
    g_cbb = stack(lambda l: small[l][R_DCBB])
    g_lbg = stack(lambda l: small[l][R_DLBG])
    g_lbb = stack(lambda l: small[l][R_DLBB])
    g_pw = stack(lambda l: dpw[l])
    g_pb = stack(lambda l: small[l][R_DPB].reshape(len(POOL_WINDOWS), PGD))
    g_ps = stack(lambda l: small[l][R_DPS])
    ws = [conv_a_w, conv_a_b, conv_b_w, conv_b_b, ln_b_g, ln_b_b, pool_w, pool_b, pool_scale, ln_g, ln_b]
    gs = [g_caw, g_cab, g_cbw, g_cbb, g_lbg, g_lbb, g_pw, g_pb, g_ps, g_lng, g_lnb]
    ms = [m_conv_a_w, m_conv_a_b, m_conv_b_w, m_conv_b_b, m_ln_b_g, m_ln_b_b, m_pool_w, m_pool_b, m_pool_scale, m_ln_g, m_ln_b]
    vs = [v_conv_a_w, v_conv_a_b, v_conv_b_w, v_conv_b_b, v_ln_b_g, v_ln_b_b, v_pool_w, v_pool_b, v_pool_scale, v_ln_g, v_ln_b]
    ds, nms, nvs = _adamw_small(ws, gs, ms, vs)

    loss = p1[o + 8, 0]

    def order(in_, small_, out_):
        return [in_, *small_[:9], out_, *small_[9:]]
    return (loss, gx[None], *order(g_in, gs, g_out), *order(d_in, ds, d_out), *order(nm_in, nms, nm_out), *order(nv_in, nvs, nv_out))
```

```python
import functools

import jax
import jax.numpy as jnp
import numpy as np
from jax import lax
from jax.experimental import pallas as pl
from jax.experimental.pallas import tpu as pltpu

F32 = jnp.float32
BF16 = jnp.bfloat16

D_MODEL = 1024
DEPTH = 2
GW = 512
D_IN = 9 * GW
D_MIX = 3 * GW
NG = D_IN // GW
POOL_WINDOWS = (2, 4, 8, 16)
PGD = 128
KA = 3
KB = 31
ALPHA = (2.0 * DEPTH) ** 0.25
LN_EPS = 1e-5
ADAM_LR, ADAM_B1, ADAM_B2, ADAM_EPS, ADAM_WD, ADAM_STEP = 0.001, 0.9, 0.999, 1e-08, 0.01, 10

N_CHIP = 4
SHARD_IN = D_IN // N_CHIP
SHARD_OUT = D_MIX // N_CHIP

SUBLANES = 8
RC = 32
HALO = 32
VMEM_LIMIT = 60 * 1024 * 1024
WOUT_GROUP = 4

R_DWA, R_DCAB, R_DWB, R_DCBB, R_DLBG, R_DLBB, R_DPB, R_DPS, N_RACC = 0, 3, 4, 35, 36, 37, 38, 39, 40


def _sig(v):
    return 0.5 * jnp.tanh(0.5 * v) + 0.5


def _chunks(n_rows, fn, unroll=1, extra=None):
    unroll = min(unroll, n_rows // RC)

    def step(m, carry):
        for u in range(unroll):
            fn(pl.multiple_of((m * unroll + u) * RC, RC))
        if extra is not None:
            extra(m)
        return carry
    lax.fori_loop(0, n_rows // (RC * unroll), step, 0)


def _fold8(v):
    return v.reshape(RC // SUBLANES, SUBLANES, v.shape[-1]).sum(axis=0)


def _build_shifts(ext_ref, sh_ref, shifts, n_rows, first_lane=None):
    for r in shifts:
        lanes = slice(0, ext_ref.shape[1]) if first_lane is None else slice(first_lane(r), ext_ref.shape[1])
        for c0 in range(0, n_rows, RC):
            n = min(RC, n_rows - c0)
            sh_ref[r, pl.ds(c0, n), lanes] = ext_ref[pl.ds(c0 + r, n), lanes]


def _tap(ext_ref, sh_ref, off, base, lanes=None):
    a, r = divmod(off, SUBLANES)
    src = ext_ref if r == 0 else sh_ref.at[r]
    if lanes is None:
        return src[pl.ds(base + SUBLANES * a, RC), :]
    return src[pl.ds(base + SUBLANES * a, RC), lanes]


def _ln_stats(v):
    mu = jnp.mean(v, axis=-1, keepdims=True)
    vc = v - mu
    var = jnp.mean(vc * vc, axis=-1, keepdims=True)
    rstd = lax.rsqrt(var + LN_EPS)
    return vc * rstd, rstd


def _ln_bwd(dy, xhat, rstd, g):
    dxh = dy * g
    m1 = jnp.mean(dxh, axis=-1, keepdims=True)
    m2 = jnp.mean(dxh * xhat, axis=-1, keepdims=True)
    return rstd * (dxh - m1 - xhat * m2)


def _for_taps(ext_ref, sh_ref, base, offsets, fn, lanes=slice(None)):
    for r in range(SUBLANES):
        offs = [o for o in offsets if o % SUBLANES == r]
        if not offs:
            continue
        a0, a1 = min(offs) // SUBLANES, max(offs) // SUBLANES
        src = ext_ref if r == 0 else sh_ref.at[r]
        win = src[pl.ds(base + SUBLANES * a0, RC + SUBLANES * (a1 - a0)), lanes]
        for o in offs:
            a = o // SUBLANES - a0
            fn(o, win[SUBLANES * a:SUBLANES * a + RC])


def _count_table():
    t = np.arange(1, RC + 1, dtype=np.float64)[:, None]
    w = np.repeat(np.asarray(POOL_WINDOWS, np.float64), PGD)[None, :]
    return jnp.asarray(1.0 / np.minimum(t, w), F32)


def _inv_count(cnt_ref, first):
    return jnp.where(first, cnt_ref[...], cnt_ref[RC - 1:RC, :])


def _hcol(h_ref, j, base):
    if len(h_ref.shape) == 3:
        return h_ref[j, pl.ds(base, RC), :].astype(F32)
    return h_ref[pl.ds(base, RC), j * GW:(j + 1) * GW].astype(F32)


def _with_comm(comm, ins, in_specs, out_shape, out_specs, scratch):
    if comm is None:
        return ins, in_specs, out_shape, out_specs, scratch
    hbm = pl.BlockSpec(memory_space=pl.ANY)
    return (ins + list(comm["ins"]), in_specs + [hbm] * len(comm["ins"]), out_shape + list(comm["out_shape"]),
            out_specs + [hbm] * len(comm["out_shape"]), scratch + list(comm["sems"]))


def _split_comm(refs, comm, n_in, n_out):
    refs = list(refs)
    if comm is None:
        return refs, None
    ci, co, cs = len(comm["ins"]), len(comm["out_shape"]), len(comm["sems"])
    own = refs[:n_in] + refs[n_in + ci:n_in + ci + n_out] + refs[n_in + ci + n_out + co:len(refs) - cs]
    return own, (refs[n_in:n_in + ci], refs[n_in + ci + n_out:n_in + ci + n_out + co], refs[len(refs) - cs:])


def _fwd_mixers(h_ref, cb_ref, ca_ref, y_scr, q_ext, ub_ext, cu_ext, sh, p_scr, pl_scr, prm, tt, t0):
    caw, cab, cbw, cbb, lbg, lbb, pw, pb, ps, cnt = prm

    def a1(base):
        q_ext[pl.ds(SUBLANES + base, RC), :] = _hcol(h_ref, 1, base) * _hcol(h_ref, 2, base)
    _chunks(tt, a1)
    _build_shifts(q_ext, sh, (6, 7), tt)

    def a2(base):
        ca = cab[...] + caw[0:1, :] * _tap(q_ext, sh, 6, base) + caw[1:2, :] * _tap(q_ext, sh, 7, base) \
            + caw[2:3, :] * _tap(q_ext, sh, 8, base)
        car = ca.astype(BF16)
        ca_ref[pl.ds(base, RC), :] = car
        z = _hcol(h_ref, 3, base)
        y_scr[pl.ds(base, RC), 0:GW] = (_hcol(h_ref, 0, base) * car.astype(F32) * (z * _sig(z))).astype(BF16)
    _chunks(tt, a2, unroll=2)
    q_ext[0:SUBLANES, :] = q_ext[tt:tt + SUBLANES, :]

    def b1(base):
        ub_ext[pl.ds(HALO + base, RC), :] = _hcol(h_ref, 4, base) * _sig(_hcol(h_ref, 5, base))
    _chunks(tt, b1)
    _build_shifts(ub_ext, sh, range(1, 8), tt + HALO - SUBLANES)

    def b2(base):
        cb = [cbb[...] + jnp.zeros((RC, GW), F32)]

        def tap(off, v):
            cb[0] = cb[0] + cbw[off - 2:off - 1, :] * v
        _for_taps(ub_ext, sh, base, range(2, 2 + KB), tap)
        cbr = cb[0].astype(BF16)
        cb_ref[pl.ds(base, RC), :] = cbr
        xhat, _ = _ln_stats(cbr.astype(F32))
        lnv = xhat * lbg[...] + lbb[...]
        z = _hcol(h_ref, 6, base)
        y_scr[pl.ds(base, RC), GW:2 * GW] = (lnv * _sig(lnv) * (z * _sig(z))).astype(BF16)
    _chunks(tt, b2, unroll=4)
    ub_ext[0:HALO, :] = ub_ext[tt:tt + HALO, :]

    def c1(base):
        cu_ext[pl.ds(16 + base, RC), :] = _hcol(h_ref, 7, base)
    _chunks(tt, c1)
    _build_shifts(cu_ext, sh, range(1, 8), tt + SUBLANES, first_lane=lambda r: 0 if r == 7 else PGD if r >= 5 else 2 * PGD)

    def c2(base):
        ic = _inv_count(cnt, base + t0 == 0)
        for g, w in enumerate(POOL_WINDOWS):
            lanes = slice(g * PGD, (g + 1) * PGD)
            acc = _tap(cu_ext, sh, 16, base, lanes)
            for j in range(1, w):
                acc = acc + _tap(cu_ext, sh, 16 - j, base, lanes)
            p = acc * ic[:, lanes] - _tap(cu_ext, sh, 16, base, lanes)
            p_scr[pl.ds(base, RC), lanes] = p.astype(BF16)
    _chunks(tt, c2, unroll=4)
    cu_ext[0:16, :] = cu_ext[tt:tt + 16, :]
    for g in range(len(POOL_WINDOWS)):
        lanes = slice(g * PGD, (g + 1) * PGD)
        pl_scr[:, lanes] = jnp.dot(p_scr[:, lanes], pw[g], preferred_element_type=F32)

    def c3(base):
        z = _hcol(h_ref, 8, base)
        yc0 = (pl_scr[pl.ds(base, RC), :] + pb[...]) * ps[...]
        y_scr[pl.ds(base, RC), 2 * GW:3 * GW] = (yc0 * (z * _sig(z))).astype(BF16)
    _chunks(tt, c3, unroll=2)


def _fwd_layer(x, win_b, wout_b, prm, ln_g, ln_b, target, *, tt, last, comm=None):
    t_len = x.shape[0]
    n_t = t_len // tt

    def body(*refs):
        refs, comm_refs = _split_comm(refs, comm, n_in, n_out)
        if last:
            (x_ref, win_hbm, wout_hbm, caw, cab, cbw, cbb, lbg, lbb, pw, pb, ps, cnt, lng, lnb, tgt_ref,
             h_ref, xb_ref, cb_ref, p_scr, ca_ref, dz_ref, dln_ref, loss_ref,
             win_v, wout_v, y_scr, o_scr, q_ext, ub_ext, cu_ext, sh, pl_scr, acc2, lacc) = refs
        else:
            (x_ref, win_hbm, wout_hbm, caw, cab, cbw, cbb, lbg, lbb, pw, pb, ps, cnt, lng, lnb,
             h_ref, xb_ref, cb_ref, p_scr, ca_ref, z_ref, xn_ref,
             win_v, wout_v, y_scr, o_scr, q_ext, ub_ext, cu_ext, sh, pl_scr) = refs
        i = pl.program_id(0)

        @pl.when(i == 0)
        def _():
            if comm is not None:
                comm["start"](*comm_refs)
            pltpu.sync_copy(win_hbm, win_v)
            pltpu.sync_copy(wout_hbm, wout_v)
            q_ext[0:SUBLANES, :] = jnp.zeros((SUBLANES, GW), F32)
            ub_ext[0:HALO, :] = jnp.zeros((HALO, GW), F32)
            cu_ext[0:16, :] = jnp.zeros((16, GW), F32)
            if last:
                acc2[...] = jnp.zeros_like(acc2)
                lacc[...] = jnp.zeros_like(lacc)

        xb_ref[...] = x_ref[...].astype(BF16)
        for j in range(NG):
            h_ref[:, j * GW:(j + 1) * GW] = jnp.dot(
                xb_ref[...], win_v[:, j * GW:(j + 1) * GW], preferred_element_type=F32).astype(BF16)

        _fwd_mixers(h_ref, cb_ref, ca_ref, y_scr, q_ext, ub_ext, cu_ext, sh, p_scr, pl_scr,
                    (caw, cab, cbw, cbb, lbg, lbb, pw, pb, ps, cnt), tt, i * tt)

        o_scr[...] = jnp.dot(y_scr[...], wout_v[...], preferred_element_type=F32)

        def post(base):
            rows = pl.ds(base, RC)
            z = ALPHA * x_ref[rows, :] + o_scr[rows, :]
            xhat, rstd = _ln_stats(z)
            xn = xhat * lng[...] + lnb[...]
            if last:
                err = xn - tgt_ref[rows, :]
                lacc[...] += _fold8(err * err)
                dxn = err * (1.0 / D_MODEL)
                acc2[0] += _fold8(dxn * xhat)
                acc2[1] += _fold8(dxn)
                dz_ref[rows, :] = _ln_bwd(dxn, xhat, rstd, lng[...])
            else:
                z_ref[rows, :] = z
                xn_ref[rows, :] = xn
        _chunks(tt, post, unroll=8)

        if last:
            @pl.when(i == n_t - 1)
            def _():
                dln_ref[...] = jnp.sum(acc2[...], axis=1)
                loss_ref[...] = jnp.zeros((SUBLANES, 128), F32) + (0.5 / D_MODEL) * jnp.sum(lacc[...])
        if comm is not None:
            @pl.when(i == n_t - 1)
            def _():
                comm["wait"](*comm_refs)

    tile = lambda c: pl.BlockSpec((tt, c), lambda i: (i, 0))
    full = lambda a: pl.BlockSpec(a.shape, lambda i: (0,) * a.ndim)
    hbm = pl.BlockSpec(memory_space=pl.ANY)
    ins = [x, win_b, wout_b, *prm, ln_g, ln_b] + ([target] if last else [])
    in_specs = [tile(D_MODEL), hbm, hbm] + [full(a) for a in (*prm, ln_g, ln_b)] + ([tile(D_MODEL)] if last else [])
    out_shape = [jax.ShapeDtypeStruct((t_len, D_IN), BF16), jax.ShapeDtypeStruct((t_len, D_MODEL), BF16),
                 jax.ShapeDtypeStruct((t_len, GW), BF16), jax.ShapeDtypeStruct((t_len, GW), BF16),
                 jax.ShapeDtypeStruct((t_len, GW), BF16)]
    out_specs = [tile(D_IN), tile(D_MODEL), tile(GW), tile(GW), tile(GW)]
    if last:
        out_shape += [jax.ShapeDtypeStruct((t_len, D_MODEL), F32), jax.ShapeDtypeStruct((2, D_MODEL), F32),
                      jax.ShapeDtypeStruct((SUBLANES, 128), F32)]
        out_specs += [tile(D_MODEL), pl.BlockSpec((2, D_MODEL), lambda i: (0, 0)),
                      pl.BlockSpec((SUBLANES, 128), lambda i: (0, 0))]
    else:
        out_shape += [jax.ShapeDtypeStruct((t_len, D_MODEL), F32), jax.ShapeDtypeStruct((t_len, D_MODEL), F32)]
        out_specs += [tile(D_MODEL), tile(D_MODEL)]
    scratch = [
        pltpu.VMEM((D_MODEL, D_IN), BF16), pltpu.VMEM((D_MIX, D_MODEL), BF16),
        pltpu.VMEM((tt, D_MIX), BF16), pltpu.VMEM((tt, D_MODEL), F32),
        pltpu.VMEM((tt + SUBLANES, GW), F32), pltpu.VMEM((tt + HALO, GW), F32), pltpu.VMEM((tt + 16, GW), F32),
        pltpu.VMEM((SUBLANES, tt + HALO, GW), F32),
        pltpu.VMEM((tt, GW), F32),
    ]
    if last:
        scratch += [pltpu.VMEM((2, SUBLANES, D_MODEL), F32), pltpu.VMEM((SUBLANES, D_MODEL), F32)]
    n_in, n_out = len(ins), len(out_shape)
    ins, in_specs, out_shape, out_specs, scratch = _with_comm(comm, ins, in_specs, out_shape, out_specs, scratch)
    return pl.pallas_call(
        body, name=("fwd_last" if last else "fwd_layer") + ("" if comm is None else "_comm"), grid=(n_t,),
        in_specs=in_specs, out_specs=out_specs, out_shape=out_shape, scratch_shapes=scratch,
        compiler_params=pltpu.CompilerParams(dimension_semantics=("arbitrary",), vmem_limit_bytes=VMEM_LIMIT),
    )(*ins)


def _dsilu(z, sz):
    return sz * (1.0 + z * (1.0 - sz))


def _bwd_layer(dz, h, cb, p, ca, win_b, wout_b, prm, z_prev, lng_prev, *, tt, comm=None):
    t_len = dz.shape[0]
    n_t = t_len // tt
    has_prev = z_prev is not None
    group = min(WOUT_GROUP, n_t)
    assert n_t % group == 0

    def body(*refs):
        refs, comm_refs = _split_comm(refs, comm, n_in, n_out)
        dz_ref, h_ref, cb_ref, p_scr, ca_ref, win_hbm, wout_hbm, caw, cab, cbw, cbb, lbg, lbb, pw, pb, ps, cnt = refs[:17]
        k = 17
        if has_prev:
            zp_ref, lngp = refs[k:k + 2]
            k += 2
        dxo_ref, dh_ref, dwout_hbm, small_ref, dpw_ref = refs[k:k + 5]
        k += 5
        if has_prev:
            dlnp_ref = refs[k]
            k += 1
        (win_v, wout_v, dzb_all, dy_scr, y_all, dx_scr, dca_ext, dcb_ext, dpn_ext, sh,
         pl_scr, dpl_scr, dp_scr, racc, dpw_acc, dwout_acc) = refs[k:k + 16]
        k += 16
        if has_prev:
            acc2 = refs[k]
        i = pl.program_id(0)
        ti = n_t - 1 - i
        t0 = ti * tt
        slot = i % group
        slot_rows = pl.ds(pl.multiple_of(slot * tt, tt), tt)
        dzb, y_scr = dzb_all.at[slot_rows], y_all.at[slot_rows]

        @pl.when(i == 0)
        def _():
            if comm is not None:
                comm["start"](*comm_refs)
            pltpu.sync_copy(win_hbm, win_v)
            pltpu.sync_copy(wout_hbm, wout_v)
            dca_ext[tt:tt + SUBLANES, :] = jnp.zeros((SUBLANES, GW), F32)
            dcb_ext[tt:tt + HALO, :] = jnp.zeros((HALO, GW), F32)
            dpn_ext[tt:tt + 16, :] = jnp.zeros((16, GW), F32)
            racc[...] = jnp.zeros_like(racc)
            dpw_acc[...] = jnp.zeros_like(dpw_acc)
            dwout_acc[...] = jnp.zeros_like(dwout_acc)
            if has_prev:
                acc2[...] = jnp.zeros_like(acc2)

        dzb[...] = dz_ref[...].astype(BF16)
        dy_scr[...] = lax.dot_general(dzb[...], wout_v[...], (((1,), (1,)), ((), ())), preferred_element_type=F32)

        def a2(base):
            rows = pl.ds(base, RC)
            ca = ca_ref[rows, :].astype(F32)
            bg, z = _hcol(h_ref, 0, base), _hcol(h_ref, 3, base)
            sz = _sig(z)
            sza = z * sz
            dya = dy_scr[rows, 0:GW]
            ya0 = bg * ca
            y_scr[rows, 0:GW] = (ya0 * sza).astype(BF16)
            dya0 = dya * sza
            dh_ref[rows, 3 * GW:4 * GW] = (dya * ya0 * _dsilu(z, sz)).astype(BF16)
            dh_ref[rows, 0:GW] = (dya0 * ca).astype(BF16)
            dca = dya0 * bg
            dca_ext[rows, :] = dca
            racc[R_DCAB] += _fold8(dca)
        _chunks(tt, a2)
        _build_shifts(dca_ext, sh, (1, 2), tt)

        def a3(base):
            rows = pl.ds(base, RC)
            wins = [_tap(dca_ext, sh, 2 - kk, base) for kk in range(KA)]
            cg, v = _hcol(h_ref, 1, base), _hcol(h_ref, 2, base)
            q = cg * v
            dq = caw[0:1, :] * wins[0] + caw[1:2, :] * wins[1] + caw[2:3, :] * wins[2]
            for kk in range(KA):
                racc[R_DWA + kk] += _fold8(q * wins[kk])
            dh_ref[rows, GW:2 * GW] = (dq * v).astype(BF16)
            dh_ref[rows, 2 * GW:3 * GW] = (dq * cg).astype(BF16)
        _chunks(tt, a3)
        dca_ext[tt:tt + SUBLANES, :] = dca_ext[0:SUBLANES, :]

        def b2(base):
            rows = pl.ds(base, RC)
            xhat, rstd = _ln_stats(cb_ref[rows, :].astype(F32))
            lnv = xhat * lbg[...] + lbb[...]
            sl = _sig(lnv)
            s = lnv * sl
            z = _hcol(h_ref, 6, base)
            sz = _sig(z)
            szb = z * sz
            y_scr[rows, GW:2 * GW] = (s * szb).astype(BF16)
            dyb = dy_scr[rows, GW:2 * GW]
            dh_ref[rows, 6 * GW:7 * GW] = (dyb * s * _dsilu(z, sz)).astype(BF16)
            dlnv = dyb * szb * _dsilu(lnv, sl)
            racc[R_DLBG] += _fold8(dlnv * xhat)
            racc[R_DLBB] += _fold8(dlnv)
            dcb = _ln_bwd(dlnv, xhat, rstd, lbg[...])
            dcb_ext[rows, :] = dcb
            racc[R_DCBB] += _fold8(dcb)
        _chunks(tt, b2, unroll=4)
        _build_shifts(dcb_ext, sh, range(1, 8), tt + HALO - SUBLANES)

        def b3(base):
            rows = pl.ds(base, RC)
            for half in range(2):
                lanes = slice(half * GW // 2, (half + 1) * GW // 2)
                v = h_ref[rows, 4 * GW + lanes.start:4 * GW + lanes.stop].astype(F32)
                gt = h_ref[rows, 5 * GW + lanes.start:5 * GW + lanes.stop].astype(F32)
                sg = _sig(gt)
                ub = v * sg
                dubv = [jnp.zeros((RC, GW // 2), F32)]

                def tap(off, win):
                    dubv[0] = dubv[0] + cbw[KB - 1 - off:KB - off, lanes] * win
                    racc[R_DWB + KB - 1 - off, :, lanes] += _fold8(ub * win)
                _for_taps(dcb_ext, sh, base, range(KB), tap, lanes)
                dub = dubv[0]
                dh_ref[rows, 4 * GW + lanes.start:4 * GW + lanes.stop] = (dub * sg).astype(BF16)
                dh_ref[rows, 5 * GW + lanes.start:5 * GW + lanes.stop] = (dub * v * sg * (1.0 - sg)).astype(BF16)
        _chunks(tt, b3, unroll=4)
        dcb_ext[tt:tt + HALO, :] = dcb_ext[0:HALO, :]

        for g in range(len(POOL_WINDOWS)):
            lanes = slice(g * PGD, (g + 1) * PGD)
            pl_scr[:, lanes] = jnp.dot(p_scr[:, lanes], pw[g], preferred_element_type=F32)

        def c3(base):
            rows = pl.ds(base, RC)
            z = _hcol(h_ref, 8, base)
            sz = _sig(z)
            szc = z * sz
            plb = pl_scr[rows, :] + pb[...]
            yc0 = plb * ps[...]
            y_scr[rows, 2 * GW:3 * GW] = (yc0 * szc).astype(BF16)
            dyc = dy_scr[rows, 2 * GW:3 * GW]
            dh_ref[rows, 8 * GW:9 * GW] = (dyc * yc0 * _dsilu(z, sz)).astype(BF16)
            dyc0 = dyc * szc
            racc[R_DPS] += _fold8(dyc0 * plb)
            dpl = dyc0 * ps[...]
            racc[R_DPB] += _fold8(dpl)
            dpl_scr[rows, :] = dpl.astype(BF16)
        _chunks(tt, c3, unroll=4)
        for g in range(len(POOL_WINDOWS)):
            lanes = slice(g * PGD, (g + 1) * PGD)
            dpw_acc[g] += lax.dot_general(p_scr[:, lanes], dpl_scr[:, lanes], (((0,), (0,)), ((), ())),
                                          preferred_element_type=F32)
            dp_scr[:, lanes] = lax.dot_general(dpl_scr[:, lanes], pw[g], (((1,), (1,)), ((), ())),
                                               preferred_element_type=F32)

        def c4(base):
            rows = pl.ds(base, RC)
            dpn_ext[rows, :] = dp_scr[rows, :] * _inv_count(cnt, base + t0 == 0)
        _chunks(tt, c4)
        _build_shifts(dpn_ext, sh, range(1, 8), tt + SUBLANES, first_lane=lambda r: 0 if r == 1 else PGD if r <= 3 else 2 * PGD)

        def c5(base):
            rows = pl.ds(base, RC)
            for g, w in enumerate(POOL_WINDOWS):
                lanes = slice(g * PGD, (g + 1) * PGD)
                acc = _tap(dpn_ext, sh, 0, base, lanes)
                for j in range(1, w):
                    acc = acc + _tap(dpn_ext, sh, j, base, lanes)
                dh_ref[rows, 7 * GW + g * PGD:7 * GW + (g + 1) * PGD] = (acc - dp_scr[rows, lanes]).astype(BF16)
        _chunks(tt, c5, unroll=4)
        dpn_ext[tt:tt + 16, :] = dpn_ext[0:16, :]

        @pl.when(slot == group - 1)
        def _():
            for r in range(D_MIX // GW):
                dwout_acc[r * GW:(r + 1) * GW, :] += lax.dot_general(
                    y_all[:, r * GW:(r + 1) * GW], dzb_all[...], (((0,), (0,)), ((), ())), preferred_element_type=F32)
        dx_scr[...] = lax.dot_general(dh_ref[...], win_v[...], (((1,), (1,)), ((), ())), preferred_element_type=F32)

        def post(base):
            rows = pl.ds(base, RC)
            dx = ALPHA * dz_ref[rows, :] + dx_scr[rows, :]
            if has_prev:
                xhat, rstd = _ln_stats(zp_ref[rows, :])
                acc2[0] += _fold8(dx * xhat)
                acc2[1] += _fold8(dx)
                dxo_ref[rows, :] = _ln_bwd(dx, xhat, rstd, lngp[...])
            else:
                dxo_ref[rows, :] = dx
        _chunks(tt, post, unroll=8)

        @pl.when(i == n_t - 1)
        def _():
            small_ref[...] = jnp.sum(racc[...], axis=1)
            dpw_ref[...] = dpw_acc[...]
            pltpu.sync_copy(dwout_acc, dwout_hbm)
            if has_prev:
                dlnp_ref[...] = jnp.sum(acc2[...], axis=1)
            if comm is not None:
                comm["wait"](*comm_refs)

    rtile = lambda c: pl.BlockSpec((tt, c), lambda i: (n_t - 1 - i, 0))
    full = lambda a: pl.BlockSpec(a.shape, lambda i: (0,) * a.ndim)
    const = lambda shp: pl.BlockSpec(shp, lambda i: (0,) * len(shp))
    hbm = pl.BlockSpec(memory_space=pl.ANY)
    ins = [dz, h, cb, p, ca, win_b, wout_b, *prm] + ([z_prev, lng_prev] if has_prev else [])
    in_specs = [rtile(D_MODEL), rtile(D_IN), rtile(GW), rtile(GW), rtile(GW), hbm, hbm] + [full(a) for a in prm] \
        + ([rtile(D_MODEL), full(lng_prev)] if has_prev else [])
    out_shape = [jax.ShapeDtypeStruct((t_len, D_MODEL), F32), jax.ShapeDtypeStruct((t_len, D_IN), BF16),
                 jax.ShapeDtypeStruct((D_MIX, D_MODEL), F32), jax.ShapeDtypeStruct((N_RACC, GW), F32),
                 jax.ShapeDtypeStruct((len(POOL_WINDOWS), PGD, PGD), F32)]
    out_specs = [rtile(D_MODEL), rtile(D_IN), hbm, const((N_RACC, GW)), const((len(POOL_WINDOWS), PGD, PGD))]
    if has_prev:
        out_shape.append(jax.ShapeDtypeStruct((2, D_MODEL), F32))
        out_specs.append(const((2, D_MODEL)))
    scratch = [
        pltpu.VMEM((D_MODEL, D_IN), BF16), pltpu.VMEM((D_MIX, D_MODEL), BF16),
        pltpu.VMEM((group * tt, D_MODEL), BF16), pltpu.VMEM((tt, D_MIX), F32), pltpu.VMEM((group * tt, D_MIX), BF16),
        pltpu.VMEM((tt, D_MODEL), F32),
        pltpu.VMEM((tt + SUBLANES, GW), F32), pltpu.VMEM((tt + HALO, GW), F32), pltpu.VMEM((tt + 16, GW), F32),
        pltpu.VMEM((SUBLANES, tt + HALO, GW), F32),
        pltpu.VMEM((tt, GW), F32), pltpu.VMEM((tt, GW), BF16), pltpu.VMEM((tt, GW), F32),
        pltpu.VMEM((N_RACC, SUBLANES, GW), F32), pltpu.VMEM((len(POOL_WINDOWS), PGD, PGD), F32),
        pltpu.VMEM((D_MIX, D_MODEL), F32),
    ]
    if has_prev:
        scratch.append(pltpu.VMEM((2, SUBLANES, D_MODEL), F32))
    n_in, n_out = len(ins), len(out_shape)
    ins, in_specs, out_shape, out_specs, scratch = _with_comm(comm, ins, in_specs, out_shape, out_specs, scratch)
    return pl.pallas_call(
        body, name=("bwd_layer_prev" if has_prev else "bwd_layer") + ("" if comm is None else "_comm"), grid=(n_t,),
        in_specs=in_specs, out_specs=out_specs, out_shape=out_shape, scratch_shapes=scratch,
        compiler_params=pltpu.CompilerParams(dimension_semantics=("arbitrary",), vmem_limit_bytes=VMEM_LIMIT),
    )(*ins)


def _wgrad_in(xb, dh, *, tk, comm=None):
    t_len = xb.shape[0]
    tk = min(tk, t_len)
    n_k = t_len // tk

    def body(*refs):
        (x_ref, dh_ref, o_ref), comm_refs = _split_comm(refs, comm, 2, 1)
        j, k = pl.program_id(0), pl.program_id(1)

        @pl.when(k == 0)
        def _():
            o_ref[...] = jnp.zeros_like(o_ref)
        if comm is not None:
            @pl.when((j == 0) & (k == 0))
            def _():
                comm["start"](*comm_refs)
        o_ref[0] += lax.dot_general(x_ref[...], dh_ref[...], (((0,), (0,)), ((), ())), preferred_element_type=F32)
        if comm is not None:
            @pl.when((j == N_CHIP - 1) & (k == n_k - 1))
            def _():
                comm["wait"](*comm_refs)

    ins = [xb, dh]
    in_specs = [pl.BlockSpec((tk, D_MODEL), lambda j, k: (k, 0)), pl.BlockSpec((tk, SHARD_IN), lambda j, k: (k, j))]
    out_shape = [jax.ShapeDtypeStruct((N_CHIP, D_MODEL, SHARD_IN), F32)]
    out_specs = [pl.BlockSpec((1, D_MODEL, SHARD_IN), lambda j, k: (j, 0, 0))]
    ins, in_specs, out_shape, out_specs, scratch = _with_comm(comm, ins, in_specs, out_shape, out_specs, [])
    outs = pl.pallas_call(
        body, name="wgrad_in" + ("" if comm is None else "_comm"), grid=(N_CHIP, n_k),
        in_specs=in_specs, out_specs=out_specs, out_shape=out_shape, scratch_shapes=scratch,
        compiler_params=pltpu.CompilerParams(dimension_semantics=("arbitrary", "arbitrary"), vmem_limit_bytes=VMEM_LIMIT),
    )(*ins)
    return outs[0] if comm is None else outs


MESH = pl.DeviceIdType.MESH
ANY = pl.BlockSpec(memory_space=pl.ANY)


def _place():
    x, y, c = lax.axis_index("x"), lax.axis_index("y"), lax.axis_index("c")
    others = [(1 - x, y), (x, 1 - y), (1 - x, 1 - y)]
    return x, y, c, 2 * x + y, [(ox, oy, 2 * ox + oy) for ox, oy in others]


def _rcopy(src, dst, send_sems, recv_sems, k, dev):
    return pltpu.make_async_remote_copy(src_ref=src, dst_ref=dst, send_sem=send_sems.at[k], recv_sem=recv_sems.at[k],
                                        device_id=dev, device_id_type=MESH)


def _gather_weights(w_in, w_out, cw):
    hi, ho = D_MODEL // 2, SHARD_OUT // 2

    def body(win_ref, wout_ref, cw_ref, owin, owout, ocw, bin_v, bout_v, send_sems, recv_sems, lsem):
        x, y, c, me, others = _place()

        def cast(l):
            for r0 in range(0, D_MODEL, 256):
                bin_v[l, r0:r0 + 256, :] = win_ref[l, r0:r0 + 256, :].astype(BF16)
            bout_v[l] = wout_ref[l].astype(BF16)
        cast(0)
        cin = pl.ds(pl.multiple_of(me * SHARD_IN, 128), SHARD_IN)
        rout = pl.ds(pl.multiple_of(me * SHARD_OUT, 128), SHARD_OUT)
        local = [pltpu.make_async_copy(bin_v.at[0], owin.at[:, cin], lsem.at[0]),
                 pltpu.make_async_copy(bout_v.at[0], owout.at[rout, :], lsem.at[1]),
                 pltpu.make_async_copy(cw_ref, ocw.at[me], lsem.at[2])]
        for cp in local:
            cp.start()

        def in_half(chip, core):
            return owin.at[pl.ds(pl.multiple_of(core * hi, 256), hi), pl.ds(pl.multiple_of(chip * SHARD_IN, 128), SHARD_IN)]

        def out_half(chip, core):
            return owout.at[pl.ds(pl.multiple_of(chip * SHARD_OUT + core * ho, 64), ho), :]

        first = []
        for k, (ox, oy, _) in enumerate(others):
            dev = (ox, oy, c)
            first.append(_rcopy(bin_v.at[0, pl.ds(pl.multiple_of(c * hi, 256), hi), :], in_half(me, c), send_sems, recv_sems, k, dev))
            first.append(_rcopy(bout_v.at[0, pl.ds(pl.multiple_of(c * ho, 64), ho), :], out_half(me, c), send_sems, recv_sems, 3 + k, dev))
            first.append(_rcopy(cw_ref, ocw.at[me], send_sems, recv_sems, 6 + k, dev))
        for cp in first:
            cp.start()
        cast(1)
        sib = (x, y, 1 - c)
        passed = []
        for k, (ox, oy, oc) in enumerate(others):
            _rcopy(in_half(oc, c), in_half(oc, c), send_sems, recv_sems, k, sib).wait_recv()
            fwd_in = _rcopy(in_half(oc, c), in_half(oc, c), send_sems, recv_sems, 9 + k, sib)
            fwd_in.start()
            _rcopy(out_half(oc, c), out_half(oc, c), send_sems, recv_sems, 3 + k, sib).wait_recv()
            fwd_out = _rcopy(out_half(oc, c), out_half(oc, c), send_sems, recv_sems, 12 + k, sib)
            fwd_out.start()
            passed += [fwd_in, fwd_out]
        for k, (ox, oy, oc) in enumerate(others):
            _rcopy(cw_ref, ocw.at[oc], send_sems, recv_sems, 6 + k, sib).wait_recv()
            _rcopy(in_half(oc, 1 - c), in_half(oc, 1 - c), send_sems, recv_sems, 9 + k, sib).wait_recv()
            _rcopy(out_half(oc, 1 - c), out_half(oc, 1 - c), send_sems, recv_sems, 12 + k, sib).wait_recv()
        for cp in first + passed:
            cp.wait_send()
        for cp in local:
            cp.wait()

    vm = pl.BlockSpec(memory_space=pltpu.VMEM)
    return pl.pallas_call(
        body, name="gather_weights",
        in_specs=[vm, vm, vm], out_specs=[ANY, ANY, ANY, vm, vm],
        out_shape=[jax.ShapeDtypeStruct((D_MODEL, D_IN), BF16), jax.ShapeDtypeStruct((D_MIX, D_MODEL), BF16),
                   jax.ShapeDtypeStruct((N_CHIP,) + cw.shape, F32),
                   jax.ShapeDtypeStruct((DEPTH, D_MODEL, SHARD_IN), BF16), jax.ShapeDtypeStruct((DEPTH, SHARD_OUT, D_MODEL), BF16)],
        scratch_shapes=[pltpu.SemaphoreType.DMA((15,)), pltpu.SemaphoreType.DMA((15,)), pltpu.SemaphoreType.DMA((3,))],
        compiler_params=pltpu.CompilerParams(vmem_limit_bytes=VMEM_LIMIT),
    )(w_in, w_out, cw)


def _gather_starts(bsh_in, bsh_out, owin, owout, send_sems, recv_sems, lsem, layer):
    x, y, c, me, others = _place()
    hi, ho = D_MODEL // 2, SHARD_OUT // 2
    pltpu.make_async_copy(bsh_in.at[layer], owin.at[:, pl.ds(pl.multiple_of(me * SHARD_IN, 128), SHARD_IN)], lsem.at[0]).start()
    pltpu.make_async_copy(bsh_out.at[layer], owout.at[pl.ds(pl.multiple_of(me * SHARD_OUT, 128), SHARD_OUT), :], lsem.at[1]).start()
    for k, (ox, oy, _) in enumerate(others):
        for t in range(2):
            pltpu.make_async_remote_copy(
                src_ref=bsh_in.at[layer, pl.ds(pl.multiple_of(c * hi, 256), hi), :],
                dst_ref=owin.at[pl.ds(pl.multiple_of(c * hi, 256), hi), pl.ds(pl.multiple_of(me * SHARD_IN, 128), SHARD_IN)],
                send_sem=send_sems.at[2 * k + t], recv_sem=recv_sems.at[2 * k + c], device_id=(ox, oy, t), device_id_type=MESH).start()
            pltpu.make_async_remote_copy(
                src_ref=bsh_out.at[layer, pl.ds(pl.multiple_of(c * ho, 64), ho), :],
                dst_ref=owout.at[pl.ds(pl.multiple_of(me * SHARD_OUT + c * ho, 64), ho), :],
                send_sem=send_sems.at[6 + 2 * k + t], recv_sem=recv_sems.at[6 + 2 * k + c], device_id=(ox, oy, t), device_id_type=MESH).start()


def _gather_waits(bsh_in, bsh_out, owin, owout, send_sems, recv_sems, lsem, layer):
    x, y, c, me, others = _place()
    hi, ho = D_MODEL // 2, SHARD_OUT // 2
    src_in = bsh_in.at[layer, pl.ds(0, hi), :]
    src_out = bsh_out.at[layer, pl.ds(0, ho), :]
    for k, (ox, oy, oc) in enumerate(others):
        for t in range(2):
            dst_in = owin.at[pl.ds(t * hi, hi), pl.ds(pl.multiple_of(oc * SHARD_IN, 128), SHARD_IN)]
            dst_out = owout.at[pl.ds(pl.multiple_of(oc * SHARD_OUT + t * ho, 64), ho), :]
            a = pltpu.make_async_remote_copy(src_ref=src_in, dst_ref=dst_in, send_sem=send_sems.at[2 * k + t],
                                             recv_sem=recv_sems.at[2 * k + t], device_id=(ox, oy, t), device_id_type=MESH)
            b = pltpu.make_async_remote_copy(src_ref=src_out, dst_ref=dst_out, send_sem=send_sems.at[6 + 2 * k + t],
                                             recv_sem=recv_sems.at[6 + 2 * k + t], device_id=(ox, oy, t), device_id_type=MESH)
            a.wait_send()
            a.wait_recv()
            b.wait_send()
            b.wait_recv()
    pltpu.make_async_copy(bsh_in.at[layer], owin.at[:, pl.ds(pl.multiple_of(me * SHARD_IN, 128), SHARD_IN)], lsem.at[0]).wait()
    pltpu.make_async_copy(bsh_out.at[layer], owout.at[pl.ds(pl.multiple_of(me * SHARD_OUT, 128), SHARD_OUT), :], lsem.at[1]).wait()


def _gather_comm(bsh_in, bsh_out, layer):
    return dict(ins=[bsh_in, bsh_out],
                out_shape=[jax.ShapeDtypeStruct((D_MODEL, D_IN), BF16), jax.ShapeDtypeStruct((D_MIX, D_MODEL), BF16)],
                sems=[pltpu.SemaphoreType.DMA((12,)), pltpu.SemaphoreType.DMA((12,)), pltpu.SemaphoreType.DMA((2,))],
                start=lambda ins, outs, sems: _gather_starts(ins[0], ins[1], outs[0], outs[1], *sems, layer),
                wait=lambda ins, outs, sems: _gather_waits(ins[0], ins[1], outs[0], outs[1], *sems, layer))


def _exchange_halves(arrs, tag):
    n = len(arrs)

    def body(*refs):
        ins, outs, (send_sems, recv_sems) = refs[:n], refs[n:2 * n], refs[2 * n:]
        x, y, c, _, _ = _place()
        cps = []
        for m in range(n):
            half = ins[m].shape[1] // 2
            cps.append(_rcopy(ins[m].at[:, pl.ds(pl.multiple_of((1 - c) * half, SUBLANES), half), :], outs[m],
                              send_sems, recv_sems, m, (x, y, 1 - c)))
        for cp in cps:
            cp.start()
        for cp in cps:
            cp.wait()

    return pl.pallas_call(
        body, name="exchange_halves_" + tag, in_specs=[ANY] * n, out_specs=[ANY] * n,
        out_shape=[jax.ShapeDtypeStruct((a.shape[0], a.shape[1] // 2, a.shape[2]), F32) for a in arrs],
        scratch_shapes=[pltpu.SemaphoreType.DMA((n,)), pltpu.SemaphoreType.DMA((n,))],
    )(*arrs)


def _add_own_half(a, got, core, *, rb, dtype):
    nj, r, cdim = a.shape
    half = r // 2

    def body(core_ref, a_ref, g_ref, o_ref):
        o_ref[...] = (a_ref[0] + g_ref[...]).astype(dtype)

    return pl.pallas_call(
        body, name="add_own_half",
        grid_spec=pltpu.PrefetchScalarGridSpec(
            num_scalar_prefetch=1, grid=(nj, half // rb),
            in_specs=[pl.BlockSpec((1, 1, rb, cdim), lambda j, i, cr: (j, cr[0], i, 0)),
                      pl.BlockSpec((1, rb, cdim), lambda j, i, cr: (j, i, 0))],
            out_specs=pl.BlockSpec((1, rb, cdim), lambda j, i, cr: (j, i, 0))),
        out_shape=jax.ShapeDtypeStruct((nj, half, cdim), dtype),
    )(core, a.reshape(nj, 2, half, cdim), got)


def _owner_starts(ins, outs, sems):
    send_sems, recv_sems, lsem = sems
    x, y, c, me, others = _place()
    for m in range(len(ins)):
        pltpu.make_async_copy(ins[m].at[me], outs[m].at[me], lsem.at[m]).start()
        for k, (ox, oy, oc) in enumerate(others):
            _rcopy(ins[m].at[oc], outs[m].at[me], send_sems, recv_sems, 3 * m + k, (ox, oy, c)).start()


def _owner_waits(ins, outs, sems):
    send_sems, recv_sems, lsem = sems
    x, y, c, me, others = _place()
    for m in range(len(ins)):
        for k, (ox, oy, oc) in enumerate(others):
            _rcopy(ins[m].at[oc], outs[m].at[oc], send_sems, recv_sems, 3 * m + k, (ox, oy, c)).wait()
        pltpu.make_async_copy(ins[m].at[me], outs[m].at[me], lsem.at[m]).wait()


def _owner_comm(arrs):
    n = len(arrs)
    return dict(ins=arrs, out_shape=[jax.ShapeDtypeStruct(a.shape, a.dtype) for a in arrs],
                sems=[pltpu.SemaphoreType.DMA((3 * n,)), pltpu.SemaphoreType.DMA((3 * n,)), pltpu.SemaphoreType.DMA((n,))],
                start=_owner_starts, wait=_owner_waits)


def _send_to_owners(arrs):
    n = len(arrs)

    def body(*refs):
        ins, outs, sems = refs[:n], refs[n:2 * n], refs[2 * n:]
        _owner_starts(ins, outs, sems)
        _owner_waits(ins, outs, sems)

    job = _owner_comm(arrs)
    return pl.pallas_call(
        body, name="send_to_owners", in_specs=[ANY] * n, out_specs=[ANY] * n,
        out_shape=job["out_shape"], scratch_shapes=job["sems"],
    )(*arrs)


def _sum_chips(a, *, rb):
    nj, r, cdim = a.shape

    def body(a_ref, o_ref):
        f = lambda k: a_ref[k].astype(F32)
        o_ref[...] = ((f(0) + f(1)) + f(2)) + f(3)

    return pl.pallas_call(
        body, name="sum_chips", grid=(r // rb,),
        in_specs=[pl.BlockSpec((nj, rb, cdim), lambda i: (0, i, 0))],
        out_specs=pl.BlockSpec((rb, cdim), lambda i: (i, 0)),
        out_shape=jax.ShapeDtypeStruct((r, cdim), F32),
    )(a)


def _sum_chips_into(a, dest, layer, core, *, rb):
    nj, half, cdim = a.shape
    nb = half // rb

    def body(*refs):
        a_ref, o_ref = refs[1], refs[-1]
        f = lambda k: a_ref[k].astype(F32)
        o_ref[0] = ((f(0) + f(1)) + f(2)) + f(3)

    grid_spec = pltpu.PrefetchScalarGridSpec(
        num_scalar_prefetch=1, grid=(nb,),
        in_specs=[pl.BlockSpec((nj, rb, cdim), lambda i, cr: (0, i, 0))] + ([] if dest is None else [ANY]),
        out_specs=pl.BlockSpec((1, rb, cdim), lambda i, cr: (layer, cr[0] * nb + i, 0)))
    return pl.pallas_call(
        body, name="sum_chips_into", grid_spec=grid_spec,
        out_shape=jax.ShapeDtypeStruct((DEPTH, 2 * half, cdim), F32),
        input_output_aliases={} if dest is None else {2: 0},
    )(*([core, a] if dest is None else [core, a, dest]))


def _spread_reduced(g_in, g_out, red_small):
    hs = red_small.shape[0]

    def body(gin_in, gout_in, sm, gin, gout, fsm, gsm, send_sems, recv_sems, lsem):
        x, y, c, me, others = _place()
        sib = (x, y, 1 - c)
        hi, ho = D_MODEL // 2, SHARD_OUT // 2
        ri, ro = pl.ds(pl.multiple_of(c * hi, SUBLANES), hi), pl.ds(pl.multiple_of(c * ho, SUBLANES), ho)
        remote = [_rcopy(gin.at[:, ri, :], gin.at[:, ri, :], send_sems, recv_sems, 0, sib),
                  _rcopy(gout.at[:, ro, :], gout.at[:, ro, :], send_sems, recv_sems, 1, sib)]
        own_small = pltpu.make_async_copy(sm, gsm.at[me], lsem.at[0])
        small = [_rcopy(sm, gsm.at[me], send_sems, recv_sems, 2 + k, (ox, oy, c)) for k, (ox, oy, _) in enumerate(others)]
        for cp in remote + [own_small] + small:
            cp.start()
        own_small.wait()
        for cp in small:
            cp.wait()
        mine = fsm.at[:, pl.ds(pl.multiple_of(c * hs, SUBLANES), hs), :]
        keep = pltpu.make_async_copy(gsm, mine, lsem.at[1])
        give = _rcopy(gsm, mine, send_sems, recv_sems, 5, sib)
        keep.start()
        give.start()
        for cp in remote + [give]:
            cp.wait()
        keep.wait()

    return pl.pallas_call(
        body, name="spread_reduced", in_specs=[ANY] * 3, out_specs=[ANY] * 4,
        out_shape=[jax.ShapeDtypeStruct(g_in.shape, F32), jax.ShapeDtypeStruct(g_out.shape, F32),
                   jax.ShapeDtypeStruct((N_CHIP, 2 * hs, GW), F32), jax.ShapeDtypeStruct((N_CHIP, hs, GW), F32)],
        input_output_aliases={0: 0, 1: 1},
        scratch_shapes=[pltpu.SemaphoreType.DMA((6,)), pltpu.SemaphoreType.DMA((6,)), pltpu.SemaphoreType.DMA((2,))],
    )(g_in, g_out, red_small)[:3]


def _adamw_math(w, g, m, v):
    m = ADAM_B1 * m + (1.0 - ADAM_B1) * g
    v = ADAM_B2 * v + (1.0 - ADAM_B2) * (g * g)
    m_hat = m / (1.0 - ADAM_B1 ** ADAM_STEP)
    v_hat = v / (1.0 - ADAM_B2 ** ADAM_STEP)
    delta = -ADAM_LR * (m_hat / (jnp.sqrt(v_hat) + ADAM_EPS) + ADAM_WD * w)
    return delta, m, v


def _adamw_big(w, g, m, v, *, rb):
    r, cdim = w.shape

    def body(w_ref, g_ref, m_ref, v_ref, d_ref, nm_ref, nv_ref, go_ref):
        g = g_ref[...]
        d_ref[...], nm_ref[...], nv_ref[...] = _adamw_math(w_ref[...], g, m_ref[...], v_ref[...])
        go_ref[...] = g

    spec = pl.BlockSpec((rb, cdim), lambda i: (i, 0))
    return pl.pallas_call(
        body, name="adamw_big", grid=(r // rb,), in_specs=[spec] * 4, out_specs=[spec] * 4,
        out_shape=[jax.ShapeDtypeStruct((r, cdim), F32)] * 4,
    )(w, g, m, v)


def _adamw_small(ws, gs, ms, vs):
    n = len(ws)

    def body(*refs):
        w, g, m, v = refs[:n], refs[n:2 * n], refs[2 * n:3 * n], refs[3 * n:4 * n]
        d, nm, nv = refs[4 * n:5 * n], refs[5 * n:6 * n], refs[6 * n:7 * n]
        for k in range(n):
            d[k][...], nm[k][...], nv[k][...] = _adamw_math(w[k][...], g[k][...], m[k][...], v[k][...])

    shapes = [jax.ShapeDtypeStruct(a.shape, F32) for a in ws]
    outs = pl.pallas_call(body, name="adamw_small", out_shape=shapes * 3)(*ws, *gs, *ms, *vs)
    return outs[:n], outs[n:2 * n], outs[2 * n:]


TT = 256
TK = 4096
CW_ROWS = 40
PACK_ROWS = 192


def _pack(rows):
    packed = jnp.concatenate(rows, axis=0)
    packed = jnp.pad(packed, ((0, PACK_ROWS - packed.shape[0]), (0, 0)))
    return packed.reshape(N_CHIP, PACK_ROWS // N_CHIP, GW)


def _reduce_to_owner_halves(parts, core1, tag):
    got = _exchange_halves(parts, tag)
    rbs = {D_MODEL: 256, SHARD_OUT: SHARD_OUT // 2, PACK_ROWS // N_CHIP: PACK_ROWS // N_CHIP // 2}
    return [_add_own_half(a, g, core1, rb=rbs[a.shape[1]], dtype=F32 if a.shape[1] == PACK_ROWS // N_CHIP else BF16)
            for a, g in zip(parts, got)]


def kernel(x, w_in, conv_a_w, conv_a_b, conv_b_w, conv_b_b, ln_b_g, ln_b_b, pool_w, pool_b, pool_scale, w_out, ln_g, ln_b, loss_target, m_w_in, m_conv_a_w, m_conv_a_b, m_conv_b_w, m_conv_b_b, m_ln_b_g, m_ln_b_b, m_pool_w, m_pool_b, m_pool_scale, m_w_out, m_ln_g, m_ln_b, v_w_in, v_conv_a_w, v_conv_a_b, v_conv_b_w, v_conv_b_b, v_ln_b_g, v_ln_b_b, v_pool_w, v_pool_b, v_pool_scale, v_w_out, v_ln_g, v_ln_b):
    chip = 2 * lax.axis_index("x") + lax.axis_index("y")
    core1 = lax.axis_index("c").reshape(1).astype(jnp.int32)
    x2, tgt = x[0], loss_target[0]

    cw = jnp.zeros((DEPTH, CW_ROWS, PGD), F32).at[:, 0:KA].set(conv_a_w).at[:, 8:8 + KB].set(conv_b_w)
    win0_b, wout0_b, cw_all, bsh_in, bsh_out = _gather_weights(w_in, w_out, cw)
    cw_full = jnp.transpose(cw_all, (1, 2, 0, 3)).reshape(DEPTH, CW_ROWS, GW)
    row = lambda a, l: a[l].reshape(1, -1)
    cnt = _count_table()
    prm = [(cw_full[l, 0:KA], row(conv_a_b, l), cw_full[l, 8:8 + KB], row(conv_b_b, l), row(ln_b_g, l), row(ln_b_b, l),
            pool_w[l].astype(BF16), row(pool_b, l), row(pool_scale, l), cnt) for l in range(DEPTH)]

    h0, xb0, cb0, pool0, ca0, z0, x1, win1_b, wout1_b = _fwd_layer(x2, win0_b, wout0_b, prm[0], row(ln_g, 0), row(ln_b, 0), None, tt=TT, last=False,
                                                  comm=_gather_comm(bsh_in, bsh_out, 1))
    h1, xb1, cb1, pool1, ca1, dz1, dln1, loss8 = _fwd_layer(x1, win1_b, wout1_b, prm[1], row(ln_g, 1), row(ln_b, 1), tgt, tt=TT, last=True)

    dz0, dh1, dwout1, small1, dpw1, dln0 = _bwd_layer(dz1, h1, cb1, pool1, ca1, win1_b, wout1_b, prm[1], z0, row(ln_g, 0), tt=TT)
    dwin1 = _wgrad_in(xb1, dh1, tk=TK)
    loss_row = jnp.pad(loss8, ((0, 0), (0, GW - loss8.shape[1])))
    pack1 = _pack([small1, dpw1.reshape(PGD, GW), dln1.reshape(4, GW), dln0.reshape(4, GW), loss_row])
    sums1 = _reduce_to_owner_halves([dwin1, dwout1.reshape(N_CHIP, SHARD_OUT, D_MODEL), pack1], core1, "1")
    gx, dh0, dwout0, small0, dpw0 = _bwd_layer(dz0, h0, cb0, pool0, ca0, win0_b, wout0_b, prm[0], None, None, tt=TT)
    pack0 = _pack([small0, dpw0.reshape(PGD, GW)])
    sums0 = _reduce_to_owner_halves([dwout0.reshape(N_CHIP, SHARD_OUT, D_MODEL), pack0], core1, "0")
    dwin0, *landed = _wgrad_in(xb0, dh0, tk=TK, comm=_owner_comm(sums1 + sums0))
    landed1, landed0 = landed[:3], landed[3:]
    landed0 = list(_send_to_owners(_reduce_to_owner_halves([dwin0], core1, "in0"))) + list(landed0)

    g_in = _sum_chips_into(landed0[0], _sum_chips_into(landed1[0], None, 1, core1, rb=256), 0, core1, rb=256)
    g_out = _sum_chips_into(landed0[1], _sum_chips_into(landed1[1], None, 1, core1, rb=SHARD_OUT // 2), 0, core1, rb=SHARD_OUT // 2)
    red_small = jnp.concatenate([_sum_chips(a, rb=PACK_ROWS // N_CHIP // 2) for a in (landed0[2], landed1[2])], axis=0)
    g_in, g_out, g_small = _spread_reduced(g_in, g_out, red_small)

    flat = lambda a: a.reshape(-1, a.shape[-1])
    unflat = lambda a, like: a.reshape(like.shape)
    d_in, nm_in, nv_in, g_in = [unflat(a, w_in) for a in _adamw_big(flat(w_in), flat(g_in), flat(m_w_in), flat(v_w_in), rb=256)]
    d_out, nm_out, nv_out, g_out = [unflat(a, w_out) for a in _adamw_big(flat(w_out), flat(g_out), flat(m_w_out), flat(v_w_out), rb=SHARD_OUT)]

    hp = PACK_ROWS // N_CHIP // 2
    unpack = lambda o: jnp.concatenate([g_small[:, o:o + hp], g_small[:, 2 * hp + o:3 * hp + o]], axis=1).reshape(PACK_ROWS, GW)
    p0, p1 = unpack(0), unpack(hp)
    small = [p0[0:N_RACC], p1[0:N_RACC]]
    dpw = [p[N_RACC:N_RACC + PGD].reshape(len(POOL_WINDOWS), PGD, PGD) for p in (p0, p1)]
    o = N_RACC + PGD
    g_lng = jnp.stack([p1[o + 4:o + 8].reshape(2, D_MODEL)[0], p1[o:o + 4].reshape(2, D_MODEL)[0]])
    g_lnb = jnp.stack([p1[o + 4:o + 8].reshape(2, D_MODEL)[1], p1[o:o + 4].reshape(2, D_MODEL)[1]])
    mine = lambda a: lax.dynamic_slice_in_dim(a, chip * PGD, PGD, axis=-1)
    stack = lambda f: jnp.stack([f(0), f(1)])
    g_caw = stack(lambda l: mine(small[l][R_DWA:R_DWA + KA]))
    g_cab = stack(lambda l: small[l][R_DCAB])
    g_cbw = stack(lambda l: mine(small[l][R_DWB:R_DWB + KB]))
    g_cbb = stack(lambda l: small[l][R_DCBB])
    g_lbg = stack(lambda l: small[l][R_DLBG])
    g_lbb = stack(lambda l: small[l][R_DLBB])
    g_pw = stack(lambda l: dpw[l])
    g_pb = stack(lambda l: small[l][R_DPB].reshape(len(POOL_WINDOWS), PGD))
    g_ps = stack(lambda l: small[l][R_DPS])
    ws = [conv_a_w, conv_a_b, conv_b_w, conv_b_b, ln_b_g, ln_b_b, pool_w, pool_b, pool_scale, ln_g, ln_b]
    gs = [g_caw, g_cab, g_cbw, g_cbb, g_lbg, g_lbb, g_pw, g_pb, g_ps, g_lng, g_lnb]
    ms = [m_conv_a_w, m_conv_a_b, m_conv_b_w, m_conv_b_b, m_ln_b_g, m_ln_b_b, m_pool_w, m_pool_b, m_pool_scale, m_ln_g, m_ln_b]
    vs = [v_conv_a_w, v_conv_a_b, v_conv_b_w, v_conv_b_b, v_ln_b_g, v_ln_b_b, v_pool_w, v_pool_b, v_pool_scale, v_ln_g, v_ln_b]
    ds, nms, nvs = _adamw_small(ws, gs, ms, vs)

    loss = p1[o + 8, 0]

    def order(in_, small_, out_):
        return [in_, *small_[:9], out_, *small_[9:]]
    return (loss, gx[None], *order(g_in, gs, g_out), *order(d_in, ds, d_out), *order(nm_in, nms, nm_out), *order(nv_in, nvs, nv_out))
```
